```python
import math
import jax, jax.numpy as jnp
from jax import lax
import numpy as np

D_MODEL = 1024
BATCH = 8
SEQ = 8192
DEPTH = 2

D_SSM = 384
SSM_GROUP = 16
N_SSM_GROUPS = D_SSM // SSM_GROUP
SSM_STATE = 64
POOL_WINDOWS = (2, 4, 8, 16)
N_POOL_GROUPS = len(POOL_WINDOWS)
POOL_GROUP = 64
D_POOL = N_POOL_GROUPS * POOL_GROUP
MAX_WINDOW = max(POOL_WINDOWS)
SGU_HEADS = 6
SGU_HEAD_DIM = 64
D_SGU = SGU_HEADS * SGU_HEAD_DIM
CHUNK = 128
D_MIX = D_SSM + D_POOL + D_SGU
D_IN = D_SSM + D_POOL + 2 * D_SGU
D_FF = ((8 * D_MODEL // 3 + 255) // 256) * 256
EPS = 1e-6

kernel_name = "hybrid_s5_pool_sgu_trunk"


def rms_norm(x, g):
    xf = x.astype(jnp.float32)
    y = xf * lax.rsqrt(jnp.mean(xf * xf, axis=-1, keepdims=True) + EPS)
    return (y * g.astype(jnp.float32)).astype(x.dtype)


def s5_mixer(u, A_re, A_im, log_dt, B_re, B_im, C_re, C_im, D_skip, w_glu, b_glu):
    f32 = jnp.float32
    bsz, seq, _ = u.shape
    uf = u.astype(f32).reshape(bsz, seq, N_SSM_GROUPS, SSM_GROUP)
    A_re = A_re.astype(f32); A_im = A_im.astype(f32)
    dt = jnp.exp(log_dt.astype(f32))[:, None]
    mag = jnp.exp(A_re * dt)
    ar = mag * jnp.cos(A_im * dt)
    ai = mag * jnp.sin(A_im * dt)
    den = A_re * A_re + A_im * A_im
    f_re = ((ar - 1.0) * A_re + ai * A_im) / den
    f_im = (ai * A_re - (ar - 1.0) * A_im) / den
    B_re = B_re.astype(f32); B_im = B_im.astype(f32)
    Bb_re = f_re[..., None] * B_re - f_im[..., None] * B_im
    Bb_im = f_re[..., None] * B_im + f_im[..., None] * B_re
    bu_re = jnp.einsum('bsgc,gnc->bsgn', uf, Bb_re)
    bu_im = jnp.einsum('bsgc,gnc->bsgn', uf, Bb_im)
    a_re = jnp.broadcast_to(ar, bu_re.shape)
    a_im = jnp.broadcast_to(ai, bu_re.shape)

    def combine(left, right):
        a1r, a1i, b1r, b1i = left
        a2r, a2i, b2r, b2i = right
        return (a1r * a2r - a1i * a2i,
                a1r * a2i + a1i * a2r,
                a2r * b1r - a2i * b1i + b2r,
                a2r * b1i + a2i * b1r + b2i)

    _, _, h_re, h_im = lax.associative_scan(combine, (a_re, a_im, bu_re, bu_im), axis=1)
    y = (jnp.einsum('bsgn,gcn->bsgc', h_re, C_re.astype(f32))
         - jnp.einsum('bsgn,gcn->bsgc', h_im, C_im.astype(f32)))
    y = y.reshape(bsz, seq, D_SSM) + D_skip.astype(f32) * uf.reshape(bsz, seq, D_SSM)
    g = jax.nn.gelu(y)
    out = g * jax.nn.sigmoid(g @ w_glu.astype(f32) + b_glu.astype(f32))
    return out.astype(u.dtype)


def pool_mixer(u, w_pool, pool_scale):
    f32 = jnp.float32
    bsz, seq, _ = u.shape
    uf = u.astype(f32).reshape(bsz, seq, N_POOL_GROUPS, POOL_GROUP)
    csum = jnp.cumsum(uf, axis=1)
    cpad = jnp.pad(csum, ((0, 0), (MAX_WINDOW, 0), (0, 0), (0, 0)))
    pos = jnp.arange(1, seq + 1)
    means = []
    for g, w in enumerate(POOL_WINDOWS):
        lagged = cpad[:, MAX_WINDOW - w:MAX_WINDOW - w + seq, g]
        count = jnp.minimum(pos, w).astype(f32)[None, :, None]
        means.append((csum[:, :, g] - lagged) / count)
    pooled = jnp.stack(means, axis=2) - uf
    mixed = jnp.einsum('bsgc,gcd->bsgd', pooled, w_pool.astype(f32))
    out = mixed.reshape(bsz, seq, D_POOL) * pool_scale.astype(f32)
    return out.astype(u.dtype)


def sgu_mixer(zu, zv, ln_g, ln_b, w_spatial, b_spatial):
    f32 = jnp.float32
    bsz, seq, _ = zu.shape
    n_chunks = seq // CHUNK
    u = jax.nn.gelu(zu.astype(f32))
    v = jax.nn.gelu(zv.astype(f32))
    mu = jnp.mean(v, axis=-1, keepdims=True)
    var = jnp.mean(jnp.square(v - mu), axis=-1, keepdims=True)
    v = (v - mu) * lax.rsqrt(var + EPS) * ln_g.astype(f32) + ln_b.astype(f32)
    vh = v.reshape(bsz, n_chunks, CHUNK, SGU_HEADS, SGU_HEAD_DIM)
    mask = jnp.tril(jnp.ones((CHUNK, CHUNK), dtype=bool))
    ws = jnp.where(mask[None], w_spatial.astype(f32), 0.0)
    mixed = jnp.einsum('hts,bnshd->bnthd', ws, vh)
    mixed = mixed + jnp.transpose(b_spatial.astype(f32))[None, None, :, :, None]
    out = u * mixed.reshape(bsz, seq, D_SGU)
    return out.astype(zu.dtype)


def _fwd_setup_inputs(seed: int = 0) -> dict:
    key = jax.random.key(seed)
    ks = jax.random.split(key, 24)
    f32 = jnp.float32
    nrm = lambda k, shape, s: (jax.random.normal(k, shape, f32) * s)
    x = jax.random.normal(ks[0], (BATCH, SEQ, D_MODEL), f32)
    g_mix = 1.0 + nrm(ks[1], (DEPTH, D_MODEL), 0.02)
    w_in = nrm(ks[2], (DEPTH, D_MODEL, D_IN), D_MODEL ** -0.5)
    A_re = -0.5 + nrm(ks[3], (DEPTH, N_SSM_GROUPS, SSM_STATE), 0.01)
    A_im = (jnp.pi * jnp.arange(SSM_STATE, dtype=f32))[None, None, :] + nrm(ks[4], (DEPTH, N_SSM_GROUPS, SSM_STATE), 0.01)
    log_dt = jax.random.uniform(ks[5], (DEPTH, N_SSM_GROUPS), f32, math.log(1e-3), math.log(1e-1))
    B_re = nrm(ks[6], (DEPTH, N_SSM_GROUPS, SSM_STATE, SSM_GROUP), (2 * SSM_GROUP) ** -0.5)
    B_im = nrm(ks[7], (DEPTH, N_SSM_GROUPS, SSM_STATE, SSM_GROUP), (2 * SSM_GROUP) ** -0.5)
    C_re = nrm(ks[8], (DEPTH, N_SSM_GROUPS, SSM_GROUP, SSM_STATE), (2 * SSM_STATE) ** -0.5)
    C_im = nrm(ks[9], (DEPTH, N_SSM_GROUPS, SSM_GROUP, SSM_STATE), (2 * SSM_STATE) ** -0.5)
    D_skip = nrm(ks[10], (DEPTH, D_SSM), 1.0)
    w_glu = nrm(ks[11], (DEPTH, D_SSM, D_SSM), D_SSM ** -0.5)
    b_glu = nrm(ks[12], (DEPTH, D_SSM), 0.01)
    w_pool = nrm(ks[13], (DEPTH, N_POOL_GROUPS, POOL_GROUP, POOL_GROUP), POOL_GROUP ** -0.5)
    pool_scale = 1.0 + nrm(ks[14], (DEPTH, D_POOL), 0.02)
    sgu_ln_g = 1.0 + nrm(ks[15], (DEPTH, D_SGU), 0.02)
    sgu_ln_b = nrm(ks[16], (DEPTH, D_SGU), 0.01)
    w_spatial = nrm(ks[17], (DEPTH, SGU_HEADS, CHUNK, CHUNK), CHUNK ** -0.5)
    b_spatial = 1.0 + nrm(ks[18], (DEPTH, SGU_HEADS, CHUNK), 0.02)
    w_out = nrm(ks[19], (DEPTH, D_MIX, D_MODEL), D_MIX ** -0.5)
    g_ffn = 1.0 + nrm(ks[20], (DEPTH, D_MODEL), 0.02)
    kf = jax.random.split(ks[21], 3)
    w_gate = nrm(kf[0], (DEPTH, D_MODEL, D_FF), D_MODEL ** -0.5)
    w_up = nrm(kf[1], (DEPTH, D_MODEL, D_FF), D_MODEL ** -0.5)
    w_down = nrm(kf[2], (DEPTH, D_FF, D_MODEL), D_FF ** -0.5)
    g_final = 1.0 + nrm(ks[22], (D_MODEL,), 0.02)
    return {"x": x, "g_mix": g_mix, "w_in": w_in, "A_re": A_re, "A_im": A_im,
            "log_dt": log_dt, "B_re": B_re, "B_im": B_im, "C_re": C_re, "C_im": C_im,
            "D_skip": D_skip, "w_glu": w_glu, "b_glu": b_glu, "w_pool": w_pool,
            "pool_scale": pool_scale, "sgu_ln_g": sgu_ln_g, "sgu_ln_b": sgu_ln_b,
            "w_spatial": w_spatial, "b_spatial": b_spatial, "w_out": w_out,
            "g_ffn": g_ffn, "w_gate": w_gate, "w_up": w_up, "w_down": w_down,
            "g_final": g_final}


def _fwd_reference(x, g_mix, w_in, A_re, A_im, log_dt, B_re, B_im, C_re, C_im, D_skip,
              w_glu, b_glu, w_pool, pool_scale, sgu_ln_g, sgu_ln_b, w_spatial, b_spatial,
              w_out, g_ffn, w_gate, w_up, w_down, g_final):
    split_points = (D_SSM, D_SSM + D_POOL, D_SSM + D_POOL + D_SGU)
    for l in range(DEPTH):
        h = rms_norm(x, g_mix[l])
        z = h @ w_in[l]
        z_a, z_b, z_u, z_v = jnp.split(z, split_points, axis=-1)
        y_a = s5_mixer(z_a, A_re[l], A_im[l], log_dt[l], B_re[l], B_im[l], C_re[l], C_im[l],
                       D_skip[l], w_glu[l], b_glu[l])
        y_b = pool_mixer(z_b, w_pool[l], pool_scale[l])
        y_c = sgu_mixer(z_u, z_v, sgu_ln_g[l], sgu_ln_b[l], w_spatial[l], b_spatial[l])
        y = jnp.concatenate([y_a, y_b, y_c], axis=-1) @ w_out[l]
        x = x + y
        h = rms_norm(x, g_ffn[l])
        x = x + (jax.nn.silu(h @ w_gate[l]) * (h @ w_up[l])) @ w_down[l]
    return rms_norm(x, g_final)


import jax as _jax
import jax.numpy as _jnp

TWIN_FORMAT = 'train_step'
FWD_PARAMS = ['x', 'g_mix', 'w_in', 'A_re', 'A_im', 'log_dt', 'B_re', 'B_im', 'C_re', 'C_im', 'D_skip', 'w_glu', 'b_glu', 'w_pool', 'pool_scale', 'sgu_ln_g', 'sgu_ln_b', 'w_spatial', 'b_spatial', 'w_out', 'g_ffn', 'w_gate', 'w_up', 'w_down', 'g_final']
TWIN_WEIGHTS = ['g_mix', 'w_in', 'A_re', 'A_im', 'log_dt', 'B_re', 'B_im', 'C_re', 'C_im', 'D_skip', 'w_glu', 'b_glu', 'w_pool', 'pool_scale', 'sgu_ln_g', 'sgu_ln_b', 'w_spatial', 'b_spatial', 'w_out', 'g_ffn', 'w_gate', 'w_up', 'w_down', 'g_final']
TWIN_DIFF_INPUT = 'x'
TWIN_INPUTS = ['x', 'g_mix', 'w_in', 'A_re', 'A_im', 'log_dt', 'B_re', 'B_im', 'C_re', 'C_im', 'D_skip', 'w_glu', 'b_glu', 'w_pool', 'pool_scale', 'sgu_ln_g', 'sgu_ln_b', 'w_spatial', 'b_spatial', 'w_out', 'g_ffn', 'w_gate', 'w_up', 'w_down', 'g_final', 'loss_target', 'm_g_mix', 'm_w_in', 'm_A_re', 'm_A_im', 'm_log_dt', 'm_B_re', 'm_B_im', 'm_C_re', 'm_C_im', 'm_D_skip', 'm_w_glu', 'm_b_glu', 'm_w_pool', 'm_pool_scale', 'm_sgu_ln_g', 'm_sgu_ln_b', 'm_w_spatial', 'm_b_spatial', 'm_w_out', 'm_g_ffn', 'm_w_gate', 'm_w_up', 'm_w_down', 'm_g_final', 'v_g_mix', 'v_w_in', 'v_A_re', 'v_A_im', 'v_log_dt', 'v_B_re', 'v_B_im', 'v_C_re', 'v_C_im', 'v_D_skip', 'v_w_glu', 'v_b_glu', 'v_w_pool', 'v_pool_scale', 'v_sgu_ln_g', 'v_sgu_ln_b', 'v_w_spatial', 'v_b_spatial', 'v_w_out', 'v_g_ffn', 'v_w_gate', 'v_w_up', 'v_w_down', 'v_g_final']
TWIN_OUTPUTS = ['loss', 'grad_x', 'grad_g_mix', 'grad_w_in', 'grad_A_re', 'grad_A_im', 'grad_log_dt', 'grad_B_re', 'grad_B_im', 'grad_C_re', 'grad_C_im', 'grad_D_skip', 'grad_w_glu', 'grad_b_glu', 'grad_w_pool', 'grad_pool_scale', 'grad_sgu_ln_g', 'grad_sgu_ln_b', 'grad_w_spatial', 'grad_b_spatial', 'grad_w_out', 'grad_g_ffn', 'grad_w_gate', 'grad_w_up', 'grad_w_down', 'grad_g_final', 'delta_g_mix', 'delta_w_in', 'delta_A_re', 'delta_A_im', 'delta_log_dt', 'delta_B_re', 'delta_B_im', 'delta_C_re', 'delta_C_im', 'delta_D_skip', 'delta_w_glu', 'delta_b_glu', 'delta_w_pool', 'delta_pool_scale', 'delta_sgu_ln_g', 'delta_sgu_ln_b', 'delta_w_spatial', 'delta_b_spatial', 'delta_w_out', 'delta_g_ffn', 'delta_w_gate', 'delta_w_up', 'delta_w_down', 'delta_g_final', 'new_m_g_mix', 'new_m_w_in', 'new_m_A_re', 'new_m_A_im', 'new_m_log_dt', 'new_m_B_re', 'new_m_B_im', 'new_m_C_re', 'new_m_C_im', 'new_m_D_skip', 'new_m_w_glu', 'new_m_b_glu', 'new_m_w_pool', 'new_m_pool_scale', 'new_m_sgu_ln_g', 'new_m_sgu_ln_b', 'new_m_w_spatial', 'new_m_b_spatial', 'new_m_w_out', 'new_m_g_ffn', 'new_m_w_gate', 'new_m_w_up', 'new_m_w_down', 'new_m_g_final', 'new_v_g_mix', 'new_v_w_in', 'new_v_A_re', 'new_v_A_im', 'new_v_log_dt', 'new_v_B_re', 'new_v_B_im', 'new_v_C_re', 'new_v_C_im', 'new_v_D_skip', 'new_v_w_glu', 'new_v_b_glu', 'new_v_w_pool', 'new_v_pool_scale', 'new_v_sgu_ln_g', 'new_v_sgu_ln_b', 'new_v_w_spatial', 'new_v_b_spatial', 'new_v_w_out', 'new_v_g_ffn', 'new_v_w_gate', 'new_v_w_up', 'new_v_w_down', 'new_v_g_final']
TWIN_LEAF_KINDS = {'loss': 'loss', 'grad_x': 'grad_x', 'grad_g_mix': 'grad_w', 'grad_w_in': 'grad_w', 'grad_A_re': 'grad_w', 'grad_A_im': 'grad_w', 'grad_log_dt': 'grad_w', 'grad_B_re': 'grad_w', 'grad_B_im': 'grad_w', 'grad_C_re': 'grad_w', 'grad_C_im': 'grad_w', 'grad_D_skip': 'grad_w', 'grad_w_glu': 'grad_w', 'grad_b_glu': 'grad_w', 'grad_w_pool': 'grad_w', 'grad_pool_scale': 'grad_w', 'grad_sgu_ln_g': 'grad_w', 'grad_sgu_ln_b': 'grad_w', 'grad_w_spatial': 'grad_w', 'grad_b_spatial': 'grad_w', 'grad_w_out': 'grad_w', 'grad_g_ffn': 'grad_w', 'grad_w_gate': 'grad_w', 'grad_w_up': 'grad_w', 'grad_w_down': 'grad_w', 'grad_g_final': 'grad_w', 'delta_g_mix': 'delta_w', 'delta_w_in': 'delta_w', 'delta_A_re': 'delta_w', 'delta_A_im': 'delta_w', 'delta_log_dt': 'delta_w', 'delta_B_re': 'delta_w', 'delta_B_im': 'delta_w', 'delta_C_re': 'delta_w', 'delta_C_im': 'delta_w', 'delta_D_skip': 'delta_w', 'delta_w_glu': 'delta_w', 'delta_b_glu': 'delta_w', 'delta_w_pool': 'delta_w', 'delta_pool_scale': 'delta_w', 'delta_sgu_ln_g': 'delta_w', 'delta_sgu_ln_b': 'delta_w', 'delta_w_spatial': 'delta_w', 'delta_b_spatial': 'delta_w', 'delta_w_out': 'delta_w', 'delta_g_ffn': 'delta_w', 'delta_w_gate': 'delta_w', 'delta_w_up': 'delta_w', 'delta_w_down': 'delta_w', 'delta_g_final': 'delta_w', 'new_m_g_mix': 'new_m', 'new_m_w_in': 'new_m', 'new_m_A_re': 'new_m', 'new_m_A_im': 'new_m', 'new_m_log_dt': 'new_m', 'new_m_B_re': 'new_m', 'new_m_B_im': 'new_m', 'new_m_C_re': 'new_m', 'new_m_C_im': 'new_m', 'new_m_D_skip': 'new_m', 'new_m_w_glu': 'new_m', 'new_m_b_glu': 'new_m', 'new_m_w_pool': 'new_m', 'new_m_pool_scale': 'new_m', 'new_m_sgu_ln_g': 'new_m', 'new_m_sgu_ln_b': 'new_m', 'new_m_w_spatial': 'new_m', 'new_m_b_spatial': 'new_m', 'new_m_w_out': 'new_m', 'new_m_g_ffn': 'new_m', 'new_m_w_gate': 'new_m', 'new_m_w_up': 'new_m', 'new_m_w_down': 'new_m', 'new_m_g_final': 'new_m', 'new_v_g_mix': 'new_v', 'new_v_w_in': 'new_v', 'new_v_A_re': 'new_v', 'new_v_A_im': 'new_v', 'new_v_log_dt': 'new_v', 'new_v_B_re': 'new_v', 'new_v_B_im': 'new_v', 'new_v_C_re': 'new_v', 'new_v_C_im': 'new_v', 'new_v_D_skip': 'new_v', 'new_v_w_glu': 'new_v', 'new_v_b_glu': 'new_v', 'new_v_w_pool': 'new_v', 'new_v_pool_scale': 'new_v', 'new_v_sgu_ln_g': 'new_v', 'new_v_sgu_ln_b': 'new_v', 'new_v_w_spatial': 'new_v', 'new_v_b_spatial': 'new_v', 'new_v_w_out': 'new_v', 'new_v_g_ffn': 'new_v', 'new_v_w_gate': 'new_v', 'new_v_w_up': 'new_v', 'new_v_w_down': 'new_v', 'new_v_g_final': 'new_v'}


def _forward(args):
    return _fwd_reference(*[args[k] for k in FWD_PARAMS])


def _output_shape():
    def fwd():
        inp = _fwd_setup_inputs(0)
        return _fwd_reference(*[inp[k] for k in FWD_PARAMS])
    out = _jax.eval_shape(fwd)
    return out.shape, out.dtype

N_MICROBATCH = 1
ADAM_LR = 0.001
ADAM_B1 = 0.9
ADAM_B2 = 0.999
ADAM_EPS = 1e-08
ADAM_WD = 0.01
ADAM_STEP = 10
PER_EXAMPLE_BATCH_AXIS = {'x': 0, 'loss_target': 0}
SHARED_INPUTS = []
_WEIGHT_DTYPES = {'g_mix': _jnp.float32, 'w_in': _jnp.float32, 'A_re': _jnp.float32, 'A_im': _jnp.float32, 'log_dt': _jnp.float32, 'B_re': _jnp.float32, 'B_im': _jnp.float32, 'C_re': _jnp.float32, 'C_im': _jnp.float32, 'D_skip': _jnp.float32, 'w_glu': _jnp.float32, 'b_glu': _jnp.float32, 'w_pool': _jnp.float32, 'pool_scale': _jnp.float32, 'sgu_ln_g': _jnp.float32, 'sgu_ln_b': _jnp.float32, 'w_spatial': _jnp.float32, 'b_spatial': _jnp.float32, 'w_out': _jnp.float32, 'g_ffn': _jnp.float32, 'w_gate': _jnp.float32, 'w_up': _jnp.float32, 'w_down': _jnp.float32, 'g_final': _jnp.float32}
MOMENT_SCALE = {'g_mix': 1.731235e-01, 'w_in': 1.440308e-01, 'A_re': 4.185746e-03, 'A_im': 4.850988e-03, 'log_dt': 1.826181e+00, 'B_re': 2.812419e-03, 'B_im': 2.833056e-03, 'C_re': 5.760842e-03, 'C_im': 5.939634e-03, 'D_skip': 8.212315e-02, 'w_glu': 2.250905e-02, 'b_glu': 3.274005e-02, 'w_pool': 1.899361e-01, 'pool_scale': 1.862077e-01, 'sgu_ln_g': 9.977175e-02, 'sgu_ln_b': 1.081905e-01, 'w_spatial': 6.930831e-02, 'b_spatial': 1.022956e-01, 'w_out': 1.542413e-01, 'g_ffn': 1.544680e-01, 'w_gate': 6.721072e-02, 'w_up': 6.511878e-02, 'w_down': 1.080370e-01, 'g_final': 6.406353e+01}


def _to_microbatches(a, axis):
    t = _jnp.moveaxis(a, axis, 0)
    t = t.reshape((N_MICROBATCH, t.shape[0] // N_MICROBATCH) + t.shape[1:])
    return _jnp.moveaxis(t, 1, axis + 1)


def setup_inputs(seed: int = 0) -> dict:
    inp = _fwd_setup_inputs(seed)
    key = _jax.random.fold_in(_jax.random.key(seed), 7919)
    shape, _ = _output_shape()
    out = dict(inp)
    out["loss_target"] = _jax.random.normal(_jax.random.fold_in(key, 0), shape, _jnp.float32)
    for i, name in enumerate(TWIN_WEIGHTS):
        w = inp[name].astype(_jnp.float32)
        if MOMENT_SCALE is None:
            s = _jnp.sqrt(_jnp.mean(_jnp.square(w)) + 1e-30)
        else:
            s = MOMENT_SCALE[name]
        km, kv = _jax.random.split(_jax.random.fold_in(key, i + 1))
        out[name] = w
        out["m_" + name] = s * _jax.random.normal(km, w.shape, _jnp.float32)
        out["v_" + name] = (s * s) * _jax.random.uniform(kv, w.shape, _jnp.float32, 0.5, 1.5)
    if N_MICROBATCH > 1:
        for name, axis in PER_EXAMPLE_BATCH_AXIS.items():
            out[name] = _to_microbatches(out[name], axis)
    return {'x': out['x'], 'g_mix': out['g_mix'], 'w_in': out['w_in'], 'A_re': out['A_re'], 'A_im': out['A_im'], 'log_dt': out['log_dt'], 'B_re': out['B_re'], 'B_im': out['B_im'], 'C_re': out['C_re'], 'C_im': out['C_im'], 'D_skip': out['D_skip'], 'w_glu': out['w_glu'], 'b_glu': out['b_glu'], 'w_pool': out['w_pool'], 'pool_scale': out['pool_scale'], 'sgu_ln_g': out['sgu_ln_g'], 'sgu_ln_b': out['sgu_ln_b'], 'w_spatial': out['w_spatial'], 'b_spatial': out['b_spatial'], 'w_out': out['w_out'], 'g_ffn': out['g_ffn'], 'w_gate': out['w_gate'], 'w_up': out['w_up'], 'w_down': out['w_down'], 'g_final': out['g_final'], 'loss_target': out['loss_target'], 'm_g_mix': out['m_g_mix'], 'm_w_in': out['m_w_in'], 'm_A_re': out['m_A_re'], 'm_A_im': out['m_A_im'], 'm_log_dt': out['m_log_dt'], 'm_B_re': out['m_B_re'], 'm_B_im': out['m_B_im'], 'm_C_re': out['m_C_re'], 'm_C_im': out['m_C_im'], 'm_D_skip': out['m_D_skip'], 'm_w_glu': out['m_w_glu'], 'm_b_glu': out['m_b_glu'], 'm_w_pool': out['m_w_pool'], 'm_pool_scale': out['m_pool_scale'], 'm_sgu_ln_g': out['m_sgu_ln_g'], 'm_sgu_ln_b': out['m_sgu_ln_b'], 'm_w_spatial': out['m_w_spatial'], 'm_b_spatial': out['m_b_spatial'], 'm_w_out': out['m_w_out'], 'm_g_ffn': out['m_g_ffn'], 'm_w_gate': out['m_w_gate'], 'm_w_up': out['m_w_up'], 'm_w_down': out['m_w_down'], 'm_g_final': out['m_g_final'], 'v_g_mix': out['v_g_mix'], 'v_w_in': out['v_w_in'], 'v_A_re': out['v_A_re'], 'v_A_im': out['v_A_im'], 'v_log_dt': out['v_log_dt'], 'v_B_re': out['v_B_re'], 'v_B_im': out['v_B_im'], 'v_C_re': out['v_C_re'], 'v_C_im': out['v_C_im'], 'v_D_skip': out['v_D_skip'], 'v_w_glu': out['v_w_glu'], 'v_b_glu': out['v_b_glu'], 'v_w_pool': out['v_w_pool'], 'v_pool_scale': out['v_pool_scale'], 'v_sgu_ln_g': out['v_sgu_ln_g'], 'v_sgu_ln_b': out['v_sgu_ln_b'], 'v_w_spatial': out['v_w_spatial'], 'v_b_spatial': out['v_b_spatial'], 'v_w_out': out['v_w_out'], 'v_g_ffn': out['v_g_ffn'], 'v_w_gate': out['v_w_gate'], 'v_w_up': out['v_w_up'], 'v_w_down': out['v_w_down'], 'v_g_final': out['v_g_final']}


def _loss(weights, diff, rest, loss_target):
    with _jax.named_scope("forward"):
        args = {**rest, TWIN_DIFF_INPUT: diff, **{k: w.astype(_WEIGHT_DTYPES[k]) for k, w in weights.items()}}
        y = _forward(args)
    with _jax.named_scope("loss_head"):
        err = _jnp.square(y.astype(_jnp.float32) - loss_target)
        return 0.5 * _jnp.sum(_jnp.mean(err, axis=-1)) if err.ndim else 0.5 * err


def _adamw(w, g, m, v):
    m = ADAM_B1 * m + (1.0 - ADAM_B1) * g
    v = ADAM_B2 * v + (1.0 - ADAM_B2) * _jnp.square(g)
    m_hat = m / (1.0 - ADAM_B1 ** ADAM_STEP)
    v_hat = v / (1.0 - ADAM_B2 ** ADAM_STEP)
    delta = -ADAM_LR * (m_hat / (_jnp.sqrt(v_hat) + ADAM_EPS) + ADAM_WD * w)
    return delta, m, v


def reference(x, g_mix, w_in, A_re, A_im, log_dt, B_re, B_im, C_re, C_im, D_skip, w_glu, b_glu, w_pool, pool_scale, sgu_ln_g, sgu_ln_b, w_spatial, b_spatial, w_out, g_ffn, w_gate, w_up, w_down, g_final, loss_target, m_g_mix, m_w_in, m_A_re, m_A_im, m_log_dt, m_B_re, m_B_im, m_C_re, m_C_im, m_D_skip, m_w_glu, m_b_glu, m_w_pool, m_pool_scale, m_sgu_ln_g, m_sgu_ln_b, m_w_spatial, m_b_spatial, m_w_out, m_g_ffn, m_w_gate, m_w_up, m_w_down, m_g_final, v_g_mix, v_w_in, v_A_re, v_A_im, v_log_dt, v_B_re, v_B_im, v_C_re, v_C_im, v_D_skip, v_w_glu, v_b_glu, v_w_pool, v_pool_scale, v_sgu_ln_g, v_sgu_ln_b, v_w_spatial, v_b_spatial, v_w_out, v_g_ffn, v_w_gate, v_w_up, v_w_down, v_g_final):
    given = dict(x=x, g_mix=g_mix, w_in=w_in, A_re=A_re, A_im=A_im, log_dt=log_dt, B_re=B_re, B_im=B_im, C_re=C_re, C_im=C_im, D_skip=D_skip, w_glu=w_glu, b_glu=b_glu, w_pool=w_pool, pool_scale=pool_scale, sgu_ln_g=sgu_ln_g, sgu_ln_b=sgu_ln_b, w_spatial=w_spatial, b_spatial=b_spatial, w_out=w_out, g_ffn=g_ffn, w_gate=w_gate, w_up=w_up, w_down=w_down, g_final=g_final, loss_target=loss_target, m_g_mix=m_g_mix, m_w_in=m_w_in, m_A_re=m_A_re, m_A_im=m_A_im, m_log_dt=m_log_dt, m_B_re=m_B_re, m_B_im=m_B_im, m_C_re=m_C_re, m_C_im=m_C_im, m_D_skip=m_D_skip, m_w_glu=m_w_glu, m_b_glu=m_b_glu, m_w_pool=m_w_pool, m_pool_scale=m_pool_scale, m_sgu_ln_g=m_sgu_ln_g, m_sgu_ln_b=m_sgu_ln_b, m_w_spatial=m_w_spatial, m_b_spatial=m_b_spatial, m_w_out=m_w_out, m_g_ffn=m_g_ffn, m_w_gate=m_w_gate, m_w_up=m_w_up, m_w_down=m_w_down, m_g_final=m_g_final, v_g_mix=v_g_mix, v_w_in=v_w_in, v_A_re=v_A_re, v_A_im=v_A_im, v_log_dt=v_log_dt, v_B_re=v_B_re, v_B_im=v_B_im, v_C_re=v_C_re, v_C_im=v_C_im, v_D_skip=v_D_skip, v_w_glu=v_w_glu, v_b_glu=v_b_glu, v_w_pool=v_w_pool, v_pool_scale=v_pool_scale, v_sgu_ln_g=v_sgu_ln_g, v_sgu_ln_b=v_sgu_ln_b, v_w_spatial=v_w_spatial, v_b_spatial=v_b_spatial, v_w_out=v_w_out, v_g_ffn=v_g_ffn, v_w_gate=v_w_gate, v_w_up=v_w_up, v_w_down=v_w_down, v_g_final=v_g_final)
    weights = {n: given[n] for n in TWIN_WEIGHTS}
    shared = {n: given[n] for n in SHARED_INPUTS}
    per_example = {n: given[n] for n in ['x']}
    grad_fn = _jax.value_and_grad(_loss, argnums=(0, 1))

    def one_microbatch(ex, loss_target):
        ex = dict(ex)
        diff = ex.pop(TWIN_DIFF_INPUT)
        return grad_fn(weights, diff, {**shared, **ex}, loss_target)

    if N_MICROBATCH == 1:
        loss, (grad_w, grad_x) = one_microbatch(per_example, given["loss_target"])
    else:
        def body(carry, xs):
            loss_sum, grad_sum = carry
            l_k, (gw_k, gx_k) = one_microbatch(xs[0], xs[1])
            with _jax.named_scope("update"):
                return (loss_sum + l_k, _jax.tree.map(_jnp.add, grad_sum, gw_k)), gx_k

        init = (_jnp.zeros((), _jnp.float32), _jax.tree.map(_jnp.zeros_like, weights))
        (loss, grad_w), grad_x = _jax.lax.scan(body, init, (per_example, given["loss_target"]))
    with _jax.named_scope("update"):
        delta_w, new_m, new_v = {}, {}, {}
        for n in TWIN_WEIGHTS:
            delta_w[n], new_m[n], new_v[n] = _adamw(weights[n], grad_w[n], given["m_" + n], given["v_" + n])
    return (loss, grad_x, *[grad_w[n] for n in TWIN_WEIGHTS], *[delta_w[n] for n in TWIN_WEIGHTS],
            *[new_m[n] for n in TWIN_WEIGHTS], *[new_v[n] for n in TWIN_WEIGHTS])
```

```python
import functools
import math

import jax
import jax.numpy as jnp
from jax import lax
from jax.experimental import pallas as pl
from jax.experimental.pallas import tpu as pltpu

F32 = jnp.float32
BF16 = jnp.bfloat16
S = jax.ShapeDtypeStruct

N_DEV = 8
D = 1024
D_SSM = 384
N_GRP = 24
GRP = 16
N_STATE = 64
D_ST = N_GRP * N_STATE
D_POOL = 256
POOL_WINDOWS = (2, 4, 8, 16)
HALO = 16
D_SGU = 384
HEADS = 6
HEAD_DIM = 64
CHUNK = 128
D_IN = 1408
D_FF = 2816
EPS = 1e-6
SCAN_BLK = 8

ADAM_LR = 0.001
ADAM_B1 = 0.9
ADAM_B2 = 0.999
ADAM_EPS = 1e-08
ADAM_WD = 0.01
ADAM_STEP = 10

GELU_C0 = math.sqrt(2.0 / math.pi)
GELU_C1 = 0.044715

TT_MIX = 256
TT_FFN = 256
TT_PROJ = 512
TK_WGRAD = 512
VMEM_MB = 2 ** 20


def _cp(vmem_mb, grid_dims=0):
    kw = dict(vmem_limit_bytes=int(vmem_mb * VMEM_MB))
    if grid_dims:
        kw["dimension_semantics"] = ("arbitrary",) * grid_dims
    return pltpu.CompilerParams(**kw)


def _row(tt, n):
    return pl.BlockSpec((tt, n), lambda i: (i, 0))


def _full(shape):
    nd = len(shape)
    return pl.BlockSpec(shape, lambda *_: (0,) * nd)


def _nn(a, b):
    return jnp.dot(a, b, preferred_element_type=F32)


def _nt(a, b):
    return lax.dot_general(a, b, (((1,), (1,)), ((), ())), preferred_element_type=F32)


def _tn(a, b):
    return lax.dot_general(a, b, (((0,), (0,)), ((), ())), preferred_element_type=F32)


def _rowsum(x):
    return jnp.sum(x, axis=0, keepdims=True)


def _rms(x):
    r = lax.rsqrt(jnp.mean(x * x, axis=-1, keepdims=True) + EPS)
    return x * r, r


def _rms_bwd(dy, xn, r, g):
    dyg = dy * g
    return r * (dyg - xn * jnp.mean(dyg * xn, axis=-1, keepdims=True))


def _gelu(x):
    t = jnp.tanh(GELU_C0 * (x + GELU_C1 * x * x * x))
    return 0.5 * x * (1.0 + t), t


def _gelu_grad(x, t):
    return 0.5 * (1.0 + t) + 0.5 * x * (1.0 - t * t) * (GELU_C0 * (1.0 + 3.0 * GELU_C1 * x * x))


def _discretise(a_re, a_im, ldt, b_re, b_im):
    dt = jnp.exp(ldt)
    mag = jnp.exp(a_re * dt)
    ar = mag * jnp.cos(a_im * dt)
    ai = mag * jnp.sin(a_im * dt)
    den = a_re * a_re + a_im * a_im
    f_re = ((ar - 1.0) * a_re + ai * a_im) / den
    f_im = (ai * a_re - (ar - 1.0) * a_im) / den
    bb_re = f_re * b_re - f_im * b_im
    bb_im = f_re * b_im + f_im * b_re
    return ar, ai, bb_re, bb_im


def _group_mask(rows, cols):
    r = lax.broadcasted_iota(jnp.int32, (rows, cols), 0) // GRP
    c = lax.broadcasted_iota(jnp.int32, (rows, cols), 1)
    c = jnp.where(c >= D_ST, c - D_ST, c) // N_STATE
    return r == c


def s5_prepare(a_re, a_im, ldt, b_re_t, b_im_t, c_re_t, c_im_t):
    def body(are_ref, aim_ref, ldt_ref, bre_ref, bim_ref, cre_ref, cim_ref, scf_ref, scb_ref, bbd_ref, cbd_ref):
        ar, ai, bb_re, bb_im = _discretise(are_ref[...], aim_ref[...], ldt_ref[...], bre_ref[...], bim_ref[...])
        mask = _group_mask(D_SSM, 2 * D_ST)
        bb = jnp.concatenate([jnp.tile(bb_re, (N_GRP, 1)), jnp.tile(bb_im, (N_GRP, 1))], axis=1)
        bbd_ref[...] = jnp.where(mask, bb, 0.0).astype(BF16)
        cc = jnp.concatenate([jnp.tile(cre_ref[...], (N_GRP, 1)), -jnp.tile(cim_ref[...], (N_GRP, 1))], axis=1)
        cbd_ref[...] = jnp.where(mask, cc, 0.0).astype(BF16)
        pw = [(ar, ai)]
        for _ in range(SCAN_BLK - 1):
            pr, pi = pw[-1]
            pw.append((pr * ar - pi * ai, pr * ai + pi * ar))
        row = lax.broadcasted_iota(jnp.int32, (SCAN_BLK, D_ST), 0)

        def stack(vals):
            out = jnp.zeros((SCAN_BLK, D_ST), F32)
            for j, v in enumerate(vals):
                out = jnp.where(row == j, v, out)
            return out

        scf_ref[0:8, :] = stack([pw[j][0] for j in range(SCAN_BLK)])
        scf_ref[8:16, :] = stack([pw[j][1] for j in range(SCAN_BLK)])
        scb_ref[0:8, :] = stack([pw[SCAN_BLK - 1 - j][0] for j in range(SCAN_BLK)])
        scb_ref[8:16, :] = stack([-pw[SCAN_BLK - 1 - j][1] for j in range(SCAN_BLK)])
        for k, d in enumerate((1, 2, 4)):
            pr, pi = pw[d - 1]
            scf_ref[16 + 16 * k:24 + 16 * k, :] = jnp.where(row >= d, pr, 0.0)
            scf_ref[24 + 16 * k:32 + 16 * k, :] = jnp.where(row >= d, pi, 0.0)
            scb_ref[16 + 16 * k:24 + 16 * k, :] = jnp.where(row <= SCAN_BLK - 1 - d, pr, 0.0)
            scb_ref[24 + 16 * k:32 + 16 * k, :] = jnp.where(row <= SCAN_BLK - 1 - d, -pi, 0.0)

    return pl.pallas_call(
        body, name="s5_prepare",
        out_shape=[S((64, D_ST), F32), S((64, D_ST), F32), S((D_SSM, 2 * D_ST), BF16), S((D_SSM, 2 * D_ST), BF16)],
        compiler_params=_cp(40),
    )(a_re, a_im, ldt, b_re_t, b_im_t, c_re_t, c_im_t)


def s5_param_bwd(a_re, a_im, ldt, b_re_t, b_im_t, da, dbt):
    def body(are_ref, aim_ref, ldt_ref, bre_ref, bim_ref, da_ref, dbt_ref, o_are, o_aim, o_ldt, o_bre, o_bim):
        _, vjp = jax.vjp(_discretise, are_ref[...], aim_ref[...], ldt_ref[...], bre_ref[...], bim_ref[...])
        da = da_ref[...]
        dbt = dbt_ref[...]
        g_are, g_aim, g_ldt, g_bre, g_bim = vjp((da[:, :D_ST], da[:, D_ST:], dbt[:, :D_ST], dbt[:, D_ST:]))
        o_are[...] = g_are
        o_aim[...] = g_aim
        o_bre[...] = g_bre
        o_bim[...] = g_bim
        grp = lax.broadcasted_iota(jnp.int32, (1, D_ST), 1) // N_STATE
        lane = lax.broadcasted_iota(jnp.int32, (1, 128), 1)
        out = jnp.zeros((1, 128), F32)
        for g in range(N_GRP):
            out = jnp.where(lane == g, jnp.sum(jnp.where(grp == g, g_ldt, 0.0), axis=1, keepdims=True), out)
        o_ldt[...] = out

    return pl.pallas_call(
        body, name="s5_param_bwd",
        out_shape=[S((1, D_ST), F32), S((1, D_ST), F32), S((1, 128), F32), S((GRP, D_ST), F32), S((GRP, D_ST), F32)],
        compiler_params=_cp(16),
    )(a_re, a_im, ldt, b_re_t, b_im_t, da, dbt)


def _scan_blocks(buf_ref, sc_ref, carry_ref, n_blk, reverse):
    def step(i, carry):
        cr, ci = carry
        blk = (n_blk - 1 - i) if reverse else i
        r0 = pl.multiple_of(blk * SCAN_BLK, SCAN_BLK)
        xr = buf_ref[pl.ds(r0, SCAN_BLK), 0:D_ST]
        xi = buf_ref[pl.ds(r0, SCAN_BLK), D_ST:2 * D_ST]
        for k, d in enumerate((1, 2, 4)):
            ar = sc_ref[16 + 16 * k:24 + 16 * k, :]
            ai = sc_ref[24 + 16 * k:32 + 16 * k, :]
            sh = (SCAN_BLK - d) if reverse else d
            rr = pltpu.roll(xr, sh, axis=0)
            ri = pltpu.roll(xi, sh, axis=0)
            xr, xi = xr + ar * rr - ai * ri, xi + ar * ri + ai * rr
        pr = sc_ref[0:8, :]
        pi = sc_ref[8:16, :]
        xr, xi = xr + pr * cr - pi * ci, xi + pr * ci + pi * cr
        buf_ref[pl.ds(r0, SCAN_BLK), 0:D_ST] = xr
        buf_ref[pl.ds(r0, SCAN_BLK), D_ST:2 * D_ST] = xi
        edge = 0 if reverse else SCAN_BLK - 1
        return xr[edge:edge + 1, :], xi[edge:edge + 1, :]

    cr, ci = lax.fori_loop(0, n_blk, step, (carry_ref[0:1, 0:D_ST], carry_ref[0:1, D_ST:2 * D_ST]))
    carry_ref[0:1, 0:D_ST] = cr
    carry_ref[0:1, D_ST:2 * D_ST] = ci


def _lane_windows(n):
    lane = lax.broadcasted_iota(jnp.int32, (1, n), 1)
    return lane // (D_POOL // len(POOL_WINDOWS))


def _select_window(grp, s2, s4, s8, s16):
    return jnp.where(grp == 0, s2, jnp.where(grp == 1, s4, jnp.where(grp == 2, s8, s16)))


def _pool_fwd(pbuf_ref, zb, halo, tile_idx, tt):
    pbuf_ref[0:HALO, :] = halo
    pbuf_ref[HALO:HALO + tt, :] = zb
    x = pbuf_ref[...]
    s2 = x + pltpu.roll(x, 1, axis=0)
    s4 = s2 + pltpu.roll(s2, 2, axis=0)
    s8 = s4 + pltpu.roll(s4, 4, axis=0)
    s16 = s8 + pltpu.roll(s8, 8, axis=0)
    grp = _lane_windows(D_POOL)
    win = _select_window(grp, s2, s4, s8, s16)[HALO:HALO + tt, :]
    width = _select_window(grp, 2.0, 4.0, 8.0, 16.0).astype(F32)
    pos = (tile_idx * tt + 1 + lax.broadcasted_iota(jnp.int32, (tt, 1), 0)).astype(F32)
    cnt = jnp.minimum(pos, width)
    return win / cnt - zb, cnt


def _sgu_fwd(zu, zv, lng, lnb, wsm_ref, bsp, mix_ref, tt):
    u, tu = _gelu(zu)
    v, tv = _gelu(zv)
    mu = jnp.mean(v, axis=-1, keepdims=True)
    vc = v - mu
    rstd = lax.rsqrt(jnp.mean(vc * vc, axis=-1, keepdims=True) + EPS)
    vhat = vc * rstd
    vnb = (vhat * lng + lnb).astype(BF16)
    for c in range(tt // CHUNK):
        rows = slice(c * CHUNK, (c + 1) * CHUNK)
        parts = [_nn(wsm_ref[h], vnb[rows, h * HEAD_DIM:(h + 1) * HEAD_DIM]) for h in range(HEADS)]
        mix_ref[rows, :] = jnp.concatenate(parts, axis=1) + bsp
    return u, tu, tv, vhat, rstd, vnb


def mixer_fwd(z, sc_f, bbd, cbd, dskip, wglu, bglu, wpool, pscale, lng, lnb, wsm, bsp):
    T = z.shape[0]
    tt = TT_MIX
    n_tiles = T // tt

    def body(z_ref, scf_ref, bbd_ref, cbd_ref, dskip_ref, wglu_ref, bglu_ref, wpool_ref, pscale_ref, lng_ref, lnb_ref,
             wsm_ref, bsp_ref, ycat_ref, hs_ref, ys_ref, carry_ref, halo_ref, pbuf_ref, mix_ref):
        i = pl.program_id(0)

        @pl.when(i == 0)
        def _():
            carry_ref[...] = jnp.zeros_like(carry_ref)
            halo_ref[...] = jnp.zeros_like(halo_ref)

        za = z_ref[:, 0:D_SSM]
        zb = z_ref[:, D_SSM:D_SSM + D_POOL]
        zu = z_ref[:, D_SSM + D_POOL:D_SSM + D_POOL + D_SGU]
        zv = z_ref[:, D_SSM + D_POOL + D_SGU:D_IN]
        hs_ref[...] = _nn(za.astype(BF16), bbd_ref[...])
        _scan_blocks(hs_ref, scf_ref, carry_ref, tt // SCAN_BLK, reverse=False)
        y = _nt(hs_ref[...].astype(BF16), cbd_ref[...]) + dskip_ref[...] * za
        ys_ref[...] = y
        g, _ = _gelu(y)
        q = _nn(g.astype(BF16), wglu_ref[...]) + bglu_ref[...]
        ycat_ref[:, 0:D_SSM] = (g * jax.nn.sigmoid(q)).astype(BF16)
        pooled, _ = _pool_fwd(pbuf_ref, zb, halo_ref[...], i, tt)
        halo_ref[...] = zb[tt - HALO:tt, :]
        ycat_ref[:, D_SSM:D_SSM + D_POOL] = (_nn(pooled.astype(BF16), wpool_ref[...]) * pscale_ref[...]).astype(BF16)
        u, _, _, _, _, _ = _sgu_fwd(zu, zv, lng_ref[...], lnb_ref[...], wsm_ref, bsp_ref[...], mix_ref, tt)
        ycat_ref[:, D_SSM + D_POOL:D] = (u * mix_ref[...]).astype(BF16)

    return pl.pallas_call(
        body, name="mixer_fwd", grid=(n_tiles,),
        in_specs=[_row(tt, D_IN), _full((64, D_ST)), _full((D_SSM, 2 * D_ST)), _full((D_SSM, 2 * D_ST)),
                  _full((1, D_SSM)), _full((D_SSM, D_SSM)), _full((1, D_SSM)), _full((D_POOL, D_POOL)),
                  _full((1, D_POOL)), _full((1, D_SGU)), _full((1, D_SGU)), _full((HEADS, CHUNK, CHUNK)),
                  _full((CHUNK, D_SGU))],
        out_specs=[_row(tt, D), _row(tt, 2 * D_ST), _row(tt, D_SSM)],
        out_shape=[S((T, D), BF16), S((T, 2 * D_ST), F32), S((T, D_SSM), F32)],
        scratch_shapes=[pltpu.VMEM((SCAN_BLK, 2 * D_ST), F32), pltpu.VMEM((HALO, D_POOL), F32),
                        pltpu.VMEM((tt + HALO, D_POOL), F32), pltpu.VMEM((tt, D_SGU), F32)],
        compiler_params=_cp(48, 1),
    )(z, sc_f, bbd, cbd, dskip, wglu, bglu, wpool, pscale, lng, lnb, wsm, bsp)


def mixer_bwd(dx1b, z, hs, ys, wout, sc_b, bbd, cbd, dskip, wglu, bglu, wpool, pscale, lng, lnb, wsm, wsmt, bsp):
    T = z.shape[0]
    tt = TT_MIX
    n_tiles = T // tt

    def rev(i):
        return n_tiles - 1 - i

    def body(dx_ref, z_ref, zprev_ref, hs_ref, hsprev_ref, ys_ref, wout_ref, scb_ref, bbd_ref, cbd_ref, dskip_ref,
             wglu_ref, bglu_ref, wpool_ref, pscale_ref, lng_ref, lnb_ref, wsm_ref, wsmt_ref, bsp_ref,
             dz_ref, o_da, o_dbt, o_dct, o_dd, o_dbglu, o_dwglu, o_dwpool, o_dpscale, o_dlng, o_dlnb, o_dws, o_dbsp,
             gbuf_ref, carry_ref, ehalo_ref, pbuf_ref, mix_ref, dvn_ref, accb_ref, accc_ref, accw_ref, accm_ref):
        i = pl.program_id(0)
        tile = rev(i)

        @pl.when(i == 0)
        def _():
            carry_ref[...] = jnp.zeros_like(carry_ref)
            ehalo_ref[...] = jnp.zeros_like(ehalo_ref)
            accb_ref[...] = jnp.zeros_like(accb_ref)
            accc_ref[...] = jnp.zeros_like(accc_ref)
            accw_ref[...] = jnp.zeros_like(accw_ref)
            accm_ref[...] = jnp.zeros_like(accm_ref)
            for o in (o_da, o_dd, o_dbglu, o_dwglu, o_dwpool, o_dpscale, o_dlng, o_dlnb):
                o[...] = jnp.zeros_like(o)

        dycat = _nt(dx_ref[...], wout_ref[...])
        d_a = dycat[:, 0:D_SSM]
        d_b = dycat[:, D_SSM:D_SSM + D_POOL]
        d_c = dycat[:, D_SSM + D_POOL:D]
        za = z_ref[:, 0:D_SSM]
        zb = z_ref[:, D_SSM:D_SSM + D_POOL]
        zu = z_ref[:, D_SSM + D_POOL:D_SSM + D_POOL + D_SGU]
        zv = z_ref[:, D_SSM + D_POOL + D_SGU:D_IN]
        first_tile = (tile > 0).astype(F32)

        y = ys_ref[...]
        g, tg = _gelu(y)
        gb = g.astype(BF16)
        sg = jax.nn.sigmoid(_nn(gb, wglu_ref[...]) + bglu_ref[...])
        dq = d_a * g * sg * (1.0 - sg)
        dqb = dq.astype(BF16)
        o_dbglu[...] += _rowsum(dq)
        o_dwglu[...] += _tn(gb, dqb)
        dy = (d_a * sg + _nt(dqb, wglu_ref[...])) * _gelu_grad(y, tg)
        o_dd[...] += _rowsum(dy * za)
        dyb = dy.astype(BF16)
        hs_t = hs_ref[...]
        accc_ref[...] += _tn(dyb, hs_t.astype(BF16))
        gbuf_ref[...] = _nn(dyb, cbd_ref[...])
        _scan_blocks(gbuf_ref, scb_ref, carry_ref, tt // SCAN_BLK, reverse=True)
        gt = gbuf_ref[...]
        row = lax.broadcasted_iota(jnp.int32, (tt, 1), 0)
        hprev = hsprev_ref[SCAN_BLK - 1:SCAN_BLK, :] * first_tile
        hsh = jnp.where(row == 0, hprev, pltpu.roll(hs_t, 1, axis=0))
        gr, gi = gt[:, 0:D_ST], gt[:, D_ST:]
        hr, hi = hsh[:, 0:D_ST], hsh[:, D_ST:]
        o_da[:, 0:D_ST] += _rowsum(gr * hr + gi * hi)
        o_da[:, D_ST:] += _rowsum(gi * hr - gr * hi)
        gtb = gt.astype(BF16)
        accb_ref[...] += _tn(za.astype(BF16), gtb)
        dz_ref[:, 0:D_SSM] = (dy * dskip_ref[...] + _nt(gtb, bbd_ref[...])).astype(BF16)

        pooled, cnt = _pool_fwd(pbuf_ref, zb, zprev_ref[:, D_SSM:D_SSM + D_POOL] * first_tile, tile, tt)
        pooledb = pooled.astype(BF16)
        mixed = _nn(pooledb, wpool_ref[...])
        o_dpscale[...] += _rowsum(d_b * mixed)
        dmixb = (d_b * pscale_ref[...]).astype(BF16)
        o_dwpool[...] += _tn(pooledb, dmixb)
        dpooled = _nt(dmixb, wpool_ref[...])
        e = dpooled / cnt
        pbuf_ref[0:tt, :] = e
        pbuf_ref[tt:tt + HALO, :] = ehalo_ref[...]
        ehalo_ref[...] = e[0:HALO, :]
        x = pbuf_ref[...]
        n = tt + HALO
        f2 = x + pltpu.roll(x, n - 1, axis=0)
        f4 = f2 + pltpu.roll(f2, n - 2, axis=0)
        f8 = f4 + pltpu.roll(f4, n - 4, axis=0)
        f16 = f8 + pltpu.roll(f8, n - 8, axis=0)
        fwd_sum = _select_window(_lane_windows(D_POOL), f2, f4, f8, f16)[0:tt, :]
        dz_ref[:, D_SSM:D_SSM + D_POOL] = (fwd_sum - dpooled).astype(BF16)

        lng = lng_ref[...]
        u, tu, tv, vhat, rstd, vnb = _sgu_fwd(zu, zv, lng, lnb_ref[...], wsm_ref, bsp_ref[...], mix_ref, tt)
        dz_ref[:, D_SSM + D_POOL:D_SSM + D_POOL + D_SGU] = (d_c * mix_ref[...] * _gelu_grad(zu, tu)).astype(BF16)
        dmix = d_c * u
        dmixb2 = dmix.astype(BF16)
        for c in range(tt // CHUNK):
            rows = slice(c * CHUNK, (c + 1) * CHUNK)
            accm_ref[...] += dmix[rows, :]
            parts = []
            for h in range(HEADS):
                cols = slice(h * HEAD_DIM, (h + 1) * HEAD_DIM)
                accw_ref[h] += _nt(dmixb2[rows, cols], vnb[rows, cols])
                parts.append(_nn(wsmt_ref[h], dmixb2[rows, cols]))
            dvn_ref[rows, :] = jnp.concatenate(parts, axis=1)
        dvn = dvn_ref[...]
        o_dlng[...] += _rowsum(dvn * vhat)
        o_dlnb[...] += _rowsum(dvn)
        dvh = dvn * lng
        dv = rstd * (dvh - jnp.mean(dvh, axis=-1, keepdims=True) - vhat * jnp.mean(dvh * vhat, axis=-1, keepdims=True))
        dz_ref[:, D_SSM + D_POOL + D_SGU:D_IN] = (dv * _gelu_grad(zv, tv)).astype(BF16)

        @pl.when(i == n_tiles - 1)
        def _():
            mask = _group_mask(D_SSM, 2 * D_ST)
            fb = jnp.zeros((GRP, 2 * D_ST), F32)
            fc = jnp.zeros((GRP, 2 * D_ST), F32)
            for gidx in range(N_GRP):
                rows = slice(gidx * GRP, (gidx + 1) * GRP)
                fb = fb + jnp.where(mask[rows, :], accb_ref[rows, :], 0.0)
                fc = fc + jnp.where(mask[rows, :], accc_ref[rows, :], 0.0)
            o_dbt[...] = fb
            o_dct[...] = fc
            tri = (lax.broadcasted_iota(jnp.int32, (CHUNK, CHUNK), 0) >= lax.broadcasted_iota(jnp.int32, (CHUNK, CHUNK), 1))
            for h in range(HEADS):
                o_dws[h] = jnp.where(tri, accw_ref[h], 0.0)
            lane = lax.broadcasted_iota(jnp.int32, (1, 128), 1)
            acc = jnp.zeros((CHUNK, 128), F32)
            for h in range(HEADS):
                sh = jnp.sum(accm_ref[:, h * HEAD_DIM:(h + 1) * HEAD_DIM], axis=1, keepdims=True)
                acc = jnp.where(lane == h, sh, acc)
            o_dbsp[...] = acc

    def rowr(n):
        return pl.BlockSpec((tt, n), lambda i: (rev(i), 0))

    zprev_spec = pl.BlockSpec((HALO, D_IN), lambda i: (jnp.maximum(rev(i) * (tt // HALO) - 1, 0), 0))
    hsprev_spec = pl.BlockSpec((SCAN_BLK, 2 * D_ST), lambda i: (jnp.maximum(rev(i) * (tt // SCAN_BLK) - 1, 0), 0))
    small = [S((1, 2 * D_ST), F32), S((GRP, 2 * D_ST), F32), S((GRP, 2 * D_ST), F32), S((1, D_SSM), F32),
             S((1, D_SSM), F32), S((D_SSM, D_SSM), F32), S((D_POOL, D_POOL), F32), S((1, D_POOL), F32),
             S((1, D_SGU), F32), S((1, D_SGU), F32), S((HEADS, CHUNK, CHUNK), F32), S((CHUNK, 128), F32)]
    return pl.pallas_call(
        body, name="mixer_bwd", grid=(n_tiles,),
        in_specs=[rowr(D), rowr(D_IN), zprev_spec, rowr(2 * D_ST), hsprev_spec, rowr(D_SSM), _full((D, D)),
                  _full((64, D_ST)), _full((D_SSM, 2 * D_ST)), _full((D_SSM, 2 * D_ST)), _full((1, D_SSM)),
                  _full((D_SSM, D_SSM)), _full((1, D_SSM)), _full((D_POOL, D_POOL)), _full((1, D_POOL)),
                  _full((1, D_SGU)), _full((1, D_SGU)), _full((HEADS, CHUNK, CHUNK)), _full((HEADS, CHUNK, CHUNK)),
                  _full((CHUNK, D_SGU))],
        out_specs=[rowr(D_IN)] + [_full(s.shape) for s in small],
        out_shape=[S((T, D_IN), BF16)] + small,
        scratch_shapes=[pltpu.VMEM((tt, 2 * D_ST), F32), pltpu.VMEM((SCAN_BLK, 2 * D_ST), F32),
                        pltpu.VMEM((HALO, D_POOL), F32), pltpu.VMEM((tt + HALO, D_POOL), F32),
                        pltpu.VMEM((tt, D_SGU), F32), pltpu.VMEM((tt, D_SGU), F32),
                        pltpu.VMEM((D_SSM, 2 * D_ST), F32), pltpu.VMEM((D_SSM, 2 * D_ST), F32),
                        pltpu.VMEM((HEADS, CHUNK, CHUNK), F32), pltpu.VMEM((CHUNK, D_SGU), F32)],
        compiler_params=_cp(56, 1),
    )(dx1b, z, z, hs, hs, ys, wout, sc_b, bbd, cbd, dskip, wglu, bglu, wpool, pscale, lng, lnb, wsm, wsmt, bsp)


def inproj_fwd(x, g, w_t):
    T = x.shape[0]
    tt = TT_PROJ

    def body(x_ref, g_ref, w_ref, h_ref, z_ref):
        xn, _ = _rms(x_ref[...])
        h = (xn * g_ref[...]).astype(BF16)
        h_ref[...] = h
        z_ref[...] = _nt(h, w_ref[...])

    return pl.pallas_call(
        body, name="inproj_fwd", grid=(T // tt,),
        in_specs=[_row(tt, D), _full((1, D)), _full((D_IN, D))],
        out_specs=[_row(tt, D), _row(tt, D_IN)],
        out_shape=[S((T, D), BF16), S((T, D_IN), F32)],
        compiler_params=_cp(40, 1),
    )(x, g, w_t)


def inproj_bwd(dzb, x, g, w_t, dx1):
    T = x.shape[0]
    tt = TT_PROJ

    def body(dz_ref, x_ref, g_ref, w_ref, dx1_ref, dx_ref, dg_ref):
        @pl.when(pl.program_id(0) == 0)
        def _():
            dg_ref[...] = jnp.zeros_like(dg_ref)

        dh = _nn(dz_ref[...], w_ref[...])
        xn, r = _rms(x_ref[...])
        dg_ref[...] += _rowsum(dh * xn)
        dx_ref[...] = dx1_ref[...] + _rms_bwd(dh, xn, r, g_ref[...])

    return pl.pallas_call(
        body, name="inproj_bwd", grid=(T // tt,),
        in_specs=[_row(tt, D_IN), _row(tt, D), _full((1, D)), _full((D_IN, D)), _row(tt, D)],
        out_specs=[_row(tt, D), _full((1, D))],
        out_shape=[S((T, D), F32), S((1, D), F32)],
        compiler_params=_cp(40, 1),
    )(dzb, x, g, w_t, dx1)


def _load_weights(pairs, sem):
    @pl.when(pl.program_id(0) == 0)
    def _():
        copies = [pltpu.make_async_copy(src, dst, sem.at[k]) for k, (src, dst) in enumerate(pairs)]
        for cp in copies:
            cp.start()
        for cp in copies:
            cp.wait()


def ffn_fwd(x, ycat, wout, g, wg_t, wu_t, wd):
    T = x.shape[0]
    tt = TT_FFN
    any_spec = pl.BlockSpec(memory_space=pl.ANY)

    def body(x_ref, ycat_ref, g_ref, wout_hbm, wg_hbm, wu_hbm, wd_hbm,
             x1_ref, h_ref, gate_ref, up_ref, act_ref, x2_ref, wout_v, wg_v, wu_v, wd_v, sem):
        _load_weights([(wout_hbm, wout_v), (wg_hbm, wg_v), (wu_hbm, wu_v), (wd_hbm, wd_v)], sem)
        x1 = x_ref[...] + _nn(ycat_ref[...], wout_v[...])
        x1_ref[...] = x1
        xn, _ = _rms(x1)
        h = (xn * g_ref[...]).astype(BF16)
        h_ref[...] = h
        gate = _nt(h, wg_v[...])
        up = _nt(h, wu_v[...])
        gate_ref[...] = gate.astype(BF16)
        up_ref[...] = up.astype(BF16)
        act = (gate * jax.nn.sigmoid(gate) * up).astype(BF16)
        act_ref[...] = act
        x2_ref[...] = x1 + _nn(act, wd_v[...])

    return pl.pallas_call(
        body, name="ffn_fwd", grid=(T // tt,),
        in_specs=[_row(tt, D), _row(tt, D), _full((1, D)), any_spec, any_spec, any_spec, any_spec],
        out_specs=[_row(tt, D), _row(tt, D), _row(tt, D_FF), _row(tt, D_FF), _row(tt, D_FF), _row(tt, D)],
        out_shape=[S((T, D), F32), S((T, D), BF16), S((T, D_FF), BF16), S((T, D_FF), BF16), S((T, D_FF), BF16),
                   S((T, D), F32)],
        scratch_shapes=[pltpu.VMEM((D, D), BF16), pltpu.VMEM((D_FF, D), BF16), pltpu.VMEM((D_FF, D), BF16),
                        pltpu.VMEM((D_FF, D), BF16), pltpu.SemaphoreType.DMA((4,))],
        compiler_params=_cp(56, 1),
    )(x, ycat, g, wout, wg_t, wu_t, wd)


def ffn_bwd(dx2, x1, gate, up, g, wg_t, wu_t, wd):
    T = x1.shape[0]
    tt = TT_FFN
    any_spec = pl.BlockSpec(memory_space=pl.ANY)

    def body(dx2_ref, x1_ref, gate_ref, up_ref, g_ref, wg_hbm, wu_hbm, wd_hbm,
             dgu_ref, dx2b_ref, dx1_ref, dx1b_ref, dg_ref, wg_v, wu_v, wd_v, sem):
        _load_weights([(wg_hbm, wg_v), (wu_hbm, wu_v), (wd_hbm, wd_v)], sem)

        @pl.when(pl.program_id(0) == 0)
        def _():
            dg_ref[...] = jnp.zeros_like(dg_ref)

        dx2 = dx2_ref[...]
        dx2b = dx2.astype(BF16)
        dx2b_ref[...] = dx2b
        dact = _nt(dx2b, wd_v[...])
        gate = gate_ref[...].astype(F32)
        up = up_ref[...].astype(F32)
        sg = jax.nn.sigmoid(gate)
        dgate = (dact * up * (sg * (1.0 + gate * (1.0 - sg)))).astype(BF16)
        dup = (dact * gate * sg).astype(BF16)
        dgu_ref[:, 0:D_FF] = dgate
        dgu_ref[:, D_FF:2 * D_FF] = dup
        dh = _nn(dgate, wg_v[...]) + _nn(dup, wu_v[...])
        xn, r = _rms(x1_ref[...])
        dg_ref[...] += _rowsum(dh * xn)
        dx1 = dx2 + _rms_bwd(dh, xn, r, g_ref[...])
        dx1_ref[...] = dx1
        dx1b_ref[...] = dx1.astype(BF16)

    return pl.pallas_call(
        body, name="ffn_bwd", grid=(T // tt,),
        in_specs=[_row(tt, D), _row(tt, D), _row(tt, D_FF), _row(tt, D_FF), _full((1, D)), any_spec, any_spec, any_spec],
        out_specs=[_row(tt, 2 * D_FF), _row(tt, D), _row(tt, D), _row(tt, D), _full((1, D))],
        out_shape=[S((T, 2 * D_FF), BF16), S((T, D), BF16), S((T, D), F32), S((T, D), BF16), S((1, D), F32)],
        scratch_shapes=[pltpu.VMEM((D_FF, D), BF16), pltpu.VMEM((D_FF, D), BF16), pltpu.VMEM((D_FF, D), BF16),
                        pltpu.SemaphoreType.DMA((3,))],
        compiler_params=_cp(56, 1),
    )(dx2, x1, gate, up, g, wg_t, wu_t, wd)


def loss_head(x, target, g):
    T = x.shape[0]
    tt = TT_PROJ

    def body(x_ref, t_ref, g_ref, dx_ref, lvec_ref, dg_ref):
        @pl.when(pl.program_id(0) == 0)
        def _():
            lvec_ref[...] = jnp.zeros_like(lvec_ref)
            dg_ref[...] = jnp.zeros_like(dg_ref)

        xn, r = _rms(x_ref[...])
        gg = g_ref[...]
        err = xn * gg - t_ref[...]
        lvec_ref[...] += _rowsum(err * err)
        dy = err * (1.0 / D)
        dg_ref[...] += _rowsum(dy * xn)
        dx_ref[...] = _rms_bwd(dy, xn, r, gg)

    return pl.pallas_call(
        body, name="loss_head", grid=(T // tt,),
        in_specs=[_row(tt, D), _row(tt, D), _full((1, D))],
        out_specs=[_row(tt, D), _full((1, D)), _full((1, D))],
        out_shape=[S((T, D), F32), S((1, D), F32), S((1, D), F32)],
        compiler_params=_cp(32, 1),
    )(x, target, g)


def wgrad(a, b, tm, name):
    T, M = a.shape
    N = b.shape[1]
    tk = min(TK_WGRAD, T)
    n_k = T // tk

    def body(a_ref, b_ref, o_ref, acc_ref):
        k = pl.program_id(1)

        @pl.when(k == 0)
        def _():
            acc_ref[...] = jnp.zeros_like(acc_ref)

        acc_ref[...] += _tn(a_ref[...], b_ref[...])

        @pl.when(k == n_k - 1)
        def _():
            o_ref[...] = acc_ref[...].astype(BF16)

    return pl.pallas_call(
        body, name=name, grid=(M // tm, n_k),
        in_specs=[pl.BlockSpec((tk, tm), lambda m, k: (k, m)), pl.BlockSpec((tk, N), lambda m, k: (k, 0))],
        out_specs=pl.BlockSpec((tm, N), lambda m, k: (m, 0)),
        out_shape=S((M, N), BF16),
        scratch_shapes=[pltpu.VMEM((tm, N), F32)],
        compiler_params=_cp(48, 2),
    )(a, b)


def _mesh_place():
    x, y, c = lax.axis_index("x"), lax.axis_index("y"), lax.axis_index("c")
    return x, y, c, 4 * x + 2 * y + c


def _peer(x, y, c, k):
    px = 1 - x if k & 4 else x
    py = 1 - y if k & 2 else y
    pc = 1 - c if k & 1 else c
    return (px, py, pc), 4 * px + 2 * py + pc


def gather_weights(shards):
    n_arr = len(shards)

    def body(*refs):
        ins, outs = refs[:n_arr], refs[n_arr:2 * n_arr]
        send_sems, recv_sems, local_sems = refs[2 * n_arr:]
        x, y, c, me = _mesh_place()
        local = [pltpu.make_async_copy(ins[a], outs[a].at[me], local_sems.at[a]) for a in range(n_arr)]
        for cp in local:
            cp.start()
        sends = []
        for k in range(1, N_DEV):
            peer, _ = _peer(x, y, c, k)
            for a in range(n_arr):
                cp = pltpu.make_async_remote_copy(
                    src_ref=ins[a], dst_ref=outs[a].at[me], send_sem=send_sems.at[a, k], recv_sem=recv_sems.at[a, k],
                    device_id=peer, device_id_type=pl.DeviceIdType.MESH)
                cp.start()
                sends.append(cp)
        for k in range(1, N_DEV):
            peer, pidx = _peer(x, y, c, k)
            for a in range(n_arr):
                pltpu.make_async_remote_copy(
                    src_ref=ins[a], dst_ref=outs[a].at[pidx], send_sem=send_sems.at[a, k], recv_sem=recv_sems.at[a, k],
                    device_id=peer, device_id_type=pl.DeviceIdType.MESH).wait_recv()
        for cp in sends:
            cp.wait_send()
        for cp in local:
            cp.wait()

    any_spec = pl.BlockSpec(memory_space=pl.ANY)
    return pl.pallas_call(
        body, name="gather_weights",
        in_specs=[any_spec] * n_arr, out_specs=[any_spec] * n_arr,
        out_shape=[S((N_DEV,) + s.shape, s.dtype) for s in shards],
        scratch_shapes=[pltpu.SemaphoreType.DMA((n_arr, N_DEV)), pltpu.SemaphoreType.DMA((n_arr, N_DEV)),
                        pltpu.SemaphoreType.DMA((n_arr,))],
    )(*shards)


def exchange_grads(grads, small):
    n_arr = len(grads)
    rows = [g.shape[0] // N_DEV for g in grads]

    def body(*refs):
        ins, sm_in = refs[:n_arr], refs[n_arr]
        outs, sm_out = refs[n_arr + 1:2 * n_arr + 1], refs[2 * n_arr + 1]
        send_sems, recv_sems, local_sems = refs[2 * n_arr + 2:]
        x, y, c, me = _mesh_place()

        def block(a, idx):
            return ins[a].at[pl.ds(pl.multiple_of(idx * rows[a], 16), rows[a])]

        local = [pltpu.make_async_copy(block(a, me), outs[a].at[me], local_sems.at[a]) for a in range(n_arr)]
        local.append(pltpu.make_async_copy(sm_in, sm_out.at[me], local_sems.at[n_arr]))
        for cp in local:
            cp.start()

        def copies(k, sending):
            peer, pidx = _peer(x, y, c, k)
            slot = me if sending else pidx
            out = [pltpu.make_async_remote_copy(
                src_ref=block(a, pidx), dst_ref=outs[a].at[slot], send_sem=send_sems.at[a, k],
                recv_sem=recv_sems.at[a, k], device_id=peer, device_id_type=pl.DeviceIdType.MESH)
                for a in range(n_arr)]
            out.append(pltpu.make_async_remote_copy(
                src_ref=sm_in, dst_ref=sm_out.at[slot], send_sem=send_sems.at[n_arr, k],
                recv_sem=recv_sems.at[n_arr, k], device_id=peer, device_id_type=pl.DeviceIdType.MESH))
            return out

        sends = []
        for k in range(1, N_DEV):
            for cp in copies(k, True):
                cp.start()
                sends.append(cp)
        for k in range(1, N_DEV):
            for cp in copies(k, False):
                cp.wait_recv()
        for cp in sends:
            cp.wait_send()
        for cp in local:
            cp.wait()

    any_spec = pl.BlockSpec(memory_space=pl.ANY)
    return pl.pallas_call(
        body, name="exchange_grads",
        in_specs=[any_spec] * (n_arr + 1), out_specs=[any_spec] * (n_arr + 1),
        out_shape=[S((N_DEV, r, g.shape[1]), g.dtype) for r, g in zip(rows, grads)] + [S((N_DEV,) + small.shape, F32)],
        scratch_shapes=[pltpu.SemaphoreType.DMA((n_arr + 1, N_DEV)), pltpu.SemaphoreType.DMA((n_arr + 1, N_DEV)),
                        pltpu.SemaphoreType.DMA((n_arr + 1,))],
    )(*grads, small)


def _adamw(w, g, m, v):
    m = ADAM_B1 * m + (1.0 - ADAM_B1) * g
    v = ADAM_B2 * v + (1.0 - ADAM_B2) * (g * g)
    m_hat = m / (1.0 - ADAM_B1 ** ADAM_STEP)
    v_hat = v / (1.0 - ADAM_B2 ** ADAM_STEP)
    delta = -ADAM_LR * (m_hat / (jnp.sqrt(v_hat) + ADAM_EPS) + ADAM_WD * w)
    return delta, m, v


def adamw_sum(parts, w, m, v, name):
    def body(p_ref, w_ref, m_ref, v_ref, g_out, d_out, m_out, v_out):
        g = p_ref[0].astype(F32)
        for k in range(1, N_DEV):
            g = g + p_ref[k].astype(F32)
        g_out[...] = g
        d_out[...], m_out[...], v_out[...] = _adamw(w_ref[...], g, m_ref[...], v_ref[...])

    return pl.pallas_call(
        body, name=name, out_shape=[S(w.shape, F32)] * 4, compiler_params=_cp(48),
    )(parts, w, m, v)


SMALL_NAMES = ("g_mix", "A_re", "A_im", "log_dt", "B_re", "B_im", "C_re", "C_im", "D_skip", "b_glu", "w_pool",
               "pool_scale", "sgu_ln_g", "sgu_ln_b", "w_spatial", "b_spatial", "g_ffn", "g_final")
BIG_NAMES = ("w_in", "w_glu", "w_out", "w_gate", "w_up", "w_down")
COLUMN_SHARDED = ("w_in", "w_gate", "w_up")
WEIGHT_ORDER = ("g_mix", "w_in", "A_re", "A_im", "log_dt", "B_re", "B_im", "C_re", "C_im", "D_skip", "w_glu", "b_glu",
                "w_pool", "pool_scale", "sgu_ln_g", "sgu_ln_b", "w_spatial", "b_spatial", "w_out", "g_ffn", "w_gate",
                "w_up", "w_down", "g_final")
SEG = 1024


def _pack(arrays):
    parts = []
    for a in arrays:
        flat = a.reshape(-1)
        parts.append(jnp.pad(flat, (0, (-flat.shape[0]) % SEG)))
    return jnp.concatenate(parts).reshape(-1, 128)


def _unpack(flat, shapes):
    flat = flat.reshape(-1)
    out, off = [], 0
    for shp in shapes:
        n = math.prod(shp)
        out.append(flat[off:off + n].reshape(shp))
        off += n + (-n) % SEG
    return out


def _state_rows(p):
    return p.reshape(1, D_ST)


def _chan_by_state(p):
    return jnp.transpose(p, (2, 0, 1)).reshape(GRP, D_ST)


def _chan_by_state_c(p):
    return jnp.transpose(p, (1, 0, 2)).reshape(GRP, D_ST)


def kernel(x, g_mix, w_in, A_re, A_im, log_dt, B_re, B_im, C_re, C_im, D_skip, w_glu, b_glu, w_pool, pool_scale, sgu_ln_g, sgu_ln_b, w_spatial, b_spatial, w_out, g_ffn, w_gate, w_up, w_down, g_final, loss_target, m_g_mix, m_w_in, m_A_re, m_A_im, m_log_dt, m_B_re, m_B_im, m_C_re, m_C_im, m_D_skip, m_w_glu, m_b_glu, m_w_pool, m_pool_scale, m_sgu_ln_g, m_sgu_ln_b, m_w_spatial, m_b_spatial, m_w_out, m_g_ffn, m_w_gate, m_w_up, m_w_down, m_g_final, v_g_mix, v_w_in, v_A_re, v_A_im, v_log_dt, v_B_re, v_B_im, v_C_re, v_C_im, v_D_skip, v_w_glu, v_b_glu, v_w_pool, v_pool_scale, v_sgu_ln_g, v_sgu_ln_b, v_w_spatial, v_b_spatial, v_w_out, v_g_ffn, v_w_gate, v_w_up, v_w_down, v_g_final):
    args = dict(locals())
    W = {n: args[n] for n in WEIGHT_ORDER}
    M = {n: args["m_" + n] for n in WEIGHT_ORDER}
    V = {n: args["v_" + n] for n in WEIGHT_ORDER}
    n_layers = g_mix.shape[0]
    x0 = x[0]
    target = loss_target[0]

    def my_rows(name, l):
        w = W[name][l]
        return (w.T if name in COLUMN_SHARDED else w).astype(BF16)

    gathered = gather_weights([my_rows(n, l) for l in range(n_layers) for n in BIG_NAMES])
    full_w = [{n: gathered[l * len(BIG_NAMES) + k].reshape(-1, gathered[l * len(BIG_NAMES) + k].shape[-1])
               for k, n in enumerate(BIG_NAMES)} for l in range(n_layers)]

    tri = jnp.tril(jnp.ones((CHUNK, CHUNK), bool))
    consts = []
    for l in range(n_layers):
        a_re, a_im = _state_rows(A_re[l]), _state_rows(A_im[l])
        ldt = jnp.repeat(log_dt[l], N_STATE).reshape(1, D_ST)
        b_re_t, b_im_t = _chan_by_state(B_re[l]), _chan_by_state(B_im[l])
        sc_f, sc_b, bbd, cbd = s5_prepare(a_re, a_im, ldt, b_re_t, b_im_t, _chan_by_state_c(C_re[l]), _chan_by_state_c(C_im[l]))
        wsm = jnp.where(tri[None], w_spatial[l], 0.0)
        wpool_bd = jnp.zeros((D_POOL, D_POOL), F32)
        for gi in range(len(POOL_WINDOWS)):
            wpool_bd = wpool_bd.at[gi * 64:(gi + 1) * 64, gi * 64:(gi + 1) * 64].set(w_pool[l, gi])
        consts.append(dict(
            disc=(a_re, a_im, ldt, b_re_t, b_im_t), sc_f=sc_f, sc_b=sc_b, bbd=bbd, cbd=cbd,
            dskip=D_skip[l].reshape(1, D_SSM), wglu=full_w[l]["w_glu"], bglu=b_glu[l].reshape(1, D_SSM),
            wpool=wpool_bd.astype(BF16), pscale=pool_scale[l].reshape(1, D_POOL),
            lng=sgu_ln_g[l].reshape(1, D_SGU), lnb=sgu_ln_b[l].reshape(1, D_SGU),
            wsm=wsm.astype(BF16), wsmt=jnp.transpose(wsm, (0, 2, 1)).astype(BF16),
            bsp=jnp.repeat(b_spatial[l].T, HEAD_DIM, axis=1),
            gmix=g_mix[l].reshape(1, D), gffn=g_ffn[l].reshape(1, D)))

    def mixer_args(c):
        return (c["bbd"], c["cbd"], c["dskip"], c["wglu"], c["bglu"], c["wpool"], c["pscale"], c["lng"], c["lnb"])

    saved = []
    xl = x0
    for l in range(n_layers):
        c, fw = consts[l], full_w[l]
        h, z = inproj_fwd(xl, c["gmix"], fw["w_in"])
        ycat, hs, ys = mixer_fwd(z, c["sc_f"], *mixer_args(c), c["wsm"], c["bsp"])
        x1, h2, gate, up, act, x2 = ffn_fwd(xl, ycat, fw["w_out"], c["gffn"], fw["w_gate"], fw["w_up"], fw["w_down"])
        saved.append(dict(x=xl, h=h, z=z, ycat=ycat, hs=hs, ys=ys, x1=x1, h2=h2, gate=gate, up=up, act=act))
        xl = x2
    dx, loss_vec, d_gfinal = loss_head(xl, target, g_final.reshape(1, D))

    big_grads = [None] * n_layers
    small_grads = [None] * n_layers
    for l in reversed(range(n_layers)):
        c, fw, sv = consts[l], full_w[l], saved[l]
        dgu, dx2b, dx1, dx1b, d_gffn = ffn_bwd(dx, sv["x1"], sv["gate"], sv["up"], c["gffn"], fw["w_gate"], fw["w_up"], fw["w_down"])
        (dzb, da, dbt, dct, dd, dbglu, dwglu, dwpool, dpscale, dlng, dlnb, dws, dbsp) = mixer_bwd(
            dx1b, sv["z"], sv["hs"], sv["ys"], fw["w_out"], c["sc_b"], *mixer_args(c), c["wsm"], c["wsmt"], c["bsp"])
        dx, d_gmix = inproj_bwd(dzb, sv["x"], c["gmix"], fw["w_in"], dx1)
        g_gu = wgrad(dgu, sv["h2"], D_FF // 2, f"wgrad_gate_up_{l}")
        big_grads[l] = dict(
            w_in=wgrad(dzb, sv["h"], D_IN, f"wgrad_in_{l}"),
            w_glu=dwglu.astype(BF16),
            w_out=wgrad(sv["ycat"], dx1b, D, f"wgrad_out_{l}"),
            w_gate=g_gu[:D_FF], w_up=g_gu[D_FF:],
            w_down=wgrad(sv["act"], dx2b, D_FF // 2, f"wgrad_down_{l}"))
        d_are, d_aim, d_ldt, d_bre_t, d_bim_t = s5_param_bwd(*c["disc"], da, dbt)
        small_grads[l] = dict(
            g_mix=d_gmix.reshape(D), A_re=d_are.reshape(N_GRP, N_STATE), A_im=d_aim.reshape(N_GRP, N_STATE),
            log_dt=d_ldt[0, :N_GRP],
            B_re=jnp.transpose(d_bre_t.reshape(GRP, N_GRP, N_STATE), (1, 2, 0)),
            B_im=jnp.transpose(d_bim_t.reshape(GRP, N_GRP, N_STATE), (1, 2, 0)),
            C_re=jnp.transpose(dct[:, :D_ST].reshape(GRP, N_GRP, N_STATE), (1, 0, 2)),
            C_im=-jnp.transpose(dct[:, D_ST:].reshape(GRP, N_GRP, N_STATE), (1, 0, 2)),
            D_skip=dd.reshape(D_SSM), b_glu=dbglu.reshape(D_SSM),
            w_pool=jnp.stack([dwpool[gi * 64:(gi + 1) * 64, gi * 64:(gi + 1) * 64] for gi in range(len(POOL_WINDOWS))]),
            pool_scale=dpscale.reshape(D_POOL), sgu_ln_g=dlng.reshape(D_SGU), sgu_ln_b=dlnb.reshape(D_SGU),
            w_spatial=dws, b_spatial=dbsp[:, :HEADS].T, g_ffn=d_gffn.reshape(D))
    grad_x = dx

    small_list = []
    for n in SMALL_NAMES:
        small_list.append(d_gfinal.reshape(D) if n == "g_final" else jnp.stack([small_grads[l][n] for l in range(n_layers)]))
    small_packed = _pack(small_list + [loss_vec.reshape(D)])
    recv = exchange_grads([big_grads[l][n] for l in range(n_layers) for n in BIG_NAMES], small_packed)
    recv_big, recv_small = recv[:-1], recv[-1]

    out = {}
    for k, n in enumerate(BIG_NAMES):
        per_layer = []
        for l in range(n_layers):
            tr = (lambda a: a.T) if n in COLUMN_SHARDED else (lambda a: a)
            res = adamw_sum(recv_big[l * len(BIG_NAMES) + k], tr(W[n][l]), tr(M[n][l]), tr(V[n][l]), f"adamw_{n}_{l}")
            per_layer.append([tr(r) for r in res])
        out[n] = [jnp.stack([per_layer[l][j] for l in range(n_layers)]) for j in range(4)]

    zero = jnp.zeros((D,), F32)
    res = adamw_sum(recv_small, _pack([W[n] for n in SMALL_NAMES] + [zero]), _pack([M[n] for n in SMALL_NAMES] + [zero]),
                    _pack([V[n] for n in SMALL_NAMES] + [zero]), "adamw_small")
    shapes = [W[n].shape for n in SMALL_NAMES] + [(D,)]
    unpacked = [_unpack(r, shapes) for r in res]
    for k, n in enumerate(SMALL_NAMES):
        out[n] = [unpacked[j][k] for j in range(4)]
    loss = (0.5 / D) * jnp.sum(unpacked[0][-1])

    return (loss, grad_x[None], *[out[n][0] for n in WEIGHT_ORDER], *[out[n][1] for n in WEIGHT_ORDER],
            *[out[n][2] for n in WEIGHT_ORDER], *[out[n][3] for n in WEIGHT_ORDER])
```

```python
import functools
import math

import jax
import jax.numpy as jnp
from jax import lax
from jax.experimental import pallas as pl
from jax.experimental.pallas import tpu as pltpu

F32 = jnp.float32
BF16 = jnp.bfloat16
S = jax.ShapeDtypeStruct

N_DEV = 8
D = 1024
D_SSM = 384
N_GRP = 24
GRP = 16
N_STATE = 64
D_ST = N_GRP * N_STATE
D_POOL = 256
POOL_WINDOWS = (2, 4, 8, 16)
HALO = 16
D_SGU = 384
HEADS = 6
HEAD_DIM = 64
CHUNK = 128
D_IN = 1408
D_FF = 2816
EPS = 1e-6
SCAN_BLK = 8

ADAM_LR = 0.001
ADAM_B1 = 0.9
ADAM_B2 = 0.999
ADAM_EPS = 1e-08
ADAM_WD = 0.01
ADAM_STEP = 10

GELU_C0 = math.sqrt(2.0 / math.pi)
GELU_C1 = 0.044715

TT_MIX = 256
TT_FFN = 256
TT_PROJ = 512
TK_WGRAD = 512
VMEM_MB = 2 ** 20


def _cp(vmem_mb, grid_dims=0):
    kw = dict(vmem_limit_bytes=int(vmem_mb * VMEM_MB))
    if grid_dims:
        kw["dimension_semantics"] = ("arbitrary",) * grid_dims
    return pltpu.CompilerParams(**kw)


def _row(tt, n):
    return pl.BlockSpec((tt, n), lambda i: (i, 0))


def _full(shape):
    nd = len(shape)
    return pl.BlockSpec(shape, lambda *_: (0,) * nd)


def _nn(a, b):
    return jnp.dot(a, b, preferred_element_type=F32)


def _nt(a, b):
    return lax.dot_general(a, b, (((1,), (1,)), ((), ())), preferred_element_type=F32)


def _tn(a, b):
    return lax.dot_general(a, b, (((0,), (0,)), ((), ())), preferred_element_type=F32)


def _rowsum(x):
    return jnp.sum(x, axis=0, keepdims=True)


def _rms(x):
    r = lax.rsqrt(jnp.mean(x * x, axis=-1, keepdims=True) + EPS)
    return x * r, r


def _rms_bwd(dy, xn, r, g):
    dyg = dy * g
    return r * (dyg - xn * jnp.mean(dyg * xn, axis=-1, keepdims=True))


def _gelu(x):
    t = jnp.tanh(GELU_C0 * (x + GELU_C1 * x * x * x))
    return 0.5 * x * (1.0 + t), t


def _gelu_grad(x, t):
    return 0.5 * (1.0 + t) + 0.5 * x * (1.0 - t * t) * (GELU_C0 * (1.0 + 3.0 * GELU_C1 * x * x))


def _discretise(a_re, a_im, ldt, b_re, b_im):
    dt = jnp.exp(ldt)
    mag = jnp.exp(a_re * dt)
    ar = mag * jnp.cos(a_im * dt)
    ai = mag * jnp.sin(a_im * dt)
    den = a_re * a_re + a_im * a_im
    f_re = ((ar - 1.0) * a_re + ai * a_im) / den
    f_im = (ai * a_re - (ar - 1.0) * a_im) / den
    bb_re = f_re * b_re - f_im * b_im
    bb_im = f_re * b_im + f_im * b_re
    return ar, ai, bb_re, bb_im


def _group_mask(rows, cols):
    r = lax.broadcasted_iota(jnp.int32, (rows, cols), 0) // GRP
    c = lax.broadcasted_iota(jnp.int32, (rows, cols), 1)
    c = jnp.where(c >= D_ST, c - D_ST, c) // N_STATE
    return r == c


def s5_prepare(a_re, a_im, ldt, b_re_t, b_im_t, c_re_t, c_im_t):
    def body(are_ref, aim_ref, ldt_ref, bre_ref, bim_ref, cre_ref, cim_ref, scf_ref, scb_ref, bbd_ref, cbd_ref):
        ar, ai, bb_re, bb_im = _discretise(are_ref[...], aim_ref[...], ldt_ref[...], bre_ref[...], bim_ref[...])
        mask = _group_mask(D_SSM, 2 * D_ST)
        bb = jnp.concatenate([jnp.tile(bb_re, (N_GRP, 1)), jnp.tile(bb_im, (N_GRP, 1))], axis=1)
        bbd_ref[...] = jnp.where(mask, bb, 0.0).astype(BF16)
        cc = jnp.concatenate([jnp.tile(cre_ref[...], (N_GRP, 1)), -jnp.tile(cim_ref[...], (N_GRP, 1))], axis=1)
        cbd_ref[...] = jnp.where(mask, cc, 0.0).astype(BF16)
        pw = [(ar, ai)]
        for _ in range(SCAN_BLK - 1):
            pr, pi = pw[-1]
            pw.append((pr * ar - pi * ai, pr * ai + pi * ar))
        row = lax.broadcasted_iota(jnp.int32, (SCAN_BLK, D_ST), 0)

        def stack(vals):
            out = jnp.zeros((SCAN_BLK, D_ST), F32)
            for j, v in enumerate(vals):
                out = jnp.where(row == j, v, out)
            return out

        scf_ref[0:8, :] = stack([pw[j][0] for j in range(SCAN_BLK)])
        scf_ref[8:16, :] = stack([pw[j][1] for j in range(SCAN_BLK)])
        scb_ref[0:8, :] = stack([pw[SCAN_BLK - 1 - j][0] for j in range(SCAN_BLK)])
        scb_ref[8:16, :] = stack([-pw[SCAN_BLK - 1 - j][1] for j in range(SCAN_BLK)])
        for k, d in enumerate((1, 2, 4)):
            pr, pi = pw[d - 1]
            scf_ref[16 + 16 * k:24 + 16 * k, :] = jnp.where(row >= d, pr, 0.0)
            scf_ref[24 + 16 * k:32 + 16 * k, :] = jnp.where(row >= d, pi, 0.0)
            scb_ref[16 + 16 * k:24 + 16 * k, :] = jnp.where(row <= SCAN_BLK - 1 - d, pr, 0.0)
            scb_ref[24 + 16 * k:32 + 16 * k, :] = jnp.where(row <= SCAN_BLK - 1 - d, -pi, 0.0)

    return pl.pallas_call(
        body, name="s5_prepare",
        out_shape=[S((64, D_ST), F32), S((64, D_ST), F32), S((D_SSM, 2 * D_ST), BF16), S((D_SSM, 2 * D_ST), BF16)],
        compiler_params=_cp(40),
    )(a_re, a_im, ldt, b_re_t, b_im_t, c_re_t, c_im_t)


def s5_param_bwd(a_re, a_im, ldt, b_re_t, b_im_t, da, dbt):
    def body(are_ref, aim_ref, ldt_ref, bre_ref, bim_ref, da_ref, dbt_ref, o_are, o_aim, o_ldt, o_bre, o_bim):
        _, vjp = jax.vjp(_discretise, are_ref[...], aim_ref[...], ldt_ref[...], bre_ref[...], bim_ref[...])
        da = da_ref[...]
        dbt = dbt_ref[...]
        g_are, g_aim, g_ldt, g_bre, g_bim = vjp((da[:, :D_ST], da[:, D_ST:], dbt[:, :D_ST], dbt[:, D_ST:]))
        o_are[...] = g_are
        o_aim[...] = g_aim
        o_bre[...] = g_bre
        o_bim[...] = g_bim
        grp = lax.broadcasted_iota(jnp.int32, (1, D_ST), 1) // N_STATE
        lane = lax.broadcasted_iota(jnp.int32, (1, 128), 1)
        out = jnp.zeros((1, 128), F32)
        for g in range(N_GRP):
            out = jnp.where(lane == g, jnp.sum(jnp.where(grp == g, g_ldt, 0.0), axis=1, keepdims=True), out)
        o_ldt[...] = out

    return pl.pallas_call(
        body, name="s5_param_bwd",
        out_shape=[S((1, D_ST), F32), S((1, D_ST), F32), S((1, 128), F32), S((GRP, D_ST), F32), S((GRP, D_ST), F32)],
        compiler_params=_cp(16),
    )(a_re, a_im, ldt, b_re_t, b_im_t, da, dbt)


def _scan_blocks(buf_ref, sc_ref, carry_ref, n_blk, reverse):
    def step(i, carry):
        cr, ci = carry
        blk = (n_blk - 1 - i) if reverse else i
        r0 = pl.multiple_of(blk * SCAN_BLK, SCAN_BLK)
        xr = buf_ref[pl.ds(r0, SCAN_BLK), 0:D_ST]
        xi = buf_ref[pl.ds(r0, SCAN_BLK), D_ST:2 * D_ST]
        for k, d in enumerate((1, 2, 4)):
            ar = sc_ref[16 + 16 * k:24 + 16 * k, :]
            ai = sc_ref[24 + 16 * k:32 + 16 * k, :]
            sh = (SCAN_BLK - d) if reverse else d
            rr = pltpu.roll(xr, sh, axis=0)
            ri = pltpu.roll(xi, sh, axis=0)
            xr, xi = xr + ar * rr - ai * ri, xi + ar * ri + ai * rr
        pr = sc_ref[0:8, :]
        pi = sc_ref[8:16, :]
        xr, xi = xr + pr * cr - pi * ci, xi + pr * ci + pi * cr
        buf_ref[pl.ds(r0, SCAN_BLK), 0:D_ST] = xr
        buf_ref[pl.ds(r0, SCAN_BLK), D_ST:2 * D_ST] = xi
        edge = 0 if reverse else SCAN_BLK - 1
        return xr[edge:edge + 1, :], xi[edge:edge + 1, :]

    cr, ci = lax.fori_loop(0, n_blk, step, (carry_ref[0:1, 0:D_ST], carry_ref[0:1, D_ST:2 * D_ST]))
    carry_ref[0:1, 0:D_ST] = cr
    carry_ref[0:1, D_ST:2 * D_ST] = ci


def _lane_windows(n):
    lane = lax.broadcasted_iota(jnp.int32, (1, n), 1)
    return lane // (D_POOL // len(POOL_WINDOWS))


def _select_window(grp, s2, s4, s8, s16):
    return jnp.where(grp == 0, s2, jnp.where(grp == 1, s4, jnp.where(grp == 2, s8, s16)))


def _pool_fwd(pbuf_ref, zb, halo, tile_idx, tt):
    pbuf_ref[0:HALO, :] = halo
    pbuf_ref[HALO:HALO + tt, :] = zb
    x = pbuf_ref[...]
    s2 = x + pltpu.roll(x, 1, axis=0)
    s4 = s2 + pltpu.roll(s2, 2, axis=0)
    s8 = s4 + pltpu.roll(s4, 4, axis=0)
    s16 = s8 + pltpu.roll(s8, 8, axis=0)
    grp = _lane_windows(D_POOL)
    win = _select_window(grp, s2, s4, s8, s16)[HALO:HALO + tt, :]
    width = _select_window(grp, 2.0, 4.0, 8.0, 16.0).astype(F32)
    pos = (tile_idx * tt + 1 + lax.broadcasted_iota(jnp.int32, (tt, 1), 0)).astype(F32)
    cnt = jnp.minimum(pos, width)
    return win / cnt - zb, cnt


def _sgu_fwd(zu, zv, lng, lnb, wsm_ref, bsp, mix_ref, tt):
    u, tu = _gelu(zu)
    v, tv = _gelu(zv)
    mu = jnp.mean(v, axis=-1, keepdims=True)
    vc = v - mu
    rstd = lax.rsqrt(jnp.mean(vc * vc, axis=-1, keepdims=True) + EPS)
    vhat = vc * rstd
    vnb = (vhat * lng + lnb).astype(BF16)
    for c in range(tt // CHUNK):
        rows = slice(c * CHUNK, (c + 1) * CHUNK)
        parts = [_nn(wsm_ref[h], vnb[rows, h * HEAD_DIM:(h + 1) * HEAD_DIM]) for h in range(HEADS)]
        mix_ref[rows, :] = jnp.concatenate(parts, axis=1) + bsp
    return u, tu, tv, vhat, rstd, vnb


def mixer_fwd(z, sc_f, bbd, cbd, dskip, wglu, bglu, wpool, pscale, lng, lnb, wsm, bsp, name, exchange=None):
    T = z.shape[0]
    tt = TT_MIX
    n_tiles = T // tt

    def body(z_ref, scf_ref, bbd_ref, cbd_ref, dskip_ref, wglu_ref, bglu_ref, wpool_ref, pscale_ref, lng_ref, lnb_ref,
             wsm_ref, bsp_ref, ycat_ref, hs_ref, ys_ref, carry_ref, halo_ref, pbuf_ref, mix_ref):
        i = pl.program_id(0)

        @pl.when(i == 0)
        def _():
            carry_ref[...] = jnp.zeros_like(carry_ref)
            halo_ref[...] = jnp.zeros_like(halo_ref)

        za = z_ref[:, 0:D_SSM]
        zb = z_ref[:, D_SSM:D_SSM + D_POOL]
        zu = z_ref[:, D_SSM + D_POOL:D_SSM + D_POOL + D_SGU]
        zv = z_ref[:, D_SSM + D_POOL + D_SGU:D_IN]
        hs_ref[...] = _nn(za.astype(BF16), bbd_ref[...])
        _scan_blocks(hs_ref, scf_ref, carry_ref, tt // SCAN_BLK, reverse=False)
        y = _nt(hs_ref[...].astype(BF16), cbd_ref[...]) + dskip_ref[...] * za
        ys_ref[...] = y
        g, _ = _gelu(y)
        q = _nn(g.astype(BF16), wglu_ref[...]) + bglu_ref[...]
        ycat_ref[:, 0:D_SSM] = (g * jax.nn.sigmoid(q)).astype(BF16)
        pooled, _ = _pool_fwd(pbuf_ref, zb, halo_ref[...], i, tt)
        halo_ref[...] = zb[tt - HALO:tt, :]
        ycat_ref[:, D_SSM:D_SSM + D_POOL] = (_nn(pooled.astype(BF16), wpool_ref[...]) * pscale_ref[...]).astype(BF16)
        u, _, _, _, _, _ = _sgu_fwd(zu, zv, lng_ref[...], lnb_ref[...], wsm_ref, bsp_ref[...], mix_ref, tt)
        ycat_ref[:, D_SSM + D_POOL:D] = (u * mix_ref[...]).astype(BF16)

    return _pallas(
        body, name=name, grid=(n_tiles,),
        in_specs=[_row(tt, D_IN), _full((64, D_ST)), _full((D_SSM, 2 * D_ST)), _full((D_SSM, 2 * D_ST)),
                  _full((1, D_SSM)), _full((D_SSM, D_SSM)), _full((1, D_SSM)), _full((D_POOL, D_POOL)),
                  _full((1, D_POOL)), _full((1, D_SGU)), _full((1, D_SGU)), _full((HEADS, CHUNK, CHUNK)),
                  _full((CHUNK, D_SGU))],
        out_specs=[_row(tt, D), _row(tt, 2 * D_ST), _row(tt, D_SSM)],
        out_shape=[S((T, D), BF16), S((T, 2 * D_ST), F32), S((T, D_SSM), F32)],
        scratch_shapes=[pltpu.VMEM((SCAN_BLK, 2 * D_ST), F32), pltpu.VMEM((HALO, D_POOL), F32),
                        pltpu.VMEM((tt + HALO, D_POOL), F32), pltpu.VMEM((tt, D_SGU), F32)],
        vmem_mb=48, operands=(z, sc_f, bbd, cbd, dskip, wglu, bglu, wpool, pscale, lng, lnb, wsm, bsp),
        exchange=exchange)


def mixer_bwd(dx1b, z, hs, ys, wout, sc_b, bbd, cbd, dskip, wglu, bglu, wpool, pscale, lng, lnb, wsm, wsmt, bsp, name,
              exchange=None):
    T = z.shape[0]
    tt = TT_MIX
    n_tiles = T // tt

    def rev(i):
        return n_tiles - 1 - i

    def body(dx_ref, z_ref, zprev_ref, hs_ref, hsprev_ref, ys_ref, wout_ref, scb_ref, bbd_ref, cbd_ref, dskip_ref,
             wglu_ref, bglu_ref, wpool_ref, pscale_ref, lng_ref, lnb_ref, wsm_ref, wsmt_ref, bsp_ref,
             dz_ref, o_da, o_dbt, o_dct, o_dd, o_dbglu, o_dwglu, o_dwpool, o_dpscale, o_dlng, o_dlnb, o_dws, o_dbsp,
             gbuf_ref, carry_ref, ehalo_ref, pbuf_ref, mix_ref, dvn_ref, accb_ref, accc_ref, accw_ref, accm_ref):
        i = pl.program_id(0)
        tile = rev(i)

        @pl.when(i == 0)
        def _():
            carry_ref[...] = jnp.zeros_like(carry_ref)
            ehalo_ref[...] = jnp.zeros_like(ehalo_ref)
            accb_ref[...] = jnp.zeros_like(accb_ref)
            accc_ref[...] = jnp.zeros_like(accc_ref)
            accw_ref[...] = jnp.zeros_like(accw_ref)
            accm_ref[...] = jnp.zeros_like(accm_ref)
            for o in (o_da, o_dd, o_dbglu, o_dwglu, o_dwpool, o_dpscale, o_dlng, o_dlnb):
                o[...] = jnp.zeros_like(o)

        dycat = _nt(dx_ref[...], wout_ref[...])
        d_a = dycat[:, 0:D_SSM]
        d_b = dycat[:, D_SSM:D_SSM + D_POOL]
        d_c = dycat[:, D_SSM + D_POOL:D]
        za = z_ref[:, 0:D_SSM]
        zb = z_ref[:, D_SSM:D_SSM + D_POOL]
        zu = z_ref[:, D_SSM + D_POOL:D_SSM + D_POOL + D_SGU]
        zv = z_ref[:, D_SSM + D_POOL + D_SGU:D_IN]
        first_tile = (tile > 0).astype(F32)

        y = ys_ref[...]
        g, tg = _gelu(y)
        gb = g.astype(BF16)
        sg = jax.nn.sigmoid(_nn(gb, wglu_ref[...]) + bglu_ref[...])
        dq = d_a * g * sg * (1.0 - sg)
        dqb = dq.astype(BF16)
        o_dbglu[...] += _rowsum(dq)
        o_dwglu[...] += _tn(gb, dqb)
        dy = (d_a * sg + _nt(dqb, wglu_ref[...])) * _gelu_grad(y, tg)
        o_dd[...] += _rowsum(dy * za)
        dyb = dy.astype(BF16)
        hs_t = hs_ref[...]
        accc_ref[...] += _tn(dyb, hs_t.astype(BF16))
        gbuf_ref[...] = _nn(dyb, cbd_ref[...])
        _scan_blocks(gbuf_ref, scb_ref, carry_ref, tt // SCAN_BLK, reverse=True)
        gt = gbuf_ref[...]
        row = lax.broadcasted_iota(jnp.int32, (tt, 1), 0)
        hprev = hsprev_ref[SCAN_BLK - 1:SCAN_BLK, :] * first_tile
        hsh = jnp.where(row == 0, hprev, pltpu.roll(hs_t, 1, axis=0))
        gr, gi = gt[:, 0:D_ST], gt[:, D_ST:]
        hr, hi = hsh[:, 0:D_ST], hsh[:, D_ST:]
        o_da[:, 0:D_ST] += _rowsum(gr * hr + gi * hi)
        o_da[:, D_ST:] += _rowsum(gi * hr - gr * hi)
        gtb = gt.astype(BF16)
        accb_ref[...] += _tn(za.astype(BF16), gtb)
        dz_ref[:, 0:D_SSM] = (dy * dskip_ref[...] + _nt(gtb, bbd_ref[...])).astype(BF16)

        pooled, cnt = _pool_fwd(pbuf_ref, zb, zprev_ref[:, D_SSM:D_SSM + D_POOL] * first_tile, tile, tt)
        pooledb = pooled.astype(BF16)
        mixed = _nn(pooledb, wpool_ref[...])
        o_dpscale[...] += _rowsum(d_b * mixed)
        dmixb = (d_b * pscale_ref[...]).astype(BF16)
        o_dwpool[...] += _tn(pooledb, dmixb)
        dpooled = _nt(dmixb, wpool_ref[...])
        e = dpooled / cnt
        pbuf_ref[0:tt, :] = e
        pbuf_ref[tt:tt + HALO, :] = ehalo_ref[...]
        ehalo_ref[...] = e[0:HALO, :]
        x = pbuf_ref[...]
        n = tt + HALO
        f2 = x + pltpu.roll(x, n - 1, axis=0)
        f4 = f2 + pltpu.roll(f2, n - 2, axis=0)
        f8 = f4 + pltpu.roll(f4, n - 4, axis=0)
        f16 = f8 + pltpu.roll(f8, n - 8, axis=0)
        fwd_sum = _select_window(_lane_windows(D_POOL), f2, f4, f8, f16)[0:tt, :]
        dz_ref[:, D_SSM:D_SSM + D_POOL] = (fwd_sum - dpooled).astype(BF16)

        lng = lng_ref[...]
        u, tu, tv, vhat, rstd, vnb = _sgu_fwd(zu, zv, lng, lnb_ref[...], wsm_ref, bsp_ref[...], mix_ref, tt)
        dz_ref[:, D_SSM + D_POOL:D_SSM + D_POOL + D_SGU] = (d_c * mix_ref[...] * _gelu_grad(zu, tu)).astype(BF16)
        dmix = d_c * u
        dmixb2 = dmix.astype(BF16)
        for c in range(tt // CHUNK):
            rows = slice(c * CHUNK, (c + 1) * CHUNK)
            accm_ref[...] += dmix[rows, :]
            parts = []
            for h in range(HEADS):
                cols = slice(h * HEAD_DIM, (h + 1) * HEAD_DIM)
                accw_ref[h] += _nt(dmixb2[rows, cols], vnb[rows, cols])
                parts.append(_nn(wsmt_ref[h], dmixb2[rows, cols]))
            dvn_ref[rows, :] = jnp.concatenate(parts, axis=1)
        dvn = dvn_ref[...]
        o_dlng[...] += _rowsum(dvn * vhat)
        o_dlnb[...] += _rowsum(dvn)
        dvh = dvn * lng
        dv = rstd * (dvh - jnp.mean(dvh, axis=-1, keepdims=True) - vhat * jnp.mean(dvh * vhat, axis=-1, keepdims=True))
        dz_ref[:, D_SSM + D_POOL + D_SGU:D_IN] = (dv * _gelu_grad(zv, tv)).astype(BF16)

        @pl.when(i == n_tiles - 1)
        def _():
            mask = _group_mask(D_SSM, 2 * D_ST)
            fb = jnp.zeros((GRP, 2 * D_ST), F32)
            fc = jnp.zeros((GRP, 2 * D_ST), F32)
            for gidx in range(N_GRP):
                rows = slice(gidx * GRP, (gidx + 1) * GRP)
                fb = fb + jnp.where(mask[rows, :], accb_ref[rows, :], 0.0)
                fc = fc + jnp.where(mask[rows, :], accc_ref[rows, :], 0.0)
            o_dbt[...] = fb
            o_dct[...] = fc
            tri = (lax.broadcasted_iota(jnp.int32, (CHUNK, CHUNK), 0) >= lax.broadcasted_iota(jnp.int32, (CHUNK, CHUNK), 1))
            for h in range(HEADS):
                o_dws[h] = jnp.where(tri, accw_ref[h], 0.0)
            lane = lax.broadcasted_iota(jnp.int32, (1, 128), 1)
            acc = jnp.zeros((CHUNK, 128), F32)
            for h in range(HEADS):
                sh = jnp.sum(accm_ref[:, h * HEAD_DIM:(h + 1) * HEAD_DIM], axis=1, keepdims=True)
                acc = jnp.where(lane == h, sh, acc)
            o_dbsp[...] = acc

    def rowr(n):
        return pl.BlockSpec((tt, n), lambda i: (rev(i), 0))

    zprev_spec = pl.BlockSpec((HALO, D_IN), lambda i: (jnp.maximum(rev(i) * (tt // HALO) - 1, 0), 0))
    hsprev_spec = pl.BlockSpec((SCAN_BLK, 2 * D_ST), lambda i: (jnp.maximum(rev(i) * (tt // SCAN_BLK) - 1, 0), 0))
    small = [S((1, 2 * D_ST), F32), S((GRP, 2 * D_ST), F32), S((GRP, 2 * D_ST), F32), S((1, D_SSM), F32),
             S((1, D_SSM), F32), S((D_SSM, D_SSM), F32), S((D_POOL, D_POOL), F32), S((1, D_POOL), F32),
             S((1, D_SGU), F32), S((1, D_SGU), F32), S((HEADS, CHUNK, CHUNK), F32), S((CHUNK, 128), F32)]
    return _pallas(
        body, name=name, grid=(n_tiles,),
        in_specs=[rowr(D), rowr(D_IN), zprev_spec, rowr(2 * D_ST), hsprev_spec, rowr(D_SSM), _full((D, D)),
                  _full((64, D_ST)), _full((D_SSM, 2 * D_ST)), _full((D_SSM, 2 * D_ST)), _full((1, D_SSM)),
                  _full((D_SSM, D_SSM)), _full((1, D_SSM)), _full((D_POOL, D_POOL)), _full((1, D_POOL)),
                  _full((1, D_SGU)), _full((1, D_SGU)), _full((HEADS, CHUNK, CHUNK)), _full((HEADS, CHUNK, CHUNK)),
                  _full((CHUNK, D_SGU))],
        out_specs=[rowr(D_IN)] + [_full(s.shape) for s in small],
        out_shape=[S((T, D_IN), BF16)] + small,
        scratch_shapes=[pltpu.VMEM((tt, 2 * D_ST), F32), pltpu.VMEM((SCAN_BLK, 2 * D_ST), F32),
                        pltpu.VMEM((HALO, D_POOL), F32), pltpu.VMEM((tt + HALO, D_POOL), F32),
                        pltpu.VMEM((tt, D_SGU), F32), pltpu.VMEM((tt, D_SGU), F32),
                        pltpu.VMEM((D_SSM, 2 * D_ST), F32), pltpu.VMEM((D_SSM, 2 * D_ST), F32),
                        pltpu.VMEM((HEADS, CHUNK, CHUNK), F32), pltpu.VMEM((CHUNK, D_SGU), F32)],
        vmem_mb=56, exchange=exchange,
        operands=(dx1b, z, z, hs, hs, ys, wout, sc_b, bbd, cbd, dskip, wglu, bglu, wpool, pscale, lng, lnb, wsm, wsmt, bsp))


def inproj_fwd(x, g, w_t, name, exchange=None):
    T = x.shape[0]
    tt = TT_PROJ

    def body(x_ref, g_ref, w_ref, h_ref, z_ref):
        xn, _ = _rms(x_ref[...])
        h = (xn * g_ref[...]).astype(BF16)
        h_ref[...] = h
        z_ref[...] = _nt(h, w_ref[...])

    return _pallas(
        body, name=name, grid=(T // tt,),
        in_specs=[_row(tt, D), _full((1, D)), _full((D_IN, D))],
        out_specs=[_row(tt, D), _row(tt, D_IN)],
        out_shape=[S((T, D), BF16), S((T, D_IN), F32)],
        scratch_shapes=[], vmem_mb=40, operands=(x, g, w_t), exchange=exchange)


def inproj_bwd(dzb, x, g, w_t, dx1):
    T = x.shape[0]
    tt = TT_PROJ

    def body(dz_ref, x_ref, g_ref, w_ref, dx1_ref, dx_ref, dg_ref):
        @pl.when(pl.program_id(0) == 0)
        def _():
            dg_ref[...] = jnp.zeros_like(dg_ref)

        dh = _nn(dz_ref[...], w_ref[...])
        xn, r = _rms(x_ref[...])
        dg_ref[...] += _rowsum(dh * xn)
        dx_ref[...] = dx1_ref[...] + _rms_bwd(dh, xn, r, g_ref[...])

    return pl.pallas_call(
        body, name="inproj_bwd", grid=(T // tt,),
        in_specs=[_row(tt, D_IN), _row(tt, D), _full((1, D)), _full((D_IN, D)), _row(tt, D)],
        out_specs=[_row(tt, D), _full((1, D))],
        out_shape=[S((T, D), F32), S((1, D), F32)],
        compiler_params=_cp(40, 1),
    )(dzb, x, g, w_t, dx1)


def _load_weights(pairs, sem):
    @pl.when(pl.program_id(0) == 0)
    def _():
        copies = [pltpu.make_async_copy(src, dst, sem.at[k]) for k, (src, dst) in enumerate(pairs)]
        for cp in copies:
            cp.start()
        for cp in copies:
            cp.wait()


def ffn_fwd(x, ycat, wout, g, wg_t, wu_t, wd, name, exchange=None):
    T = x.shape[0]
    tt = TT_FFN
    any_spec = pl.BlockSpec(memory_space=pl.ANY)

    def body(x_ref, ycat_ref, g_ref, wout_hbm, wg_hbm, wu_hbm, wd_hbm,
             x1_ref, h_ref, gate_ref, up_ref, act_ref, x2_ref, wout_v, wg_v, wu_v, wd_v, sem):
        _load_weights([(wout_hbm, wout_v), (wg_hbm, wg_v), (wu_hbm, wu_v), (wd_hbm, wd_v)], sem)
        x1 = x_ref[...] + _nn(ycat_ref[...], wout_v[...])
        x1_ref[...] = x1
        xn, _ = _rms(x1)
        h = (xn * g_ref[...]).astype(BF16)
        h_ref[...] = h
        gate = _nt(h, wg_v[...])
        up = _nt(h, wu_v[...])
        gate_ref[...] = gate.astype(BF16)
        up_ref[...] = up.astype(BF16)
        act = (gate * jax.nn.sigmoid(gate) * up).astype(BF16)
        act_ref[...] = act
        x2_ref[...] = x1 + _nn(act, wd_v[...])

    return _pallas(
        body, name=name, grid=(T // tt,),
        in_specs=[_row(tt, D), _row(tt, D), _full((1, D)), any_spec, any_spec, any_spec, any_spec],
        out_specs=[_row(tt, D), _row(tt, D), _row(tt, D_FF), _row(tt, D_FF), _row(tt, D_FF), _row(tt, D)],
        out_shape=[S((T, D), F32), S((T, D), BF16), S((T, D_FF), BF16), S((T, D_FF), BF16), S((T, D_FF), BF16),
                   S((T, D), F32)],
        scratch_shapes=[pltpu.VMEM((D, D), BF16), pltpu.VMEM((D_FF, D), BF16), pltpu.VMEM((D_FF, D), BF16),
                        pltpu.VMEM((D_FF, D), BF16), pltpu.SemaphoreType.DMA((4,))],
        vmem_mb=56, operands=(x, ycat, g, wout, wg_t, wu_t, wd), exchange=exchange)


def ffn_bwd(dx2, x1, gate, up, g, wg_t, wu_t, wd, name, exchange=None):
    T = x1.shape[0]
    tt = TT_FFN
    any_spec = pl.BlockSpec(memory_space=pl.ANY)

    def body(dx2_ref, x1_ref, gate_ref, up_ref, g_ref, wg_hbm, wu_hbm, wd_hbm,
             dgu_ref, dx2b_ref, dx1_ref, dx1b_ref, dg_ref, wg_v, wu_v, wd_v, sem):
        _load_weights([(wg_hbm, wg_v), (wu_hbm, wu_v), (wd_hbm, wd_v)], sem)

        @pl.when(pl.program_id(0) == 0)
        def _():
            dg_ref[...] = jnp.zeros_like(dg_ref)

        dx2 = dx2_ref[...]
        dx2b = dx2.astype(BF16)
        dx2b_ref[...] = dx2b
        dact = _nt(dx2b, wd_v[...])
        gate = gate_ref[...].astype(F32)
        up = up_ref[...].astype(F32)
        sg = jax.nn.sigmoid(gate)
        dgate = (dact * up * (sg * (1.0 + gate * (1.0 - sg)))).astype(BF16)
        dup = (dact * gate * sg).astype(BF16)
        dgu_ref[:, 0:D_FF] = dgate
        dgu_ref[:, D_FF:2 * D_FF] = dup
        dh = _nn(dgate, wg_v[...]) + _nn(dup, wu_v[...])
        xn, r = _rms(x1_ref[...])
        dg_ref[...] += _rowsum(dh * xn)
        dx1 = dx2 + _rms_bwd(dh, xn, r, g_ref[...])
        dx1_ref[...] = dx1
        dx1b_ref[...] = dx1.astype(BF16)

    return _pallas(
        body, name=name, grid=(T // tt,),
        in_specs=[_row(tt, D), _row(tt, D), _row(tt, D_FF), _row(tt, D_FF), _full((1, D)), any_spec, any_spec, any_spec],
        out_specs=[_row(tt, 2 * D_FF), _row(tt, D), _row(tt, D), _row(tt, D), _full((1, D))],
        out_shape=[S((T, 2 * D_FF), BF16), S((T, D), BF16), S((T, D), F32), S((T, D), BF16), S((1, D), F32)],
        scratch_shapes=[pltpu.VMEM((D_FF, D), BF16), pltpu.VMEM((D_FF, D), BF16), pltpu.VMEM((D_FF, D), BF16),
                        pltpu.SemaphoreType.DMA((3,))],
        vmem_mb=56, operands=(dx2, x1, gate, up, g, wg_t, wu_t, wd), exchange=exchange)


def loss_head(x, target, g):
    T = x.shape[0]
    tt = TT_PROJ

    def body(x_ref, t_ref, g_ref, dx_ref, lvec_ref, dg_ref):
        @pl.when(pl.program_id(0) == 0)
        def _():
            lvec_ref[...] = jnp.zeros_like(lvec_ref)
            dg_ref[...] = jnp.zeros_like(dg_ref)

        xn, r = _rms(x_ref[...])
        gg = g_ref[...]
        err = xn * gg - t_ref[...]
        lvec_ref[...] += _rowsum(err * err)
        dy = err * (1.0 / D)
        dg_ref[...] += _rowsum(dy * xn)
        dx_ref[...] = _rms_bwd(dy, xn, r, gg)

    return pl.pallas_call(
        body, name="loss_head", grid=(T // tt,),
        in_specs=[_row(tt, D), _row(tt, D), _full((1, D))],
        out_specs=[_row(tt, D), _full((1, D)), _full((1, D))],
        out_shape=[S((T, D), F32), S((1, D), F32), S((1, D), F32)],
        compiler_params=_cp(32, 1),
    )(x, target, g)


def wgrad(a, b, tm, name):
    T, M = a.shape
    N = b.shape[1]
    tk = min(TK_WGRAD, T)
    n_k = T // tk

    def body(a_ref, b_ref, o_ref, acc_ref):
        k = pl.program_id(1)

        @pl.when(k == 0)
        def _():
            acc_ref[...] = jnp.zeros_like(acc_ref)

        acc_ref[...] += _tn(a_ref[...], b_ref[...])

        @pl.when(k == n_k - 1)
        def _():
            o_ref[...] = acc_ref[...].astype(BF16)

    return pl.pallas_call(
        body, name=name, grid=(M // tm, n_k),
        in_specs=[pl.BlockSpec((tk, tm), lambda m, k: (k, m)), pl.BlockSpec((tk, N), lambda m, k: (k, 0))],
        out_specs=pl.BlockSpec((tm, N), lambda m, k: (m, 0)),
        out_shape=S((M, N), BF16),
        scratch_shapes=[pltpu.VMEM((tm, N), F32)],
        compiler_params=_cp(48, 2),
    )(a, b)


def _mesh_place():
    x, y, c = lax.axis_index("x"), lax.axis_index("y"), lax.axis_index("c")
    return x, y, c, 4 * x + 2 * y + c


def _peer(x, y, c, k):
    px = 1 - x if k & 4 else x
    py = 1 - y if k & 2 else y
    pc = 1 - c if k & 1 else c
    return (px, py, pc), 4 * px + 2 * py + pc


class _Exchange:
    def __init__(self, gather=(), scatter=()):
        self.entries = [(a, None, a.shape[0]) for a in gather] + [(a, off, rows) for a, off, rows in scatter]

    @property
    def n(self):
        return len(self.entries)

    def operands(self):
        return [e[0] for e in self.entries]

    def out_shapes(self):
        return [S((N_DEV, rows, a.shape[1]), a.dtype) for a, _, rows in self.entries]

    def sems(self):
        return [pltpu.SemaphoreType.DMA((self.n, N_DEV)), pltpu.SemaphoreType.DMA((self.n, N_DEV)),
                pltpu.SemaphoreType.DMA((self.n,))]

    def _src(self, ref, e, idx):
        _, off, rows = self.entries[e]
        if off is None:
            return ref
        return ref.at[pl.ds(pl.multiple_of(off + idx * rows, 16), rows)]

    def _copies(self, ins, outs, sems, k, sending):
        send_sems, recv_sems, _ = sems
        x, y, c, me = _mesh_place()
        peer, pidx = _peer(x, y, c, k)
        slot = me if sending else pidx
        return [pltpu.make_async_remote_copy(
            src_ref=self._src(ins[e], e, pidx), dst_ref=outs[e].at[slot], send_sem=send_sems.at[e, k],
            recv_sem=recv_sems.at[e, k], device_id=peer, device_id_type=pl.DeviceIdType.MESH) for e in range(self.n)]

    def _local(self, ins, outs, sems):
        me = _mesh_place()[3]
        return [pltpu.make_async_copy(self._src(ins[e], e, me), outs[e].at[me], sems[2].at[e]) for e in range(self.n)]

    def start(self, ins, outs, sems):
        for cp in self._local(ins, outs, sems):
            cp.start()
        for k in range(1, N_DEV):
            for cp in self._copies(ins, outs, sems, k, True):
                cp.start()

    def wait(self, ins, outs, sems):
        for k in range(1, N_DEV):
            for cp in self._copies(ins, outs, sems, k, False):
                cp.wait_recv()
        for k in range(1, N_DEV):
            for cp in self._copies(ins, outs, sems, k, True):
                cp.wait_send()
        for cp in self._local(ins, outs, sems):
            cp.wait()


def _pallas(body, *, name, grid, in_specs, out_specs, out_shape, scratch_shapes, vmem_mb, operands, exchange=None):
    n_in, n_out, n_scr = len(in_specs), len(out_specs), len(scratch_shapes)
    if exchange is None:
        res = pl.pallas_call(body, name=name, grid=grid, in_specs=in_specs, out_specs=out_specs, out_shape=out_shape,
                             scratch_shapes=scratch_shapes, compiler_params=_cp(vmem_mb, len(grid)))(*operands)
        return list(res), []
    ex = exchange

    def hosted(*refs):
        ins, ex_in = refs[:n_in], refs[n_in:n_in + ex.n]
        outs = refs[n_in + ex.n:n_in + ex.n + n_out]
        ex_out = refs[n_in + ex.n + n_out:n_in + 2 * ex.n + n_out]
        scr = refs[n_in + 2 * ex.n + n_out:]
        sems = scr[n_scr:]
        step = pl.program_id(0)

        @pl.when(step == 0)
        def _():
            ex.start(ex_in, ex_out, sems)

        body(*ins, *outs, *scr[:n_scr])

        @pl.when(step == grid[0] - 1)
        def _():
            ex.wait(ex_in, ex_out, sems)

    any_spec = pl.BlockSpec(memory_space=pl.ANY)
    res = pl.pallas_call(
        hosted, name=name, grid=grid, in_specs=list(in_specs) + [any_spec] * ex.n,
        out_specs=list(out_specs) + [any_spec] * ex.n, out_shape=list(out_shape) + ex.out_shapes(),
        scratch_shapes=list(scratch_shapes) + ex.sems(), compiler_params=_cp(vmem_mb, len(grid)),
    )(*operands, *ex.operands())
    return list(res[:n_out]), list(res[n_out:])


def exchange_only(ex, name):
    def body(*refs):
        ins, outs, sems = refs[:ex.n], refs[ex.n:2 * ex.n], refs[2 * ex.n:]
        ex.start(ins, outs, sems)
        ex.wait(ins, outs, sems)

    any_spec = pl.BlockSpec(memory_space=pl.ANY)
    return list(pl.pallas_call(body, name=name, in_specs=[any_spec] * ex.n, out_specs=[any_spec] * ex.n,
                               out_shape=ex.out_shapes(), scratch_shapes=ex.sems())(*ex.operands()))


def _adamw(w, g, m, v):
    m = ADAM_B1 * m + (1.0 - ADAM_B1) * g
    v = ADAM_B2 * v + (1.0 - ADAM_B2) * (g * g)
    m_hat = m / (1.0 - ADAM_B1 ** ADAM_STEP)
    v_hat = v / (1.0 - ADAM_B2 ** ADAM_STEP)
    delta = -ADAM_LR * (m_hat / (jnp.sqrt(v_hat) + ADAM_EPS) + ADAM_WD * w)
    return delta, m, v


def _sum_parts(p_ref, rows=slice(None)):
    g = p_ref[0, rows].astype(F32)
    for k in range(1, N_DEV):
        g = g + p_ref[k, rows].astype(F32)
    return g


def adamw_layers(parts, w, m, v, name):
    n_l = len(parts)

    def body(*refs):
        p_refs = refs[:n_l]
        w_ref, m_ref, v_ref, g_out, d_out, m_out, v_out = refs[n_l:]
        for l in range(n_l):
            g = _sum_parts(p_refs[l])
            g_out[l] = g
            d_out[l], m_out[l], v_out[l] = _adamw(w_ref[l], g, m_ref[l], v_ref[l])

    return pl.pallas_call(
        body, name=name, out_shape=[S(w.shape, F32)] * 4, compiler_params=_cp(48),
    )(*parts, w, m, v)


def adamw_segments(parts, segments, w, m, v, name):
    n_p = len(parts)

    def body(*refs):
        p_refs = refs[:n_p]
        w_ref, m_ref, v_ref, g_out, d_out, m_out, v_out = refs[n_p:]
        for part, src, dst, rows in segments:
            g = _sum_parts(p_refs[part], slice(src, src + rows))
            to = slice(dst, dst + rows)
            g_out[to] = g
            d_out[to], m_out[to], v_out[to] = _adamw(w_ref[to], g, m_ref[to], v_ref[to])

    return pl.pallas_call(
        body, name=name, out_shape=[S(w.shape, F32)] * 4, compiler_params=_cp(48),
    )(*parts, w, m, v)


SMALL_LAYER = ("g_mix", "A_re", "A_im", "log_dt", "B_re", "B_im", "C_re", "C_im", "D_skip", "b_glu", "w_pool",
               "pool_scale", "sgu_ln_g", "sgu_ln_b", "w_spatial", "b_spatial", "g_ffn")
BIG_NAMES = ("w_in", "w_glu", "w_out", "w_gate", "w_up", "w_down")
COLUMN_SHARDED = ("w_in", "w_gate", "w_up")
WEIGHT_ORDER = ("g_mix", "w_in", "A_re", "A_im", "log_dt", "B_re", "B_im", "C_re", "C_im", "D_skip", "w_glu", "b_glu",
                "w_pool", "pool_scale", "sgu_ln_g", "sgu_ln_b", "w_spatial", "b_spatial", "w_out", "g_ffn", "w_gate",
                "w_up", "w_down", "g_final")
SEG = 1024


def _pack(arrays):
    parts = []
    for a in arrays:
        flat = a.reshape(-1)
        parts.append(jnp.pad(flat, (0, (-flat.shape[0]) % SEG)))
    return jnp.concatenate(parts).reshape(-1, 128)


def _state_rows(p):
    return p.reshape(1, D_ST)


def _chan_by_state(p):
    return jnp.transpose(p, (2, 0, 1)).reshape(GRP, D_ST)


def _chan_by_state_c(p):
    return jnp.transpose(p, (1, 0, 2)).reshape(GRP, D_ST)


def kernel(x, g_mix, w_in, A_re, A_im, log_dt, B_re, B_im, C_re, C_im, D_skip, w_glu, b_glu, w_pool, pool_scale, sgu_ln_g, sgu_ln_b, w_spatial, b_spatial, w_out, g_ffn, w_gate, w_up, w_down, g_final, loss_target, m_g_mix, m_w_in, m_A_re, m_A_im, m_log_dt, m_B_re, m_B_im, m_C_re, m_C_im, m_D_skip, m_w_glu, m_b_glu, m_w_pool, m_pool_scale, m_sgu_ln_g, m_sgu_ln_b, m_w_spatial, m_b_spatial, m_w_out, m_g_ffn, m_w_gate, m_w_up, m_w_down, m_g_final, v_g_mix, v_w_in, v_A_re, v_A_im, v_log_dt, v_B_re, v_B_im, v_C_re, v_C_im, v_D_skip, v_w_glu, v_b_glu, v_w_pool, v_pool_scale, v_sgu_ln_g, v_sgu_ln_b, v_w_spatial, v_b_spatial, v_w_out, v_g_ffn, v_w_gate, v_w_up, v_w_down, v_g_final):
    args = dict(locals())
    W = {n: args[n] for n in WEIGHT_ORDER}
    M = {n: args["m_" + n] for n in WEIGHT_ORDER}
    V = {n: args["v_" + n] for n in WEIGHT_ORDER}
    n_layers = g_mix.shape[0]
    x0 = x[0]
    target = loss_target[0]

    def my_rows(name, l):
        w = W[name][l]
        return (w.T if name in COLUMN_SHARDED else w).astype(BF16)

    full_w = [dict() for _ in range(n_layers)]

    def gather_of(*which):
        return _Exchange(gather=[my_rows(n, l) for n, l in which])

    def keep_gathered(which, arrays):
        for (n, l), a in zip(which, arrays):
            full_w[l][n] = a.reshape(-1, a.shape[-1])

    tri = jnp.tril(jnp.ones((CHUNK, CHUNK), bool))
    consts = []
    for l in range(n_layers):
        a_re, a_im = _state_rows(A_re[l]), _state_rows(A_im[l])
        ldt = jnp.repeat(log_dt[l], N_STATE).reshape(1, D_ST)
        b_re_t, b_im_t = _chan_by_state(B_re[l]), _chan_by_state(B_im[l])
        sc_f, sc_b, bbd, cbd = s5_prepare(a_re, a_im, ldt, b_re_t, b_im_t, _chan_by_state_c(C_re[l]), _chan_by_state_c(C_im[l]))
        wsm = jnp.where(tri[None], w_spatial[l], 0.0)
        wpool_bd = jnp.zeros((D_POOL, D_POOL), F32)
        for gi in range(len(POOL_WINDOWS)):
            wpool_bd = wpool_bd.at[gi * 64:(gi + 1) * 64, gi * 64:(gi + 1) * 64].set(w_pool[l, gi])
        consts.append(dict(
            disc=(a_re, a_im, ldt, b_re_t, b_im_t), sc_f=sc_f, sc_b=sc_b, bbd=bbd, cbd=cbd,
            dskip=D_skip[l].reshape(1, D_SSM), bglu=b_glu[l].reshape(1, D_SSM),
            wpool=wpool_bd.astype(BF16), pscale=pool_scale[l].reshape(1, D_POOL),
            lng=sgu_ln_g[l].reshape(1, D_SGU), lnb=sgu_ln_b[l].reshape(1, D_SGU),
            wsm=wsm.astype(BF16), wsmt=jnp.transpose(wsm, (0, 2, 1)).astype(BF16),
            bsp=jnp.repeat(b_spatial[l].T, HEAD_DIM, axis=1),
            gmix=g_mix[l].reshape(1, D), gffn=g_ffn[l].reshape(1, D)))

    def mixer_args(l):
        c = consts[l]
        return (c["bbd"], c["cbd"], c["dskip"], full_w[l]["w_glu"], c["bglu"], c["wpool"], c["pscale"], c["lng"], c["lnb"])

    first_needed = [("w_in", 0)]
    keep_gathered(first_needed, exchange_only(gather_of(*first_needed), "gather_first"))
    carried_fwd = {
        ("inproj", 0): [("w_glu", 0), ("w_out", 0)],
        ("mixer", 0): [("w_gate", 0), ("w_up", 0), ("w_down", 0)],
        ("ffn", 0): [("w_in", 1), ("w_glu", 1), ("w_out", 1), ("w_gate", 1)],
        ("mixer", 1): [("w_up", 1), ("w_down", 1)],
    }

    def carried(kind, l):
        which = carried_fwd.get((kind, l))
        return which, (gather_of(*which) if which else None)

    saved = []
    xl = x0
    for l in range(n_layers):
        c, fw = consts[l], full_w[l]
        which, ex = carried("inproj", l)
        (h, z), got = inproj_fwd(xl, c["gmix"], fw["w_in"], f"inproj_fwd_{l}", ex)
        keep_gathered(which or [], got)
        which, ex = carried("mixer", l)
        (ycat, hs, ys), got = mixer_fwd(z, c["sc_f"], *mixer_args(l), c["wsm"], c["bsp"], f"mixer_fwd_{l}", ex)
        keep_gathered(which or [], got)
        which, ex = carried("ffn", l)
        (x1, h2, gate, up, act, x2), got = ffn_fwd(xl, ycat, fw["w_out"], c["gffn"], fw["w_gate"], fw["w_up"], fw["w_down"],
                                                    f"ffn_fwd_{l}", ex)
        keep_gathered(which or [], got)
        saved.append(dict(x=xl, h=h, z=z, ycat=ycat, hs=hs, ys=ys, x1=x1, h2=h2, gate=gate, up=up, act=act))
        xl = x2
    dx, loss_vec, d_gfinal = loss_head(xl, target, g_final.reshape(1, D))

    recv_big = {}
    recv_small = [None] * n_layers

    def keep_received(which, arrays):
        for key, a in zip(which, arrays):
            if key[0] == "small":
                recv_small[key[1]] = a
            else:
                recv_big[key] = a

    pending = None
    for l in reversed(range(n_layers)):
        c, fw, sv = consts[l], full_w[l], saved[l]
        (dgu, dx2b, dx1, dx1b, d_gffn), got = ffn_bwd(dx, sv["x1"], sv["gate"], sv["up"], c["gffn"], fw["w_gate"], fw["w_up"],
                                                     fw["w_down"], f"ffn_bwd_{l}", pending[1] if pending else None)
        if pending:
            keep_received(pending[0], got)
        g_gu = wgrad(dgu, sv["h2"], D_FF // 2, f"wgrad_gate_up_{l}")
        g_down = wgrad(sv["act"], dx2b, D_FF // 2, f"wgrad_down_{l}")
        ffn_rows = D_FF // N_DEV
        ex = _Exchange(scatter=[(g_gu, 0, ffn_rows), (g_gu, D_FF, ffn_rows), (g_down, 0, ffn_rows)])
        (dzb, da, dbt, dct, dd, dbglu, dwglu, dwpool, dpscale, dlng, dlnb, dws, dbsp), got = mixer_bwd(
            dx1b, sv["z"], sv["hs"], sv["ys"], fw["w_out"], c["sc_b"], *mixer_args(l), c["wsm"], c["wsmt"], c["bsp"],
            f"mixer_bwd_{l}", ex)
        keep_received([("w_gate", l), ("w_up", l), ("w_down", l)], got)
        dx, d_gmix = inproj_bwd(dzb, sv["x"], c["gmix"], fw["w_in"], dx1)
        g_in = wgrad(dzb, sv["h"], D_IN, f"wgrad_in_{l}")
        g_out = wgrad(sv["ycat"], dx1b, D, f"wgrad_out_{l}")
        d_are, d_aim, d_ldt, d_bre_t, d_bim_t = s5_param_bwd(*c["disc"], da, dbt)
        small = dict(
            g_mix=d_gmix.reshape(D), A_re=d_are.reshape(N_GRP, N_STATE), A_im=d_aim.reshape(N_GRP, N_STATE),
            log_dt=d_ldt[0, :N_GRP],
            B_re=jnp.transpose(d_bre_t.reshape(GRP, N_GRP, N_STATE), (1, 2, 0)),
            B_im=jnp.transpose(d_bim_t.reshape(GRP, N_GRP, N_STATE), (1, 2, 0)),
            C_re=jnp.transpose(dct[:, :D_ST].reshape(GRP, N_GRP, N_STATE), (1, 0, 2)),
            C_im=-jnp.transpose(dct[:, D_ST:].reshape(GRP, N_GRP, N_STATE), (1, 0, 2)),
            D_skip=dd.reshape(D_SSM), b_glu=dbglu.reshape(D_SSM),
            w_pool=jnp.stack([dwpool[gi * 64:(gi + 1) * 64, gi * 64:(gi + 1) * 64] for gi in range(len(POOL_WINDOWS))]),
            pool_scale=dpscale.reshape(D_POOL), sgu_ln_g=dlng.reshape(D_SGU), sgu_ln_b=dlnb.reshape(D_SGU),
            w_spatial=dws, b_spatial=dbsp[:, :HEADS].T, g_ffn=d_gffn.reshape(D))
        packed = [small[n] for n in SMALL_LAYER] + ([d_gfinal.reshape(D), loss_vec.reshape(D)] if l == 0 else [])
        pending = ([("small", l), ("w_in", l), ("w_out", l), ("w_glu", l)],
                   _Exchange(gather=[_pack(packed)],
                             scatter=[(g_in, 0, D_IN // N_DEV), (g_out, 0, D // N_DEV), (dwglu.astype(BF16), 0, D_SSM // N_DEV)]))
    grad_x = dx
    keep_received(pending[0], exchange_only(pending[1], "exchange_last"))

    out = {}
    for n in BIG_NAMES:
        tr = (lambda a: jnp.transpose(a, (0, 2, 1))) if n in COLUMN_SHARDED else (lambda a: a)
        res = adamw_layers([recv_big[(n, l)] for l in range(n_layers)], tr(W[n]), tr(M[n]), tr(V[n]), f"adamw_{n}")
        out[n] = [tr(r) for r in res]

    seg_rows = [(-(-math.prod(W[n].shape[1:]) // SEG)) * (SEG // 128) for n in SMALL_LAYER]
    segments, src, dst = [], 0, 0
    for rows in seg_rows:
        segments += [(l, src, dst + l * rows, rows) for l in range(n_layers)]
        src += rows
        dst += n_layers * rows
    tile_rows = SEG // 128
    segments += [(0, src, dst, tile_rows), (0, src + tile_rows, dst + tile_rows, tile_rows)]

    def pack_params(P):
        parts = []
        for n, rows in zip(SMALL_LAYER, seg_rows):
            flat = P[n].reshape(n_layers, -1)
            parts.append(jnp.pad(flat, ((0, 0), (0, rows * 128 - flat.shape[1]))).reshape(-1))
        return jnp.concatenate(parts + [P["g_final"], jnp.zeros((SEG,), F32)]).reshape(-1, 128)

    res = adamw_segments(recv_small, segments, pack_params(W), pack_params(M), pack_params(V), "adamw_small")
    for j in range(4):
        flat, off = res[j].reshape(-1), 0
        for n, rows in zip(SMALL_LAYER, seg_rows):
            size = math.prod(W[n].shape[1:])
            piece = flat[off:off + n_layers * rows * 128].reshape(n_layers, rows * 128)[:, :size].reshape(W[n].shape)
            out.setdefault(n, []).append(piece)
            off += n_layers * rows * 128
        out.setdefault("g_final", []).append(flat[off:off + D])
        if j == 0:
            loss = (0.5 / D) * jnp.sum(flat[off + SEG:off + SEG + D])

    return (loss, grad_x[None], *[out[n][0] for n in WEIGHT_ORDER], *[out[n][1] for n in WEIGHT_ORDER],
            *[out[n][2] for n in WEIGHT_ORDER], *[out[n][3] for n in WEIGHT_ORDER])
```

```python
import functools
import math

import jax
import jax.numpy as jnp
from jax import lax
from jax.experimental import pallas as pl
from jax.experimental.pallas import tpu as pltpu

F32 = jnp.float32
BF16 = jnp.bfloat16
S = jax.ShapeDtypeStruct

N_DEV = 8
D = 1024
D_SSM = 384
N_GRP = 24
GRP = 16
N_STATE = 64
D_ST = N_GRP * N_STATE
D_POOL = 256
POOL_WINDOWS = (2, 4, 8, 16)
HALO = 16
D_SGU = 384
HEADS = 6
HEAD_DIM = 64
CHUNK = 128
D_IN = 1408
D_FF = 2816
EPS = 1e-6
SCAN_BLK = 8

ADAM_LR = 0.001
ADAM_B1 = 0.9
ADAM_B2 = 0.999
ADAM_EPS = 1e-08
ADAM_WD = 0.01
ADAM_STEP = 10

GELU_C0 = math.sqrt(2.0 / math.pi)
GELU_C1 = 0.044715

TT_MIX = 256
SEG_LEN = TT_MIX // SCAN_BLK
TT_FFN = 256
TT_PROJ = 512
TK_WGRAD = 512
VMEM_MB = 2 ** 20


def _cp(vmem_mb, grid_dims=0):
    kw = dict(vmem_limit_bytes=int(vmem_mb * VMEM_MB))
    if grid_dims:
        kw["dimension_semantics"] = ("arbitrary",) * grid_dims
    return pltpu.CompilerParams(**kw)


def _row(tt, n):
    return pl.BlockSpec((tt, n), lambda i: (i, 0))


def _full(shape):
    nd = len(shape)
    return pl.BlockSpec(shape, lambda *_: (0,) * nd)


def _nn(a, b):
    return jnp.dot(a, b, preferred_element_type=F32)


def _nt(a, b):
    return lax.dot_general(a, b, (((1,), (1,)), ((), ())), preferred_element_type=F32)


def _tn(a, b):
    return lax.dot_general(a, b, (((0,), (0,)), ((), ())), preferred_element_type=F32)


def _rowsum(x):
    return jnp.sum(x, axis=0, keepdims=True)


def _rms(x):
    r = lax.rsqrt(jnp.mean(x * x, axis=-1, keepdims=True) + EPS)
    return x * r, r


def _rms_bwd(dy, xn, r, g):
    dyg = dy * g
    return r * (dyg - xn * jnp.mean(dyg * xn, axis=-1, keepdims=True))


def _gelu(x):
    t = jnp.tanh(GELU_C0 * (x + GELU_C1 * x * x * x))
    return 0.5 * x * (1.0 + t), t


def _gelu_grad(x, t):
    return 0.5 * (1.0 + t) + 0.5 * x * (1.0 - t * t) * (GELU_C0 * (1.0 + 3.0 * GELU_C1 * x * x))


def _discretise(a_re, a_im, ldt, b_re, b_im):
    dt = jnp.exp(ldt)
    mag = jnp.exp(a_re * dt)
    ar = mag * jnp.cos(a_im * dt)
    ai = mag * jnp.sin(a_im * dt)
    den = a_re * a_re + a_im * a_im
    f_re = ((ar - 1.0) * a_re + ai * a_im) / den
    f_im = (ai * a_re - (ar - 1.0) * a_im) / den
    bb_re = f_re * b_re - f_im * b_im
    bb_im = f_re * b_im + f_im * b_re
    return ar, ai, bb_re, bb_im


def _group_mask(rows, cols):
    r = lax.broadcasted_iota(jnp.int32, (rows, cols), 0) // GRP
    c = lax.broadcasted_iota(jnp.int32, (rows, cols), 1)
    c = jnp.where(c >= D_ST, c - D_ST, c) // N_STATE
    return r == c


def s5_prepare(a_re, a_im, ldt, b_re_t, b_im_t, c_re_t, c_im_t):
    def body(are_ref, aim_ref, ldt_ref, bre_ref, bim_ref, cre_ref, cim_ref, sc_ref, bbd_ref, cbd_ref):
        ar, ai, bb_re, bb_im = _discretise(are_ref[...], aim_ref[...], ldt_ref[...], bre_ref[...], bim_ref[...])
        mask = _group_mask(D_SSM, 2 * D_ST)
        bb = jnp.concatenate([jnp.tile(bb_re, (N_GRP, 1)), jnp.tile(bb_im, (N_GRP, 1))], axis=1)
        bbd_ref[...] = jnp.where(mask, bb, 0.0).astype(BF16)
        cc = jnp.concatenate([jnp.tile(cre_ref[...], (N_GRP, 1)), -jnp.tile(cim_ref[...], (N_GRP, 1))], axis=1)
        cbd_ref[...] = jnp.where(mask, cc, 0.0).astype(BF16)
        pr, pi = ar, ai
        for _ in range(SEG_LEN - 1):
            pr, pi = pr * ar - pi * ai, pr * ai + pi * ar
        for k, v in enumerate((ar, ai, pr, pi)):
            sc_ref[8 * k:8 * k + 8, :] = jnp.broadcast_to(v, (SCAN_BLK, D_ST))

    return pl.pallas_call(
        body, name="s5_prepare",
        out_shape=[S((32, D_ST), F32), S((D_SSM, 2 * D_ST), BF16), S((D_SSM, 2 * D_ST), BF16)],
        compiler_params=_cp(40),
    )(a_re, a_im, ldt, b_re_t, b_im_t, c_re_t, c_im_t)


def s5_param_bwd(a_re, a_im, ldt, b_re_t, b_im_t, da, dbt):
    def body(are_ref, aim_ref, ldt_ref, bre_ref, bim_ref, da_ref, dbt_ref, o_are, o_aim, o_ldt, o_bre, o_bim):
        _, vjp = jax.vjp(_discretise, are_ref[...], aim_ref[...], ldt_ref[...], bre_ref[...], bim_ref[...])
        da = da_ref[...]
        dbt = dbt_ref[...]
        g_are, g_aim, g_ldt, g_bre, g_bim = vjp((da[:, :D_ST], da[:, D_ST:], dbt[:, :D_ST], dbt[:, D_ST:]))
        o_are[...] = g_are
        o_aim[...] = g_aim
        o_bre[...] = g_bre
        o_bim[...] = g_bim
        grp = lax.broadcasted_iota(jnp.int32, (1, D_ST), 1) // N_STATE
        lane = lax.broadcasted_iota(jnp.int32, (1, 128), 1)
        out = jnp.zeros((1, 128), F32)
        for g in range(N_GRP):
            out = jnp.where(lane == g, jnp.sum(jnp.where(grp == g, g_ldt, 0.0), axis=1, keepdims=True), out)
        o_ldt[...] = out

    return pl.pallas_call(
        body, name="s5_param_bwd",
        out_shape=[S((1, D_ST), F32), S((1, D_ST), F32), S((1, 128), F32), S((GRP, D_ST), F32), S((GRP, D_ST), F32)],
        compiler_params=_cp(16),
    )(a_re, a_im, ldt, b_re_t, b_im_t, da, dbt)


_CH = ((0, 256), (256, D_SSM))
_ST = ((0, 1024), (1024, D_ST))


def _bd_expand(xb, w_ref, out_ref):
    for (c0, c1), (s0, s1) in zip(_CH, _ST):
        for half in (0, D_ST):
            out_ref[:, half + s0:half + s1] = _nn(xb[:, c0:c1], w_ref[c0:c1, half + s0:half + s1])


def _bd_contract(hb, w_ref):
    parts = []
    for (c0, c1), (s0, s1) in zip(_CH, _ST):
        parts.append(_nt(hb[:, s0:s1], w_ref[c0:c1, s0:s1]) + _nt(hb[:, D_ST + s0:D_ST + s1], w_ref[c0:c1, D_ST + s0:D_ST + s1]))
    return jnp.concatenate(parts, axis=1)


def _bd_accumulate(acc_ref, xb, hb):
    for (c0, c1), (s0, s1) in zip(_CH, _ST):
        for half in (0, D_ST):
            acc_ref[c0:c1, half + s0:half + s1] += _tn(xb[:, c0:c1], hb[:, half + s0:half + s1])


def _interleave_matrices(tt):
    r = lax.broadcasted_iota(jnp.int32, (tt, tt), 0)
    t = lax.broadcasted_iota(jnp.int32, (tt, tt), 1)
    seg = tt // SCAN_BLK
    p = (t == (r % SCAN_BLK) * seg + r // SCAN_BLK)
    pt = (r == (t % SCAN_BLK) * seg + t // SCAN_BLK)
    return p.astype(BF16), pt.astype(BF16)


def _interleave_f32(p, x):
    hi = x.astype(BF16)
    lo = (x - hi.astype(F32)).astype(BF16)
    return _nn(p, hi) + _nn(p, lo)


def _scan_tile(buf_ref, sc_ref, carry_ref, n_blk, reverse):
    ar = sc_ref[0:8, :]
    ai = -sc_ref[8:16, :] if reverse else sc_ref[8:16, :]

    def rows(i):
        blk = (n_blk - 1 - i) if reverse else i
        return pl.ds(pl.multiple_of(blk * SCAN_BLK, SCAN_BLK), SCAN_BLK)

    def local(i, x):
        xr, xi = x
        r = rows(i)
        xr, xi = buf_ref[r, 0:D_ST] + ar * xr - ai * xi, buf_ref[r, D_ST:2 * D_ST] + ar * xi + ai * xr
        buf_ref[r, 0:D_ST] = xr
        buf_ref[r, D_ST:2 * D_ST] = xi
        return xr, xi

    zero = jnp.zeros((SCAN_BLK, D_ST), F32)
    end_r, end_i = lax.fori_loop(0, n_blk, local, (zero, zero), unroll=2)

    seg_r = sc_ref[16:17, :]
    seg_i = -sc_ref[24:25, :] if reverse else sc_ref[24:25, :]
    cr, ci = carry_ref[0:1, 0:D_ST], carry_ref[0:1, D_ST:2 * D_ST]
    sub = lax.broadcasted_iota(jnp.int32, (SCAN_BLK, D_ST), 0)
    in_r, in_i = zero, zero
    for s in (reversed(range(SCAN_BLK)) if reverse else range(SCAN_BLK)):
        in_r = jnp.where(sub == s, cr, in_r)
        in_i = jnp.where(sub == s, ci, in_i)
        cr, ci = end_r[s:s + 1, :] + seg_r * cr - seg_i * ci, end_i[s:s + 1, :] + seg_r * ci + seg_i * cr
    carry_ref[0:1, 0:D_ST] = cr
    carry_ref[0:1, D_ST:2 * D_ST] = ci

    def fix(i, d):
        dr, di = d
        dr, di = ar * dr - ai * di, ar * di + ai * dr
        r = rows(i)
        buf_ref[r, 0:D_ST] += dr
        buf_ref[r, D_ST:2 * D_ST] += di
        return dr, di

    lax.fori_loop(0, n_blk, fix, (in_r, in_i), unroll=2)


def _lane_windows(n):
    lane = lax.broadcasted_iota(jnp.int32, (1, n), 1)
    return lane // (D_POOL // len(POOL_WINDOWS))


def _select_window(grp, s2, s4, s8, s16):
    return jnp.where(grp == 0, s2, jnp.where(grp == 1, s4, jnp.where(grp == 2, s8, s16)))


def _pool_fwd(pbuf_ref, zb, halo, tile_idx, tt):
    pbuf_ref[0:HALO, :] = halo
    pbuf_ref[HALO:HALO + tt, :] = zb
    x = pbuf_ref[...]
    s2 = x + pltpu.roll(x, 1, axis=0)
    s4 = s2 + pltpu.roll(s2, 2, axis=0)
    s8 = s4 + pltpu.roll(s4, 4, axis=0)
    s16 = s8 + pltpu.roll(s8, 8, axis=0)
    grp = _lane_windows(D_POOL)
    win = _select_window(grp, s2, s4, s8, s16)[HALO:HALO + tt, :]
    width = _select_window(grp, 2.0, 4.0, 8.0, 16.0).astype(F32)
    pos = (tile_idx * tt + 1 + lax.broadcasted_iota(jnp.int32, (tt, 1), 0)).astype(F32)
    cnt = jnp.minimum(pos, width)
    return win / cnt - zb, cnt


def _sgu_fwd(zu, zv, lng, lnb, wsm_ref, bsp, mix_ref, tt):
    u, tu = _gelu(zu)
    v, tv = _gelu(zv)
    mu = jnp.mean(v, axis=-1, keepdims=True)
    vc = v - mu
    rstd = lax.rsqrt(jnp.mean(vc * vc, axis=-1, keepdims=True) + EPS)
    vhat = vc * rstd
    vnb = (vhat * lng + lnb).astype(BF16)
    for c in range(tt // CHUNK):
        rows = slice(c * CHUNK, (c + 1) * CHUNK)
        parts = [_nn(wsm_ref[h], vnb[rows, h * HEAD_DIM:(h + 1) * HEAD_DIM]) for h in range(HEADS)]
        mix_ref[rows, :] = jnp.concatenate(parts, axis=1) + bsp
    return u, tu, tv, vhat, rstd, vnb


def mixer_fwd(z, sc_f, bbd, cbd, dskip, wglu, bglu, wpool, pscale, lng, lnb, wsm, bsp, name, exchange=None):
    T = z.shape[0]
    tt = TT_MIX
    n_tiles = T // tt

    def body(z_ref, scf_ref, bbd_ref, cbd_ref, dskip_ref, wglu_ref, bglu_ref, wpool_ref, pscale_ref, lng_ref, lnb_ref,
             wsm_ref, bsp_ref, ycat_ref, hs_ref, ys_ref, carry_ref, halo_ref, pbuf_ref, mix_ref):
        i = pl.program_id(0)

        @pl.when(i == 0)
        def _():
            carry_ref[...] = jnp.zeros_like(carry_ref)
            halo_ref[...] = jnp.zeros_like(halo_ref)

        za = z_ref[:, 0:D_SSM]
        zb = z_ref[:, D_SSM:D_SSM + D_POOL]
        zu = z_ref[:, D_SSM + D_POOL:D_SSM + D_POOL + D_SGU]
        zv = z_ref[:, D_SSM + D_POOL + D_SGU:D_IN]
        p, pt = _interleave_matrices(tt)
        za = _interleave_f32(p, za)
        _bd_expand(za.astype(BF16), bbd_ref, hs_ref)
        _scan_tile(hs_ref, scf_ref, carry_ref, tt // SCAN_BLK, reverse=False)
        y = _bd_contract(hs_ref[...].astype(BF16), cbd_ref) + dskip_ref[...] * za
        ys_ref[...] = y
        g, _ = _gelu(y)
        q = _nn(g.astype(BF16), wglu_ref[...]) + bglu_ref[...]
        ycat_ref[:, 0:D_SSM] = _nn(pt, (g * jax.nn.sigmoid(q)).astype(BF16)).astype(BF16)
        pooled, _ = _pool_fwd(pbuf_ref, zb, halo_ref[...], i, tt)
        halo_ref[...] = zb[tt - HALO:tt, :]
        ycat_ref[:, D_SSM:D_SSM + D_POOL] = (_nn(pooled.astype(BF16), wpool_ref[...]) * pscale_ref[...]).astype(BF16)
        u, _, _, _, _, _ = _sgu_fwd(zu, zv, lng_ref[...], lnb_ref[...], wsm_ref, bsp_ref[...], mix_ref, tt)
        ycat_ref[:, D_SSM + D_POOL:D] = (u * mix_ref[...]).astype(BF16)

    return _pallas(
        body, name=name, grid=(n_tiles,),
        in_specs=[_row(tt, D_IN), _full((32, D_ST)), _full((D_SSM, 2 * D_ST)), _full((D_SSM, 2 * D_ST)),
                  _full((1, D_SSM)), _full((D_SSM, D_SSM)), _full((1, D_SSM)), _full((D_POOL, D_POOL)),
                  _full((1, D_POOL)), _full((1, D_SGU)), _full((1, D_SGU)), _full((HEADS, CHUNK, CHUNK)),
                  _full((CHUNK, D_SGU))],
        out_specs=[_row(tt, D), _row(tt, 2 * D_ST), _row(tt, D_SSM)],
        out_shape=[S((T, D), BF16), S((T, 2 * D_ST), F32), S((T, D_SSM), F32)],
        scratch_shapes=[pltpu.VMEM((SCAN_BLK, 2 * D_ST), F32), pltpu.VMEM((HALO, D_POOL), F32),
                        pltpu.VMEM((tt + HALO, D_POOL), F32), pltpu.VMEM((tt, D_SGU), F32)],
        vmem_mb=48, operands=(z, sc_f, bbd, cbd, dskip, wglu, bglu, wpool, pscale, lng, lnb, wsm, bsp),
        exchange=exchange)


def mixer_bwd(dx1b, z, hs, ys, wout, sc_b, bbd, cbd, dskip, wglu, bglu, wpool, pscale, lng, lnb, wsm, wsmt, bsp, name,
              exchange=None):
    T = z.shape[0]
    tt = TT_MIX
    n_tiles = T // tt

    def rev(i):
        return n_tiles - 1 - i

    def body(dx_ref, z_ref, zprev_ref, hs_ref, hsprev_ref, ys_ref, wout_ref, sc_ref, bbd_ref, cbd_ref, dskip_ref,
             wglu_ref, bglu_ref, wpool_ref, pscale_ref, lng_ref, lnb_ref, wsm_ref, wsmt_ref, bsp_ref,
             dz_ref, o_da, o_dbt, o_dct, o_dd, o_dbglu, o_dwglu, o_dwpool, o_dpscale, o_dlng, o_dlnb, o_dws, o_dbsp,
             gbuf_ref, carry_ref, ehalo_ref, pbuf_ref, mix_ref, dvn_ref, accb_ref, accc_ref, accw_ref, accm_ref):
        i = pl.program_id(0)
        tile = rev(i)

        @pl.when(i == 0)
        def _():
            carry_ref[...] = jnp.zeros_like(carry_ref)
            ehalo_ref[...] = jnp.zeros_like(ehalo_ref)
            accb_ref[...] = jnp.zeros_like(accb_ref)
            accc_ref[...] = jnp.zeros_like(accc_ref)
            accw_ref[...] = jnp.zeros_like(accw_ref)
            accm_ref[...] = jnp.zeros_like(accm_ref)
            for o in (o_da, o_dd, o_dbglu, o_dwglu, o_dwpool, o_dpscale, o_dlng, o_dlnb):
                o[...] = jnp.zeros_like(o)

        p, pt = _interleave_matrices(tt)
        dxb = dx_ref[...]
        d_a = _nt(_nn(p, dxb).astype(BF16), wout_ref[0:D_SSM, :])
        d_bc = _nt(dxb, wout_ref[D_SSM:D, :])
        d_b = d_bc[:, 0:D_POOL]
        d_c = d_bc[:, D_POOL:D_POOL + D_SGU]
        za = _interleave_f32(p, z_ref[:, 0:D_SSM])
        zb = z_ref[:, D_SSM:D_SSM + D_POOL]
        zu = z_ref[:, D_SSM + D_POOL:D_SSM + D_POOL + D_SGU]
        zv = z_ref[:, D_SSM + D_POOL + D_SGU:D_IN]
        first_tile = (tile > 0).astype(F32)

        y = ys_ref[...]
        g, tg = _gelu(y)
        gb = g.astype(BF16)
        sg = jax.nn.sigmoid(_nn(gb, wglu_ref[...]) + bglu_ref[...])
        dq = d_a * g * sg * (1.0 - sg)
        dqb = dq.astype(BF16)
        o_dbglu[...] += _rowsum(dq)
        o_dwglu[...] += _tn(gb, dqb)
        dy = (d_a * sg + _nt(dqb, wglu_ref[...])) * _gelu_grad(y, tg)
        o_dd[...] += _rowsum(dy * za)
        dyb = dy.astype(BF16)
        _bd_accumulate(accc_ref, dyb, hs_ref[...].astype(BF16))
        _bd_expand(dyb, cbd_ref, gbuf_ref)
        _scan_tile(gbuf_ref, sc_ref, carry_ref, tt // SCAN_BLK, reverse=True)
        hprev = hsprev_ref[SCAN_BLK - 1:SCAN_BLK, :] * first_tile
        sub = lax.broadcasted_iota(jnp.int32, (SCAN_BLK, 1), 0)
        edge = jnp.where(sub == 0, hprev, pltpu.roll(hs_ref[tt - SCAN_BLK:tt, :], 1, axis=0))

        def da_terms(gr, gi, hr, hi):
            return _rowsum(gr * hr + gi * hi), _rowsum(gi * hr - gr * hi)

        body_re, body_im = da_terms(gbuf_ref[SCAN_BLK:tt, 0:D_ST], gbuf_ref[SCAN_BLK:tt, D_ST:],
                                    hs_ref[0:tt - SCAN_BLK, 0:D_ST], hs_ref[0:tt - SCAN_BLK, D_ST:])
        edge_re, edge_im = da_terms(gbuf_ref[0:SCAN_BLK, 0:D_ST], gbuf_ref[0:SCAN_BLK, D_ST:], edge[:, 0:D_ST], edge[:, D_ST:])
        o_da[:, 0:D_ST] += body_re + edge_re
        o_da[:, D_ST:] += body_im + edge_im
        gtb = gbuf_ref[...].astype(BF16)
        _bd_accumulate(accb_ref, za.astype(BF16), gtb)
        dza = (dy * dskip_ref[...] + _bd_contract(gtb, bbd_ref)).astype(BF16)
        dz_ref[:, 0:D_SSM] = _nn(pt, dza).astype(BF16)

        pooled, cnt = _pool_fwd(pbuf_ref, zb, zprev_ref[:, D_SSM:D_SSM + D_POOL] * first_tile, tile, tt)
        pooledb = pooled.astype(BF16)
        mixed = _nn(pooledb, wpool_ref[...])
        o_dpscale[...] += _rowsum(d_b * mixed)
        dmixb = (d_b * pscale_ref[...]).astype(BF16)
        o_dwpool[...] += _tn(pooledb, dmixb)
        dpooled = _nt(dmixb, wpool_ref[...])
        e = dpooled / cnt
        pbuf_ref[0:tt, :] = e
        pbuf_ref[tt:tt + HALO, :] = ehalo_ref[...]
        ehalo_ref[...] = e[0:HALO, :]
        x = pbuf_ref[...]
        n = tt + HALO
        f2 = x + pltpu.roll(x, n - 1, axis=0)
        f4 = f2 + pltpu.roll(f2, n - 2, axis=0)
        f8 = f4 + pltpu.roll(f4, n - 4, axis=0)
        f16 = f8 + pltpu.roll(f8, n - 8, axis=0)
        fwd_sum = _select_window(_lane_windows(D_POOL), f2, f4, f8, f16)[0:tt, :]
        dz_ref[:, D_SSM:D_SSM + D_POOL] = (fwd_sum - dpooled).astype(BF16)

        lng = lng_ref[...]
        u, tu, tv, vhat, rstd, vnb = _sgu_fwd(zu, zv, lng, lnb_ref[...], wsm_ref, bsp_ref[...], mix_ref, tt)
        dz_ref[:, D_SSM + D_POOL:D_SSM + D_POOL + D_SGU] = (d_c * mix_ref[...] * _gelu_grad(zu, tu)).astype(BF16)
        dmix = d_c * u
        dmixb2 = dmix.astype(BF16)
        for c in range(tt // CHUNK):
            rows = slice(c * CHUNK, (c + 1) * CHUNK)
            accm_ref[...] += dmix[rows, :]
            parts = []
            for h in range(HEADS):
                cols = slice(h * HEAD_DIM, (h + 1) * HEAD_DIM)
                accw_ref[h] += _nt(dmixb2[rows, cols], vnb[rows, cols])
                parts.append(_nn(wsmt_ref[h], dmixb2[rows, cols]))
            dvn_ref[rows, :] = jnp.concatenate(parts, axis=1)
        dvn = dvn_ref[...]
        o_dlng[...] += _rowsum(dvn * vhat)
        o_dlnb[...] += _rowsum(dvn)
        dvh = dvn * lng
        dv = rstd * (dvh - jnp.mean(dvh, axis=-1, keepdims=True) - vhat * jnp.mean(dvh * vhat, axis=-1, keepdims=True))
        dz_ref[:, D_SSM + D_POOL + D_SGU:D_IN] = (dv * _gelu_grad(zv, tv)).astype(BF16)

        @pl.when(i == n_tiles - 1)
        def _():
            mask = _group_mask(D_SSM, 2 * D_ST)
            fb = jnp.zeros((GRP, 2 * D_ST), F32)
            fc = jnp.zeros((GRP, 2 * D_ST), F32)
            for gidx in range(N_GRP):
                rows = slice(gidx * GRP, (gidx + 1) * GRP)
                fb = fb + jnp.where(mask[rows, :], accb_ref[rows, :], 0.0)
                fc = fc + jnp.where(mask[rows, :], accc_ref[rows, :], 0.0)
            o_dbt[...] = fb
            o_dct[...] = fc
            tri = (lax.broadcasted_iota(jnp.int32, (CHUNK, CHUNK), 0) >= lax.broadcasted_iota(jnp.int32, (CHUNK, CHUNK), 1))
            for h in range(HEADS):
                o_dws[h] = jnp.where(tri, accw_ref[h], 0.0)
            lane = lax.broadcasted_iota(jnp.int32, (1, 128), 1)
            acc = jnp.zeros((CHUNK, 128), F32)
            for h in range(HEADS):
                sh = jnp.sum(accm_ref[:, h * HEAD_DIM:(h + 1) * HEAD_DIM], axis=1, keepdims=True)
                acc = jnp.where(lane == h, sh, acc)
            o_dbsp[...] = acc

    def rowr(n):
        return pl.BlockSpec((tt, n), lambda i: (rev(i), 0))

    zprev_spec = pl.BlockSpec((HALO, D_IN), lambda i: (jnp.maximum(rev(i) * (tt // HALO) - 1, 0), 0))
    hsprev_spec = pl.BlockSpec((SCAN_BLK, 2 * D_ST), lambda i: (jnp.maximum(rev(i) * (tt // SCAN_BLK) - 1, 0), 0))
    small = [S((1, 2 * D_ST), F32), S((GRP, 2 * D_ST), F32), S((GRP, 2 * D_ST), F32), S((1, D_SSM), F32),
             S((1, D_SSM), F32), S((D_SSM, D_SSM), F32), S((D_POOL, D_POOL), F32), S((1, D_POOL), F32),
             S((1, D_SGU), F32), S((1, D_SGU), F32), S((HEADS, CHUNK, CHUNK), F32), S((CHUNK, 128), F32)]
    return _pallas(
        body, name=name, grid=(n_tiles,),
        in_specs=[rowr(D), rowr(D_IN), zprev_spec, rowr(2 * D_ST), hsprev_spec, rowr(D_SSM), _full((D, D)),
                  _full((32, D_ST)), _full((D_SSM, 2 * D_ST)), _full((D_SSM, 2 * D_ST)), _full((1, D_SSM)),
                  _full((D_SSM, D_SSM)), _full((1, D_SSM)), _full((D_POOL, D_POOL)), _full((1, D_POOL)),
                  _full((1, D_SGU)), _full((1, D_SGU)), _full((HEADS, CHUNK, CHUNK)), _full((HEADS, CHUNK, CHUNK)),
                  _full((CHUNK, D_SGU))],
        out_specs=[rowr(D_IN)] + [_full(s.shape) for s in small],
        out_shape=[S((T, D_IN), BF16)] + small,
        scratch_shapes=[pltpu.VMEM((tt, 2 * D_ST), F32), pltpu.VMEM((SCAN_BLK, 2 * D_ST), F32),
                        pltpu.VMEM((HALO, D_POOL), F32), pltpu.VMEM((tt + HALO, D_POOL), F32),
                        pltpu.VMEM((tt, D_SGU), F32), pltpu.VMEM((tt, D_SGU), F32),
                        pltpu.VMEM((D_SSM, 2 * D_ST), F32), pltpu.VMEM((D_SSM, 2 * D_ST), F32),
                        pltpu.VMEM((HEADS, CHUNK, CHUNK), F32), pltpu.VMEM((CHUNK, D_SGU), F32)],
        vmem_mb=56, exchange=exchange,
        operands=(dx1b, z, z, hs, hs, ys, wout, sc_b, bbd, cbd, dskip, wglu, bglu, wpool, pscale, lng, lnb, wsm, wsmt, bsp))


def inproj_fwd(x, g, w_t, name, exchange=None):
    T = x.shape[0]
    tt = TT_PROJ

    def body(x_ref, g_ref, w_ref, h_ref, z_ref):
        xn, _ = _rms(x_ref[...])
        h = (xn * g_ref[...]).astype(BF16)
        h_ref[...] = h
        z_ref[...] = _nt(h, w_ref[...])

    return _pallas(
        body, name=name, grid=(T // tt,),
        in_specs=[_row(tt, D), _full((1, D)), _full((D_IN, D))],
        out_specs=[_row(tt, D), _row(tt, D_IN)],
        out_shape=[S((T, D), BF16), S((T, D_IN), F32)],
        scratch_shapes=[], vmem_mb=40, operands=(x, g, w_t), exchange=exchange)


def inproj_bwd(dzb, x, g, w_t, dx1):
    T = x.shape[0]
    tt = TT_PROJ

    def body(dz_ref, x_ref, g_ref, w_ref, dx1_ref, dx_ref, dg_ref):
        @pl.when(pl.program_id(0) == 0)
        def _():
            dg_ref[...] = jnp.zeros_like(dg_ref)

        dh = _nn(dz_ref[...], w_ref[...])
        xn, r = _rms(x_ref[...])
        dg_ref[...] += _rowsum(dh * xn)
        dx_ref[...] = dx1_ref[...] + _rms_bwd(dh, xn, r, g_ref[...])

    return pl.pallas_call(
        body, name="inproj_bwd", grid=(T // tt,),
        in_specs=[_row(tt, D_IN), _row(tt, D), _full((1, D)), _full((D_IN, D)), _row(tt, D)],
        out_specs=[_row(tt, D), _full((1, D))],
        out_shape=[S((T, D), F32), S((1, D), F32)],
        compiler_params=_cp(40, 1),
    )(dzb, x, g, w_t, dx1)


def _load_weights(pairs, sem):
    @pl.when(pl.program_id(0) == 0)
    def _():
        copies = [pltpu.make_async_copy(src, dst, sem.at[k]) for k, (src, dst) in enumerate(pairs)]
        for cp in copies:
            cp.start()
        for cp in copies:
            cp.wait()


def ffn_fwd(x, ycat, wout, g, wg_t, wu_t, wd, name, exchange=None):
    T = x.shape[0]
    tt = TT_FFN
    any_spec = pl.BlockSpec(memory_space=pl.ANY)

    def body(x_ref, ycat_ref, g_ref, wout_hbm, wg_hbm, wu_hbm, wd_hbm,
             x1_ref, h_ref, gate_ref, up_ref, act_ref, x2_ref, wout_v, wg_v, wu_v, wd_v, sem):
        _load_weights([(wout_hbm, wout_v), (wg_hbm, wg_v), (wu_hbm, wu_v), (wd_hbm, wd_v)], sem)
        x1 = x_ref[...] + _nn(ycat_ref[...], wout_v[...])
        x1_ref[...] = x1
        xn, _ = _rms(x1)
        h = (xn * g_ref[...]).astype(BF16)
        h_ref[...] = h
        gate = _nt(h, wg_v[...])
        up = _nt(h, wu_v[...])
        gate_ref[...] = gate.astype(BF16)
        up_ref[...] = up.astype(BF16)
        act = (gate * jax.nn.sigmoid(gate) * up).astype(BF16)
        act_ref[...] = act
        x2_ref[...] = x1 + _nn(act, wd_v[...])

    return _pallas(
        body, name=name, grid=(T // tt,),
        in_specs=[_row(tt, D), _row(tt, D), _full((1, D)), any_spec, any_spec, any_spec, any_spec],
        out_specs=[_row(tt, D), _row(tt, D), _row(tt, D_FF), _row(tt, D_FF), _row(tt, D_FF), _row(tt, D)],
        out_shape=[S((T, D), F32), S((T, D), BF16), S((T, D_FF), BF16), S((T, D_FF), BF16), S((T, D_FF), BF16),
                   S((T, D), F32)],
        scratch_shapes=[pltpu.VMEM((D, D), BF16), pltpu.VMEM((D_FF, D), BF16), pltpu.VMEM((D_FF, D), BF16),
                        pltpu.VMEM((D_FF, D), BF16), pltpu.SemaphoreType.DMA((4,))],
        vmem_mb=56, operands=(x, ycat, g, wout, wg_t, wu_t, wd), exchange=exchange)


def ffn_bwd(dx2, x1, gate, up, g, wg_t, wu_t, wd, name, exchange=None):
    T = x1.shape[0]
    tt = TT_FFN
    any_spec = pl.BlockSpec(memory_space=pl.ANY)

    def body(dx2_ref, x1_ref, gate_ref, up_ref, g_ref, wg_hbm, wu_hbm, wd_hbm,
             dgu_ref, dx2b_ref, dx1_ref, dx1b_ref, dg_ref, wg_v, wu_v, wd_v, sem):
        _load_weights([(wg_hbm, wg_v), (wu_hbm, wu_v), (wd_hbm, wd_v)], sem)

        @pl.when(pl.program_id(0) == 0)
        def _():
            dg_ref[...] = jnp.zeros_like(dg_ref)

        dx2 = dx2_ref[...]
        dx2b = dx2.astype(BF16)
        dx2b_ref[...] = dx2b
        dact = _nt(dx2b, wd_v[...])
        gate = gate_ref[...].astype(F32)
        up = up_ref[...].astype(F32)
        sg = jax.nn.sigmoid(gate)
        dgate = (dact * up * (sg * (1.0 + gate * (1.0 - sg)))).astype(BF16)
        dup = (dact * gate * sg).astype(BF16)
        dgu_ref[:, 0:D_FF] = dgate
        dgu_ref[:, D_FF:2 * D_FF] = dup
        dh = _nn(dgate, wg_v[...]) + _nn(dup, wu_v[...])
        xn, r = _rms(x1_ref[...])
        dg_ref[...] += _rowsum(dh * xn)
        dx1 = dx2 + _rms_bwd(dh, xn, r, g_ref[...])
        dx1_ref[...] = dx1
        dx1b_ref[...] = dx1.astype(BF16)

    return _pallas(
        body, name=name, grid=(T // tt,),
        in_specs=[_row(tt, D), _row(tt, D), _row(tt, D_FF), _row(tt, D_FF), _full((1, D)), any_spec, any_spec, any_spec],
        out_specs=[_row(tt, 2 * D_FF), _row(tt, D), _row(tt, D), _row(tt, D), _full((1, D))],
        out_shape=[S((T, 2 * D_FF), BF16), S((T, D), BF16), S((T, D), F32), S((T, D), BF16), S((1, D), F32)],
        scratch_shapes=[pltpu.VMEM((D_FF, D), BF16), pltpu.VMEM((D_FF, D), BF16), pltpu.VMEM((D_FF, D), BF16),
                        pltpu.SemaphoreType.DMA((3,))],
        vmem_mb=56, operands=(dx2, x1, gate, up, g, wg_t, wu_t, wd), exchange=exchange)


def loss_head(x, target, g):
    T = x.shape[0]
    tt = TT_PROJ

    def body(x_ref, t_ref, g_ref, dx_ref, lvec_ref, dg_ref):
        @pl.when(pl.program_id(0) == 0)
        def _():
            lvec_ref[...] = jnp.zeros_like(lvec_ref)
            dg_ref[...] = jnp.zeros_like(dg_ref)

        xn, r = _rms(x_ref[...])
        gg = g_ref[...]
        err = xn * gg - t_ref[...]
        lvec_ref[...] += _rowsum(err * err)
        dy = err * (1.0 / D)
        dg_ref[...] += _rowsum(dy * xn)
        dx_ref[...] = _rms_bwd(dy, xn, r, gg)

    return pl.pallas_call(
        body, name="loss_head", grid=(T // tt,),
        in_specs=[_row(tt, D), _row(tt, D), _full((1, D))],
        out_specs=[_row(tt, D), _full((1, D)), _full((1, D))],
        out_shape=[S((T, D), F32), S((1, D), F32), S((1, D), F32)],
        compiler_params=_cp(32, 1),
    )(x, target, g)


def wgrad(a, b, tm, name):
    T, M = a.shape
    N = b.shape[1]
    tk = min(TK_WGRAD, T)
    n_k = T // tk

    def body(a_ref, b_ref, o_ref, acc_ref):
        k = pl.program_id(1)

        @pl.when(k == 0)
        def _():
            acc_ref[...] = jnp.zeros_like(acc_ref)

        acc_ref[...] += _tn(a_ref[...], b_ref[...])

        @pl.when(k == n_k - 1)
        def _():
            o_ref[...] = acc_ref[...].astype(BF16)

    return pl.pallas_call(
        body, name=name, grid=(M // tm, n_k),
        in_specs=[pl.BlockSpec((tk, tm), lambda m, k: (k, m)), pl.BlockSpec((tk, N), lambda m, k: (k, 0))],
        out_specs=pl.BlockSpec((tm, N), lambda m, k: (m, 0)),
        out_shape=S((M, N), BF16),
        scratch_shapes=[pltpu.VMEM((tm, N), F32)],
        compiler_params=_cp(48, 2),
    )(a, b)


def _mesh_place():
    x, y, c = lax.axis_index("x"), lax.axis_index("y"), lax.axis_index("c")
    return x, y, c, 4 * x + 2 * y + c


def _peer(x, y, c, k):
    px = 1 - x if k & 4 else x
    py = 1 - y if k & 2 else y
    pc = 1 - c if k & 1 else c
    return (px, py, pc), 4 * px + 2 * py + pc


class _Exchange:
    def __init__(self, gather=(), scatter=()):
        self.entries = [(a, None, a.shape[0]) for a in gather] + [(a, off, rows) for a, off, rows in scatter]

    @property
    def n(self):
        return len(self.entries)

    def operands(self):
        return [e[0] for e in self.entries]

    def out_shapes(self):
        return [S((N_DEV, rows, a.shape[1]), a.dtype) for a, _, rows in self.entries]

    def sems(self):
        return [pltpu.SemaphoreType.DMA((self.n, N_DEV)), pltpu.SemaphoreType.DMA((self.n, N_DEV)),
                pltpu.SemaphoreType.DMA((self.n,))]

    def _src(self, ref, e, idx):
        _, off, rows = self.entries[e]
        if off is None:
            return ref
        return ref.at[pl.ds(pl.multiple_of(off + idx * rows, 16), rows)]

    def _copies(self, ins, outs, sems, k, sending):
        send_sems, recv_sems, _ = sems
        x, y, c, me = _mesh_place()
        peer, pidx = _peer(x, y, c, k)
        slot = me if sending else pidx
        return [pltpu.make_async_remote_copy(
            src_ref=self._src(ins[e], e, pidx), dst_ref=outs[e].at[slot], send_sem=send_sems.at[e, k],
            recv_sem=recv_sems.at[e, k], device_id=peer, device_id_type=pl.DeviceIdType.MESH) for e in range(self.n)]

    def _local(self, ins, outs, sems):
        me = _mesh_place()[3]
        return [pltpu.make_async_copy(self._src(ins[e], e, me), outs[e].at[me], sems[2].at[e]) for e in range(self.n)]

    def start(self, ins, outs, sems):
        for cp in self._local(ins, outs, sems):
            cp.start()
        for k in range(1, N_DEV):
            for cp in self._copies(ins, outs, sems, k, True):
                cp.start()

    def wait(self, ins, outs, sems):
        for k in range(1, N_DEV):
            for cp in self._copies(ins, outs, sems, k, False):
                cp.wait_recv()
        for k in range(1, N_DEV):
            for cp in self._copies(ins, outs, sems, k, True):
                cp.wait_send()
        for cp in self._local(ins, outs, sems):
            cp.wait()


def _pallas(body, *, name, grid, in_specs, out_specs, out_shape, scratch_shapes, vmem_mb, operands, exchange=None):
    n_in, n_out, n_scr = len(in_specs), len(out_specs), len(scratch_shapes)
    if exchange is None:
        res = pl.pallas_call(body, name=name, grid=grid, in_specs=in_specs, out_specs=out_specs, out_shape=out_shape,
                             scratch_shapes=scratch_shapes, compiler_params=_cp(vmem_mb, len(grid)))(*operands)
        return list(res), []
    ex = exchange

    def hosted(*refs):
        ins, ex_in = refs[:n_in], refs[n_in:n_in + ex.n]
        outs = refs[n_in + ex.n:n_in + ex.n + n_out]
        ex_out = refs[n_in + ex.n + n_out:n_in + 2 * ex.n + n_out]
        scr = refs[n_in + 2 * ex.n + n_out:]
        sems = scr[n_scr:]
        step = pl.program_id(0)

        @pl.when(step == 0)
        def _():
            ex.start(ex_in, ex_out, sems)

        body(*ins, *outs, *scr[:n_scr])

        @pl.when(step == grid[0] - 1)
        def _():
            ex.wait(ex_in, ex_out, sems)

    any_spec = pl.BlockSpec(memory_space=pl.ANY)
    res = pl.pallas_call(
        hosted, name=name, grid=grid, in_specs=list(in_specs) + [any_spec] * ex.n,
        out_specs=list(out_specs) + [any_spec] * ex.n, out_shape=list(out_shape) + ex.out_shapes(),
        scratch_shapes=list(scratch_shapes) + ex.sems(), compiler_params=_cp(vmem_mb, len(grid)),
    )(*operands, *ex.operands())
    return list(res[:n_out]), list(res[n_out:])


def exchange_only(ex, name):
    def body(*refs):
        ins, outs, sems = refs[:ex.n], refs[ex.n:2 * ex.n], refs[2 * ex.n:]
        ex.start(ins, outs, sems)
        ex.wait(ins, outs, sems)

    any_spec = pl.BlockSpec(memory_space=pl.ANY)
    return list(pl.pallas_call(body, name=name, in_specs=[any_spec] * ex.n, out_specs=[any_spec] * ex.n,
                               out_shape=ex.out_shapes(), scratch_shapes=ex.sems())(*ex.operands()))


def _adamw(w, g, m, v):
    m = ADAM_B1 * m + (1.0 - ADAM_B1) * g
    v = ADAM_B2 * v + (1.0 - ADAM_B2) * (g * g)
    m_hat = m / (1.0 - ADAM_B1 ** ADAM_STEP)
    v_hat = v / (1.0 - ADAM_B2 ** ADAM_STEP)
    delta = -ADAM_LR * (m_hat / (jnp.sqrt(v_hat) + ADAM_EPS) + ADAM_WD * w)
    return delta, m, v


def _sum_parts(p_ref, rows=slice(None)):
    g = p_ref[0, rows].astype(F32)
    for k in range(1, N_DEV):
        g = g + p_ref[k, rows].astype(F32)
    return g


def adamw_layers(parts, w, m, v, name):
    n_l = len(parts)

    def body(*refs):
        p_refs = refs[:n_l]
        w_ref, m_ref, v_ref, g_out, d_out, m_out, v_out = refs[n_l:]
        for l in range(n_l):
            g = _sum_parts(p_refs[l])
            g_out[l] = g
            d_out[l], m_out[l], v_out[l] = _adamw(w_ref[l], g, m_ref[l], v_ref[l])

    return pl.pallas_call(
        body, name=name, out_shape=[S(w.shape, F32)] * 4, compiler_params=_cp(48),
    )(*parts, w, m, v)


def adamw_segments(parts, segments, w, m, v, name):
    n_p = len(parts)

    def body(*refs):
        p_refs = refs[:n_p]
        w_ref, m_ref, v_ref, g_out, d_out, m_out, v_out = refs[n_p:]
        for part, src, dst, rows in segments:
            g = _sum_parts(p_refs[part], slice(src, src + rows))
            to = slice(dst, dst + rows)
            g_out[to] = g
            d_out[to], m_out[to], v_out[to] = _adamw(w_ref[to], g, m_ref[to], v_ref[to])

    return pl.pallas_call(
        body, name=name, out_shape=[S(w.shape, F32)] * 4, compiler_params=_cp(48),
    )(*parts, w, m, v)


SMALL_LAYER = ("g_mix", "A_re", "A_im", "log_dt", "B_re", "B_im", "C_re", "C_im", "D_skip", "b_glu", "w_pool",
               "pool_scale", "sgu_ln_g", "sgu_ln_b", "w_spatial", "b_spatial", "g_ffn")
BIG_NAMES = ("w_in", "w_glu", "w_out", "w_gate", "w_up", "w_down")
COLUMN_SHARDED = ("w_in", "w_gate", "w_up")
WEIGHT_ORDER = ("g_mix", "w_in", "A_re", "A_im", "log_dt", "B_re", "B_im", "C_re", "C_im", "D_skip", "w_glu", "b_glu",
                "w_pool", "pool_scale", "sgu_ln_g", "sgu_ln_b", "w_spatial", "b_spatial", "w_out", "g_ffn", "w_gate",
                "w_up", "w_down", "g_final")
SEG = 1024


def _pack(arrays):
    parts = []
    for a in arrays:
        flat = a.reshape(-1)
        parts.append(jnp.pad(flat, (0, (-flat.shape[0]) % SEG)))
    return jnp.concatenate(parts).reshape(-1, 128)


def _state_rows(p):
    return p.reshape(1, D_ST)


def _chan_by_state(p):
    return jnp.transpose(p, (2, 0, 1)).reshape(GRP, D_ST)


def _chan_by_state_c(p):
    return jnp.transpose(p, (1, 0, 2)).reshape(GRP, D_ST)


def kernel(x, g_mix, w_in, A_re, A_im, log_dt, B_re, B_im, C_re, C_im, D_skip, w_glu, b_glu, w_pool, pool_scale, sgu_ln_g, sgu_ln_b, w_spatial, b_spatial, w_out, g_ffn, w_gate, w_up, w_down, g_final, loss_target, m_g_mix, m_w_in, m_A_re, m_A_im, m_log_dt, m_B_re, m_B_im, m_C_re, m_C_im, m_D_skip, m_w_glu, m_b_glu, m_w_pool, m_pool_scale, m_sgu_ln_g, m_sgu_ln_b, m_w_spatial, m_b_spatial, m_w_out, m_g_ffn, m_w_gate, m_w_up, m_w_down, m_g_final, v_g_mix, v_w_in, v_A_re, v_A_im, v_log_dt, v_B_re, v_B_im, v_C_re, v_C_im, v_D_skip, v_w_glu, v_b_glu, v_w_pool, v_pool_scale, v_sgu_ln_g, v_sgu_ln_b, v_w_spatial, v_b_spatial, v_w_out, v_g_ffn, v_w_gate, v_w_up, v_w_down, v_g_final):
    args = dict(locals())
    W = {n: args[n] for n in WEIGHT_ORDER}
    M = {n: args["m_" + n] for n in WEIGHT_ORDER}
    V = {n: args["v_" + n] for n in WEIGHT_ORDER}
    n_layers = g_mix.shape[0]
    x0 = x[0]
    target = loss_target[0]

    def my_rows(name, l):
        w = W[name][l]
        return (w.T if name in COLUMN_SHARDED else w).astype(BF16)

    full_w = [dict() for _ in range(n_layers)]

    def gather_of(*which):
        return _Exchange(gather=[my_rows(n, l) for n, l in which])

    def keep_gathered(which, arrays):
        for (n, l), a in zip(which, arrays):
            full_w[l][n] = a.reshape(-1, a.shape[-1])

    tri = jnp.tril(jnp.ones((CHUNK, CHUNK), bool))
    consts = []
    for l in range(n_layers):
        a_re, a_im = _state_rows(A_re[l]), _state_rows(A_im[l])
        ldt = jnp.repeat(log_dt[l], N_STATE).reshape(1, D_ST)
        b_re_t, b_im_t = _chan_by_state(B_re[l]), _chan_by_state(B_im[l])
        sc, bbd, cbd = s5_prepare(a_re, a_im, ldt, b_re_t, b_im_t, _chan_by_state_c(C_re[l]), _chan_by_state_c(C_im[l]))
        wsm = jnp.where(tri[None], w_spatial[l], 0.0)
        wpool_bd = jnp.zeros((D_POOL, D_POOL), F32)
        for gi in range(len(POOL_WINDOWS)):
            wpool_bd = wpool_bd.at[gi * 64:(gi + 1) * 64, gi * 64:(gi + 1) * 64].set(w_pool[l, gi])
        consts.append(dict(
            disc=(a_re, a_im, ldt, b_re_t, b_im_t), sc=sc, bbd=bbd, cbd=cbd,
            dskip=D_skip[l].reshape(1, D_SSM), bglu=b_glu[l].reshape(1, D_SSM),
            wpool=wpool_bd.astype(BF16), pscale=pool_scale[l].reshape(1, D_POOL),
            lng=sgu_ln_g[l].reshape(1, D_SGU), lnb=sgu_ln_b[l].reshape(1, D_SGU),
            wsm=wsm.astype(BF16), wsmt=jnp.transpose(wsm, (0, 2, 1)).astype(BF16),
            bsp=jnp.repeat(b_spatial[l].T, HEAD_DIM, axis=1),
            gmix=g_mix[l].reshape(1, D), gffn=g_ffn[l].reshape(1, D)))

    def mixer_args(l):
        c = consts[l]
        return (c["bbd"], c["cbd"], c["dskip"], full_w[l]["w_glu"], c["bglu"], c["wpool"], c["pscale"], c["lng"], c["lnb"])

    first_needed = [("w_in", 0)]
    keep_gathered(first_needed, exchange_only(gather_of(*first_needed), "gather_first"))
    carried_fwd = {
        ("inproj", 0): [("w_glu", 0), ("w_out", 0)],
        ("mixer", 0): [("w_gate", 0), ("w_up", 0), ("w_down", 0)],
        ("ffn", 0): [("w_in", 1), ("w_glu", 1), ("w_out", 1), ("w_gate", 1)],
        ("mixer", 1): [("w_up", 1), ("w_down", 1)],
    }

    def carried(kind, l):
        which = carried_fwd.get((kind, l))
        return which, (gather_of(*which) if which else None)

    saved = []
    xl = x0
    for l in range(n_layers):
        c, fw = consts[l], full_w[l]
        which, ex = carried("inproj", l)
        (h, z), got = inproj_fwd(xl, c["gmix"], fw["w_in"], f"inproj_fwd_{l}", ex)
        keep_gathered(which or [], got)
        which, ex = carried("mixer", l)
        (ycat, hs, ys), got = mixer_fwd(z, c["sc"], *mixer_args(l), c["wsm"], c["bsp"], f"mixer_fwd_{l}", ex)
        keep_gathered(which or [], got)
        which, ex = carried("ffn", l)
        (x1, h2, gate, up, act, x2), got = ffn_fwd(xl, ycat, fw["w_out"], c["gffn"], fw["w_gate"], fw["w_up"], fw["w_down"],
                                                    f"ffn_fwd_{l}", ex)
        keep_gathered(which or [], got)
        saved.append(dict(x=xl, h=h, z=z, ycat=ycat, hs=hs, ys=ys, x1=x1, h2=h2, gate=gate, up=up, act=act))
        xl = x2
    dx, loss_vec, d_gfinal = loss_head(xl, target, g_final.reshape(1, D))

    recv_big = {}
    recv_small = [None] * n_layers

    def keep_received(which, arrays):
        for key, a in zip(which, arrays):
            if key[0] == "small":
                recv_small[key[1]] = a
            else:
                recv_big[key] = a

    pending = None
    for l in reversed(range(n_layers)):
        c, fw, sv = consts[l], full_w[l], saved[l]
        (dgu, dx2b, dx1, dx1b, d_gffn), got = ffn_bwd(dx, sv["x1"], sv["gate"], sv["up"], c["gffn"], fw["w_gate"], fw["w_up"],
                                                     fw["w_down"], f"ffn_bwd_{l}", pending[1] if pending else None)
        if pending:
            keep_received(pending[0], got)
        g_gu = wgrad(dgu, sv["h2"], D_FF // 2, f"wgrad_gate_up_{l}")
        g_down = wgrad(sv["act"], dx2b, D_FF // 2, f"wgrad_down_{l}")
        g_out = wgrad(sv["ycat"], dx1b, D, f"wgrad_out_{l}")
        ffn_rows = D_FF // N_DEV
        ex = _Exchange(scatter=[(g_gu, 0, ffn_rows), (g_gu, D_FF, ffn_rows), (g_down, 0, ffn_rows), (g_out, 0, D // N_DEV)])
        (dzb, da, dbt, dct, dd, dbglu, dwglu, dwpool, dpscale, dlng, dlnb, dws, dbsp), got = mixer_bwd(
            dx1b, sv["z"], sv["hs"], sv["ys"], fw["w_out"], c["sc"], *mixer_args(l), c["wsm"], c["wsmt"], c["bsp"],
            f"mixer_bwd_{l}", ex)
        keep_received([("w_gate", l), ("w_up", l), ("w_down", l), ("w_out", l)], got)
        dx, d_gmix = inproj_bwd(dzb, sv["x"], c["gmix"], fw["w_in"], dx1)
        g_in = wgrad(dzb, sv["h"], D_IN, f"wgrad_in_{l}")
        d_are, d_aim, d_ldt, d_bre_t, d_bim_t = s5_param_bwd(*c["disc"], da, dbt)
        small = dict(
            g_mix=d_gmix.reshape(D), A_re=d_are.reshape(N_GRP, N_STATE), A_im=d_aim.reshape(N_GRP, N_STATE),
            log_dt=d_ldt[0, :N_GRP],
            B_re=jnp.transpose(d_bre_t.reshape(GRP, N_GRP, N_STATE), (1, 2, 0)),
            B_im=jnp.transpose(d_bim_t.reshape(GRP, N_GRP, N_STATE), (1, 2, 0)),
            C_re=jnp.transpose(dct[:, :D_ST].reshape(GRP, N_GRP, N_STATE), (1, 0, 2)),
            C_im=-jnp.transpose(dct[:, D_ST:].reshape(GRP, N_GRP, N_STATE), (1, 0, 2)),
            D_skip=dd.reshape(D_SSM), b_glu=dbglu.reshape(D_SSM),
            w_pool=jnp.stack([dwpool[gi * 64:(gi + 1) * 64, gi * 64:(gi + 1) * 64] for gi in range(len(POOL_WINDOWS))]),
            pool_scale=dpscale.reshape(D_POOL), sgu_ln_g=dlng.reshape(D_SGU), sgu_ln_b=dlnb.reshape(D_SGU),
            w_spatial=dws, b_spatial=dbsp[:, :HEADS].T, g_ffn=d_gffn.reshape(D))
        packed = [small[n] for n in SMALL_LAYER] + ([d_gfinal.reshape(D), loss_vec.reshape(D)] if l == 0 else [])
        pending = ([("small", l), ("w_in", l), ("w_glu", l)],
                   _Exchange(gather=[_pack(packed)],
                             scatter=[(g_in, 0, D_IN // N_DEV), (dwglu.astype(BF16), 0, D_SSM // N_DEV)]))
    grad_x = dx
    keep_received(pending[0], exchange_only(pending[1], "exchange_last"))

    out = {}
    for n in BIG_NAMES:
        tr = (lambda a: jnp.transpose(a, (0, 2, 1))) if n in COLUMN_SHARDED else (lambda a: a)
        res = adamw_layers([recv_big[(n, l)] for l in range(n_layers)], tr(W[n]), tr(M[n]), tr(V[n]), f"adamw_{n}")
        out[n] = [tr(r) for r in res]

    seg_rows = [(-(-math.prod(W[n].shape[1:]) // SEG)) * (SEG // 128) for n in SMALL_LAYER]
    segments, src, dst = [], 0, 0
    for rows in seg_rows:
        segments += [(l, src, dst + l * rows, rows) for l in range(n_layers)]
        src += rows
        dst += n_layers * rows
    tile_rows = SEG // 128
    segments += [(0, src, dst, tile_rows), (0, src + tile_rows, dst + tile_rows, tile_rows)]

    def pack_params(P):
        parts = []
        for n, rows in zip(SMALL_LAYER, seg_rows):
            flat = P[n].reshape(n_layers, -1)
            parts.append(jnp.pad(flat, ((0, 0), (0, rows * 128 - flat.shape[1]))).reshape(-1))
        return jnp.concatenate(parts + [P["g_final"], jnp.zeros((SEG,), F32)]).reshape(-1, 128)

    res = adamw_segments(recv_small, segments, pack_params(W), pack_params(M), pack_params(V), "adamw_small")
    for j in range(4):
        flat, off = res[j].reshape(-1), 0
        for n, rows in zip(SMALL_LAYER, seg_rows):
            size = math.prod(W[n].shape[1:])
            piece = flat[off:off + n_layers * rows * 128].reshape(n_layers, rows * 128)[:, :size].reshape(W[n].shape)
            out.setdefault(n, []).append(piece)
            off += n_layers * rows * 128
        out.setdefault("g_final", []).append(flat[off:off + D])
        if j == 0:
            loss = (0.5 / D) * jnp.sum(flat[off + SEG:off + SEG + D])

    return (loss, grad_x[None], *[out[n][0] for n in WEIGHT_ORDER], *[out[n][1] for n in WEIGHT_ORDER],
            *[out[n][2] for n in WEIGHT_ORDER], *[out[n][3] for n in WEIGHT_ORDER])
```

```python
import functools
import math

import jax
import jax.numpy as jnp
from jax import lax
from jax.experimental import pallas as pl
from jax.experimental.pallas import tpu as pltpu

F32 = jnp.float32
BF16 = jnp.bfloat16
S = jax.ShapeDtypeStruct

N_DEV = 8
D = 1024
D_SSM = 384
N_GRP = 24
GRP = 16
N_STATE = 64
D_ST = N_GRP * N_STATE
D_POOL = 256
POOL_WINDOWS = (2, 4, 8, 16)
HALO = 16
D_SGU = 384
HEADS = 6
HEAD_DIM = 64
CHUNK = 128
D_IN = 1408
D_FF = 2816
EPS = 1e-6
SCAN_BLK = 8

ADAM_LR = 0.001
ADAM_B1 = 0.9
ADAM_B2 = 0.999
ADAM_EPS = 1e-08
ADAM_WD = 0.01
ADAM_STEP = 10

GELU_C0 = math.sqrt(2.0 / math.pi)
GELU_C1 = 0.044715

TT_MIX = 256
SEG_LEN = TT_MIX // SCAN_BLK
TT_FFN = 256
TT_PROJ = 512
TK_WGRAD = 512
VMEM_MB = 2 ** 20


def _cp(vmem_mb, grid_dims=0):
    kw = dict(vmem_limit_bytes=int(vmem_mb * VMEM_MB))
    if grid_dims:
        kw["dimension_semantics"] = ("arbitrary",) * grid_dims
    return pltpu.CompilerParams(**kw)


def _row(tt, n):
    return pl.BlockSpec((tt, n), lambda i: (i, 0))


def _full(shape):
    nd = len(shape)
    return pl.BlockSpec(shape, lambda *_: (0,) * nd)


def _nn(a, b):
    return jnp.dot(a, b, preferred_element_type=F32)


def _nt(a, b):
    return lax.dot_general(a, b, (((1,), (1,)), ((), ())), preferred_element_type=F32)


def _tn(a, b):
    return lax.dot_general(a, b, (((0,), (0,)), ((), ())), preferred_element_type=F32)


def _rowsum(x):
    return jnp.sum(x, axis=0, keepdims=True)


def _rms(x):
    r = lax.rsqrt(jnp.mean(x * x, axis=-1, keepdims=True) + EPS)
    return x * r, r


def _rms_bwd(dy, xn, r, g):
    dyg = dy * g
    return r * (dyg - xn * jnp.mean(dyg * xn, axis=-1, keepdims=True))


def _gelu(x):
    t = jnp.tanh(GELU_C0 * (x + GELU_C1 * x * x * x))
    return 0.5 * x * (1.0 + t), t


def _gelu_grad(x, t):
    return 0.5 * (1.0 + t) + 0.5 * x * (1.0 - t * t) * (GELU_C0 * (1.0 + 3.0 * GELU_C1 * x * x))


def _discretise(a_re, a_im, ldt, b_re, b_im):
    dt = jnp.exp(ldt)
    mag = jnp.exp(a_re * dt)
    ar = mag * jnp.cos(a_im * dt)
    ai = mag * jnp.sin(a_im * dt)
    den = a_re * a_re + a_im * a_im
    f_re = ((ar - 1.0) * a_re + ai * a_im) / den
    f_im = (ai * a_re - (ar - 1.0) * a_im) / den
    bb_re = f_re * b_re - f_im * b_im
    bb_im = f_re * b_im + f_im * b_re
    return ar, ai, bb_re, bb_im


def _group_mask(rows, cols):
    r = lax.broadcasted_iota(jnp.int32, (rows, cols), 0) // GRP
    c = lax.broadcasted_iota(jnp.int32, (rows, cols), 1)
    c = jnp.where(c >= D_ST, c - D_ST, c) // N_STATE
    return r == c


def s5_prepare(a_re, a_im, ldt, b_re_t, b_im_t, c_re_t, c_im_t):
    def body(are_ref, aim_ref, ldt_ref, bre_ref, bim_ref, cre_ref, cim_ref, sc_ref, bbd_ref, cbd_ref):
        ar, ai, bb_re, bb_im = _discretise(are_ref[...], aim_ref[...], ldt_ref[...], bre_ref[...], bim_ref[...])
        mask = _group_mask(D_SSM, 2 * D_ST)
        bb = jnp.concatenate([jnp.tile(bb_re, (N_GRP, 1)), jnp.tile(bb_im, (N_GRP, 1))], axis=1)
        bbd_ref[...] = jnp.where(mask, bb, 0.0).astype(BF16)
        cc = jnp.concatenate([jnp.tile(cre_ref[...], (N_GRP, 1)), -jnp.tile(cim_ref[...], (N_GRP, 1))], axis=1)
        cbd_ref[...] = jnp.where(mask, cc, 0.0).astype(BF16)
        pr, pi = ar, ai
        for _ in range(SEG_LEN - 1):
            pr, pi = pr * ar - pi * ai, pr * ai + pi * ar
        for k, v in enumerate((ar, ai, pr, pi)):
            sc_ref[8 * k:8 * k + 8, :] = jnp.broadcast_to(v, (SCAN_BLK, D_ST))

    return pl.pallas_call(
        body, name="s5_prepare",
        out_shape=[S((32, D_ST), F32), S((D_SSM, 2 * D_ST), BF16), S((D_SSM, 2 * D_ST), BF16)],
        compiler_params=_cp(40),
    )(a_re, a_im, ldt, b_re_t, b_im_t, c_re_t, c_im_t)


def s5_param_bwd(a_re, a_im, ldt, b_re_t, b_im_t, da, dbt):
    def body(are_ref, aim_ref, ldt_ref, bre_ref, bim_ref, da_ref, dbt_ref, o_are, o_aim, o_ldt, o_bre, o_bim):
        _, vjp = jax.vjp(_discretise, are_ref[...], aim_ref[...], ldt_ref[...], bre_ref[...], bim_ref[...])
        da = da_ref[...]
        dbt = dbt_ref[...]
        g_are, g_aim, g_ldt, g_bre, g_bim = vjp((da[:, :D_ST], da[:, D_ST:], dbt[:, :D_ST], dbt[:, D_ST:]))
        o_are[...] = g_are
        o_aim[...] = g_aim
        o_bre[...] = g_bre
        o_bim[...] = g_bim
        grp = lax.broadcasted_iota(jnp.int32, (1, D_ST), 1) // N_STATE
        lane = lax.broadcasted_iota(jnp.int32, (1, 128), 1)
        out = jnp.zeros((1, 128), F32)
        for g in range(N_GRP):
            out = jnp.where(lane == g, jnp.sum(jnp.where(grp == g, g_ldt, 0.0), axis=1, keepdims=True), out)
        o_ldt[...] = out

    return pl.pallas_call(
        body, name="s5_param_bwd",
        out_shape=[S((1, D_ST), F32), S((1, D_ST), F32), S((1, 128), F32), S((GRP, D_ST), F32), S((GRP, D_ST), F32)],
        compiler_params=_cp(16),
    )(a_re, a_im, ldt, b_re_t, b_im_t, da, dbt)


_CH = ((0, 256), (256, D_SSM))
_ST = ((0, 1024), (1024, D_ST))


def _bd_expand(xb, w_ref, out_ref):
    for (c0, c1), (s0, s1) in zip(_CH, _ST):
        for half in (0, D_ST):
            out_ref[:, half + s0:half + s1] = _nn(xb[:, c0:c1], w_ref[c0:c1, half + s0:half + s1])


def _bd_contract(hb, w_ref):
    parts = []
    for (c0, c1), (s0, s1) in zip(_CH, _ST):
        parts.append(_nt(hb[:, s0:s1], w_ref[c0:c1, s0:s1]) + _nt(hb[:, D_ST + s0:D_ST + s1], w_ref[c0:c1, D_ST + s0:D_ST + s1]))
    return jnp.concatenate(parts, axis=1)


def _bd_accumulate(acc_ref, xb, hb):
    for (c0, c1), (s0, s1) in zip(_CH, _ST):
        for half in (0, D_ST):
            acc_ref[c0:c1, half + s0:half + s1] += _tn(xb[:, c0:c1], hb[:, half + s0:half + s1])


def _interleave_matrices(tt):
    r = lax.broadcasted_iota(jnp.int32, (tt, tt), 0)
    t = lax.broadcasted_iota(jnp.int32, (tt, tt), 1)
    seg = tt // SCAN_BLK
    p = (t == (r % SCAN_BLK) * seg + r // SCAN_BLK)
    pt = (r == (t % SCAN_BLK) * seg + t // SCAN_BLK)
    return p.astype(BF16), pt.astype(BF16)


def _interleave_f32(p, x):
    hi = x.astype(BF16)
    lo = (x - hi.astype(F32)).astype(BF16)
    return _nn(p, hi) + _nn(p, lo)


def _scan_tile(buf_ref, sc_ref, carry_ref, n_blk, reverse):
    ar = sc_ref[0:8, :]
    ai = -sc_ref[8:16, :] if reverse else sc_ref[8:16, :]

    def rows(i):
        blk = (n_blk - 1 - i) if reverse else i
        return pl.ds(pl.multiple_of(blk * SCAN_BLK, SCAN_BLK), SCAN_BLK)

    def local(i, x):
        xr, xi = x
        r = rows(i)
        xr, xi = buf_ref[r, 0:D_ST] + ar * xr - ai * xi, buf_ref[r, D_ST:2 * D_ST] + ar * xi + ai * xr
        buf_ref[r, 0:D_ST] = xr
        buf_ref[r, D_ST:2 * D_ST] = xi
        return xr, xi

    zero = jnp.zeros((SCAN_BLK, D_ST), F32)
    end_r, end_i = lax.fori_loop(0, n_blk, local, (zero, zero), unroll=2)

    seg_r = sc_ref[16:17, :]
    seg_i = -sc_ref[24:25, :] if reverse else sc_ref[24:25, :]
    cr, ci = carry_ref[0:1, 0:D_ST], carry_ref[0:1, D_ST:2 * D_ST]
    sub = lax.broadcasted_iota(jnp.int32, (SCAN_BLK, D_ST), 0)
    in_r, in_i = zero, zero
    for s in (reversed(range(SCAN_BLK)) if reverse else range(SCAN_BLK)):
        in_r = jnp.where(sub == s, cr, in_r)
        in_i = jnp.where(sub == s, ci, in_i)
        cr, ci = end_r[s:s + 1, :] + seg_r * cr - seg_i * ci, end_i[s:s + 1, :] + seg_r * ci + seg_i * cr
    carry_ref[0:1, 0:D_ST] = cr
    carry_ref[0:1, D_ST:2 * D_ST] = ci

    def fix(i, d):
        dr, di = d
        dr, di = ar * dr - ai * di, ar * di + ai * dr
        r = rows(i)
        buf_ref[r, 0:D_ST] += dr
        buf_ref[r, D_ST:2 * D_ST] += di
        return dr, di

    lax.fori_loop(0, n_blk, fix, (in_r, in_i), unroll=2)


def _lane_windows(n):
    lane = lax.broadcasted_iota(jnp.int32, (1, n), 1)
    return lane // (D_POOL // len(POOL_WINDOWS))


def _select_window(grp, s2, s4, s8, s16):
    return jnp.where(grp == 0, s2, jnp.where(grp == 1, s4, jnp.where(grp == 2, s8, s16)))


def _pool_fwd(pbuf_ref, zb, halo, tile_idx, tt):
    pbuf_ref[0:HALO, :] = halo
    pbuf_ref[HALO:HALO + tt, :] = zb
    x = pbuf_ref[...]
    s2 = x + pltpu.roll(x, 1, axis=0)
    s4 = s2 + pltpu.roll(s2, 2, axis=0)
    s8 = s4 + pltpu.roll(s4, 4, axis=0)
    s16 = s8 + pltpu.roll(s8, 8, axis=0)
    grp = _lane_windows(D_POOL)
    win = _select_window(grp, s2, s4, s8, s16)[HALO:HALO + tt, :]
    width = _select_window(grp, 2.0, 4.0, 8.0, 16.0).astype(F32)
    pos = (tile_idx * tt + 1 + lax.broadcasted_iota(jnp.int32, (tt, 1), 0)).astype(F32)
    cnt = jnp.minimum(pos, width)
    return win / cnt - zb, cnt


def _sgu_fwd(zu, zv, lng, lnb, wsm_ref, bsp, mix_ref, tt):
    u, tu = _gelu(zu)
    v, tv = _gelu(zv)
    mu = jnp.mean(v, axis=-1, keepdims=True)
    vc = v - mu
    rstd = lax.rsqrt(jnp.mean(vc * vc, axis=-1, keepdims=True) + EPS)
    vhat = vc * rstd
    vnb = (vhat * lng + lnb).astype(BF16)
    for c in range(tt // CHUNK):
        rows = slice(c * CHUNK, (c + 1) * CHUNK)
        parts = [_nn(wsm_ref[h], vnb[rows, h * HEAD_DIM:(h + 1) * HEAD_DIM]) for h in range(HEADS)]
        mix_ref[rows, :] = jnp.concatenate(parts, axis=1) + bsp
    return u, tu, tv, vhat, rstd, vnb


def mixer_fwd(z, sc_f, bbd, cbd, dskip, wglu, bglu, wpool, pscale, lng, lnb, wsm, bsp, name, exchange=None):
    T = z.shape[0]
    tt = TT_MIX
    n_tiles = T // tt

    def body(z_ref, scf_ref, bbd_ref, cbd_ref, dskip_ref, wglu_ref, bglu_ref, wpool_ref, pscale_ref, lng_ref, lnb_ref,
             wsm_ref, bsp_ref, ycat_ref, hs_ref, ys_ref, carry_ref, halo_ref, pbuf_ref, mix_ref):
        i = pl.program_id(0)

        @pl.when(i == 0)
        def _():
            carry_ref[...] = jnp.zeros_like(carry_ref)
            halo_ref[...] = jnp.zeros_like(halo_ref)

        za = z_ref[:, 0:D_SSM]
        zb = z_ref[:, D_SSM:D_SSM + D_POOL]
        zu = z_ref[:, D_SSM + D_POOL:D_SSM + D_POOL + D_SGU]
        zv = z_ref[:, D_SSM + D_POOL + D_SGU:D_IN]
        p, pt = _interleave_matrices(tt)
        za = _interleave_f32(p, za)
        _bd_expand(za.astype(BF16), bbd_ref, hs_ref)
        _scan_tile(hs_ref, scf_ref, carry_ref, tt // SCAN_BLK, reverse=False)
        y = _bd_contract(hs_ref[...].astype(BF16), cbd_ref) + dskip_ref[...] * za
        ys_ref[...] = y
        g, _ = _gelu(y)
        q = _nn(g.astype(BF16), wglu_ref[...]) + bglu_ref[...]
        ycat_ref[:, 0:D_SSM] = _nn(pt, (g * jax.nn.sigmoid(q)).astype(BF16)).astype(BF16)
        pooled, _ = _pool_fwd(pbuf_ref, zb, halo_ref[...], i, tt)
        halo_ref[...] = zb[tt - HALO:tt, :]
        ycat_ref[:, D_SSM:D_SSM + D_POOL] = (_nn(pooled.astype(BF16), wpool_ref[...]) * pscale_ref[...]).astype(BF16)
        u, _, _, _, _, _ = _sgu_fwd(zu, zv, lng_ref[...], lnb_ref[...], wsm_ref, bsp_ref[...], mix_ref, tt)
        ycat_ref[:, D_SSM + D_POOL:D] = (u * mix_ref[...]).astype(BF16)

    return _pallas(
        body, name=name, grid=(n_tiles,),
        in_specs=[_row(tt, D_IN), _full((32, D_ST)), _full((D_SSM, 2 * D_ST)), _full((D_SSM, 2 * D_ST)),
                  _full((1, D_SSM)), _full((D_SSM, D_SSM)), _full((1, D_SSM)), _full((D_POOL, D_POOL)),
                  _full((1, D_POOL)), _full((1, D_SGU)), _full((1, D_SGU)), _full((HEADS, CHUNK, CHUNK)),
                  _full((CHUNK, D_SGU))],
        out_specs=[_row(tt, D), _row(tt, 2 * D_ST), _row(tt, D_SSM)],
        out_shape=[S((T, D), BF16), S((T, 2 * D_ST), F32), S((T, D_SSM), F32)],
        scratch_shapes=[pltpu.VMEM((SCAN_BLK, 2 * D_ST), F32), pltpu.VMEM((HALO, D_POOL), F32),
                        pltpu.VMEM((tt + HALO, D_POOL), F32), pltpu.VMEM((tt, D_SGU), F32)],
        vmem_mb=48, operands=(z, sc_f, bbd, cbd, dskip, wglu, bglu, wpool, pscale, lng, lnb, wsm, bsp),
        exchange=exchange)


def mixer_bwd(dx1b, z, hs, ys, wout, sc_b, bbd, cbd, dskip, wglu, bglu, wpool, pscale, lng, lnb, wsm, wsmt, bsp, name,
              exchange=None):
    T = z.shape[0]
    tt = TT_MIX
    n_tiles = T // tt

    def rev(i):
        return n_tiles - 1 - i

    def body(dx_ref, z_ref, zprev_ref, hs_ref, hsprev_ref, ys_ref, wout_ref, sc_ref, bbd_ref, cbd_ref, dskip_ref,
             wglu_ref, bglu_ref, wpool_ref, pscale_ref, lng_ref, lnb_ref, wsm_ref, wsmt_ref, bsp_ref,
             dz_ref, o_da, o_dbt, o_dct, o_dd, o_dbglu, o_dwglu, o_dwpool, o_dpscale, o_dlng, o_dlnb, o_dws, o_dbsp,
             gbuf_ref, carry_ref, ehalo_ref, pbuf_ref, mix_ref, dvn_ref, accb_ref, accc_ref, accw_ref, accm_ref):
        i = pl.program_id(0)
        tile = rev(i)

        @pl.when(i == 0)
        def _():
            carry_ref[...] = jnp.zeros_like(carry_ref)
            ehalo_ref[...] = jnp.zeros_like(ehalo_ref)
            accb_ref[...] = jnp.zeros_like(accb_ref)
            accc_ref[...] = jnp.zeros_like(accc_ref)
            accw_ref[...] = jnp.zeros_like(accw_ref)
            accm_ref[...] = jnp.zeros_like(accm_ref)
            for o in (o_da, o_dd, o_dbglu, o_dwglu, o_dwpool, o_dpscale, o_dlng, o_dlnb):
                o[...] = jnp.zeros_like(o)

        p, pt = _interleave_matrices(tt)
        dxb = dx_ref[...]
        d_a = _nt(_nn(p, dxb).astype(BF16), wout_ref[0:D_SSM, :])
        d_bc = _nt(dxb, wout_ref[D_SSM:D, :])
        d_b = d_bc[:, 0:D_POOL]
        d_c = d_bc[:, D_POOL:D_POOL + D_SGU]
        za = _interleave_f32(p, z_ref[:, 0:D_SSM])
        zb = z_ref[:, D_SSM:D_SSM + D_POOL]
        zu = z_ref[:, D_SSM + D_POOL:D_SSM + D_POOL + D_SGU]
        zv = z_ref[:, D_SSM + D_POOL + D_SGU:D_IN]
        first_tile = (tile > 0).astype(F32)

        y = ys_ref[...]
        g, tg = _gelu(y)
        gb = g.astype(BF16)
        sg = jax.nn.sigmoid(_nn(gb, wglu_ref[...]) + bglu_ref[...])
        dq = d_a * g * sg * (1.0 - sg)
        dqb = dq.astype(BF16)
        o_dbglu[...] += _rowsum(dq)
        o_dwglu[...] += _tn(gb, dqb)
        dy = (d_a * sg + _nt(dqb, wglu_ref[...])) * _gelu_grad(y, tg)
        o_dd[...] += _rowsum(dy * za)
        dyb = dy.astype(BF16)
        _bd_accumulate(accc_ref, dyb, hs_ref[...].astype(BF16))
        _bd_expand(dyb, cbd_ref, gbuf_ref)
        _scan_tile(gbuf_ref, sc_ref, carry_ref, tt // SCAN_BLK, reverse=True)
        hprev = hsprev_ref[SCAN_BLK - 1:SCAN_BLK, :] * first_tile
        sub = lax.broadcasted_iota(jnp.int32, (SCAN_BLK, 1), 0)
        edge = jnp.where(sub == 0, hprev, pltpu.roll(hs_ref[tt - SCAN_BLK:tt, :], 1, axis=0))

        def da_terms(gr, gi, hr, hi):
            return _rowsum(gr * hr + gi * hi), _rowsum(gi * hr - gr * hi)

        body_re, body_im = da_terms(gbuf_ref[SCAN_BLK:tt, 0:D_ST], gbuf_ref[SCAN_BLK:tt, D_ST:],
                                    hs_ref[0:tt - SCAN_BLK, 0:D_ST], hs_ref[0:tt - SCAN_BLK, D_ST:])
        edge_re, edge_im = da_terms(gbuf_ref[0:SCAN_BLK, 0:D_ST], gbuf_ref[0:SCAN_BLK, D_ST:], edge[:, 0:D_ST], edge[:, D_ST:])
        o_da[:, 0:D_ST] += body_re + edge_re
        o_da[:, D_ST:] += body_im + edge_im
        gtb = gbuf_ref[...].astype(BF16)
        _bd_accumulate(accb_ref, za.astype(BF16), gtb)
        dza = (dy * dskip_ref[...] + _bd_contract(gtb, bbd_ref)).astype(BF16)
        dz_ref[:, 0:D_SSM] = _nn(pt, dza).astype(BF16)

        pooled, cnt = _pool_fwd(pbuf_ref, zb, zprev_ref[:, D_SSM:D_SSM + D_POOL] * first_tile, tile, tt)
        pooledb = pooled.astype(BF16)
        mixed = _nn(pooledb, wpool_ref[...])
        o_dpscale[...] += _rowsum(d_b * mixed)
        dmixb = (d_b * pscale_ref[...]).astype(BF16)
        o_dwpool[...] += _tn(pooledb, dmixb)
        dpooled = _nt(dmixb, wpool_ref[...])
        e = dpooled / cnt
        pbuf_ref[0:tt, :] = e
        pbuf_ref[tt:tt + HALO, :] = ehalo_ref[...]
        ehalo_ref[...] = e[0:HALO, :]
        x = pbuf_ref[...]
        n = tt + HALO
        f2 = x + pltpu.roll(x, n - 1, axis=0)
        f4 = f2 + pltpu.roll(f2, n - 2, axis=0)
        f8 = f4 + pltpu.roll(f4, n - 4, axis=0)
        f16 = f8 + pltpu.roll(f8, n - 8, axis=0)
        fwd_sum = _select_window(_lane_windows(D_POOL), f2, f4, f8, f16)[0:tt, :]
        dz_ref[:, D_SSM:D_SSM + D_POOL] = (fwd_sum - dpooled).astype(BF16)

        lng = lng_ref[...]
        u, tu, tv, vhat, rstd, vnb = _sgu_fwd(zu, zv, lng, lnb_ref[...], wsm_ref, bsp_ref[...], mix_ref, tt)
        dz_ref[:, D_SSM + D_POOL:D_SSM + D_POOL + D_SGU] = (d_c * mix_ref[...] * _gelu_grad(zu, tu)).astype(BF16)
        dmix = d_c * u
        dmixb2 = dmix.astype(BF16)
        for c in range(tt // CHUNK):
            rows = slice(c * CHUNK, (c + 1) * CHUNK)
            accm_ref[...] += dmix[rows, :]
            parts = []
            for h in range(HEADS):
                cols = slice(h * HEAD_DIM, (h + 1) * HEAD_DIM)
                accw_ref[h] += _nt(dmixb2[rows, cols], vnb[rows, cols])
                parts.append(_nn(wsmt_ref[h], dmixb2[rows, cols]))
            dvn_ref[rows, :] = jnp.concatenate(parts, axis=1)
        dvn = dvn_ref[...]
        o_dlng[...] += _rowsum(dvn * vhat)
        o_dlnb[...] += _rowsum(dvn)
        dvh = dvn * lng
        dv = rstd * (dvh - jnp.mean(dvh, axis=-1, keepdims=True) - vhat * jnp.mean(dvh * vhat, axis=-1, keepdims=True))
        dz_ref[:, D_SSM + D_POOL + D_SGU:D_IN] = (dv * _gelu_grad(zv, tv)).astype(BF16)

        @pl.when(i == n_tiles - 1)
        def _():
            mask = _group_mask(D_SSM, 2 * D_ST)
            fb = jnp.zeros((GRP, 2 * D_ST), F32)
            fc = jnp.zeros((GRP, 2 * D_ST), F32)
            for gidx in range(N_GRP):
                rows = slice(gidx * GRP, (gidx + 1) * GRP)
                fb = fb + jnp.where(mask[rows, :], accb_ref[rows, :], 0.0)
                fc = fc + jnp.where(mask[rows, :], accc_ref[rows, :], 0.0)
            o_dbt[...] = fb
            o_dct[...] = fc
            tri = (lax.broadcasted_iota(jnp.int32, (CHUNK, CHUNK), 0) >= lax.broadcasted_iota(jnp.int32, (CHUNK, CHUNK), 1))
            for h in range(HEADS):
                o_dws[h] = jnp.where(tri, accw_ref[h], 0.0)
            lane = lax.broadcasted_iota(jnp.int32, (1, 128), 1)
            acc = jnp.zeros((CHUNK, 128), F32)
            for h in range(HEADS):
                sh = jnp.sum(accm_ref[:, h * HEAD_DIM:(h + 1) * HEAD_DIM], axis=1, keepdims=True)
                acc = jnp.where(lane == h, sh, acc)
            o_dbsp[...] = acc

    def rowr(n):
        return pl.BlockSpec((tt, n), lambda i: (rev(i), 0))

    zprev_spec = pl.BlockSpec((HALO, D_IN), lambda i: (jnp.maximum(rev(i) * (tt // HALO) - 1, 0), 0))
    hsprev_spec = pl.BlockSpec((SCAN_BLK, 2 * D_ST), lambda i: (jnp.maximum(rev(i) * (tt // SCAN_BLK) - 1, 0), 0))
    small = [S((1, 2 * D_ST), F32), S((GRP, 2 * D_ST), F32), S((GRP, 2 * D_ST), F32), S((1, D_SSM), F32),
             S((1, D_SSM), F32), S((D_SSM, D_SSM), F32), S((D_POOL, D_POOL), F32), S((1, D_POOL), F32),
             S((1, D_SGU), F32), S((1, D_SGU), F32), S((HEADS, CHUNK, CHUNK), F32), S((CHUNK, 128), F32)]
    return _pallas(
        body, name=name, grid=(n_tiles,),
        in_specs=[rowr(D), rowr(D_IN), zprev_spec, rowr(2 * D_ST), hsprev_spec, rowr(D_SSM), _full((D, D)),
                  _full((32, D_ST)), _full((D_SSM, 2 * D_ST)), _full((D_SSM, 2 * D_ST)), _full((1, D_SSM)),
                  _full((D_SSM, D_SSM)), _full((1, D_SSM)), _full((D_POOL, D_POOL)), _full((1, D_POOL)),
                  _full((1, D_SGU)), _full((1, D_SGU)), _full((HEADS, CHUNK, CHUNK)), _full((HEADS, CHUNK, CHUNK)),
                  _full((CHUNK, D_SGU))],
        out_specs=[rowr(D_IN)] + [_full(s.shape) for s in small],
        out_shape=[S((T, D_IN), BF16)] + small,
        scratch_shapes=[pltpu.VMEM((tt, 2 * D_ST), F32), pltpu.VMEM((SCAN_BLK, 2 * D_ST), F32),
                        pltpu.VMEM((HALO, D_POOL), F32), pltpu.VMEM((tt + HALO, D_POOL), F32),
                        pltpu.VMEM((tt, D_SGU), F32), pltpu.VMEM((tt, D_SGU), F32),
                        pltpu.VMEM((D_SSM, 2 * D_ST), F32), pltpu.VMEM((D_SSM, 2 * D_ST), F32),
                        pltpu.VMEM((HEADS, CHUNK, CHUNK), F32), pltpu.VMEM((CHUNK, D_SGU), F32)],
        vmem_mb=56, exchange=exchange,
        operands=(dx1b, z, z, hs, hs, ys, wout, sc_b, bbd, cbd, dskip, wglu, bglu, wpool, pscale, lng, lnb, wsm, wsmt, bsp))


def inproj_fwd(x, g, w_t, name, exchange=None):
    T = x.shape[0]
    tt = TT_PROJ

    def body(x_ref, g_ref, w_ref, h_ref, z_ref):
        xn, _ = _rms(x_ref[...])
        h = (xn * g_ref[...]).astype(BF16)
        h_ref[...] = h
        z_ref[...] = _nt(h, w_ref[...])

    return _pallas(
        body, name=name, grid=(T // tt,),
        in_specs=[_row(tt, D), _full((1, D)), _full((D_IN, D))],
        out_specs=[_row(tt, D), _row(tt, D_IN)],
        out_shape=[S((T, D), BF16), S((T, D_IN), F32)],
        scratch_shapes=[], vmem_mb=40, operands=(x, g, w_t), exchange=exchange)


def inproj_bwd(dzb, x, g, w_t, dx1):
    T = x.shape[0]
    tt = TT_PROJ

    def body(dz_ref, x_ref, g_ref, w_ref, dx1_ref, dx_ref, dg_ref):
        @pl.when(pl.program_id(0) == 0)
        def _():
            dg_ref[...] = jnp.zeros_like(dg_ref)

        dh = _nn(dz_ref[...], w_ref[...])
        xn, r = _rms(x_ref[...])
        dg_ref[...] += _rowsum(dh * xn)
        dx_ref[...] = dx1_ref[...] + _rms_bwd(dh, xn, r, g_ref[...])

    return pl.pallas_call(
        body, name="inproj_bwd", grid=(T // tt,),
        in_specs=[_row(tt, D_IN), _row(tt, D), _full((1, D)), _full((D_IN, D)), _row(tt, D)],
        out_specs=[_row(tt, D), _full((1, D))],
        out_shape=[S((T, D), F32), S((1, D), F32)],
        compiler_params=_cp(40, 1),
    )(dzb, x, g, w_t, dx1)


def _load_weights(pairs, sem):
    @pl.when(pl.program_id(0) == 0)
    def _():
        copies = [pltpu.make_async_copy(src, dst, sem.at[k]) for k, (src, dst) in enumerate(pairs)]
        for cp in copies:
            cp.start()
        for cp in copies:
            cp.wait()


def ffn_fwd(x, ycat, wout, g, wg_t, wu_t, wd, name, exchange=None):
    T = x.shape[0]
    tt = TT_FFN
    any_spec = pl.BlockSpec(memory_space=pl.ANY)

    def body(x_ref, ycat_ref, g_ref, wout_hbm, wg_hbm, wu_hbm, wd_hbm,
             x1_ref, h_ref, gate_ref, up_ref, act_ref, x2_ref, wout_v, wg_v, wu_v, wd_v, sem):
        _load_weights([(wout_hbm, wout_v), (wg_hbm, wg_v), (wu_hbm, wu_v), (wd_hbm, wd_v)], sem)
        x1 = x_ref[...] + _nn(ycat_ref[...], wout_v[...])
        x1_ref[...] = x1
        xn, _ = _rms(x1)
        h = (xn * g_ref[...]).astype(BF16)
        h_ref[...] = h
        gate = _nt(h, wg_v[...])
        up = _nt(h, wu_v[...])
        gate_ref[...] = gate.astype(BF16)
        up_ref[...] = up.astype(BF16)
        act = (gate * jax.nn.sigmoid(gate) * up).astype(BF16)
        act_ref[...] = act
        x2_ref[...] = x1 + _nn(act, wd_v[...])

    return _pallas(
        body, name=name, grid=(T // tt,),
        in_specs=[_row(tt, D), _row(tt, D), _full((1, D)), any_spec, any_spec, any_spec, any_spec],
        out_specs=[_row(tt, D), _row(tt, D), _row(tt, D_FF), _row(tt, D_FF), _row(tt, D_FF), _row(tt, D)],
        out_shape=[S((T, D), F32), S((T, D), BF16), S((T, D_FF), BF16), S((T, D_FF), BF16), S((T, D_FF), BF16),
                   S((T, D), F32)],
        scratch_shapes=[pltpu.VMEM((D, D), BF16), pltpu.VMEM((D_FF, D), BF16), pltpu.VMEM((D_FF, D), BF16),
                        pltpu.VMEM((D_FF, D), BF16), pltpu.SemaphoreType.DMA((4,))],
        vmem_mb=56, operands=(x, ycat, g, wout, wg_t, wu_t, wd), exchange=exchange)


def ffn_bwd(dx2, x1, gate, up, g, wg_t, wu_t, wd, name, exchange=None):
    T = x1.shape[0]
    tt = TT_FFN
    any_spec = pl.BlockSpec(memory_space=pl.ANY)

    def body(dx2_ref, x1_ref, gate_ref, up_ref, g_ref, wg_hbm, wu_hbm, wd_hbm,
             dgu_ref, dx2b_ref, dx1_ref, dx1b_ref, dg_ref, wg_v, wu_v, wd_v, sem):
        _load_weights([(wg_hbm, wg_v), (wu_hbm, wu_v), (wd_hbm, wd_v)], sem)

        @pl.when(pl.program_id(0) == 0)
        def _():
            dg_ref[...] = jnp.zeros_like(dg_ref)

        dx2 = dx2_ref[...]
        dx2b = dx2.astype(BF16)
        dx2b_ref[...] = dx2b
        dact = _nt(dx2b, wd_v[...])
        gate = gate_ref[...].astype(F32)
        up = up_ref[...].astype(F32)
        sg = jax.nn.sigmoid(gate)
        dgate = (dact * up * (sg * (1.0 + gate * (1.0 - sg)))).astype(BF16)
        dup = (dact * gate * sg).astype(BF16)
        dgu_ref[:, 0:D_FF] = dgate
        dgu_ref[:, D_FF:2 * D_FF] = dup
        dh = _nn(dgate, wg_v[...]) + _nn(dup, wu_v[...])
        xn, r = _rms(x1_ref[...])
        dg_ref[...] += _rowsum(dh * xn)
        dx1 = dx2 + _rms_bwd(dh, xn, r, g_ref[...])
        dx1_ref[...] = dx1
        dx1b_ref[...] = dx1.astype(BF16)

    return _pallas(
        body, name=name, grid=(T // tt,),
        in_specs=[_row(tt, D), _row(tt, D), _row(tt, D_FF), _row(tt, D_FF), _full((1, D)), any_spec, any_spec, any_spec],
        out_specs=[_row(tt, 2 * D_FF), _row(tt, D), _row(tt, D), _row(tt, D), _full((1, D))],
        out_shape=[S((T, 2 * D_FF), BF16), S((T, D), BF16), S((T, D), F32), S((T, D), BF16), S((1, D), F32)],
        scratch_shapes=[pltpu.VMEM((D_FF, D), BF16), pltpu.VMEM((D_FF, D), BF16), pltpu.VMEM((D_FF, D), BF16),
                        pltpu.SemaphoreType.DMA((3,))],
        vmem_mb=56, operands=(dx2, x1, gate, up, g, wg_t, wu_t, wd), exchange=exchange)


def loss_head(x, target, g):
    T = x.shape[0]
    tt = TT_PROJ

    def body(x_ref, t_ref, g_ref, dx_ref, lvec_ref, dg_ref):
        @pl.when(pl.program_id(0) == 0)
        def _():
            lvec_ref[...] = jnp.zeros_like(lvec_ref)
            dg_ref[...] = jnp.zeros_like(dg_ref)

        xn, r = _rms(x_ref[...])
        gg = g_ref[...]
        err = xn * gg - t_ref[...]
        lvec_ref[...] += _rowsum(err * err)
        dy = err * (1.0 / D)
        dg_ref[...] += _rowsum(dy * xn)
        dx_ref[...] = _rms_bwd(dy, xn, r, gg)

    return pl.pallas_call(
        body, name="loss_head", grid=(T // tt,),
        in_specs=[_row(tt, D), _row(tt, D), _full((1, D))],
        out_specs=[_row(tt, D), _full((1, D)), _full((1, D))],
        out_shape=[S((T, D), F32), S((1, D), F32), S((1, D), F32)],
        compiler_params=_cp(32, 1),
    )(x, target, g)


def wgrad(a, b, tm, name):
    T, M = a.shape
    N = b.shape[1]
    tk = min(TK_WGRAD, T)
    n_k = T // tk

    def body(a_ref, b_ref, o_ref, acc_ref):
        k = pl.program_id(1)

        @pl.when(k == 0)
        def _():
            acc_ref[...] = jnp.zeros_like(acc_ref)

        acc_ref[...] += _tn(a_ref[...], b_ref[...])

        @pl.when(k == n_k - 1)
        def _():
            o_ref[...] = acc_ref[...].astype(BF16)

    return pl.pallas_call(
        body, name=name, grid=(M // tm, n_k),
        in_specs=[pl.BlockSpec((tk, tm), lambda m, k: (k, m)), pl.BlockSpec((tk, N), lambda m, k: (k, 0))],
        out_specs=pl.BlockSpec((tm, N), lambda m, k: (m, 0)),
        out_shape=S((M, N), BF16),
        scratch_shapes=[pltpu.VMEM((tm, N), F32)],
        compiler_params=_cp(48, 2),
    )(a, b)


def _mesh_place():
    x, y, c = lax.axis_index("x"), lax.axis_index("y"), lax.axis_index("c")
    return x, y, c, 4 * x + 2 * y + c


def _peer(x, y, c, k):
    px = 1 - x if k & 4 else x
    py = 1 - y if k & 2 else y
    pc = 1 - c if k & 1 else c
    return (px, py, pc), 4 * px + 2 * py + pc


class _Exchange:
    SAME_CORE = (2, 4, 6)

    def __init__(self, gather=(), scatter=()):
        self.entries = [(a, None, a.shape[0]) for a in gather] + [(a, off, rows) for a, off, rows in scatter]
        self.n_gather = len(gather)

    @property
    def n(self):
        return len(self.entries)

    def operands(self):
        return [e[0] for e in self.entries]

    def out_shapes(self):
        return [S((N_DEV, rows, a.shape[1]), a.dtype) for a, _, rows in self.entries]

    def sems(self):
        return [pltpu.SemaphoreType.DMA((self.n, N_DEV)), pltpu.SemaphoreType.DMA((self.n, N_DEV)),
                pltpu.SemaphoreType.DMA((self.n,))]

    def _src(self, ref, e, idx):
        _, off, rows = self.entries[e]
        if off is None:
            return ref
        return ref.at[pl.ds(pl.multiple_of(off + idx * rows, 16), rows)]

    def _masks(self, e):
        return (1,) + self.SAME_CORE if e < self.n_gather else tuple(range(1, N_DEV))

    def _copy(self, ins, outs, sems, e, k, sending, passing_on=False):
        send_sems, recv_sems, _ = sems
        x, y, c, me = _mesh_place()
        peer, pidx = _peer(x, y, c, k)
        if passing_on:
            return pltpu.make_async_remote_copy(
                src_ref=outs[e].at[pidx], dst_ref=outs[e].at[pidx], send_sem=send_sems.at[e, k | 1],
                recv_sem=recv_sems.at[e, k | 1], device_id=_peer(x, y, c, 1)[0], device_id_type=pl.DeviceIdType.MESH)
        return pltpu.make_async_remote_copy(
            src_ref=self._src(ins[e], e, pidx), dst_ref=outs[e].at[me if sending else pidx], send_sem=send_sems.at[e, k],
            recv_sem=recv_sems.at[e, k], device_id=peer, device_id_type=pl.DeviceIdType.MESH)

    def _local(self, ins, outs, sems):
        me = _mesh_place()[3]
        return [pltpu.make_async_copy(self._src(ins[e], e, me), outs[e].at[me], sems[2].at[e]) for e in range(self.n)]

    def start(self, ins, outs, sems):
        for cp in self._local(ins, outs, sems):
            cp.start()
        for k in range(1, N_DEV):
            for e in range(self.n):
                if k in self._masks(e):
                    self._copy(ins, outs, sems, e, k, True).start()

    def forward(self, ins, outs, sems):
        for k in self.SAME_CORE:
            for e in range(self.n_gather):
                self._copy(ins, outs, sems, e, k, False).wait_recv()
                self._copy(ins, outs, sems, e, k, False, passing_on=True).start()

    def wait(self, ins, outs, sems):
        for k in range(1, N_DEV):
            for e in range(self.n):
                if e >= self.n_gather or k % 2:
                    self._copy(ins, outs, sems, e, k, False).wait_recv()
        for k in range(1, N_DEV):
            for e in range(self.n):
                self._copy(ins, outs, sems, e, k, True).wait_send()
        for cp in self._local(ins, outs, sems):
            cp.wait()


def _pallas(body, *, name, grid, in_specs, out_specs, out_shape, scratch_shapes, vmem_mb, operands, exchange=None):
    n_in, n_out, n_scr = len(in_specs), len(out_specs), len(scratch_shapes)
    if exchange is None:
        res = pl.pallas_call(body, name=name, grid=grid, in_specs=in_specs, out_specs=out_specs, out_shape=out_shape,
                             scratch_shapes=scratch_shapes, compiler_params=_cp(vmem_mb, len(grid)))(*operands)
        return list(res), []
    ex = exchange

    def hosted(*refs):
        ins, ex_in = refs[:n_in], refs[n_in:n_in + ex.n]
        outs = refs[n_in + ex.n:n_in + ex.n + n_out]
        ex_out = refs[n_in + ex.n + n_out:n_in + 2 * ex.n + n_out]
        scr = refs[n_in + 2 * ex.n + n_out:]
        sems = scr[n_scr:]
        step = pl.program_id(0)

        @pl.when(step == 0)
        def _():
            ex.start(ex_in, ex_out, sems)

        body(*ins, *outs, *scr[:n_scr])

        if ex.n_gather:
            @pl.when(step == max(grid[0] - 1 - max(2, grid[0] // 8), 0))
            def _():
                ex.forward(ex_in, ex_out, sems)

        @pl.when(step == grid[0] - 1)
        def _():
            ex.wait(ex_in, ex_out, sems)

    any_spec = pl.BlockSpec(memory_space=pl.ANY)
    res = pl.pallas_call(
        hosted, name=name, grid=grid, in_specs=list(in_specs) + [any_spec] * ex.n,
        out_specs=list(out_specs) + [any_spec] * ex.n, out_shape=list(out_shape) + ex.out_shapes(),
        scratch_shapes=list(scratch_shapes) + ex.sems(), compiler_params=_cp(vmem_mb, len(grid)),
    )(*operands, *ex.operands())
    return list(res[:n_out]), list(res[n_out:])


def exchange_only(ex, name):
    def body(*refs):
        ins, outs, sems = refs[:ex.n], refs[ex.n:2 * ex.n], refs[2 * ex.n:]
        ex.start(ins, outs, sems)
        ex.forward(ins, outs, sems)
        ex.wait(ins, outs, sems)

    any_spec = pl.BlockSpec(memory_space=pl.ANY)
    return list(pl.pallas_call(body, name=name, in_specs=[any_spec] * ex.n, out_specs=[any_spec] * ex.n,
                               out_shape=ex.out_shapes(), scratch_shapes=ex.sems())(*ex.operands()))


def _adamw(w, g, m, v):
    m = ADAM_B1 * m + (1.0 - ADAM_B1) * g
    v = ADAM_B2 * v + (1.0 - ADAM_B2) * (g * g)
    m_hat = m / (1.0 - ADAM_B1 ** ADAM_STEP)
    v_hat = v / (1.0 - ADAM_B2 ** ADAM_STEP)
    delta = -ADAM_LR * (m_hat / (jnp.sqrt(v_hat) + ADAM_EPS) + ADAM_WD * w)
    return delta, m, v


def _sum_parts(p_ref, rows=slice(None)):
    g = p_ref[0, rows].astype(F32)
    for k in range(1, N_DEV):
        g = g + p_ref[k, rows].astype(F32)
    return g


def adamw_layers(parts, w, m, v, name, transposed):
    n_l = len(parts)

    def body(*refs):
        p_refs = refs[:n_l]
        w_ref, m_ref, v_ref, g_out, d_out, m_out, v_out = refs[n_l:]
        for l in range(n_l):
            g = _sum_parts(p_refs[l])
            if transposed:
                g = g.T
            g_out[l] = g
            d_out[l], m_out[l], v_out[l] = _adamw(w_ref[l], g, m_ref[l], v_ref[l])

    return pl.pallas_call(
        body, name=name, out_shape=[S(w.shape, F32)] * 4, compiler_params=_cp(48),
    )(*parts, w, m, v)


def adamw_segments(parts, segments, w, m, v, name):
    n_p = len(parts)

    def body(*refs):
        p_refs = refs[:n_p]
        w_ref, m_ref, v_ref, g_out, d_out, m_out, v_out = refs[n_p:]
        for part, src, dst, rows in segments:
            g = _sum_parts(p_refs[part], slice(src, src + rows))
            to = slice(dst, dst + rows)
            g_out[to] = g
            d_out[to], m_out[to], v_out[to] = _adamw(w_ref[to], g, m_ref[to], v_ref[to])

    return pl.pallas_call(
        body, name=name, out_shape=[S(w.shape, F32)] * 4, compiler_params=_cp(48),
    )(*parts, w, m, v)


SMALL_LAYER = ("g_mix", "A_re", "A_im", "log_dt", "B_re", "B_im", "C_re", "C_im", "D_skip", "b_glu", "w_pool",
               "pool_scale", "sgu_ln_g", "sgu_ln_b", "w_spatial", "b_spatial", "g_ffn")
BIG_NAMES = ("w_in", "w_glu", "w_out", "w_gate", "w_up", "w_down")
COLUMN_SHARDED = ("w_in", "w_gate", "w_up")
WEIGHT_ORDER = ("g_mix", "w_in", "A_re", "A_im", "log_dt", "B_re", "B_im", "C_re", "C_im", "D_skip", "w_glu", "b_glu",
                "w_pool", "pool_scale", "sgu_ln_g", "sgu_ln_b", "w_spatial", "b_spatial", "w_out", "g_ffn", "w_gate",
                "w_up", "w_down", "g_final")
SEG = 1024


def _pack(arrays):
    parts = []
    for a in arrays:
        flat = a.reshape(-1)
        parts.append(jnp.pad(flat, (0, (-flat.shape[0]) % SEG)))
    return jnp.concatenate(parts).reshape(-1, 128)


def _state_rows(p):
    return p.reshape(1, D_ST)


def _chan_by_state(p):
    return jnp.transpose(p, (2, 0, 1)).reshape(GRP, D_ST)


def _chan_by_state_c(p):
    return jnp.transpose(p, (1, 0, 2)).reshape(GRP, D_ST)


def kernel(x, g_mix, w_in, A_re, A_im, log_dt, B_re, B_im, C_re, C_im, D_skip, w_glu, b_glu, w_pool, pool_scale, sgu_ln_g, sgu_ln_b, w_spatial, b_spatial, w_out, g_ffn, w_gate, w_up, w_down, g_final, loss_target, m_g_mix, m_w_in, m_A_re, m_A_im, m_log_dt, m_B_re, m_B_im, m_C_re, m_C_im, m_D_skip, m_w_glu, m_b_glu, m_w_pool, m_pool_scale, m_sgu_ln_g, m_sgu_ln_b, m_w_spatial, m_b_spatial, m_w_out, m_g_ffn, m_w_gate, m_w_up, m_w_down, m_g_final, v_g_mix, v_w_in, v_A_re, v_A_im, v_log_dt, v_B_re, v_B_im, v_C_re, v_C_im, v_D_skip, v_w_glu, v_b_glu, v_w_pool, v_pool_scale, v_sgu_ln_g, v_sgu_ln_b, v_w_spatial, v_b_spatial, v_w_out, v_g_ffn, v_w_gate, v_w_up, v_w_down, v_g_final):
    args = dict(locals())
    W = {n: args[n] for n in WEIGHT_ORDER}
    M = {n: args["m_" + n] for n in WEIGHT_ORDER}
    V = {n: args["v_" + n] for n in WEIGHT_ORDER}
    n_layers = g_mix.shape[0]
    x0 = x[0]
    target = loss_target[0]

    def my_rows(name, l):
        w = W[name][l]
        return (w.T if name in COLUMN_SHARDED else w).astype(BF16)

    full_w = [dict() for _ in range(n_layers)]

    def gather_of(*which):
        return _Exchange(gather=[my_rows(n, l) for n, l in which])

    def keep_gathered(which, arrays):
        for (n, l), a in zip(which, arrays):
            full_w[l][n] = a.reshape(-1, a.shape[-1])

    tri = jnp.tril(jnp.ones((CHUNK, CHUNK), bool))
    consts = []
    for l in range(n_layers):
        a_re, a_im = _state_rows(A_re[l]), _state_rows(A_im[l])
        ldt = jnp.repeat(log_dt[l], N_STATE).reshape(1, D_ST)
        b_re_t, b_im_t = _chan_by_state(B_re[l]), _chan_by_state(B_im[l])
        sc, bbd, cbd = s5_prepare(a_re, a_im, ldt, b_re_t, b_im_t, _chan_by_state_c(C_re[l]), _chan_by_state_c(C_im[l]))
        wsm = jnp.where(tri[None], w_spatial[l], 0.0)
        wpool_bd = jnp.zeros((D_POOL, D_POOL), F32)
        for gi in range(len(POOL_WINDOWS)):
            wpool_bd = wpool_bd.at[gi * 64:(gi + 1) * 64, gi * 64:(gi + 1) * 64].set(w_pool[l, gi])
        consts.append(dict(
            disc=(a_re, a_im, ldt, b_re_t, b_im_t), sc=sc, bbd=bbd, cbd=cbd,
            dskip=D_skip[l].reshape(1, D_SSM), bglu=b_glu[l].reshape(1, D_SSM),
            wpool=wpool_bd.astype(BF16), pscale=pool_scale[l].reshape(1, D_POOL),
            lng=sgu_ln_g[l].reshape(1, D_SGU), lnb=sgu_ln_b[l].reshape(1, D_SGU),
            wsm=wsm.astype(BF16), wsmt=jnp.transpose(wsm, (0, 2, 1)).astype(BF16),
            bsp=jnp.repeat(b_spatial[l].T, HEAD_DIM, axis=1),
            gmix=g_mix[l].reshape(1, D), gffn=g_ffn[l].reshape(1, D)))

    def mixer_args(l):
        c = consts[l]
        return (c["bbd"], c["cbd"], c["dskip"], full_w[l]["w_glu"], c["bglu"], c["wpool"], c["pscale"], c["lng"], c["lnb"])

    first_needed = [("w_in", 0)]
    keep_gathered(first_needed, exchange_only(gather_of(*first_needed), "gather_first"))
    carried_fwd = {
        ("inproj", 0): [("w_glu", 0), ("w_out", 0)],
        ("mixer", 0): [("w_gate", 0), ("w_up", 0), ("w_down", 0)],
        ("ffn", 0): [("w_in", 1), ("w_glu", 1), ("w_out", 1), ("w_gate", 1)],
        ("mixer", 1): [("w_up", 1), ("w_down", 1)],
    }

    def carried(kind, l):
        which = carried_fwd.get((kind, l))
        return which, (gather_of(*which) if which else None)

    saved = []
    xl = x0
    for l in range(n_layers):
        c, fw = consts[l], full_w[l]
        which, ex = carried("inproj", l)
        (h, z), got = inproj_fwd(xl, c["gmix"], fw["w_in"], f"inproj_fwd_{l}", ex)
        keep_gathered(which or [], got)
        which, ex = carried("mixer", l)
        (ycat, hs, ys), got = mixer_fwd(z, c["sc"], *mixer_args(l), c["wsm"], c["bsp"], f"mixer_fwd_{l}", ex)
        keep_gathered(which or [], got)
        which, ex = carried("ffn", l)
        (x1, h2, gate, up, act, x2), got = ffn_fwd(xl, ycat, fw["w_out"], c["gffn"], fw["w_gate"], fw["w_up"], fw["w_down"],
                                                    f"ffn_fwd_{l}", ex)
        keep_gathered(which or [], got)
        saved.append(dict(x=xl, h=h, z=z, ycat=ycat, hs=hs, ys=ys, x1=x1, h2=h2, gate=gate, up=up, act=act))
        xl = x2
    dx, loss_vec, d_gfinal = loss_head(xl, target, g_final.reshape(1, D))

    recv_big = {}
    recv_small = [None] * n_layers

    def keep_received(which, arrays):
        for key, a in zip(which, arrays):
            if key[0] == "small":
                recv_small[key[1]] = a
            else:
                recv_big[key] = a

    pending = None
    for l in reversed(range(n_layers)):
        c, fw, sv = consts[l], full_w[l], saved[l]
        (dgu, dx2b, dx1, dx1b, d_gffn), got = ffn_bwd(dx, sv["x1"], sv["gate"], sv["up"], c["gffn"], fw["w_gate"], fw["w_up"],
                                                     fw["w_down"], f"ffn_bwd_{l}", pending[1] if pending else None)
        if pending:
            keep_received(pending[0], got)
        g_gu = wgrad(dgu, sv["h2"], D_FF // 2, f"wgrad_gate_up_{l}")
        g_down = wgrad(sv["act"], dx2b, D_FF // 2, f"wgrad_down_{l}")
        g_out = wgrad(sv["ycat"], dx1b, D, f"wgrad_out_{l}")
        ffn_rows = D_FF // N_DEV
        ex = _Exchange(scatter=[(g_gu, 0, ffn_rows), (g_gu, D_FF, ffn_rows), (g_down, 0, ffn_rows), (g_out, 0, D // N_DEV)])
        (dzb, da, dbt, dct, dd, dbglu, dwglu, dwpool, dpscale, dlng, dlnb, dws, dbsp), got = mixer_bwd(
            dx1b, sv["z"], sv["hs"], sv["ys"], fw["w_out"], c["sc"], *mixer_args(l), c["wsm"], c["wsmt"], c["bsp"],
            f"mixer_bwd_{l}", ex)
        keep_received([("w_gate", l), ("w_up", l), ("w_down", l), ("w_out", l)], got)
        dx, d_gmix = inproj_bwd(dzb, sv["x"], c["gmix"], fw["w_in"], dx1)
        g_in = wgrad(dzb, sv["h"], D_IN, f"wgrad_in_{l}")
        d_are, d_aim, d_ldt, d_bre_t, d_bim_t = s5_param_bwd(*c["disc"], da, dbt)
        small = dict(
            g_mix=d_gmix.reshape(D), A_re=d_are.reshape(N_GRP, N_STATE), A_im=d_aim.reshape(N_GRP, N_STATE),
            log_dt=d_ldt[0, :N_GRP],
            B_re=jnp.transpose(d_bre_t.reshape(GRP, N_GRP, N_STATE), (1, 2, 0)),
            B_im=jnp.transpose(d_bim_t.reshape(GRP, N_GRP, N_STATE), (1, 2, 0)),
            C_re=jnp.transpose(dct[:, :D_ST].reshape(GRP, N_GRP, N_STATE), (1, 0, 2)),
            C_im=-jnp.transpose(dct[:, D_ST:].reshape(GRP, N_GRP, N_STATE), (1, 0, 2)),
            D_skip=dd.reshape(D_SSM), b_glu=dbglu.reshape(D_SSM),
            w_pool=jnp.stack([dwpool[gi * 64:(gi + 1) * 64, gi * 64:(gi + 1) * 64] for gi in range(len(POOL_WINDOWS))]),
            pool_scale=dpscale.reshape(D_POOL), sgu_ln_g=dlng.reshape(D_SGU), sgu_ln_b=dlnb.reshape(D_SGU),
            w_spatial=dws, b_spatial=dbsp[:, :HEADS].T, g_ffn=d_gffn.reshape(D))
        packed = [small[n] for n in SMALL_LAYER] + ([d_gfinal.reshape(D), loss_vec.reshape(D)] if l == 0 else [])
        pending = ([("small", l), ("w_in", l), ("w_glu", l)],
                   _Exchange(gather=[_pack(packed)],
                             scatter=[(g_in, 0, D_IN // N_DEV), (dwglu.astype(BF16), 0, D_SSM // N_DEV)]))
    grad_x = dx
    keep_received(pending[0], exchange_only(pending[1], "exchange_last"))

    out = {}
    for n in BIG_NAMES:
        out[n] = adamw_layers([recv_big[(n, l)] for l in range(n_layers)], W[n], M[n], V[n], f"adamw_{n}",
                              transposed=n in COLUMN_SHARDED)

    seg_rows = [(-(-math.prod(W[n].shape[1:]) // SEG)) * (SEG // 128) for n in SMALL_LAYER]
    segments, src, dst = [], 0, 0
    for rows in seg_rows:
        segments += [(l, src, dst + l * rows, rows) for l in range(n_layers)]
        src += rows
        dst += n_layers * rows
    tile_rows = SEG // 128
    segments += [(0, src, dst, tile_rows), (0, src + tile_rows, dst + tile_rows, tile_rows)]

    def pack_params(P):
        parts = []
        for n, rows in zip(SMALL_LAYER, seg_rows):
            flat = P[n].reshape(n_layers, -1)
            parts.append(jnp.pad(flat, ((0, 0), (0, rows * 128 - flat.shape[1]))).reshape(-1))
        return jnp.concatenate(parts + [P["g_final"], jnp.zeros((SEG,), F32)]).reshape(-1, 128)

    res = adamw_segments(recv_small, segments, pack_params(W), pack_params(M), pack_params(V), "adamw_small")
    for j in range(4):
        flat, off = res[j].reshape(-1), 0
        for n, rows in zip(SMALL_LAYER, seg_rows):
            size = math.prod(W[n].shape[1:])
            piece = flat[off:off + n_layers * rows * 128].reshape(n_layers, rows * 128)[:, :size].reshape(W[n].shape)
            out.setdefault(n, []).append(piece)
            off += n_layers * rows * 128
        out.setdefault("g_final", []).append(flat[off:off + D])
        if j == 0:
            loss = (0.5 / D) * jnp.sum(flat[off + SEG:off + SEG + D])

    return (loss, grad_x[None], *[out[n][0] for n in WEIGHT_ORDER], *[out[n][1] for n in WEIGHT_ORDER],
            *[out[n][2] for n in WEIGHT_ORDER], *[out[n][3] for n in WEIGHT_ORDER])
```

```python
import functools
import math

import jax
import jax.numpy as jnp
from jax import lax
from jax.experimental import pallas as pl
from jax.experimental.pallas import tpu as pltpu

F32 = jnp.float32
BF16 = jnp.bfloat16
S = jax.ShapeDtypeStruct

N_DEV = 8
D = 1024
D_SSM = 384
N_GRP = 24
GRP = 16
N_STATE = 64
D_ST = N_GRP * N_STATE
D_POOL = 256
POOL_WINDOWS = (2, 4, 8, 16)
HALO = 16
D_SGU = 384
HEADS = 6
HEAD_DIM = 64
CHUNK = 128
D_IN = 1408
D_FF = 2816
EPS = 1e-6
SCAN_BLK = 8

ADAM_LR = 0.001
ADAM_B1 = 0.9
ADAM_B2 = 0.999
ADAM_EPS = 1e-08
ADAM_WD = 0.01
ADAM_STEP = 10

GELU_C0 = math.sqrt(2.0 / math.pi)
GELU_C1 = 0.044715

TT_MIX = 256
SEG_LEN = TT_MIX // SCAN_BLK
TT_FFN = 256
TT_PROJ = 512
TK_WGRAD = 2048
VMEM_MB = 2 ** 20


def _cp(vmem_mb, grid_dims=0):
    kw = dict(vmem_limit_bytes=int(vmem_mb * VMEM_MB))
    if grid_dims:
        kw["dimension_semantics"] = ("arbitrary",) * grid_dims
    return pltpu.CompilerParams(**kw)


def _row(tt, n):
    return pl.BlockSpec((tt, n), lambda i: (i, 0))


def _full(shape):
    nd = len(shape)
    return pl.BlockSpec(shape, lambda *_: (0,) * nd)


def _nn(a, b):
    return jnp.dot(a, b, preferred_element_type=F32)


def _nt(a, b):
    return lax.dot_general(a, b, (((1,), (1,)), ((), ())), preferred_element_type=F32)


def _tn(a, b):
    return lax.dot_general(a, b, (((0,), (0,)), ((), ())), preferred_element_type=F32)


def _rowsum(x):
    return jnp.sum(x, axis=0, keepdims=True)


def _rms(x):
    r = lax.rsqrt(jnp.mean(x * x, axis=-1, keepdims=True) + EPS)
    return x * r, r


def _rms_bwd(dy, xn, r, g):
    dyg = dy * g
    return r * (dyg - xn * jnp.mean(dyg * xn, axis=-1, keepdims=True))


def _gelu(x):
    s = jax.nn.sigmoid(x * (2.0 * GELU_C0 + (2.0 * GELU_C0 * GELU_C1) * (x * x)))
    return x * s, s


def _gelu_grad(x, s):
    return s * (1.0 + x * (1.0 - s) * (2.0 * GELU_C0 + (6.0 * GELU_C0 * GELU_C1) * (x * x)))


def _discretise(a_re, a_im, ldt, b_re, b_im):
    dt = jnp.exp(ldt)
    mag = jnp.exp(a_re * dt)
    ar = mag * jnp.cos(a_im * dt)
    ai = mag * jnp.sin(a_im * dt)
    den = a_re * a_re + a_im * a_im
    f_re = ((ar - 1.0) * a_re + ai * a_im) / den
    f_im = (ai * a_re - (ar - 1.0) * a_im) / den
    bb_re = f_re * b_re - f_im * b_im
    bb_im = f_re * b_im + f_im * b_re
    return ar, ai, bb_re, bb_im


def _group_mask(rows, cols):
    r = lax.broadcasted_iota(jnp.int32, (rows, cols), 0) // GRP
    c = lax.broadcasted_iota(jnp.int32, (rows, cols), 1)
    c = jnp.where(c >= D_ST, c - D_ST, c) // N_STATE
    return r == c


def s5_prepare(a_re, a_im, ldt, b_re_t, b_im_t, c_re_t, c_im_t):
    def body(are_ref, aim_ref, ldt_ref, bre_ref, bim_ref, cre_ref, cim_ref, sc_ref, bbd_ref, cbd_ref):
        ar, ai, bb_re, bb_im = _discretise(are_ref[...], aim_ref[...], ldt_ref[...], bre_ref[...], bim_ref[...])
        mask = _group_mask(D_SSM, 2 * D_ST)
        bb = jnp.concatenate([jnp.tile(bb_re, (N_GRP, 1)), jnp.tile(bb_im, (N_GRP, 1))], axis=1)
        bbd_ref[...] = jnp.where(mask, bb, 0.0).astype(BF16)
        cc = jnp.concatenate([jnp.tile(cre_ref[...], (N_GRP, 1)), -jnp.tile(cim_ref[...], (N_GRP, 1))], axis=1)
        cbd_ref[...] = jnp.where(mask, cc, 0.0).astype(BF16)
        pr, pi = ar, ai
        for _ in range(SEG_LEN - 1):
            pr, pi = pr * ar - pi * ai, pr * ai + pi * ar
        for k, v in enumerate((ar, ai, pr, pi)):
            sc_ref[8 * k:8 * k + 8, :] = jnp.broadcast_to(v, (SCAN_BLK, D_ST))

    return pl.pallas_call(
        body, name="s5_prepare",
        out_shape=[S((32, D_ST), F32), S((D_SSM, 2 * D_ST), BF16), S((D_SSM, 2 * D_ST), BF16)],
        compiler_params=_cp(40),
    )(a_re, a_im, ldt, b_re_t, b_im_t, c_re_t, c_im_t)


def s5_param_bwd(a_re, a_im, ldt, b_re_t, b_im_t, da, dbt):
    def body(are_ref, aim_ref, ldt_ref, bre_ref, bim_ref, da_ref, dbt_ref, o_are, o_aim, o_ldt, o_bre, o_bim):
        _, vjp = jax.vjp(_discretise, are_ref[...], aim_ref[...], ldt_ref[...], bre_ref[...], bim_ref[...])
        da = da_ref[...]
        dbt = dbt_ref[...]
        g_are, g_aim, g_ldt, g_bre, g_bim = vjp((da[:, :D_ST], da[:, D_ST:], dbt[:, :D_ST], dbt[:, D_ST:]))
        o_are[...] = g_are
        o_aim[...] = g_aim
        o_bre[...] = g_bre
        o_bim[...] = g_bim
        grp = lax.broadcasted_iota(jnp.int32, (1, D_ST), 1) // N_STATE
        lane = lax.broadcasted_iota(jnp.int32, (1, 128), 1)
        out = jnp.zeros((1, 128), F32)
        for g in range(N_GRP):
            out = jnp.where(lane == g, jnp.sum(jnp.where(grp == g, g_ldt, 0.0), axis=1, keepdims=True), out)
        o_ldt[...] = out

    return pl.pallas_call(
        body, name="s5_param_bwd",
        out_shape=[S((1, D_ST), F32), S((1, D_ST), F32), S((1, 128), F32), S((GRP, D_ST), F32), S((GRP, D_ST), F32)],
        compiler_params=_cp(16),
    )(a_re, a_im, ldt, b_re_t, b_im_t, da, dbt)


_CH = ((0, 256), (256, D_SSM))
_ST = ((0, 1024), (1024, D_ST))


def _bd_expand(xb, w_ref, out_ref):
    for (c0, c1), (s0, s1) in zip(_CH, _ST):
        for half in (0, D_ST):
            out_ref[:, half + s0:half + s1] = _nn(xb[:, c0:c1], w_ref[c0:c1, half + s0:half + s1])


def _bd_contract(hb, w_ref):
    parts = []
    for (c0, c1), (s0, s1) in zip(_CH, _ST):
        parts.append(_nt(hb[:, s0:s1], w_ref[c0:c1, s0:s1]) + _nt(hb[:, D_ST + s0:D_ST + s1], w_ref[c0:c1, D_ST + s0:D_ST + s1]))
    return jnp.concatenate(parts, axis=1)


def _bd_accumulate(acc_ref, xb, hb):
    for (c0, c1), (s0, s1) in zip(_CH, _ST):
        for half in (0, D_ST):
            acc_ref[c0:c1, half + s0:half + s1] += _tn(xb[:, c0:c1], hb[:, half + s0:half + s1])


def _interleave_matrices(tt):
    r = lax.broadcasted_iota(jnp.int32, (tt, tt), 0)
    t = lax.broadcasted_iota(jnp.int32, (tt, tt), 1)
    p = (t == (r % SCAN_BLK) * (tt // SCAN_BLK) + r // SCAN_BLK).astype(BF16)
    return p, p.T


def _interleave_f32(p, x):
    hi = x.astype(BF16)
    lo = (x - hi.astype(F32)).astype(BF16)
    return _nn(p, hi) + _nn(p, lo)


def _scan_tile(buf_ref, sc_ref, carry_ref, n_blk, reverse):
    ar = sc_ref[0:8, :]
    ai = -sc_ref[8:16, :] if reverse else sc_ref[8:16, :]

    def rows(i):
        blk = (n_blk - 1 - i) if reverse else i
        return pl.ds(pl.multiple_of(blk * SCAN_BLK, SCAN_BLK), SCAN_BLK)

    def local(i, x):
        xr, xi = x
        r = rows(i)
        xr, xi = buf_ref[r, 0:D_ST] + ar * xr - ai * xi, buf_ref[r, D_ST:2 * D_ST] + ar * xi + ai * xr
        buf_ref[r, 0:D_ST] = xr
        buf_ref[r, D_ST:2 * D_ST] = xi
        return xr, xi

    zero = jnp.zeros((SCAN_BLK, D_ST), F32)
    end_r, end_i = lax.fori_loop(0, n_blk, local, (zero, zero), unroll=2)

    seg_r = sc_ref[16:17, :]
    seg_i = -sc_ref[24:25, :] if reverse else sc_ref[24:25, :]
    cr, ci = carry_ref[0:1, 0:D_ST], carry_ref[0:1, D_ST:2 * D_ST]
    sub = lax.broadcasted_iota(jnp.int32, (SCAN_BLK, D_ST), 0)
    in_r, in_i = zero, zero
    for s in (reversed(range(SCAN_BLK)) if reverse else range(SCAN_BLK)):
        in_r = jnp.where(sub == s, cr, in_r)
        in_i = jnp.where(sub == s, ci, in_i)
        cr, ci = end_r[s:s + 1, :] + seg_r * cr - seg_i * ci, end_i[s:s + 1, :] + seg_r * ci + seg_i * cr
    carry_ref[0:1, 0:D_ST] = cr
    carry_ref[0:1, D_ST:2 * D_ST] = ci

    def fix(i, d):
        dr, di = d
        dr, di = ar * dr - ai * di, ar * di + ai * dr
        r = rows(i)
        buf_ref[r, 0:D_ST] += dr
        buf_ref[r, D_ST:2 * D_ST] += di
        return dr, di

    lax.fori_loop(0, n_blk, fix, (in_r, in_i), unroll=2)


def _lane_windows(n):
    lane = lax.broadcasted_iota(jnp.int32, (1, n), 1)
    return lane // (D_POOL // len(POOL_WINDOWS))


def _select_window(grp, s2, s4, s8, s16):
    return jnp.where(grp == 0, s2, jnp.where(grp == 1, s4, jnp.where(grp == 2, s8, s16)))


def _pool_fwd(pbuf_ref, zb, halo, tile_idx, tt):
    pbuf_ref[0:HALO, :] = halo
    pbuf_ref[HALO:HALO + tt, :] = zb
    x = pbuf_ref[...]
    s2 = x + pltpu.roll(x, 1, axis=0)
    s4 = s2 + pltpu.roll(s2, 2, axis=0)
    s8 = s4 + pltpu.roll(s4, 4, axis=0)
    s16 = s8 + pltpu.roll(s8, 8, axis=0)
    grp = _lane_windows(D_POOL)
    win = _select_window(grp, s2, s4, s8, s16)[HALO:HALO + tt, :]
    width = _select_window(grp, 2.0, 4.0, 8.0, 16.0).astype(F32)
    pos = (tile_idx * tt + 1 + lax.broadcasted_iota(jnp.int32, (tt, 1), 0)).astype(F32)
    cnt = jnp.minimum(pos, width)
    return win / cnt - zb, cnt


def _sgu_fwd(zu, zv, lng, lnb, wsm_ref, bsp, mix_ref, tt):
    u, tu = _gelu(zu)
    v, tv = _gelu(zv)
    mu = jnp.mean(v, axis=-1, keepdims=True)
    vc = v - mu
    rstd = lax.rsqrt(jnp.mean(vc * vc, axis=-1, keepdims=True) + EPS)
    vhat = vc * rstd
    vnb = (vhat * lng + lnb).astype(BF16)
    for c in range(tt // CHUNK):
        rows = slice(c * CHUNK, (c + 1) * CHUNK)
        parts = [_nn(wsm_ref[h], vnb[rows, h * HEAD_DIM:(h + 1) * HEAD_DIM]) for h in range(HEADS)]
        mix_ref[rows, :] = jnp.concatenate(parts, axis=1) + bsp
    return u, tu, tv, vhat, rstd, vnb


def mixer_fwd(z, sc, bbd, cbd, dskip, wglu, bglu, wpool, pscale, lng, lnb, wsm, bsp, perm, name, exchange=None):
    T = z.shape[0]
    tt = TT_MIX
    n_tiles = T // tt

    def body(z_ref, sc_ref, bbd_ref, cbd_ref, dskip_ref, wglu_ref, bglu_ref, wpool_ref, pscale_ref, lng_ref, lnb_ref,
             wsm_ref, bsp_ref, p_ref, pt_ref, ycat_ref, hs_ref, ys_ref, carry_ref, halo_ref, pbuf_ref, mix_ref):
        i = pl.program_id(0)

        @pl.when(i == 0)
        def _():
            carry_ref[...] = jnp.zeros_like(carry_ref)
            halo_ref[...] = jnp.zeros_like(halo_ref)

        za = z_ref[:, 0:D_SSM]
        zb = z_ref[:, D_SSM:D_SSM + D_POOL]
        zu = z_ref[:, D_SSM + D_POOL:D_SSM + D_POOL + D_SGU]
        zv = z_ref[:, D_SSM + D_POOL + D_SGU:D_IN]
        p, pt = p_ref[...], pt_ref[...]
        za = _interleave_f32(p, za)
        _bd_expand(za.astype(BF16), bbd_ref, hs_ref)
        _scan_tile(hs_ref, sc_ref, carry_ref, tt // SCAN_BLK, reverse=False)
        y = _bd_contract(hs_ref[...].astype(BF16), cbd_ref) + dskip_ref[...] * za
        ys_ref[...] = y
        g, _ = _gelu(y)
        q = _nn(g.astype(BF16), wglu_ref[...]) + bglu_ref[...]
        ycat_ref[:, 0:D_SSM] = _nn(pt, (g * jax.nn.sigmoid(q)).astype(BF16)).astype(BF16)
        pooled, _ = _pool_fwd(pbuf_ref, zb, halo_ref[...], i, tt)
        halo_ref[...] = zb[tt - HALO:tt, :]
        ycat_ref[:, D_SSM:D_SSM + D_POOL] = (_nn(pooled.astype(BF16), wpool_ref[...]) * pscale_ref[...]).astype(BF16)
        u, _, _, _, _, _ = _sgu_fwd(zu, zv, lng_ref[...], lnb_ref[...], wsm_ref, bsp_ref[...], mix_ref, tt)
        ycat_ref[:, D_SSM + D_POOL:D] = (u * mix_ref[...]).astype(BF16)

    return _pallas(
        body, name=name, grid=(n_tiles,),
        in_specs=[_row(tt, D_IN), _full((32, D_ST)), _full((D_SSM, 2 * D_ST)), _full((D_SSM, 2 * D_ST)),
                  _full((1, D_SSM)), _full((D_SSM, D_SSM)), _full((1, D_SSM)), _full((D_POOL, D_POOL)),
                  _full((1, D_POOL)), _full((1, D_SGU)), _full((1, D_SGU)), _full((HEADS, CHUNK, CHUNK)),
                  _full((CHUNK, D_SGU)), _full((tt, tt)), _full((tt, tt))],
        out_specs=[_row(tt, D), _row(tt, 2 * D_ST), _row(tt, D_SSM)],
        out_shape=[S((T, D), BF16), S((T, 2 * D_ST), F32), S((T, D_SSM), F32)],
        scratch_shapes=[pltpu.VMEM((SCAN_BLK, 2 * D_ST), F32), pltpu.VMEM((HALO, D_POOL), F32),
                        pltpu.VMEM((tt + HALO, D_POOL), F32), pltpu.VMEM((tt, D_SGU), F32)],
        vmem_mb=48, operands=(z, sc, bbd, cbd, dskip, wglu, bglu, wpool, pscale, lng, lnb, wsm, bsp, *perm),
        exchange=exchange)


def mixer_bwd(dx1b, z, hs, ys, wout, sc, bbd, cbd, dskip, wglu, bglu, wpool, pscale, lng, lnb, wsm, wsmt, bsp, perm, name,
              exchange=None):
    T = z.shape[0]
    tt = TT_MIX
    n_tiles = T // tt

    def rev(i):
        return n_tiles - 1 - i

    def body(dx_ref, z_ref, zprev_ref, hs_ref, hsprev_ref, ys_ref, wout_ref, sc_ref, bbd_ref, cbd_ref, dskip_ref,
             wglu_ref, bglu_ref, wpool_ref, pscale_ref, lng_ref, lnb_ref, wsm_ref, wsmt_ref, bsp_ref, p_ref, pt_ref,
             dz_ref, o_da, o_dbt, o_dct, o_dd, o_dbglu, o_dwglu, o_dwpool, o_dpscale, o_dlng, o_dlnb, o_dws, o_dbsp,
             gbuf_ref, carry_ref, ehalo_ref, pbuf_ref, mix_ref, dvn_ref, accb_ref, accc_ref, accw_ref, accm_ref):
        i = pl.program_id(0)
        tile = rev(i)

        @pl.when(i == 0)
        def _():
            carry_ref[...] = jnp.zeros_like(carry_ref)
            ehalo_ref[...] = jnp.zeros_like(ehalo_ref)
            accb_ref[...] = jnp.zeros_like(accb_ref)
            accc_ref[...] = jnp.zeros_like(accc_ref)
            accw_ref[...] = jnp.zeros_like(accw_ref)
            accm_ref[...] = jnp.zeros_like(accm_ref)
            for o in (o_da, o_dd, o_dbglu, o_dwglu, o_dwpool, o_dpscale, o_dlng, o_dlnb):
                o[...] = jnp.zeros_like(o)

        p, pt = p_ref[...], pt_ref[...]
        dxb = dx_ref[...]
        d_a = _nt(_nn(p, dxb).astype(BF16), wout_ref[0:D_SSM, :])
        d_bc = _nt(dxb, wout_ref[D_SSM:D, :])
        d_b = d_bc[:, 0:D_POOL]
        d_c = d_bc[:, D_POOL:D_POOL + D_SGU]
        za = _interleave_f32(p, z_ref[:, 0:D_SSM])
        zb = z_ref[:, D_SSM:D_SSM + D_POOL]
        zu = z_ref[:, D_SSM + D_POOL:D_SSM + D_POOL + D_SGU]
        zv = z_ref[:, D_SSM + D_POOL + D_SGU:D_IN]
        first_tile = (tile > 0).astype(F32)

        y = ys_ref[...]
        g, tg = _gelu(y)
        gb = g.astype(BF16)
        sg = jax.nn.sigmoid(_nn(gb, wglu_ref[...]) + bglu_ref[...])
        dq = d_a * g * sg * (1.0 - sg)
        dqb = dq.astype(BF16)
        o_dbglu[...] += _rowsum(dq)
        o_dwglu[...] += _tn(gb, dqb)
        dy = (d_a * sg + _nt(dqb, wglu_ref[...])) * _gelu_grad(y, tg)
        o_dd[...] += _rowsum(dy * za)
        dyb = dy.astype(BF16)
        _bd_accumulate(accc_ref, dyb, hs_ref[...].astype(BF16))
        _bd_expand(dyb, cbd_ref, gbuf_ref)
        _scan_tile(gbuf_ref, sc_ref, carry_ref, tt // SCAN_BLK, reverse=True)
        hprev = hsprev_ref[SCAN_BLK - 1:SCAN_BLK, :] * first_tile
        sub = lax.broadcasted_iota(jnp.int32, (SCAN_BLK, 1), 0)
        edge = jnp.where(sub == 0, hprev, pltpu.roll(hs_ref[tt - SCAN_BLK:tt, :], 1, axis=0))

        def da_terms(gr, gi, hr, hi):
            return _rowsum(gr * hr + gi * hi), _rowsum(gi * hr - gr * hi)

        body_re, body_im = da_terms(gbuf_ref[SCAN_BLK:tt, 0:D_ST], gbuf_ref[SCAN_BLK:tt, D_ST:],
                                    hs_ref[0:tt - SCAN_BLK, 0:D_ST], hs_ref[0:tt - SCAN_BLK, D_ST:])
        edge_re, edge_im = da_terms(gbuf_ref[0:SCAN_BLK, 0:D_ST], gbuf_ref[0:SCAN_BLK, D_ST:], edge[:, 0:D_ST], edge[:, D_ST:])
        o_da[:, 0:D_ST] += body_re + edge_re
        o_da[:, D_ST:] += body_im + edge_im
        gtb = gbuf_ref[...].astype(BF16)
        _bd_accumulate(accb_ref, za.astype(BF16), gtb)
        dza = (dy * dskip_ref[...] + _bd_contract(gtb, bbd_ref)).astype(BF16)
        dz_ref[:, 0:D_SSM] = _nn(pt, dza).astype(BF16)

        pooled, cnt = _pool_fwd(pbuf_ref, zb, zprev_ref[:, D_SSM:D_SSM + D_POOL] * first_tile, tile, tt)
        pooledb = pooled.astype(BF16)
        mixed = _nn(pooledb, wpool_ref[...])
        o_dpscale[...] += _rowsum(d_b * mixed)
        dmixb = (d_b * pscale_ref[...]).astype(BF16)
        o_dwpool[...] += _tn(pooledb, dmixb)
        dpooled = _nt(dmixb, wpool_ref[...])
        e = dpooled / cnt
        pbuf_ref[0:tt, :] = e
        pbuf_ref[tt:tt + HALO, :] = ehalo_ref[...]
        ehalo_ref[...] = e[0:HALO, :]
        x = pbuf_ref[...]
        n = tt + HALO
        f2 = x + pltpu.roll(x, n - 1, axis=0)
        f4 = f2 + pltpu.roll(f2, n - 2, axis=0)
        f8 = f4 + pltpu.roll(f4, n - 4, axis=0)
        f16 = f8 + pltpu.roll(f8, n - 8, axis=0)
        fwd_sum = _select_window(_lane_windows(D_POOL), f2, f4, f8, f16)[0:tt, :]
        dz_ref[:, D_SSM:D_SSM + D_POOL] = (fwd_sum - dpooled).astype(BF16)

        lng = lng_ref[...]
        u, tu, tv, vhat, rstd, vnb = _sgu_fwd(zu, zv, lng, lnb_ref[...], wsm_ref, bsp_ref[...], mix_ref, tt)
        dz_ref[:, D_SSM + D_POOL:D_SSM + D_POOL + D_SGU] = (d_c * mix_ref[...] * _gelu_grad(zu, tu)).astype(BF16)
        dmix = d_c * u
        dmixb2 = dmix.astype(BF16)
        for c in range(tt // CHUNK):
            rows = slice(c * CHUNK, (c + 1) * CHUNK)
            accm_ref[...] += dmix[rows, :]
            parts = []
            for h in range(HEADS):
                cols = slice(h * HEAD_DIM, (h + 1) * HEAD_DIM)
                accw_ref[h] += _nt(dmixb2[rows, cols], vnb[rows, cols])
                parts.append(_nn(wsmt_ref[h], dmixb2[rows, cols]))
            dvn_ref[rows, :] = jnp.concatenate(parts, axis=1)
        dvn = dvn_ref[...]
        o_dlng[...] += _rowsum(dvn * vhat)
        o_dlnb[...] += _rowsum(dvn)
        dvh = dvn * lng
        dv = rstd * (dvh - jnp.mean(dvh, axis=-1, keepdims=True) - vhat * jnp.mean(dvh * vhat, axis=-1, keepdims=True))
        dz_ref[:, D_SSM + D_POOL + D_SGU:D_IN] = (dv * _gelu_grad(zv, tv)).astype(BF16)

        @pl.when(i == n_tiles - 1)
        def _():
            mask = _group_mask(D_SSM, 2 * D_ST)
            fb = jnp.zeros((GRP, 2 * D_ST), F32)
            fc = jnp.zeros((GRP, 2 * D_ST), F32)
            for gidx in range(N_GRP):
                rows = slice(gidx * GRP, (gidx + 1) * GRP)
                fb = fb + jnp.where(mask[rows, :], accb_ref[rows, :], 0.0)
                fc = fc + jnp.where(mask[rows, :], accc_ref[rows, :], 0.0)
            o_dbt[...] = fb
            o_dct[...] = fc
            tri = (lax.broadcasted_iota(jnp.int32, (CHUNK, CHUNK), 0) >= lax.broadcasted_iota(jnp.int32, (CHUNK, CHUNK), 1))
            for h in range(HEADS):
                o_dws[h] = jnp.where(tri, accw_ref[h], 0.0)
            lane = lax.broadcasted_iota(jnp.int32, (1, 128), 1)
            acc = jnp.zeros((CHUNK, 128), F32)
            for h in range(HEADS):
                sh = jnp.sum(accm_ref[:, h * HEAD_DIM:(h + 1) * HEAD_DIM], axis=1, keepdims=True)
                acc = jnp.where(lane == h, sh, acc)
            o_dbsp[...] = acc

    def rowr(n):
        return pl.BlockSpec((tt, n), lambda i: (rev(i), 0))

    zprev_spec = pl.BlockSpec((HALO, D_IN), lambda i: (jnp.maximum(rev(i) * (tt // HALO) - 1, 0), 0))
    hsprev_spec = pl.BlockSpec((SCAN_BLK, 2 * D_ST), lambda i: (jnp.maximum(rev(i) * (tt // SCAN_BLK) - 1, 0), 0))
    small = [S((1, 2 * D_ST), F32), S((GRP, 2 * D_ST), F32), S((GRP, 2 * D_ST), F32), S((1, D_SSM), F32),
             S((1, D_SSM), F32), S((D_SSM, D_SSM), F32), S((D_POOL, D_POOL), F32), S((1, D_POOL), F32),
             S((1, D_SGU), F32), S((1, D_SGU), F32), S((HEADS, CHUNK, CHUNK), F32), S((CHUNK, 128), F32)]
    return _pallas(
        body, name=name, grid=(n_tiles,),
        in_specs=[rowr(D), rowr(D_IN), zprev_spec, rowr(2 * D_ST), hsprev_spec, rowr(D_SSM), _full((D, D)),
                  _full((32, D_ST)), _full((D_SSM, 2 * D_ST)), _full((D_SSM, 2 * D_ST)), _full((1, D_SSM)),
                  _full((D_SSM, D_SSM)), _full((1, D_SSM)), _full((D_POOL, D_POOL)), _full((1, D_POOL)),
                  _full((1, D_SGU)), _full((1, D_SGU)), _full((HEADS, CHUNK, CHUNK)), _full((HEADS, CHUNK, CHUNK)),
                  _full((CHUNK, D_SGU)), _full((tt, tt)), _full((tt, tt))],
        out_specs=[rowr(D_IN)] + [_full(s.shape) for s in small],
        out_shape=[S((T, D_IN), BF16)] + small,
        scratch_shapes=[pltpu.VMEM((tt, 2 * D_ST), F32), pltpu.VMEM((SCAN_BLK, 2 * D_ST), F32),
                        pltpu.VMEM((HALO, D_POOL), F32), pltpu.VMEM((tt + HALO, D_POOL), F32),
                        pltpu.VMEM((tt, D_SGU), F32), pltpu.VMEM((tt, D_SGU), F32),
                        pltpu.VMEM((D_SSM, 2 * D_ST), F32), pltpu.VMEM((D_SSM, 2 * D_ST), F32),
                        pltpu.VMEM((HEADS, CHUNK, CHUNK), F32), pltpu.VMEM((CHUNK, D_SGU), F32)],
        vmem_mb=56, exchange=exchange,
        operands=(dx1b, z, z, hs, hs, ys, wout, sc, bbd, cbd, dskip, wglu, bglu, wpool, pscale, lng, lnb, wsm, wsmt, bsp, *perm))


def inproj_fwd(x, g, w_t, name, exchange=None):
    T = x.shape[0]
    tt = TT_PROJ

    def body(x_ref, g_ref, w_ref, h_ref, z_ref):
        xn, _ = _rms(x_ref[...])
        h = (xn * g_ref[...]).astype(BF16)
        h_ref[...] = h
        z_ref[...] = _nt(h, w_ref[...])

    return _pallas(
        body, name=name, grid=(T // tt,),
        in_specs=[_row(tt, D), _full((1, D)), _full((D_IN, D))],
        out_specs=[_row(tt, D), _row(tt, D_IN)],
        out_shape=[S((T, D), BF16), S((T, D_IN), F32)],
        scratch_shapes=[], vmem_mb=40, operands=(x, g, w_t), exchange=exchange)


def inproj_bwd(dzb, x, g, w_t, dx1):
    T = x.shape[0]
    tt = TT_PROJ

    def body(dz_ref, x_ref, g_ref, w_ref, dx1_ref, dx_ref, dg_ref):
        @pl.when(pl.program_id(0) == 0)
        def _():
            dg_ref[...] = jnp.zeros_like(dg_ref)

        dh = _nn(dz_ref[...], w_ref[...])
        xn, r = _rms(x_ref[...])
        dg_ref[...] += _rowsum(dh * xn)
        dx_ref[...] = dx1_ref[...] + _rms_bwd(dh, xn, r, g_ref[...])

    return pl.pallas_call(
        body, name="inproj_bwd", grid=(T // tt,),
        in_specs=[_row(tt, D_IN), _row(tt, D), _full((1, D)), _full((D_IN, D)), _row(tt, D)],
        out_specs=[_row(tt, D), _full((1, D))],
        out_shape=[S((T, D), F32), S((1, D), F32)],
        compiler_params=_cp(40, 1),
    )(dzb, x, g, w_t, dx1)


def _load_weights(pairs, sem):
    @pl.when(pl.program_id(0) == 0)
    def _():
        copies = [pltpu.make_async_copy(src, dst, sem.at[k]) for k, (src, dst) in enumerate(pairs)]
        for cp in copies:
            cp.start()
        for cp in copies:
            cp.wait()


def ffn_fwd(x, ycat, wout, g, wg_t, wu_t, wd, name, exchange=None, head=None):
    T = x.shape[0]
    tt = TT_FFN
    any_spec = pl.BlockSpec(memory_space=pl.ANY)

    def body(*refs):
        if head is None:
            (x_ref, ycat_ref, g_ref, wout_hbm, wg_hbm, wu_hbm, wd_hbm,
             x1_ref, h_ref, gate_ref, up_ref, act_ref, x2_ref, wout_v, wg_v, wu_v, wd_v, sem) = refs
        else:
            (x_ref, ycat_ref, g_ref, t_ref, gf_ref, wout_hbm, wg_hbm, wu_hbm, wd_hbm,
             x1_ref, h_ref, gate_ref, up_ref, act_ref, x2_ref, lvec_ref, dgf_ref, wout_v, wg_v, wu_v, wd_v, sem) = refs
        _ffn_fwd_tile(x_ref, ycat_ref, g_ref, wout_hbm, wg_hbm, wu_hbm, wd_hbm, x1_ref, h_ref, gate_ref, up_ref, act_ref,
                      x2_ref, wout_v, wg_v, wu_v, wd_v, sem)
        if head is not None:
            @pl.when(pl.program_id(0) == 0)
            def _():
                lvec_ref[...] = jnp.zeros_like(lvec_ref)
                dgf_ref[...] = jnp.zeros_like(dgf_ref)

            xn, r = _rms(x2_ref[...])
            gf = gf_ref[...]
            err = xn * gf - t_ref[...]
            lvec_ref[...] += _rowsum(err * err)
            dy = err * (1.0 / D)
            dgf_ref[...] += _rowsum(dy * xn)
            x2_ref[...] = _rms_bwd(dy, xn, r, gf)

    def _ffn_fwd_tile(x_ref, ycat_ref, g_ref, wout_hbm, wg_hbm, wu_hbm, wd_hbm,
                      x1_ref, h_ref, gate_ref, up_ref, act_ref, x2_ref, wout_v, wg_v, wu_v, wd_v, sem):
        _load_weights([(wout_hbm, wout_v), (wg_hbm, wg_v), (wu_hbm, wu_v), (wd_hbm, wd_v)], sem)
        x1 = x_ref[...] + _nn(ycat_ref[...], wout_v[...])
        x1_ref[...] = x1
        xn, _ = _rms(x1)
        h = (xn * g_ref[...]).astype(BF16)
        h_ref[...] = h
        gate = _nt(h, wg_v[...])
        up = _nt(h, wu_v[...])
        gate_ref[...] = gate.astype(BF16)
        up_ref[...] = up.astype(BF16)
        act = (gate * jax.nn.sigmoid(gate) * up).astype(BF16)
        act_ref[...] = act
        x2_ref[...] = x1 + _nn(act, wd_v[...])

    with_head = head is not None
    return _pallas(
        body, name=name, grid=(T // tt,),
        in_specs=[_row(tt, D), _row(tt, D), _full((1, D))] + ([_row(tt, D), _full((1, D))] if with_head else [])
        + [any_spec, any_spec, any_spec, any_spec],
        out_specs=[_row(tt, D), _row(tt, D), _row(tt, D_FF), _row(tt, D_FF), _row(tt, D_FF), _row(tt, D)]
        + ([_full((1, D)), _full((1, D))] if with_head else []),
        out_shape=[S((T, D), F32), S((T, D), BF16), S((T, D_FF), BF16), S((T, D_FF), BF16), S((T, D_FF), BF16),
                   S((T, D), F32)] + ([S((1, D), F32), S((1, D), F32)] if with_head else []),
        scratch_shapes=[pltpu.VMEM((D, D), BF16), pltpu.VMEM((D_FF, D), BF16), pltpu.VMEM((D_FF, D), BF16),
                        pltpu.VMEM((D_FF, D), BF16), pltpu.SemaphoreType.DMA((4,))],
        vmem_mb=56, operands=(x, ycat, g) + (tuple(head) if with_head else ()) + (wout, wg_t, wu_t, wd), exchange=exchange)


def ffn_bwd(dx2, x1, gate, up, g, wg_t, wu_t, wd, name, exchange=None):
    T = x1.shape[0]
    tt = TT_FFN
    any_spec = pl.BlockSpec(memory_space=pl.ANY)

    def body(dx2_ref, x1_ref, gate_ref, up_ref, g_ref, wg_hbm, wu_hbm, wd_hbm,
             dgu_ref, dx2b_ref, dx1_ref, dx1b_ref, dg_ref, wg_v, wu_v, wd_v, sem):
        _load_weights([(wg_hbm, wg_v), (wu_hbm, wu_v), (wd_hbm, wd_v)], sem)

        @pl.when(pl.program_id(0) == 0)
        def _():
            dg_ref[...] = jnp.zeros_like(dg_ref)

        dx2 = dx2_ref[...]
        dx2b = dx2.astype(BF16)
        dx2b_ref[...] = dx2b
        dact = _nt(dx2b, wd_v[...])
        gate = gate_ref[...].astype(F32)
        up = up_ref[...].astype(F32)
        sg = jax.nn.sigmoid(gate)
        dgate = (dact * up * (sg * (1.0 + gate * (1.0 - sg)))).astype(BF16)
        dup = (dact * gate * sg).astype(BF16)
        dgu_ref[:, 0:D_FF] = dgate
        dgu_ref[:, D_FF:2 * D_FF] = dup
        dh = _nn(dgate, wg_v[...]) + _nn(dup, wu_v[...])
        xn, r = _rms(x1_ref[...])
        dg_ref[...] += _rowsum(dh * xn)
        dx1 = dx2 + _rms_bwd(dh, xn, r, g_ref[...])
        dx1_ref[...] = dx1
        dx1b_ref[...] = dx1.astype(BF16)

    return _pallas(
        body, name=name, grid=(T // tt,),
        in_specs=[_row(tt, D), _row(tt, D), _row(tt, D_FF), _row(tt, D_FF), _full((1, D)), any_spec, any_spec, any_spec],
        out_specs=[_row(tt, 2 * D_FF), _row(tt, D), _row(tt, D), _row(tt, D), _full((1, D))],
        out_shape=[S((T, 2 * D_FF), BF16), S((T, D), BF16), S((T, D), F32), S((T, D), BF16), S((1, D), F32)],
        scratch_shapes=[pltpu.VMEM((D_FF, D), BF16), pltpu.VMEM((D_FF, D), BF16), pltpu.VMEM((D_FF, D), BF16),
                        pltpu.SemaphoreType.DMA((3,))],
        vmem_mb=56, operands=(dx2, x1, gate, up, g, wg_t, wu_t, wd), exchange=exchange)


def wgrad(a, b, tm, name, exchange=None, tk=TK_WGRAD):
    T, M = a.shape
    N = b.shape[1]
    tk = min(tk, T)
    n_k = T // tk

    def body(a_ref, b_ref, o_ref, acc_ref):
        k = pl.program_id(1)

        @pl.when(k == 0)
        def _():
            acc_ref[...] = jnp.zeros_like(acc_ref)

        acc_ref[...] += _tn(a_ref[...], b_ref[...])

        @pl.when(k == n_k - 1)
        def _():
            o_ref[...] = acc_ref[...].astype(BF16)

    (out,), got = _pallas(
        body, name=name, grid=(M // tm, n_k),
        in_specs=[pl.BlockSpec((tk, tm), lambda m, k: (k, m)), pl.BlockSpec((tk, N), lambda m, k: (k, 0))],
        out_specs=[pl.BlockSpec((tm, N), lambda m, k: (m, 0))],
        out_shape=[S((M, N), BF16)],
        scratch_shapes=[pltpu.VMEM((tm, N), F32)],
        vmem_mb=48, operands=(a, b), exchange=exchange)
    return out if exchange is None else (out, got)


def _mesh_place():
    x, y, c = lax.axis_index("x"), lax.axis_index("y"), lax.axis_index("c")
    return x, y, c, 4 * x + 2 * y + c


def _peer(x, y, c, k):
    px = 1 - x if k & 4 else x
    py = 1 - y if k & 2 else y
    pc = 1 - c if k & 1 else c
    return (px, py, pc), 4 * px + 2 * py + pc


class _Exchange:
    SAME_CORE = (2, 4, 6)

    def __init__(self, gather=(), scatter=()):
        self.entries = [(a, None, a.shape[0]) for a in gather] + [(a, off, rows) for a, off, rows in scatter]
        self.n_gather = len(gather)

    @property
    def n(self):
        return len(self.entries)

    def operands(self):
        return [e[0] for e in self.entries]

    def out_shapes(self):
        return [S((N_DEV, rows, a.shape[1]), a.dtype) for a, _, rows in self.entries]

    def sems(self):
        return [pltpu.SemaphoreType.DMA((self.n, N_DEV)), pltpu.SemaphoreType.DMA((self.n, N_DEV)),
                pltpu.SemaphoreType.DMA((self.n,))]

    def _src(self, ref, e, idx):
        _, off, rows = self.entries[e]
        if off is None:
            return ref
        return ref.at[pl.ds(pl.multiple_of(off + idx * rows, 16), rows)]

    def _masks(self, e):
        return (1,) + self.SAME_CORE if e < self.n_gather else tuple(range(1, N_DEV))

    def _copy(self, ins, outs, sems, e, k, sending, passing_on=False):
        send_sems, recv_sems, _ = sems
        x, y, c, me = _mesh_place()
        peer, pidx = _peer(x, y, c, k)
        if passing_on:
            return pltpu.make_async_remote_copy(
                src_ref=outs[e].at[pidx], dst_ref=outs[e].at[pidx], send_sem=send_sems.at[e, k | 1],
                recv_sem=recv_sems.at[e, k | 1], device_id=_peer(x, y, c, 1)[0], device_id_type=pl.DeviceIdType.MESH)
        return pltpu.make_async_remote_copy(
            src_ref=self._src(ins[e], e, pidx), dst_ref=outs[e].at[me if sending else pidx], send_sem=send_sems.at[e, k],
            recv_sem=recv_sems.at[e, k], device_id=peer, device_id_type=pl.DeviceIdType.MESH)

    def _local(self, ins, outs, sems):
        me = _mesh_place()[3]
        return [pltpu.make_async_copy(self._src(ins[e], e, me), outs[e].at[me], sems[2].at[e]) for e in range(self.n)]

    def start(self, ins, outs, sems):
        for cp in self._local(ins, outs, sems):
            cp.start()
        for k in range(1, N_DEV):
            for e in range(self.n):
                if k in self._masks(e):
                    self._copy(ins, outs, sems, e, k, True).start()

    def forward(self, ins, outs, sems):
        for k in self.SAME_CORE:
            for e in range(self.n_gather):
                self._copy(ins, outs, sems, e, k, False).wait_recv()
                self._copy(ins, outs, sems, e, k, False, passing_on=True).start()

    def wait(self, ins, outs, sems):
        for k in range(1, N_DEV):
            for e in range(self.n):
                if e >= self.n_gather or k % 2:
                    self._copy(ins, outs, sems, e, k, False).wait_recv()
        for k in range(1, N_DEV):
            for e in range(self.n):
                self._copy(ins, outs, sems, e, k, True).wait_send()
        for cp in self._local(ins, outs, sems):
            cp.wait()


def _pallas(body, *, name, grid, in_specs, out_specs, out_shape, scratch_shapes, vmem_mb, operands, exchange=None):
    n_in, n_out, n_scr = len(in_specs), len(out_specs), len(scratch_shapes)
    n_steps = math.prod(grid)
    if exchange is None:
        res = pl.pallas_call(body, name=name, grid=grid, in_specs=in_specs, out_specs=out_specs, out_shape=out_shape,
                             scratch_shapes=scratch_shapes, compiler_params=_cp(vmem_mb, len(grid)))(*operands)
        return list(res), []
    ex = exchange

    def hosted(*refs):
        ins, ex_in = refs[:n_in], refs[n_in:n_in + ex.n]
        outs = refs[n_in + ex.n:n_in + ex.n + n_out]
        ex_out = refs[n_in + ex.n + n_out:n_in + 2 * ex.n + n_out]
        scr = refs[n_in + 2 * ex.n + n_out:]
        sems = scr[n_scr:]
        step = pl.program_id(0)
        for axis in range(1, len(grid)):
            step = step * grid[axis] + pl.program_id(axis)

        @pl.when(step == 0)
        def _():
            ex.start(ex_in, ex_out, sems)

        body(*ins, *outs, *scr[:n_scr])

        if ex.n_gather:
            @pl.when(step == max(n_steps - 1 - max(2, n_steps // 8), 0))
            def _():
                ex.forward(ex_in, ex_out, sems)

        @pl.when(step == n_steps - 1)
        def _():
            ex.wait(ex_in, ex_out, sems)

    any_spec = pl.BlockSpec(memory_space=pl.ANY)
    res = pl.pallas_call(
        hosted, name=name, grid=grid, in_specs=list(in_specs) + [any_spec] * ex.n,
        out_specs=list(out_specs) + [any_spec] * ex.n, out_shape=list(out_shape) + ex.out_shapes(),
        scratch_shapes=list(scratch_shapes) + ex.sems(), compiler_params=_cp(vmem_mb, len(grid)),
    )(*operands, *ex.operands())
    return list(res[:n_out]), list(res[n_out:])


def exchange_only(ex, name):
    def body(*refs):
        ins, outs, sems = refs[:ex.n], refs[ex.n:2 * ex.n], refs[2 * ex.n:]
        ex.start(ins, outs, sems)
        ex.forward(ins, outs, sems)
        ex.wait(ins, outs, sems)

    any_spec = pl.BlockSpec(memory_space=pl.ANY)
    return list(pl.pallas_call(body, name=name, in_specs=[any_spec] * ex.n, out_specs=[any_spec] * ex.n,
                               out_shape=ex.out_shapes(), scratch_shapes=ex.sems())(*ex.operands()))


def _adamw(w, g, m, v):
    m = ADAM_B1 * m + (1.0 - ADAM_B1) * g
    v = ADAM_B2 * v + (1.0 - ADAM_B2) * (g * g)
    m_hat = m / (1.0 - ADAM_B1 ** ADAM_STEP)
    v_hat = v / (1.0 - ADAM_B2 ** ADAM_STEP)
    delta = -ADAM_LR * (m_hat / (jnp.sqrt(v_hat) + ADAM_EPS) + ADAM_WD * w)
    return delta, m, v


def _sum_parts(p_ref, rows=slice(None)):
    g = p_ref[0, rows].astype(F32)
    for k in range(1, N_DEV):
        g = g + p_ref[k, rows].astype(F32)
    return g


def adamw_layers(parts, w, m, v, name):
    n_l = len(parts)

    def body(*refs):
        p_refs = refs[:n_l]
        w_ref, m_ref, v_ref, g_out, d_out, m_out, v_out = refs[n_l:]
        for l in range(n_l):
            g = _sum_parts(p_refs[l])
            g_out[l] = g
            d_out[l], m_out[l], v_out[l] = _adamw(w_ref[l], g, m_ref[l], v_ref[l])

    return pl.pallas_call(
        body, name=name, out_shape=[S(w.shape, F32)] * 4, compiler_params=_cp(48),
    )(*parts, w, m, v)


def adamw_segments(parts, segments, w, m, v, name):
    n_p = len(parts)

    def body(*refs):
        p_refs = refs[:n_p]
        w_ref, m_ref, v_ref, g_out, d_out, m_out, v_out = refs[n_p:]
        for part, src, dst, rows in segments:
            g = _sum_parts(p_refs[part], slice(src, src + rows))
            to = slice(dst, dst + rows)
            g_out[to] = g
            d_out[to], m_out[to], v_out[to] = _adamw(w_ref[to], g, m_ref[to], v_ref[to])

    return pl.pallas_call(
        body, name=name, out_shape=[S(w.shape, F32)] * 4, compiler_params=_cp(48),
    )(*parts, w, m, v)


SMALL_LAYER = ("g_mix", "A_re", "A_im", "log_dt", "B_re", "B_im", "C_re", "C_im", "D_skip", "b_glu", "w_pool",
               "pool_scale", "sgu_ln_g", "sgu_ln_b", "w_spatial", "b_spatial", "g_ffn")
BIG_NAMES = ("w_in", "w_glu", "w_out", "w_gate", "w_up", "w_down")
COLUMN_SHARDED = ("w_in", "w_gate", "w_up")
WEIGHT_ORDER = ("g_mix", "w_in", "A_re", "A_im", "log_dt", "B_re", "B_im", "C_re", "C_im", "D_skip", "w_glu", "b_glu",
                "w_pool", "pool_scale", "sgu_ln_g", "sgu_ln_b", "w_spatial", "b_spatial", "w_out", "g_ffn", "w_gate",
                "w_up", "w_down", "g_final")
SEG = 1024


def _pack(arrays):
    parts = []
    for a in arrays:
        flat = a.reshape(-1)
        parts.append(jnp.pad(flat, (0, (-flat.shape[0]) % SEG)))
    return jnp.concatenate(parts).reshape(-1, 128)


def _state_rows(p):
    return p.reshape(1, D_ST)


def _chan_by_state(p):
    return jnp.transpose(p, (2, 0, 1)).reshape(GRP, D_ST)


def _chan_by_state_c(p):
    return jnp.transpose(p, (1, 0, 2)).reshape(GRP, D_ST)


def kernel(x, g_mix, w_in, A_re, A_im, log_dt, B_re, B_im, C_re, C_im, D_skip, w_glu, b_glu, w_pool, pool_scale, sgu_ln_g, sgu_ln_b, w_spatial, b_spatial, w_out, g_ffn, w_gate, w_up, w_down, g_final, loss_target, m_g_mix, m_w_in, m_A_re, m_A_im, m_log_dt, m_B_re, m_B_im, m_C_re, m_C_im, m_D_skip, m_w_glu, m_b_glu, m_w_pool, m_pool_scale, m_sgu_ln_g, m_sgu_ln_b, m_w_spatial, m_b_spatial, m_w_out, m_g_ffn, m_w_gate, m_w_up, m_w_down, m_g_final, v_g_mix, v_w_in, v_A_re, v_A_im, v_log_dt, v_B_re, v_B_im, v_C_re, v_C_im, v_D_skip, v_w_glu, v_b_glu, v_w_pool, v_pool_scale, v_sgu_ln_g, v_sgu_ln_b, v_w_spatial, v_b_spatial, v_w_out, v_g_ffn, v_w_gate, v_w_up, v_w_down, v_g_final):
    args = dict(locals())
    W = {n: args[n] for n in WEIGHT_ORDER}
    M = {n: args["m_" + n] for n in WEIGHT_ORDER}
    V = {n: args["v_" + n] for n in WEIGHT_ORDER}
    n_layers = g_mix.shape[0]
    x0 = x[0]
    target = loss_target[0]

    def my_rows(name, l):
        w = W[name][l]
        return (w.T if name in COLUMN_SHARDED else w).astype(BF16)

    full_w = [dict() for _ in range(n_layers)]

    def gather_of(*which):
        return _Exchange(gather=[my_rows(n, l) for n, l in which])

    def keep_gathered(which, arrays):
        for (n, l), a in zip(which, arrays):
            full_w[l][n] = a.reshape(-1, a.shape[-1])

    tri = jnp.tril(jnp.ones((CHUNK, CHUNK), bool))
    perm = _interleave_matrices(TT_MIX)
    consts = []
    for l in range(n_layers):
        a_re, a_im = _state_rows(A_re[l]), _state_rows(A_im[l])
        ldt = jnp.repeat(log_dt[l], N_STATE).reshape(1, D_ST)
        b_re_t, b_im_t = _chan_by_state(B_re[l]), _chan_by_state(B_im[l])
        sc, bbd, cbd = s5_prepare(a_re, a_im, ldt, b_re_t, b_im_t, _chan_by_state_c(C_re[l]), _chan_by_state_c(C_im[l]))
        wsm = jnp.where(tri[None], w_spatial[l], 0.0)
        wpool_bd = jnp.zeros((D_POOL, D_POOL), F32)
        for gi in range(len(POOL_WINDOWS)):
            wpool_bd = wpool_bd.at[gi * 64:(gi + 1) * 64, gi * 64:(gi + 1) * 64].set(w_pool[l, gi])
        consts.append(dict(
            disc=(a_re, a_im, ldt, b_re_t, b_im_t), sc=sc, bbd=bbd, cbd=cbd,
            dskip=D_skip[l].reshape(1, D_SSM), bglu=b_glu[l].reshape(1, D_SSM),
            wpool=wpool_bd.astype(BF16), pscale=pool_scale[l].reshape(1, D_POOL),
            lng=sgu_ln_g[l].reshape(1, D_SGU), lnb=sgu_ln_b[l].reshape(1, D_SGU),
            wsm=wsm.astype(BF16), wsmt=jnp.transpose(wsm, (0, 2, 1)).astype(BF16),
            bsp=jnp.repeat(b_spatial[l].T, HEAD_DIM, axis=1),
            gmix=g_mix[l].reshape(1, D), gffn=g_ffn[l].reshape(1, D)))

    def mixer_args(l):
        c = consts[l]
        return (c["bbd"], c["cbd"], c["dskip"], full_w[l]["w_glu"], c["bglu"], c["wpool"], c["pscale"], c["lng"], c["lnb"])

    first_needed = [("w_in", 0)]
    keep_gathered(first_needed, exchange_only(gather_of(*first_needed), "gather_first"))
    carried_fwd = {
        ("inproj", 0): [("w_glu", 0), ("w_out", 0)],
        ("mixer", 0): [("w_gate", 0), ("w_up", 0), ("w_down", 0)],
        ("ffn", 0): [("w_in", 1), ("w_glu", 1), ("w_out", 1), ("w_gate", 1)],
        ("mixer", 1): [("w_up", 1), ("w_down", 1)],
    }

    def carried(kind, l):
        which = carried_fwd.get((kind, l))
        return which, (gather_of(*which) if which else None)

    saved = []
    xl = x0
    for l in range(n_layers):
        c, fw = consts[l], full_w[l]
        which, ex = carried("inproj", l)
        (h, z), got = inproj_fwd(xl, c["gmix"], fw["w_in"], f"inproj_fwd_{l}", ex)
        keep_gathered(which or [], got)
        which, ex = carried("mixer", l)
        (ycat, hs, ys), got = mixer_fwd(z, c["sc"], *mixer_args(l), c["wsm"], c["bsp"], perm, f"mixer_fwd_{l}", ex)
        keep_gathered(which or [], got)
        which, ex = carried("ffn", l)
        head = (target, g_final.reshape(1, D)) if l == n_layers - 1 else None
        res, got = ffn_fwd(xl, ycat, fw["w_out"], c["gffn"], fw["w_gate"], fw["w_up"], fw["w_down"], f"ffn_fwd_{l}", ex, head)
        keep_gathered(which or [], got)
        x1, h2, gate, up, act, x2 = res[:6]
        saved.append(dict(x=xl, h=h, z=z, ycat=ycat, hs=hs, ys=ys, x1=x1, h2=h2, gate=gate, up=up, act=act))
        xl = x2
    dx, loss_vec, d_gfinal = xl, res[6], res[7]

    recv_big = {}
    recv_small = [None] * n_layers

    def keep_received(which, arrays):
        for key, a in zip(which, arrays):
            if key[0] == "small":
                recv_small[key[1]] = a
            else:
                recv_big[key] = a

    pending = None
    for l in reversed(range(n_layers)):
        c, fw, sv = consts[l], full_w[l], saved[l]
        (dgu, dx2b, dx1, dx1b, d_gffn), got = ffn_bwd(dx, sv["x1"], sv["gate"], sv["up"], c["gffn"], fw["w_gate"], fw["w_up"],
                                                     fw["w_down"], f"ffn_bwd_{l}", pending[1] if pending else None)
        if pending:
            keep_received(pending[0], got)
        g_gu = wgrad(dgu, sv["h2"], D_FF // 2, f"wgrad_gate_up_{l}")
        g_down = wgrad(sv["act"], dx2b, D_FF // 2, f"wgrad_down_{l}")
        g_out = wgrad(sv["ycat"], dx1b, D, f"wgrad_out_{l}")
        ffn_rows = D_FF // N_DEV
        ex = _Exchange(scatter=[(g_gu, 0, ffn_rows), (g_gu, D_FF, ffn_rows), (g_down, 0, ffn_rows), (g_out, 0, D // N_DEV)])
        (dzb, da, dbt, dct, dd, dbglu, dwglu, dwpool, dpscale, dlng, dlnb, dws, dbsp), got = mixer_bwd(
            dx1b, sv["z"], sv["hs"], sv["ys"], fw["w_out"], c["sc"], *mixer_args(l), c["wsm"], c["wsmt"], c["bsp"],
            perm, f"mixer_bwd_{l}", ex)
        keep_received([("w_gate", l), ("w_up", l), ("w_down", l), ("w_out", l)], got)
        dx, d_gmix = inproj_bwd(dzb, sv["x"], c["gmix"], fw["w_in"], dx1)
        d_are, d_aim, d_ldt, d_bre_t, d_bim_t = s5_param_bwd(*c["disc"], da, dbt)
        small = dict(
            g_mix=d_gmix.reshape(D), A_re=d_are.reshape(N_GRP, N_STATE), A_im=d_aim.reshape(N_GRP, N_STATE),
            log_dt=d_ldt[0, :N_GRP],
            B_re=jnp.transpose(d_bre_t.reshape(GRP, N_GRP, N_STATE), (1, 2, 0)),
            B_im=jnp.transpose(d_bim_t.reshape(GRP, N_GRP, N_STATE), (1, 2, 0)),
            C_re=jnp.transpose(dct[:, :D_ST].reshape(GRP, N_GRP, N_STATE), (1, 0, 2)),
            C_im=-jnp.transpose(dct[:, D_ST:].reshape(GRP, N_GRP, N_STATE), (1, 0, 2)),
            D_skip=dd.reshape(D_SSM), b_glu=dbglu.reshape(D_SSM),
            w_pool=jnp.stack([dwpool[gi * 64:(gi + 1) * 64, gi * 64:(gi + 1) * 64] for gi in range(len(POOL_WINDOWS))]),
            pool_scale=dpscale.reshape(D_POOL), sgu_ln_g=dlng.reshape(D_SGU), sgu_ln_b=dlnb.reshape(D_SGU),
            w_spatial=dws, b_spatial=dbsp[:, :HEADS].T, g_ffn=d_gffn.reshape(D))
        packed = [small[n] for n in SMALL_LAYER] + ([d_gfinal.reshape(D), loss_vec.reshape(D)] if l == 0 else [])
        g_in, got = wgrad(dzb, sv["h"], D_IN, f"wgrad_in_{l}", tk=TK_WGRAD // 4, exchange=_Exchange(
            gather=[_pack(packed)], scatter=[(dwglu.astype(BF16), 0, D_SSM // N_DEV)]))
        keep_received([("small", l), ("w_glu", l)], got)
        pending = ([("w_in", l)], _Exchange(scatter=[(g_in, 0, D_IN // N_DEV)]))
    grad_x = dx
    keep_received(pending[0], exchange_only(pending[1], "exchange_last"))

    out = {}
    for n in BIG_NAMES:
        tr = (lambda a: jnp.transpose(a, (0, 2, 1))) if n in COLUMN_SHARDED else (lambda a: a)
        res = adamw_layers([recv_big[(n, l)] for l in range(n_layers)], tr(W[n]), tr(M[n]), tr(V[n]), f"adamw_{n}")
        out[n] = [tr(r) for r in res]

    seg_rows = [(-(-math.prod(W[n].shape[1:]) // SEG)) * (SEG // 128) for n in SMALL_LAYER]
    segments, src, dst = [], 0, 0
    for rows in seg_rows:
        segments += [(l, src, dst + l * rows, rows) for l in range(n_layers)]
        src += rows
        dst += n_layers * rows
    tile_rows = SEG // 128
    segments += [(0, src, dst, tile_rows), (0, src + tile_rows, dst + tile_rows, tile_rows)]

    def pack_params(P):
        parts = []
        for n, rows in zip(SMALL_LAYER, seg_rows):
            flat = P[n].reshape(n_layers, -1)
            parts.append(jnp.pad(flat, ((0, 0), (0, rows * 128 - flat.shape[1]))).reshape(-1))
        return jnp.concatenate(parts + [P["g_final"], jnp.zeros((SEG,), F32)]).reshape(-1, 128)

    res = adamw_segments(recv_small, segments, pack_params(W), pack_params(M), pack_params(V), "adamw_small")
    for j in range(4):
        flat, off = res[j].reshape(-1), 0
        for n, rows in zip(SMALL_LAYER, seg_rows):
            size = math.prod(W[n].shape[1:])
            piece = flat[off:off + n_layers * rows * 128].reshape(n_layers, rows * 128)[:, :size].reshape(W[n].shape)
            out.setdefault(n, []).append(piece)
            off += n_layers * rows * 128
        out.setdefault("g_final", []).append(flat[off:off + D])
        if j == 0:
            loss = (0.5 / D) * jnp.sum(flat[off + SEG:off + SEG + D])

    return (loss, grad_x[None], *[out[n][0] for n in WEIGHT_ORDER], *[out[n][1] for n in WEIGHT_ORDER],
            *[out[n][2] for n in WEIGHT_ORDER], *[out[n][3] for n in WEIGHT_ORDER])
```

```python
import functools
import math

import jax
import jax.numpy as jnp
from jax import lax
from jax.experimental import pallas as pl
from jax.experimental.pallas import tpu as pltpu

F32 = jnp.float32
BF16 = jnp.bfloat16
S = jax.ShapeDtypeStruct

N_DEV = 8
D = 1024
D_SSM = 384
N_GRP = 24
GRP = 16
N_STATE = 64
D_ST = N_GRP * N_STATE
D_POOL = 256
POOL_WINDOWS = (2, 4, 8, 16)
HALO = 16
D_SGU = 384
HEADS = 6
HEAD_DIM = 64
CHUNK = 128
D_IN = 1408
D_FF = 2816
EPS = 1e-6
SCAN_BLK = 8

ADAM_LR = 0.001
ADAM_B1 = 0.9
ADAM_B2 = 0.999
ADAM_EPS = 1e-08
ADAM_WD = 0.01
ADAM_STEP = 10

GELU_C0 = math.sqrt(2.0 / math.pi)
GELU_C1 = 0.044715

TT_MIX = 256
SEG_LEN = TT_MIX // SCAN_BLK
TT_FFN = 256
TT_PROJ = 512
TK_WGRAD = 2048
VMEM_MB = 2 ** 20


def _cp(vmem_mb, grid_dims=0):
    kw = dict(vmem_limit_bytes=int(vmem_mb * VMEM_MB))
    if grid_dims:
        kw["dimension_semantics"] = ("arbitrary",) * grid_dims
    return pltpu.CompilerParams(**kw)


def _row(tt, n):
    return pl.BlockSpec((tt, n), lambda i: (i, 0))


def _full(shape):
    nd = len(shape)
    return pl.BlockSpec(shape, lambda *_: (0,) * nd)


def _nn(a, b):
    return jnp.dot(a, b, preferred_element_type=F32)


def _nt(a, b):
    return lax.dot_general(a, b, (((1,), (1,)), ((), ())), preferred_element_type=F32)


def _tn(a, b):
    return lax.dot_general(a, b, (((0,), (0,)), ((), ())), preferred_element_type=F32)


def _rowsum(x):
    return jnp.sum(x, axis=0, keepdims=True)


def _rms(x):
    r = lax.rsqrt(jnp.mean(x * x, axis=-1, keepdims=True) + EPS)
    return x * r, r


def _rms_bwd(dy, xn, r, g):
    dyg = dy * g
    return r * (dyg - xn * jnp.mean(dyg * xn, axis=-1, keepdims=True))


def _sigmoid(x):
    return pl.reciprocal(1.0 + jnp.exp(-x), approx=True)


def _gelu(x):
    s = _sigmoid(x * (2.0 * GELU_C0 + (2.0 * GELU_C0 * GELU_C1) * (x * x)))
    return x * s, s


def _gelu_grad(x, s):
    return s * (1.0 + x * (1.0 - s) * (2.0 * GELU_C0 + (6.0 * GELU_C0 * GELU_C1) * (x * x)))


def _discretise(a_re, a_im, ldt, b_re, b_im):
    dt = jnp.exp(ldt)
    mag = jnp.exp(a_re * dt)
    ar = mag * jnp.cos(a_im * dt)
    ai = mag * jnp.sin(a_im * dt)
    den = a_re * a_re + a_im * a_im
    f_re = ((ar - 1.0) * a_re + ai * a_im) / den
    f_im = (ai * a_re - (ar - 1.0) * a_im) / den
    bb_re = f_re * b_re - f_im * b_im
    bb_im = f_re * b_im + f_im * b_re
    return ar, ai, bb_re, bb_im


def _group_mask(rows, cols):
    r = lax.broadcasted_iota(jnp.int32, (rows, cols), 0) // GRP
    c = lax.broadcasted_iota(jnp.int32, (rows, cols), 1)
    c = jnp.where(c >= D_ST, c - D_ST, c) // N_STATE
    return r == c


def s5_prepare(a_re, a_im, ldt, b_re_t, b_im_t, c_re_t, c_im_t):
    def body(are_ref, aim_ref, ldt_ref, bre_ref, bim_ref, cre_ref, cim_ref, sc_ref, bbd_ref, cbd_ref):
        ar, ai, bb_re, bb_im = _discretise(are_ref[...], aim_ref[...], ldt_ref[...], bre_ref[...], bim_ref[...])
        mask = _group_mask(D_SSM, 2 * D_ST)
        bb = jnp.concatenate([jnp.tile(bb_re, (N_GRP, 1)), jnp.tile(bb_im, (N_GRP, 1))], axis=1)
        bbd_ref[...] = jnp.where(mask, bb, 0.0).astype(BF16)
        cc = jnp.concatenate([jnp.tile(cre_ref[...], (N_GRP, 1)), -jnp.tile(cim_ref[...], (N_GRP, 1))], axis=1)
        cbd_ref[...] = jnp.where(mask, cc, 0.0).astype(BF16)
        pr, pi = ar, ai
        for _ in range(SEG_LEN - 1):
            pr, pi = pr * ar - pi * ai, pr * ai + pi * ar
        for k, v in enumerate((ar, ai, pr, pi)):
            sc_ref[8 * k:8 * k + 8, :] = jnp.broadcast_to(v, (SCAN_BLK, D_ST))

    return pl.pallas_call(
        body, name="s5_prepare",
        out_shape=[S((32, D_ST), F32), S((D_SSM, 2 * D_ST), BF16), S((D_SSM, 2 * D_ST), BF16)],
        compiler_params=_cp(40),
    )(a_re, a_im, ldt, b_re_t, b_im_t, c_re_t, c_im_t)


def s5_param_bwd(a_re, a_im, ldt, b_re_t, b_im_t, da, dbt):
    def body(are_ref, aim_ref, ldt_ref, bre_ref, bim_ref, da_ref, dbt_ref, o_are, o_aim, o_ldt, o_bre, o_bim):
        _, vjp = jax.vjp(_discretise, are_ref[...], aim_ref[...], ldt_ref[...], bre_ref[...], bim_ref[...])
        da = da_ref[...]
        dbt = dbt_ref[...]
        g_are, g_aim, g_ldt, g_bre, g_bim = vjp((da[:, :D_ST], da[:, D_ST:], dbt[:, :D_ST], dbt[:, D_ST:]))
        o_are[...] = g_are
        o_aim[...] = g_aim
        o_bre[...] = g_bre
        o_bim[...] = g_bim
        grp = lax.broadcasted_iota(jnp.int32, (1, D_ST), 1) // N_STATE
        lane = lax.broadcasted_iota(jnp.int32, (1, 128), 1)
        out = jnp.zeros((1, 128), F32)
        for g in range(N_GRP):
            out = jnp.where(lane == g, jnp.sum(jnp.where(grp == g, g_ldt, 0.0), axis=1, keepdims=True), out)
        o_ldt[...] = out

    return pl.pallas_call(
        body, name="s5_param_bwd",
        out_shape=[S((1, D_ST), F32), S((1, D_ST), F32), S((1, 128), F32), S((GRP, D_ST), F32), S((GRP, D_ST), F32)],
        compiler_params=_cp(16),
    )(a_re, a_im, ldt, b_re_t, b_im_t, da, dbt)


_CH = ((0, 256), (256, D_SSM))
_ST = ((0, 1024), (1024, D_ST))


def _bd_expand(xb, w_ref, out_ref):
    for (c0, c1), (s0, s1) in zip(_CH, _ST):
        for half in (0, D_ST):
            out_ref[:, half + s0:half + s1] = _nn(xb[:, c0:c1], w_ref[c0:c1, half + s0:half + s1])


def _bd_contract(hb, w_ref):
    parts = []
    for (c0, c1), (s0, s1) in zip(_CH, _ST):
        parts.append(_nt(hb[:, s0:s1], w_ref[c0:c1, s0:s1]) + _nt(hb[:, D_ST + s0:D_ST + s1], w_ref[c0:c1, D_ST + s0:D_ST + s1]))
    return jnp.concatenate(parts, axis=1)


def _bd_accumulate(acc_ref, xb, hb):
    for (c0, c1), (s0, s1) in zip(_CH, _ST):
        for half in (0, D_ST):
            acc_ref[c0:c1, half + s0:half + s1] += _tn(xb[:, c0:c1], hb[:, half + s0:half + s1])


def _interleave_matrices(tt):
    r = lax.broadcasted_iota(jnp.int32, (tt, tt), 0)
    t = lax.broadcasted_iota(jnp.int32, (tt, tt), 1)
    p = (t == (r % SCAN_BLK) * (tt // SCAN_BLK) + r // SCAN_BLK).astype(BF16)
    return p, p.T


def _interleave_f32(p, x):
    hi = x.astype(BF16)
    lo = (x - hi.astype(F32)).astype(BF16)
    return _nn(p, hi) + _nn(p, lo)


def _scan_tile(buf_ref, sc_ref, carry_ref, n_blk, reverse):
    ar = sc_ref[0:8, :]
    ai = -sc_ref[8:16, :] if reverse else sc_ref[8:16, :]

    def rows(i):
        blk = (n_blk - 1 - i) if reverse else i
        return pl.ds(pl.multiple_of(blk * SCAN_BLK, SCAN_BLK), SCAN_BLK)

    def local(i, x):
        xr, xi = x
        r = rows(i)
        xr, xi = buf_ref[r, 0:D_ST] + ar * xr - ai * xi, buf_ref[r, D_ST:2 * D_ST] + ar * xi + ai * xr
        buf_ref[r, 0:D_ST] = xr
        buf_ref[r, D_ST:2 * D_ST] = xi
        return xr, xi

    zero = jnp.zeros((SCAN_BLK, D_ST), F32)
    end_r, end_i = lax.fori_loop(0, n_blk, local, (zero, zero), unroll=2)

    seg_r = sc_ref[16:17, :]
    seg_i = -sc_ref[24:25, :] if reverse else sc_ref[24:25, :]
    cr, ci = carry_ref[0:1, 0:D_ST], carry_ref[0:1, D_ST:2 * D_ST]
    sub = lax.broadcasted_iota(jnp.int32, (SCAN_BLK, D_ST), 0)
    in_r, in_i = zero, zero
    for s in (reversed(range(SCAN_BLK)) if reverse else range(SCAN_BLK)):
        in_r = jnp.where(sub == s, cr, in_r)
        in_i = jnp.where(sub == s, ci, in_i)
        cr, ci = end_r[s:s + 1, :] + seg_r * cr - seg_i * ci, end_i[s:s + 1, :] + seg_r * ci + seg_i * cr
    carry_ref[0:1, 0:D_ST] = cr
    carry_ref[0:1, D_ST:2 * D_ST] = ci

    def fix(i, d):
        dr, di = d
        dr, di = ar * dr - ai * di, ar * di + ai * dr
        r = rows(i)
        buf_ref[r, 0:D_ST] += dr
        buf_ref[r, D_ST:2 * D_ST] += di
        return dr, di

    lax.fori_loop(0, n_blk, fix, (in_r, in_i), unroll=2)


def _lane_windows(n):
    lane = lax.broadcasted_iota(jnp.int32, (1, n), 1)
    return lane // (D_POOL // len(POOL_WINDOWS))


def _select_window(grp, s2, s4, s8, s16):
    return jnp.where(grp == 0, s2, jnp.where(grp == 1, s4, jnp.where(grp == 2, s8, s16)))


def _pool_fwd(pbuf_ref, zb, halo, tile_idx, tt):
    pbuf_ref[0:HALO, :] = halo
    pbuf_ref[HALO:HALO + tt, :] = zb
    x = pbuf_ref[...]
    s2 = x + pltpu.roll(x, 1, axis=0)
    s4 = s2 + pltpu.roll(s2, 2, axis=0)
    s8 = s4 + pltpu.roll(s4, 4, axis=0)
    s16 = s8 + pltpu.roll(s8, 8, axis=0)
    grp = _lane_windows(D_POOL)
    win = _select_window(grp, s2, s4, s8, s16)[HALO:HALO + tt, :]
    width = _select_window(grp, 2.0, 4.0, 8.0, 16.0).astype(F32)
    pos = (tile_idx * tt + 1 + lax.broadcasted_iota(jnp.int32, (tt, 1), 0)).astype(F32)
    cnt = jnp.minimum(pos, width)
    return win / cnt - zb, cnt


def _sgu_fwd(zu, zv, lng, lnb, wsm_ref, bsp, mix_ref, tt):
    u, tu = _gelu(zu)
    v, tv = _gelu(zv)
    mu = jnp.mean(v, axis=-1, keepdims=True)
    vc = v - mu
    rstd = lax.rsqrt(jnp.mean(vc * vc, axis=-1, keepdims=True) + EPS)
    vhat = vc * rstd
    vnb = (vhat * lng + lnb).astype(BF16)
    for c in range(tt // CHUNK):
        rows = slice(c * CHUNK, (c + 1) * CHUNK)
        parts = [_nn(wsm_ref[h], vnb[rows, h * HEAD_DIM:(h + 1) * HEAD_DIM]) for h in range(HEADS)]
        mix_ref[rows, :] = jnp.concatenate(parts, axis=1) + bsp
    return u, tu, tv, vhat, rstd, vnb


def mixer_fwd(z, sc, bbd, cbd, dskip, wglu, bglu, wpool, pscale, lng, lnb, wsm, bsp, perm, name, exchange=None):
    T = z.shape[0]
    tt = TT_MIX
    n_tiles = T // tt

    def body(z_ref, sc_ref, bbd_ref, cbd_ref, dskip_ref, wglu_ref, bglu_ref, wpool_ref, pscale_ref, lng_ref, lnb_ref,
             wsm_ref, bsp_ref, p_ref, pt_ref, ycat_ref, hs_ref, ys_ref, carry_ref, halo_ref, pbuf_ref, mix_ref):
        i = pl.program_id(0)

        @pl.when(i == 0)
        def _():
            carry_ref[...] = jnp.zeros_like(carry_ref)
            halo_ref[...] = jnp.zeros_like(halo_ref)

        za = z_ref[:, 0:D_SSM]
        zb = z_ref[:, D_SSM:D_SSM + D_POOL]
        zu = z_ref[:, D_SSM + D_POOL:D_SSM + D_POOL + D_SGU]
        zv = z_ref[:, D_SSM + D_POOL + D_SGU:D_IN]
        p, pt = p_ref[...], pt_ref[...]
        za = _interleave_f32(p, za)
        _bd_expand(za.astype(BF16), bbd_ref, hs_ref)
        _scan_tile(hs_ref, sc_ref, carry_ref, tt // SCAN_BLK, reverse=False)
        y = _bd_contract(hs_ref[...].astype(BF16), cbd_ref) + dskip_ref[...] * za
        ys_ref[...] = y
        g, _ = _gelu(y)
        q = _nn(g.astype(BF16), wglu_ref[...]) + bglu_ref[...]
        ycat_ref[:, 0:D_SSM] = _nn(pt, (g * jax.nn.sigmoid(q)).astype(BF16)).astype(BF16)
        pooled, _ = _pool_fwd(pbuf_ref, zb, halo_ref[...], i, tt)
        halo_ref[...] = zb[tt - HALO:tt, :]
        ycat_ref[:, D_SSM:D_SSM + D_POOL] = (_nn(pooled.astype(BF16), wpool_ref[...]) * pscale_ref[...]).astype(BF16)
        u, _, _, _, _, _ = _sgu_fwd(zu, zv, lng_ref[...], lnb_ref[...], wsm_ref, bsp_ref[...], mix_ref, tt)
        ycat_ref[:, D_SSM + D_POOL:D] = (u * mix_ref[...]).astype(BF16)

    return _pallas(
        body, name=name, grid=(n_tiles,),
        in_specs=[_row(tt, D_IN), _full((32, D_ST)), _full((D_SSM, 2 * D_ST)), _full((D_SSM, 2 * D_ST)),
                  _full((1, D_SSM)), _full((D_SSM, D_SSM)), _full((1, D_SSM)), _full((D_POOL, D_POOL)),
                  _full((1, D_POOL)), _full((1, D_SGU)), _full((1, D_SGU)), _full((HEADS, CHUNK, CHUNK)),
                  _full((CHUNK, D_SGU)), _full((tt, tt)), _full((tt, tt))],
        out_specs=[_row(tt, D), _row(tt, 2 * D_ST), _row(tt, D_SSM)],
        out_shape=[S((T, D), BF16), S((T, 2 * D_ST), F32), S((T, D_SSM), F32)],
        scratch_shapes=[pltpu.VMEM((SCAN_BLK, 2 * D_ST), F32), pltpu.VMEM((HALO, D_POOL), F32),
                        pltpu.VMEM((tt + HALO, D_POOL), F32), pltpu.VMEM((tt, D_SGU), F32)],
        vmem_mb=48, operands=(z, sc, bbd, cbd, dskip, wglu, bglu, wpool, pscale, lng, lnb, wsm, bsp, *perm),
        exchange=exchange)


def mixer_bwd(dx1b, z, hs, ys, dz, wout_a, sc, bbd, cbd, dskip, wglu, bglu, perm, name, exchange=None):
    T = z.shape[0]
    tt = TT_MIX
    n_tiles = T // tt

    def rev(i):
        return n_tiles - 1 - i

    def body(dx_ref, z_ref, hs_ref, hsprev_ref, ys_ref, dz_in, wout_ref, sc_ref, bbd_ref, cbd_ref, dskip_ref,
             wglu_ref, bglu_ref, p_ref, pt_ref, dz_ref, o_da, o_dbt, o_dct, o_dd, o_dbglu, o_dwglu,
             gbuf_ref, carry_ref, accb_ref, accc_ref):
        i = pl.program_id(0)
        tile = rev(i)

        @pl.when(i == 0)
        def _():
            carry_ref[...] = jnp.zeros_like(carry_ref)
            accb_ref[...] = jnp.zeros_like(accb_ref)
            accc_ref[...] = jnp.zeros_like(accc_ref)
            for o in (o_da, o_dd, o_dbglu, o_dwglu):
                o[...] = jnp.zeros_like(o)

        p, pt = p_ref[...], pt_ref[...]
        d_a = _nt(_nn(p, dx_ref[...]).astype(BF16), wout_ref[...])
        za = _interleave_f32(p, z_ref[...])
        first_tile = (tile > 0).astype(F32)

        y = ys_ref[...]
        g, tg = _gelu(y)
        gb = g.astype(BF16)
        sg = jax.nn.sigmoid(_nn(gb, wglu_ref[...]) + bglu_ref[...])
        dq = d_a * g * sg * (1.0 - sg)
        dqb = dq.astype(BF16)
        o_dbglu[...] += _rowsum(dq)
        o_dwglu[...] += _tn(gb, dqb)
        dy = (d_a * sg + _nt(dqb, wglu_ref[...])) * _gelu_grad(y, tg)
        o_dd[...] += _rowsum(dy * za)
        dyb = dy.astype(BF16)
        _bd_accumulate(accc_ref, dyb, hs_ref[...].astype(BF16))
        _bd_expand(dyb, cbd_ref, gbuf_ref)
        _scan_tile(gbuf_ref, sc_ref, carry_ref, tt // SCAN_BLK, reverse=True)
        hprev = hsprev_ref[SCAN_BLK - 1:SCAN_BLK, :] * first_tile
        sub = lax.broadcasted_iota(jnp.int32, (SCAN_BLK, 1), 0)
        edge = jnp.where(sub == 0, hprev, pltpu.roll(hs_ref[tt - SCAN_BLK:tt, :], 1, axis=0))

        def da_terms(gr, gi, hr, hi):
            return _rowsum(gr * hr + gi * hi), _rowsum(gi * hr - gr * hi)

        body_re, body_im = da_terms(gbuf_ref[SCAN_BLK:tt, 0:D_ST], gbuf_ref[SCAN_BLK:tt, D_ST:],
                                    hs_ref[0:tt - SCAN_BLK, 0:D_ST], hs_ref[0:tt - SCAN_BLK, D_ST:])
        edge_re, edge_im = da_terms(gbuf_ref[0:SCAN_BLK, 0:D_ST], gbuf_ref[0:SCAN_BLK, D_ST:], edge[:, 0:D_ST], edge[:, D_ST:])
        o_da[:, 0:D_ST] += body_re + edge_re
        o_da[:, D_ST:] += body_im + edge_im
        gtb = gbuf_ref[...].astype(BF16)
        _bd_accumulate(accb_ref, za.astype(BF16), gtb)
        dza = (dy * dskip_ref[...] + _bd_contract(gtb, bbd_ref)).astype(BF16)
        dz_ref[...] = _nn(pt, dza).astype(BF16)

        @pl.when(i == n_tiles - 1)
        def _():
            mask = _group_mask(D_SSM, 2 * D_ST)
            fb = jnp.zeros((GRP, 2 * D_ST), F32)
            fc = jnp.zeros((GRP, 2 * D_ST), F32)
            for gidx in range(N_GRP):
                rows = slice(gidx * GRP, (gidx + 1) * GRP)
                fb = fb + jnp.where(mask[rows, :], accb_ref[rows, :], 0.0)
                fc = fc + jnp.where(mask[rows, :], accc_ref[rows, :], 0.0)
            o_dbt[...] = fb
            o_dct[...] = fc

    def rowr(n):
        return pl.BlockSpec((tt, n), lambda i: (rev(i), 0))

    hsprev_spec = pl.BlockSpec((SCAN_BLK, 2 * D_ST), lambda i: (jnp.maximum(rev(i) * (tt // SCAN_BLK) - 1, 0), 0))
    small = [S((1, 2 * D_ST), F32), S((GRP, 2 * D_ST), F32), S((GRP, 2 * D_ST), F32), S((1, D_SSM), F32),
             S((1, D_SSM), F32), S((D_SSM, D_SSM), F32)]
    return _pallas(
        body, name=name, grid=(n_tiles,),
        in_specs=[rowr(D), rowr(D_SSM), rowr(2 * D_ST), hsprev_spec, rowr(D_SSM), pl.BlockSpec(memory_space=pl.ANY),
                  _full((D_SSM, D)), _full((32, D_ST)), _full((D_SSM, 2 * D_ST)), _full((D_SSM, 2 * D_ST)),
                  _full((1, D_SSM)), _full((D_SSM, D_SSM)), _full((1, D_SSM)), _full((tt, tt)), _full((tt, tt))],
        out_specs=[rowr(D_SSM)] + [_full(s.shape) for s in small],
        out_shape=[S((T, D_IN), BF16)] + small,
        scratch_shapes=[pltpu.VMEM((tt, 2 * D_ST), F32), pltpu.VMEM((SCAN_BLK, 2 * D_ST), F32),
                        pltpu.VMEM((D_SSM, 2 * D_ST), F32), pltpu.VMEM((D_SSM, 2 * D_ST), F32)],
        vmem_mb=56, exchange=exchange, aliases={5: 0},
        operands=(dx1b, z, hs, hs, ys, dz, wout_a, sc, bbd, cbd, dskip, wglu, bglu, *perm))


def inproj_fwd(x, g, w_t, name, exchange=None):
    T = x.shape[0]
    tt = TT_PROJ

    def body(x_ref, g_ref, w_ref, h_ref, z_ref):
        xn, _ = _rms(x_ref[...])
        h = (xn * g_ref[...]).astype(BF16)
        h_ref[...] = h
        z_ref[...] = _nt(h, w_ref[...])

    return _pallas(
        body, name=name, grid=(T // tt,),
        in_specs=[_row(tt, D), _full((1, D)), _full((D_IN, D))],
        out_specs=[_row(tt, D), _row(tt, D_IN)],
        out_shape=[S((T, D), BF16), S((T, D_IN), F32)],
        scratch_shapes=[], vmem_mb=40, operands=(x, g, w_t), exchange=exchange)


def inproj_bwd(dzb, x, g, w_t, dx1):
    T = x.shape[0]
    tt = TT_PROJ

    def body(dz_ref, x_ref, g_ref, w_ref, dx1_ref, dx_ref, dg_ref):
        @pl.when(pl.program_id(0) == 0)
        def _():
            dg_ref[...] = jnp.zeros_like(dg_ref)

        dh = _nn(dz_ref[...], w_ref[...])
        xn, r = _rms(x_ref[...])
        dg_ref[...] += _rowsum(dh * xn)
        dx_ref[...] = dx1_ref[...] + _rms_bwd(dh, xn, r, g_ref[...])

    return pl.pallas_call(
        body, name="inproj_bwd", grid=(T // tt,),
        in_specs=[_row(tt, D_IN), _row(tt, D), _full((1, D)), _full((D_IN, D)), _row(tt, D)],
        out_specs=[_row(tt, D), _full((1, D))],
        out_shape=[S((T, D), F32), S((1, D), F32)],
        compiler_params=_cp(40, 1),
    )(dzb, x, g, w_t, dx1)


def _load_weights(pairs, sem):
    @pl.when(pl.program_id(0) == 0)
    def _():
        copies = [pltpu.make_async_copy(src, dst, sem.at[k]) for k, (src, dst) in enumerate(pairs)]
        for cp in copies:
            cp.start()
        for cp in copies:
            cp.wait()


def ffn_fwd(x, ycat, wout, g, wg_t, wu_t, wd, name, exchange=None, head=None):
    T = x.shape[0]
    tt = TT_FFN
    any_spec = pl.BlockSpec(memory_space=pl.ANY)

    def body(*refs):
        if head is None:
            (x_ref, ycat_ref, g_ref, wout_hbm, wg_hbm, wu_hbm, wd_hbm,
             x1_ref, h_ref, gate_ref, up_ref, act_ref, x2_ref, wout_v, wg_v, wu_v, wd_v, sem) = refs
        else:
            (x_ref, ycat_ref, g_ref, t_ref, gf_ref, wout_hbm, wg_hbm, wu_hbm, wd_hbm,
             x1_ref, h_ref, gate_ref, up_ref, act_ref, x2_ref, lvec_ref, dgf_ref, wout_v, wg_v, wu_v, wd_v, sem) = refs
        _ffn_fwd_tile(x_ref, ycat_ref, g_ref, wout_hbm, wg_hbm, wu_hbm, wd_hbm, x1_ref, h_ref, gate_ref, up_ref, act_ref,
                      x2_ref, wout_v, wg_v, wu_v, wd_v, sem)
        if head is not None:
            @pl.when(pl.program_id(0) == 0)
            def _():
                lvec_ref[...] = jnp.zeros_like(lvec_ref)
                dgf_ref[...] = jnp.zeros_like(dgf_ref)

            xn, r = _rms(x2_ref[...])
            gf = gf_ref[...]
            err = xn * gf - t_ref[...]
            lvec_ref[...] += _rowsum(err * err)
            dy = err * (1.0 / D)
            dgf_ref[...] += _rowsum(dy * xn)
            x2_ref[...] = _rms_bwd(dy, xn, r, gf)

    def _ffn_fwd_tile(x_ref, ycat_ref, g_ref, wout_hbm, wg_hbm, wu_hbm, wd_hbm,
                      x1_ref, h_ref, gate_ref, up_ref, act_ref, x2_ref, wout_v, wg_v, wu_v, wd_v, sem):
        _load_weights([(wout_hbm, wout_v), (wg_hbm, wg_v), (wu_hbm, wu_v), (wd_hbm, wd_v)], sem)
        x1 = x_ref[...] + _nn(ycat_ref[...], wout_v[...])
        x1_ref[...] = x1
        xn, _ = _rms(x1)
        h = (xn * g_ref[...]).astype(BF16)
        h_ref[...] = h
        gate = _nt(h, wg_v[...])
        up = _nt(h, wu_v[...])
        gate_ref[...] = gate.astype(BF16)
        up_ref[...] = up.astype(BF16)
        act = (gate * jax.nn.sigmoid(gate) * up).astype(BF16)
        act_ref[...] = act
        x2_ref[...] = x1 + _nn(act, wd_v[...])

    with_head = head is not None
    return _pallas(
        body, name=name, grid=(T // tt,),
        in_specs=[_row(tt, D), _row(tt, D), _full((1, D))] + ([_row(tt, D), _full((1, D))] if with_head else [])
        + [any_spec, any_spec, any_spec, any_spec],
        out_specs=[_row(tt, D), _row(tt, D), _row(tt, D_FF), _row(tt, D_FF), _row(tt, D_FF), _row(tt, D)]
        + ([_full((1, D)), _full((1, D))] if with_head else []),
        out_shape=[S((T, D), F32), S((T, D), BF16), S((T, D_FF), BF16), S((T, D_FF), BF16), S((T, D_FF), BF16),
                   S((T, D), F32)] + ([S((1, D), F32), S((1, D), F32)] if with_head else []),
        scratch_shapes=[pltpu.VMEM((D, D), BF16), pltpu.VMEM((D_FF, D), BF16), pltpu.VMEM((D_FF, D), BF16),
                        pltpu.VMEM((D_FF, D), BF16), pltpu.SemaphoreType.DMA((4,))],
        vmem_mb=56, operands=(x, ycat, g) + (tuple(head) if with_head else ()) + (wout, wg_t, wu_t, wd), exchange=exchange)


def _pool_sgu_bwd(i, n_tiles, tile, tt, d_bc, z_ref, zprev_ref, wpool_ref, pscale_ref, lng_ref, lnb_ref, wsm_ref, wsmt_ref,
                  bsp_ref, dz_ref, o_dwpool, o_dpscale, o_dlng, o_dlnb, o_dws, o_dbsp,
                  ehalo_ref, pbuf_ref, mix_ref, dvn_ref, accw_ref, accm_ref):
    @pl.when(i == 0)
    def _():
        ehalo_ref[...] = jnp.zeros_like(ehalo_ref)
        accw_ref[...] = jnp.zeros_like(accw_ref)
        accm_ref[...] = jnp.zeros_like(accm_ref)
        for o in (o_dwpool, o_dpscale, o_dlng, o_dlnb):
            o[...] = jnp.zeros_like(o)

    d_b = d_bc[:, 0:D_POOL]
    d_c = d_bc[:, D_POOL:D_POOL + D_SGU]
    zb = z_ref[:, D_SSM:D_SSM + D_POOL]
    zu = z_ref[:, D_SSM + D_POOL:D_SSM + D_POOL + D_SGU]
    zv = z_ref[:, D_SSM + D_POOL + D_SGU:D_IN]
    not_first = (tile > 0).astype(F32)

    pooled, cnt = _pool_fwd(pbuf_ref, zb, zprev_ref[:, D_SSM:D_SSM + D_POOL] * not_first, tile, tt)
    pooledb = pooled.astype(BF16)
    mixed = _nn(pooledb, wpool_ref[...])
    o_dpscale[...] += _rowsum(d_b * mixed)
    dmixb = (d_b * pscale_ref[...]).astype(BF16)
    o_dwpool[...] += _tn(pooledb, dmixb)
    dpooled = _nt(dmixb, wpool_ref[...])
    e = dpooled / cnt
    pbuf_ref[0:tt, :] = e
    pbuf_ref[tt:tt + HALO, :] = ehalo_ref[...]
    ehalo_ref[...] = e[0:HALO, :]
    x = pbuf_ref[...]
    n = tt + HALO
    f2 = x + pltpu.roll(x, n - 1, axis=0)
    f4 = f2 + pltpu.roll(f2, n - 2, axis=0)
    f8 = f4 + pltpu.roll(f4, n - 4, axis=0)
    f16 = f8 + pltpu.roll(f8, n - 8, axis=0)
    fwd_sum = _select_window(_lane_windows(D_POOL), f2, f4, f8, f16)[0:tt, :]
    dz_ref[:, D_SSM:D_SSM + D_POOL] = (fwd_sum - dpooled).astype(BF16)

    lng = lng_ref[...]
    u, su, sv, vhat, rstd, vnb = _sgu_fwd(zu, zv, lng, lnb_ref[...], wsm_ref, bsp_ref[...], mix_ref, tt)
    dz_ref[:, D_SSM + D_POOL:D_SSM + D_POOL + D_SGU] = (d_c * mix_ref[...] * _gelu_grad(zu, su)).astype(BF16)
    dmix = d_c * u
    dmixb2 = dmix.astype(BF16)
    for c in range(tt // CHUNK):
        rows = slice(c * CHUNK, (c + 1) * CHUNK)
        accm_ref[...] += dmix[rows, :]
        parts = []
        for h in range(HEADS):
            cols = slice(h * HEAD_DIM, (h + 1) * HEAD_DIM)
            accw_ref[h] += _nt(dmixb2[rows, cols], vnb[rows, cols])
            parts.append(_nn(wsmt_ref[h], dmixb2[rows, cols]))
        dvn_ref[rows, :] = jnp.concatenate(parts, axis=1)
    dvn = dvn_ref[...]
    o_dlng[...] += _rowsum(dvn * vhat)
    o_dlnb[...] += _rowsum(dvn)
    dvh = dvn * lng
    dv = rstd * (dvh - jnp.mean(dvh, axis=-1, keepdims=True) - vhat * jnp.mean(dvh * vhat, axis=-1, keepdims=True))
    dz_ref[:, D_SSM + D_POOL + D_SGU:D_IN] = (dv * _gelu_grad(zv, sv)).astype(BF16)

    @pl.when(i == n_tiles - 1)
    def _():
        tri = (lax.broadcasted_iota(jnp.int32, (CHUNK, CHUNK), 0) >= lax.broadcasted_iota(jnp.int32, (CHUNK, CHUNK), 1))
        for h in range(HEADS):
            o_dws[h] = jnp.where(tri, accw_ref[h], 0.0)
        lane = lax.broadcasted_iota(jnp.int32, (1, 128), 1)
        acc = jnp.zeros((CHUNK, 128), F32)
        for h in range(HEADS):
            sh = jnp.sum(accm_ref[:, h * HEAD_DIM:(h + 1) * HEAD_DIM], axis=1, keepdims=True)
            acc = jnp.where(lane == h, sh, acc)
        o_dbsp[...] = acc


def ffn_bwd(dx2, x1, gate, up, g, wg_t, wu_t, wd, z, wout_bc, wpool, pscale, lng, lnb, wsm, wsmt, bsp, name, exchange=None):
    T = x1.shape[0]
    tt = TT_FFN
    n_tiles = T // tt
    any_spec = pl.BlockSpec(memory_space=pl.ANY)

    def rev(i):
        return n_tiles - 1 - i

    def body(dx2_ref, x1_ref, gate_ref, up_ref, g_ref, z_ref, zprev_ref, woutbc_ref, wpool_ref, pscale_ref, lng_ref, lnb_ref,
             wsm_ref, wsmt_ref, bsp_ref, wg_hbm, wu_hbm, wd_hbm,
             dgu_ref, dx2b_ref, dx1_ref, dx1b_ref, dg_ref, dz_ref, o_dwpool, o_dpscale, o_dlng, o_dlnb, o_dws, o_dbsp,
             wg_v, wu_v, wd_v, sem, ehalo_ref, pbuf_ref, mix_ref, dvn_ref, accw_ref, accm_ref):
        _load_weights([(wg_hbm, wg_v), (wu_hbm, wu_v), (wd_hbm, wd_v)], sem)
        i = pl.program_id(0)

        @pl.when(i == 0)
        def _():
            dg_ref[...] = jnp.zeros_like(dg_ref)

        dx2 = dx2_ref[...]
        dx2b = dx2.astype(BF16)
        dx2b_ref[...] = dx2b
        dact = _nt(dx2b, wd_v[...])
        gate = gate_ref[...].astype(F32)
        up = up_ref[...].astype(F32)
        sg = jax.nn.sigmoid(gate)
        dgate = (dact * up * (sg * (1.0 + gate * (1.0 - sg)))).astype(BF16)
        dup = (dact * gate * sg).astype(BF16)
        dgu_ref[:, 0:D_FF] = dgate
        dgu_ref[:, D_FF:2 * D_FF] = dup
        dh = _nn(dgate, wg_v[...]) + _nn(dup, wu_v[...])
        xn, r = _rms(x1_ref[...])
        dg_ref[...] += _rowsum(dh * xn)
        dx1 = dx2 + _rms_bwd(dh, xn, r, g_ref[...])
        dx1_ref[...] = dx1
        dx1b = dx1.astype(BF16)
        dx1b_ref[...] = dx1b
        dz_ref[:, 0:D_SSM] = jnp.zeros((tt, D_SSM), BF16)
        _pool_sgu_bwd(i, n_tiles, rev(i), tt, _nt(dx1b, woutbc_ref[...]), z_ref, zprev_ref, wpool_ref, pscale_ref, lng_ref,
                      lnb_ref, wsm_ref, wsmt_ref, bsp_ref, dz_ref, o_dwpool, o_dpscale, o_dlng, o_dlnb, o_dws, o_dbsp,
                      ehalo_ref, pbuf_ref, mix_ref, dvn_ref, accw_ref, accm_ref)

    def rowr(n):
        return pl.BlockSpec((tt, n), lambda i: (rev(i), 0))

    zprev_spec = pl.BlockSpec((HALO, D_IN), lambda i: (jnp.maximum(rev(i) * (tt // HALO) - 1, 0), 0))
    small = [S((D_POOL, D_POOL), F32), S((1, D_POOL), F32), S((1, D_SGU), F32), S((1, D_SGU), F32),
             S((HEADS, CHUNK, CHUNK), F32), S((CHUNK, 128), F32)]
    return _pallas(
        body, name=name, grid=(n_tiles,),
        in_specs=[rowr(D), rowr(D), rowr(D_FF), rowr(D_FF), _full((1, D)), rowr(D_IN), zprev_spec,
                  _full((D_POOL + D_SGU, D)), _full((D_POOL, D_POOL)), _full((1, D_POOL)), _full((1, D_SGU)),
                  _full((1, D_SGU)), _full((HEADS, CHUNK, CHUNK)), _full((HEADS, CHUNK, CHUNK)), _full((CHUNK, D_SGU)),
                  any_spec, any_spec, any_spec],
        out_specs=[rowr(2 * D_FF), rowr(D), rowr(D), rowr(D), _full((1, D)), rowr(D_IN)] + [_full(s.shape) for s in small],
        out_shape=[S((T, 2 * D_FF), BF16), S((T, D), BF16), S((T, D), F32), S((T, D), BF16), S((1, D), F32),
                   S((T, D_IN), BF16)] + small,
        scratch_shapes=[pltpu.VMEM((D_FF, D), BF16), pltpu.VMEM((D_FF, D), BF16), pltpu.VMEM((D_FF, D), BF16),
                        pltpu.SemaphoreType.DMA((3,)), pltpu.VMEM((HALO, D_POOL), F32),
                        pltpu.VMEM((tt + HALO, D_POOL), F32), pltpu.VMEM((tt, D_SGU), F32), pltpu.VMEM((tt, D_SGU), F32),
                        pltpu.VMEM((HEADS, CHUNK, CHUNK), F32), pltpu.VMEM((CHUNK, D_SGU), F32)],
        vmem_mb=60, exchange=exchange,
        operands=(dx2, x1, gate, up, g, z, z, wout_bc, wpool, pscale, lng, lnb, wsm, wsmt, bsp, wg_t, wu_t, wd))


def wgrad(a, b, tm, name, exchange=None, tk=TK_WGRAD):
    T, M = a.shape
    N = b.shape[1]
    tk = min(tk, T)
    n_k = T // tk

    def body(a_ref, b_ref, o_ref, acc_ref):
        k = pl.program_id(1)

        @pl.when(k == 0)
        def _():
            acc_ref[...] = jnp.zeros_like(acc_ref)

        acc_ref[...] += _tn(a_ref[...], b_ref[...])

        @pl.when(k == n_k - 1)
        def _():
            o_ref[...] = acc_ref[...].astype(BF16)

    (out,), got = _pallas(
        body, name=name, grid=(M // tm, n_k),
        in_specs=[pl.BlockSpec((tk, tm), lambda m, k: (k, m)), pl.BlockSpec((tk, N), lambda m, k: (k, 0))],
        out_specs=[pl.BlockSpec((tm, N), lambda m, k: (m, 0))],
        out_shape=[S((M, N), BF16)],
        scratch_shapes=[pltpu.VMEM((tm, N), F32)],
        vmem_mb=48, operands=(a, b), exchange=exchange)
    return out if exchange is None else (out, got)


def _mesh_place():
    x, y, c = lax.axis_index("x"), lax.axis_index("y"), lax.axis_index("c")
    return x, y, c, 4 * x + 2 * y + c


def _peer(x, y, c, k):
    px = 1 - x if k & 4 else x
    py = 1 - y if k & 2 else y
    pc = 1 - c if k & 1 else c
    return (px, py, pc), 4 * px + 2 * py + pc


class _Exchange:
    SAME_CORE = (2, 4, 6)

    def __init__(self, gather=(), scatter=()):
        self.entries = [(a, None, a.shape[0]) for a in gather] + [(a, off, rows) for a, off, rows in scatter]
        self.n_gather = len(gather)

    @property
    def n(self):
        return len(self.entries)

    def operands(self):
        return [e[0] for e in self.entries]

    def out_shapes(self):
        return [S((N_DEV, rows, a.shape[1]), a.dtype) for a, _, rows in self.entries]

    def sems(self):
        return [pltpu.SemaphoreType.DMA((self.n, N_DEV)), pltpu.SemaphoreType.DMA((self.n, N_DEV)),
                pltpu.SemaphoreType.DMA((self.n,))]

    def _src(self, ref, e, idx):
        _, off, rows = self.entries[e]
        if off is None:
            return ref
        return ref.at[pl.ds(pl.multiple_of(off + idx * rows, 16), rows)]

    def _masks(self, e):
        return (1,) + self.SAME_CORE if e < self.n_gather else tuple(range(1, N_DEV))

    def _copy(self, ins, outs, sems, e, k, sending, passing_on=False):
        send_sems, recv_sems, _ = sems
        x, y, c, me = _mesh_place()
        peer, pidx = _peer(x, y, c, k)
        if passing_on:
            return pltpu.make_async_remote_copy(
                src_ref=outs[e].at[pidx], dst_ref=outs[e].at[pidx], send_sem=send_sems.at[e, k | 1],
                recv_sem=recv_sems.at[e, k | 1], device_id=_peer(x, y, c, 1)[0], device_id_type=pl.DeviceIdType.MESH)
        return pltpu.make_async_remote_copy(
            src_ref=self._src(ins[e], e, pidx), dst_ref=outs[e].at[me if sending else pidx], send_sem=send_sems.at[e, k],
            recv_sem=recv_sems.at[e, k], device_id=peer, device_id_type=pl.DeviceIdType.MESH)

    def _local(self, ins, outs, sems):
        me = _mesh_place()[3]
        return [pltpu.make_async_copy(self._src(ins[e], e, me), outs[e].at[me], sems[2].at[e]) for e in range(self.n)]

    def start(self, ins, outs, sems):
        for cp in self._local(ins, outs, sems):
            cp.start()
        for k in range(1, N_DEV):
            for e in range(self.n):
                if k in self._masks(e):
                    self._copy(ins, outs, sems, e, k, True).start()

    def forward(self, ins, outs, sems):
        for k in self.SAME_CORE:
            for e in range(self.n_gather):
                self._copy(ins, outs, sems, e, k, False).wait_recv()
                self._copy(ins, outs, sems, e, k, False, passing_on=True).start()

    def wait(self, ins, outs, sems):
        for k in range(1, N_DEV):
            for e in range(self.n):
                if e >= self.n_gather or k % 2:
                    self._copy(ins, outs, sems, e, k, False).wait_recv()
        for k in range(1, N_DEV):
            for e in range(self.n):
                self._copy(ins, outs, sems, e, k, True).wait_send()
        for cp in self._local(ins, outs, sems):
            cp.wait()


def _pallas(body, *, name, grid, in_specs, out_specs, out_shape, scratch_shapes, vmem_mb, operands, exchange=None,
            aliases=None):
    n_in, n_out, n_scr = len(in_specs), len(out_specs), len(scratch_shapes)
    n_steps = math.prod(grid)
    aliases = aliases or {}
    if exchange is None:
        res = pl.pallas_call(body, name=name, grid=grid, in_specs=in_specs, out_specs=out_specs, out_shape=out_shape,
                             scratch_shapes=scratch_shapes, input_output_aliases=aliases,
                             compiler_params=_cp(vmem_mb, len(grid)))(*operands)
        return list(res), []
    ex = exchange

    def hosted(*refs):
        ins, ex_in = refs[:n_in], refs[n_in:n_in + ex.n]
        outs = refs[n_in + ex.n:n_in + ex.n + n_out]
        ex_out = refs[n_in + ex.n + n_out:n_in + 2 * ex.n + n_out]
        scr = refs[n_in + 2 * ex.n + n_out:]
        sems = scr[n_scr:]
        step = pl.program_id(0)
        for axis in range(1, len(grid)):
            step = step * grid[axis] + pl.program_id(axis)

        @pl.when(step == 0)
        def _():
            ex.start(ex_in, ex_out, sems)

        body(*ins, *outs, *scr[:n_scr])

        if ex.n_gather:
            @pl.when(step == max(n_steps - 1 - max(2, n_steps // 8), 0))
            def _():
                ex.forward(ex_in, ex_out, sems)

        @pl.when(step == n_steps - 1)
        def _():
            ex.wait(ex_in, ex_out, sems)

    any_spec = pl.BlockSpec(memory_space=pl.ANY)
    res = pl.pallas_call(
        hosted, name=name, grid=grid, in_specs=list(in_specs) + [any_spec] * ex.n,
        out_specs=list(out_specs) + [any_spec] * ex.n, out_shape=list(out_shape) + ex.out_shapes(),
        scratch_shapes=list(scratch_shapes) + ex.sems(), input_output_aliases=aliases,
        compiler_params=_cp(vmem_mb, len(grid)),
    )(*operands, *ex.operands())
    return list(res[:n_out]), list(res[n_out:])


def exchange_only(ex, name):
    def body(*refs):
        ins, outs, sems = refs[:ex.n], refs[ex.n:2 * ex.n], refs[2 * ex.n:]
        ex.start(ins, outs, sems)
        ex.forward(ins, outs, sems)
        ex.wait(ins, outs, sems)

    any_spec = pl.BlockSpec(memory_space=pl.ANY)
    return list(pl.pallas_call(body, name=name, in_specs=[any_spec] * ex.n, out_specs=[any_spec] * ex.n,
                               out_shape=ex.out_shapes(), scratch_shapes=ex.sems())(*ex.operands()))


def _adamw(w, g, m, v):
    m = ADAM_B1 * m + (1.0 - ADAM_B1) * g
    v = ADAM_B2 * v + (1.0 - ADAM_B2) * (g * g)
    m_hat = m / (1.0 - ADAM_B1 ** ADAM_STEP)
    v_hat = v / (1.0 - ADAM_B2 ** ADAM_STEP)
    delta = -ADAM_LR * (m_hat / (jnp.sqrt(v_hat) + ADAM_EPS) + ADAM_WD * w)
    return delta, m, v


def _sum_parts(p_ref, rows=slice(None)):
    g = p_ref[0, rows].astype(F32)
    for k in range(1, N_DEV):
        g = g + p_ref[k, rows].astype(F32)
    return g


def adamw_layers(parts, w, m, v, name):
    n_l = len(parts)

    def body(*refs):
        p_refs = refs[:n_l]
        w_ref, m_ref, v_ref, g_out, d_out, m_out, v_out = refs[n_l:]
        for l in range(n_l):
            g = _sum_parts(p_refs[l])
            g_out[l] = g
            d_out[l], m_out[l], v_out[l] = _adamw(w_ref[l], g, m_ref[l], v_ref[l])

    return pl.pallas_call(
        body, name=name, out_shape=[S(w.shape, F32)] * 4, compiler_params=_cp(48),
    )(*parts, w, m, v)


def adamw_segments(parts, segments, w, m, v, name):
    n_p = len(parts)

    def body(*refs):
        p_refs = refs[:n_p]
        w_ref, m_ref, v_ref, g_out, d_out, m_out, v_out = refs[n_p:]
        for part, src, dst, rows in segments:
            g = _sum_parts(p_refs[part], slice(src, src + rows))
            to = slice(dst, dst + rows)
            g_out[to] = g
            d_out[to], m_out[to], v_out[to] = _adamw(w_ref[to], g, m_ref[to], v_ref[to])

    return pl.pallas_call(
        body, name=name, out_shape=[S(w.shape, F32)] * 4, compiler_params=_cp(48),
    )(*parts, w, m, v)


SMALL_LAYER = ("g_mix", "A_re", "A_im", "log_dt", "B_re", "B_im", "C_re", "C_im", "D_skip", "b_glu", "w_pool",
               "pool_scale", "sgu_ln_g", "sgu_ln_b", "w_spatial", "b_spatial", "g_ffn")
BIG_NAMES = ("w_in", "w_glu", "w_out", "w_gate", "w_up", "w_down")
COLUMN_SHARDED = ("w_in", "w_gate", "w_up")
WEIGHT_ORDER = ("g_mix", "w_in", "A_re", "A_im", "log_dt", "B_re", "B_im", "C_re", "C_im", "D_skip", "w_glu", "b_glu",
                "w_pool", "pool_scale", "sgu_ln_g", "sgu_ln_b", "w_spatial", "b_spatial", "w_out", "g_ffn", "w_gate",
                "w_up", "w_down", "g_final")
SEG = 1024


def _pack(arrays):
    parts = []
    for a in arrays:
        flat = a.reshape(-1)
        parts.append(jnp.pad(flat, (0, (-flat.shape[0]) % SEG)))
    return jnp.concatenate(parts).reshape(-1, 128)


def _state_rows(p):
    return p.reshape(1, D_ST)


def _chan_by_state(p):
    return jnp.transpose(p, (2, 0, 1)).reshape(GRP, D_ST)


def _chan_by_state_c(p):
    return jnp.transpose(p, (1, 0, 2)).reshape(GRP, D_ST)


def kernel(x, g_mix, w_in, A_re, A_im, log_dt, B_re, B_im, C_re, C_im, D_skip, w_glu, b_glu, w_pool, pool_scale, sgu_ln_g, sgu_ln_b, w_spatial, b_spatial, w_out, g_ffn, w_gate, w_up, w_down, g_final, loss_target, m_g_mix, m_w_in, m_A_re, m_A_im, m_log_dt, m_B_re, m_B_im, m_C_re, m_C_im, m_D_skip, m_w_glu, m_b_glu, m_w_pool, m_pool_scale, m_sgu_ln_g, m_sgu_ln_b, m_w_spatial, m_b_spatial, m_w_out, m_g_ffn, m_w_gate, m_w_up, m_w_down, m_g_final, v_g_mix, v_w_in, v_A_re, v_A_im, v_log_dt, v_B_re, v_B_im, v_C_re, v_C_im, v_D_skip, v_w_glu, v_b_glu, v_w_pool, v_pool_scale, v_sgu_ln_g, v_sgu_ln_b, v_w_spatial, v_b_spatial, v_w_out, v_g_ffn, v_w_gate, v_w_up, v_w_down, v_g_final):
    args = dict(locals())
    W = {n: args[n] for n in WEIGHT_ORDER}
    M = {n: args["m_" + n] for n in WEIGHT_ORDER}
    V = {n: args["v_" + n] for n in WEIGHT_ORDER}
    n_layers = g_mix.shape[0]
    x0 = x[0]
    target = loss_target[0]

    def my_rows(name, l):
        w = W[name][l]
        return (w.T if name in COLUMN_SHARDED else w).astype(BF16)

    full_w = [dict() for _ in range(n_layers)]

    def gather_of(*which):
        return _Exchange(gather=[my_rows(n, l) for n, l in which])

    def keep_gathered(which, arrays):
        for (n, l), a in zip(which, arrays):
            full_w[l][n] = a.reshape(-1, a.shape[-1])

    tri = jnp.tril(jnp.ones((CHUNK, CHUNK), bool))
    perm = _interleave_matrices(TT_MIX)
    consts = []
    for l in range(n_layers):
        a_re, a_im = _state_rows(A_re[l]), _state_rows(A_im[l])
        ldt = jnp.repeat(log_dt[l], N_STATE).reshape(1, D_ST)
        b_re_t, b_im_t = _chan_by_state(B_re[l]), _chan_by_state(B_im[l])
        sc, bbd, cbd = s5_prepare(a_re, a_im, ldt, b_re_t, b_im_t, _chan_by_state_c(C_re[l]), _chan_by_state_c(C_im[l]))
        wsm = jnp.where(tri[None], w_spatial[l], 0.0)
        wpool_bd = jnp.zeros((D_POOL, D_POOL), F32)
        for gi in range(len(POOL_WINDOWS)):
            wpool_bd = wpool_bd.at[gi * 64:(gi + 1) * 64, gi * 64:(gi + 1) * 64].set(w_pool[l, gi])
        consts.append(dict(
            disc=(a_re, a_im, ldt, b_re_t, b_im_t), sc=sc, bbd=bbd, cbd=cbd,
            dskip=D_skip[l].reshape(1, D_SSM), bglu=b_glu[l].reshape(1, D_SSM),
            wpool=wpool_bd.astype(BF16), pscale=pool_scale[l].reshape(1, D_POOL),
            lng=sgu_ln_g[l].reshape(1, D_SGU), lnb=sgu_ln_b[l].reshape(1, D_SGU),
            wsm=wsm.astype(BF16), wsmt=jnp.transpose(wsm, (0, 2, 1)).astype(BF16),
            bsp=jnp.repeat(b_spatial[l].T, HEAD_DIM, axis=1),
            gmix=g_mix[l].reshape(1, D), gffn=g_ffn[l].reshape(1, D)))

    def mixer_args(l):
        c = consts[l]
        return (c["bbd"], c["cbd"], c["dskip"], full_w[l]["w_glu"], c["bglu"], c["wpool"], c["pscale"], c["lng"], c["lnb"])

    first_needed = [("w_in", 0)]
    keep_gathered(first_needed, exchange_only(gather_of(*first_needed), "gather_first"))
    carried_fwd = {
        ("inproj", 0): [("w_glu", 0), ("w_out", 0)],
        ("mixer", 0): [("w_gate", 0), ("w_up", 0), ("w_down", 0)],
        ("ffn", 0): [("w_in", 1), ("w_glu", 1), ("w_out", 1), ("w_gate", 1)],
        ("mixer", 1): [("w_up", 1), ("w_down", 1)],
    }

    def carried(kind, l):
        which = carried_fwd.get((kind, l))
        return which, (gather_of(*which) if which else None)

    saved = []
    xl = x0
    for l in range(n_layers):
        c, fw = consts[l], full_w[l]
        which, ex = carried("inproj", l)
        (h, z), got = inproj_fwd(xl, c["gmix"], fw["w_in"], f"inproj_fwd_{l}", ex)
        keep_gathered(which or [], got)
        which, ex = carried("mixer", l)
        (ycat, hs, ys), got = mixer_fwd(z, c["sc"], *mixer_args(l), c["wsm"], c["bsp"], perm, f"mixer_fwd_{l}", ex)
        keep_gathered(which or [], got)
        which, ex = carried("ffn", l)
        head = (target, g_final.reshape(1, D)) if l == n_layers - 1 else None
        res, got = ffn_fwd(xl, ycat, fw["w_out"], c["gffn"], fw["w_gate"], fw["w_up"], fw["w_down"], f"ffn_fwd_{l}", ex, head)
        keep_gathered(which or [], got)
        x1, h2, gate, up, act, x2 = res[:6]
        saved.append(dict(x=xl, h=h, z=z, ycat=ycat, hs=hs, ys=ys, x1=x1, h2=h2, gate=gate, up=up, act=act))
        xl = x2
    dx, loss_vec, d_gfinal = xl, res[6], res[7]

    recv_big = {}
    recv_small = [None] * n_layers

    def keep_received(which, arrays):
        for key, a in zip(which, arrays):
            if key[0] == "small":
                recv_small[key[1]] = a
            else:
                recv_big[key] = a

    pending = None
    for l in reversed(range(n_layers)):
        c, fw, sv = consts[l], full_w[l], saved[l]
        (dgu, dx2b, dx1, dx1b, d_gffn, dz_rest, dwpool, dpscale, dlng, dlnb, dws, dbsp), got = ffn_bwd(
            dx, sv["x1"], sv["gate"], sv["up"], c["gffn"], fw["w_gate"], fw["w_up"], fw["w_down"], sv["z"],
            fw["w_out"][D_SSM:], c["wpool"], c["pscale"], c["lng"], c["lnb"], c["wsm"], c["wsmt"], c["bsp"],
            f"ffn_bwd_{l}", pending[1] if pending else None)
        if pending:
            keep_received(pending[0], got)
        g_gu = wgrad(dgu, sv["h2"], D_FF // 2, f"wgrad_gate_up_{l}")
        g_down = wgrad(sv["act"], dx2b, D_FF // 2, f"wgrad_down_{l}")
        g_out = wgrad(sv["ycat"], dx1b, D, f"wgrad_out_{l}")
        ffn_rows = D_FF // N_DEV
        ex = _Exchange(scatter=[(g_gu, 0, ffn_rows), (g_gu, D_FF, ffn_rows), (g_down, 0, ffn_rows), (g_out, 0, D // N_DEV)])
        (dzb, da, dbt, dct, dd, dbglu, dwglu), got = mixer_bwd(
            dx1b, sv["z"], sv["hs"], sv["ys"], dz_rest, fw["w_out"][:D_SSM], c["sc"], c["bbd"], c["cbd"], c["dskip"],
            fw["w_glu"], c["bglu"], perm, f"mixer_bwd_{l}", ex)
        keep_received([("w_gate", l), ("w_up", l), ("w_down", l), ("w_out", l)], got)
        dx, d_gmix = inproj_bwd(dzb, sv["x"], c["gmix"], fw["w_in"], dx1)
        d_are, d_aim, d_ldt, d_bre_t, d_bim_t = s5_param_bwd(*c["disc"], da, dbt)
        small = dict(
            g_mix=d_gmix.reshape(D), A_re=d_are.reshape(N_GRP, N_STATE), A_im=d_aim.reshape(N_GRP, N_STATE),
            log_dt=d_ldt[0, :N_GRP],
            B_re=jnp.transpose(d_bre_t.reshape(GRP, N_GRP, N_STATE), (1, 2, 0)),
            B_im=jnp.transpose(d_bim_t.reshape(GRP, N_GRP, N_STATE), (1, 2, 0)),
            C_re=jnp.transpose(dct[:, :D_ST].reshape(GRP, N_GRP, N_STATE), (1, 0, 2)),
            C_im=-jnp.transpose(dct[:, D_ST:].reshape(GRP, N_GRP, N_STATE), (1, 0, 2)),
            D_skip=dd.reshape(D_SSM), b_glu=dbglu.reshape(D_SSM),
            w_pool=jnp.stack([dwpool[gi * 64:(gi + 1) * 64, gi * 64:(gi + 1) * 64] for gi in range(len(POOL_WINDOWS))]),
            pool_scale=dpscale.reshape(D_POOL), sgu_ln_g=dlng.reshape(D_SGU), sgu_ln_b=dlnb.reshape(D_SGU),
            w_spatial=dws, b_spatial=dbsp[:, :HEADS].T, g_ffn=d_gffn.reshape(D))
        packed = [small[n] for n in SMALL_LAYER] + ([d_gfinal.reshape(D), loss_vec.reshape(D)] if l == 0 else [])
        small_entries = dict(gather=[_pack(packed)], scatter=[(dwglu.astype(BF16), 0, D_SSM // N_DEV)])
        if l > 0:
            g_in = wgrad(dzb, sv["h"], D_IN, f"wgrad_in_{l}")
            small_entries["scatter"].append((g_in, 0, D_IN // N_DEV))
            pending = ([("small", l), ("w_glu", l), ("w_in", l)], _Exchange(**small_entries))
        else:
            g_in, got = wgrad(dzb, sv["h"], D_IN, f"wgrad_in_{l}", tk=TK_WGRAD // 4, exchange=_Exchange(**small_entries))
            keep_received([("small", l), ("w_glu", l)], got)
            pending = ([("w_in", l)], _Exchange(scatter=[(g_in, 0, D_IN // N_DEV)]))
    grad_x = dx
    keep_received(pending[0], exchange_only(pending[1], "exchange_last"))

    out = {}
    for n in BIG_NAMES:
        tr = (lambda a: jnp.transpose(a, (0, 2, 1))) if n in COLUMN_SHARDED else (lambda a: a)
        res = adamw_layers([recv_big[(n, l)] for l in range(n_layers)], tr(W[n]), tr(M[n]), tr(V[n]), f"adamw_{n}")
        out[n] = [tr(r) for r in res]

    seg_rows = [(-(-math.prod(W[n].shape[1:]) // SEG)) * (SEG // 128) for n in SMALL_LAYER]
    segments, src, dst = [], 0, 0
    for rows in seg_rows:
        segments += [(l, src, dst + l * rows, rows) for l in range(n_layers)]
        src += rows
        dst += n_layers * rows
    tile_rows = SEG // 128
    segments += [(0, src, dst, tile_rows), (0, src + tile_rows, dst + tile_rows, tile_rows)]

    def pack_params(P):
        parts = []
        for n, rows in zip(SMALL_LAYER, seg_rows):
            flat = P[n].reshape(n_layers, -1)
            parts.append(jnp.pad(flat, ((0, 0), (0, rows * 128 - flat.shape[1]))).reshape(-1))
        return jnp.concatenate(parts + [P["g_final"], jnp.zeros((SEG,), F32)]).reshape(-1, 128)

    res = adamw_segments(recv_small, segments, pack_params(W), pack_params(M), pack_params(V), "adamw_small")
    for j in range(4):
        flat, off = res[j].reshape(-1), 0
        for n, rows in zip(SMALL_LAYER, seg_rows):
            size = math.prod(W[n].shape[1:])
            piece = flat[off:off + n_layers * rows * 128].reshape(n_layers, rows * 128)[:, :size].reshape(W[n].shape)
            out.setdefault(n, []).append(piece)
            off += n_layers * rows * 128
        out.setdefault("g_final", []).append(flat[off:off + D])
        if j == 0:
            loss = (0.5 / D) * jnp.sum(flat[off + SEG:off + SEG + D])

    return (loss, grad_x[None], *[out[n][0] for n in WEIGHT_ORDER], *[out[n][1] for n in WEIGHT_ORDER],
            *[out[n][2] for n in WEIGHT_ORDER], *[out[n][3] for n in WEIGHT_ORDER])
```

```python
import functools
import math

import jax
import jax.numpy as jnp
from jax import lax
from jax.experimental import pallas as pl
from jax.experimental.pallas import tpu as pltpu

F32 = jnp.float32
BF16 = jnp.bfloat16
S = jax.ShapeDtypeStruct

N_DEV = 8
D = 1024
D_SSM = 384
N_GRP = 24
GRP = 16
N_STATE = 64
D_ST = N_GRP * N_STATE
D_POOL = 256
POOL_WINDOWS = (2, 4, 8, 16)
HALO = 16
D_SGU = 384
HEADS = 6
HEAD_DIM = 64
CHUNK = 128
D_IN = 1408
D_FF = 2816
EPS = 1e-6
SCAN_BLK = 8

ADAM_LR = 0.001
ADAM_B1 = 0.9
ADAM_B2 = 0.999
ADAM_EPS = 1e-08
ADAM_WD = 0.01
ADAM_STEP = 10

GELU_C0 = math.sqrt(2.0 / math.pi)
GELU_C1 = 0.044715

TT_MIX = 256
SEG_LEN = TT_MIX // SCAN_BLK
TT_FFN = 256
TT_PROJ = 512
TK_WGRAD = 2048
VMEM_MB = 2 ** 20


def _cp(vmem_mb, grid_dims=0):
    kw = dict(vmem_limit_bytes=int(vmem_mb * VMEM_MB))
    if grid_dims:
        kw["dimension_semantics"] = ("arbitrary",) * grid_dims
    return pltpu.CompilerParams(**kw)


def _row(tt, n):
    return pl.BlockSpec((tt, n), lambda i: (i, 0))


def _full(shape):
    nd = len(shape)
    return pl.BlockSpec(shape, lambda *_: (0,) * nd)


def _nn(a, b):
    return jnp.dot(a, b, preferred_element_type=F32)


def _nt(a, b):
    return lax.dot_general(a, b, (((1,), (1,)), ((), ())), preferred_element_type=F32)


def _tn(a, b):
    return lax.dot_general(a, b, (((0,), (0,)), ((), ())), preferred_element_type=F32)


def _rowsum(x):
    return jnp.sum(x, axis=0, keepdims=True)


def _rms(x):
    r = lax.rsqrt(jnp.mean(x * x, axis=-1, keepdims=True) + EPS)
    return x * r, r


def _rms_bwd(dy, xn, r, g):
    dyg = dy * g
    return r * (dyg - xn * jnp.mean(dyg * xn, axis=-1, keepdims=True))


def _gelu(x):
    s = jax.nn.sigmoid(x * (2.0 * GELU_C0 + (2.0 * GELU_C0 * GELU_C1) * (x * x)))
    return x * s, s


def _gelu_grad(x, s):
    return s * (1.0 + x * (1.0 - s) * (2.0 * GELU_C0 + (6.0 * GELU_C0 * GELU_C1) * (x * x)))


def _discretise(a_re, a_im, ldt, b_re, b_im):
    dt = jnp.exp(ldt)
    mag = jnp.exp(a_re * dt)
    ar = mag * jnp.cos(a_im * dt)
    ai = mag * jnp.sin(a_im * dt)
    den = a_re * a_re + a_im * a_im
    f_re = ((ar - 1.0) * a_re + ai * a_im) / den
    f_im = (ai * a_re - (ar - 1.0) * a_im) / den
    bb_re = f_re * b_re - f_im * b_im
    bb_im = f_re * b_im + f_im * b_re
    return ar, ai, bb_re, bb_im


def _group_mask(rows, cols):
    r = lax.broadcasted_iota(jnp.int32, (rows, cols), 0) // GRP
    c = lax.broadcasted_iota(jnp.int32, (rows, cols), 1)
    c = jnp.where(c >= D_ST, c - D_ST, c) // N_STATE
    return r == c


def s5_prepare(a_re, a_im, ldt, b_re_t, b_im_t, c_re_t, c_im_t):
    def body(are_ref, aim_ref, ldt_ref, bre_ref, bim_ref, cre_ref, cim_ref, sc_ref, bbd_ref, cbd_ref):
        ar, ai, bb_re, bb_im = _discretise(are_ref[...], aim_ref[...], ldt_ref[...], bre_ref[...], bim_ref[...])
        mask = _group_mask(D_SSM, 2 * D_ST)
        bb = jnp.concatenate([jnp.tile(bb_re, (N_GRP, 1)), jnp.tile(bb_im, (N_GRP, 1))], axis=1)
        bbd_ref[...] = jnp.where(mask, bb, 0.0).astype(BF16)
        cc = jnp.concatenate([jnp.tile(cre_ref[...], (N_GRP, 1)), -jnp.tile(cim_ref[...], (N_GRP, 1))], axis=1)
        cbd_ref[...] = jnp.where(mask, cc, 0.0).astype(BF16)
        pr, pi = ar, ai
        for _ in range(SEG_LEN - 1):
            pr, pi = pr * ar - pi * ai, pr * ai + pi * ar
        for k, v in enumerate((ar, ai, pr, pi)):
            sc_ref[8 * k:8 * k + 8, :] = jnp.broadcast_to(v, (SCAN_BLK, D_ST))

    return pl.pallas_call(
        body, name="s5_prepare",
        out_shape=[S((32, D_ST), F32), S((D_SSM, 2 * D_ST), BF16), S((D_SSM, 2 * D_ST), BF16)],
        compiler_params=_cp(40),
    )(a_re, a_im, ldt, b_re_t, b_im_t, c_re_t, c_im_t)


def s5_param_bwd(a_re, a_im, ldt, b_re_t, b_im_t, da, dbt):
    def body(are_ref, aim_ref, ldt_ref, bre_ref, bim_ref, da_ref, dbt_ref, o_are, o_aim, o_ldt, o_bre, o_bim):
        _, vjp = jax.vjp(_discretise, are_ref[...], aim_ref[...], ldt_ref[...], bre_ref[...], bim_ref[...])
        da = da_ref[...]
        dbt = dbt_ref[...]
        g_are, g_aim, g_ldt, g_bre, g_bim = vjp((da[:, :D_ST], da[:, D_ST:], dbt[:, :D_ST], dbt[:, D_ST:]))
        o_are[...] = g_are
        o_aim[...] = g_aim
        o_bre[...] = g_bre
        o_bim[...] = g_bim
        grp = lax.broadcasted_iota(jnp.int32, (1, D_ST), 1) // N_STATE
        lane = lax.broadcasted_iota(jnp.int32, (1, 128), 1)
        out = jnp.zeros((1, 128), F32)
        for g in range(N_GRP):
            out = jnp.where(lane == g, jnp.sum(jnp.where(grp == g, g_ldt, 0.0), axis=1, keepdims=True), out)
        o_ldt[...] = out

    return pl.pallas_call(
        body, name="s5_param_bwd",
        out_shape=[S((1, D_ST), F32), S((1, D_ST), F32), S((1, 128), F32), S((GRP, D_ST), F32), S((GRP, D_ST), F32)],
        compiler_params=_cp(16),
    )(a_re, a_im, ldt, b_re_t, b_im_t, da, dbt)


_CH = ((0, 256), (256, D_SSM))
_ST = ((0, 1024), (1024, D_ST))


def _bd_expand(xb, w_ref, out_ref):
    for (c0, c1), (s0, s1) in zip(_CH, _ST):
        for half in (0, D_ST):
            out_ref[:, half + s0:half + s1] = _nn(xb[:, c0:c1], w_ref[c0:c1, half + s0:half + s1])


def _bd_contract(hb, w_ref):
    parts = []
    for (c0, c1), (s0, s1) in zip(_CH, _ST):
        parts.append(_nt(hb[:, s0:s1], w_ref[c0:c1, s0:s1]) + _nt(hb[:, D_ST + s0:D_ST + s1], w_ref[c0:c1, D_ST + s0:D_ST + s1]))
    return jnp.concatenate(parts, axis=1)


def _bd_accumulate(acc_ref, xb, hb):
    for (c0, c1), (s0, s1) in zip(_CH, _ST):
        for half in (0, D_ST):
            acc_ref[c0:c1, half + s0:half + s1] += _tn(xb[:, c0:c1], hb[:, half + s0:half + s1])


def _interleave_matrices(tt):
    r = lax.broadcasted_iota(jnp.int32, (tt, tt), 0)
    t = lax.broadcasted_iota(jnp.int32, (tt, tt), 1)
    p = (t == (r % SCAN_BLK) * (tt // SCAN_BLK) + r // SCAN_BLK).astype(BF16)
    return p, p.T


def _interleave_f32(p, x):
    hi = x.astype(BF16)
    lo = (x - hi.astype(F32)).astype(BF16)
    return _nn(p, hi) + _nn(p, lo)


def _scan_tile(buf_ref, sc_ref, carry_ref, n_blk, reverse):
    ar = sc_ref[0:8, :]
    ai = -sc_ref[8:16, :] if reverse else sc_ref[8:16, :]

    def rows(i):
        blk = (n_blk - 1 - i) if reverse else i
        return pl.ds(pl.multiple_of(blk * SCAN_BLK, SCAN_BLK), SCAN_BLK)

    def local(i, x):
        xr, xi = x
        r = rows(i)
        xr, xi = buf_ref[r, 0:D_ST] + ar * xr - ai * xi, buf_ref[r, D_ST:2 * D_ST] + ar * xi + ai * xr
        buf_ref[r, 0:D_ST] = xr
        buf_ref[r, D_ST:2 * D_ST] = xi
        return xr, xi

    zero = jnp.zeros((SCAN_BLK, D_ST), F32)
    end_r, end_i = lax.fori_loop(0, n_blk, local, (zero, zero), unroll=True)

    seg_r = sc_ref[16:17, :]
    seg_i = -sc_ref[24:25, :] if reverse else sc_ref[24:25, :]
    cr, ci = carry_ref[0:1, 0:D_ST], carry_ref[0:1, D_ST:2 * D_ST]
    sub = lax.broadcasted_iota(jnp.int32, (SCAN_BLK, D_ST), 0)
    in_r, in_i = zero, zero
    for s in (reversed(range(SCAN_BLK)) if reverse else range(SCAN_BLK)):
        in_r = jnp.where(sub == s, cr, in_r)
        in_i = jnp.where(sub == s, ci, in_i)
        cr, ci = end_r[s:s + 1, :] + seg_r * cr - seg_i * ci, end_i[s:s + 1, :] + seg_r * ci + seg_i * cr
    carry_ref[0:1, 0:D_ST] = cr
    carry_ref[0:1, D_ST:2 * D_ST] = ci

    def fix(i, d):
        dr, di = d
        dr, di = ar * dr - ai * di, ar * di + ai * dr
        r = rows(i)
        buf_ref[r, 0:D_ST] += dr
        buf_ref[r, D_ST:2 * D_ST] += di
        return dr, di

    lax.fori_loop(0, n_blk, fix, (in_r, in_i), unroll=True)


def _lane_windows(n):
    lane = lax.broadcasted_iota(jnp.int32, (1, n), 1)
    return lane // (D_POOL // len(POOL_WINDOWS))


def _select_window(grp, s2, s4, s8, s16):
    return jnp.where(grp == 0, s2, jnp.where(grp == 1, s4, jnp.where(grp == 2, s8, s16)))


def _pool_fwd(pbuf_ref, zb, halo, tile_idx, tt):
    pbuf_ref[0:HALO, :] = halo
    pbuf_ref[HALO:HALO + tt, :] = zb
    x = pbuf_ref[...]
    s2 = x + pltpu.roll(x, 1, axis=0)
    s4 = s2 + pltpu.roll(s2, 2, axis=0)
    s8 = s4 + pltpu.roll(s4, 4, axis=0)
    s16 = s8 + pltpu.roll(s8, 8, axis=0)
    grp = _lane_windows(D_POOL)
    win = _select_window(grp, s2, s4, s8, s16)[HALO:HALO + tt, :]
    width = _select_window(grp, 2.0, 4.0, 8.0, 16.0).astype(F32)
    pos = (tile_idx * tt + 1 + lax.broadcasted_iota(jnp.int32, (tt, 1), 0)).astype(F32)
    cnt = jnp.minimum(pos, width)
    return win / cnt - zb, cnt


def _sgu_fwd(zu, zv, lng, lnb, wsm_ref, bsp, mix_ref, tt):
    u, tu = _gelu(zu)
    v, tv = _gelu(zv)
    mu = jnp.mean(v, axis=-1, keepdims=True)
    vc = v - mu
    rstd = lax.rsqrt(jnp.mean(vc * vc, axis=-1, keepdims=True) + EPS)
    vhat = vc * rstd
    vnb = (vhat * lng + lnb).astype(BF16)
    for c in range(tt // CHUNK):
        rows = slice(c * CHUNK, (c + 1) * CHUNK)
        parts = [_nn(wsm_ref[h], vnb[rows, h * HEAD_DIM:(h + 1) * HEAD_DIM]) for h in range(HEADS)]
        mix_ref[rows, :] = jnp.concatenate(parts, axis=1) + bsp
    return u, tu, tv, vhat, rstd, vnb


def mixer_fwd(z, sc, bbd, cbd, dskip, wglu, bglu, wpool, pscale, lng, lnb, wsm, bsp, perm, name, exchange=None):
    T = z.shape[0]
    tt = TT_MIX
    n_tiles = T // tt

    def body(z_ref, sc_ref, bbd_ref, cbd_ref, dskip_ref, wglu_ref, bglu_ref, wpool_ref, pscale_ref, lng_ref, lnb_ref,
             wsm_ref, bsp_ref, p_ref, pt_ref, ycat_ref, hs_ref, ys_ref, carry_ref, halo_ref, pbuf_ref, mix_ref):
        i = pl.program_id(0)

        @pl.when(i == 0)
        def _():
            carry_ref[...] = jnp.zeros_like(carry_ref)
            halo_ref[...] = jnp.zeros_like(halo_ref)

        za = z_ref[:, 0:D_SSM]
        zb = z_ref[:, D_SSM:D_SSM + D_POOL]
        zu = z_ref[:, D_SSM + D_POOL:D_SSM + D_POOL + D_SGU]
        zv = z_ref[:, D_SSM + D_POOL + D_SGU:D_IN]
        p, pt = p_ref[...], pt_ref[...]
        za = _interleave_f32(p, za)
        _bd_expand(za.astype(BF16), bbd_ref, hs_ref)
        _scan_tile(hs_ref, sc_ref, carry_ref, tt // SCAN_BLK, reverse=False)
        y = _bd_contract(hs_ref[...].astype(BF16), cbd_ref) + dskip_ref[...] * za
        ys_ref[...] = y
        g, _ = _gelu(y)
        q = _nn(g.astype(BF16), wglu_ref[...]) + bglu_ref[...]
        ycat_ref[:, 0:D_SSM] = _nn(pt, (g * jax.nn.sigmoid(q)).astype(BF16)).astype(BF16)
        pooled, _ = _pool_fwd(pbuf_ref, zb, halo_ref[...], i, tt)
        halo_ref[...] = zb[tt - HALO:tt, :]
        ycat_ref[:, D_SSM:D_SSM + D_POOL] = (_nn(pooled.astype(BF16), wpool_ref[...]) * pscale_ref[...]).astype(BF16)
        u, _, _, _, _, _ = _sgu_fwd(zu, zv, lng_ref[...], lnb_ref[...], wsm_ref, bsp_ref[...], mix_ref, tt)
        ycat_ref[:, D_SSM + D_POOL:D] = (u * mix_ref[...]).astype(BF16)

    return _pallas(
        body, name=name, grid=(n_tiles,),
        in_specs=[_row(tt, D_IN), _full((32, D_ST)), _full((D_SSM, 2 * D_ST)), _full((D_SSM, 2 * D_ST)),
                  _full((1, D_SSM)), _full((D_SSM, D_SSM)), _full((1, D_SSM)), _full((D_POOL, D_POOL)),
                  _full((1, D_POOL)), _full((1, D_SGU)), _full((1, D_SGU)), _full((HEADS, CHUNK, CHUNK)),
                  _full((CHUNK, D_SGU)), _full((tt, tt)), _full((tt, tt))],
        out_specs=[_row(tt, D), _row(tt, 2 * D_ST), _row(tt, D_SSM)],
        out_shape=[S((T, D), BF16), S((T, 2 * D_ST), F32), S((T, D_SSM), F32)],
        scratch_shapes=[pltpu.VMEM((SCAN_BLK, 2 * D_ST), F32), pltpu.VMEM((HALO, D_POOL), F32),
                        pltpu.VMEM((tt + HALO, D_POOL), F32), pltpu.VMEM((tt, D_SGU), F32)],
        vmem_mb=48, operands=(z, sc, bbd, cbd, dskip, wglu, bglu, wpool, pscale, lng, lnb, wsm, bsp, *perm),
        exchange=exchange)


def mixer_bwd(dx1b, z, hs, ys, wout, sc, bbd, cbd, dskip, wglu, bglu, wpool, pscale, lng, lnb, wsm, wsmt, bsp, perm, name,
              exchange=None):
    T = z.shape[0]
    tt = TT_MIX
    n_tiles = T // tt

    def rev(i):
        return n_tiles - 1 - i

    def body(dx_ref, z_ref, zprev_ref, hs_ref, hsprev_ref, ys_ref, wout_ref, sc_ref, bbd_ref, cbd_ref, dskip_ref,
             wglu_ref, bglu_ref, wpool_ref, pscale_ref, lng_ref, lnb_ref, wsm_ref, wsmt_ref, bsp_ref, p_ref, pt_ref,
             dz_ref, o_da, o_dbt, o_dct, o_dd, o_dbglu, o_dwglu, o_dwpool, o_dpscale, o_dlng, o_dlnb, o_dws, o_dbsp,
             gbuf_ref, carry_ref, accb_ref, accc_ref, ehalo_ref, pbuf_ref, mix_ref, dvn_ref, accw_ref, accm_ref):
        i = pl.program_id(0)
        tile = rev(i)

        @pl.when(i == 0)
        def _():
            carry_ref[...] = jnp.zeros_like(carry_ref)
            accb_ref[...] = jnp.zeros_like(accb_ref)
            accc_ref[...] = jnp.zeros_like(accc_ref)
            for o in (o_da, o_dd, o_dbglu, o_dwglu):
                o[...] = jnp.zeros_like(o)

        p, pt = p_ref[...], pt_ref[...]
        dxb = dx_ref[...]
        d_a = _nt(_nn(p, dxb).astype(BF16), wout_ref[0:D_SSM, :])
        d_bc = _nt(dxb, wout_ref[D_SSM:D, :])
        za = _interleave_f32(p, z_ref[:, 0:D_SSM])
        first_tile = (tile > 0).astype(F32)

        y = ys_ref[...]
        g, tg = _gelu(y)
        gb = g.astype(BF16)
        sg = jax.nn.sigmoid(_nn(gb, wglu_ref[...]) + bglu_ref[...])
        dq = d_a * g * sg * (1.0 - sg)
        dqb = dq.astype(BF16)
        o_dbglu[...] += _rowsum(dq)
        o_dwglu[...] += _tn(gb, dqb)
        dy = (d_a * sg + _nt(dqb, wglu_ref[...])) * _gelu_grad(y, tg)
        o_dd[...] += _rowsum(dy * za)
        dyb = dy.astype(BF16)
        _bd_accumulate(accc_ref, dyb, hs_ref[...].astype(BF16))
        _bd_expand(dyb, cbd_ref, gbuf_ref)
        _scan_tile(gbuf_ref, sc_ref, carry_ref, tt // SCAN_BLK, reverse=True)
        hprev = hsprev_ref[SCAN_BLK - 1:SCAN_BLK, :] * first_tile
        sub = lax.broadcasted_iota(jnp.int32, (SCAN_BLK, 1), 0)
        edge = jnp.where(sub == 0, hprev, pltpu.roll(hs_ref[tt - SCAN_BLK:tt, :], 1, axis=0))

        def da_terms(gr, gi, hr, hi):
            return _rowsum(gr * hr + gi * hi), _rowsum(gi * hr - gr * hi)

        body_re, body_im = da_terms(gbuf_ref[SCAN_BLK:tt, 0:D_ST], gbuf_ref[SCAN_BLK:tt, D_ST:],
                                    hs_ref[0:tt - SCAN_BLK, 0:D_ST], hs_ref[0:tt - SCAN_BLK, D_ST:])
        edge_re, edge_im = da_terms(gbuf_ref[0:SCAN_BLK, 0:D_ST], gbuf_ref[0:SCAN_BLK, D_ST:], edge[:, 0:D_ST], edge[:, D_ST:])
        o_da[:, 0:D_ST] += body_re + edge_re
        o_da[:, D_ST:] += body_im + edge_im
        gtb = gbuf_ref[...].astype(BF16)
        _bd_accumulate(accb_ref, za.astype(BF16), gtb)
        dza = (dy * dskip_ref[...] + _bd_contract(gtb, bbd_ref)).astype(BF16)
        dz_ref[:, 0:D_SSM] = _nn(pt, dza).astype(BF16)
        _pool_sgu_bwd(i, n_tiles, tile, tt, d_bc, z_ref, zprev_ref, wpool_ref, pscale_ref, lng_ref, lnb_ref, wsm_ref,
                      wsmt_ref, bsp_ref, dz_ref, o_dwpool, o_dpscale, o_dlng, o_dlnb, o_dws, o_dbsp,
                      ehalo_ref, pbuf_ref, mix_ref, dvn_ref, accw_ref, accm_ref)

        @pl.when(i == n_tiles - 1)
        def _():
            mask = _group_mask(D_SSM, 2 * D_ST)
            fb = jnp.zeros((GRP, 2 * D_ST), F32)
            fc = jnp.zeros((GRP, 2 * D_ST), F32)
            for gidx in range(N_GRP):
                rows = slice(gidx * GRP, (gidx + 1) * GRP)
                fb = fb + jnp.where(mask[rows, :], accb_ref[rows, :], 0.0)
                fc = fc + jnp.where(mask[rows, :], accc_ref[rows, :], 0.0)
            o_dbt[...] = fb
            o_dct[...] = fc

    def rowr(n):
        return pl.BlockSpec((tt, n), lambda i: (rev(i), 0))

    zprev_spec = pl.BlockSpec((HALO, D_IN), lambda i: (jnp.maximum(rev(i) * (tt // HALO) - 1, 0), 0))
    hsprev_spec = pl.BlockSpec((SCAN_BLK, 2 * D_ST), lambda i: (jnp.maximum(rev(i) * (tt // SCAN_BLK) - 1, 0), 0))
    small = [S((1, 2 * D_ST), F32), S((GRP, 2 * D_ST), F32), S((GRP, 2 * D_ST), F32), S((1, D_SSM), F32),
             S((1, D_SSM), F32), S((D_SSM, D_SSM), F32), S((D_POOL, D_POOL), F32), S((1, D_POOL), F32),
             S((1, D_SGU), F32), S((1, D_SGU), F32), S((HEADS, CHUNK, CHUNK), F32), S((CHUNK, 128), F32)]
    return _pallas(
        body, name=name, grid=(n_tiles,),
        in_specs=[rowr(D), rowr(D_IN), zprev_spec, rowr(2 * D_ST), hsprev_spec, rowr(D_SSM), _full((D, D)),
                  _full((32, D_ST)), _full((D_SSM, 2 * D_ST)), _full((D_SSM, 2 * D_ST)), _full((1, D_SSM)),
                  _full((D_SSM, D_SSM)), _full((1, D_SSM)), _full((D_POOL, D_POOL)), _full((1, D_POOL)),
                  _full((1, D_SGU)), _full((1, D_SGU)), _full((HEADS, CHUNK, CHUNK)), _full((HEADS, CHUNK, CHUNK)),
                  _full((CHUNK, D_SGU)), _full((tt, tt)), _full((tt, tt))],
        out_specs=[rowr(D_IN)] + [_full(s.shape) for s in small],
        out_shape=[S((T, D_IN), BF16)] + small,
        scratch_shapes=[pltpu.VMEM((tt, 2 * D_ST), F32), pltpu.VMEM((SCAN_BLK, 2 * D_ST), F32),
                        pltpu.VMEM((D_SSM, 2 * D_ST), F32), pltpu.VMEM((D_SSM, 2 * D_ST), F32),
                        pltpu.VMEM((HALO, D_POOL), F32), pltpu.VMEM((tt + HALO, D_POOL), F32),
                        pltpu.VMEM((tt, D_SGU), F32), pltpu.VMEM((tt, D_SGU), F32),
                        pltpu.VMEM((HEADS, CHUNK, CHUNK), F32), pltpu.VMEM((CHUNK, D_SGU), F32)],
        vmem_mb=56, exchange=exchange,
        operands=(dx1b, z, z, hs, hs, ys, wout, sc, bbd, cbd, dskip, wglu, bglu, wpool, pscale, lng, lnb, wsm, wsmt, bsp, *perm))


def inproj_fwd(x, g, w_t, name, exchange=None):
    T = x.shape[0]
    tt = TT_PROJ

    def body(x_ref, g_ref, w_ref, h_ref, z_ref):
        xn, _ = _rms(x_ref[...])
        h = (xn * g_ref[...]).astype(BF16)
        h_ref[...] = h
        z_ref[...] = _nt(h, w_ref[...])

    return _pallas(
        body, name=name, grid=(T // tt,),
        in_specs=[_row(tt, D), _full((1, D)), _full((D_IN, D))],
        out_specs=[_row(tt, D), _row(tt, D_IN)],
        out_shape=[S((T, D), BF16), S((T, D_IN), F32)],
        scratch_shapes=[], vmem_mb=40, operands=(x, g, w_t), exchange=exchange)


def inproj_bwd(dzb, x, g, w_t, dx1):
    T = x.shape[0]
    tt = TT_PROJ

    def body(dz_ref, x_ref, g_ref, w_ref, dx1_ref, dx_ref, dg_ref):
        @pl.when(pl.program_id(0) == 0)
        def _():
            dg_ref[...] = jnp.zeros_like(dg_ref)

        dh = _nn(dz_ref[...], w_ref[...])
        xn, r = _rms(x_ref[...])
        dg_ref[...] += _rowsum(dh * xn)
        dx_ref[...] = dx1_ref[...] + _rms_bwd(dh, xn, r, g_ref[...])

    return pl.pallas_call(
        body, name="inproj_bwd", grid=(T // tt,),
        in_specs=[_row(tt, D_IN), _row(tt, D), _full((1, D)), _full((D_IN, D)), _row(tt, D)],
        out_specs=[_row(tt, D), _full((1, D))],
        out_shape=[S((T, D), F32), S((1, D), F32)],
        compiler_params=_cp(40, 1),
    )(dzb, x, g, w_t, dx1)


def _load_weights(pairs, sem):
    @pl.when(pl.program_id(0) == 0)
    def _():
        copies = [pltpu.make_async_copy(src, dst, sem.at[k]) for k, (src, dst) in enumerate(pairs)]
        for cp in copies:
            cp.start()
        for cp in copies:
            cp.wait()


def ffn_fwd(x, ycat, wout, g, wg_t, wu_t, wd, name, exchange=None, head=None):
    T = x.shape[0]
    tt = TT_FFN
    any_spec = pl.BlockSpec(memory_space=pl.ANY)

    def body(*refs):
        if head is None:
            (x_ref, ycat_ref, g_ref, wout_hbm, wg_hbm, wu_hbm, wd_hbm,
             x1_ref, h_ref, gate_ref, up_ref, act_ref, x2_ref, wout_v, wg_v, wu_v, wd_v, sem) = refs
        else:
            (x_ref, ycat_ref, g_ref, t_ref, gf_ref, wout_hbm, wg_hbm, wu_hbm, wd_hbm,
             x1_ref, h_ref, gate_ref, up_ref, act_ref, x2_ref, lvec_ref, dgf_ref, wout_v, wg_v, wu_v, wd_v, sem) = refs
        _ffn_fwd_tile(x_ref, ycat_ref, g_ref, wout_hbm, wg_hbm, wu_hbm, wd_hbm, x1_ref, h_ref, gate_ref, up_ref, act_ref,
                      x2_ref, wout_v, wg_v, wu_v, wd_v, sem)
        if head is not None:
            @pl.when(pl.program_id(0) == 0)
            def _():
                lvec_ref[...] = jnp.zeros_like(lvec_ref)
                dgf_ref[...] = jnp.zeros_like(dgf_ref)

            xn, r = _rms(x2_ref[...])
            gf = gf_ref[...]
            err = xn * gf - t_ref[...]
            lvec_ref[...] += _rowsum(err * err)
            dy = err * (1.0 / D)
            dgf_ref[...] += _rowsum(dy * xn)
            x2_ref[...] = _rms_bwd(dy, xn, r, gf)

    def _ffn_fwd_tile(x_ref, ycat_ref, g_ref, wout_hbm, wg_hbm, wu_hbm, wd_hbm,
                      x1_ref, h_ref, gate_ref, up_ref, act_ref, x2_ref, wout_v, wg_v, wu_v, wd_v, sem):
        _load_weights([(wout_hbm, wout_v), (wg_hbm, wg_v), (wu_hbm, wu_v), (wd_hbm, wd_v)], sem)
        x1 = x_ref[...] + _nn(ycat_ref[...], wout_v[...])
        x1_ref[...] = x1
        xn, _ = _rms(x1)
        h = (xn * g_ref[...]).astype(BF16)
        h_ref[...] = h
        gate = _nt(h, wg_v[...])
        up = _nt(h, wu_v[...])
        gate_ref[...] = gate.astype(BF16)
        up_ref[...] = up.astype(BF16)
        act = (gate * jax.nn.sigmoid(gate) * up).astype(BF16)
        act_ref[...] = act
        x2_ref[...] = x1 + _nn(act, wd_v[...])

    with_head = head is not None
    return _pallas(
        body, name=name, grid=(T // tt,),
        in_specs=[_row(tt, D), _row(tt, D), _full((1, D))] + ([_row(tt, D), _full((1, D))] if with_head else [])
        + [any_spec, any_spec, any_spec, any_spec],
        out_specs=[_row(tt, D), _row(tt, D), _row(tt, D_FF), _row(tt, D_FF), _row(tt, D_FF), _row(tt, D)]
        + ([_full((1, D)), _full((1, D))] if with_head else []),
        out_shape=[S((T, D), F32), S((T, D), BF16), S((T, D_FF), BF16), S((T, D_FF), BF16), S((T, D_FF), BF16),
                   S((T, D), F32)] + ([S((1, D), F32), S((1, D), F32)] if with_head else []),
        scratch_shapes=[pltpu.VMEM((D, D), BF16), pltpu.VMEM((D_FF, D), BF16), pltpu.VMEM((D_FF, D), BF16),
                        pltpu.VMEM((D_FF, D), BF16), pltpu.SemaphoreType.DMA((4,))],
        vmem_mb=56, operands=(x, ycat, g) + (tuple(head) if with_head else ()) + (wout, wg_t, wu_t, wd), exchange=exchange)


def _pool_sgu_bwd(i, n_tiles, tile, tt, d_bc, z_ref, zprev_ref, wpool_ref, pscale_ref, lng_ref, lnb_ref, wsm_ref, wsmt_ref,
                  bsp_ref, dz_ref, o_dwpool, o_dpscale, o_dlng, o_dlnb, o_dws, o_dbsp,
                  ehalo_ref, pbuf_ref, mix_ref, dvn_ref, accw_ref, accm_ref):
    @pl.when(i == 0)
    def _():
        ehalo_ref[...] = jnp.zeros_like(ehalo_ref)
        accw_ref[...] = jnp.zeros_like(accw_ref)
        accm_ref[...] = jnp.zeros_like(accm_ref)
        for o in (o_dwpool, o_dpscale, o_dlng, o_dlnb):
            o[...] = jnp.zeros_like(o)

    d_b = d_bc[:, 0:D_POOL]
    d_c = d_bc[:, D_POOL:D_POOL + D_SGU]
    zb = z_ref[:, D_SSM:D_SSM + D_POOL]
    zu = z_ref[:, D_SSM + D_POOL:D_SSM + D_POOL + D_SGU]
    zv = z_ref[:, D_SSM + D_POOL + D_SGU:D_IN]
    not_first = (tile > 0).astype(F32)

    pooled, cnt = _pool_fwd(pbuf_ref, zb, zprev_ref[:, D_SSM:D_SSM + D_POOL] * not_first, tile, tt)
    pooledb = pooled.astype(BF16)
    mixed = _nn(pooledb, wpool_ref[...])
    o_dpscale[...] += _rowsum(d_b * mixed)
    dmixb = (d_b * pscale_ref[...]).astype(BF16)
    o_dwpool[...] += _tn(pooledb, dmixb)
    dpooled = _nt(dmixb, wpool_ref[...])
    e = dpooled / cnt
    pbuf_ref[0:tt, :] = e
    pbuf_ref[tt:tt + HALO, :] = ehalo_ref[...]
    ehalo_ref[...] = e[0:HALO, :]
    x = pbuf_ref[...]
    n = tt + HALO
    f2 = x + pltpu.roll(x, n - 1, axis=0)
    f4 = f2 + pltpu.roll(f2, n - 2, axis=0)
    f8 = f4 + pltpu.roll(f4, n - 4, axis=0)
    f16 = f8 + pltpu.roll(f8, n - 8, axis=0)
    fwd_sum = _select_window(_lane_windows(D_POOL), f2, f4, f8, f16)[0:tt, :]
    dz_ref[:, D_SSM:D_SSM + D_POOL] = (fwd_sum - dpooled).astype(BF16)

    lng = lng_ref[...]
    u, su, sv, vhat, rstd, vnb = _sgu_fwd(zu, zv, lng, lnb_ref[...], wsm_ref, bsp_ref[...], mix_ref, tt)
    dz_ref[:, D_SSM + D_POOL:D_SSM + D_POOL + D_SGU] = (d_c * mix_ref[...] * _gelu_grad(zu, su)).astype(BF16)
    dmix = d_c * u
    dmixb2 = dmix.astype(BF16)
    for c in range(tt // CHUNK):
        rows = slice(c * CHUNK, (c + 1) * CHUNK)
        accm_ref[...] += dmix[rows, :]
        parts = []
        for h in range(HEADS):
            cols = slice(h * HEAD_DIM, (h + 1) * HEAD_DIM)
            accw_ref[h] += _nt(dmixb2[rows, cols], vnb[rows, cols])
            parts.append(_nn(wsmt_ref[h], dmixb2[rows, cols]))
        dvn_ref[rows, :] = jnp.concatenate(parts, axis=1)
    dvn = dvn_ref[...]
    o_dlng[...] += _rowsum(dvn * vhat)
    o_dlnb[...] += _rowsum(dvn)
    dvh = dvn * lng
    dv = rstd * (dvh - jnp.mean(dvh, axis=-1, keepdims=True) - vhat * jnp.mean(dvh * vhat, axis=-1, keepdims=True))
    dz_ref[:, D_SSM + D_POOL + D_SGU:D_IN] = (dv * _gelu_grad(zv, sv)).astype(BF16)

    @pl.when(i == n_tiles - 1)
    def _():
        tri = (lax.broadcasted_iota(jnp.int32, (CHUNK, CHUNK), 0) >= lax.broadcasted_iota(jnp.int32, (CHUNK, CHUNK), 1))
        for h in range(HEADS):
            o_dws[h] = jnp.where(tri, accw_ref[h], 0.0)
        lane = lax.broadcasted_iota(jnp.int32, (1, 128), 1)
        acc = jnp.zeros((CHUNK, 128), F32)
        for h in range(HEADS):
            sh = jnp.sum(accm_ref[:, h * HEAD_DIM:(h + 1) * HEAD_DIM], axis=1, keepdims=True)
            acc = jnp.where(lane == h, sh, acc)
        o_dbsp[...] = acc


def ffn_bwd(dx2, x1, gate, up, g, wg_t, wu_t, wd, name, exchange=None):
    T = x1.shape[0]
    tt = TT_FFN
    any_spec = pl.BlockSpec(memory_space=pl.ANY)

    def body(dx2_ref, x1_ref, gate_ref, up_ref, g_ref, wg_hbm, wu_hbm, wd_hbm,
             dgu_ref, dx2b_ref, dx1_ref, dx1b_ref, dg_ref, wg_v, wu_v, wd_v, sem):
        _load_weights([(wg_hbm, wg_v), (wu_hbm, wu_v), (wd_hbm, wd_v)], sem)

        @pl.when(pl.program_id(0) == 0)
        def _():
            dg_ref[...] = jnp.zeros_like(dg_ref)

        dx2 = dx2_ref[...]
        dx2b = dx2.astype(BF16)
        dx2b_ref[...] = dx2b
        dact = _nt(dx2b, wd_v[...])
        gate = gate_ref[...].astype(F32)
        up = up_ref[...].astype(F32)
        sg = jax.nn.sigmoid(gate)
        dgate = (dact * up * (sg * (1.0 + gate * (1.0 - sg)))).astype(BF16)
        dup = (dact * gate * sg).astype(BF16)
        dgu_ref[:, 0:D_FF] = dgate
        dgu_ref[:, D_FF:2 * D_FF] = dup
        dh = _nn(dgate, wg_v[...]) + _nn(dup, wu_v[...])
        xn, r = _rms(x1_ref[...])
        dg_ref[...] += _rowsum(dh * xn)
        dx1 = dx2 + _rms_bwd(dh, xn, r, g_ref[...])
        dx1_ref[...] = dx1
        dx1b_ref[...] = dx1.astype(BF16)

    return _pallas(
        body, name=name, grid=(T // tt,),
        in_specs=[_row(tt, D), _row(tt, D), _row(tt, D_FF), _row(tt, D_FF), _full((1, D)), any_spec, any_spec, any_spec],
        out_specs=[_row(tt, 2 * D_FF), _row(tt, D), _row(tt, D), _row(tt, D), _full((1, D))],
        out_shape=[S((T, 2 * D_FF), BF16), S((T, D), BF16), S((T, D), F32), S((T, D), BF16), S((1, D), F32)],
        scratch_shapes=[pltpu.VMEM((D_FF, D), BF16), pltpu.VMEM((D_FF, D), BF16), pltpu.VMEM((D_FF, D), BF16),
                        pltpu.SemaphoreType.DMA((3,))],
        vmem_mb=56, operands=(dx2, x1, gate, up, g, wg_t, wu_t, wd), exchange=exchange)


def wgrad(a, b, tm, name, exchange=None, tk=TK_WGRAD):
    T, M = a.shape
    N = b.shape[1]
    tk = min(tk, T)
    n_k = T // tk

    def body(a_ref, b_ref, o_ref, acc_ref):
        k = pl.program_id(1)

        @pl.when(k == 0)
        def _():
            acc_ref[...] = jnp.zeros_like(acc_ref)

        acc_ref[...] += _tn(a_ref[...], b_ref[...])

        @pl.when(k == n_k - 1)
        def _():
            o_ref[...] = acc_ref[...].astype(BF16)

    (out,), got = _pallas(
        body, name=name, grid=(M // tm, n_k),
        in_specs=[pl.BlockSpec((tk, tm), lambda m, k: (k, m)), pl.BlockSpec((tk, N), lambda m, k: (k, 0))],
        out_specs=[pl.BlockSpec((tm, N), lambda m, k: (m, 0))],
        out_shape=[S((M, N), BF16)],
        scratch_shapes=[pltpu.VMEM((tm, N), F32)],
        vmem_mb=48, operands=(a, b), exchange=exchange)
    return out if exchange is None else (out, got)


def _mesh_place():
    x, y, c = lax.axis_index("x"), lax.axis_index("y"), lax.axis_index("c")
    return x, y, c, 4 * x + 2 * y + c


def _peer(x, y, c, k):
    px = 1 - x if k & 4 else x
    py = 1 - y if k & 2 else y
    pc = 1 - c if k & 1 else c
    return (px, py, pc), 4 * px + 2 * py + pc


class _Exchange:
    SAME_CORE = (2, 4, 6)

    def __init__(self, gather=(), scatter=()):
        self.entries = [(a, None, a.shape[0]) for a in gather] + [(a, off, rows) for a, off, rows in scatter]
        self.n_gather = len(gather)

    @property
    def n(self):
        return len(self.entries)

    def operands(self):
        return [e[0] for e in self.entries]

    def out_shapes(self):
        return [S((N_DEV, rows, a.shape[1]), a.dtype) for a, _, rows in self.entries]

    def sems(self):
        return [pltpu.SemaphoreType.DMA((self.n, N_DEV)), pltpu.SemaphoreType.DMA((self.n, N_DEV)),
                pltpu.SemaphoreType.DMA((self.n,))]

    def _src(self, ref, e, idx):
        _, off, rows = self.entries[e]
        if off is None:
            return ref
        return ref.at[pl.ds(pl.multiple_of(off + idx * rows, 16), rows)]

    def _masks(self, e):
        return (1,) + self.SAME_CORE if e < self.n_gather else tuple(range(1, N_DEV))

    def _copy(self, ins, outs, sems, e, k, sending, passing_on=False):
        send_sems, recv_sems, _ = sems
        x, y, c, me = _mesh_place()
        peer, pidx = _peer(x, y, c, k)
        if passing_on:
            return pltpu.make_async_remote_copy(
                src_ref=outs[e].at[pidx], dst_ref=outs[e].at[pidx], send_sem=send_sems.at[e, k | 1],
                recv_sem=recv_sems.at[e, k | 1], device_id=_peer(x, y, c, 1)[0], device_id_type=pl.DeviceIdType.MESH)
        return pltpu.make_async_remote_copy(
            src_ref=self._src(ins[e], e, pidx), dst_ref=outs[e].at[me if sending else pidx], send_sem=send_sems.at[e, k],
            recv_sem=recv_sems.at[e, k], device_id=peer, device_id_type=pl.DeviceIdType.MESH)

    def _local(self, ins, outs, sems):
        me = _mesh_place()[3]
        return [pltpu.make_async_copy(self._src(ins[e], e, me), outs[e].at[me], sems[2].at[e]) for e in range(self.n)]

    def start(self, ins, outs, sems):
        for cp in self._local(ins, outs, sems):
            cp.start()
        for k in range(1, N_DEV):
            for e in range(self.n):
                if k in self._masks(e):
                    self._copy(ins, outs, sems, e, k, True).start()

    def forward(self, ins, outs, sems):
        for k in self.SAME_CORE:
            for e in range(self.n_gather):
                self._copy(ins, outs, sems, e, k, False).wait_recv()
                self._copy(ins, outs, sems, e, k, False, passing_on=True).start()

    def wait(self, ins, outs, sems):
        for k in range(1, N_DEV):
            for e in range(self.n):
                if e >= self.n_gather or k % 2:
                    self._copy(ins, outs, sems, e, k, False).wait_recv()
        for k in range(1, N_DEV):
            for e in range(self.n):
                self._copy(ins, outs, sems, e, k, True).wait_send()
        for cp in self._local(ins, outs, sems):
            cp.wait()


def _pallas(body, *, name, grid, in_specs, out_specs, out_shape, scratch_shapes, vmem_mb, operands, exchange=None,
            aliases=None):
    n_in, n_out, n_scr = len(in_specs), len(out_specs), len(scratch_shapes)
    n_steps = math.prod(grid)
    aliases = aliases or {}
    if exchange is None:
        res = pl.pallas_call(body, name=name, grid=grid, in_specs=in_specs, out_specs=out_specs, out_shape=out_shape,
                             scratch_shapes=scratch_shapes, input_output_aliases=aliases,
                             compiler_params=_cp(vmem_mb, len(grid)))(*operands)
        return list(res), []
    ex = exchange

    def hosted(*refs):
        ins, ex_in = refs[:n_in], refs[n_in:n_in + ex.n]
        outs = refs[n_in + ex.n:n_in + ex.n + n_out]
        ex_out = refs[n_in + ex.n + n_out:n_in + 2 * ex.n + n_out]
        scr = refs[n_in + 2 * ex.n + n_out:]
        sems = scr[n_scr:]
        step = pl.program_id(0)
        for axis in range(1, len(grid)):
            step = step * grid[axis] + pl.program_id(axis)

        @pl.when(step == 0)
        def _():
            ex.start(ex_in, ex_out, sems)

        body(*ins, *outs, *scr[:n_scr])

        if ex.n_gather:
            @pl.when(step == max(n_steps - 1 - max(2, n_steps // 8), 0))
            def _():
                ex.forward(ex_in, ex_out, sems)

        @pl.when(step == n_steps - 1)
        def _():
            ex.wait(ex_in, ex_out, sems)

    any_spec = pl.BlockSpec(memory_space=pl.ANY)
    res = pl.pallas_call(
        hosted, name=name, grid=grid, in_specs=list(in_specs) + [any_spec] * ex.n,
        out_specs=list(out_specs) + [any_spec] * ex.n, out_shape=list(out_shape) + ex.out_shapes(),
        scratch_shapes=list(scratch_shapes) + ex.sems(), input_output_aliases=aliases,
        compiler_params=_cp(vmem_mb, len(grid)),
    )(*operands, *ex.operands())
    return list(res[:n_out]), list(res[n_out:])


def exchange_only(ex, name):
    def body(*refs):
        ins, outs, sems = refs[:ex.n], refs[ex.n:2 * ex.n], refs[2 * ex.n:]
        ex.start(ins, outs, sems)
        ex.forward(ins, outs, sems)
        ex.wait(ins, outs, sems)

    any_spec = pl.BlockSpec(memory_space=pl.ANY)
    return list(pl.pallas_call(body, name=name, in_specs=[any_spec] * ex.n, out_specs=[any_spec] * ex.n,
                               out_shape=ex.out_shapes(), scratch_shapes=ex.sems())(*ex.operands()))


def _adamw(w, g, m, v):
    m = ADAM_B1 * m + (1.0 - ADAM_B1) * g
    v = ADAM_B2 * v + (1.0 - ADAM_B2) * (g * g)
    m_hat = m / (1.0 - ADAM_B1 ** ADAM_STEP)
    v_hat = v / (1.0 - ADAM_B2 ** ADAM_STEP)
    delta = -ADAM_LR * (m_hat / (jnp.sqrt(v_hat) + ADAM_EPS) + ADAM_WD * w)
    return delta, m, v


def _sum_parts(p_ref, rows=slice(None)):
    g = p_ref[0, rows].astype(F32)
    for k in range(1, N_DEV):
        g = g + p_ref[k, rows].astype(F32)
    return g


def adamw_layers(parts, w, m, v, name):
    n_l = len(parts)

    def body(*refs):
        p_refs = refs[:n_l]
        w_ref, m_ref, v_ref, g_out, d_out, m_out, v_out = refs[n_l:]
        for l in range(n_l):
            g = _sum_parts(p_refs[l])
            g_out[l] = g
            d_out[l], m_out[l], v_out[l] = _adamw(w_ref[l], g, m_ref[l], v_ref[l])

    return pl.pallas_call(
        body, name=name, out_shape=[S(w.shape, F32)] * 4, compiler_params=_cp(48),
    )(*parts, w, m, v)


def adamw_segments(parts, segments, w, m, v, name):
    n_p = len(parts)

    def body(*refs):
        p_refs = refs[:n_p]
        w_ref, m_ref, v_ref, g_out, d_out, m_out, v_out = refs[n_p:]
        for part, src, dst, rows in segments:
            g = _sum_parts(p_refs[part], slice(src, src + rows))
            to = slice(dst, dst + rows)
            g_out[to] = g
            d_out[to], m_out[to], v_out[to] = _adamw(w_ref[to], g, m_ref[to], v_ref[to])

    return pl.pallas_call(
        body, name=name, out_shape=[S(w.shape, F32)] * 4, compiler_params=_cp(48),
    )(*parts, w, m, v)


SMALL_LAYER = ("g_mix", "A_re", "A_im", "log_dt", "B_re", "B_im", "C_re", "C_im", "D_skip", "b_glu", "w_pool",
               "pool_scale", "sgu_ln_g", "sgu_ln_b", "w_spatial", "b_spatial", "g_ffn")
BIG_NAMES = ("w_in", "w_glu", "w_out", "w_gate", "w_up", "w_down")
COLUMN_SHARDED = ("w_in", "w_gate", "w_up")
WEIGHT_ORDER = ("g_mix", "w_in", "A_re", "A_im", "log_dt", "B_re", "B_im", "C_re", "C_im", "D_skip", "w_glu", "b_glu",
                "w_pool", "pool_scale", "sgu_ln_g", "sgu_ln_b", "w_spatial", "b_spatial", "w_out", "g_ffn", "w_gate",
                "w_up", "w_down", "g_final")
SEG = 1024


def _pack(arrays):
    parts = []
    for a in arrays:
        flat = a.reshape(-1)
        parts.append(jnp.pad(flat, (0, (-flat.shape[0]) % SEG)))
    return jnp.concatenate(parts).reshape(-1, 128)


def _state_rows(p):
    return p.reshape(1, D_ST)


def _chan_by_state(p):
    return jnp.transpose(p, (2, 0, 1)).reshape(GRP, D_ST)


def _chan_by_state_c(p):
    return jnp.transpose(p, (1, 0, 2)).reshape(GRP, D_ST)


def kernel(x, g_mix, w_in, A_re, A_im, log_dt, B_re, B_im, C_re, C_im, D_skip, w_glu, b_glu, w_pool, pool_scale, sgu_ln_g, sgu_ln_b, w_spatial, b_spatial, w_out, g_ffn, w_gate, w_up, w_down, g_final, loss_target, m_g_mix, m_w_in, m_A_re, m_A_im, m_log_dt, m_B_re, m_B_im, m_C_re, m_C_im, m_D_skip, m_w_glu, m_b_glu, m_w_pool, m_pool_scale, m_sgu_ln_g, m_sgu_ln_b, m_w_spatial, m_b_spatial, m_w_out, m_g_ffn, m_w_gate, m_w_up, m_w_down, m_g_final, v_g_mix, v_w_in, v_A_re, v_A_im, v_log_dt, v_B_re, v_B_im, v_C_re, v_C_im, v_D_skip, v_w_glu, v_b_glu, v_w_pool, v_pool_scale, v_sgu_ln_g, v_sgu_ln_b, v_w_spatial, v_b_spatial, v_w_out, v_g_ffn, v_w_gate, v_w_up, v_w_down, v_g_final):
    args = dict(locals())
    W = {n: args[n] for n in WEIGHT_ORDER}
    M = {n: args["m_" + n] for n in WEIGHT_ORDER}
    V = {n: args["v_" + n] for n in WEIGHT_ORDER}
    n_layers = g_mix.shape[0]
    x0 = x[0]
    target = loss_target[0]

    def my_rows(name, l):
        w = W[name][l]
        return (w.T if name in COLUMN_SHARDED else w).astype(BF16)

    full_w = [dict() for _ in range(n_layers)]

    def gather_of(*which):
        return _Exchange(gather=[my_rows(n, l) for n, l in which])

    def keep_gathered(which, arrays):
        for (n, l), a in zip(which, arrays):
            full_w[l][n] = a.reshape(-1, a.shape[-1])

    tri = jnp.tril(jnp.ones((CHUNK, CHUNK), bool))
    perm = _interleave_matrices(TT_MIX)
    consts = []
    for l in range(n_layers):
        a_re, a_im = _state_rows(A_re[l]), _state_rows(A_im[l])
        ldt = jnp.repeat(log_dt[l], N_STATE).reshape(1, D_ST)
        b_re_t, b_im_t = _chan_by_state(B_re[l]), _chan_by_state(B_im[l])
        sc, bbd, cbd = s5_prepare(a_re, a_im, ldt, b_re_t, b_im_t, _chan_by_state_c(C_re[l]), _chan_by_state_c(C_im[l]))
        wsm = jnp.where(tri[None], w_spatial[l], 0.0)
        wpool_bd = jnp.zeros((D_POOL, D_POOL), F32)
        for gi in range(len(POOL_WINDOWS)):
            wpool_bd = wpool_bd.at[gi * 64:(gi + 1) * 64, gi * 64:(gi + 1) * 64].set(w_pool[l, gi])
        consts.append(dict(
            disc=(a_re, a_im, ldt, b_re_t, b_im_t), sc=sc, bbd=bbd, cbd=cbd,
            dskip=D_skip[l].reshape(1, D_SSM), bglu=b_glu[l].reshape(1, D_SSM),
            wpool=wpool_bd.astype(BF16), pscale=pool_scale[l].reshape(1, D_POOL),
            lng=sgu_ln_g[l].reshape(1, D_SGU), lnb=sgu_ln_b[l].reshape(1, D_SGU),
            wsm=wsm.astype(BF16), wsmt=jnp.transpose(wsm, (0, 2, 1)).astype(BF16),
            bsp=jnp.repeat(b_spatial[l].T, HEAD_DIM, axis=1),
            gmix=g_mix[l].reshape(1, D), gffn=g_ffn[l].reshape(1, D)))

    def mixer_args(l):
        c = consts[l]
        return (c["bbd"], c["cbd"], c["dskip"], full_w[l]["w_glu"], c["bglu"], c["wpool"], c["pscale"], c["lng"], c["lnb"])

    first_needed = [("w_in", 0)]
    keep_gathered(first_needed, exchange_only(gather_of(*first_needed), "gather_first"))
    carried_fwd = {
        ("inproj", 0): [("w_glu", 0), ("w_out", 0)],
        ("mixer", 0): [("w_gate", 0), ("w_up", 0), ("w_down", 0)],
        ("ffn", 0): [("w_in", 1), ("w_glu", 1), ("w_out", 1), ("w_gate", 1)],
        ("mixer", 1): [("w_up", 1), ("w_down", 1)],
    }

    def carried(kind, l):
        which = carried_fwd.get((kind, l))
        return which, (gather_of(*which) if which else None)

    saved = []
    xl = x0
    for l in range(n_layers):
        c, fw = consts[l], full_w[l]
        which, ex = carried("inproj", l)
        (h, z), got = inproj_fwd(xl, c["gmix"], fw["w_in"], f"inproj_fwd_{l}", ex)
        keep_gathered(which or [], got)
        which, ex = carried("mixer", l)
        (ycat, hs, ys), got = mixer_fwd(z, c["sc"], *mixer_args(l), c["wsm"], c["bsp"], perm, f"mixer_fwd_{l}", ex)
        keep_gathered(which or [], got)
        which, ex = carried("ffn", l)
        head = (target, g_final.reshape(1, D)) if l == n_layers - 1 else None
        res, got = ffn_fwd(xl, ycat, fw["w_out"], c["gffn"], fw["w_gate"], fw["w_up"], fw["w_down"], f"ffn_fwd_{l}", ex, head)
        keep_gathered(which or [], got)
        x1, h2, gate, up, act, x2 = res[:6]
        saved.append(dict(x=xl, h=h, z=z, ycat=ycat, hs=hs, ys=ys, x1=x1, h2=h2, gate=gate, up=up, act=act))
        xl = x2
    dx, loss_vec, d_gfinal = xl, res[6], res[7]

    recv_big = {}
    recv_small = [None] * n_layers

    def keep_received(which, arrays):
        for key, a in zip(which, arrays):
            if key[0] == "small":
                recv_small[key[1]] = a
            else:
                recv_big[key] = a

    pending = None
    for l in reversed(range(n_layers)):
        c, fw, sv = consts[l], full_w[l], saved[l]
        (dgu, dx2b, dx1, dx1b, d_gffn), got = ffn_bwd(dx, sv["x1"], sv["gate"], sv["up"], c["gffn"], fw["w_gate"], fw["w_up"],
                                                     fw["w_down"], f"ffn_bwd_{l}", pending[1] if pending else None)
        if pending:
            keep_received(pending[0], got)
        g_gu = wgrad(dgu, sv["h2"], D_FF // 2, f"wgrad_gate_up_{l}")
        g_down = wgrad(sv["act"], dx2b, D_FF // 2, f"wgrad_down_{l}")
        g_out = wgrad(sv["ycat"], dx1b, D, f"wgrad_out_{l}")
        ffn_rows = D_FF // N_DEV
        ex = _Exchange(scatter=[(g_gu, 0, ffn_rows), (g_gu, D_FF, ffn_rows), (g_down, 0, ffn_rows), (g_out, 0, D // N_DEV)])
        (dzb, da, dbt, dct, dd, dbglu, dwglu, dwpool, dpscale, dlng, dlnb, dws, dbsp), got = mixer_bwd(
            dx1b, sv["z"], sv["hs"], sv["ys"], fw["w_out"], c["sc"], *mixer_args(l), c["wsm"], c["wsmt"], c["bsp"],
            perm, f"mixer_bwd_{l}", ex)
        keep_received([("w_gate", l), ("w_up", l), ("w_down", l), ("w_out", l)], got)
        dx, d_gmix = inproj_bwd(dzb, sv["x"], c["gmix"], fw["w_in"], dx1)
        d_are, d_aim, d_ldt, d_bre_t, d_bim_t = s5_param_bwd(*c["disc"], da, dbt)
        small = dict(
            g_mix=d_gmix.reshape(D), A_re=d_are.reshape(N_GRP, N_STATE), A_im=d_aim.reshape(N_GRP, N_STATE),
            log_dt=d_ldt[0, :N_GRP],
            B_re=jnp.transpose(d_bre_t.reshape(GRP, N_GRP, N_STATE), (1, 2, 0)),
            B_im=jnp.transpose(d_bim_t.reshape(GRP, N_GRP, N_STATE), (1, 2, 0)),
            C_re=jnp.transpose(dct[:, :D_ST].reshape(GRP, N_GRP, N_STATE), (1, 0, 2)),
            C_im=-jnp.transpose(dct[:, D_ST:].reshape(GRP, N_GRP, N_STATE), (1, 0, 2)),
            D_skip=dd.reshape(D_SSM), b_glu=dbglu.reshape(D_SSM),
            w_pool=jnp.stack([dwpool[gi * 64:(gi + 1) * 64, gi * 64:(gi + 1) * 64] for gi in range(len(POOL_WINDOWS))]),
            pool_scale=dpscale.reshape(D_POOL), sgu_ln_g=dlng.reshape(D_SGU), sgu_ln_b=dlnb.reshape(D_SGU),
            w_spatial=dws, b_spatial=dbsp[:, :HEADS].T, g_ffn=d_gffn.reshape(D))
        packed = [small[n] for n in SMALL_LAYER] + ([d_gfinal.reshape(D), loss_vec.reshape(D)] if l == 0 else [])
        small_entries = dict(gather=[_pack(packed)], scatter=[(dwglu.astype(BF16), 0, D_SSM // N_DEV)])
        if l > 0:
            g_in = wgrad(dzb, sv["h"], D_IN, f"wgrad_in_{l}")
            small_entries["scatter"].append((g_in, 0, D_IN // N_DEV))
            pending = ([("small", l), ("w_glu", l), ("w_in", l)], _Exchange(**small_entries))
        else:
            g_in, got = wgrad(dzb, sv["h"], D_IN, f"wgrad_in_{l}", tk=TK_WGRAD // 4, exchange=_Exchange(**small_entries))
            keep_received([("small", l), ("w_glu", l)], got)
            pending = ([("w_in", l)], _Exchange(scatter=[(g_in, 0, D_IN // N_DEV)]))
    grad_x = dx
    keep_received(pending[0], exchange_only(pending[1], "exchange_last"))

    out = {}
    for n in BIG_NAMES:
        tr = (lambda a: jnp.transpose(a, (0, 2, 1))) if n in COLUMN_SHARDED else (lambda a: a)
        res = adamw_layers([recv_big[(n, l)] for l in range(n_layers)], tr(W[n]), tr(M[n]), tr(V[n]), f"adamw_{n}")
        out[n] = [tr(r) for r in res]

    seg_rows = [(-(-math.prod(W[n].shape[1:]) // SEG)) * (SEG // 128) for n in SMALL_LAYER]
    segments, src, dst = [], 0, 0
    for rows in seg_rows:
        segments += [(l, src, dst + l * rows, rows) for l in range(n_layers)]
        src += rows
        dst += n_layers * rows
    tile_rows = SEG // 128
    segments += [(0, src, dst, tile_rows), (0, src + tile_rows, dst + tile_rows, tile_rows)]

    def pack_params(P):
        parts = []
        for n, rows in zip(SMALL_LAYER, seg_rows):
            flat = P[n].reshape(n_layers, -1)
            parts.append(jnp.pad(flat, ((0, 0), (0, rows * 128 - flat.shape[1]))).reshape(-1))
        return jnp.concatenate(parts + [P["g_final"], jnp.zeros((SEG,), F32)]).reshape(-1, 128)

    res = adamw_segments(recv_small, segments, pack_params(W), pack_params(M), pack_params(V), "adamw_small")
    for j in range(4):
        flat, off = res[j].reshape(-1), 0
        for n, rows in zip(SMALL_LAYER, seg_rows):
            size = math.prod(W[n].shape[1:])
            piece = flat[off:off + n_layers * rows * 128].reshape(n_layers, rows * 128)[:, :size].reshape(W[n].shape)
            out.setdefault(n, []).append(piece)
            off += n_layers * rows * 128
        out.setdefault("g_final", []).append(flat[off:off + D])
        if j == 0:
            loss = (0.5 / D) * jnp.sum(flat[off + SEG:off + SEG + D])

    return (loss, grad_x[None], *[out[n][0] for n in WEIGHT_ORDER], *[out[n][1] for n in WEIGHT_ORDER],
            *[out[n][2] for n in WEIGHT_ORDER], *[out[n][3] for n in WEIGHT_ORDER])
```

```python
import functools
import math

import jax
import jax.numpy as jnp
from jax import lax
from jax.experimental import pallas as pl
from jax.experimental.pallas import tpu as pltpu

F32 = jnp.float32
BF16 = jnp.bfloat16
S = jax.ShapeDtypeStruct

N_DEV = 8
D = 1024
D_SSM = 384
N_GRP = 24
GRP = 16
N_STATE = 64
D_ST = N_GRP * N_STATE
D_POOL = 256
POOL_WINDOWS = (2, 4, 8, 16)
HALO = 16
D_SGU = 384
HEADS = 6
HEAD_DIM = 64
CHUNK = 128
D_IN = 1408
D_FF = 2816
EPS = 1e-6
SCAN_BLK = 8

ADAM_LR = 0.001
ADAM_B1 = 0.9
ADAM_B2 = 0.999
ADAM_EPS = 1e-08
ADAM_WD = 0.01
ADAM_STEP = 10

GELU_C0 = math.sqrt(2.0 / math.pi)
GELU_C1 = 0.044715

TT_MIX = 256
SEG_LEN = TT_MIX // SCAN_BLK
TT_FFN = 256
TT_PROJ = 512
TK_WGRAD = 2048
VMEM_MB = 2 ** 20


def _cp(vmem_mb, grid_dims=0):
    kw = dict(vmem_limit_bytes=int(vmem_mb * VMEM_MB))
    if grid_dims:
        kw["dimension_semantics"] = ("arbitrary",) * grid_dims
    return pltpu.CompilerParams(**kw)


def _row(tt, n):
    return pl.BlockSpec((tt, n), lambda i: (i, 0))


def _full(shape):
    nd = len(shape)
    return pl.BlockSpec(shape, lambda *_: (0,) * nd)


def _nn(a, b):
    return jnp.dot(a, b, preferred_element_type=F32)


def _nt(a, b):
    return lax.dot_general(a, b, (((1,), (1,)), ((), ())), preferred_element_type=F32)


def _tn(a, b):
    return lax.dot_general(a, b, (((0,), (0,)), ((), ())), preferred_element_type=F32)


def _rowsum(x):
    return jnp.sum(x, axis=0, keepdims=True)


def _rms(x):
    r = lax.rsqrt(jnp.mean(x * x, axis=-1, keepdims=True) + EPS)
    return x * r, r


def _rms_bwd(dy, xn, r, g):
    dyg = dy * g
    return r * (dyg - xn * jnp.mean(dyg * xn, axis=-1, keepdims=True))


def _gelu(x):
    s = jax.nn.sigmoid(x * (2.0 * GELU_C0 + (2.0 * GELU_C0 * GELU_C1) * (x * x)))
    return x * s, s


def _gelu_grad(x, s):
    return s * (1.0 + x * (1.0 - s) * (2.0 * GELU_C0 + (6.0 * GELU_C0 * GELU_C1) * (x * x)))


def _discretise(a_re, a_im, ldt, b_re, b_im):
    dt = jnp.exp(ldt)
    mag = jnp.exp(a_re * dt)
    ar = mag * jnp.cos(a_im * dt)
    ai = mag * jnp.sin(a_im * dt)
    den = a_re * a_re + a_im * a_im
    f_re = ((ar - 1.0) * a_re + ai * a_im) / den
    f_im = (ai * a_re - (ar - 1.0) * a_im) / den
    bb_re = f_re * b_re - f_im * b_im
    bb_im = f_re * b_im + f_im * b_re
    return ar, ai, bb_re, bb_im


def _group_mask(rows, cols):
    r = lax.broadcasted_iota(jnp.int32, (rows, cols), 0) // GRP
    c = lax.broadcasted_iota(jnp.int32, (rows, cols), 1)
    c = jnp.where(c >= D_ST, c - D_ST, c) // N_STATE
    return r == c


def s5_prepare(a_re, a_im, ldt, b_re_t, b_im_t, c_re_t, c_im_t):
    def body(are_ref, aim_ref, ldt_ref, bre_ref, bim_ref, cre_ref, cim_ref, sc_ref, bbd_ref, cbd_ref):
        ar, ai, bb_re, bb_im = _discretise(are_ref[...], aim_ref[...], ldt_ref[...], bre_ref[...], bim_ref[...])
        mask = _group_mask(D_SSM, 2 * D_ST)
        bb = jnp.concatenate([jnp.tile(bb_re, (N_GRP, 1)), jnp.tile(bb_im, (N_GRP, 1))], axis=1)
        bbd_ref[...] = jnp.where(mask, bb, 0.0).astype(BF16)
        cc = jnp.concatenate([jnp.tile(cre_ref[...], (N_GRP, 1)), -jnp.tile(cim_ref[...], (N_GRP, 1))], axis=1)
        cbd_ref[...] = jnp.where(mask, cc, 0.0).astype(BF16)
        pr, pi = ar, ai
        for _ in range(SEG_LEN - 1):
            pr, pi = pr * ar - pi * ai, pr * ai + pi * ar
        for k, v in enumerate((ar, ai, pr, pi)):
            sc_ref[8 * k:8 * k + 8, :] = jnp.broadcast_to(v, (SCAN_BLK, D_ST))

    return pl.pallas_call(
        body, name="s5_prepare",
        out_shape=[S((32, D_ST), F32), S((D_SSM, 2 * D_ST), BF16), S((D_SSM, 2 * D_ST), BF16)],
        compiler_params=_cp(40),
    )(a_re, a_im, ldt, b_re_t, b_im_t, c_re_t, c_im_t)


def s5_param_bwd(a_re, a_im, ldt, b_re_t, b_im_t, da, dbt):
    def body(are_ref, aim_ref, ldt_ref, bre_ref, bim_ref, da_ref, dbt_ref, o_are, o_aim, o_ldt, o_bre, o_bim):
        _, vjp = jax.vjp(_discretise, are_ref[...], aim_ref[...], ldt_ref[...], bre_ref[...], bim_ref[...])
        da = da_ref[...]
        dbt = dbt_ref[...]
        g_are, g_aim, g_ldt, g_bre, g_bim = vjp((da[:, :D_ST], da[:, D_ST:], dbt[:, :D_ST], dbt[:, D_ST:]))
        o_are[...] = g_are
        o_aim[...] = g_aim
        o_bre[...] = g_bre
        o_bim[...] = g_bim
        grp = lax.broadcasted_iota(jnp.int32, (1, D_ST), 1) // N_STATE
        lane = lax.broadcasted_iota(jnp.int32, (1, 128), 1)
        out = jnp.zeros((1, 128), F32)
        for g in range(N_GRP):
            out = jnp.where(lane == g, jnp.sum(jnp.where(grp == g, g_ldt, 0.0), axis=1, keepdims=True), out)
        o_ldt[...] = out

    return pl.pallas_call(
        body, name="s5_param_bwd",
        out_shape=[S((1, D_ST), F32), S((1, D_ST), F32), S((1, 128), F32), S((GRP, D_ST), F32), S((GRP, D_ST), F32)],
        compiler_params=_cp(16),
    )(a_re, a_im, ldt, b_re_t, b_im_t, da, dbt)


_CH = ((0, 256), (256, D_SSM))
_ST = ((0, 1024), (1024, D_ST))


def _bd_expand(xb, w_ref, out_ref):
    for (c0, c1), (s0, s1) in zip(_CH, _ST):
        for half in (0, D_ST):
            out_ref[:, half + s0:half + s1] = _nn(xb[:, c0:c1], w_ref[c0:c1, half + s0:half + s1])


def _bd_contract(hb, w_ref):
    parts = []
    for (c0, c1), (s0, s1) in zip(_CH, _ST):
        parts.append(_nt(hb[:, s0:s1], w_ref[c0:c1, s0:s1]) + _nt(hb[:, D_ST + s0:D_ST + s1], w_ref[c0:c1, D_ST + s0:D_ST + s1]))
    return jnp.concatenate(parts, axis=1)


def _bd_accumulate(acc_ref, xb, hb):
    for (c0, c1), (s0, s1) in zip(_CH, _ST):
        for half in (0, D_ST):
            acc_ref[c0:c1, half + s0:half + s1] += _tn(xb[:, c0:c1], hb[:, half + s0:half + s1])


def _interleave_matrices(tt):
    r = lax.broadcasted_iota(jnp.int32, (tt, tt), 0)
    t = lax.broadcasted_iota(jnp.int32, (tt, tt), 1)
    p = (t == (r % SCAN_BLK) * (tt // SCAN_BLK) + r // SCAN_BLK).astype(BF16)
    return p, p.T


def _interleave_f32(p, x):
    hi = x.astype(BF16)
    lo = (x - hi.astype(F32)).astype(BF16)
    return _nn(p, hi) + _nn(p, lo)


def _scan_tile(buf_ref, sc_ref, carry_ref, n_blk, reverse):
    ar = sc_ref[0:8, :]
    ai = -sc_ref[8:16, :] if reverse else sc_ref[8:16, :]

    def rows(i):
        blk = (n_blk - 1 - i) if reverse else i
        return pl.ds(pl.multiple_of(blk * SCAN_BLK, SCAN_BLK), SCAN_BLK)

    def local(i, x):
        xr, xi = x
        r = rows(i)
        xr, xi = buf_ref[r, 0:D_ST] + ar * xr - ai * xi, buf_ref[r, D_ST:2 * D_ST] + ar * xi + ai * xr
        buf_ref[r, 0:D_ST] = xr
        buf_ref[r, D_ST:2 * D_ST] = xi
        return xr, xi

    zero = jnp.zeros((SCAN_BLK, D_ST), F32)
    end_r, end_i = lax.fori_loop(0, n_blk, local, (zero, zero), unroll=True)

    seg_r = sc_ref[16:17, :]
    seg_i = -sc_ref[24:25, :] if reverse else sc_ref[24:25, :]
    cr, ci = carry_ref[0:1, 0:D_ST], carry_ref[0:1, D_ST:2 * D_ST]
    sub = lax.broadcasted_iota(jnp.int32, (SCAN_BLK, D_ST), 0)
    in_r, in_i = zero, zero
    for s in (reversed(range(SCAN_BLK)) if reverse else range(SCAN_BLK)):
        in_r = jnp.where(sub == s, cr, in_r)
        in_i = jnp.where(sub == s, ci, in_i)
        cr, ci = end_r[s:s + 1, :] + seg_r * cr - seg_i * ci, end_i[s:s + 1, :] + seg_r * ci + seg_i * cr
    carry_ref[0:1, 0:D_ST] = cr
    carry_ref[0:1, D_ST:2 * D_ST] = ci

    def fix(i, d):
        dr, di = d
        dr, di = ar * dr - ai * di, ar * di + ai * dr
        r = rows(i)
        buf_ref[r, 0:D_ST] += dr
        buf_ref[r, D_ST:2 * D_ST] += di
        return dr, di

    lax.fori_loop(0, n_blk, fix, (in_r, in_i), unroll=True)


def _lane_windows(n):
    lane = lax.broadcasted_iota(jnp.int32, (1, n), 1)
    return lane // (D_POOL // len(POOL_WINDOWS))


def _select_window(grp, s2, s4, s8, s16):
    return jnp.where(grp == 0, s2, jnp.where(grp == 1, s4, jnp.where(grp == 2, s8, s16)))


def _pool_fwd(pbuf_ref, zb, halo, tile_idx, tt):
    pbuf_ref[0:HALO, :] = halo
    pbuf_ref[HALO:HALO + tt, :] = zb
    x = pbuf_ref[...]
    s2 = x + pltpu.roll(x, 1, axis=0)
    s4 = s2 + pltpu.roll(s2, 2, axis=0)
    s8 = s4 + pltpu.roll(s4, 4, axis=0)
    s16 = s8 + pltpu.roll(s8, 8, axis=0)
    grp = _lane_windows(D_POOL)
    win = _select_window(grp, s2, s4, s8, s16)[HALO:HALO + tt, :]
    width = _select_window(grp, 2.0, 4.0, 8.0, 16.0).astype(F32)
    pos = (tile_idx * tt + 1 + lax.broadcasted_iota(jnp.int32, (tt, 1), 0)).astype(F32)
    cnt = jnp.minimum(pos, width)
    return win / cnt - zb, cnt


def _sgu_fwd(zu, zv, lng, lnb, wsm_ref, bsp, mix_ref, tt):
    u, tu = _gelu(zu)
    v, tv = _gelu(zv)
    mu = jnp.mean(v, axis=-1, keepdims=True)
    vc = v - mu
    rstd = lax.rsqrt(jnp.mean(vc * vc, axis=-1, keepdims=True) + EPS)
    vhat = vc * rstd
    vnb = (vhat * lng + lnb).astype(BF16)
    _head_mix(wsm_ref, vnb, mix_ref, tt, bsp)
    return u, tu, tv, vhat, rstd, vnb


def _chunk_pairs(tt):
    n_ch = tt // CHUNK
    return [list(range(c, min(c + 2, n_ch))) for c in range(0, n_ch, 2)]


def _head_cols(xb, chunks, h):
    return jnp.concatenate([xb[c * CHUNK:(c + 1) * CHUNK, h * HEAD_DIM:(h + 1) * HEAD_DIM] for c in chunks], axis=1)


def _head_mix(w_ref, xb, out_ref, tt, add=None):
    for chunks in _chunk_pairs(tt):
        per_head = [_nn(w_ref[h], _head_cols(xb, chunks, h)) for h in range(HEADS)]
        for k, c in enumerate(chunks):
            block = jnp.concatenate([r[:, k * HEAD_DIM:(k + 1) * HEAD_DIM] for r in per_head], axis=1)
            out_ref[c * CHUNK:(c + 1) * CHUNK, :] = block if add is None else block + add


def mixer_fwd(z, sc, bbd, cbd, dskip, wglu, bglu, wpool, pscale, lng, lnb, wsm, bsp, perm, name, exchange=None):
    T = z.shape[0]
    tt = TT_MIX
    n_tiles = T // tt

    def body(z_ref, sc_ref, bbd_ref, cbd_ref, dskip_ref, wglu_ref, bglu_ref, wpool_ref, pscale_ref, lng_ref, lnb_ref,
             wsm_ref, bsp_ref, p_ref, pt_ref, ycat_ref, hs_ref, ys_ref, carry_ref, halo_ref, pbuf_ref, mix_ref):
        i = pl.program_id(0)

        @pl.when(i == 0)
        def _():
            carry_ref[...] = jnp.zeros_like(carry_ref)
            halo_ref[...] = jnp.zeros_like(halo_ref)

        za = z_ref[:, 0:D_SSM]
        zb = z_ref[:, D_SSM:D_SSM + D_POOL]
        zu = z_ref[:, D_SSM + D_POOL:D_SSM + D_POOL + D_SGU]
        zv = z_ref[:, D_SSM + D_POOL + D_SGU:D_IN]
        p, pt = p_ref[...], pt_ref[...]
        za = _interleave_f32(p, za)
        _bd_expand(za.astype(BF16), bbd_ref, hs_ref)
        _scan_tile(hs_ref, sc_ref, carry_ref, tt // SCAN_BLK, reverse=False)
        y = _bd_contract(hs_ref[...].astype(BF16), cbd_ref) + dskip_ref[...] * za
        ys_ref[...] = y
        g, _ = _gelu(y)
        q = _nn(g.astype(BF16), wglu_ref[...]) + bglu_ref[...]
        ycat_ref[:, 0:D_SSM] = _nn(pt, (g * jax.nn.sigmoid(q)).astype(BF16)).astype(BF16)
        pooled, _ = _pool_fwd(pbuf_ref, zb, halo_ref[...], i, tt)
        halo_ref[...] = zb[tt - HALO:tt, :]
        ycat_ref[:, D_SSM:D_SSM + D_POOL] = (_nn(pooled.astype(BF16), wpool_ref[...]) * pscale_ref[...]).astype(BF16)
        u, _, _, _, _, _ = _sgu_fwd(zu, zv, lng_ref[...], lnb_ref[...], wsm_ref, bsp_ref[...], mix_ref, tt)
        ycat_ref[:, D_SSM + D_POOL:D] = (u * mix_ref[...]).astype(BF16)

    return _pallas(
        body, name=name, grid=(n_tiles,),
        in_specs=[_row(tt, D_IN), _full((32, D_ST)), _full((D_SSM, 2 * D_ST)), _full((D_SSM, 2 * D_ST)),
                  _full((1, D_SSM)), _full((D_SSM, D_SSM)), _full((1, D_SSM)), _full((D_POOL, D_POOL)),
                  _full((1, D_POOL)), _full((1, D_SGU)), _full((1, D_SGU)), _full((HEADS, CHUNK, CHUNK)),
                  _full((CHUNK, D_SGU)), _full((tt, tt)), _full((tt, tt))],
        out_specs=[_row(tt, D), _row(tt, 2 * D_ST), _row(tt, D_SSM)],
        out_shape=[S((T, D), BF16), S((T, 2 * D_ST), F32), S((T, D_SSM), F32)],
        scratch_shapes=[pltpu.VMEM((SCAN_BLK, 2 * D_ST), F32), pltpu.VMEM((HALO, D_POOL), F32),
                        pltpu.VMEM((tt + HALO, D_POOL), F32), pltpu.VMEM((tt, D_SGU), F32)],
        vmem_mb=48, operands=(z, sc, bbd, cbd, dskip, wglu, bglu, wpool, pscale, lng, lnb, wsm, bsp, *perm),
        exchange=exchange)


def mixer_bwd(dx1b, z, hs, ys, wout, sc, bbd, cbd, dskip, wglu, bglu, wpool, pscale, lng, lnb, wsm, wsmt, bsp, perm, name,
              exchange=None):
    T = z.shape[0]
    tt = TT_MIX
    n_tiles = T // tt

    def rev(i):
        return n_tiles - 1 - i

    def body(dx_ref, z_ref, zprev_ref, hs_ref, hsprev_ref, ys_ref, wout_ref, sc_ref, bbd_ref, cbd_ref, dskip_ref,
             wglu_ref, bglu_ref, wpool_ref, pscale_ref, lng_ref, lnb_ref, wsm_ref, wsmt_ref, bsp_ref, p_ref, pt_ref,
             dz_ref, o_da, o_dbt, o_dct, o_dd, o_dbglu, o_dwglu, o_dwpool, o_dpscale, o_dlng, o_dlnb, o_dws, o_dbsp,
             gbuf_ref, carry_ref, accb_ref, accc_ref, ehalo_ref, pbuf_ref, mix_ref, dvn_ref, accw_ref, accm_ref):
        i = pl.program_id(0)
        tile = rev(i)

        @pl.when(i == 0)
        def _():
            carry_ref[...] = jnp.zeros_like(carry_ref)
            accb_ref[...] = jnp.zeros_like(accb_ref)
            accc_ref[...] = jnp.zeros_like(accc_ref)
            for o in (o_da, o_dd, o_dbglu, o_dwglu):
                o[...] = jnp.zeros_like(o)

        p, pt = p_ref[...], pt_ref[...]
        dxb = dx_ref[...]
        d_a = _nt(_nn(p, dxb).astype(BF16), wout_ref[0:D_SSM, :])
        d_bc = _nt(dxb, wout_ref[D_SSM:D, :])
        za = _interleave_f32(p, z_ref[:, 0:D_SSM])
        first_tile = (tile > 0).astype(F32)

        y = ys_ref[...]
        g, tg = _gelu(y)
        gb = g.astype(BF16)
        sg = jax.nn.sigmoid(_nn(gb, wglu_ref[...]) + bglu_ref[...])
        dq = d_a * g * sg * (1.0 - sg)
        dqb = dq.astype(BF16)
        o_dbglu[...] += _rowsum(dq)
        o_dwglu[...] += _tn(gb, dqb)
        dy = (d_a * sg + _nt(dqb, wglu_ref[...])) * _gelu_grad(y, tg)
        o_dd[...] += _rowsum(dy * za)
        dyb = dy.astype(BF16)
        _bd_accumulate(accc_ref, dyb, hs_ref[...].astype(BF16))
        _bd_expand(dyb, cbd_ref, gbuf_ref)
        _scan_tile(gbuf_ref, sc_ref, carry_ref, tt // SCAN_BLK, reverse=True)
        hprev = hsprev_ref[SCAN_BLK - 1:SCAN_BLK, :] * first_tile
        sub = lax.broadcasted_iota(jnp.int32, (SCAN_BLK, 1), 0)
        edge = jnp.where(sub == 0, hprev, pltpu.roll(hs_ref[tt - SCAN_BLK:tt, :], 1, axis=0))

        def da_terms(gr, gi, hr, hi):
            return _rowsum(gr * hr + gi * hi), _rowsum(gi * hr - gr * hi)

        body_re, body_im = da_terms(gbuf_ref[SCAN_BLK:tt, 0:D_ST], gbuf_ref[SCAN_BLK:tt, D_ST:],
                                    hs_ref[0:tt - SCAN_BLK, 0:D_ST], hs_ref[0:tt - SCAN_BLK, D_ST:])
        edge_re, edge_im = da_terms(gbuf_ref[0:SCAN_BLK, 0:D_ST], gbuf_ref[0:SCAN_BLK, D_ST:], edge[:, 0:D_ST], edge[:, D_ST:])
        o_da[:, 0:D_ST] += body_re + edge_re
        o_da[:, D_ST:] += body_im + edge_im
        gtb = gbuf_ref[...].astype(BF16)
        _bd_accumulate(accb_ref, za.astype(BF16), gtb)
        dza = (dy * dskip_ref[...] + _bd_contract(gtb, bbd_ref)).astype(BF16)
        dz_ref[:, 0:D_SSM] = _nn(pt, dza).astype(BF16)
        _pool_sgu_bwd(i, n_tiles, tile, tt, d_bc, z_ref, zprev_ref, wpool_ref, pscale_ref, lng_ref, lnb_ref, wsm_ref,
                      wsmt_ref, bsp_ref, dz_ref, o_dwpool, o_dpscale, o_dlng, o_dlnb, o_dws, o_dbsp,
                      ehalo_ref, pbuf_ref, mix_ref, dvn_ref, accw_ref, accm_ref)

        @pl.when(i == n_tiles - 1)
        def _():
            mask = _group_mask(D_SSM, 2 * D_ST)
            for acc_ref, o_ref in ((accb_ref, o_dbt), (accc_ref, o_dct)):
                fold = jnp.zeros((GRP, 2 * D_ST), F32)
                for gidx in range(N_GRP):
                    rows = slice(gidx * GRP, (gidx + 1) * GRP)
                    fold = fold + jnp.where(mask[rows, :], acc_ref[rows, :], 0.0)
                o_ref[...] = fold

    def rowr(n):
        return pl.BlockSpec((tt, n), lambda i: (rev(i), 0))

    zprev_spec = pl.BlockSpec((HALO, D_IN), lambda i: (jnp.maximum(rev(i) * (tt // HALO) - 1, 0), 0))
    hsprev_spec = pl.BlockSpec((SCAN_BLK, 2 * D_ST), lambda i: (jnp.maximum(rev(i) * (tt // SCAN_BLK) - 1, 0), 0))
    small = [S((1, 2 * D_ST), F32), S((GRP, 2 * D_ST), F32), S((GRP, 2 * D_ST), F32), S((1, D_SSM), F32),
             S((1, D_SSM), F32), S((D_SSM, D_SSM), F32), S((D_POOL, D_POOL), F32), S((1, D_POOL), F32),
             S((1, D_SGU), F32), S((1, D_SGU), F32), S((HEADS, CHUNK, CHUNK), F32), S((CHUNK, 128), F32)]
    return _pallas(
        body, name=name, grid=(n_tiles,),
        in_specs=[rowr(D), rowr(D_IN), zprev_spec, rowr(2 * D_ST), hsprev_spec, rowr(D_SSM), _full((D, D)),
                  _full((32, D_ST)), _full((D_SSM, 2 * D_ST)), _full((D_SSM, 2 * D_ST)), _full((1, D_SSM)),
                  _full((D_SSM, D_SSM)), _full((1, D_SSM)), _full((D_POOL, D_POOL)), _full((1, D_POOL)),
                  _full((1, D_SGU)), _full((1, D_SGU)), _full((HEADS, CHUNK, CHUNK)), _full((HEADS, CHUNK, CHUNK)),
                  _full((CHUNK, D_SGU)), _full((tt, tt)), _full((tt, tt))],
        out_specs=[rowr(D_IN)] + [_full(s.shape) for s in small],
        out_shape=[S((T, D_IN), BF16)] + small,
        scratch_shapes=[pltpu.VMEM((tt, 2 * D_ST), F32), pltpu.VMEM((SCAN_BLK, 2 * D_ST), F32),
                        pltpu.VMEM((D_SSM, 2 * D_ST), F32), pltpu.VMEM((D_SSM, 2 * D_ST), F32),
                        pltpu.VMEM((HALO, D_POOL), F32), pltpu.VMEM((tt + HALO, D_POOL), F32),
                        pltpu.VMEM((tt, D_SGU), F32), pltpu.VMEM((tt, D_SGU), F32),
                        pltpu.VMEM((HEADS, CHUNK, CHUNK), F32), pltpu.VMEM((CHUNK, D_SGU), F32)],
        vmem_mb=56, exchange=exchange,
        operands=(dx1b, z, z, hs, hs, ys, wout, sc, bbd, cbd, dskip, wglu, bglu, wpool, pscale, lng, lnb, wsm, wsmt, bsp, *perm))


def inproj_fwd(x, g, w_t, name, exchange=None):
    T = x.shape[0]
    tt = TT_PROJ

    def body(x_ref, g_ref, w_ref, h_ref, z_ref):
        xn, _ = _rms(x_ref[...])
        h = (xn * g_ref[...]).astype(BF16)
        h_ref[...] = h
        z_ref[...] = _nt(h, w_ref[...])

    return _pallas(
        body, name=name, grid=(T // tt,),
        in_specs=[_row(tt, D), _full((1, D)), _full((D_IN, D))],
        out_specs=[_row(tt, D), _row(tt, D_IN)],
        out_shape=[S((T, D), BF16), S((T, D_IN), F32)],
        scratch_shapes=[], vmem_mb=40, operands=(x, g, w_t), exchange=exchange)


def norm_fwd(x, g, name, exchange=None):
    T = x.shape[0]
    tt = TT_PROJ

    def body(x_ref, g_ref, h_ref):
        xn, _ = _rms(x_ref[...])
        h_ref[...] = (xn * g_ref[...]).astype(BF16)

    return _pallas(body, name=name, grid=(T // tt,), in_specs=[_row(tt, D), _full((1, D))], out_specs=[_row(tt, D)],
                   out_shape=[S((T, D), BF16)], scratch_shapes=[], vmem_mb=24, operands=(x, g), exchange=exchange)


def proj_fwd(h, w_t, name, exchange=None):
    T = h.shape[0]
    tt = TT_PROJ

    def body(h_ref, w_ref, z_ref):
        z_ref[...] = _nt(h_ref[...], w_ref[...])

    return _pallas(body, name=name, grid=(T // tt,), in_specs=[_row(tt, D), _full((D_IN, D))], out_specs=[_row(tt, D_IN)],
                   out_shape=[S((T, D_IN), F32)], scratch_shapes=[], vmem_mb=32, operands=(h, w_t), exchange=exchange)


def inproj_bwd(dzb, x, g, w_t, dx1):
    T = x.shape[0]
    tt = TT_PROJ

    def body(dz_ref, x_ref, g_ref, w_ref, dx1_ref, dx_ref, dg_ref):
        @pl.when(pl.program_id(0) == 0)
        def _():
            dg_ref[...] = jnp.zeros_like(dg_ref)

        dh = _nn(dz_ref[...], w_ref[...])
        xn, r = _rms(x_ref[...])
        dg_ref[...] += _rowsum(dh * xn)
        dx_ref[...] = dx1_ref[...] + _rms_bwd(dh, xn, r, g_ref[...])

    return pl.pallas_call(
        body, name="inproj_bwd", grid=(T // tt,),
        in_specs=[_row(tt, D_IN), _row(tt, D), _full((1, D)), _full((D_IN, D)), _row(tt, D)],
        out_specs=[_row(tt, D), _full((1, D))],
        out_shape=[S((T, D), F32), S((1, D), F32)],
        compiler_params=_cp(40, 1),
    )(dzb, x, g, w_t, dx1)


def _load_weights(pairs, sem):
    @pl.when(pl.program_id(0) == 0)
    def _():
        copies = [pltpu.make_async_copy(src, dst, sem.at[k]) for k, (src, dst) in enumerate(pairs)]
        for cp in copies:
            cp.start()
        for cp in copies:
            cp.wait()


def ffn_fwd(x, ycat, wout, g, wg_t, wu_t, wd, name, exchange=None, head=None):
    T = x.shape[0]
    tt = TT_FFN
    any_spec = pl.BlockSpec(memory_space=pl.ANY)

    def body(*refs):
        if head is None:
            (x_ref, ycat_ref, g_ref, wout_hbm, wg_hbm, wu_hbm, wd_hbm,
             x1_ref, h_ref, gate_ref, up_ref, act_ref, x2_ref, wout_v, wg_v, wu_v, wd_v, sem) = refs
        else:
            (x_ref, ycat_ref, g_ref, t_ref, gf_ref, wout_hbm, wg_hbm, wu_hbm, wd_hbm,
             x1_ref, h_ref, gate_ref, up_ref, act_ref, x2_ref, lvec_ref, dgf_ref, wout_v, wg_v, wu_v, wd_v, sem) = refs
        _ffn_fwd_tile(x_ref, ycat_ref, g_ref, wout_hbm, wg_hbm, wu_hbm, wd_hbm, x1_ref, h_ref, gate_ref, up_ref, act_ref,
                      x2_ref, wout_v, wg_v, wu_v, wd_v, sem)
        if head is not None:
            @pl.when(pl.program_id(0) == 0)
            def _():
                lvec_ref[...] = jnp.zeros_like(lvec_ref)
                dgf_ref[...] = jnp.zeros_like(dgf_ref)

            xn, r = _rms(x2_ref[...])
            gf = gf_ref[...]
            err = xn * gf - t_ref[...]
            lvec_ref[...] += _rowsum(err * err)
            dy = err * (1.0 / D)
            dgf_ref[...] += _rowsum(dy * xn)
            x2_ref[...] = _rms_bwd(dy, xn, r, gf)

    def _ffn_fwd_tile(x_ref, ycat_ref, g_ref, wout_hbm, wg_hbm, wu_hbm, wd_hbm,
                      x1_ref, h_ref, gate_ref, up_ref, act_ref, x2_ref, wout_v, wg_v, wu_v, wd_v, sem):
        _load_weights([(wout_hbm, wout_v), (wg_hbm, wg_v), (wu_hbm, wu_v), (wd_hbm, wd_v)], sem)
        x1 = x_ref[...] + _nn(ycat_ref[...], wout_v[...])
        x1_ref[...] = x1
        xn, _ = _rms(x1)
        h = (xn * g_ref[...]).astype(BF16)
        h_ref[...] = h
        gate = _nt(h, wg_v[...])
        up = _nt(h, wu_v[...])
        gate_ref[...] = gate.astype(BF16)
        up_ref[...] = up.astype(BF16)
        act = (gate * jax.nn.sigmoid(gate) * up).astype(BF16)
        act_ref[...] = act
        x2_ref[...] = x1 + _nn(act, wd_v[...])

    with_head = head is not None
    return _pallas(
        body, name=name, grid=(T // tt,),
        in_specs=[_row(tt, D), _row(tt, D), _full((1, D))] + ([_row(tt, D), _full((1, D))] if with_head else [])
        + [any_spec, any_spec, any_spec, any_spec],
        out_specs=[_row(tt, D), _row(tt, D), _row(tt, D_FF), _row(tt, D_FF), _row(tt, D_FF), _row(tt, D)]
        + ([_full((1, D)), _full((1, D))] if with_head else []),
        out_shape=[S((T, D), F32), S((T, D), BF16), S((T, D_FF), BF16), S((T, D_FF), BF16), S((T, D_FF), BF16),
                   S((T, D), F32)] + ([S((1, D), F32), S((1, D), F32)] if with_head else []),
        scratch_shapes=[pltpu.VMEM((D, D), BF16), pltpu.VMEM((D_FF, D), BF16), pltpu.VMEM((D_FF, D), BF16),
                        pltpu.VMEM((D_FF, D), BF16), pltpu.SemaphoreType.DMA((4,))],
        vmem_mb=56, operands=(x, ycat, g) + (tuple(head) if with_head else ()) + (wout, wg_t, wu_t, wd), exchange=exchange)


def _pool_sgu_bwd(i, n_tiles, tile, tt, d_bc, z_ref, zprev_ref, wpool_ref, pscale_ref, lng_ref, lnb_ref, wsm_ref, wsmt_ref,
                  bsp_ref, dz_ref, o_dwpool, o_dpscale, o_dlng, o_dlnb, o_dws, o_dbsp,
                  ehalo_ref, pbuf_ref, mix_ref, dvn_ref, accw_ref, accm_ref):
    @pl.when(i == 0)
    def _():
        ehalo_ref[...] = jnp.zeros_like(ehalo_ref)
        accw_ref[...] = jnp.zeros_like(accw_ref)
        accm_ref[...] = jnp.zeros_like(accm_ref)
        for o in (o_dwpool, o_dpscale, o_dlng, o_dlnb):
            o[...] = jnp.zeros_like(o)

    d_b = d_bc[:, 0:D_POOL]
    d_c = d_bc[:, D_POOL:D_POOL + D_SGU]
    zb = z_ref[:, D_SSM:D_SSM + D_POOL]
    zu = z_ref[:, D_SSM + D_POOL:D_SSM + D_POOL + D_SGU]
    zv = z_ref[:, D_SSM + D_POOL + D_SGU:D_IN]
    not_first = (tile > 0).astype(F32)

    pooled, cnt = _pool_fwd(pbuf_ref, zb, zprev_ref[:, D_SSM:D_SSM + D_POOL] * not_first, tile, tt)
    pooledb = pooled.astype(BF16)
    mixed = _nn(pooledb, wpool_ref[...])
    o_dpscale[...] += _rowsum(d_b * mixed)
    dmixb = (d_b * pscale_ref[...]).astype(BF16)
    o_dwpool[...] += _tn(pooledb, dmixb)
    dpooled = _nt(dmixb, wpool_ref[...])
    e = dpooled / cnt
    pbuf_ref[0:tt, :] = e
    pbuf_ref[tt:tt + HALO, :] = ehalo_ref[...]
    ehalo_ref[...] = e[0:HALO, :]
    x = pbuf_ref[...]
    n = tt + HALO
    f2 = x + pltpu.roll(x, n - 1, axis=0)
    f4 = f2 + pltpu.roll(f2, n - 2, axis=0)
    f8 = f4 + pltpu.roll(f4, n - 4, axis=0)
    f16 = f8 + pltpu.roll(f8, n - 8, axis=0)
    fwd_sum = _select_window(_lane_windows(D_POOL), f2, f4, f8, f16)[0:tt, :]
    dz_ref[:, D_SSM:D_SSM + D_POOL] = (fwd_sum - dpooled).astype(BF16)

    lng = lng_ref[...]
    u, su, sv, vhat, rstd, vnb = _sgu_fwd(zu, zv, lng, lnb_ref[...], wsm_ref, bsp_ref[...], mix_ref, tt)
    dz_ref[:, D_SSM + D_POOL:D_SSM + D_POOL + D_SGU] = (d_c * mix_ref[...] * _gelu_grad(zu, su)).astype(BF16)
    dmix = d_c * u
    dmixb2 = dmix.astype(BF16)
    for c in range(tt // CHUNK):
        accm_ref[...] += dmix[c * CHUNK:(c + 1) * CHUNK, :]
    for chunks in _chunk_pairs(tt):
        for h in range(HEADS):
            accw_ref[h] += _nt(_head_cols(dmixb2, chunks, h), _head_cols(vnb, chunks, h))
    _head_mix(wsmt_ref, dmixb2, dvn_ref, tt)
    dvn = dvn_ref[...]
    o_dlng[...] += _rowsum(dvn * vhat)
    o_dlnb[...] += _rowsum(dvn)
    dvh = dvn * lng
    dv = rstd * (dvh - jnp.mean(dvh, axis=-1, keepdims=True) - vhat * jnp.mean(dvh * vhat, axis=-1, keepdims=True))
    dz_ref[:, D_SSM + D_POOL + D_SGU:D_IN] = (dv * _gelu_grad(zv, sv)).astype(BF16)

    @pl.when(i == n_tiles - 1)
    def _():
        tri = (lax.broadcasted_iota(jnp.int32, (CHUNK, CHUNK), 0) >= lax.broadcasted_iota(jnp.int32, (CHUNK, CHUNK), 1))
        for h in range(HEADS):
            o_dws[h] = jnp.where(tri, accw_ref[h], 0.0)
        lane = lax.broadcasted_iota(jnp.int32, (1, 128), 1)
        acc = jnp.zeros((CHUNK, 128), F32)
        for h in range(HEADS):
            sh = jnp.sum(accm_ref[:, h * HEAD_DIM:(h + 1) * HEAD_DIM], axis=1, keepdims=True)
            acc = jnp.where(lane == h, sh, acc)
        o_dbsp[...] = acc


def ffn_bwd(dx2, x1, gate, up, g, wg_t, wu_t, wd, name, exchange=None):
    T = x1.shape[0]
    tt = TT_FFN
    any_spec = pl.BlockSpec(memory_space=pl.ANY)

    def body(dx2_ref, x1_ref, gate_ref, up_ref, g_ref, wg_hbm, wu_hbm, wd_hbm,
             dgu_ref, dx2b_ref, dx1_ref, dx1b_ref, dg_ref, wg_v, wu_v, wd_v, sem):
        _load_weights([(wg_hbm, wg_v), (wu_hbm, wu_v), (wd_hbm, wd_v)], sem)

        @pl.when(pl.program_id(0) == 0)
        def _():
            dg_ref[...] = jnp.zeros_like(dg_ref)

        dx2 = dx2_ref[...]
        dx2b = dx2.astype(BF16)
        dx2b_ref[...] = dx2b
        dact = _nt(dx2b, wd_v[...])
        gate = gate_ref[...].astype(F32)
        up = up_ref[...].astype(F32)
        sg = jax.nn.sigmoid(gate)
        dgate = (dact * up * (sg * (1.0 + gate * (1.0 - sg)))).astype(BF16)
        dup = (dact * gate * sg).astype(BF16)
        dgu_ref[:, 0:D_FF] = dgate
        dgu_ref[:, D_FF:2 * D_FF] = dup
        dh = _nn(dgate, wg_v[...]) + _nn(dup, wu_v[...])
        xn, r = _rms(x1_ref[...])
        dg_ref[...] += _rowsum(dh * xn)
        dx1 = dx2 + _rms_bwd(dh, xn, r, g_ref[...])
        dx1_ref[...] = dx1
        dx1b_ref[...] = dx1.astype(BF16)

    return _pallas(
        body, name=name, grid=(T // tt,),
        in_specs=[_row(tt, D), _row(tt, D), _row(tt, D_FF), _row(tt, D_FF), _full((1, D)), any_spec, any_spec, any_spec],
        out_specs=[_row(tt, 2 * D_FF), _row(tt, D), _row(tt, D), _row(tt, D), _full((1, D))],
        out_shape=[S((T, 2 * D_FF), BF16), S((T, D), BF16), S((T, D), F32), S((T, D), BF16), S((1, D), F32)],
        scratch_shapes=[pltpu.VMEM((D_FF, D), BF16), pltpu.VMEM((D_FF, D), BF16), pltpu.VMEM((D_FF, D), BF16),
                        pltpu.SemaphoreType.DMA((3,))],
        vmem_mb=56, operands=(dx2, x1, gate, up, g, wg_t, wu_t, wd), exchange=exchange)


def wgrad(a, b, tm, name, exchange=None, tk=TK_WGRAD):
    T, M = a.shape
    N = b.shape[1]
    tk = min(tk, T)
    n_k = T // tk

    def body(a_ref, b_ref, o_ref, acc_ref):
        k = pl.program_id(1)

        @pl.when(k == 0)
        def _():
            acc_ref[...] = jnp.zeros_like(acc_ref)

        acc_ref[...] += _tn(a_ref[...], b_ref[...])

        @pl.when(k == n_k - 1)
        def _():
            o_ref[...] = acc_ref[...].astype(BF16)

    (out,), got = _pallas(
        body, name=name, grid=(M // tm, n_k),
        in_specs=[pl.BlockSpec((tk, tm), lambda m, k: (k, m)), pl.BlockSpec((tk, N), lambda m, k: (k, 0))],
        out_specs=[pl.BlockSpec((tm, N), lambda m, k: (m, 0))],
        out_shape=[S((M, N), BF16)],
        scratch_shapes=[pltpu.VMEM((tm, N), F32)],
        vmem_mb=48, operands=(a, b), exchange=exchange)
    return out if exchange is None else (out, got)


def _mesh_place():
    x, y, c = lax.axis_index("x"), lax.axis_index("y"), lax.axis_index("c")
    return x, y, c, 4 * x + 2 * y + c


def _peer(x, y, c, k):
    px = 1 - x if k & 4 else x
    py = 1 - y if k & 2 else y
    pc = 1 - c if k & 1 else c
    return (px, py, pc), 4 * px + 2 * py + pc


class _Exchange:
    SAME_CORE = (2, 4, 6)

    def __init__(self, gather=(), scatter=()):
        self.entries = [(a, None, a.shape[0]) for a in gather] + [(a, off, rows) for a, off, rows in scatter]
        self.n_gather = len(gather)

    @property
    def n(self):
        return len(self.entries)

    def operands(self):
        return [e[0] for e in self.entries]

    def out_shapes(self):
        return [S((N_DEV, rows, a.shape[1]), a.dtype) for a, _, rows in self.entries]

    def sems(self):
        return [pltpu.SemaphoreType.DMA((self.n, N_DEV)), pltpu.SemaphoreType.DMA((self.n, N_DEV)),
                pltpu.SemaphoreType.DMA((self.n,))]

    def _src(self, ref, e, idx):
        _, off, rows = self.entries[e]
        if off is None:
            return ref
        return ref.at[pl.ds(pl.multiple_of(off + idx * rows, 16), rows)]

    def _masks(self, e):
        return (1,) + self.SAME_CORE if e < self.n_gather else tuple(range(1, N_DEV))

    def _copy(self, ins, outs, sems, e, k, sending, passing_on=False):
        send_sems, recv_sems, _ = sems
        x, y, c, me = _mesh_place()
        peer, pidx = _peer(x, y, c, k)
        if passing_on:
            return pltpu.make_async_remote_copy(
                src_ref=outs[e].at[pidx], dst_ref=outs[e].at[pidx], send_sem=send_sems.at[e, k | 1],
                recv_sem=recv_sems.at[e, k | 1], device_id=_peer(x, y, c, 1)[0], device_id_type=pl.DeviceIdType.MESH)
        return pltpu.make_async_remote_copy(
            src_ref=self._src(ins[e], e, pidx), dst_ref=outs[e].at[me if sending else pidx], send_sem=send_sems.at[e, k],
            recv_sem=recv_sems.at[e, k], device_id=peer, device_id_type=pl.DeviceIdType.MESH)

    def _local(self, ins, outs, sems):
        me = _mesh_place()[3]
        return [pltpu.make_async_copy(self._src(ins[e], e, me), outs[e].at[me], sems[2].at[e]) for e in range(self.n)]

    def start(self, ins, outs, sems):
        for cp in self._local(ins, outs, sems):
            cp.start()
        for k in range(1, N_DEV):
            for e in range(self.n):
                if k in self._masks(e):
                    self._copy(ins, outs, sems, e, k, True).start()

    def forward(self, ins, outs, sems):
        for k in self.SAME_CORE:
            for e in range(self.n_gather):
                self._copy(ins, outs, sems, e, k, False).wait_recv()
                self._copy(ins, outs, sems, e, k, False, passing_on=True).start()

    def wait(self, ins, outs, sems):
        for k in range(1, N_DEV):
            for e in range(self.n):
                if e >= self.n_gather or k % 2:
                    self._copy(ins, outs, sems, e, k, False).wait_recv()
        for k in range(1, N_DEV):
            for e in range(self.n):
                self._copy(ins, outs, sems, e, k, True).wait_send()
        for cp in self._local(ins, outs, sems):
            cp.wait()


def _pallas(body, *, name, grid, in_specs, out_specs, out_shape, scratch_shapes, vmem_mb, operands, exchange=None,
            aliases=None):
    n_in, n_out, n_scr = len(in_specs), len(out_specs), len(scratch_shapes)
    n_steps = math.prod(grid)
    aliases = aliases or {}
    if exchange is None:
        res = pl.pallas_call(body, name=name, grid=grid, in_specs=in_specs, out_specs=out_specs, out_shape=out_shape,
                             scratch_shapes=scratch_shapes, input_output_aliases=aliases,
                             compiler_params=_cp(vmem_mb, len(grid)))(*operands)
        return list(res), []
    ex = exchange

    def hosted(*refs):
        ins, ex_in = refs[:n_in], refs[n_in:n_in + ex.n]
        outs = refs[n_in + ex.n:n_in + ex.n + n_out]
        ex_out = refs[n_in + ex.n + n_out:n_in + 2 * ex.n + n_out]
        scr = refs[n_in + 2 * ex.n + n_out:]
        sems = scr[n_scr:]
        step = pl.program_id(0)
        for axis in range(1, len(grid)):
            step = step * grid[axis] + pl.program_id(axis)

        @pl.when(step == 0)
        def _():
            ex.start(ex_in, ex_out, sems)

        body(*ins, *outs, *scr[:n_scr])

        if ex.n_gather:
            @pl.when(step == max(n_steps - 1 - max(2, n_steps // 8), 0))
            def _():
                ex.forward(ex_in, ex_out, sems)

        @pl.when(step == n_steps - 1)
        def _():
            ex.wait(ex_in, ex_out, sems)

    any_spec = pl.BlockSpec(memory_space=pl.ANY)
    res = pl.pallas_call(
        hosted, name=name, grid=grid, in_specs=list(in_specs) + [any_spec] * ex.n,
        out_specs=list(out_specs) + [any_spec] * ex.n, out_shape=list(out_shape) + ex.out_shapes(),
        scratch_shapes=list(scratch_shapes) + ex.sems(), input_output_aliases=aliases,
        compiler_params=_cp(vmem_mb, len(grid)),
    )(*operands, *ex.operands())
    return list(res[:n_out]), list(res[n_out:])


def exchange_only(ex, name):
    def body(*refs):
        ins, outs, sems = refs[:ex.n], refs[ex.n:2 * ex.n], refs[2 * ex.n:]
        ex.start(ins, outs, sems)
        ex.forward(ins, outs, sems)
        ex.wait(ins, outs, sems)

    any_spec = pl.BlockSpec(memory_space=pl.ANY)
    return list(pl.pallas_call(body, name=name, in_specs=[any_spec] * ex.n, out_specs=[any_spec] * ex.n,
                               out_shape=ex.out_shapes(), scratch_shapes=ex.sems())(*ex.operands()))


def _adamw(w, g, m, v):
    m = ADAM_B1 * m + (1.0 - ADAM_B1) * g
    v = ADAM_B2 * v + (1.0 - ADAM_B2) * (g * g)
    m_hat = m / (1.0 - ADAM_B1 ** ADAM_STEP)
    v_hat = v / (1.0 - ADAM_B2 ** ADAM_STEP)
    delta = -ADAM_LR * (m_hat / (jnp.sqrt(v_hat) + ADAM_EPS) + ADAM_WD * w)
    return delta, m, v


def _sum_parts(p_ref, rows=slice(None)):
    g = p_ref[0, rows].astype(F32)
    for k in range(1, N_DEV):
        g = g + p_ref[k, rows].astype(F32)
    return g


def adamw_layers(parts, w, m, v, name):
    n_l = len(parts)

    def body(*refs):
        p_refs = refs[:n_l]
        w_ref, m_ref, v_ref, g_out, d_out, m_out, v_out = refs[n_l:]
        for l in range(n_l):
            g = _sum_parts(p_refs[l])
            g_out[l] = g
            d_out[l], m_out[l], v_out[l] = _adamw(w_ref[l], g, m_ref[l], v_ref[l])

    return pl.pallas_call(
        body, name=name, out_shape=[S(w.shape, F32)] * 4, compiler_params=_cp(48),
    )(*parts, w, m, v)


def adamw_segments(parts, segments, w, m, v, name):
    n_p = len(parts)

    def body(*refs):
        p_refs = refs[:n_p]
        w_ref, m_ref, v_ref, g_out, d_out, m_out, v_out = refs[n_p:]
        for part, src, dst, rows in segments:
            g = _sum_parts(p_refs[part], slice(src, src + rows))
            to = slice(dst, dst + rows)
            g_out[to] = g
            d_out[to], m_out[to], v_out[to] = _adamw(w_ref[to], g, m_ref[to], v_ref[to])

    return pl.pallas_call(
        body, name=name, out_shape=[S(w.shape, F32)] * 4, compiler_params=_cp(48),
    )(*parts, w, m, v)


SMALL_LAYER = ("g_mix", "A_re", "A_im", "log_dt", "B_re", "B_im", "C_re", "C_im", "D_skip", "b_glu", "w_pool",
               "pool_scale", "sgu_ln_g", "sgu_ln_b", "w_spatial", "b_spatial", "g_ffn")
BIG_NAMES = ("w_in", "w_glu", "w_out", "w_gate", "w_up", "w_down")
COLUMN_SHARDED = ("w_in", "w_gate", "w_up")
WEIGHT_ORDER = ("g_mix", "w_in", "A_re", "A_im", "log_dt", "B_re", "B_im", "C_re", "C_im", "D_skip", "w_glu", "b_glu",
                "w_pool", "pool_scale", "sgu_ln_g", "sgu_ln_b", "w_spatial", "b_spatial", "w_out", "g_ffn", "w_gate",
                "w_up", "w_down", "g_final")
SEG = 1024


def _pack(arrays):
    parts = []
    for a in arrays:
        flat = a.reshape(-1)
        parts.append(jnp.pad(flat, (0, (-flat.shape[0]) % SEG)))
    return jnp.concatenate(parts).reshape(-1, 128)


def _state_rows(p):
    return p.reshape(1, D_ST)


def _chan_by_state(p):
    return jnp.transpose(p, (2, 0, 1)).reshape(GRP, D_ST)


def _chan_by_state_c(p):
    return jnp.transpose(p, (1, 0, 2)).reshape(GRP, D_ST)


def kernel(x, g_mix, w_in, A_re, A_im, log_dt, B_re, B_im, C_re, C_im, D_skip, w_glu, b_glu, w_pool, pool_scale, sgu_ln_g, sgu_ln_b, w_spatial, b_spatial, w_out, g_ffn, w_gate, w_up, w_down, g_final, loss_target, m_g_mix, m_w_in, m_A_re, m_A_im, m_log_dt, m_B_re, m_B_im, m_C_re, m_C_im, m_D_skip, m_w_glu, m_b_glu, m_w_pool, m_pool_scale, m_sgu_ln_g, m_sgu_ln_b, m_w_spatial, m_b_spatial, m_w_out, m_g_ffn, m_w_gate, m_w_up, m_w_down, m_g_final, v_g_mix, v_w_in, v_A_re, v_A_im, v_log_dt, v_B_re, v_B_im, v_C_re, v_C_im, v_D_skip, v_w_glu, v_b_glu, v_w_pool, v_pool_scale, v_sgu_ln_g, v_sgu_ln_b, v_w_spatial, v_b_spatial, v_w_out, v_g_ffn, v_w_gate, v_w_up, v_w_down, v_g_final):
    args = dict(locals())
    W = {n: args[n] for n in WEIGHT_ORDER}
    M = {n: args["m_" + n] for n in WEIGHT_ORDER}
    V = {n: args["v_" + n] for n in WEIGHT_ORDER}
    n_layers = g_mix.shape[0]
    x0 = x[0]
    target = loss_target[0]

    def my_rows(name, l):
        w = W[name][l]
        return (w.T if name in COLUMN_SHARDED else w).astype(BF16)

    full_w = [dict() for _ in range(n_layers)]

    def gather_of(*which):
        return _Exchange(gather=[my_rows(n, l) for n, l in which])

    def keep_gathered(which, arrays):
        for (n, l), a in zip(which, arrays):
            full_w[l][n] = a.reshape(-1, a.shape[-1])

    tri = jnp.tril(jnp.ones((CHUNK, CHUNK), bool))
    perm = _interleave_matrices(TT_MIX)
    consts = []
    for l in range(n_layers):
        a_re, a_im = _state_rows(A_re[l]), _state_rows(A_im[l])
        ldt = jnp.repeat(log_dt[l], N_STATE).reshape(1, D_ST)
        b_re_t, b_im_t = _chan_by_state(B_re[l]), _chan_by_state(B_im[l])
        sc, bbd, cbd = s5_prepare(a_re, a_im, ldt, b_re_t, b_im_t, _chan_by_state_c(C_re[l]), _chan_by_state_c(C_im[l]))
        wsm = jnp.where(tri[None], w_spatial[l], 0.0)
        wpool_bd = jnp.zeros((D_POOL, D_POOL), F32)
        for gi in range(len(POOL_WINDOWS)):
            wpool_bd = wpool_bd.at[gi * 64:(gi + 1) * 64, gi * 64:(gi + 1) * 64].set(w_pool[l, gi])
        consts.append(dict(
            disc=(a_re, a_im, ldt, b_re_t, b_im_t), sc=sc, bbd=bbd, cbd=cbd,
            dskip=D_skip[l].reshape(1, D_SSM), bglu=b_glu[l].reshape(1, D_SSM),
            wpool=wpool_bd.astype(BF16), pscale=pool_scale[l].reshape(1, D_POOL),
            lng=sgu_ln_g[l].reshape(1, D_SGU), lnb=sgu_ln_b[l].reshape(1, D_SGU),
            wsm=wsm.astype(BF16), wsmt=jnp.transpose(wsm, (0, 2, 1)).astype(BF16),
            bsp=jnp.repeat(b_spatial[l].T, HEAD_DIM, axis=1),
            gmix=g_mix[l].reshape(1, D), gffn=g_ffn[l].reshape(1, D)))

    def mixer_args(l):
        c = consts[l]
        return (c["bbd"], c["cbd"], c["dskip"], full_w[l]["w_glu"], c["bglu"], c["wpool"], c["pscale"], c["lng"], c["lnb"])

    carried_fwd = {
        ("norm", 0): [("w_in", 0)],
        ("inproj", 0): [("w_glu", 0), ("w_out", 0)],
        ("mixer", 0): [("w_gate", 0), ("w_up", 0), ("w_down", 0)],
        ("ffn", 0): [("w_in", 1), ("w_glu", 1), ("w_out", 1), ("w_gate", 1)],
        ("mixer", 1): [("w_up", 1), ("w_down", 1)],
    }

    def carried(kind, l):
        which = carried_fwd.get((kind, l))
        return which, (gather_of(*which) if which else None)

    saved = []
    xl = x0
    for l in range(n_layers):
        c, fw = consts[l], full_w[l]
        if l == 0:
            which, ex = carried("norm", l)
            (h,), got = norm_fwd(xl, c["gmix"], f"norm_fwd_{l}", ex)
            keep_gathered(which, got)
            which, ex = carried("inproj", l)
            (z,), got = proj_fwd(h, fw["w_in"], f"proj_fwd_{l}", ex)
        else:
            which, ex = carried("inproj", l)
            (h, z), got = inproj_fwd(xl, c["gmix"], fw["w_in"], f"inproj_fwd_{l}", ex)
        keep_gathered(which or [], got)
        which, ex = carried("mixer", l)
        (ycat, hs, ys), got = mixer_fwd(z, c["sc"], *mixer_args(l), c["wsm"], c["bsp"], perm, f"mixer_fwd_{l}", ex)
        keep_gathered(which or [], got)
        which, ex = carried("ffn", l)
        head = (target, g_final.reshape(1, D)) if l == n_layers - 1 else None
        res, got = ffn_fwd(xl, ycat, fw["w_out"], c["gffn"], fw["w_gate"], fw["w_up"], fw["w_down"], f"ffn_fwd_{l}", ex, head)
        keep_gathered(which or [], got)
        x1, h2, gate, up, act, x2 = res[:6]
        saved.append(dict(x=xl, h=h, z=z, ycat=ycat, hs=hs, ys=ys, x1=x1, h2=h2, gate=gate, up=up, act=act))
        xl = x2
    dx, loss_vec, d_gfinal = xl, res[6], res[7]

    recv_big = {}
    recv_small = [None] * n_layers

    def keep_received(which, arrays):
        for key, a in zip(which, arrays):
            if key[0] == "small":
                recv_small[key[1]] = a
            else:
                recv_big[key] = a

    pending = None
    for l in reversed(range(n_layers)):
        c, fw, sv = consts[l], full_w[l], saved[l]
        (dgu, dx2b, dx1, dx1b, d_gffn), got = ffn_bwd(dx, sv["x1"], sv["gate"], sv["up"], c["gffn"], fw["w_gate"], fw["w_up"],
                                                     fw["w_down"], f"ffn_bwd_{l}", pending[1] if pending else None)
        if pending:
            keep_received(pending[0], got)
        g_gu = wgrad(dgu, sv["h2"], D_FF // 2, f"wgrad_gate_up_{l}")
        g_down = wgrad(sv["act"], dx2b, D_FF // 2, f"wgrad_down_{l}")
        g_out = wgrad(sv["ycat"], dx1b, D, f"wgrad_out_{l}")
        ffn_rows = D_FF // N_DEV
        ex = _Exchange(scatter=[(g_gu, 0, ffn_rows), (g_gu, D_FF, ffn_rows), (g_down, 0, ffn_rows), (g_out, 0, D // N_DEV)])
        (dzb, da, dbt, dct, dd, dbglu, dwglu, dwpool, dpscale, dlng, dlnb, dws, dbsp), got = mixer_bwd(
            dx1b, sv["z"], sv["hs"], sv["ys"], fw["w_out"], c["sc"], *mixer_args(l), c["wsm"], c["wsmt"], c["bsp"],
            perm, f"mixer_bwd_{l}", ex)
        keep_received([("w_gate", l), ("w_up", l), ("w_down", l), ("w_out", l)], got)
        dx, d_gmix = inproj_bwd(dzb, sv["x"], c["gmix"], fw["w_in"], dx1)
        d_are, d_aim, d_ldt, d_bre_t, d_bim_t = s5_param_bwd(*c["disc"], da, dbt)
        small = dict(
            g_mix=d_gmix.reshape(D), A_re=d_are.reshape(N_GRP, N_STATE), A_im=d_aim.reshape(N_GRP, N_STATE),
            log_dt=d_ldt[0, :N_GRP],
            B_re=jnp.transpose(d_bre_t.reshape(GRP, N_GRP, N_STATE), (1, 2, 0)),
            B_im=jnp.transpose(d_bim_t.reshape(GRP, N_GRP, N_STATE), (1, 2, 0)),
            C_re=jnp.transpose(dct[:, :D_ST].reshape(GRP, N_GRP, N_STATE), (1, 0, 2)),
            C_im=-jnp.transpose(dct[:, D_ST:].reshape(GRP, N_GRP, N_STATE), (1, 0, 2)),
            D_skip=dd.reshape(D_SSM), b_glu=dbglu.reshape(D_SSM),
            w_pool=jnp.stack([dwpool[gi * 64:(gi + 1) * 64, gi * 64:(gi + 1) * 64] for gi in range(len(POOL_WINDOWS))]),
            pool_scale=dpscale.reshape(D_POOL), sgu_ln_g=dlng.reshape(D_SGU), sgu_ln_b=dlnb.reshape(D_SGU),
            w_spatial=dws, b_spatial=dbsp[:, :HEADS].T, g_ffn=d_gffn.reshape(D))
        packed = [small[n] for n in SMALL_LAYER] + ([d_gfinal.reshape(D), loss_vec.reshape(D)] if l == 0 else [])
        small_entries = dict(gather=[_pack(packed)], scatter=[(dwglu.astype(BF16), 0, D_SSM // N_DEV)])
        if l > 0:
            g_in = wgrad(dzb, sv["h"], D_IN, f"wgrad_in_{l}")
            small_entries["scatter"].append((g_in, 0, D_IN // N_DEV))
            pending = ([("small", l), ("w_glu", l), ("w_in", l)], _Exchange(**small_entries))
        else:
            g_in, got = wgrad(dzb, sv["h"], D_IN, f"wgrad_in_{l}", tk=TK_WGRAD // 4, exchange=_Exchange(**small_entries))
            keep_received([("small", l), ("w_glu", l)], got)
            pending = ([("w_in", l)], _Exchange(scatter=[(g_in, 0, D_IN // N_DEV)]))
    grad_x = dx
    keep_received(pending[0], exchange_only(pending[1], "exchange_last"))

    out = {}
    for n in BIG_NAMES:
        tr = (lambda a: jnp.transpose(a, (0, 2, 1))) if n in COLUMN_SHARDED else (lambda a: a)
        res = adamw_layers([recv_big[(n, l)] for l in range(n_layers)], tr(W[n]), tr(M[n]), tr(V[n]), f"adamw_{n}")
        out[n] = [tr(r) for r in res]

    seg_rows = [(-(-math.prod(W[n].shape[1:]) // SEG)) * (SEG // 128) for n in SMALL_LAYER]
    segments, src, dst = [], 0, 0
    for rows in seg_rows:
        segments += [(l, src, dst + l * rows, rows) for l in range(n_layers)]
        src += rows
        dst += n_layers * rows
    tile_rows = SEG // 128
    segments += [(0, src, dst, tile_rows), (0, src + tile_rows, dst + tile_rows, tile_rows)]

    def pack_params(P):
        parts = []
        for n, rows in zip(SMALL_LAYER, seg_rows):
            flat = P[n].reshape(n_layers, -1)
            parts.append(jnp.pad(flat, ((0, 0), (0, rows * 128 - flat.shape[1]))).reshape(-1))
        return jnp.concatenate(parts + [P["g_final"], jnp.zeros((SEG,), F32)]).reshape(-1, 128)

    res = adamw_segments(recv_small, segments, pack_params(W), pack_params(M), pack_params(V), "adamw_small")
    for j in range(4):
        flat, off = res[j].reshape(-1), 0
        for n, rows in zip(SMALL_LAYER, seg_rows):
            size = math.prod(W[n].shape[1:])
            piece = flat[off:off + n_layers * rows * 128].reshape(n_layers, rows * 128)[:, :size].reshape(W[n].shape)
            out.setdefault(n, []).append(piece)
            off += n_layers * rows * 128
        out.setdefault("g_final", []).append(flat[off:off + D])
        if j == 0:
            loss = (0.5 / D) * jnp.sum(flat[off + SEG:off + SEG + D])

    return (loss, grad_x[None], *[out[n][0] for n in WEIGHT_ORDER], *[out[n][1] for n in WEIGHT_ORDER],
            *[out[n][2] for n in WEIGHT_ORDER], *[out[n][3] for n in WEIGHT_ORDER])
```

```python
import functools
import math

import jax
import jax.numpy as jnp
from jax import lax
from jax.experimental import pallas as pl
from jax.experimental.pallas import tpu as pltpu

F32 = jnp.float32
BF16 = jnp.bfloat16
S = jax.ShapeDtypeStruct

N_DEV = 8
D = 1024
D_SSM = 384
N_GRP = 24
GRP = 16
N_STATE = 64
D_ST = N_GRP * N_STATE
D_POOL = 256
POOL_WINDOWS = (2, 4, 8, 16)
HALO = 16
D_SGU = 384
HEADS = 6
HEAD_DIM = 64
CHUNK = 128
D_IN = 1408
D_FF = 2816
EPS = 1e-6
SCAN_BLK = 8

ADAM_LR = 0.001
ADAM_B1 = 0.9
ADAM_B2 = 0.999
ADAM_EPS = 1e-08
ADAM_WD = 0.01
ADAM_STEP = 10

GELU_C0 = math.sqrt(2.0 / math.pi)
GELU_C1 = 0.044715

TT_MIX = 256
SEG_LEN = TT_MIX // SCAN_BLK
TT_FFN = 256
TT_PROJ = 512
TK_WGRAD = 2048
VMEM_MB = 2 ** 20


def _cp(vmem_mb, grid_dims=0):
    kw = dict(vmem_limit_bytes=int(vmem_mb * VMEM_MB))
    if grid_dims:
        kw["dimension_semantics"] = ("arbitrary",) * grid_dims
    return pltpu.CompilerParams(**kw)


def _row(tt, n):
    return pl.BlockSpec((tt, n), lambda i: (i, 0))


def _full(shape):
    nd = len(shape)
    return pl.BlockSpec(shape, lambda *_: (0,) * nd)


def _nn(a, b):
    return jnp.dot(a, b, preferred_element_type=F32)


def _nt(a, b):
    return lax.dot_general(a, b, (((1,), (1,)), ((), ())), preferred_element_type=F32)


def _tn(a, b):
    return lax.dot_general(a, b, (((0,), (0,)), ((), ())), preferred_element_type=F32)


def _rowsum(x):
    return jnp.sum(x, axis=0, keepdims=True)


def _rms(x):
    r = lax.rsqrt(jnp.mean(x * x, axis=-1, keepdims=True) + EPS)
    return x * r, r


def _rms_bwd(dy, xn, r, g):
    dyg = dy * g
    return r * (dyg - xn * jnp.mean(dyg * xn, axis=-1, keepdims=True))


def _gelu(x):
    s = jax.nn.sigmoid(x * (2.0 * GELU_C0 + (2.0 * GELU_C0 * GELU_C1) * (x * x)))
    return x * s, s


def _gelu_grad(x, s):
    return s * (1.0 + x * (1.0 - s) * (2.0 * GELU_C0 + (6.0 * GELU_C0 * GELU_C1) * (x * x)))


def _discretise(a_re, a_im, ldt, b_re, b_im):
    dt = jnp.exp(ldt)
    mag = jnp.exp(a_re * dt)
    ar = mag * jnp.cos(a_im * dt)
    ai = mag * jnp.sin(a_im * dt)
    den = a_re * a_re + a_im * a_im
    f_re = ((ar - 1.0) * a_re + ai * a_im) / den
    f_im = (ai * a_re - (ar - 1.0) * a_im) / den
    bb_re = f_re * b_re - f_im * b_im
    bb_im = f_re * b_im + f_im * b_re
    return ar, ai, bb_re, bb_im


def _group_mask(rows, cols):
    r = lax.broadcasted_iota(jnp.int32, (rows, cols), 0) // GRP
    c = lax.broadcasted_iota(jnp.int32, (rows, cols), 1)
    c = jnp.where(c >= D_ST, c - D_ST, c) // N_STATE
    return r == c


def s5_prepare(a_re, a_im, ldt, b_re_t, b_im_t, c_re_t, c_im_t):
    def body(are_ref, aim_ref, ldt_ref, bre_ref, bim_ref, cre_ref, cim_ref, sc_ref, bbd_ref, cbd_ref):
        ar, ai, bb_re, bb_im = _discretise(are_ref[...], aim_ref[...], ldt_ref[...], bre_ref[...], bim_ref[...])
        mask = _group_mask(D_SSM, 2 * D_ST)
        bb = jnp.concatenate([jnp.tile(bb_re, (N_GRP, 1)), jnp.tile(bb_im, (N_GRP, 1))], axis=1)
        bbd_ref[...] = jnp.where(mask, bb, 0.0).astype(BF16)
        cc = jnp.concatenate([jnp.tile(cre_ref[...], (N_GRP, 1)), -jnp.tile(cim_ref[...], (N_GRP, 1))], axis=1)
        cbd_ref[...] = jnp.where(mask, cc, 0.0).astype(BF16)
        pr, pi = ar, ai
        for _ in range(SEG_LEN - 1):
            pr, pi = pr * ar - pi * ai, pr * ai + pi * ar
        for k, v in enumerate((ar, ai, pr, pi)):
            sc_ref[8 * k:8 * k + 8, :] = jnp.broadcast_to(v, (SCAN_BLK, D_ST))

    return pl.pallas_call(
        body, name="s5_prepare",
        out_shape=[S((32, D_ST), F32), S((D_SSM, 2 * D_ST), BF16), S((D_SSM, 2 * D_ST), BF16)],
        compiler_params=_cp(40),
    )(a_re, a_im, ldt, b_re_t, b_im_t, c_re_t, c_im_t)


def s5_param_bwd(a_re, a_im, ldt, b_re_t, b_im_t, da, dbt):
    def body(are_ref, aim_ref, ldt_ref, bre_ref, bim_ref, da_ref, dbt_ref, o_are, o_aim, o_ldt, o_bre, o_bim):
        _, vjp = jax.vjp(_discretise, are_ref[...], aim_ref[...], ldt_ref[...], bre_ref[...], bim_ref[...])
        da = da_ref[...]
        dbt = dbt_ref[...]
        g_are, g_aim, g_ldt, g_bre, g_bim = vjp((da[:, :D_ST], da[:, D_ST:], dbt[:, :D_ST], dbt[:, D_ST:]))
        o_are[...] = g_are
        o_aim[...] = g_aim
        o_bre[...] = g_bre
        o_bim[...] = g_bim
        grp = lax.broadcasted_iota(jnp.int32, (1, D_ST), 1) // N_STATE
        lane = lax.broadcasted_iota(jnp.int32, (1, 128), 1)
        out = jnp.zeros((1, 128), F32)
        for g in range(N_GRP):
            out = jnp.where(lane == g, jnp.sum(jnp.where(grp == g, g_ldt, 0.0), axis=1, keepdims=True), out)
        o_ldt[...] = out

    return pl.pallas_call(
        body, name="s5_param_bwd",
        out_shape=[S((1, D_ST), F32), S((1, D_ST), F32), S((1, 128), F32), S((GRP, D_ST), F32), S((GRP, D_ST), F32)],
        compiler_params=_cp(16),
    )(a_re, a_im, ldt, b_re_t, b_im_t, da, dbt)


_CH = ((0, 256), (256, D_SSM))
_ST = ((0, 1024), (1024, D_ST))


def _bd_expand(xb, w_ref, out_ref):
    for (c0, c1), (s0, s1) in zip(_CH, _ST):
        for half in (0, D_ST):
            out_ref[:, half + s0:half + s1] = _nn(xb[:, c0:c1], w_ref[c0:c1, half + s0:half + s1])


def _bd_contract(hb, w_ref):
    parts = []
    for (c0, c1), (s0, s1) in zip(_CH, _ST):
        parts.append(_nt(hb[:, s0:s1], w_ref[c0:c1, s0:s1]) + _nt(hb[:, D_ST + s0:D_ST + s1], w_ref[c0:c1, D_ST + s0:D_ST + s1]))
    return jnp.concatenate(parts, axis=1)


def _bd_accumulate(acc_ref, xb, hb):
    for (c0, c1), (s0, s1) in zip(_CH, _ST):
        for half in (0, D_ST):
            acc_ref[c0:c1, half + s0:half + s1] += _tn(xb[:, c0:c1], hb[:, half + s0:half + s1])


def _interleave_matrices(tt):
    r = lax.broadcasted_iota(jnp.int32, (tt, tt), 0)
    t = lax.broadcasted_iota(jnp.int32, (tt, tt), 1)
    p = (t == (r % SCAN_BLK) * (tt // SCAN_BLK) + r // SCAN_BLK).astype(BF16)
    return p, p.T


def _interleave_f32(p, x):
    hi = x.astype(BF16)
    lo = (x - hi.astype(F32)).astype(BF16)
    return _nn(p, hi) + _nn(p, lo)


def _scan_tile(buf_ref, sc_ref, carry_ref, n_blk, reverse):
    ar = sc_ref[0:8, :]
    ai = -sc_ref[8:16, :] if reverse else sc_ref[8:16, :]

    def rows(i):
        blk = (n_blk - 1 - i) if reverse else i
        return pl.ds(pl.multiple_of(blk * SCAN_BLK, SCAN_BLK), SCAN_BLK)

    def local(i, x):
        xr, xi = x
        r = rows(i)
        xr, xi = buf_ref[r, 0:D_ST] + ar * xr - ai * xi, buf_ref[r, D_ST:2 * D_ST] + ar * xi + ai * xr
        buf_ref[r, 0:D_ST] = xr
        buf_ref[r, D_ST:2 * D_ST] = xi
        return xr, xi

    zero = jnp.zeros((SCAN_BLK, D_ST), F32)
    end_r, end_i = lax.fori_loop(0, n_blk, local, (zero, zero), unroll=True)

    seg_r = sc_ref[16:17, :]
    seg_i = -sc_ref[24:25, :] if reverse else sc_ref[24:25, :]
    cr, ci = carry_ref[0:1, 0:D_ST], carry_ref[0:1, D_ST:2 * D_ST]
    sub = lax.broadcasted_iota(jnp.int32, (SCAN_BLK, D_ST), 0)
    in_r, in_i = zero, zero
    for s in (reversed(range(SCAN_BLK)) if reverse else range(SCAN_BLK)):
        in_r = jnp.where(sub == s, cr, in_r)
        in_i = jnp.where(sub == s, ci, in_i)
        cr, ci = end_r[s:s + 1, :] + seg_r * cr - seg_i * ci, end_i[s:s + 1, :] + seg_r * ci + seg_i * cr
    carry_ref[0:1, 0:D_ST] = cr
    carry_ref[0:1, D_ST:2 * D_ST] = ci

    def fix(i, d):
        dr, di = d
        dr, di = ar * dr - ai * di, ar * di + ai * dr
        r = rows(i)
        buf_ref[r, 0:D_ST] += dr
        buf_ref[r, D_ST:2 * D_ST] += di
        return dr, di

    lax.fori_loop(0, n_blk, fix, (in_r, in_i), unroll=True)


def _lane_windows(n):
    lane = lax.broadcasted_iota(jnp.int32, (1, n), 1)
    return lane // (D_POOL // len(POOL_WINDOWS))


def _select_window(grp, s2, s4, s8, s16):
    return jnp.where(grp == 0, s2, jnp.where(grp == 1, s4, jnp.where(grp == 2, s8, s16)))


def _pool_fwd(pbuf_ref, zb, halo, tile_idx, tt):
    pbuf_ref[0:HALO, :] = halo
    pbuf_ref[HALO:HALO + tt, :] = zb
    x = pbuf_ref[...]
    s2 = x + pltpu.roll(x, 1, axis=0)
    s4 = s2 + pltpu.roll(s2, 2, axis=0)
    s8 = s4 + pltpu.roll(s4, 4, axis=0)
    s16 = s8 + pltpu.roll(s8, 8, axis=0)
    grp = _lane_windows(D_POOL)
    win = _select_window(grp, s2, s4, s8, s16)[HALO:HALO + tt, :]
    width = _select_window(grp, 2.0, 4.0, 8.0, 16.0).astype(F32)
    pos = (tile_idx * tt + 1 + lax.broadcasted_iota(jnp.int32, (tt, 1), 0)).astype(F32)
    cnt = jnp.minimum(pos, width)
    return win / cnt - zb, cnt


def _sgu_fwd(zu, zv, lng, lnb, wsm_ref, bsp, mix_ref, tt):
    u, tu = _gelu(zu)
    v, tv = _gelu(zv)
    mu = jnp.mean(v, axis=-1, keepdims=True)
    vc = v - mu
    rstd = lax.rsqrt(jnp.mean(vc * vc, axis=-1, keepdims=True) + EPS)
    vhat = vc * rstd
    vnb = (vhat * lng + lnb).astype(BF16)
    _head_mix(wsm_ref, vnb, mix_ref, tt, bsp)
    return u, tu, tv, vhat, rstd, vnb


def _chunk_pairs(tt):
    n_ch = tt // CHUNK
    return [list(range(c, min(c + 2, n_ch))) for c in range(0, n_ch, 2)]


def _head_cols(xb, chunks, h):
    return jnp.concatenate([xb[c * CHUNK:(c + 1) * CHUNK, h * HEAD_DIM:(h + 1) * HEAD_DIM] for c in chunks], axis=1)


def _head_mix(w_ref, xb, out_ref, tt, add=None):
    for chunks in _chunk_pairs(tt):
        per_head = [_nn(w_ref[h], _head_cols(xb, chunks, h)) for h in range(HEADS)]
        for k, c in enumerate(chunks):
            block = jnp.concatenate([r[:, k * HEAD_DIM:(k + 1) * HEAD_DIM] for r in per_head], axis=1)
            out_ref[c * CHUNK:(c + 1) * CHUNK, :] = block if add is None else block + add


def mixer_fwd(z, sc, bbd, cbd, dskip, wglu, bglu, wpool, pscale, lng, lnb, wsm, bsp, perm, name, exchange=None):
    T = z.shape[0]
    tt = TT_MIX
    n_tiles = T // tt

    def body(z_ref, sc_ref, bbd_ref, cbd_ref, dskip_ref, wglu_ref, bglu_ref, wpool_ref, pscale_ref, lng_ref, lnb_ref,
             wsm_ref, bsp_ref, p_ref, pt_ref, ycat_ref, hs_ref, ys_ref, carry_ref, halo_ref, pbuf_ref, mix_ref):
        i = pl.program_id(0)

        @pl.when(i == 0)
        def _():
            carry_ref[...] = jnp.zeros_like(carry_ref)
            halo_ref[...] = jnp.zeros_like(halo_ref)

        za = z_ref[:, 0:D_SSM]
        zb = z_ref[:, D_SSM:D_SSM + D_POOL]
        zu = z_ref[:, D_SSM + D_POOL:D_SSM + D_POOL + D_SGU]
        zv = z_ref[:, D_SSM + D_POOL + D_SGU:D_IN]
        p, pt = p_ref[...], pt_ref[...]
        za = _interleave_f32(p, za)
        _bd_expand(za.astype(BF16), bbd_ref, hs_ref)
        _scan_tile(hs_ref, sc_ref, carry_ref, tt // SCAN_BLK, reverse=False)
        y = _bd_contract(hs_ref[...].astype(BF16), cbd_ref) + dskip_ref[...] * za
        ys_ref[...] = y
        g, _ = _gelu(y)
        q = _nn(g.astype(BF16), wglu_ref[...]) + bglu_ref[...]
        ycat_ref[:, 0:D_SSM] = _nn(pt, (g * jax.nn.sigmoid(q)).astype(BF16)).astype(BF16)
        pooled, _ = _pool_fwd(pbuf_ref, zb, halo_ref[...], i, tt)
        halo_ref[...] = zb[tt - HALO:tt, :]
        ycat_ref[:, D_SSM:D_SSM + D_POOL] = (_nn(pooled.astype(BF16), wpool_ref[...]) * pscale_ref[...]).astype(BF16)
        u, _, _, _, _, _ = _sgu_fwd(zu, zv, lng_ref[...], lnb_ref[...], wsm_ref, bsp_ref[...], mix_ref, tt)
        ycat_ref[:, D_SSM + D_POOL:D] = (u * mix_ref[...]).astype(BF16)

    return _pallas(
        body, name=name, grid=(n_tiles,),
        in_specs=[_row(tt, D_IN), _full((32, D_ST)), _full((D_SSM, 2 * D_ST)), _full((D_SSM, 2 * D_ST)),
                  _full((1, D_SSM)), _full((D_SSM, D_SSM)), _full((1, D_SSM)), _full((D_POOL, D_POOL)),
                  _full((1, D_POOL)), _full((1, D_SGU)), _full((1, D_SGU)), _full((HEADS, CHUNK, CHUNK)),
                  _full((CHUNK, D_SGU)), _full((tt, tt)), _full((tt, tt))],
        out_specs=[_row(tt, D), _row(tt, 2 * D_ST), _row(tt, D_SSM)],
        out_shape=[S((T, D), BF16), S((T, 2 * D_ST), F32), S((T, D_SSM), F32)],
        scratch_shapes=[pltpu.VMEM((SCAN_BLK, 2 * D_ST), F32), pltpu.VMEM((HALO, D_POOL), F32),
                        pltpu.VMEM((tt + HALO, D_POOL), F32), pltpu.VMEM((tt, D_SGU), F32)],
        vmem_mb=48, operands=(z, sc, bbd, cbd, dskip, wglu, bglu, wpool, pscale, lng, lnb, wsm, bsp, *perm),
        exchange=exchange)


def mixer_bwd(dx1b, z, hs, ys, wout, sc, bbd, cbd, dskip, wglu, bglu, wpool, pscale, lng, lnb, wsm, wsmt, bsp, perm, name,
              exchange=None):
    T = z.shape[0]
    tt = TT_MIX
    n_tiles = T // tt

    def rev(i):
        return n_tiles - 1 - i

    def body(dx_ref, z_ref, zprev_ref, hs_ref, hsprev_ref, ys_ref, wout_ref, sc_ref, bbd_ref, cbd_ref, dskip_ref,
             wglu_ref, bglu_ref, wpool_ref, pscale_ref, lng_ref, lnb_ref, wsm_ref, wsmt_ref, bsp_ref, p_ref, pt_ref,
             dz_ref, o_da, o_dbt, o_dct, o_dd, o_dbglu, o_dwglu, o_dwpool, o_dpscale, o_dlng, o_dlnb, o_dws, o_dbsp,
             gbuf_ref, carry_ref, accb_ref, accc_ref, ehalo_ref, pbuf_ref, mix_ref, dvn_ref, accw_ref, accm_ref):
        i = pl.program_id(0)
        tile = rev(i)

        @pl.when(i == 0)
        def _():
            carry_ref[...] = jnp.zeros_like(carry_ref)
            accb_ref[...] = jnp.zeros_like(accb_ref)
            accc_ref[...] = jnp.zeros_like(accc_ref)
            for o in (o_da, o_dd, o_dbglu, o_dwglu):
                o[...] = jnp.zeros_like(o)

        p, pt = p_ref[...], pt_ref[...]
        dxb = dx_ref[...]
        d_a = _nt(_nn(p, dxb).astype(BF16), wout_ref[0:D_SSM, :])
        d_bc = _nt(dxb, wout_ref[D_SSM:D, :])
        za = _interleave_f32(p, z_ref[:, 0:D_SSM])
        first_tile = (tile > 0).astype(F32)

        y = ys_ref[...]
        g, tg = _gelu(y)
        gb = g.astype(BF16)
        sg = jax.nn.sigmoid(_nn(gb, wglu_ref[...]) + bglu_ref[...])
        dq = d_a * g * sg * (1.0 - sg)
        dqb = dq.astype(BF16)
        o_dbglu[...] += _rowsum(dq)
        o_dwglu[...] += _tn(gb, dqb)
        dy = (d_a * sg + _nt(dqb, wglu_ref[...])) * _gelu_grad(y, tg)
        o_dd[...] += _rowsum(dy * za)
        dyb = dy.astype(BF16)
        _bd_accumulate(accc_ref, dyb, hs_ref[...].astype(BF16))
        _bd_expand(dyb, cbd_ref, gbuf_ref)
        _scan_tile(gbuf_ref, sc_ref, carry_ref, tt // SCAN_BLK, reverse=True)
        hprev = hsprev_ref[SCAN_BLK - 1:SCAN_BLK, :] * first_tile
        sub = lax.broadcasted_iota(jnp.int32, (SCAN_BLK, 1), 0)
        edge = jnp.where(sub == 0, hprev, pltpu.roll(hs_ref[tt - SCAN_BLK:tt, :], 1, axis=0))

        def da_terms(gr, gi, hr, hi):
            return _rowsum(gr * hr + gi * hi), _rowsum(gi * hr - gr * hi)

        body_re, body_im = da_terms(gbuf_ref[SCAN_BLK:tt, 0:D_ST], gbuf_ref[SCAN_BLK:tt, D_ST:],
                                    hs_ref[0:tt - SCAN_BLK, 0:D_ST], hs_ref[0:tt - SCAN_BLK, D_ST:])
        edge_re, edge_im = da_terms(gbuf_ref[0:SCAN_BLK, 0:D_ST], gbuf_ref[0:SCAN_BLK, D_ST:], edge[:, 0:D_ST], edge[:, D_ST:])
        o_da[:, 0:D_ST] += body_re + edge_re
        o_da[:, D_ST:] += body_im + edge_im
        gtb = gbuf_ref[...].astype(BF16)
        _bd_accumulate(accb_ref, za.astype(BF16), gtb)
        dza = (dy * dskip_ref[...] + _bd_contract(gtb, bbd_ref)).astype(BF16)
        dz_ref[:, 0:D_SSM] = _nn(pt, dza).astype(BF16)
        _pool_sgu_bwd(i, n_tiles, tile, tt, d_bc, z_ref, zprev_ref, wpool_ref, pscale_ref, lng_ref, lnb_ref, wsm_ref,
                      wsmt_ref, bsp_ref, dz_ref, o_dwpool, o_dpscale, o_dlng, o_dlnb, o_dws, o_dbsp,
                      ehalo_ref, pbuf_ref, mix_ref, dvn_ref, accw_ref, accm_ref)

        @pl.when(i == n_tiles - 1)
        def _():
            mask = _group_mask(D_SSM, 2 * D_ST)
            for acc_ref, o_ref in ((accb_ref, o_dbt), (accc_ref, o_dct)):
                fold = jnp.zeros((GRP, 2 * D_ST), F32)
                for gidx in range(N_GRP):
                    rows = slice(gidx * GRP, (gidx + 1) * GRP)
                    fold = fold + jnp.where(mask[rows, :], acc_ref[rows, :], 0.0)
                o_ref[...] = fold

    def rowr(n):
        return pl.BlockSpec((tt, n), lambda i: (rev(i), 0))

    zprev_spec = pl.BlockSpec((HALO, D_IN), lambda i: (jnp.maximum(rev(i) * (tt // HALO) - 1, 0), 0))
    hsprev_spec = pl.BlockSpec((SCAN_BLK, 2 * D_ST), lambda i: (jnp.maximum(rev(i) * (tt // SCAN_BLK) - 1, 0), 0))
    small = [S((1, 2 * D_ST), F32), S((GRP, 2 * D_ST), F32), S((GRP, 2 * D_ST), F32), S((1, D_SSM), F32),
             S((1, D_SSM), F32), S((D_SSM, D_SSM), F32), S((D_POOL, D_POOL), F32), S((1, D_POOL), F32),
             S((1, D_SGU), F32), S((1, D_SGU), F32), S((HEADS, CHUNK, CHUNK), F32), S((CHUNK, 128), F32)]
    return _pallas(
        body, name=name, grid=(n_tiles,),
        in_specs=[rowr(D), rowr(D_IN), zprev_spec, rowr(2 * D_ST), hsprev_spec, rowr(D_SSM), _full((D, D)),
                  _full((32, D_ST)), _full((D_SSM, 2 * D_ST)), _full((D_SSM, 2 * D_ST)), _full((1, D_SSM)),
                  _full((D_SSM, D_SSM)), _full((1, D_SSM)), _full((D_POOL, D_POOL)), _full((1, D_POOL)),
                  _full((1, D_SGU)), _full((1, D_SGU)), _full((HEADS, CHUNK, CHUNK)), _full((HEADS, CHUNK, CHUNK)),
                  _full((CHUNK, D_SGU)), _full((tt, tt)), _full((tt, tt))],
        out_specs=[rowr(D_IN)] + [_full(s.shape) for s in small],
        out_shape=[S((T, D_IN), BF16)] + small,
        scratch_shapes=[pltpu.VMEM((tt, 2 * D_ST), F32), pltpu.VMEM((SCAN_BLK, 2 * D_ST), F32),
                        pltpu.VMEM((D_SSM, 2 * D_ST), F32), pltpu.VMEM((D_SSM, 2 * D_ST), F32),
                        pltpu.VMEM((HALO, D_POOL), F32), pltpu.VMEM((tt + HALO, D_POOL), F32),
                        pltpu.VMEM((tt, D_SGU), F32), pltpu.VMEM((tt, D_SGU), F32),
                        pltpu.VMEM((HEADS, CHUNK, CHUNK), F32), pltpu.VMEM((CHUNK, D_SGU), F32)],
        vmem_mb=56, exchange=exchange,
        operands=(dx1b, z, z, hs, hs, ys, wout, sc, bbd, cbd, dskip, wglu, bglu, wpool, pscale, lng, lnb, wsm, wsmt, bsp, *perm))


def inproj_fwd(x, g, w_t, name, exchange=None):
    T = x.shape[0]
    tt = TT_PROJ

    def body(x_ref, g_ref, w_ref, h_ref, z_ref):
        xn, _ = _rms(x_ref[...])
        h = (xn * g_ref[...]).astype(BF16)
        h_ref[...] = h
        z_ref[...] = _nt(h, w_ref[...])

    return _pallas(
        body, name=name, grid=(T // tt,),
        in_specs=[_row(tt, D), _full((1, D)), _full((D_IN, D))],
        out_specs=[_row(tt, D), _row(tt, D_IN)],
        out_shape=[S((T, D), BF16), S((T, D_IN), F32)],
        scratch_shapes=[], vmem_mb=40, operands=(x, g, w_t), exchange=exchange)


def inproj_bwd(dzb, x, g, w_t, dx1):
    T = x.shape[0]
    tt = TT_PROJ

    def body(dz_ref, x_ref, g_ref, w_ref, dx1_ref, dx_ref, dg_ref):
        @pl.when(pl.program_id(0) == 0)
        def _():
            dg_ref[...] = jnp.zeros_like(dg_ref)

        dh = _nn(dz_ref[...], w_ref[...])
        xn, r = _rms(x_ref[...])
        dg_ref[...] += _rowsum(dh * xn)
        dx_ref[...] = dx1_ref[...] + _rms_bwd(dh, xn, r, g_ref[...])

    return pl.pallas_call(
        body, name="inproj_bwd", grid=(T // tt,),
        in_specs=[_row(tt, D_IN), _row(tt, D), _full((1, D)), _full((D_IN, D)), _row(tt, D)],
        out_specs=[_row(tt, D), _full((1, D))],
        out_shape=[S((T, D), F32), S((1, D), F32)],
        compiler_params=_cp(40, 1),
    )(dzb, x, g, w_t, dx1)


def _load_weights(pairs, sem):
    @pl.when(pl.program_id(0) == 0)
    def _():
        copies = [pltpu.make_async_copy(src, dst, sem.at[k]) for k, (src, dst) in enumerate(pairs)]
        for cp in copies:
            cp.start()
        for cp in copies:
            cp.wait()


def ffn_fwd(x, ycat, wout, g, wg_t, wu_t, wd, name, exchange=None, head=None):
    T = x.shape[0]
    tt = TT_FFN
    any_spec = pl.BlockSpec(memory_space=pl.ANY)

    def body(*refs):
        if head is None:
            (x_ref, ycat_ref, g_ref, wout_hbm, wg_hbm, wu_hbm, wd_hbm,
             x1_ref, h_ref, gate_ref, up_ref, act_ref, x2_ref, wout_v, wg_v, wu_v, wd_v, sem) = refs
        else:
            (x_ref, ycat_ref, g_ref, t_ref, gf_ref, wout_hbm, wg_hbm, wu_hbm, wd_hbm,
             x1_ref, h_ref, gate_ref, up_ref, act_ref, x2_ref, lvec_ref, dgf_ref, wout_v, wg_v, wu_v, wd_v, sem) = refs
        _ffn_fwd_tile(x_ref, ycat_ref, g_ref, wout_hbm, wg_hbm, wu_hbm, wd_hbm, x1_ref, h_ref, gate_ref, up_ref, act_ref,
                      x2_ref, wout_v, wg_v, wu_v, wd_v, sem)
        if head is not None:
            @pl.when(pl.program_id(0) == 0)
            def _():
                lvec_ref[...] = jnp.zeros_like(lvec_ref)
                dgf_ref[...] = jnp.zeros_like(dgf_ref)

            xn, r = _rms(x2_ref[...])
            gf = gf_ref[...]
            err = xn * gf - t_ref[...]
            lvec_ref[...] += _rowsum(err * err)
            dy = err * (1.0 / D)
            dgf_ref[...] += _rowsum(dy * xn)
            x2_ref[...] = _rms_bwd(dy, xn, r, gf)

    def _ffn_fwd_tile(x_ref, ycat_ref, g_ref, wout_hbm, wg_hbm, wu_hbm, wd_hbm,
                      x1_ref, h_ref, gate_ref, up_ref, act_ref, x2_ref, wout_v, wg_v, wu_v, wd_v, sem):
        _load_weights([(wout_hbm, wout_v), (wg_hbm, wg_v), (wu_hbm, wu_v), (wd_hbm, wd_v)], sem)
        x1 = x_ref[...] + _nn(ycat_ref[...], wout_v[...])
        x1_ref[...] = x1
        xn, _ = _rms(x1)
        h = (xn * g_ref[...]).astype(BF16)
        h_ref[...] = h
        gate = _nt(h, wg_v[...])
        up = _nt(h, wu_v[...])
        gate_ref[...] = gate.astype(BF16)
        up_ref[...] = up.astype(BF16)
        act = (gate * jax.nn.sigmoid(gate) * up).astype(BF16)
        act_ref[...] = act
        x2_ref[...] = x1 + _nn(act, wd_v[...])

    with_head = head is not None
    return _pallas(
        body, name=name, grid=(T // tt,),
        in_specs=[_row(tt, D), _row(tt, D), _full((1, D))] + ([_row(tt, D), _full((1, D))] if with_head else [])
        + [any_spec, any_spec, any_spec, any_spec],
        out_specs=[_row(tt, D), _row(tt, D), _row(tt, D_FF), _row(tt, D_FF), _row(tt, D_FF), _row(tt, D)]
        + ([_full((1, D)), _full((1, D))] if with_head else []),
        out_shape=[S((T, D), F32), S((T, D), BF16), S((T, D_FF), BF16), S((T, D_FF), BF16), S((T, D_FF), BF16),
                   S((T, D), F32)] + ([S((1, D), F32), S((1, D), F32)] if with_head else []),
        scratch_shapes=[pltpu.VMEM((D, D), BF16), pltpu.VMEM((D_FF, D), BF16), pltpu.VMEM((D_FF, D), BF16),
                        pltpu.VMEM((D_FF, D), BF16), pltpu.SemaphoreType.DMA((4,))],
        vmem_mb=56, operands=(x, ycat, g) + (tuple(head) if with_head else ()) + (wout, wg_t, wu_t, wd), exchange=exchange)


def _pool_sgu_bwd(i, n_tiles, tile, tt, d_bc, z_ref, zprev_ref, wpool_ref, pscale_ref, lng_ref, lnb_ref, wsm_ref, wsmt_ref,
                  bsp_ref, dz_ref, o_dwpool, o_dpscale, o_dlng, o_dlnb, o_dws, o_dbsp,
                  ehalo_ref, pbuf_ref, mix_ref, dvn_ref, accw_ref, accm_ref):
    @pl.when(i == 0)
    def _():
        ehalo_ref[...] = jnp.zeros_like(ehalo_ref)
        accw_ref[...] = jnp.zeros_like(accw_ref)
        accm_ref[...] = jnp.zeros_like(accm_ref)
        for o in (o_dwpool, o_dpscale, o_dlng, o_dlnb):
            o[...] = jnp.zeros_like(o)

    d_b = d_bc[:, 0:D_POOL]
    d_c = d_bc[:, D_POOL:D_POOL + D_SGU]
    zb = z_ref[:, D_SSM:D_SSM + D_POOL]
    zu = z_ref[:, D_SSM + D_POOL:D_SSM + D_POOL + D_SGU]
    zv = z_ref[:, D_SSM + D_POOL + D_SGU:D_IN]
    not_first = (tile > 0).astype(F32)

    pooled, cnt = _pool_fwd(pbuf_ref, zb, zprev_ref[:, D_SSM:D_SSM + D_POOL] * not_first, tile, tt)
    pooledb = pooled.astype(BF16)
    mixed = _nn(pooledb, wpool_ref[...])
    o_dpscale[...] += _rowsum(d_b * mixed)
    dmixb = (d_b * pscale_ref[...]).astype(BF16)
    o_dwpool[...] += _tn(pooledb, dmixb)
    dpooled = _nt(dmixb, wpool_ref[...])
    e = dpooled / cnt
    pbuf_ref[0:tt, :] = e
    pbuf_ref[tt:tt + HALO, :] = ehalo_ref[...]
    ehalo_ref[...] = e[0:HALO, :]
    x = pbuf_ref[...]
    n = tt + HALO
    f2 = x + pltpu.roll(x, n - 1, axis=0)
    f4 = f2 + pltpu.roll(f2, n - 2, axis=0)
    f8 = f4 + pltpu.roll(f4, n - 4, axis=0)
    f16 = f8 + pltpu.roll(f8, n - 8, axis=0)
    fwd_sum = _select_window(_lane_windows(D_POOL), f2, f4, f8, f16)[0:tt, :]
    dz_ref[:, D_SSM:D_SSM + D_POOL] = (fwd_sum - dpooled).astype(BF16)

    lng = lng_ref[...]
    u, su, sv, vhat, rstd, vnb = _sgu_fwd(zu, zv, lng, lnb_ref[...], wsm_ref, bsp_ref[...], mix_ref, tt)
    dz_ref[:, D_SSM + D_POOL:D_SSM + D_POOL + D_SGU] = (d_c * mix_ref[...] * _gelu_grad(zu, su)).astype(BF16)
    dmix = d_c * u
    dmixb2 = dmix.astype(BF16)
    for c in range(tt // CHUNK):
        accm_ref[...] += dmix[c * CHUNK:(c + 1) * CHUNK, :]
    for chunks in _chunk_pairs(tt):
        for h in range(HEADS):
            accw_ref[h] += _nt(_head_cols(dmixb2, chunks, h), _head_cols(vnb, chunks, h))
    _head_mix(wsmt_ref, dmixb2, dvn_ref, tt)
    dvn = dvn_ref[...]
    o_dlng[...] += _rowsum(dvn * vhat)
    o_dlnb[...] += _rowsum(dvn)
    dvh = dvn * lng
    dv = rstd * (dvh - jnp.mean(dvh, axis=-1, keepdims=True) - vhat * jnp.mean(dvh * vhat, axis=-1, keepdims=True))
    dz_ref[:, D_SSM + D_POOL + D_SGU:D_IN] = (dv * _gelu_grad(zv, sv)).astype(BF16)

    @pl.when(i == n_tiles - 1)
    def _():
        tri = (lax.broadcasted_iota(jnp.int32, (CHUNK, CHUNK), 0) >= lax.broadcasted_iota(jnp.int32, (CHUNK, CHUNK), 1))
        for h in range(HEADS):
            o_dws[h] = jnp.where(tri, accw_ref[h], 0.0)
        lane = lax.broadcasted_iota(jnp.int32, (1, 128), 1)
        acc = jnp.zeros((CHUNK, 128), F32)
        for h in range(HEADS):
            sh = jnp.sum(accm_ref[:, h * HEAD_DIM:(h + 1) * HEAD_DIM], axis=1, keepdims=True)
            acc = jnp.where(lane == h, sh, acc)
        o_dbsp[...] = acc


def ffn_bwd(dx2, x1, gate, up, g, wg_t, wu_t, wd, name, exchange=None):
    T = x1.shape[0]
    tt = TT_FFN
    any_spec = pl.BlockSpec(memory_space=pl.ANY)

    def body(dx2_ref, x1_ref, gate_ref, up_ref, g_ref, wg_hbm, wu_hbm, wd_hbm,
             dgu_ref, dx2b_ref, dx1_ref, dx1b_ref, dg_ref, wg_v, wu_v, wd_v, sem):
        _load_weights([(wg_hbm, wg_v), (wu_hbm, wu_v), (wd_hbm, wd_v)], sem)

        @pl.when(pl.program_id(0) == 0)
        def _():
            dg_ref[...] = jnp.zeros_like(dg_ref)

        dx2 = dx2_ref[...]
        dx2b = dx2.astype(BF16)
        dx2b_ref[...] = dx2b
        dact = _nt(dx2b, wd_v[...])
        gate = gate_ref[...].astype(F32)
        up = up_ref[...].astype(F32)
        sg = jax.nn.sigmoid(gate)
        dgate = (dact * up * (sg * (1.0 + gate * (1.0 - sg)))).astype(BF16)
        dup = (dact * gate * sg).astype(BF16)
        dgu_ref[:, 0:D_FF] = dgate
        dgu_ref[:, D_FF:2 * D_FF] = dup
        dh = _nn(dgate, wg_v[...]) + _nn(dup, wu_v[...])
        xn, r = _rms(x1_ref[...])
        dg_ref[...] += _rowsum(dh * xn)
        dx1 = dx2 + _rms_bwd(dh, xn, r, g_ref[...])
        dx1_ref[...] = dx1
        dx1b_ref[...] = dx1.astype(BF16)

    return _pallas(
        body, name=name, grid=(T // tt,),
        in_specs=[_row(tt, D), _row(tt, D), _row(tt, D_FF), _row(tt, D_FF), _full((1, D)), any_spec, any_spec, any_spec],
        out_specs=[_row(tt, 2 * D_FF), _row(tt, D), _row(tt, D), _row(tt, D), _full((1, D))],
        out_shape=[S((T, 2 * D_FF), BF16), S((T, D), BF16), S((T, D), F32), S((T, D), BF16), S((1, D), F32)],
        scratch_shapes=[pltpu.VMEM((D_FF, D), BF16), pltpu.VMEM((D_FF, D), BF16), pltpu.VMEM((D_FF, D), BF16),
                        pltpu.SemaphoreType.DMA((3,))],
        vmem_mb=56, operands=(dx2, x1, gate, up, g, wg_t, wu_t, wd), exchange=exchange)


def wgrad(a, b, tm, name, exchange=None, tk=TK_WGRAD):
    T, M = a.shape
    N = b.shape[1]
    tk = min(tk, T)
    n_k = T // tk

    def body(a_ref, b_ref, o_ref, acc_ref):
        k = pl.program_id(1)

        @pl.when(k == 0)
        def _():
            acc_ref[...] = jnp.zeros_like(acc_ref)

        acc_ref[...] += _tn(a_ref[...], b_ref[...])

        @pl.when(k == n_k - 1)
        def _():
            o_ref[...] = acc_ref[...].astype(BF16)

    (out,), got = _pallas(
        body, name=name, grid=(M // tm, n_k),
        in_specs=[pl.BlockSpec((tk, tm), lambda m, k: (k, m)), pl.BlockSpec((tk, N), lambda m, k: (k, 0))],
        out_specs=[pl.BlockSpec((tm, N), lambda m, k: (m, 0))],
        out_shape=[S((M, N), BF16)],
        scratch_shapes=[pltpu.VMEM((tm, N), F32)],
        vmem_mb=48, operands=(a, b), exchange=exchange)
    return out if exchange is None else (out, got)


def _mesh_place():
    x, y, c = lax.axis_index("x"), lax.axis_index("y"), lax.axis_index("c")
    return x, y, c, 4 * x + 2 * y + c


def _peer(x, y, c, k):
    px = 1 - x if k & 4 else x
    py = 1 - y if k & 2 else y
    pc = 1 - c if k & 1 else c
    return (px, py, pc), 4 * px + 2 * py + pc


class _Exchange:
    SAME_CORE = (2, 4, 6)

    def __init__(self, gather=(), scatter=()):
        self.entries = [(a, None, a.shape[0]) for a in gather] + [(a, off, rows) for a, off, rows in scatter]
        self.n_gather = len(gather)

    @property
    def n(self):
        return len(self.entries)

    def operands(self):
        return [e[0] for e in self.entries]

    def out_shapes(self):
        return [S((N_DEV, rows, a.shape[1]), a.dtype) for a, _, rows in self.entries]

    def sems(self):
        return [pltpu.SemaphoreType.DMA((self.n, N_DEV)), pltpu.SemaphoreType.DMA((self.n, N_DEV)),
                pltpu.SemaphoreType.DMA((self.n,))]

    def _src(self, ref, e, idx):
        _, off, rows = self.entries[e]
        if off is None:
            return ref
        return ref.at[pl.ds(pl.multiple_of(off + idx * rows, 16), rows)]

    def _masks(self, e):
        return (1,) + self.SAME_CORE if e < self.n_gather else tuple(range(1, N_DEV))

    def _copy(self, ins, outs, sems, e, k, sending, passing_on=False):
        send_sems, recv_sems, _ = sems
        x, y, c, me = _mesh_place()
        peer, pidx = _peer(x, y, c, k)
        if passing_on:
            return pltpu.make_async_remote_copy(
                src_ref=outs[e].at[pidx], dst_ref=outs[e].at[pidx], send_sem=send_sems.at[e, k | 1],
                recv_sem=recv_sems.at[e, k | 1], device_id=_peer(x, y, c, 1)[0], device_id_type=pl.DeviceIdType.MESH)
        return pltpu.make_async_remote_copy(
            src_ref=self._src(ins[e], e, pidx), dst_ref=outs[e].at[me if sending else pidx], send_sem=send_sems.at[e, k],
            recv_sem=recv_sems.at[e, k], device_id=peer, device_id_type=pl.DeviceIdType.MESH)

    def _local(self, ins, outs, sems):
        me = _mesh_place()[3]
        return [pltpu.make_async_copy(self._src(ins[e], e, me), outs[e].at[me], sems[2].at[e]) for e in range(self.n)]

    def start(self, ins, outs, sems):
        for cp in self._local(ins, outs, sems):
            cp.start()
        for k in range(1, N_DEV):
            for e in range(self.n):
                if k in self._masks(e):
                    self._copy(ins, outs, sems, e, k, True).start()

    def forward(self, ins, outs, sems):
        for k in self.SAME_CORE:
            for e in range(self.n_gather):
                self._copy(ins, outs, sems, e, k, False).wait_recv()
                self._copy(ins, outs, sems, e, k, False, passing_on=True).start()

    def wait(self, ins, outs, sems):
        for k in range(1, N_DEV):
            for e in range(self.n):
                if e >= self.n_gather or k % 2:
                    self._copy(ins, outs, sems, e, k, False).wait_recv()
        for k in range(1, N_DEV):
            for e in range(self.n):
                self._copy(ins, outs, sems, e, k, True).wait_send()
        for cp in self._local(ins, outs, sems):
            cp.wait()


def _pallas(body, *, name, grid, in_specs, out_specs, out_shape, scratch_shapes, vmem_mb, operands, exchange=None,
            aliases=None):
    n_in, n_out, n_scr = len(in_specs), len(out_specs), len(scratch_shapes)
    n_steps = math.prod(grid)
    aliases = aliases or {}
    if exchange is None:
        res = pl.pallas_call(body, name=name, grid=grid, in_specs=in_specs, out_specs=out_specs, out_shape=out_shape,
                             scratch_shapes=scratch_shapes, input_output_aliases=aliases,
                             compiler_params=_cp(vmem_mb, len(grid)))(*operands)
        return list(res), []
    ex = exchange

    def hosted(*refs):
        ins, ex_in = refs[:n_in], refs[n_in:n_in + ex.n]
        outs = refs[n_in + ex.n:n_in + ex.n + n_out]
        ex_out = refs[n_in + ex.n + n_out:n_in + 2 * ex.n + n_out]
        scr = refs[n_in + 2 * ex.n + n_out:]
        sems = scr[n_scr:]
        step = pl.program_id(0)
        for axis in range(1, len(grid)):
            step = step * grid[axis] + pl.program_id(axis)

        @pl.when(step == 0)
        def _():
            ex.start(ex_in, ex_out, sems)

        body(*ins, *outs, *scr[:n_scr])

        if ex.n_gather:
            @pl.when(step == max(n_steps - 1 - max(2, n_steps // 8), 0))
            def _():
                ex.forward(ex_in, ex_out, sems)

        @pl.when(step == n_steps - 1)
        def _():
            ex.wait(ex_in, ex_out, sems)

    any_spec = pl.BlockSpec(memory_space=pl.ANY)
    res = pl.pallas_call(
        hosted, name=name, grid=grid, in_specs=list(in_specs) + [any_spec] * ex.n,
        out_specs=list(out_specs) + [any_spec] * ex.n, out_shape=list(out_shape) + ex.out_shapes(),
        scratch_shapes=list(scratch_shapes) + ex.sems(), input_output_aliases=aliases,
        compiler_params=_cp(vmem_mb, len(grid)),
    )(*operands, *ex.operands())
    return list(res[:n_out]), list(res[n_out:])


def exchange_only(ex, name):
    def body(*refs):
        ins, outs, sems = refs[:ex.n], refs[ex.n:2 * ex.n], refs[2 * ex.n:]
        ex.start(ins, outs, sems)
        ex.forward(ins, outs, sems)
        ex.wait(ins, outs, sems)

    any_spec = pl.BlockSpec(memory_space=pl.ANY)
    return list(pl.pallas_call(body, name=name, in_specs=[any_spec] * ex.n, out_specs=[any_spec] * ex.n,
                               out_shape=ex.out_shapes(), scratch_shapes=ex.sems())(*ex.operands()))


def _adamw(w, g, m, v):
    m = ADAM_B1 * m + (1.0 - ADAM_B1) * g
    v = ADAM_B2 * v + (1.0 - ADAM_B2) * (g * g)
    m_hat = m / (1.0 - ADAM_B1 ** ADAM_STEP)
    v_hat = v / (1.0 - ADAM_B2 ** ADAM_STEP)
    delta = -ADAM_LR * (m_hat / (jnp.sqrt(v_hat) + ADAM_EPS) + ADAM_WD * w)
    return delta, m, v


def _sum_parts(p_ref, rows=slice(None)):
    g = p_ref[0, rows].astype(F32)
    for k in range(1, N_DEV):
        g = g + p_ref[k, rows].astype(F32)
    return g


def adamw_layers(parts, w, m, v, name):
    n_l = len(parts)

    def body(*refs):
        p_refs = refs[:n_l]
        w_ref, m_ref, v_ref, g_out, d_out, m_out, v_out = refs[n_l:]
        for l in range(n_l):
            g = _sum_parts(p_refs[l])
            g_out[l] = g
            d_out[l], m_out[l], v_out[l] = _adamw(w_ref[l], g, m_ref[l], v_ref[l])

    return pl.pallas_call(
        body, name=name, out_shape=[S(w.shape, F32)] * 4, compiler_params=_cp(48),
    )(*parts, w, m, v)


def adamw_segments(parts, segments, w, m, v, name):
    n_p = len(parts)

    def body(*refs):
        p_refs = refs[:n_p]
        w_ref, m_ref, v_ref, g_out, d_out, m_out, v_out = refs[n_p:]
        for part, src, dst, rows in segments:
            g = _sum_parts(p_refs[part], slice(src, src + rows))
            to = slice(dst, dst + rows)
            g_out[to] = g
            d_out[to], m_out[to], v_out[to] = _adamw(w_ref[to], g, m_ref[to], v_ref[to])

    return pl.pallas_call(
        body, name=name, out_shape=[S(w.shape, F32)] * 4, compiler_params=_cp(48),
    )(*parts, w, m, v)


SMALL_LAYER = ("g_mix", "A_re", "A_im", "log_dt", "B_re", "B_im", "C_re", "C_im", "D_skip", "b_glu", "w_pool",
               "pool_scale", "sgu_ln_g", "sgu_ln_b", "w_spatial", "b_spatial", "g_ffn")
BIG_NAMES = ("w_in", "w_glu", "w_out", "w_gate", "w_up", "w_down")
COLUMN_SHARDED = ("w_in", "w_gate", "w_up")
WEIGHT_ORDER = ("g_mix", "w_in", "A_re", "A_im", "log_dt", "B_re", "B_im", "C_re", "C_im", "D_skip", "w_glu", "b_glu",
                "w_pool", "pool_scale", "sgu_ln_g", "sgu_ln_b", "w_spatial", "b_spatial", "w_out", "g_ffn", "w_gate",
                "w_up", "w_down", "g_final")
SEG = 1024
SMALL_AS_BF16 = ("B_re", "B_im", "C_re", "C_im", "w_pool", "w_spatial")


def _pack(arrays, dtype=F32):
    seg = SEG * 4 // jnp.dtype(dtype).itemsize
    parts = []
    for a in arrays:
        flat = a.reshape(-1).astype(dtype)
        parts.append(jnp.pad(flat, (0, (-flat.shape[0]) % seg)))
    return jnp.concatenate(parts).reshape(-1, 128)


def _state_rows(p):
    return p.reshape(1, D_ST)


def _chan_by_state(p):
    return jnp.transpose(p, (2, 0, 1)).reshape(GRP, D_ST)


def _chan_by_state_c(p):
    return jnp.transpose(p, (1, 0, 2)).reshape(GRP, D_ST)


def kernel(x, g_mix, w_in, A_re, A_im, log_dt, B_re, B_im, C_re, C_im, D_skip, w_glu, b_glu, w_pool, pool_scale, sgu_ln_g, sgu_ln_b, w_spatial, b_spatial, w_out, g_ffn, w_gate, w_up, w_down, g_final, loss_target, m_g_mix, m_w_in, m_A_re, m_A_im, m_log_dt, m_B_re, m_B_im, m_C_re, m_C_im, m_D_skip, m_w_glu, m_b_glu, m_w_pool, m_pool_scale, m_sgu_ln_g, m_sgu_ln_b, m_w_spatial, m_b_spatial, m_w_out, m_g_ffn, m_w_gate, m_w_up, m_w_down, m_g_final, v_g_mix, v_w_in, v_A_re, v_A_im, v_log_dt, v_B_re, v_B_im, v_C_re, v_C_im, v_D_skip, v_w_glu, v_b_glu, v_w_pool, v_pool_scale, v_sgu_ln_g, v_sgu_ln_b, v_w_spatial, v_b_spatial, v_w_out, v_g_ffn, v_w_gate, v_w_up, v_w_down, v_g_final):
    args = dict(locals())
    W = {n: args[n] for n in WEIGHT_ORDER}
    M = {n: args["m_" + n] for n in WEIGHT_ORDER}
    V = {n: args["v_" + n] for n in WEIGHT_ORDER}
    n_layers = g_mix.shape[0]
    x0 = x[0]
    target = loss_target[0]

    def my_rows(name, l):
        w = W[name][l]
        return (w.T if name in COLUMN_SHARDED else w).astype(BF16)

    full_w = [dict() for _ in range(n_layers)]

    def gather_of(*which):
        return _Exchange(gather=[my_rows(n, l) for n, l in which])

    def keep_gathered(which, arrays):
        for (n, l), a in zip(which, arrays):
            full_w[l][n] = a.reshape(-1, a.shape[-1])

    tri = jnp.tril(jnp.ones((CHUNK, CHUNK), bool))
    perm = _interleave_matrices(TT_MIX)
    consts = []
    for l in range(n_layers):
        a_re, a_im = _state_rows(A_re[l]), _state_rows(A_im[l])
        ldt = jnp.repeat(log_dt[l], N_STATE).reshape(1, D_ST)
        b_re_t, b_im_t = _chan_by_state(B_re[l]), _chan_by_state(B_im[l])
        sc, bbd, cbd = s5_prepare(a_re, a_im, ldt, b_re_t, b_im_t, _chan_by_state_c(C_re[l]), _chan_by_state_c(C_im[l]))
        wsm = jnp.where(tri[None], w_spatial[l], 0.0)
        wpool_bd = jnp.zeros((D_POOL, D_POOL), F32)
        for gi in range(len(POOL_WINDOWS)):
            wpool_bd = wpool_bd.at[gi * 64:(gi + 1) * 64, gi * 64:(gi + 1) * 64].set(w_pool[l, gi])
        consts.append(dict(
            disc=(a_re, a_im, ldt, b_re_t, b_im_t), sc=sc, bbd=bbd, cbd=cbd,
            dskip=D_skip[l].reshape(1, D_SSM), bglu=b_glu[l].reshape(1, D_SSM),
            wpool=wpool_bd.astype(BF16), pscale=pool_scale[l].reshape(1, D_POOL),
            lng=sgu_ln_g[l].reshape(1, D_SGU), lnb=sgu_ln_b[l].reshape(1, D_SGU),
            wsm=wsm.astype(BF16), wsmt=jnp.transpose(wsm, (0, 2, 1)).astype(BF16),
            bsp=jnp.repeat(b_spatial[l].T, HEAD_DIM, axis=1),
            gmix=g_mix[l].reshape(1, D), gffn=g_ffn[l].reshape(1, D)))

    def mixer_args(l):
        c = consts[l]
        return (c["bbd"], c["cbd"], c["dskip"], full_w[l]["w_glu"], c["bglu"], c["wpool"], c["pscale"], c["lng"], c["lnb"])

    first_needed = [("w_in", 0)]
    keep_gathered(first_needed, exchange_only(gather_of(*first_needed), "gather_first"))
    carried_fwd = {
        ("inproj", 0): [("w_glu", 0), ("w_out", 0)],
        ("mixer", 0): [("w_gate", 0), ("w_up", 0), ("w_down", 0)],
        ("ffn", 0): [("w_in", 1), ("w_glu", 1), ("w_out", 1), ("w_gate", 1)],
        ("mixer", 1): [("w_up", 1), ("w_down", 1)],
    }

    def carried(kind, l):
        which = carried_fwd.get((kind, l))
        return which, (gather_of(*which) if which else None)

    saved = []
    xl = x0
    for l in range(n_layers):
        c, fw = consts[l], full_w[l]
        which, ex = carried("inproj", l)
        (h, z), got = inproj_fwd(xl, c["gmix"], fw["w_in"], f"inproj_fwd_{l}", ex)
        keep_gathered(which or [], got)
        which, ex = carried("mixer", l)
        (ycat, hs, ys), got = mixer_fwd(z, c["sc"], *mixer_args(l), c["wsm"], c["bsp"], perm, f"mixer_fwd_{l}", ex)
        keep_gathered(which or [], got)
        which, ex = carried("ffn", l)
        head = (target, g_final.reshape(1, D)) if l == n_layers - 1 else None
        res, got = ffn_fwd(xl, ycat, fw["w_out"], c["gffn"], fw["w_gate"], fw["w_up"], fw["w_down"], f"ffn_fwd_{l}", ex, head)
        keep_gathered(which or [], got)
        x1, h2, gate, up, act, x2 = res[:6]
        saved.append(dict(x=xl, h=h, z=z, ycat=ycat, hs=hs, ys=ys, x1=x1, h2=h2, gate=gate, up=up, act=act))
        xl = x2
    dx, loss_vec, d_gfinal = xl, res[6], res[7]

    recv_big = {}
    recv_small = [None] * (2 * n_layers)

    def keep_received(which, arrays):
        for key, a in zip(which, arrays):
            if key[0] == "small":
                recv_small[key[1]] = a
            elif key[0] == "small16":
                recv_small[n_layers + key[1]] = a
            else:
                recv_big[key] = a

    pending = None
    for l in reversed(range(n_layers)):
        c, fw, sv = consts[l], full_w[l], saved[l]
        (dgu, dx2b, dx1, dx1b, d_gffn), got = ffn_bwd(dx, sv["x1"], sv["gate"], sv["up"], c["gffn"], fw["w_gate"], fw["w_up"],
                                                     fw["w_down"], f"ffn_bwd_{l}", pending[1] if pending else None)
        if pending:
            keep_received(pending[0], got)
        g_gu = wgrad(dgu, sv["h2"], D_FF // 2, f"wgrad_gate_up_{l}")
        g_down = wgrad(sv["act"], dx2b, D_FF // 2, f"wgrad_down_{l}")
        g_out = wgrad(sv["ycat"], dx1b, D, f"wgrad_out_{l}")
        ffn_rows = D_FF // N_DEV
        ex = _Exchange(scatter=[(g_gu, 0, ffn_rows), (g_gu, D_FF, ffn_rows), (g_down, 0, ffn_rows), (g_out, 0, D // N_DEV)])
        (dzb, da, dbt, dct, dd, dbglu, dwglu, dwpool, dpscale, dlng, dlnb, dws, dbsp), got = mixer_bwd(
            dx1b, sv["z"], sv["hs"], sv["ys"], fw["w_out"], c["sc"], *mixer_args(l), c["wsm"], c["wsmt"], c["bsp"],
            perm, f"mixer_bwd_{l}", ex)
        keep_received([("w_gate", l), ("w_up", l), ("w_down", l), ("w_out", l)], got)
        dx, d_gmix = inproj_bwd(dzb, sv["x"], c["gmix"], fw["w_in"], dx1)
        d_are, d_aim, d_ldt, d_bre_t, d_bim_t = s5_param_bwd(*c["disc"], da, dbt)
        small = dict(
            g_mix=d_gmix.reshape(D), A_re=d_are.reshape(N_GRP, N_STATE), A_im=d_aim.reshape(N_GRP, N_STATE),
            log_dt=d_ldt[0, :N_GRP],
            B_re=jnp.transpose(d_bre_t.reshape(GRP, N_GRP, N_STATE), (1, 0, 2)),
            B_im=jnp.transpose(d_bim_t.reshape(GRP, N_GRP, N_STATE), (1, 0, 2)),
            C_re=jnp.transpose(dct[:, :D_ST].reshape(GRP, N_GRP, N_STATE), (1, 0, 2)),
            C_im=-jnp.transpose(dct[:, D_ST:].reshape(GRP, N_GRP, N_STATE), (1, 0, 2)),
            D_skip=dd.reshape(D_SSM), b_glu=dbglu.reshape(D_SSM),
            w_pool=jnp.stack([dwpool[gi * 64:(gi + 1) * 64, gi * 64:(gi + 1) * 64] for gi in range(len(POOL_WINDOWS))]),
            pool_scale=dpscale.reshape(D_POOL), sgu_ln_g=dlng.reshape(D_SGU), sgu_ln_b=dlnb.reshape(D_SGU),
            w_spatial=dws, b_spatial=dbsp[:, :HEADS].T, g_ffn=d_gffn.reshape(D))
        packed = [small[n] for n in SMALL_LAYER if n not in SMALL_AS_BF16]
        packed += [d_gfinal.reshape(D), loss_vec.reshape(D)] if l == 0 else []
        packed16 = [small[n] for n in SMALL_LAYER if n in SMALL_AS_BF16]
        small_entries = dict(gather=[_pack(packed), _pack(packed16, BF16)], scatter=[(dwglu.astype(BF16), 0, D_SSM // N_DEV)])
        if l > 0:
            g_in = wgrad(dzb, sv["h"], D_IN, f"wgrad_in_{l}")
            small_entries["scatter"].append((g_in, 0, D_IN // N_DEV))
            pending = ([("small", l), ("small16", l), ("w_glu", l), ("w_in", l)], _Exchange(**small_entries))
        else:
            g_in, got = wgrad(dzb, sv["h"], D_IN, f"wgrad_in_{l}", tk=TK_WGRAD // 4, exchange=_Exchange(**small_entries))
            keep_received([("small", l), ("small16", l), ("w_glu", l)], got)
            pending = ([("w_in", l)], _Exchange(scatter=[(g_in, 0, D_IN // N_DEV)]))
    grad_x = dx
    keep_received(pending[0], exchange_only(pending[1], "exchange_last"))

    out = {}
    for n in BIG_NAMES:
        tr = (lambda a: jnp.transpose(a, (0, 2, 1))) if n in COLUMN_SHARDED else (lambda a: a)
        res = adamw_layers([recv_big[(n, l)] for l in range(n_layers)], tr(W[n]), tr(M[n]), tr(V[n]), f"adamw_{n}")
        out[n] = [tr(r) for r in res]

    seg_rows = [(-(-math.prod(W[n].shape[1:]) // SEG)) * (SEG // 128) for n in SMALL_LAYER]
    segments, src, src16, dst = [], 0, 0, 0
    for n, rows in zip(SMALL_LAYER, seg_rows):
        if n in SMALL_AS_BF16:
            segments += [(n_layers + l, src16, dst + l * rows, rows) for l in range(n_layers)]
            src16 += rows
        else:
            segments += [(l, src, dst + l * rows, rows) for l in range(n_layers)]
            src += rows
        dst += n_layers * rows
    tile_rows = SEG // 128
    segments += [(0, src, dst, tile_rows), (0, src + tile_rows, dst + tile_rows, tile_rows)]

    channel_major = ("B_re", "B_im")

    def pack_params(P):
        parts = []
        for n, rows in zip(SMALL_LAYER, seg_rows):
            flat = (jnp.swapaxes(P[n], -1, -2) if n in channel_major else P[n]).reshape(n_layers, -1)
            parts.append(jnp.pad(flat, ((0, 0), (0, rows * 128 - flat.shape[1]))).reshape(-1))
        return jnp.concatenate(parts + [P["g_final"], jnp.zeros((SEG,), F32)]).reshape(-1, 128)

    res = adamw_segments(recv_small, segments, pack_params(W), pack_params(M), pack_params(V), "adamw_small")
    for j in range(4):
        flat, off = res[j].reshape(-1), 0
        for n, rows in zip(SMALL_LAYER, seg_rows):
            size = math.prod(W[n].shape[1:])
            piece = flat[off:off + n_layers * rows * 128].reshape(n_layers, rows * 128)[:, :size]
            if n in channel_major:
                piece = jnp.swapaxes(piece.reshape(W[n].shape[:-2] + W[n].shape[:-3:-1]), -1, -2)
            else:
                piece = piece.reshape(W[n].shape)
            out.setdefault(n, []).append(piece)
            off += n_layers * rows * 128
        out.setdefault("g_final", []).append(flat[off:off + D])
        if j == 0:
            loss = (0.5 / D) * jnp.sum(flat[off + SEG:off + SEG + D])

    return (loss, grad_x[None], *[out[n][0] for n in WEIGHT_ORDER], *[out[n][1] for n in WEIGHT_ORDER],
            *[out[n][2] for n in WEIGHT_ORDER], *[out[n][3] for n in WEIGHT_ORDER])
```

```python
import functools
import math

import jax
import jax.numpy as jnp
from jax import lax
from jax.experimental import pallas as pl
from jax.experimental.pallas import tpu as pltpu

F32 = jnp.float32
BF16 = jnp.bfloat16
S = jax.ShapeDtypeStruct

N_DEV = 8
D = 1024
D_SSM = 384
N_GRP = 24
GRP = 16
N_STATE = 64
D_ST = N_GRP * N_STATE
D_POOL = 256
POOL_WINDOWS = (2, 4, 8, 16)
HALO = 16
D_SGU = 384
HEADS = 6
HEAD_DIM = 64
CHUNK = 128
D_IN = 1408
D_FF = 2816
EPS = 1e-6
SCAN_BLK = 8

ADAM_LR = 0.001
ADAM_B1 = 0.9
ADAM_B2 = 0.999
ADAM_EPS = 1e-08
ADAM_WD = 0.01
ADAM_STEP = 10

GELU_C0 = math.sqrt(2.0 / math.pi)
GELU_C1 = 0.044715

TT_MIX = 256
SEG_LEN = TT_MIX // SCAN_BLK
TT_FFN = 256
TT_PROJ = 512
TK_WGRAD = 2048
VMEM_MB = 2 ** 20


def _cp(vmem_mb, grid_dims=0):
    kw = dict(vmem_limit_bytes=int(vmem_mb * VMEM_MB))
    if grid_dims:
        kw["dimension_semantics"] = ("arbitrary",) * grid_dims
    return pltpu.CompilerParams(**kw)


def _row(tt, n):
    return pl.BlockSpec((tt, n), lambda i: (i, 0))


def _full(shape):
    nd = len(shape)
    return pl.BlockSpec(shape, lambda *_: (0,) * nd)


def _nn(a, b):
    return jnp.dot(a, b, preferred_element_type=F32)


def _nt(a, b):
    return lax.dot_general(a, b, (((1,), (1,)), ((), ())), preferred_element_type=F32)


def _tn(a, b):
    return lax.dot_general(a, b, (((0,), (0,)), ((), ())), preferred_element_type=F32)


def _rowsum(x):
    return jnp.sum(x, axis=0, keepdims=True)


def _rms(x):
    r = lax.rsqrt(jnp.mean(x * x, axis=-1, keepdims=True) + EPS)
    return x * r, r


def _rms_bwd(dy, xn, r, g):
    dyg = dy * g
    return r * (dyg - xn * jnp.mean(dyg * xn, axis=-1, keepdims=True))


def _gelu(x):
    s = jax.nn.sigmoid(x * (2.0 * GELU_C0 + (2.0 * GELU_C0 * GELU_C1) * (x * x)))
    return x * s, s


def _gelu_grad(x, s):
    return s * (1.0 + x * (1.0 - s) * (2.0 * GELU_C0 + (6.0 * GELU_C0 * GELU_C1) * (x * x)))


def _discretise(a_re, a_im, ldt, b_re, b_im):
    dt = jnp.exp(ldt)
    mag = jnp.exp(a_re * dt)
    ar = mag * jnp.cos(a_im * dt)
    ai = mag * jnp.sin(a_im * dt)
    den = a_re * a_re + a_im * a_im
    f_re = ((ar - 1.0) * a_re + ai * a_im) / den
    f_im = (ai * a_re - (ar - 1.0) * a_im) / den
    bb_re = f_re * b_re - f_im * b_im
    bb_im = f_re * b_im + f_im * b_re
    return ar, ai, bb_re, bb_im


def _group_mask(rows, cols):
    r = lax.broadcasted_iota(jnp.int32, (rows, cols), 0) // GRP
    c = lax.broadcasted_iota(jnp.int32, (rows, cols), 1)
    c = jnp.where(c >= D_ST, c - D_ST, c) // N_STATE
    return r == c


def s5_prepare(a_re, a_im, ldt, b_re_t, b_im_t, c_re_t, c_im_t):
    def body(are_ref, aim_ref, ldt_ref, bre_ref, bim_ref, cre_ref, cim_ref, sc_ref, bbd_ref, cbd_ref):
        ar, ai, bb_re, bb_im = _discretise(are_ref[...], aim_ref[...], ldt_ref[...], bre_ref[...], bim_ref[...])
        mask = _group_mask(D_SSM, 2 * D_ST)
        bb = jnp.concatenate([jnp.tile(bb_re, (N_GRP, 1)), jnp.tile(bb_im, (N_GRP, 1))], axis=1)
        bbd_ref[...] = jnp.where(mask, bb, 0.0).astype(BF16)
        cc = jnp.concatenate([jnp.tile(cre_ref[...], (N_GRP, 1)), -jnp.tile(cim_ref[...], (N_GRP, 1))], axis=1)
        cbd_ref[...] = jnp.where(mask, cc, 0.0).astype(BF16)
        pr, pi = ar, ai
        for _ in range(SEG_LEN - 1):
            pr, pi = pr * ar - pi * ai, pr * ai + pi * ar
        for k, v in enumerate((ar, ai, pr, pi)):
            sc_ref[8 * k:8 * k + 8, :] = jnp.broadcast_to(v, (SCAN_BLK, D_ST))

    return pl.pallas_call(
        body, name="s5_prepare",
        out_shape=[S((32, D_ST), F32), S((D_SSM, 2 * D_ST), BF16), S((D_SSM, 2 * D_ST), BF16)],
        compiler_params=_cp(40),
    )(a_re, a_im, ldt, b_re_t, b_im_t, c_re_t, c_im_t)


def s5_param_bwd(a_re, a_im, ldt, b_re_t, b_im_t, da, dbt):
    def body(are_ref, aim_ref, ldt_ref, bre_ref, bim_ref, da_ref, dbt_ref, o_are, o_aim, o_ldt, o_bre, o_bim):
        _, vjp = jax.vjp(_discretise, are_ref[...], aim_ref[...], ldt_ref[...], bre_ref[...], bim_ref[...])
        da = da_ref[...]
        dbt = dbt_ref[...]
        g_are, g_aim, g_ldt, g_bre, g_bim = vjp((da[:, :D_ST], da[:, D_ST:], dbt[:, :D_ST], dbt[:, D_ST:]))
        o_are[...] = g_are
        o_aim[...] = g_aim
        o_bre[...] = g_bre
        o_bim[...] = g_bim
        grp = lax.broadcasted_iota(jnp.int32, (1, D_ST), 1) // N_STATE
        lane = lax.broadcasted_iota(jnp.int32, (1, 128), 1)
        out = jnp.zeros((1, 128), F32)
        for g in range(N_GRP):
            out = jnp.where(lane == g, jnp.sum(jnp.where(grp == g, g_ldt, 0.0), axis=1, keepdims=True), out)
        o_ldt[...] = out

    return pl.pallas_call(
        body, name="s5_param_bwd",
        out_shape=[S((1, D_ST), F32), S((1, D_ST), F32), S((1, 128), F32), S((GRP, D_ST), F32), S((GRP, D_ST), F32)],
        compiler_params=_cp(16),
    )(a_re, a_im, ldt, b_re_t, b_im_t, da, dbt)


_CH = ((0, 256), (256, D_SSM))
_ST = ((0, 1024), (1024, D_ST))


def _bd_expand(xb, w_ref, out_ref):
    for (c0, c1), (s0, s1) in zip(_CH, _ST):
        for half in (0, D_ST):
            out_ref[:, half + s0:half + s1] = _nn(xb[:, c0:c1], w_ref[c0:c1, half + s0:half + s1])


def _bd_contract(hb, w_ref):
    parts = []
    for (c0, c1), (s0, s1) in zip(_CH, _ST):
        parts.append(_nt(hb[:, s0:s1], w_ref[c0:c1, s0:s1]) + _nt(hb[:, D_ST + s0:D_ST + s1], w_ref[c0:c1, D_ST + s0:D_ST + s1]))
    return jnp.concatenate(parts, axis=1)


def _bd_accumulate(acc_ref, xb, hb):
    for j in range(N_GRP // 4):
        ch = slice(4 * GRP * j, 4 * GRP * (j + 1))
        for half in (0, D_ST):
            st = slice(half + 4 * N_STATE * j, half + 4 * N_STATE * (j + 1))
            acc_ref[ch, st] += _tn(xb[:, ch], hb[:, st])


def _interleave_matrices(tt):
    r = lax.broadcasted_iota(jnp.int32, (tt, tt), 0)
    t = lax.broadcasted_iota(jnp.int32, (tt, tt), 1)
    p = (t == (r % SCAN_BLK) * (tt // SCAN_BLK) + r // SCAN_BLK).astype(BF16)
    return p, p.T


def _interleave_f32(p, x):
    hi = x.astype(BF16)
    lo = (x - hi.astype(F32)).astype(BF16)
    return _nn(p, hi) + _nn(p, lo)


def _scan_tile(buf_ref, sc_ref, carry_ref, n_blk, reverse):
    ar = sc_ref[0:8, :]
    ai = -sc_ref[8:16, :] if reverse else sc_ref[8:16, :]

    def rows(i):
        blk = (n_blk - 1 - i) if reverse else i
        return pl.ds(pl.multiple_of(blk * SCAN_BLK, SCAN_BLK), SCAN_BLK)

    def local(i, x):
        xr, xi = x
        r = rows(i)
        xr, xi = buf_ref[r, 0:D_ST] + ar * xr - ai * xi, buf_ref[r, D_ST:2 * D_ST] + ar * xi + ai * xr
        buf_ref[r, 0:D_ST] = xr
        buf_ref[r, D_ST:2 * D_ST] = xi
        return xr, xi

    zero = jnp.zeros((SCAN_BLK, D_ST), F32)
    end_r, end_i = lax.fori_loop(0, n_blk, local, (zero, zero), unroll=True)

    seg_r = sc_ref[16:17, :]
    seg_i = -sc_ref[24:25, :] if reverse else sc_ref[24:25, :]
    cr, ci = carry_ref[0:1, 0:D_ST], carry_ref[0:1, D_ST:2 * D_ST]
    sub = lax.broadcasted_iota(jnp.int32, (SCAN_BLK, D_ST), 0)
    in_r, in_i = zero, zero
    for s in (reversed(range(SCAN_BLK)) if reverse else range(SCAN_BLK)):
        in_r = jnp.where(sub == s, cr, in_r)
        in_i = jnp.where(sub == s, ci, in_i)
        cr, ci = end_r[s:s + 1, :] + seg_r * cr - seg_i * ci, end_i[s:s + 1, :] + seg_r * ci + seg_i * cr
    carry_ref[0:1, 0:D_ST] = cr
    carry_ref[0:1, D_ST:2 * D_ST] = ci

    def fix(i, d):
        dr, di = d
        dr, di = ar * dr - ai * di, ar * di + ai * dr
        r = rows(i)
        buf_ref[r, 0:D_ST] += dr
        buf_ref[r, D_ST:2 * D_ST] += di
        return dr, di

    lax.fori_loop(0, n_blk, fix, (in_r, in_i), unroll=True)


def _lane_windows(n):
    lane = lax.broadcasted_iota(jnp.int32, (1, n), 1)
    return lane // (D_POOL // len(POOL_WINDOWS))


def _select_window(grp, s2, s4, s8, s16):
    return jnp.where(grp == 0, s2, jnp.where(grp == 1, s4, jnp.where(grp == 2, s8, s16)))


def _pool_fwd(pbuf_ref, zb, halo, tile_idx, tt):
    pbuf_ref[0:HALO, :] = halo
    pbuf_ref[HALO:HALO + tt, :] = zb
    x = pbuf_ref[...]
    s2 = x + pltpu.roll(x, 1, axis=0)
    s4 = s2 + pltpu.roll(s2, 2, axis=0)
    s8 = s4 + pltpu.roll(s4, 4, axis=0)
    s16 = s8 + pltpu.roll(s8, 8, axis=0)
    grp = _lane_windows(D_POOL)
    win = _select_window(grp, s2, s4, s8, s16)[HALO:HALO + tt, :]
    width = _select_window(grp, 2.0, 4.0, 8.0, 16.0).astype(F32)
    pos = (tile_idx * tt + 1 + lax.broadcasted_iota(jnp.int32, (tt, 1), 0)).astype(F32)
    cnt = jnp.minimum(pos, width)
    return win / cnt - zb, cnt


def _sgu_fwd(zu, zv, lng, lnb, wsm_ref, bsp, mix_ref, tt):
    u, tu = _gelu(zu)
    v, tv = _gelu(zv)
    mu = jnp.mean(v, axis=-1, keepdims=True)
    vc = v - mu
    rstd = lax.rsqrt(jnp.mean(vc * vc, axis=-1, keepdims=True) + EPS)
    vhat = vc * rstd
    vnb = (vhat * lng + lnb).astype(BF16)
    _head_mix(wsm_ref, vnb, mix_ref, tt, bsp)
    return u, tu, tv, vhat, rstd, vnb


def _chunk_pairs(tt):
    n_ch = tt // CHUNK
    return [list(range(c, min(c + 2, n_ch))) for c in range(0, n_ch, 2)]


def _head_cols(xb, chunks, h):
    return jnp.concatenate([xb[c * CHUNK:(c + 1) * CHUNK, h * HEAD_DIM:(h + 1) * HEAD_DIM] for c in chunks], axis=1)


def _head_mix(w_ref, xb, out_ref, tt, add=None):
    for chunks in _chunk_pairs(tt):
        per_head = [_nn(w_ref[h], _head_cols(xb, chunks, h)) for h in range(HEADS)]
        for k, c in enumerate(chunks):
            block = jnp.concatenate([r[:, k * HEAD_DIM:(k + 1) * HEAD_DIM] for r in per_head], axis=1)
            out_ref[c * CHUNK:(c + 1) * CHUNK, :] = block if add is None else block + add


def mixer_fwd(z, sc, bbd, cbd, dskip, wglu, bglu, wpool, pscale, lng, lnb, wsm, bsp, perm, name, exchange=None):
    T = z.shape[0]
    tt = TT_MIX
    n_tiles = T // tt

    def body(z_ref, sc_ref, bbd_ref, cbd_ref, dskip_ref, wglu_ref, bglu_ref, wpool_ref, pscale_ref, lng_ref, lnb_ref,
             wsm_ref, bsp_ref, p_ref, pt_ref, ycat_ref, hs_ref, ys_ref, carry_ref, halo_ref, pbuf_ref, mix_ref):
        i = pl.program_id(0)

        @pl.when(i == 0)
        def _():
            carry_ref[...] = jnp.zeros_like(carry_ref)
            halo_ref[...] = jnp.zeros_like(halo_ref)

        za = z_ref[:, 0:D_SSM]
        zb = z_ref[:, D_SSM:D_SSM + D_POOL]
        zu = z_ref[:, D_SSM + D_POOL:D_SSM + D_POOL + D_SGU]
        zv = z_ref[:, D_SSM + D_POOL + D_SGU:D_IN]
        p, pt = p_ref[...], pt_ref[...]
        za = _interleave_f32(p, za)
        _bd_expand(za.astype(BF16), bbd_ref, hs_ref)
        _scan_tile(hs_ref, sc_ref, carry_ref, tt // SCAN_BLK, reverse=False)
        y = _bd_contract(hs_ref[...].astype(BF16), cbd_ref) + dskip_ref[...] * za
        ys_ref[...] = y
        g, _ = _gelu(y)
        q = _nn(g.astype(BF16), wglu_ref[...]) + bglu_ref[...]
        ycat_ref[:, 0:D_SSM] = _nn(pt, (g * jax.nn.sigmoid(q)).astype(BF16)).astype(BF16)
        pooled, _ = _pool_fwd(pbuf_ref, zb, halo_ref[...], i, tt)
        halo_ref[...] = zb[tt - HALO:tt, :]
        ycat_ref[:, D_SSM:D_SSM + D_POOL] = (_nn(pooled.astype(BF16), wpool_ref[...]) * pscale_ref[...]).astype(BF16)
        u, _, _, _, _, _ = _sgu_fwd(zu, zv, lng_ref[...], lnb_ref[...], wsm_ref, bsp_ref[...], mix_ref, tt)
        ycat_ref[:, D_SSM + D_POOL:D] = (u * mix_ref[...]).astype(BF16)

    return _pallas(
        body, name=name, grid=(n_tiles,),
        in_specs=[_row(tt, D_IN), _full((32, D_ST)), _full((D_SSM, 2 * D_ST)), _full((D_SSM, 2 * D_ST)),
                  _full((1, D_SSM)), _full((D_SSM, D_SSM)), _full((1, D_SSM)), _full((D_POOL, D_POOL)),
                  _full((1, D_POOL)), _full((1, D_SGU)), _full((1, D_SGU)), _full((HEADS, CHUNK, CHUNK)),
                  _full((CHUNK, D_SGU)), _full((tt, tt)), _full((tt, tt))],
        out_specs=[_row(tt, D), _row(tt, 2 * D_ST), _row(tt, D_SSM)],
        out_shape=[S((T, D), BF16), S((T, 2 * D_ST), F32), S((T, D_SSM), F32)],
        scratch_shapes=[pltpu.VMEM((SCAN_BLK, 2 * D_ST), F32), pltpu.VMEM((HALO, D_POOL), F32),
                        pltpu.VMEM((tt + HALO, D_POOL), F32), pltpu.VMEM((tt, D_SGU), F32)],
        vmem_mb=48, operands=(z, sc, bbd, cbd, dskip, wglu, bglu, wpool, pscale, lng, lnb, wsm, bsp, *perm),
        exchange=exchange)


def mixer_bwd(dx1b, z, hs, ys, wout, sc, bbd, cbd, dskip, wglu, bglu, wpool, pscale, lng, lnb, wsm, wsmt, bsp, perm, name,
              exchange=None):
    T = z.shape[0]
    tt = TT_MIX
    n_tiles = T // tt

    def rev(i):
        return n_tiles - 1 - i

    def body(dx_ref, z_ref, zprev_ref, hs_ref, hsprev_ref, ys_ref, wout_ref, sc_ref, bbd_ref, cbd_ref, dskip_ref,
             wglu_ref, bglu_ref, wpool_ref, pscale_ref, lng_ref, lnb_ref, wsm_ref, wsmt_ref, bsp_ref, p_ref, pt_ref,
             dz_ref, o_da, o_dbt, o_dct, o_dd, o_dbglu, o_dwglu, o_dwpool, o_dpscale, o_dlng, o_dlnb, o_dws, o_dbsp,
             gbuf_ref, carry_ref, accb_ref, accc_ref, ehalo_ref, pbuf_ref, mix_ref, dvn_ref, accw_ref, accm_ref):
        i = pl.program_id(0)
        tile = rev(i)

        @pl.when(i == 0)
        def _():
            carry_ref[...] = jnp.zeros_like(carry_ref)
            accb_ref[...] = jnp.zeros_like(accb_ref)
            accc_ref[...] = jnp.zeros_like(accc_ref)
            for o in (o_da, o_dd, o_dbglu, o_dwglu):
                o[...] = jnp.zeros_like(o)

        p, pt = p_ref[...], pt_ref[...]
        dxb = dx_ref[...]
        d_a = _nt(_nn(p, dxb).astype(BF16), wout_ref[0:D_SSM, :])
        d_bc = _nt(dxb, wout_ref[D_SSM:D, :])
        za = _interleave_f32(p, z_ref[:, 0:D_SSM])
        first_tile = (tile > 0).astype(F32)

        y = ys_ref[...]
        g, tg = _gelu(y)
        gb = g.astype(BF16)
        sg = jax.nn.sigmoid(_nn(gb, wglu_ref[...]) + bglu_ref[...])
        dq = d_a * g * sg * (1.0 - sg)
        dqb = dq.astype(BF16)
        o_dbglu[...] += _rowsum(dq)
        o_dwglu[...] += _tn(gb, dqb)
        dy = (d_a * sg + _nt(dqb, wglu_ref[...])) * _gelu_grad(y, tg)
        o_dd[...] += _rowsum(dy * za)
        dyb = dy.astype(BF16)
        _bd_accumulate(accc_ref, dyb, hs_ref[...].astype(BF16))
        _bd_expand(dyb, cbd_ref, gbuf_ref)
        _scan_tile(gbuf_ref, sc_ref, carry_ref, tt // SCAN_BLK, reverse=True)
        hprev = hsprev_ref[SCAN_BLK - 1:SCAN_BLK, :] * first_tile
        sub = lax.broadcasted_iota(jnp.int32, (SCAN_BLK, 1), 0)
        edge = jnp.where(sub == 0, hprev, pltpu.roll(hs_ref[tt - SCAN_BLK:tt, :], 1, axis=0))

        def da_terms(gr, gi, hr, hi):
            return _rowsum(gr * hr + gi * hi), _rowsum(gi * hr - gr * hi)

        body_re, body_im = da_terms(gbuf_ref[SCAN_BLK:tt, 0:D_ST], gbuf_ref[SCAN_BLK:tt, D_ST:],
                                    hs_ref[0:tt - SCAN_BLK, 0:D_ST], hs_ref[0:tt - SCAN_BLK, D_ST:])
        edge_re, edge_im = da_terms(gbuf_ref[0:SCAN_BLK, 0:D_ST], gbuf_ref[0:SCAN_BLK, D_ST:], edge[:, 0:D_ST], edge[:, D_ST:])
        o_da[:, 0:D_ST] += body_re + edge_re
        o_da[:, D_ST:] += body_im + edge_im
        gtb = gbuf_ref[...].astype(BF16)
        _bd_accumulate(accb_ref, za.astype(BF16), gtb)
        dza = (dy * dskip_ref[...] + _bd_contract(gtb, bbd_ref)).astype(BF16)
        dz_ref[:, 0:D_SSM] = _nn(pt, dza).astype(BF16)
        _pool_sgu_bwd(i, n_tiles, tile, tt, d_bc, z_ref, zprev_ref, wpool_ref, pscale_ref, lng_ref, lnb_ref, wsm_ref,
                      wsmt_ref, bsp_ref, dz_ref, o_dwpool, o_dpscale, o_dlng, o_dlnb, o_dws, o_dbsp,
                      ehalo_ref, pbuf_ref, mix_ref, dvn_ref, accw_ref, accm_ref)

        @pl.when(i == n_tiles - 1)
        def _():
            mask = _group_mask(D_SSM, 2 * D_ST)
            for acc_ref, o_ref in ((accb_ref, o_dbt), (accc_ref, o_dct)):
                fold = jnp.zeros((GRP, 2 * D_ST), F32)
                for gidx in range(N_GRP):
                    rows = slice(gidx * GRP, (gidx + 1) * GRP)
                    fold = fold + jnp.where(mask[rows, :], acc_ref[rows, :], 0.0)
                o_ref[...] = fold

    def rowr(n):
        return pl.BlockSpec((tt, n), lambda i: (rev(i), 0))

    zprev_spec = pl.BlockSpec((HALO, D_IN), lambda i: (jnp.maximum(rev(i) * (tt // HALO) - 1, 0), 0))
    hsprev_spec = pl.BlockSpec((SCAN_BLK, 2 * D_ST), lambda i: (jnp.maximum(rev(i) * (tt // SCAN_BLK) - 1, 0), 0))
    small = [S((1, 2 * D_ST), F32), S((GRP, 2 * D_ST), F32), S((GRP, 2 * D_ST), F32), S((1, D_SSM), F32),
             S((1, D_SSM), F32), S((D_SSM, D_SSM), F32), S((D_POOL, D_POOL), F32), S((1, D_POOL), F32),
             S((1, D_SGU), F32), S((1, D_SGU), F32), S((HEADS, CHUNK, CHUNK), F32), S((CHUNK, 128), F32)]
    return _pallas(
        body, name=name, grid=(n_tiles,),
        in_specs=[rowr(D), rowr(D_IN), zprev_spec, rowr(2 * D_ST), hsprev_spec, rowr(D_SSM), _full((D, D)),
                  _full((32, D_ST)), _full((D_SSM, 2 * D_ST)), _full((D_SSM, 2 * D_ST)), _full((1, D_SSM)),
                  _full((D_SSM, D_SSM)), _full((1, D_SSM)), _full((D_POOL, D_POOL)), _full((1, D_POOL)),
                  _full((1, D_SGU)), _full((1, D_SGU)), _full((HEADS, CHUNK, CHUNK)), _full((HEADS, CHUNK, CHUNK)),
                  _full((CHUNK, D_SGU)), _full((tt, tt)), _full((tt, tt))],
        out_specs=[rowr(D_IN)] + [_full(s.shape) for s in small],
        out_shape=[S((T, D_IN), BF16)] + small,
        scratch_shapes=[pltpu.VMEM((tt, 2 * D_ST), F32), pltpu.VMEM((SCAN_BLK, 2 * D_ST), F32),
                        pltpu.VMEM((D_SSM, 2 * D_ST), F32), pltpu.VMEM((D_SSM, 2 * D_ST), F32),
                        pltpu.VMEM((HALO, D_POOL), F32), pltpu.VMEM((tt + HALO, D_POOL), F32),
                        pltpu.VMEM((tt, D_SGU), F32), pltpu.VMEM((tt, D_SGU), F32),
                        pltpu.VMEM((HEADS, CHUNK, CHUNK), F32), pltpu.VMEM((CHUNK, D_SGU), F32)],
        vmem_mb=56, exchange=exchange,
        operands=(dx1b, z, z, hs, hs, ys, wout, sc, bbd, cbd, dskip, wglu, bglu, wpool, pscale, lng, lnb, wsm, wsmt, bsp, *perm))


def inproj_fwd(x, g, w_t, name, exchange=None):
    T = x.shape[0]
    tt = TT_PROJ

    def body(x_ref, g_ref, w_ref, h_ref, z_ref):
        xn, _ = _rms(x_ref[...])
        h = (xn * g_ref[...]).astype(BF16)
        h_ref[...] = h
        z_ref[...] = _nt(h, w_ref[...])

    return _pallas(
        body, name=name, grid=(T // tt,),
        in_specs=[_row(tt, D), _full((1, D)), _full((D_IN, D))],
        out_specs=[_row(tt, D), _row(tt, D_IN)],
        out_shape=[S((T, D), BF16), S((T, D_IN), F32)],
        scratch_shapes=[], vmem_mb=40, operands=(x, g, w_t), exchange=exchange)


def inproj_bwd(dzb, x, g, w_t, dx1):
    T = x.shape[0]
    tt = TT_PROJ

    def body(dz_ref, x_ref, g_ref, w_ref, dx1_ref, dx_ref, dg_ref):
        @pl.when(pl.program_id(0) == 0)
        def _():
            dg_ref[...] = jnp.zeros_like(dg_ref)

        dh = _nn(dz_ref[...], w_ref[...])
        xn, r = _rms(x_ref[...])
        dg_ref[...] += _rowsum(dh * xn)
        dx_ref[...] = dx1_ref[...] + _rms_bwd(dh, xn, r, g_ref[...])

    return pl.pallas_call(
        body, name="inproj_bwd", grid=(T // tt,),
        in_specs=[_row(tt, D_IN), _row(tt, D), _full((1, D)), _full((D_IN, D)), _row(tt, D)],
        out_specs=[_row(tt, D), _full((1, D))],
        out_shape=[S((T, D), F32), S((1, D), F32)],
        compiler_params=_cp(40, 1),
    )(dzb, x, g, w_t, dx1)


def _load_weights(pairs, sem):
    @pl.when(pl.program_id(0) == 0)
    def _():
        copies = [pltpu.make_async_copy(src, dst, sem.at[k]) for k, (src, dst) in enumerate(pairs)]
        for cp in copies:
            cp.start()
        for cp in copies:
            cp.wait()


def ffn_fwd(x, ycat, wout, g, wg_t, wu_t, wd, name, exchange=None, head=None):
    T = x.shape[0]
    tt = TT_FFN
    any_spec = pl.BlockSpec(memory_space=pl.ANY)

    def body(*refs):
        if head is None:
            (x_ref, ycat_ref, g_ref, wout_hbm, wg_hbm, wu_hbm, wd_hbm,
             x1_ref, h_ref, gate_ref, up_ref, act_ref, x2_ref, wout_v, wg_v, wu_v, wd_v, sem) = refs
        else:
            (x_ref, ycat_ref, g_ref, t_ref, gf_ref, wout_hbm, wg_hbm, wu_hbm, wd_hbm,
             x1_ref, h_ref, gate_ref, up_ref, act_ref, x2_ref, lvec_ref, dgf_ref, wout_v, wg_v, wu_v, wd_v, sem) = refs
        _ffn_fwd_tile(x_ref, ycat_ref, g_ref, wout_hbm, wg_hbm, wu_hbm, wd_hbm, x1_ref, h_ref, gate_ref, up_ref, act_ref,
                      x2_ref, wout_v, wg_v, wu_v, wd_v, sem)
        if head is not None:
            @pl.when(pl.program_id(0) == 0)
            def _():
                lvec_ref[...] = jnp.zeros_like(lvec_ref)
                dgf_ref[...] = jnp.zeros_like(dgf_ref)

            xn, r = _rms(x2_ref[...])
            gf = gf_ref[...]
            err = xn * gf - t_ref[...]
            lvec_ref[...] += _rowsum(err * err)
            dy = err * (1.0 / D)
            dgf_ref[...] += _rowsum(dy * xn)
            x2_ref[...] = _rms_bwd(dy, xn, r, gf)

    def _ffn_fwd_tile(x_ref, ycat_ref, g_ref, wout_hbm, wg_hbm, wu_hbm, wd_hbm,
                      x1_ref, h_ref, gate_ref, up_ref, act_ref, x2_ref, wout_v, wg_v, wu_v, wd_v, sem):
        _load_weights([(wout_hbm, wout_v), (wg_hbm, wg_v), (wu_hbm, wu_v), (wd_hbm, wd_v)], sem)
        x1 = x_ref[...] + _nn(ycat_ref[...], wout_v[...])
        x1_ref[...] = x1
        xn, _ = _rms(x1)
        h = (xn * g_ref[...]).astype(BF16)
        h_ref[...] = h
        gate = _nt(h, wg_v[...])
        up = _nt(h, wu_v[...])
        gate_ref[...] = gate.astype(BF16)
        up_ref[...] = up.astype(BF16)
        act = (gate * jax.nn.sigmoid(gate) * up).astype(BF16)
        act_ref[...] = act
        x2_ref[...] = x1 + _nn(act, wd_v[...])

    with_head = head is not None
    return _pallas(
        body, name=name, grid=(T // tt,),
        in_specs=[_row(tt, D), _row(tt, D), _full((1, D))] + ([_row(tt, D), _full((1, D))] if with_head else [])
        + [any_spec, any_spec, any_spec, any_spec],
        out_specs=[_row(tt, D), _row(tt, D), _row(tt, D_FF), _row(tt, D_FF), _row(tt, D_FF), _row(tt, D)]
        + ([_full((1, D)), _full((1, D))] if with_head else []),
        out_shape=[S((T, D), F32), S((T, D), BF16), S((T, D_FF), BF16), S((T, D_FF), BF16), S((T, D_FF), BF16),
                   S((T, D), F32)] + ([S((1, D), F32), S((1, D), F32)] if with_head else []),
        scratch_shapes=[pltpu.VMEM((D, D), BF16), pltpu.VMEM((D_FF, D), BF16), pltpu.VMEM((D_FF, D), BF16),
                        pltpu.VMEM((D_FF, D), BF16), pltpu.SemaphoreType.DMA((4,))],
        vmem_mb=56, operands=(x, ycat, g) + (tuple(head) if with_head else ()) + (wout, wg_t, wu_t, wd), exchange=exchange)


def _pool_sgu_bwd(i, n_tiles, tile, tt, d_bc, z_ref, zprev_ref, wpool_ref, pscale_ref, lng_ref, lnb_ref, wsm_ref, wsmt_ref,
                  bsp_ref, dz_ref, o_dwpool, o_dpscale, o_dlng, o_dlnb, o_dws, o_dbsp,
                  ehalo_ref, pbuf_ref, mix_ref, dvn_ref, accw_ref, accm_ref):
    @pl.when(i == 0)
    def _():
        ehalo_ref[...] = jnp.zeros_like(ehalo_ref)
        accw_ref[...] = jnp.zeros_like(accw_ref)
        accm_ref[...] = jnp.zeros_like(accm_ref)
        for o in (o_dwpool, o_dpscale, o_dlng, o_dlnb):
            o[...] = jnp.zeros_like(o)

    d_b = d_bc[:, 0:D_POOL]
    d_c = d_bc[:, D_POOL:D_POOL + D_SGU]
    zb = z_ref[:, D_SSM:D_SSM + D_POOL]
    zu = z_ref[:, D_SSM + D_POOL:D_SSM + D_POOL + D_SGU]
    zv = z_ref[:, D_SSM + D_POOL + D_SGU:D_IN]
    not_first = (tile > 0).astype(F32)

    pooled, cnt = _pool_fwd(pbuf_ref, zb, zprev_ref[:, D_SSM:D_SSM + D_POOL] * not_first, tile, tt)
    pooledb = pooled.astype(BF16)
    mixed = _nn(pooledb, wpool_ref[...])
    o_dpscale[...] += _rowsum(d_b * mixed)
    dmixb = (d_b * pscale_ref[...]).astype(BF16)
    o_dwpool[...] += _tn(pooledb, dmixb)
    dpooled = _nt(dmixb, wpool_ref[...])
    e = dpooled / cnt
    pbuf_ref[0:tt, :] = e
    pbuf_ref[tt:tt + HALO, :] = ehalo_ref[...]
    ehalo_ref[...] = e[0:HALO, :]
    x = pbuf_ref[...]
    n = tt + HALO
    f2 = x + pltpu.roll(x, n - 1, axis=0)
    f4 = f2 + pltpu.roll(f2, n - 2, axis=0)
    f8 = f4 + pltpu.roll(f4, n - 4, axis=0)
    f16 = f8 + pltpu.roll(f8, n - 8, axis=0)
    fwd_sum = _select_window(_lane_windows(D_POOL), f2, f4, f8, f16)[0:tt, :]
    dz_ref[:, D_SSM:D_SSM + D_POOL] = (fwd_sum - dpooled).astype(BF16)

    lng = lng_ref[...]
    u, su, sv, vhat, rstd, vnb = _sgu_fwd(zu, zv, lng, lnb_ref[...], wsm_ref, bsp_ref[...], mix_ref, tt)
    dz_ref[:, D_SSM + D_POOL:D_SSM + D_POOL + D_SGU] = (d_c * mix_ref[...] * _gelu_grad(zu, su)).astype(BF16)
    dmix = d_c * u
    dmixb2 = dmix.astype(BF16)
    for c in range(tt // CHUNK):
        accm_ref[...] += dmix[c * CHUNK:(c + 1) * CHUNK, :]
    for chunks in _chunk_pairs(tt):
        for h in range(HEADS):
            accw_ref[h] += _nt(_head_cols(dmixb2, chunks, h), _head_cols(vnb, chunks, h))
    _head_mix(wsmt_ref, dmixb2, dvn_ref, tt)
    dvn = dvn_ref[...]
    o_dlng[...] += _rowsum(dvn * vhat)
    o_dlnb[...] += _rowsum(dvn)
    dvh = dvn * lng
    dv = rstd * (dvh - jnp.mean(dvh, axis=-1, keepdims=True) - vhat * jnp.mean(dvh * vhat, axis=-1, keepdims=True))
    dz_ref[:, D_SSM + D_POOL + D_SGU:D_IN] = (dv * _gelu_grad(zv, sv)).astype(BF16)

    @pl.when(i == n_tiles - 1)
    def _():
        tri = (lax.broadcasted_iota(jnp.int32, (CHUNK, CHUNK), 0) >= lax.broadcasted_iota(jnp.int32, (CHUNK, CHUNK), 1))
        for h in range(HEADS):
            o_dws[h] = jnp.where(tri, accw_ref[h], 0.0)
        lane = lax.broadcasted_iota(jnp.int32, (1, 128), 1)
        acc = jnp.zeros((CHUNK, 128), F32)
        for h in range(HEADS):
            sh = jnp.sum(accm_ref[:, h * HEAD_DIM:(h + 1) * HEAD_DIM], axis=1, keepdims=True)
            acc = jnp.where(lane == h, sh, acc)
        o_dbsp[...] = acc


def ffn_bwd(dx2, x1, gate, up, g, wg_t, wu_t, wd, name, exchange=None):
    T = x1.shape[0]
    tt = TT_FFN
    any_spec = pl.BlockSpec(memory_space=pl.ANY)

    def body(dx2_ref, x1_ref, gate_ref, up_ref, g_ref, wg_hbm, wu_hbm, wd_hbm,
             dgu_ref, dx2b_ref, dx1_ref, dx1b_ref, dg_ref, wg_v, wu_v, wd_v, sem):
        _load_weights([(wg_hbm, wg_v), (wu_hbm, wu_v), (wd_hbm, wd_v)], sem)

        @pl.when(pl.program_id(0) == 0)
        def _():
            dg_ref[...] = jnp.zeros_like(dg_ref)

        dx2 = dx2_ref[...]
        dx2b = dx2.astype(BF16)
        dx2b_ref[...] = dx2b
        dact = _nt(dx2b, wd_v[...])
        gate = gate_ref[...].astype(F32)
        up = up_ref[...].astype(F32)
        sg = jax.nn.sigmoid(gate)
        dgate = (dact * up * (sg * (1.0 + gate * (1.0 - sg)))).astype(BF16)
        dup = (dact * gate * sg).astype(BF16)
        dgu_ref[:, 0:D_FF] = dgate
        dgu_ref[:, D_FF:2 * D_FF] = dup
        dh = _nn(dgate, wg_v[...]) + _nn(dup, wu_v[...])
        xn, r = _rms(x1_ref[...])
        dg_ref[...] += _rowsum(dh * xn)
        dx1 = dx2 + _rms_bwd(dh, xn, r, g_ref[...])
        dx1_ref[...] = dx1
        dx1b_ref[...] = dx1.astype(BF16)

    return _pallas(
        body, name=name, grid=(T // tt,),
        in_specs=[_row(tt, D), _row(tt, D), _row(tt, D_FF), _row(tt, D_FF), _full((1, D)), any_spec, any_spec, any_spec],
        out_specs=[_row(tt, 2 * D_FF), _row(tt, D), _row(tt, D), _row(tt, D), _full((1, D))],
        out_shape=[S((T, 2 * D_FF), BF16), S((T, D), BF16), S((T, D), F32), S((T, D), BF16), S((1, D), F32)],
        scratch_shapes=[pltpu.VMEM((D_FF, D), BF16), pltpu.VMEM((D_FF, D), BF16), pltpu.VMEM((D_FF, D), BF16),
                        pltpu.SemaphoreType.DMA((3,))],
        vmem_mb=56, operands=(dx2, x1, gate, up, g, wg_t, wu_t, wd), exchange=exchange)


def wgrad(a, b, tm, name, exchange=None, tk=TK_WGRAD):
    T, M = a.shape
    N = b.shape[1]
    tk = min(tk, T)
    n_k = T // tk

    def body(a_ref, b_ref, o_ref, acc_ref):
        k = pl.program_id(1)

        @pl.when(k == 0)
        def _():
            acc_ref[...] = jnp.zeros_like(acc_ref)

        acc_ref[...] += _tn(a_ref[...], b_ref[...])

        @pl.when(k == n_k - 1)
        def _():
            o_ref[...] = acc_ref[...].astype(BF16)

    (out,), got = _pallas(
        body, name=name, grid=(M // tm, n_k),
        in_specs=[pl.BlockSpec((tk, tm), lambda m, k: (k, m)), pl.BlockSpec((tk, N), lambda m, k: (k, 0))],
        out_specs=[pl.BlockSpec((tm, N), lambda m, k: (m, 0))],
        out_shape=[S((M, N), BF16)],
        scratch_shapes=[pltpu.VMEM((tm, N), F32)],
        vmem_mb=48, operands=(a, b), exchange=exchange)
    return out if exchange is None else (out, got)


def _mesh_place():
    x, y, c = lax.axis_index("x"), lax.axis_index("y"), lax.axis_index("c")
    return x, y, c, 4 * x + 2 * y + c


def _peer(x, y, c, k):
    px = 1 - x if k & 4 else x
    py = 1 - y if k & 2 else y
    pc = 1 - c if k & 1 else c
    return (px, py, pc), 4 * px + 2 * py + pc


class _Exchange:
    SAME_CORE = (2, 4, 6)

    def __init__(self, gather=(), scatter=()):
        self.entries = [(a, None, a.shape[0]) for a in gather] + [(a, off, rows) for a, off, rows in scatter]
        self.n_gather = len(gather)

    @property
    def n(self):
        return len(self.entries)

    def operands(self):
        return [e[0] for e in self.entries]

    def out_shapes(self):
        return [S((N_DEV, rows, a.shape[1]), a.dtype) for a, _, rows in self.entries]

    def sems(self):
        return [pltpu.SemaphoreType.DMA((self.n, N_DEV)), pltpu.SemaphoreType.DMA((self.n, N_DEV)),
                pltpu.SemaphoreType.DMA((self.n,))]

    def _src(self, ref, e, idx):
        _, off, rows = self.entries[e]
        if off is None:
            return ref
        return ref.at[pl.ds(pl.multiple_of(off + idx * rows, 16), rows)]

    def _masks(self, e):
        return (1,) + self.SAME_CORE if e < self.n_gather else tuple(range(1, N_DEV))

    def _copy(self, ins, outs, sems, e, k, sending, passing_on=False):
        send_sems, recv_sems, _ = sems
        x, y, c, me = _mesh_place()
        peer, pidx = _peer(x, y, c, k)
        if passing_on:
            return pltpu.make_async_remote_copy(
                src_ref=outs[e].at[pidx], dst_ref=outs[e].at[pidx], send_sem=send_sems.at[e, k | 1],
                recv_sem=recv_sems.at[e, k | 1], device_id=_peer(x, y, c, 1)[0], device_id_type=pl.DeviceIdType.MESH)
        return pltpu.make_async_remote_copy(
            src_ref=self._src(ins[e], e, pidx), dst_ref=outs[e].at[me if sending else pidx], send_sem=send_sems.at[e, k],
            recv_sem=recv_sems.at[e, k], device_id=peer, device_id_type=pl.DeviceIdType.MESH)

    def _local(self, ins, outs, sems):
        me = _mesh_place()[3]
        return [pltpu.make_async_copy(self._src(ins[e], e, me), outs[e].at[me], sems[2].at[e]) for e in range(self.n)]

    def start(self, ins, outs, sems):
        for cp in self._local(ins, outs, sems):
            cp.start()
        for k in range(1, N_DEV):
            for e in range(self.n):
                if k in self._masks(e):
                    self._copy(ins, outs, sems, e, k, True).start()

    def forward(self, ins, outs, sems):
        for k in self.SAME_CORE:
            for e in range(self.n_gather):
                self._copy(ins, outs, sems, e, k, False).wait_recv()
                self._copy(ins, outs, sems, e, k, False, passing_on=True).start()

    def wait(self, ins, outs, sems):
        for k in range(1, N_DEV):
            for e in range(self.n):
                if e >= self.n_gather or k % 2:
                    self._copy(ins, outs, sems, e, k, False).wait_recv()
        for k in range(1, N_DEV):
            for e in range(self.n):
                self._copy(ins, outs, sems, e, k, True).wait_send()
        for cp in self._local(ins, outs, sems):
            cp.wait()


def _pallas(body, *, name, grid, in_specs, out_specs, out_shape, scratch_shapes, vmem_mb, operands, exchange=None,
            aliases=None):
    n_in, n_out, n_scr = len(in_specs), len(out_specs), len(scratch_shapes)
    n_steps = math.prod(grid)
    aliases = aliases or {}
    if exchange is None:
        res = pl.pallas_call(body, name=name, grid=grid, in_specs=in_specs, out_specs=out_specs, out_shape=out_shape,
                             scratch_shapes=scratch_shapes, input_output_aliases=aliases,
                             compiler_params=_cp(vmem_mb, len(grid)))(*operands)
        return list(res), []
    ex = exchange

    def hosted(*refs):
        ins, ex_in = refs[:n_in], refs[n_in:n_in + ex.n]
        outs = refs[n_in + ex.n:n_in + ex.n + n_out]
        ex_out = refs[n_in + ex.n + n_out:n_in + 2 * ex.n + n_out]
        scr = refs[n_in + 2 * ex.n + n_out:]
        sems = scr[n_scr:]
        step = pl.program_id(0)
        for axis in range(1, len(grid)):
            step = step * grid[axis] + pl.program_id(axis)

        @pl.when(step == 0)
        def _():
            ex.start(ex_in, ex_out, sems)

        body(*ins, *outs, *scr[:n_scr])

        if ex.n_gather:
            @pl.when(step == max(n_steps - 1 - max(2, n_steps // 8), 0))
            def _():
                ex.forward(ex_in, ex_out, sems)

        @pl.when(step == n_steps - 1)
        def _():
            ex.wait(ex_in, ex_out, sems)

    any_spec = pl.BlockSpec(memory_space=pl.ANY)
    res = pl.pallas_call(
        hosted, name=name, grid=grid, in_specs=list(in_specs) + [any_spec] * ex.n,
        out_specs=list(out_specs) + [any_spec] * ex.n, out_shape=list(out_shape) + ex.out_shapes(),
        scratch_shapes=list(scratch_shapes) + ex.sems(), input_output_aliases=aliases,
        compiler_params=_cp(vmem_mb, len(grid)),
    )(*operands, *ex.operands())
    return list(res[:n_out]), list(res[n_out:])


def exchange_only(ex, name):
    def body(*refs):
        ins, outs, sems = refs[:ex.n], refs[ex.n:2 * ex.n], refs[2 * ex.n:]
        ex.start(ins, outs, sems)
        ex.forward(ins, outs, sems)
        ex.wait(ins, outs, sems)

    any_spec = pl.BlockSpec(memory_space=pl.ANY)
    return list(pl.pallas_call(body, name=name, in_specs=[any_spec] * ex.n, out_specs=[any_spec] * ex.n,
                               out_shape=ex.out_shapes(), scratch_shapes=ex.sems())(*ex.operands()))


def _adamw(w, g, m, v):
    m = ADAM_B1 * m + (1.0 - ADAM_B1) * g
    v = ADAM_B2 * v + (1.0 - ADAM_B2) * (g * g)
    m_hat = m / (1.0 - ADAM_B1 ** ADAM_STEP)
    v_hat = v / (1.0 - ADAM_B2 ** ADAM_STEP)
    delta = -ADAM_LR * (m_hat / (jnp.sqrt(v_hat) + ADAM_EPS) + ADAM_WD * w)
    return delta, m, v


def _sum_parts(p_ref, rows=slice(None)):
    g = p_ref[0, rows].astype(F32)
    for k in range(1, N_DEV):
        g = g + p_ref[k, rows].astype(F32)
    return g


def adamw_layers(parts, w, m, v, name):
    n_l = len(parts)

    def body(*refs):
        p_refs = refs[:n_l]
        w_ref, m_ref, v_ref, g_out, d_out, m_out, v_out = refs[n_l:]
        for l in range(n_l):
            g = _sum_parts(p_refs[l])
            g_out[l] = g
            d_out[l], m_out[l], v_out[l] = _adamw(w_ref[l], g, m_ref[l], v_ref[l])

    return pl.pallas_call(
        body, name=name, out_shape=[S(w.shape, F32)] * 4, compiler_params=_cp(48),
    )(*parts, w, m, v)


def adamw_segments(parts, segments, w, m, v, name):
    n_p = len(parts)

    def body(*refs):
        p_refs = refs[:n_p]
        w_ref, m_ref, v_ref, g_out, d_out, m_out, v_out = refs[n_p:]
        for part, src, dst, rows in segments:
            g = _sum_parts(p_refs[part], slice(src, src + rows))
            to = slice(dst, dst + rows)
            g_out[to] = g
            d_out[to], m_out[to], v_out[to] = _adamw(w_ref[to], g, m_ref[to], v_ref[to])

    return pl.pallas_call(
        body, name=name, out_shape=[S(w.shape, F32)] * 4, compiler_params=_cp(48),
    )(*parts, w, m, v)


SMALL_LAYER = ("g_mix", "A_re", "A_im", "log_dt", "B_re", "B_im", "C_re", "C_im", "D_skip", "b_glu", "w_pool",
               "pool_scale", "sgu_ln_g", "sgu_ln_b", "w_spatial", "b_spatial", "g_ffn")
BIG_NAMES = ("w_in", "w_glu", "w_out", "w_gate", "w_up", "w_down")
COLUMN_SHARDED = ("w_in", "w_gate", "w_up")
WEIGHT_ORDER = ("g_mix", "w_in", "A_re", "A_im", "log_dt", "B_re", "B_im", "C_re", "C_im", "D_skip", "w_glu", "b_glu",
                "w_pool", "pool_scale", "sgu_ln_g", "sgu_ln_b", "w_spatial", "b_spatial", "w_out", "g_ffn", "w_gate",
                "w_up", "w_down", "g_final")
SEG = 1024
SMALL_AS_BF16 = ("B_re", "B_im", "C_re", "C_im", "w_pool", "w_spatial")


def _pack(arrays, dtype=F32):
    seg = SEG * 4 // jnp.dtype(dtype).itemsize
    parts = []
    for a in arrays:
        flat = a.reshape(-1).astype(dtype)
        parts.append(jnp.pad(flat, (0, (-flat.shape[0]) % seg)))
    return jnp.concatenate(parts).reshape(-1, 128)


def _state_rows(p):
    return p.reshape(1, D_ST)


def _chan_by_state(p):
    return jnp.transpose(p, (2, 0, 1)).reshape(GRP, D_ST)


def _chan_by_state_c(p):
    return jnp.transpose(p, (1, 0, 2)).reshape(GRP, D_ST)


def kernel(x, g_mix, w_in, A_re, A_im, log_dt, B_re, B_im, C_re, C_im, D_skip, w_glu, b_glu, w_pool, pool_scale, sgu_ln_g, sgu_ln_b, w_spatial, b_spatial, w_out, g_ffn, w_gate, w_up, w_down, g_final, loss_target, m_g_mix, m_w_in, m_A_re, m_A_im, m_log_dt, m_B_re, m_B_im, m_C_re, m_C_im, m_D_skip, m_w_glu, m_b_glu, m_w_pool, m_pool_scale, m_sgu_ln_g, m_sgu_ln_b, m_w_spatial, m_b_spatial, m_w_out, m_g_ffn, m_w_gate, m_w_up, m_w_down, m_g_final, v_g_mix, v_w_in, v_A_re, v_A_im, v_log_dt, v_B_re, v_B_im, v_C_re, v_C_im, v_D_skip, v_w_glu, v_b_glu, v_w_pool, v_pool_scale, v_sgu_ln_g, v_sgu_ln_b, v_w_spatial, v_b_spatial, v_w_out, v_g_ffn, v_w_gate, v_w_up, v_w_down, v_g_final):
    args = dict(locals())
    W = {n: args[n] for n in WEIGHT_ORDER}
    M = {n: args["m_" + n] for n in WEIGHT_ORDER}
    V = {n: args["v_" + n] for n in WEIGHT_ORDER}
    n_layers = g_mix.shape[0]
    x0 = x[0]
    target = loss_target[0]

    def my_rows(name, l):
        w = W[name][l]
        return (w.T if name in COLUMN_SHARDED else w).astype(BF16)

    full_w = [dict() for _ in range(n_layers)]

    def gather_of(*which):
        return _Exchange(gather=[my_rows(n, l) for n, l in which])

    def keep_gathered(which, arrays):
        for (n, l), a in zip(which, arrays):
            full_w[l][n] = a.reshape(-1, a.shape[-1])

    tri = jnp.tril(jnp.ones((CHUNK, CHUNK), bool))
    perm = _interleave_matrices(TT_MIX)
    consts = []
    for l in range(n_layers):
        a_re, a_im = _state_rows(A_re[l]), _state_rows(A_im[l])
        ldt = jnp.repeat(log_dt[l], N_STATE).reshape(1, D_ST)
        b_re_t, b_im_t = _chan_by_state(B_re[l]), _chan_by_state(B_im[l])
        sc, bbd, cbd = s5_prepare(a_re, a_im, ldt, b_re_t, b_im_t, _chan_by_state_c(C_re[l]), _chan_by_state_c(C_im[l]))
        wsm = jnp.where(tri[None], w_spatial[l], 0.0)
        wpool_bd = jnp.zeros((D_POOL, D_POOL), F32)
        for gi in range(len(POOL_WINDOWS)):
            wpool_bd = wpool_bd.at[gi * 64:(gi + 1) * 64, gi * 64:(gi + 1) * 64].set(w_pool[l, gi])
        consts.append(dict(
            disc=(a_re, a_im, ldt, b_re_t, b_im_t), sc=sc, bbd=bbd, cbd=cbd,
            dskip=D_skip[l].reshape(1, D_SSM), bglu=b_glu[l].reshape(1, D_SSM),
            wpool=wpool_bd.astype(BF16), pscale=pool_scale[l].reshape(1, D_POOL),
            lng=sgu_ln_g[l].reshape(1, D_SGU), lnb=sgu_ln_b[l].reshape(1, D_SGU),
            wsm=wsm.astype(BF16), wsmt=jnp.transpose(wsm, (0, 2, 1)).astype(BF16),
            bsp=jnp.repeat(b_spatial[l].T, HEAD_DIM, axis=1),
            gmix=g_mix[l].reshape(1, D), gffn=g_ffn[l].reshape(1, D)))

    def mixer_args(l):
        c = consts[l]
        return (c["bbd"], c["cbd"], c["dskip"], full_w[l]["w_glu"], c["bglu"], c["wpool"], c["pscale"], c["lng"], c["lnb"])

    first_needed = [("w_in", 0)]
    keep_gathered(first_needed, exchange_only(gather_of(*first_needed), "gather_first"))
    carried_fwd = {
        ("inproj", 0): [("w_glu", 0), ("w_out", 0)],
        ("mixer", 0): [("w_gate", 0), ("w_up", 0), ("w_down", 0)],
        ("ffn", 0): [("w_in", 1), ("w_glu", 1), ("w_out", 1), ("w_gate", 1)],
        ("mixer", 1): [("w_up", 1), ("w_down", 1)],
    }

    def carried(kind, l):
        which = carried_fwd.get((kind, l))
        return which, (gather_of(*which) if which else None)

    saved = []
    xl = x0
    for l in range(n_layers):
        c, fw = consts[l], full_w[l]
        which, ex = carried("inproj", l)
        (h, z), got = inproj_fwd(xl, c["gmix"], fw["w_in"], f"inproj_fwd_{l}", ex)
        keep_gathered(which or [], got)
        which, ex = carried("mixer", l)
        (ycat, hs, ys), got = mixer_fwd(z, c["sc"], *mixer_args(l), c["wsm"], c["bsp"], perm, f"mixer_fwd_{l}", ex)
        keep_gathered(which or [], got)
        which, ex = carried("ffn", l)
        head = (target, g_final.reshape(1, D)) if l == n_layers - 1 else None
        res, got = ffn_fwd(xl, ycat, fw["w_out"], c["gffn"], fw["w_gate"], fw["w_up"], fw["w_down"], f"ffn_fwd_{l}", ex, head)
        keep_gathered(which or [], got)
        x1, h2, gate, up, act, x2 = res[:6]
        saved.append(dict(x=xl, h=h, z=z, ycat=ycat, hs=hs, ys=ys, x1=x1, h2=h2, gate=gate, up=up, act=act))
        xl = x2
    dx, loss_vec, d_gfinal = xl, res[6], res[7]

    recv_big = {}
    recv_small = [None] * (2 * n_layers)

    def keep_received(which, arrays):
        for key, a in zip(which, arrays):
            if key[0] == "small":
                recv_small[key[1]] = a
            elif key[0] == "small16":
                recv_small[n_layers + key[1]] = a
            else:
                recv_big[key] = a

    pending = None
    for l in reversed(range(n_layers)):
        c, fw, sv = consts[l], full_w[l], saved[l]
        (dgu, dx2b, dx1, dx1b, d_gffn), got = ffn_bwd(dx, sv["x1"], sv["gate"], sv["up"], c["gffn"], fw["w_gate"], fw["w_up"],
                                                     fw["w_down"], f"ffn_bwd_{l}", pending[1] if pending else None)
        if pending:
            keep_received(pending[0], got)
        g_gu = wgrad(dgu, sv["h2"], D_FF // 2, f"wgrad_gate_up_{l}")
        g_down = wgrad(sv["act"], dx2b, D_FF // 2, f"wgrad_down_{l}")
        g_out = wgrad(sv["ycat"], dx1b, D, f"wgrad_out_{l}")
        ffn_rows = D_FF // N_DEV
        ex = _Exchange(scatter=[(g_gu, 0, ffn_rows), (g_gu, D_FF, ffn_rows), (g_down, 0, ffn_rows), (g_out, 0, D // N_DEV)])
        (dzb, da, dbt, dct, dd, dbglu, dwglu, dwpool, dpscale, dlng, dlnb, dws, dbsp), got = mixer_bwd(
            dx1b, sv["z"], sv["hs"], sv["ys"], fw["w_out"], c["sc"], *mixer_args(l), c["wsm"], c["wsmt"], c["bsp"],
            perm, f"mixer_bwd_{l}", ex)
        keep_received([("w_gate", l), ("w_up", l), ("w_down", l), ("w_out", l)], got)
        dx, d_gmix = inproj_bwd(dzb, sv["x"], c["gmix"], fw["w_in"], dx1)
        d_are, d_aim, d_ldt, d_bre_t, d_bim_t = s5_param_bwd(*c["disc"], da, dbt)
        small = dict(
            g_mix=d_gmix.reshape(D), A_re=d_are.reshape(N_GRP, N_STATE), A_im=d_aim.reshape(N_GRP, N_STATE),
            log_dt=d_ldt[0, :N_GRP],
            B_re=jnp.transpose(d_bre_t.reshape(GRP, N_GRP, N_STATE), (1, 0, 2)),
            B_im=jnp.transpose(d_bim_t.reshape(GRP, N_GRP, N_STATE), (1, 0, 2)),
            C_re=jnp.transpose(dct[:, :D_ST].reshape(GRP, N_GRP, N_STATE), (1, 0, 2)),
            C_im=-jnp.transpose(dct[:, D_ST:].reshape(GRP, N_GRP, N_STATE), (1, 0, 2)),
            D_skip=dd.reshape(D_SSM), b_glu=dbglu.reshape(D_SSM),
            w_pool=jnp.stack([dwpool[gi * 64:(gi + 1) * 64, gi * 64:(gi + 1) * 64] for gi in range(len(POOL_WINDOWS))]),
            pool_scale=dpscale.reshape(D_POOL), sgu_ln_g=dlng.reshape(D_SGU), sgu_ln_b=dlnb.reshape(D_SGU),
            w_spatial=dws, b_spatial=dbsp[:, :HEADS].T, g_ffn=d_gffn.reshape(D))
        packed = [small[n] for n in SMALL_LAYER if n not in SMALL_AS_BF16]
        packed += [d_gfinal.reshape(D), loss_vec.reshape(D)] if l == 0 else []
        packed16 = [small[n] for n in SMALL_LAYER if n in SMALL_AS_BF16]
        small_entries = dict(gather=[_pack(packed), _pack(packed16, BF16)], scatter=[(dwglu.astype(BF16), 0, D_SSM // N_DEV)])
        if l > 0:
            g_in = wgrad(dzb, sv["h"], D_IN, f"wgrad_in_{l}")
            small_entries["scatter"].append((g_in, 0, D_IN // N_DEV))
            pending = ([("small", l), ("small16", l), ("w_glu", l), ("w_in", l)], _Exchange(**small_entries))
        else:
            g_in, got = wgrad(dzb, sv["h"], D_IN, f"wgrad_in_{l}", tk=TK_WGRAD // 4, exchange=_Exchange(**small_entries))
            keep_received([("small", l), ("small16", l), ("w_glu", l)], got)
            pending = ([("w_in", l)], _Exchange(scatter=[(g_in, 0, D_IN // N_DEV)]))
    grad_x = dx
    keep_received(pending[0], exchange_only(pending[1], "exchange_last"))

    out = {}
    for n in BIG_NAMES:
        tr = (lambda a: jnp.transpose(a, (0, 2, 1))) if n in COLUMN_SHARDED else (lambda a: a)
        res = adamw_layers([recv_big[(n, l)] for l in range(n_layers)], tr(W[n]), tr(M[n]), tr(V[n]), f"adamw_{n}")
        out[n] = [tr(r) for r in res]

    seg_rows = [(-(-math.prod(W[n].shape[1:]) // SEG)) * (SEG // 128) for n in SMALL_LAYER]
    segments, src, src16, dst = [], 0, 0, 0
    for n, rows in zip(SMALL_LAYER, seg_rows):
        if n in SMALL_AS_BF16:
            segments += [(n_layers + l, src16, dst + l * rows, rows) for l in range(n_layers)]
            src16 += rows
        else:
            segments += [(l, src, dst + l * rows, rows) for l in range(n_layers)]
            src += rows
        dst += n_layers * rows
    tile_rows = SEG // 128
    segments += [(0, src, dst, tile_rows), (0, src + tile_rows, dst + tile_rows, tile_rows)]

    channel_major = ("B_re", "B_im")

    def pack_params(P):
        parts = []
        for n, rows in zip(SMALL_LAYER, seg_rows):
            flat = (jnp.swapaxes(P[n], -1, -2) if n in channel_major else P[n]).reshape(n_layers, -1)
            parts.append(jnp.pad(flat, ((0, 0), (0, rows * 128 - flat.shape[1]))).reshape(-1))
        return jnp.concatenate(parts + [P["g_final"], jnp.zeros((SEG,), F32)]).reshape(-1, 128)

    res = adamw_segments(recv_small, segments, pack_params(W), pack_params(M), pack_params(V), "adamw_small")
    for j in range(4):
        flat, off = res[j].reshape(-1), 0
        for n, rows in zip(SMALL_LAYER, seg_rows):
            size = math.prod(W[n].shape[1:])
            piece = flat[off:off + n_layers * rows * 128].reshape(n_layers, rows * 128)[:, :size]
            if n in channel_major:
                piece = jnp.swapaxes(piece.reshape(W[n].shape[:-2] + W[n].shape[:-3:-1]), -1, -2)
            else:
                piece = piece.reshape(W[n].shape)
            out.setdefault(n, []).append(piece)
            off += n_layers * rows * 128
        out.setdefault("g_final", []).append(flat[off:off + D])
        if j == 0:
            loss = (0.5 / D) * jnp.sum(flat[off + SEG:off + SEG + D])

    return (loss, grad_x[None], *[out[n][0] for n in WEIGHT_ORDER], *[out[n][1] for n in WEIGHT_ORDER],
            *[out[n][2] for n in WEIGHT_ORDER], *[out[n][3] for n in WEIGHT_ORDER])
```

```python
import math

import jax
import jax.numpy as jnp
from jax import lax
from jax.experimental import pallas as pl
from jax.experimental.pallas import tpu as pltpu

F32 = jnp.float32
BF16 = jnp.bfloat16
S = jax.ShapeDtypeStruct

N_DEV = 8
D = 1024
D_SSM = 384
N_GRP = 24
GRP = 16
N_STATE = 64
D_ST = N_GRP * N_STATE
D_POOL = 256
POOL_WINDOWS = (2, 4, 8, 16)
HALO = 16
D_SGU = 384
HEADS = 6
HEAD_DIM = 64
CHUNK = 128
D_IN = 1408
D_FF = 2816
EPS = 1e-6
SCAN_BLK = 8

ADAM_LR = 0.001
ADAM_B1 = 0.9
ADAM_B2 = 0.999
ADAM_EPS = 1e-08
ADAM_WD = 0.01
ADAM_STEP = 10

GELU_C0 = math.sqrt(2.0 / math.pi)
GELU_C1 = 0.044715

TT_MIX = 256
SEG_LEN = TT_MIX // SCAN_BLK
TT_FFN = 256
TT_PROJ = 1024
TK_WGRAD = 2048
VMEM_MB = 2 ** 20


def _cp(vmem_mb, grid_dims=0):
    kw = dict(vmem_limit_bytes=int(vmem_mb * VMEM_MB))
    if grid_dims:
        kw["dimension_semantics"] = ("arbitrary",) * grid_dims
    return pltpu.CompilerParams(**kw)


def _row(tt, n):
    return pl.BlockSpec((tt, n), lambda i: (i, 0))


def _full(shape):
    nd = len(shape)
    return pl.BlockSpec(shape, lambda *_: (0,) * nd)


def _nn(a, b):
    return jnp.dot(a, b, preferred_element_type=F32)


def _nt(a, b):
    return lax.dot_general(a, b, (((1,), (1,)), ((), ())), preferred_element_type=F32)


def _tn(a, b):
    return lax.dot_general(a, b, (((0,), (0,)), ((), ())), preferred_element_type=F32)


def _rowsum(x):
    return jnp.sum(x, axis=0, keepdims=True)


def _rms(x):
    r = lax.rsqrt(jnp.mean(x * x, axis=-1, keepdims=True) + EPS)
    return x * r, r


def _rms_bwd(dy, xn, r, g):
    dyg = dy * g
    return r * (dyg - xn * jnp.mean(dyg * xn, axis=-1, keepdims=True))


def _gelu(x):
    s = jax.nn.sigmoid(x * (2.0 * GELU_C0 + (2.0 * GELU_C0 * GELU_C1) * (x * x)))
    return x * s, s


def _gelu_grad(x, s):
    return s * (1.0 + x * (1.0 - s) * (2.0 * GELU_C0 + (6.0 * GELU_C0 * GELU_C1) * (x * x)))


def _discretise(a_re, a_im, ldt, b_re, b_im):
    dt = jnp.exp(ldt)
    mag = jnp.exp(a_re * dt)
    ar = mag * jnp.cos(a_im * dt)
    ai = mag * jnp.sin(a_im * dt)
    den = a_re * a_re + a_im * a_im
    f_re = ((ar - 1.0) * a_re + ai * a_im) / den
    f_im = (ai * a_re - (ar - 1.0) * a_im) / den
    bb_re = f_re * b_re - f_im * b_im
    bb_im = f_re * b_im + f_im * b_re
    return ar, ai, bb_re, bb_im


def _group_mask(rows, cols):
    r = lax.broadcasted_iota(jnp.int32, (rows, cols), 0) // GRP
    c = lax.broadcasted_iota(jnp.int32, (rows, cols), 1)
    c = jnp.where(c >= D_ST, c - D_ST, c) // N_STATE
    return r == c


def s5_prepare(a_re, a_im, ldt, b_re_t, b_im_t, c_re_t, c_im_t):
    def body(are_ref, aim_ref, ldt_ref, bre_ref, bim_ref, cre_ref, cim_ref, sc_ref, bbd_ref, cbd_ref):
        ar, ai, bb_re, bb_im = _discretise(are_ref[...], aim_ref[...], ldt_ref[...], bre_ref[...], bim_ref[...])
        mask = _group_mask(D_SSM, 2 * D_ST)
        bb = jnp.concatenate([jnp.tile(bb_re, (N_GRP, 1)), jnp.tile(bb_im, (N_GRP, 1))], axis=1)
        bbd_ref[...] = jnp.where(mask, bb, 0.0).astype(BF16)
        cc = jnp.concatenate([jnp.tile(cre_ref[...], (N_GRP, 1)), -jnp.tile(cim_ref[...], (N_GRP, 1))], axis=1)
        cbd_ref[...] = jnp.where(mask, cc, 0.0).astype(BF16)
        pr, pi = ar, ai
        for _ in range(SEG_LEN - 1):
            pr, pi = pr * ar - pi * ai, pr * ai + pi * ar
        for k, v in enumerate((ar, ai, pr, pi)):
            sc_ref[8 * k:8 * k + 8, :] = jnp.broadcast_to(v, (SCAN_BLK, D_ST))

    return pl.pallas_call(
        body, name="s5_prepare",
        out_shape=[S((32, D_ST), F32), S((D_SSM, 2 * D_ST), BF16), S((D_SSM, 2 * D_ST), BF16)],
        compiler_params=_cp(40),
    )(a_re, a_im, ldt, b_re_t, b_im_t, c_re_t, c_im_t)


def s5_param_bwd(a_re, a_im, ldt, b_re_t, b_im_t, da, dbt):
    def body(are_ref, aim_ref, ldt_ref, bre_ref, bim_ref, da_ref, dbt_ref, o_are, o_aim, o_ldt, o_bre, o_bim):
        _, vjp = jax.vjp(_discretise, are_ref[...], aim_ref[...], ldt_ref[...], bre_ref[...], bim_ref[...])
        da = da_ref[...]
        dbt = dbt_ref[...]
        g_are, g_aim, g_ldt, g_bre, g_bim = vjp((da[:, :D_ST], da[:, D_ST:], dbt[:, :D_ST], dbt[:, D_ST:]))
        o_are[...] = g_are
        o_aim[...] = g_aim
        o_bre[...] = g_bre
        o_bim[...] = g_bim
        grp = lax.broadcasted_iota(jnp.int32, (1, D_ST), 1) // N_STATE
        lane = lax.broadcasted_iota(jnp.int32, (1, 128), 1)
        out = jnp.zeros((1, 128), F32)
        for g in range(N_GRP):
            out = jnp.where(lane == g, jnp.sum(jnp.where(grp == g, g_ldt, 0.0), axis=1, keepdims=True), out)
        o_ldt[...] = out

    return pl.pallas_call(
        body, name="s5_param_bwd",
        out_shape=[S((1, D_ST), F32), S((1, D_ST), F32), S((1, 128), F32), S((GRP, D_ST), F32), S((GRP, D_ST), F32)],
        compiler_params=_cp(16),
    )(a_re, a_im, ldt, b_re_t, b_im_t, da, dbt)


_CH = ((0, 256), (256, D_SSM))
_ST = ((0, 1024), (1024, D_ST))


def _bd_expand(xb, w_ref, out_ref):
    for (c0, c1), (s0, s1) in zip(_CH, _ST):
        for half in (0, D_ST):
            out_ref[:, half + s0:half + s1] = _nn(xb[:, c0:c1], w_ref[c0:c1, half + s0:half + s1])


def _bd_contract(hb, w_ref):
    parts = []
    for (c0, c1), (s0, s1) in zip(_CH, _ST):
        parts.append(_nt(hb[:, s0:s1], w_ref[c0:c1, s0:s1]) + _nt(hb[:, D_ST + s0:D_ST + s1], w_ref[c0:c1, D_ST + s0:D_ST + s1]))
    return jnp.concatenate(parts, axis=1)


def _bd_accumulate(acc_ref, xb, hb):
    for j in range(N_GRP // 4):
        ch = slice(4 * GRP * j, 4 * GRP * (j + 1))
        for half in (0, D_ST):
            st = slice(half + 4 * N_STATE * j, half + 4 * N_STATE * (j + 1))
            acc_ref[ch, st] += _tn(xb[:, ch], hb[:, st])


def _interleave_matrices(tt):
    r = lax.broadcasted_iota(jnp.int32, (tt, tt), 0)
    t = lax.broadcasted_iota(jnp.int32, (tt, tt), 1)
    p = (t == (r % SCAN_BLK) * (tt // SCAN_BLK) + r // SCAN_BLK).astype(BF16)
    return p, p.T


def _interleave_f32(p, x):
    hi = x.astype(BF16)
    lo = (x - hi.astype(F32)).astype(BF16)
    return _nn(p, hi) + _nn(p, lo)


def _scan_tile(buf_ref, sc_ref, carry_ref, n_blk, reverse, on_block=None):
    ar = sc_ref[0:8, :]
    ai = -sc_ref[8:16, :] if reverse else sc_ref[8:16, :]

    def rows(i):
        blk = (n_blk - 1 - i) if reverse else i
        return pl.ds(pl.multiple_of(blk * SCAN_BLK, SCAN_BLK), SCAN_BLK)

    def local(i, x):
        xr, xi = x
        r = rows(i)
        xr, xi = buf_ref[r, 0:D_ST] + ar * xr - ai * xi, buf_ref[r, D_ST:2 * D_ST] + ar * xi + ai * xr
        buf_ref[r, 0:D_ST] = xr
        buf_ref[r, D_ST:2 * D_ST] = xi
        return xr, xi

    zero = jnp.zeros((SCAN_BLK, D_ST), F32)
    end_r, end_i = lax.fori_loop(0, n_blk, local, (zero, zero), unroll=True)

    seg_r = sc_ref[16:17, :]
    seg_i = -sc_ref[24:25, :] if reverse else sc_ref[24:25, :]
    cr, ci = carry_ref[0:1, 0:D_ST], carry_ref[0:1, D_ST:2 * D_ST]
    sub = lax.broadcasted_iota(jnp.int32, (SCAN_BLK, D_ST), 0)
    in_r, in_i = zero, zero
    for s in (reversed(range(SCAN_BLK)) if reverse else range(SCAN_BLK)):
        in_r = jnp.where(sub == s, cr, in_r)
        in_i = jnp.where(sub == s, ci, in_i)
        cr, ci = end_r[s:s + 1, :] + seg_r * cr - seg_i * ci, end_i[s:s + 1, :] + seg_r * ci + seg_i * cr
    carry_ref[0:1, 0:D_ST] = cr
    carry_ref[0:1, D_ST:2 * D_ST] = ci

    dr, di = in_r, in_i
    for i in range(n_blk):
        blk = (n_blk - 1 - i) if reverse else i
        r = slice(blk * SCAN_BLK, (blk + 1) * SCAN_BLK)
        dr, di = ar * dr - ai * di, ar * di + ai * dr
        xr, xi = buf_ref[r, 0:D_ST] + dr, buf_ref[r, D_ST:2 * D_ST] + di
        buf_ref[r, 0:D_ST] = xr
        buf_ref[r, D_ST:2 * D_ST] = xi
        if on_block is not None:
            on_block(blk, xr, xi)


def _lane_windows(n):
    lane = lax.broadcasted_iota(jnp.int32, (1, n), 1)
    return lane // (D_POOL // len(POOL_WINDOWS))


def _select_window(grp, s2, s4, s8, s16):
    return jnp.where(grp == 0, s2, jnp.where(grp == 1, s4, jnp.where(grp == 2, s8, s16)))


def _pool_fwd(pbuf_ref, zb, halo, tile_idx, tt):
    pbuf_ref[0:HALO, :] = halo
    pbuf_ref[HALO:HALO + tt, :] = zb
    x = pbuf_ref[...]
    s2 = x + pltpu.roll(x, 1, axis=0)
    s4 = s2 + pltpu.roll(s2, 2, axis=0)
    s8 = s4 + pltpu.roll(s4, 4, axis=0)
    s16 = s8 + pltpu.roll(s8, 8, axis=0)
    grp = _lane_windows(D_POOL)
    win = _select_window(grp, s2, s4, s8, s16)[HALO:HALO + tt, :]
    width = _select_window(grp, 2.0, 4.0, 8.0, 16.0).astype(F32)
    pos = (tile_idx * tt + 1 + lax.broadcasted_iota(jnp.int32, (tt, 1), 0)).astype(F32)
    cnt = jnp.minimum(pos, width)
    return win / cnt - zb, cnt


def _sgu_fwd(zu, zv, lng, lnb, wsm_ref, bsp, mix_ref, tt):
    u, tu = _gelu(zu)
    v, tv = _gelu(zv)
    mu = jnp.mean(v, axis=-1, keepdims=True)
    vc = v - mu
    rstd = lax.rsqrt(jnp.mean(vc * vc, axis=-1, keepdims=True) + EPS)
    vhat = vc * rstd
    vnb = (vhat * lng + lnb).astype(BF16)
    _head_mix(wsm_ref, vnb, mix_ref, tt, bsp)
    return u, tu, tv, vhat, rstd, vnb


def _chunk_pairs(tt):
    n_ch = tt // CHUNK
    return [list(range(c, min(c + 2, n_ch))) for c in range(0, n_ch, 2)]


def _head_cols(xb, chunks, h):
    return jnp.concatenate([xb[c * CHUNK:(c + 1) * CHUNK, h * HEAD_DIM:(h + 1) * HEAD_DIM] for c in chunks], axis=1)


def _head_mix(w_ref, xb, out_ref, tt, add=None):
    for chunks in _chunk_pairs(tt):
        per_head = [_nn(w_ref[h], _head_cols(xb, chunks, h)) for h in range(HEADS)]
        for k, c in enumerate(chunks):
            block = jnp.concatenate([r[:, k * HEAD_DIM:(k + 1) * HEAD_DIM] for r in per_head], axis=1)
            out_ref[c * CHUNK:(c + 1) * CHUNK, :] = block if add is None else block + add


def mixer_fwd(z, sc, bbd, cbd, dskip, wglu, bglu, wpool, pscale, lng, lnb, wsm, bsp, perm, name, exchange=None):
    T = z.shape[0]
    tt = TT_MIX
    n_tiles = T // tt

    def body(z_ref, sc_ref, bbd_ref, cbd_ref, dskip_ref, wglu_ref, bglu_ref, wpool_ref, pscale_ref, lng_ref, lnb_ref,
             wsm_ref, bsp_ref, p_ref, pt_ref, ycat_ref, hs_ref, ys_ref, carry_ref, halo_ref, pbuf_ref, mix_ref):
        i = pl.program_id(0)

        @pl.when(i == 0)
        def _():
            carry_ref[...] = jnp.zeros_like(carry_ref)
            halo_ref[...] = jnp.zeros_like(halo_ref)

        za = z_ref[:, 0:D_SSM]
        zb = z_ref[:, D_SSM:D_SSM + D_POOL]
        zu = z_ref[:, D_SSM + D_POOL:D_SSM + D_POOL + D_SGU]
        zv = z_ref[:, D_SSM + D_POOL + D_SGU:D_IN]
        p, pt = p_ref[...], pt_ref[...]
        za = _interleave_f32(p, za)
        _bd_expand(za.astype(BF16), bbd_ref, hs_ref)
        _scan_tile(hs_ref, sc_ref, carry_ref, tt // SCAN_BLK, reverse=False)
        y = _bd_contract(hs_ref[...].astype(BF16), cbd_ref) + dskip_ref[...] * za
        ys_ref[...] = y
        g, _ = _gelu(y)
        q = _nn(g.astype(BF16), wglu_ref[...]) + bglu_ref[...]
        ycat_ref[:, 0:D_SSM] = _nn(pt, (g * jax.nn.sigmoid(q)).astype(BF16)).astype(BF16)
        pooled, _ = _pool_fwd(pbuf_ref, zb, halo_ref[...], i, tt)
        halo_ref[...] = zb[tt - HALO:tt, :]
        ycat_ref[:, D_SSM:D_SSM + D_POOL] = (_nn(pooled.astype(BF16), wpool_ref[...]) * pscale_ref[...]).astype(BF16)
        u, _, _, _, _, _ = _sgu_fwd(zu, zv, lng_ref[...], lnb_ref[...], wsm_ref, bsp_ref[...], mix_ref, tt)
        ycat_ref[:, D_SSM + D_POOL:D] = (u * mix_ref[...]).astype(BF16)

    return _pallas(
        body, name=name, grid=(n_tiles,),
        in_specs=[_row(tt, D_IN), _full((32, D_ST)), _full((D_SSM, 2 * D_ST)), _full((D_SSM, 2 * D_ST)),
                  _full((1, D_SSM)), _full((D_SSM, D_SSM)), _full((1, D_SSM)), _full((D_POOL, D_POOL)),
                  _full((1, D_POOL)), _full((1, D_SGU)), _full((1, D_SGU)), _full((HEADS, CHUNK, CHUNK)),
                  _full((CHUNK, D_SGU)), _full((tt, tt)), _full((tt, tt))],
        out_specs=[_row(tt, D), _row(tt, 2 * D_ST), _row(tt, D_SSM)],
        out_shape=[S((T, D), BF16), S((T, 2 * D_ST), F32), S((T, D_SSM), F32)],
        scratch_shapes=[pltpu.VMEM((SCAN_BLK, 2 * D_ST), F32), pltpu.VMEM((HALO, D_POOL), F32),
                        pltpu.VMEM((tt + HALO, D_POOL), F32), pltpu.VMEM((tt, D_SGU), F32)],
        vmem_mb=48, operands=(z, sc, bbd, cbd, dskip, wglu, bglu, wpool, pscale, lng, lnb, wsm, bsp, *perm),
        exchange=exchange)


def mixer_bwd(dx1b, z, hs, ys, wout, sc, bbd, cbd, dskip, wglu, bglu, wpool, pscale, lng, lnb, wsm, wsmt, bsp, perm, name,
              exchange=None):
    T = z.shape[0]
    tt = TT_MIX
    n_tiles = T // tt

    def rev(i):
        return n_tiles - 1 - i

    def body(dx_ref, z_ref, zprev_ref, hs_ref, hsprev_ref, ys_ref, wout_ref, sc_ref, bbd_ref, cbd_ref, dskip_ref,
             wglu_ref, bglu_ref, wpool_ref, pscale_ref, lng_ref, lnb_ref, wsm_ref, wsmt_ref, bsp_ref, p_ref, pt_ref,
             dz_ref, o_da, o_dbt, o_dct, o_dd, o_dbglu, o_dwglu, o_dwpool, o_dpscale, o_dlng, o_dlnb, o_dws, o_dbsp,
             gbuf_ref, carry_ref, accb_ref, accc_ref, ehalo_ref, pbuf_ref, mix_ref, dvn_ref, accw_ref, accm_ref):
        i = pl.program_id(0)
        tile = rev(i)

        @pl.when(i == 0)
        def _():
            carry_ref[...] = jnp.zeros_like(carry_ref)
            accb_ref[...] = jnp.zeros_like(accb_ref)
            accc_ref[...] = jnp.zeros_like(accc_ref)
            for o in (o_da, o_dd, o_dbglu, o_dwglu):
                o[...] = jnp.zeros_like(o)

        p, pt = p_ref[...], pt_ref[...]
        dxb = dx_ref[...]
        d_a = _nt(_nn(p, dxb).astype(BF16), wout_ref[0:D_SSM, :])
        d_bc = _nt(dxb, wout_ref[D_SSM:D, :])
        za = _interleave_f32(p, z_ref[:, 0:D_SSM])
        first_tile = (tile > 0).astype(F32)

        y = ys_ref[...]
        g, tg = _gelu(y)
        gb = g.astype(BF16)
        sg = jax.nn.sigmoid(_nn(gb, wglu_ref[...]) + bglu_ref[...])
        dq = d_a * g * sg * (1.0 - sg)
        dqb = dq.astype(BF16)
        o_dbglu[...] += _rowsum(dq)
        o_dwglu[...] += _tn(gb, dqb)
        dy = (d_a * sg + _nt(dqb, wglu_ref[...])) * _gelu_grad(y, tg)
        o_dd[...] += _rowsum(dy * za)
        dyb = dy.astype(BF16)
        _bd_accumulate(accc_ref, dyb, hs_ref[...].astype(BF16))
        _bd_expand(dyb, cbd_ref, gbuf_ref)
        hprev = hsprev_ref[SCAN_BLK - 1:SCAN_BLK, :] * first_tile
        sub = lax.broadcasted_iota(jnp.int32, (SCAN_BLK, 1), 0)
        edge = jnp.where(sub == 0, hprev, pltpu.roll(hs_ref[tt - SCAN_BLK:tt, :], 1, axis=0))
        da = [jnp.zeros((SCAN_BLK, D_ST), F32), jnp.zeros((SCAN_BLK, D_ST), F32)]

        def da_terms(blk, gr, gi):
            before = edge if blk == 0 else hs_ref[(blk - 1) * SCAN_BLK:blk * SCAN_BLK, :]
            hr, hi = before[:, 0:D_ST], before[:, D_ST:]
            da[0] = da[0] + (gr * hr + gi * hi)
            da[1] = da[1] + (gi * hr - gr * hi)

        _scan_tile(gbuf_ref, sc_ref, carry_ref, tt // SCAN_BLK, reverse=True, on_block=da_terms)
        o_da[:, 0:D_ST] += _rowsum(da[0])
        o_da[:, D_ST:] += _rowsum(da[1])
        gtb = gbuf_ref[...].astype(BF16)
        _bd_accumulate(accb_ref, za.astype(BF16), gtb)
        dza = (dy * dskip_ref[...] + _bd_contract(gtb, bbd_ref)).astype(BF16)
        dz_ref[:, 0:D_SSM] = _nn(pt, dza).astype(BF16)
        _pool_sgu_bwd(i, n_tiles, tile, tt, d_bc, z_ref, zprev_ref, wpool_ref, pscale_ref, lng_ref, lnb_ref, wsm_ref,
                      wsmt_ref, bsp_ref, dz_ref, o_dwpool, o_dpscale, o_dlng, o_dlnb, o_dws, o_dbsp,
                      ehalo_ref, pbuf_ref, mix_ref, dvn_ref, accw_ref, accm_ref)

        @pl.when(i == n_tiles - 1)
        def _():
            mask = _group_mask(D_SSM, 2 * D_ST)
            for acc_ref, o_ref in ((accb_ref, o_dbt), (accc_ref, o_dct)):
                fold = jnp.zeros((GRP, 2 * D_ST), F32)
                for gidx in range(N_GRP):
                    rows = slice(gidx * GRP, (gidx + 1) * GRP)
                    fold = fold + jnp.where(mask[rows, :], acc_ref[rows, :], 0.0)
                o_ref[...] = fold

    def rowr(n):
        return pl.BlockSpec((tt, n), lambda i: (rev(i), 0))

    zprev_spec = pl.BlockSpec((HALO, D_IN), lambda i: (jnp.maximum(rev(i) * (tt // HALO) - 1, 0), 0))
    hsprev_spec = pl.BlockSpec((SCAN_BLK, 2 * D_ST), lambda i: (jnp.maximum(rev(i) * (tt // SCAN_BLK) - 1, 0), 0))
    small = [S((1, 2 * D_ST), F32), S((GRP, 2 * D_ST), F32), S((GRP, 2 * D_ST), F32), S((1, D_SSM), F32),
             S((1, D_SSM), F32), S((D_SSM, D_SSM), F32), S((D_POOL, D_POOL), F32), S((1, D_POOL), F32),
             S((1, D_SGU), F32), S((1, D_SGU), F32), S((HEADS, CHUNK, CHUNK), F32), S((CHUNK, 128), F32)]
    return _pallas(
        body, name=name, grid=(n_tiles,),
        in_specs=[rowr(D), rowr(D_IN), zprev_spec, rowr(2 * D_ST), hsprev_spec, rowr(D_SSM), _full((D, D)),
                  _full((32, D_ST)), _full((D_SSM, 2 * D_ST)), _full((D_SSM, 2 * D_ST)), _full((1, D_SSM)),
                  _full((D_SSM, D_SSM)), _full((1, D_SSM)), _full((D_POOL, D_POOL)), _full((1, D_POOL)),
                  _full((1, D_SGU)), _full((1, D_SGU)), _full((HEADS, CHUNK, CHUNK)), _full((HEADS, CHUNK, CHUNK)),
                  _full((CHUNK, D_SGU)), _full((tt, tt)), _full((tt, tt))],
        out_specs=[rowr(D_IN)] + [_full(s.shape) for s in small],
        out_shape=[S((T, D_IN), BF16)] + small,
        scratch_shapes=[pltpu.VMEM((tt, 2 * D_ST), F32), pltpu.VMEM((SCAN_BLK, 2 * D_ST), F32),
                        pltpu.VMEM((D_SSM, 2 * D_ST), F32), pltpu.VMEM((D_SSM, 2 * D_ST), F32),
                        pltpu.VMEM((HALO, D_POOL), F32), pltpu.VMEM((tt + HALO, D_POOL), F32),
                        pltpu.VMEM((tt, D_SGU), F32), pltpu.VMEM((tt, D_SGU), F32),
                        pltpu.VMEM((HEADS, CHUNK, CHUNK), F32), pltpu.VMEM((CHUNK, D_SGU), F32)],
        vmem_mb=56, exchange=exchange,
        operands=(dx1b, z, z, hs, hs, ys, wout, sc, bbd, cbd, dskip, wglu, bglu, wpool, pscale, lng, lnb, wsm, wsmt, bsp, *perm))


def inproj_fwd(x, g, w_t, name, exchange=None):
    T = x.shape[0]
    tt = min(TT_PROJ, T)

    def body(x_ref, g_ref, w_ref, h_ref, z_ref):
        xn, _ = _rms(x_ref[...])
        h = (xn * g_ref[...]).astype(BF16)
        h_ref[...] = h
        z_ref[...] = _nt(h, w_ref[...])

    return _pallas(
        body, name=name, grid=(T // tt,),
        in_specs=[_row(tt, D), _full((1, D)), _full((D_IN, D))],
        out_specs=[_row(tt, D), _row(tt, D_IN)],
        out_shape=[S((T, D), BF16), S((T, D_IN), F32)],
        scratch_shapes=[], vmem_mb=48, operands=(x, g, w_t), exchange=exchange)


def inproj_bwd(dzb, x, g, w_t, dx1):
    T = x.shape[0]
    tt = min(TT_PROJ, T)

    def body(dz_ref, x_ref, g_ref, w_ref, dx1_ref, dx_ref, dg_ref):
        @pl.when(pl.program_id(0) == 0)
        def _():
            dg_ref[...] = jnp.zeros_like(dg_ref)

        dh = _nn(dz_ref[...], w_ref[...])
        xn, r = _rms(x_ref[...])
        dg_ref[...] += _rowsum(dh * xn)
        dx_ref[...] = dx1_ref[...] + _rms_bwd(dh, xn, r, g_ref[...])

    return pl.pallas_call(
        body, name="inproj_bwd", grid=(T // tt,),
        in_specs=[_row(tt, D_IN), _row(tt, D), _full((1, D)), _full((D_IN, D)), _row(tt, D)],
        out_specs=[_row(tt, D), _full((1, D))],
        out_shape=[S((T, D), F32), S((1, D), F32)],
        compiler_params=_cp(48, 1),
    )(dzb, x, g, w_t, dx1)


def _load_weights(pairs, sem):
    @pl.when(pl.program_id(0) == 0)
    def _():
        copies = [pltpu.make_async_copy(src, dst, sem.at[k]) for k, (src, dst) in enumerate(pairs)]
        for cp in copies:
            cp.start()
        for cp in copies:
            cp.wait()


def ffn_fwd(x, ycat, wout, g, wg_t, wu_t, wd, name, exchange=None, head=None):
    T = x.shape[0]
    tt = TT_FFN
    any_spec = pl.BlockSpec(memory_space=pl.ANY)

    def body(*refs):
        if head is None:
            (x_ref, ycat_ref, g_ref, wout_hbm, wg_hbm, wu_hbm, wd_hbm,
             x1_ref, h_ref, gate_ref, up_ref, act_ref, x2_ref, wout_v, wg_v, wu_v, wd_v, sem) = refs
        else:
            (x_ref, ycat_ref, g_ref, t_ref, gf_ref, wout_hbm, wg_hbm, wu_hbm, wd_hbm,
             x1_ref, h_ref, gate_ref, up_ref, act_ref, x2_ref, lvec_ref, dgf_ref, wout_v, wg_v, wu_v, wd_v, sem) = refs
        _ffn_fwd_tile(x_ref, ycat_ref, g_ref, wout_hbm, wg_hbm, wu_hbm, wd_hbm, x1_ref, h_ref, gate_ref, up_ref, act_ref,
                      x2_ref, wout_v, wg_v, wu_v, wd_v, sem)
        if head is not None:
            @pl.when(pl.program_id(0) == 0)
            def _():
                lvec_ref[...] = jnp.zeros_like(lvec_ref)
                dgf_ref[...] = jnp.zeros_like(dgf_ref)

            xn, r = _rms(x2_ref[...])
            gf = gf_ref[...]
            err = xn * gf - t_ref[...]
            lvec_ref[...] += _rowsum(err * err)
            dy = err * (1.0 / D)
            dgf_ref[...] += _rowsum(dy * xn)
            x2_ref[...] = _rms_bwd(dy, xn, r, gf)

    def _ffn_fwd_tile(x_ref, ycat_ref, g_ref, wout_hbm, wg_hbm, wu_hbm, wd_hbm,
                      x1_ref, h_ref, gate_ref, up_ref, act_ref, x2_ref, wout_v, wg_v, wu_v, wd_v, sem):
        _load_weights([(wout_hbm, wout_v), (wg_hbm, wg_v), (wu_hbm, wu_v), (wd_hbm, wd_v)], sem)
        x1 = x_ref[...] + _nn(ycat_ref[...], wout_v[...])
        x1_ref[...] = x1
        xn, _ = _rms(x1)
        h = (xn * g_ref[...]).astype(BF16)
        h_ref[...] = h
        gate = _nt(h, wg_v[...])
        up = _nt(h, wu_v[...])
        gate_ref[...] = gate.astype(BF16)
        up_ref[...] = up.astype(BF16)
        act = (gate * jax.nn.sigmoid(gate) * up).astype(BF16)
        act_ref[...] = act
        x2_ref[...] = x1 + _nn(act, wd_v[...])

    with_head = head is not None
    return _pallas(
        body, name=name, grid=(T // tt,),
        in_specs=[_row(tt, D), _row(tt, D), _full((1, D))] + ([_row(tt, D), _full((1, D))] if with_head else [])
        + [any_spec, any_spec, any_spec, any_spec],
        out_specs=[_row(tt, D), _row(tt, D), _row(tt, D_FF), _row(tt, D_FF), _row(tt, D_FF), _row(tt, D)]
        + ([_full((1, D)), _full((1, D))] if with_head else []),
        out_shape=[S((T, D), F32), S((T, D), BF16), S((T, D_FF), BF16), S((T, D_FF), BF16), S((T, D_FF), BF16),
                   S((T, D), F32)] + ([S((1, D), F32), S((1, D), F32)] if with_head else []),
        scratch_shapes=[pltpu.VMEM((D, D), BF16), pltpu.VMEM((D_FF, D), BF16), pltpu.VMEM((D_FF, D), BF16),
                        pltpu.VMEM((D_FF, D), BF16), pltpu.SemaphoreType.DMA((4,))],
        vmem_mb=56, operands=(x, ycat, g) + (tuple(head) if with_head else ()) + (wout, wg_t, wu_t, wd), exchange=exchange)


def _pool_sgu_bwd(i, n_tiles, tile, tt, d_bc, z_ref, zprev_ref, wpool_ref, pscale_ref, lng_ref, lnb_ref, wsm_ref, wsmt_ref,
                  bsp_ref, dz_ref, o_dwpool, o_dpscale, o_dlng, o_dlnb, o_dws, o_dbsp,
                  ehalo_ref, pbuf_ref, mix_ref, dvn_ref, accw_ref, accm_ref):
    @pl.when(i == 0)
    def _():
        ehalo_ref[...] = jnp.zeros_like(ehalo_ref)
        accw_ref[...] = jnp.zeros_like(accw_ref)
        accm_ref[...] = jnp.zeros_like(accm_ref)
        for o in (o_dwpool, o_dpscale, o_dlng, o_dlnb):
            o[...] = jnp.zeros_like(o)

    d_b = d_bc[:, 0:D_POOL]
    d_c = d_bc[:, D_POOL:D_POOL + D_SGU]
    zb = z_ref[:, D_SSM:D_SSM + D_POOL]
    zu = z_ref[:, D_SSM + D_POOL:D_SSM + D_POOL + D_SGU]
    zv = z_ref[:, D_SSM + D_POOL + D_SGU:D_IN]
    not_first = (tile > 0).astype(F32)

    pooled, cnt = _pool_fwd(pbuf_ref, zb, zprev_ref[:, D_SSM:D_SSM + D_POOL] * not_first, tile, tt)
    pooledb = pooled.astype(BF16)
    mixed = _nn(pooledb, wpool_ref[...])
    o_dpscale[...] += _rowsum(d_b * mixed)
    dmixb = (d_b * pscale_ref[...]).astype(BF16)
    o_dwpool[...] += _tn(pooledb, dmixb)
    dpooled = _nt(dmixb, wpool_ref[...])
    e = dpooled / cnt
    pbuf_ref[0:tt, :] = e
    pbuf_ref[tt:tt + HALO, :] = ehalo_ref[...]
    ehalo_ref[...] = e[0:HALO, :]
    x = pbuf_ref[...]
    n = tt + HALO
    f2 = x + pltpu.roll(x, n - 1, axis=0)
    f4 = f2 + pltpu.roll(f2, n - 2, axis=0)
    f8 = f4 + pltpu.roll(f4, n - 4, axis=0)
    f16 = f8 + pltpu.roll(f8, n - 8, axis=0)
    fwd_sum = _select_window(_lane_windows(D_POOL), f2, f4, f8, f16)[0:tt, :]
    dz_ref[:, D_SSM:D_SSM + D_POOL] = (fwd_sum - dpooled).astype(BF16)

    lng = lng_ref[...]
    u, su, sv, vhat, rstd, vnb = _sgu_fwd(zu, zv, lng, lnb_ref[...], wsm_ref, bsp_ref[...], mix_ref, tt)
    dz_ref[:, D_SSM + D_POOL:D_SSM + D_POOL + D_SGU] = (d_c * mix_ref[...] * _gelu_grad(zu, su)).astype(BF16)
    dmix = d_c * u
    dmixb2 = dmix.astype(BF16)
    for c in range(tt // CHUNK):
        accm_ref[...] += dmix[c * CHUNK:(c + 1) * CHUNK, :]
    for chunks in _chunk_pairs(tt):
        for h in range(HEADS):
            accw_ref[h] += _nt(_head_cols(dmixb2, chunks, h), _head_cols(vnb, chunks, h))
    _head_mix(wsmt_ref, dmixb2, dvn_ref, tt)
    dvn = dvn_ref[...]
    o_dlng[...] += _rowsum(dvn * vhat)
    o_dlnb[...] += _rowsum(dvn)
    dvh = dvn * lng
    dv = rstd * (dvh - jnp.mean(dvh, axis=-1, keepdims=True) - vhat * jnp.mean(dvh * vhat, axis=-1, keepdims=True))
    dz_ref[:, D_SSM + D_POOL + D_SGU:D_IN] = (dv * _gelu_grad(zv, sv)).astype(BF16)

    @pl.when(i == n_tiles - 1)
    def _():
        tri = (lax.broadcasted_iota(jnp.int32, (CHUNK, CHUNK), 0) >= lax.broadcasted_iota(jnp.int32, (CHUNK, CHUNK), 1))
        for h in range(HEADS):
            o_dws[h] = jnp.where(tri, accw_ref[h], 0.0)
        lane = lax.broadcasted_iota(jnp.int32, (1, 128), 1)
        acc = jnp.zeros((CHUNK, 128), F32)
        for h in range(HEADS):
            sh = jnp.sum(accm_ref[:, h * HEAD_DIM:(h + 1) * HEAD_DIM], axis=1, keepdims=True)
            acc = jnp.where(lane == h, sh, acc)
        o_dbsp[...] = acc


def ffn_bwd(dx2, x1, gate, up, g, wg_t, wu_t, wd, name, exchange=None):
    T = x1.shape[0]
    tt = TT_FFN
    any_spec = pl.BlockSpec(memory_space=pl.ANY)

    def body(dx2_ref, x1_ref, gate_ref, up_ref, g_ref, wg_hbm, wu_hbm, wd_hbm,
             dgu_ref, dx2b_ref, dx1_ref, dx1b_ref, dg_ref, wg_v, wu_v, wd_v, sem):
        _load_weights([(wg_hbm, wg_v), (wu_hbm, wu_v), (wd_hbm, wd_v)], sem)

        @pl.when(pl.program_id(0) == 0)
        def _():
            dg_ref[...] = jnp.zeros_like(dg_ref)

        dx2 = dx2_ref[...]
        dx2b = dx2.astype(BF16)
        dx2b_ref[...] = dx2b
        dact = _nt(dx2b, wd_v[...])
        gate = gate_ref[...].astype(F32)
        up = up_ref[...].astype(F32)
        sg = jax.nn.sigmoid(gate)
        dgate = (dact * up * (sg * (1.0 + gate * (1.0 - sg)))).astype(BF16)
        dup = (dact * gate * sg).astype(BF16)
        dgu_ref[:, 0:D_FF] = dgate
        dgu_ref[:, D_FF:2 * D_FF] = dup
        dh = _nn(dgate, wg_v[...]) + _nn(dup, wu_v[...])
        xn, r = _rms(x1_ref[...])
        dg_ref[...] += _rowsum(dh * xn)
        dx1 = dx2 + _rms_bwd(dh, xn, r, g_ref[...])
        dx1_ref[...] = dx1
        dx1b_ref[...] = dx1.astype(BF16)

    return _pallas(
        body, name=name, grid=(T // tt,),
        in_specs=[_row(tt, D), _row(tt, D), _row(tt, D_FF), _row(tt, D_FF), _full((1, D)), any_spec, any_spec, any_spec],
        out_specs=[_row(tt, 2 * D_FF), _row(tt, D), _row(tt, D), _row(tt, D), _full((1, D))],
        out_shape=[S((T, 2 * D_FF), BF16), S((T, D), BF16), S((T, D), F32), S((T, D), BF16), S((1, D), F32)],
        scratch_shapes=[pltpu.VMEM((D_FF, D), BF16), pltpu.VMEM((D_FF, D), BF16), pltpu.VMEM((D_FF, D), BF16),
                        pltpu.SemaphoreType.DMA((3,))],
        vmem_mb=56, operands=(dx2, x1, gate, up, g, wg_t, wu_t, wd), exchange=exchange)


def wgrad(a, b, tm, name, exchange=None, tk=TK_WGRAD):
    T, M = a.shape
    N = b.shape[1]
    tk = min(tk, T)
    n_k = T // tk

    def body(a_ref, b_ref, o_ref, acc_ref):
        k = pl.program_id(1)

        @pl.when(k == 0)
        def _():
            acc_ref[...] = jnp.zeros_like(acc_ref)

        acc_ref[...] += _tn(a_ref[...], b_ref[...])

        @pl.when(k == n_k - 1)
        def _():
            o_ref[...] = acc_ref[...].astype(BF16)

    (out,), got = _pallas(
        body, name=name, grid=(M // tm, n_k),
        in_specs=[pl.BlockSpec((tk, tm), lambda m, k: (k, m)), pl.BlockSpec((tk, N), lambda m, k: (k, 0))],
        out_specs=[pl.BlockSpec((tm, N), lambda m, k: (m, 0))],
        out_shape=[S((M, N), BF16)],
        scratch_shapes=[pltpu.VMEM((tm, N), F32)],
        vmem_mb=48, operands=(a, b), exchange=exchange)
    return out if exchange is None else (out, got)


def _mesh_place():
    x, y, c = lax.axis_index("x"), lax.axis_index("y"), lax.axis_index("c")
    return x, y, c, 4 * x + 2 * y + c


def _peer(x, y, c, k):
    px = 1 - x if k & 4 else x
    py = 1 - y if k & 2 else y
    pc = 1 - c if k & 1 else c
    return (px, py, pc), 4 * px + 2 * py + pc


class _Exchange:
    SAME_CORE = (2, 4, 6)

    def __init__(self, gather=(), scatter=()):
        self.entries = [(a, None, a.shape[0]) for a in gather] + [(a, off, rows) for a, off, rows in scatter]
        self.n_gather = len(gather)

    @property
    def n(self):
        return len(self.entries)

    def operands(self):
        return [e[0] for e in self.entries]

    def out_shapes(self):
        return [S((N_DEV, rows, a.shape[1]), a.dtype) for a, _, rows in self.entries]

    def sems(self):
        return [pltpu.SemaphoreType.DMA((self.n, N_DEV)), pltpu.SemaphoreType.DMA((self.n, N_DEV)),
                pltpu.SemaphoreType.DMA((self.n,))]

    def _src(self, ref, e, idx):
        _, off, rows = self.entries[e]
        if off is None:
            return ref
        return ref.at[pl.ds(pl.multiple_of(off + idx * rows, 16), rows)]

    def _masks(self, e):
        return (1,) + self.SAME_CORE if e < self.n_gather else tuple(range(1, N_DEV))

    def _copy(self, ins, outs, sems, e, k, sending, passing_on=False):
        send_sems, recv_sems, _ = sems
        x, y, c, me = _mesh_place()
        peer, pidx = _peer(x, y, c, k)
        if passing_on:
            return pltpu.make_async_remote_copy(
                src_ref=outs[e].at[pidx], dst_ref=outs[e].at[pidx], send_sem=send_sems.at[e, k | 1],
                recv_sem=recv_sems.at[e, k | 1], device_id=_peer(x, y, c, 1)[0], device_id_type=pl.DeviceIdType.MESH)
        return pltpu.make_async_remote_copy(
            src_ref=self._src(ins[e], e, pidx), dst_ref=outs[e].at[me if sending else pidx], send_sem=send_sems.at[e, k],
            recv_sem=recv_sems.at[e, k], device_id=peer, device_id_type=pl.DeviceIdType.MESH)

    def _local(self, ins, outs, sems):
        me = _mesh_place()[3]
        return [pltpu.make_async_copy(self._src(ins[e], e, me), outs[e].at[me], sems[2].at[e]) for e in range(self.n)]

    def start(self, ins, outs, sems):
        for cp in self._local(ins, outs, sems):
            cp.start()
        for k in range(1, N_DEV):
            for e in range(self.n):
                if k in self._masks(e):
                    self._copy(ins, outs, sems, e, k, True).start()

    def forward(self, ins, outs, sems):
        for k in self.SAME_CORE:
            for e in range(self.n_gather):
                self._copy(ins, outs, sems, e, k, False).wait_recv()
                self._copy(ins, outs, sems, e, k, False, passing_on=True).start()

    def wait(self, ins, outs, sems):
        for k in range(1, N_DEV):
            for e in range(self.n):
                if e >= self.n_gather or k % 2:
                    self._copy(ins, outs, sems, e, k, False).wait_recv()
        for k in range(1, N_DEV):
            for e in range(self.n):
                self._copy(ins, outs, sems, e, k, True).wait_send()
        for cp in self._local(ins, outs, sems):
            cp.wait()


def _pallas(body, *, name, grid, in_specs, out_specs, out_shape, scratch_shapes, vmem_mb, operands, exchange=None,
            aliases=None):
    n_in, n_out, n_scr = len(in_specs), len(out_specs), len(scratch_shapes)
    n_steps = math.prod(grid)
    aliases = aliases or {}
    if exchange is None:
        res = pl.pallas_call(body, name=name, grid=grid, in_specs=in_specs, out_specs=out_specs, out_shape=out_shape,
                             scratch_shapes=scratch_shapes, input_output_aliases=aliases,
                             compiler_params=_cp(vmem_mb, len(grid)))(*operands)
        return list(res), []
    ex = exchange

    def hosted(*refs):
        ins, ex_in = refs[:n_in], refs[n_in:n_in + ex.n]
        outs = refs[n_in + ex.n:n_in + ex.n + n_out]
        ex_out = refs[n_in + ex.n + n_out:n_in + 2 * ex.n + n_out]
        scr = refs[n_in + 2 * ex.n + n_out:]
        sems = scr[n_scr:]
        step = pl.program_id(0)
        for axis in range(1, len(grid)):
            step = step * grid[axis] + pl.program_id(axis)

        @pl.when(step == 0)
        def _():
            ex.start(ex_in, ex_out, sems)

        body(*ins, *outs, *scr[:n_scr])

        if ex.n_gather:
            @pl.when(step == max(n_steps - 1 - max(2, n_steps // 8), 0))
            def _():
                ex.forward(ex_in, ex_out, sems)

        @pl.when(step == n_steps - 1)
        def _():
            ex.wait(ex_in, ex_out, sems)

    any_spec = pl.BlockSpec(memory_space=pl.ANY)
    res = pl.pallas_call(
        hosted, name=name, grid=grid, in_specs=list(in_specs) + [any_spec] * ex.n,
        out_specs=list(out_specs) + [any_spec] * ex.n, out_shape=list(out_shape) + ex.out_shapes(),
        scratch_shapes=list(scratch_shapes) + ex.sems(), input_output_aliases=aliases,
        compiler_params=_cp(vmem_mb, len(grid)),
    )(*operands, *ex.operands())
    return list(res[:n_out]), list(res[n_out:])


def exchange_only(ex, name):
    def body(*refs):
        ins, outs, sems = refs[:ex.n], refs[ex.n:2 * ex.n], refs[2 * ex.n:]
        ex.start(ins, outs, sems)
        ex.forward(ins, outs, sems)
        ex.wait(ins, outs, sems)

    any_spec = pl.BlockSpec(memory_space=pl.ANY)
    return list(pl.pallas_call(body, name=name, in_specs=[any_spec] * ex.n, out_specs=[any_spec] * ex.n,
                               out_shape=ex.out_shapes(), scratch_shapes=ex.sems())(*ex.operands()))


def _adamw(w, g, m, v):
    m = ADAM_B1 * m + (1.0 - ADAM_B1) * g
    v = ADAM_B2 * v + (1.0 - ADAM_B2) * (g * g)
    m_hat = m / (1.0 - ADAM_B1 ** ADAM_STEP)
    v_hat = v / (1.0 - ADAM_B2 ** ADAM_STEP)
    delta = -ADAM_LR * (m_hat / (jnp.sqrt(v_hat) + ADAM_EPS) + ADAM_WD * w)
    return delta, m, v


def _sum_parts(p_ref, rows=slice(None)):
    g = p_ref[0, rows].astype(F32)
    for k in range(1, N_DEV):
        g = g + p_ref[k, rows].astype(F32)
    return g


def adamw_layers(parts, w, m, v, name):
    n_l = len(parts)

    def body(*refs):
        p_refs = refs[:n_l]
        w_ref, m_ref, v_ref, g_out, d_out, m_out, v_out = refs[n_l:]
        for l in range(n_l):
            g = _sum_parts(p_refs[l])
            g_out[l] = g
            d_out[l], m_out[l], v_out[l] = _adamw(w_ref[l], g, m_ref[l], v_ref[l])

    return pl.pallas_call(
        body, name=name, out_shape=[S(w.shape, F32)] * 4, compiler_params=_cp(48),
    )(*parts, w, m, v)


def adamw_segments(parts, segments, w, m, v, name):
    n_p = len(parts)

    def body(*refs):
        p_refs = refs[:n_p]
        w_ref, m_ref, v_ref, g_out, d_out, m_out, v_out = refs[n_p:]
        for part, src, dst, rows in segments:
            g = _sum_parts(p_refs[part], slice(src, src + rows))
            to = slice(dst, dst + rows)
            g_out[to] = g
            d_out[to], m_out[to], v_out[to] = _adamw(w_ref[to], g, m_ref[to], v_ref[to])

    return pl.pallas_call(
        body, name=name, out_shape=[S(w.shape, F32)] * 4, compiler_params=_cp(48),
    )(*parts, w, m, v)


SMALL_LAYER = ("g_mix", "A_re", "A_im", "log_dt", "B_re", "B_im", "C_re", "C_im", "D_skip", "b_glu", "w_pool",
               "pool_scale", "sgu_ln_g", "sgu_ln_b", "w_spatial", "b_spatial", "g_ffn")
BIG_NAMES = ("w_in", "w_glu", "w_out", "w_gate", "w_up", "w_down")
COLUMN_SHARDED = ("w_in", "w_gate", "w_up")
WEIGHT_ORDER = ("g_mix", "w_in", "A_re", "A_im", "log_dt", "B_re", "B_im", "C_re", "C_im", "D_skip", "w_glu", "b_glu",
                "w_pool", "pool_scale", "sgu_ln_g", "sgu_ln_b", "w_spatial", "b_spatial", "w_out", "g_ffn", "w_gate",
                "w_up", "w_down", "g_final")
SEG = 1024
SMALL_AS_BF16 = ("B_re", "B_im", "C_re", "C_im", "w_pool", "w_spatial")


def _pack(arrays, dtype=F32):
    seg = SEG * 4 // jnp.dtype(dtype).itemsize
    parts = []
    for a in arrays:
        flat = a.reshape(-1).astype(dtype)
        parts.append(jnp.pad(flat, (0, (-flat.shape[0]) % seg)))
    return jnp.concatenate(parts).reshape(-1, 128)


def _state_rows(p):
    return p.reshape(1, D_ST)


def _chan_by_state(p):
    return jnp.transpose(p, (2, 0, 1)).reshape(GRP, D_ST)


def _chan_by_state_c(p):
    return jnp.transpose(p, (1, 0, 2)).reshape(GRP, D_ST)


def kernel(x, g_mix, w_in, A_re, A_im, log_dt, B_re, B_im, C_re, C_im, D_skip, w_glu, b_glu, w_pool, pool_scale, sgu_ln_g, sgu_ln_b, w_spatial, b_spatial, w_out, g_ffn, w_gate, w_up, w_down, g_final, loss_target, m_g_mix, m_w_in, m_A_re, m_A_im, m_log_dt, m_B_re, m_B_im, m_C_re, m_C_im, m_D_skip, m_w_glu, m_b_glu, m_w_pool, m_pool_scale, m_sgu_ln_g, m_sgu_ln_b, m_w_spatial, m_b_spatial, m_w_out, m_g_ffn, m_w_gate, m_w_up, m_w_down, m_g_final, v_g_mix, v_w_in, v_A_re, v_A_im, v_log_dt, v_B_re, v_B_im, v_C_re, v_C_im, v_D_skip, v_w_glu, v_b_glu, v_w_pool, v_pool_scale, v_sgu_ln_g, v_sgu_ln_b, v_w_spatial, v_b_spatial, v_w_out, v_g_ffn, v_w_gate, v_w_up, v_w_down, v_g_final):
    args = dict(locals())
    W = {n: args[n] for n in WEIGHT_ORDER}
    M = {n: args["m_" + n] for n in WEIGHT_ORDER}
    V = {n: args["v_" + n] for n in WEIGHT_ORDER}
    n_layers = g_mix.shape[0]
    x0 = x[0]
    target = loss_target[0]

    def my_rows(name, l):
        w = W[name][l]
        return (w.T if name in COLUMN_SHARDED else w).astype(BF16)

    full_w = [dict() for _ in range(n_layers)]

    def gather_of(*which):
        return _Exchange(gather=[my_rows(n, l) for n, l in which])

    def keep_gathered(which, arrays):
        for (n, l), a in zip(which, arrays):
            full_w[l][n] = a.reshape(-1, a.shape[-1])

    tri = jnp.tril(jnp.ones((CHUNK, CHUNK), bool))
    perm = _interleave_matrices(TT_MIX)
    consts = []
    for l in range(n_layers):
        a_re, a_im = _state_rows(A_re[l]), _state_rows(A_im[l])
        ldt = jnp.repeat(log_dt[l], N_STATE).reshape(1, D_ST)
        b_re_t, b_im_t = _chan_by_state(B_re[l]), _chan_by_state(B_im[l])
        sc, bbd, cbd = s5_prepare(a_re, a_im, ldt, b_re_t, b_im_t, _chan_by_state_c(C_re[l]), _chan_by_state_c(C_im[l]))
        wsm = jnp.where(tri[None], w_spatial[l], 0.0)
        wpool_bd = jnp.zeros((D_POOL, D_POOL), F32)
        for gi in range(len(POOL_WINDOWS)):
            wpool_bd = wpool_bd.at[gi * 64:(gi + 1) * 64, gi * 64:(gi + 1) * 64].set(w_pool[l, gi])
        consts.append(dict(
            disc=(a_re, a_im, ldt, b_re_t, b_im_t), sc=sc, bbd=bbd, cbd=cbd,
            dskip=D_skip[l].reshape(1, D_SSM), bglu=b_glu[l].reshape(1, D_SSM),
            wpool=wpool_bd.astype(BF16), pscale=pool_scale[l].reshape(1, D_POOL),
            lng=sgu_ln_g[l].reshape(1, D_SGU), lnb=sgu_ln_b[l].reshape(1, D_SGU),
            wsm=wsm.astype(BF16), wsmt=jnp.transpose(wsm, (0, 2, 1)).astype(BF16),
            bsp=jnp.repeat(b_spatial[l].T, HEAD_DIM, axis=1),
            gmix=g_mix[l].reshape(1, D), gffn=g_ffn[l].reshape(1, D)))

    def mixer_args(l):
        c = consts[l]
        return (c["bbd"], c["cbd"], c["dskip"], full_w[l]["w_glu"], c["bglu"], c["wpool"], c["pscale"], c["lng"], c["lnb"])

    first_needed = [("w_in", 0)]
    keep_gathered(first_needed, exchange_only(gather_of(*first_needed), "gather_first"))
    carried_fwd = {
        ("inproj", 0): [("w_glu", 0), ("w_out", 0)],
        ("mixer", 0): [("w_gate", 0), ("w_up", 0), ("w_down", 0)],
        ("ffn", 0): [("w_in", 1), ("w_glu", 1), ("w_out", 1), ("w_gate", 1)],
        ("mixer", 1): [("w_up", 1), ("w_down", 1)],
    }

    def carried(kind, l):
        which = carried_fwd.get((kind, l))
        return which, (gather_of(*which) if which else None)

    saved = []
    xl = x0
    for l in range(n_layers):
        c, fw = consts[l], full_w[l]
        which, ex = carried("inproj", l)
        (h, z), got = inproj_fwd(xl, c["gmix"], fw["w_in"], f"inproj_fwd_{l}", ex)
        keep_gathered(which or [], got)
        which, ex = carried("mixer", l)
        (ycat, hs, ys), got = mixer_fwd(z, c["sc"], *mixer_args(l), c["wsm"], c["bsp"], perm, f"mixer_fwd_{l}", ex)
        keep_gathered(which or [], got)
        which, ex = carried("ffn", l)
        head = (target, g_final.reshape(1, D)) if l == n_layers - 1 else None
        res, got = ffn_fwd(xl, ycat, fw["w_out"], c["gffn"], fw["w_gate"], fw["w_up"], fw["w_down"], f"ffn_fwd_{l}", ex, head)
        keep_gathered(which or [], got)
        x1, h2, gate, up, act, x2 = res[:6]
        saved.append(dict(x=xl, h=h, z=z, ycat=ycat, hs=hs, ys=ys, x1=x1, h2=h2, gate=gate, up=up, act=act))
        xl = x2
    dx, loss_vec, d_gfinal = xl, res[6], res[7]

    recv_big = {}
    recv_small = [None] * (2 * n_layers)

    def keep_received(which, arrays):
        for key, a in zip(which, arrays):
            if key[0] == "small":
                recv_small[key[1]] = a
            elif key[0] == "small16":
                recv_small[n_layers + key[1]] = a
            else:
                recv_big[key] = a

    pending = None
    for l in reversed(range(n_layers)):
        c, fw, sv = consts[l], full_w[l], saved[l]
        (dgu, dx2b, dx1, dx1b, d_gffn), got = ffn_bwd(dx, sv["x1"], sv["gate"], sv["up"], c["gffn"], fw["w_gate"], fw["w_up"],
                                                     fw["w_down"], f"ffn_bwd_{l}", pending[1] if pending else None)
        if pending:
            keep_received(pending[0], got)
        g_gu = wgrad(dgu, sv["h2"], D_FF // 2, f"wgrad_gate_up_{l}")
        g_down = wgrad(sv["act"], dx2b, D_FF // 2, f"wgrad_down_{l}")
        g_out = wgrad(sv["ycat"], dx1b, D, f"wgrad_out_{l}")
        ffn_rows = D_FF // N_DEV
        ex = _Exchange(scatter=[(g_gu, 0, ffn_rows), (g_gu, D_FF, ffn_rows), (g_down, 0, ffn_rows), (g_out, 0, D // N_DEV)])
        (dzb, da, dbt, dct, dd, dbglu, dwglu, dwpool, dpscale, dlng, dlnb, dws, dbsp), got = mixer_bwd(
            dx1b, sv["z"], sv["hs"], sv["ys"], fw["w_out"], c["sc"], *mixer_args(l), c["wsm"], c["wsmt"], c["bsp"],
            perm, f"mixer_bwd_{l}", ex)
        keep_received([("w_gate", l), ("w_up", l), ("w_down", l), ("w_out", l)], got)
        dx, d_gmix = inproj_bwd(dzb, sv["x"], c["gmix"], fw["w_in"], dx1)
        d_are, d_aim, d_ldt, d_bre_t, d_bim_t = s5_param_bwd(*c["disc"], da, dbt)
        small = dict(
            g_mix=d_gmix.reshape(D), A_re=d_are.reshape(N_GRP, N_STATE), A_im=d_aim.reshape(N_GRP, N_STATE),
            log_dt=d_ldt[0, :N_GRP],
            B_re=jnp.transpose(d_bre_t.reshape(GRP, N_GRP, N_STATE), (1, 0, 2)),
            B_im=jnp.transpose(d_bim_t.reshape(GRP, N_GRP, N_STATE), (1, 0, 2)),
            C_re=jnp.transpose(dct[:, :D_ST].reshape(GRP, N_GRP, N_STATE), (1, 0, 2)),
            C_im=-jnp.transpose(dct[:, D_ST:].reshape(GRP, N_GRP, N_STATE), (1, 0, 2)),
            D_skip=dd.reshape(D_SSM), b_glu=dbglu.reshape(D_SSM),
            w_pool=jnp.stack([dwpool[gi * 64:(gi + 1) * 64, gi * 64:(gi + 1) * 64] for gi in range(len(POOL_WINDOWS))]),
            pool_scale=dpscale.reshape(D_POOL), sgu_ln_g=dlng.reshape(D_SGU), sgu_ln_b=dlnb.reshape(D_SGU),
            w_spatial=dws, b_spatial=dbsp[:, :HEADS].T, g_ffn=d_gffn.reshape(D))
        packed = [small[n] for n in SMALL_LAYER if n not in SMALL_AS_BF16]
        packed += [d_gfinal.reshape(D), loss_vec.reshape(D)] if l == 0 else []
        packed16 = [small[n] for n in SMALL_LAYER if n in SMALL_AS_BF16]
        small_entries = dict(gather=[_pack(packed), _pack(packed16, BF16)], scatter=[(dwglu.astype(BF16), 0, D_SSM // N_DEV)])
        if l > 0:
            g_in = wgrad(dzb, sv["h"], D_IN, f"wgrad_in_{l}")
            small_entries["scatter"].append((g_in, 0, D_IN // N_DEV))
            pending = ([("small", l), ("small16", l), ("w_glu", l), ("w_in", l)], _Exchange(**small_entries))
        else:
            g_in, got = wgrad(dzb, sv["h"], D_IN, f"wgrad_in_{l}", tk=TK_WGRAD // 4, exchange=_Exchange(**small_entries))
            keep_received([("small", l), ("small16", l), ("w_glu", l)], got)
            pending = ([("w_in", l)], _Exchange(scatter=[(g_in, 0, D_IN // N_DEV)]))
    grad_x = dx
    keep_received(pending[0], exchange_only(pending[1], "exchange_last"))

    out = {}
    for n in BIG_NAMES:
        tr = (lambda a: jnp.transpose(a, (0, 2, 1))) if n in COLUMN_SHARDED else (lambda a: a)
        res = adamw_layers([recv_big[(n, l)] for l in range(n_layers)], tr(W[n]), tr(M[n]), tr(V[n]), f"adamw_{n}")
        out[n] = [tr(r) for r in res]

    seg_rows = [(-(-math.prod(W[n].shape[1:]) // SEG)) * (SEG // 128) for n in SMALL_LAYER]
    segments, src, src16, dst = [], 0, 0, 0
    for n, rows in zip(SMALL_LAYER, seg_rows):
        if n in SMALL_AS_BF16:
            segments += [(n_layers + l, src16, dst + l * rows, rows) for l in range(n_layers)]
            src16 += rows
        else:
            segments += [(l, src, dst + l * rows, rows) for l in range(n_layers)]
            src += rows
        dst += n_layers * rows
    tile_rows = SEG // 128
    segments += [(0, src, dst, tile_rows), (0, src + tile_rows, dst + tile_rows, tile_rows)]

    channel_major = ("B_re", "B_im")

    def pack_params(P):
        parts = []
        for n, rows in zip(SMALL_LAYER, seg_rows):
            flat = (jnp.swapaxes(P[n], -1, -2) if n in channel_major else P[n]).reshape(n_layers, -1)
            parts.append(jnp.pad(flat, ((0, 0), (0, rows * 128 - flat.shape[1]))).reshape(-1))
        return jnp.concatenate(parts + [P["g_final"], jnp.zeros((SEG,), F32)]).reshape(-1, 128)

    res = adamw_segments(recv_small, segments, pack_params(W), pack_params(M), pack_params(V), "adamw_small")
    for j in range(4):
        flat, off = res[j].reshape(-1), 0
        for n, rows in zip(SMALL_LAYER, seg_rows):
            size = math.prod(W[n].shape[1:])
            piece = flat[off:off + n_layers * rows * 128].reshape(n_layers, rows * 128)[:, :size]
            if n in channel_major:
                piece = jnp.swapaxes(piece.reshape(W[n].shape[:-2] + W[n].shape[:-3:-1]), -1, -2)
            else:
                piece = piece.reshape(W[n].shape)
            out.setdefault(n, []).append(piece)
            off += n_layers * rows * 128
        out.setdefault("g_final", []).append(flat[off:off + D])
        if j == 0:
            loss = (0.5 / D) * jnp.sum(flat[off + SEG:off + SEG + D])

    return (loss, grad_x[None], *[out[n][0] for n in WEIGHT_ORDER], *[out[n][1] for n in WEIGHT_ORDER],
            *[out[n][2] for n in WEIGHT_ORDER], *[out[n][3] for n in WEIGHT_ORDER])
```

```python
import math

import jax
import jax.numpy as jnp
from jax import lax
from jax.experimental import pallas as pl
from jax.experimental.pallas import tpu as pltpu

F32 = jnp.float32
BF16 = jnp.bfloat16
S = jax.ShapeDtypeStruct

N_DEV = 8
D = 1024
D_SSM = 384
N_GRP = 24
GRP = 16
N_STATE = 64
D_ST = N_GRP * N_STATE
D_POOL = 256
POOL_WINDOWS = (2, 4, 8, 16)
HALO = 16
D_SGU = 384
HEADS = 6
HEAD_DIM = 64
CHUNK = 128
D_IN = 1408
D_FF = 2816
EPS = 1e-6
SCAN_BLK = 8

ADAM_LR = 0.001
ADAM_B1 = 0.9
ADAM_B2 = 0.999
ADAM_EPS = 1e-08
ADAM_WD = 0.01
ADAM_STEP = 10

GELU_C0 = math.sqrt(2.0 / math.pi)
GELU_C1 = 0.044715

TT_MIX = 256
SEG_LEN = TT_MIX // SCAN_BLK
TT_FFN = 256
TT_PROJ = 1024
TK_WGRAD = 2048
VMEM_MB = 2 ** 20


def _cp(vmem_mb, grid_dims=0):
    kw = dict(vmem_limit_bytes=int(vmem_mb * VMEM_MB))
    if grid_dims:
        kw["dimension_semantics"] = ("arbitrary",) * grid_dims
    return pltpu.CompilerParams(**kw)


def _row(tt, n):
    return pl.BlockSpec((tt, n), lambda i: (i, 0))


def _full(shape):
    nd = len(shape)
    return pl.BlockSpec(shape, lambda *_: (0,) * nd)


def _nn(a, b):
    return jnp.dot(a, b, preferred_element_type=F32)


def _nt(a, b):
    return lax.dot_general(a, b, (((1,), (1,)), ((), ())), preferred_element_type=F32)


def _tn(a, b):
    return lax.dot_general(a, b, (((0,), (0,)), ((), ())), preferred_element_type=F32)


def _rowsum(x):
    return jnp.sum(x, axis=0, keepdims=True)


def _rms(x):
    r = lax.rsqrt(jnp.mean(x * x, axis=-1, keepdims=True) + EPS)
    return x * r, r


def _rms_bwd(dy, xn, r, g):
    dyg = dy * g
    return r * (dyg - xn * jnp.mean(dyg * xn, axis=-1, keepdims=True))


def _gelu(x):
    s = jax.nn.sigmoid(x * (2.0 * GELU_C0 + (2.0 * GELU_C0 * GELU_C1) * (x * x)))
    return x * s, s


def _gelu_grad(x, s):
    return s * (1.0 + x * (1.0 - s) * (2.0 * GELU_C0 + (6.0 * GELU_C0 * GELU_C1) * (x * x)))


def _discretise(a_re, a_im, ldt, b_re, b_im):
    dt = jnp.exp(ldt)
    mag = jnp.exp(a_re * dt)
    ar = mag * jnp.cos(a_im * dt)
    ai = mag * jnp.sin(a_im * dt)
    den = a_re * a_re + a_im * a_im
    f_re = ((ar - 1.0) * a_re + ai * a_im) / den
    f_im = (ai * a_re - (ar - 1.0) * a_im) / den
    bb_re = f_re * b_re - f_im * b_im
    bb_im = f_re * b_im + f_im * b_re
    return ar, ai, bb_re, bb_im


def _group_mask(rows, cols):
    r = lax.broadcasted_iota(jnp.int32, (rows, cols), 0) // GRP
    c = lax.broadcasted_iota(jnp.int32, (rows, cols), 1)
    c = jnp.where(c >= D_ST, c - D_ST, c) // N_STATE
    return r == c


def s5_prepare(a_re, a_im, ldt, b_re_t, b_im_t, c_re_t, c_im_t):
    def body(are_ref, aim_ref, ldt_ref, bre_ref, bim_ref, cre_ref, cim_ref, sc_ref, bbd_ref, cbd_ref):
        ar, ai, bb_re, bb_im = _discretise(are_ref[...], aim_ref[...], ldt_ref[...], bre_ref[...], bim_ref[...])
        mask = _group_mask(D_SSM, 2 * D_ST)
        bb = jnp.concatenate([jnp.tile(bb_re, (N_GRP, 1)), jnp.tile(bb_im, (N_GRP, 1))], axis=1)
        bbd_ref[...] = jnp.where(mask, bb, 0.0).astype(BF16)
        cc = jnp.concatenate([jnp.tile(cre_ref[...], (N_GRP, 1)), -jnp.tile(cim_ref[...], (N_GRP, 1))], axis=1)
        cbd_ref[...] = jnp.where(mask, cc, 0.0).astype(BF16)
        pr, pi = ar, ai
        for _ in range(SEG_LEN - 1):
            pr, pi = pr * ar - pi * ai, pr * ai + pi * ar
        for k, v in enumerate((ar, ai, pr, pi)):
            sc_ref[8 * k:8 * k + 8, :] = jnp.broadcast_to(v, (SCAN_BLK, D_ST))

    return pl.pallas_call(
        body, name="s5_prepare",
        out_shape=[S((32, D_ST), F32), S((D_SSM, 2 * D_ST), BF16), S((D_SSM, 2 * D_ST), BF16)],
        compiler_params=_cp(40),
    )(a_re, a_im, ldt, b_re_t, b_im_t, c_re_t, c_im_t)


def s5_param_bwd(a_re, a_im, ldt, b_re_t, b_im_t, da, dbt):
    def body(are_ref, aim_ref, ldt_ref, bre_ref, bim_ref, da_ref, dbt_ref, o_are, o_aim, o_ldt, o_bre, o_bim):
        _, vjp = jax.vjp(_discretise, are_ref[...], aim_ref[...], ldt_ref[...], bre_ref[...], bim_ref[...])
        da = da_ref[...]
        dbt = dbt_ref[...]
        g_are, g_aim, g_ldt, g_bre, g_bim = vjp((da[:, :D_ST], da[:, D_ST:], dbt[:, :D_ST], dbt[:, D_ST:]))
        o_are[...] = g_are
        o_aim[...] = g_aim
        o_bre[...] = g_bre
        o_bim[...] = g_bim
        grp = lax.broadcasted_iota(jnp.int32, (1, D_ST), 1) // N_STATE
        lane = lax.broadcasted_iota(jnp.int32, (1, 128), 1)
        out = jnp.zeros((1, 128), F32)
        for g in range(N_GRP):
            out = jnp.where(lane == g, jnp.sum(jnp.where(grp == g, g_ldt, 0.0), axis=1, keepdims=True), out)
        o_ldt[...] = out

    return pl.pallas_call(
        body, name="s5_param_bwd",
        out_shape=[S((1, D_ST), F32), S((1, D_ST), F32), S((1, 128), F32), S((GRP, D_ST), F32), S((GRP, D_ST), F32)],
        compiler_params=_cp(16),
    )(a_re, a_im, ldt, b_re_t, b_im_t, da, dbt)


_CH = ((0, 256), (256, D_SSM))
_ST = ((0, 1024), (1024, D_ST))


def _bd_expand(xb, w_ref, out_ref):
    for (c0, c1), (s0, s1) in zip(_CH, _ST):
        for half in (0, D_ST):
            out_ref[:, half + s0:half + s1] = _nn(xb[:, c0:c1], w_ref[c0:c1, half + s0:half + s1])


def _bd_contract(hb, w_ref):
    parts = []
    for (c0, c1), (s0, s1) in zip(_CH, _ST):
        parts.append(_nt(hb[:, s0:s1], w_ref[c0:c1, s0:s1]) + _nt(hb[:, D_ST + s0:D_ST + s1], w_ref[c0:c1, D_ST + s0:D_ST + s1]))
    return jnp.concatenate(parts, axis=1)


def _bd_accumulate(acc_ref, xb, hb):
    for j in range(N_GRP // 4):
        ch = slice(4 * GRP * j, 4 * GRP * (j + 1))
        for half in (0, D_ST):
            st = slice(half + 4 * N_STATE * j, half + 4 * N_STATE * (j + 1))
            acc_ref[ch, st] += _tn(xb[:, ch], hb[:, st])


def _interleave_matrices(tt):
    r = lax.broadcasted_iota(jnp.int32, (tt, tt), 0)
    t = lax.broadcasted_iota(jnp.int32, (tt, tt), 1)
    p = (t == (r % SCAN_BLK) * (tt // SCAN_BLK) + r // SCAN_BLK).astype(BF16)
    return p, p.T


def _interleave_f32(p, x):
    hi = x.astype(BF16)
    lo = (x - hi.astype(F32)).astype(BF16)
    return _nn(p, hi) + _nn(p, lo)


def _scan_tile(buf_ref, sc_ref, carry_ref, n_blk, reverse, on_block=None):
    ar = sc_ref[0:8, :]
    ai = -sc_ref[8:16, :] if reverse else sc_ref[8:16, :]

    def rows(i):
        blk = (n_blk - 1 - i) if reverse else i
        return pl.ds(pl.multiple_of(blk * SCAN_BLK, SCAN_BLK), SCAN_BLK)

    def local(i, x):
        xr, xi = x
        r = rows(i)
        xr, xi = buf_ref[r, 0:D_ST] + ar * xr - ai * xi, buf_ref[r, D_ST:2 * D_ST] + ar * xi + ai * xr
        buf_ref[r, 0:D_ST] = xr
        buf_ref[r, D_ST:2 * D_ST] = xi
        return xr, xi

    zero = jnp.zeros((SCAN_BLK, D_ST), F32)
    end_r, end_i = lax.fori_loop(0, n_blk, local, (zero, zero), unroll=True)

    seg_r = sc_ref[16:17, :]
    seg_i = -sc_ref[24:25, :] if reverse else sc_ref[24:25, :]
    cr, ci = carry_ref[0:1, 0:D_ST], carry_ref[0:1, D_ST:2 * D_ST]
    sub = lax.broadcasted_iota(jnp.int32, (SCAN_BLK, D_ST), 0)
    in_r, in_i = zero, zero
    for s in (reversed(range(SCAN_BLK)) if reverse else range(SCAN_BLK)):
        in_r = jnp.where(sub == s, cr, in_r)
        in_i = jnp.where(sub == s, ci, in_i)
        cr, ci = end_r[s:s + 1, :] + seg_r * cr - seg_i * ci, end_i[s:s + 1, :] + seg_r * ci + seg_i * cr
    carry_ref[0:1, 0:D_ST] = cr
    carry_ref[0:1, D_ST:2 * D_ST] = ci

    dr, di = in_r, in_i
    for i in range(n_blk):
        blk = (n_blk - 1 - i) if reverse else i
        r = slice(blk * SCAN_BLK, (blk + 1) * SCAN_BLK)
        dr, di = ar * dr - ai * di, ar * di + ai * dr
        xr, xi = buf_ref[r, 0:D_ST] + dr, buf_ref[r, D_ST:2 * D_ST] + di
        buf_ref[r, 0:D_ST] = xr
        buf_ref[r, D_ST:2 * D_ST] = xi
        if on_block is not None:
            on_block(blk, xr, xi)


def _lane_windows(n):
    lane = lax.broadcasted_iota(jnp.int32, (1, n), 1)
    return lane // (D_POOL // len(POOL_WINDOWS))


def _select_window(grp, s2, s4, s8, s16):
    return jnp.where(grp == 0, s2, jnp.where(grp == 1, s4, jnp.where(grp == 2, s8, s16)))


def _pool_fwd(pbuf_ref, zb, halo, tile_idx, tt):
    pbuf_ref[0:HALO, :] = halo
    pbuf_ref[HALO:HALO + tt, :] = zb
    x = pbuf_ref[...]
    s2 = x + pltpu.roll(x, 1, axis=0)
    s4 = s2 + pltpu.roll(s2, 2, axis=0)
    s8 = s4 + pltpu.roll(s4, 4, axis=0)
    s16 = s8 + pltpu.roll(s8, 8, axis=0)
    grp = _lane_windows(D_POOL)
    win = _select_window(grp, s2, s4, s8, s16)[HALO:HALO + tt, :]
    width = _select_window(grp, 2.0, 4.0, 8.0, 16.0).astype(F32)
    pos = (tile_idx * tt + 1 + lax.broadcasted_iota(jnp.int32, (tt, 1), 0)).astype(F32)
    cnt = jnp.minimum(pos, width)
    return win / cnt - zb, cnt


def _sgu_fwd(zu, zv, lng, lnb, wsm_ref, bsp, mix_ref, tt):
    u, tu = _gelu(zu)
    v, tv = _gelu(zv)
    mu = jnp.mean(v, axis=-1, keepdims=True)
    vc = v - mu
    rstd = lax.rsqrt(jnp.mean(vc * vc, axis=-1, keepdims=True) + EPS)
    vhat = vc * rstd
    vnb = (vhat * lng + lnb).astype(BF16)
    _head_mix(wsm_ref, vnb, mix_ref, tt, bsp)
    return u, tu, tv, vhat, rstd, vnb


def _chunk_pairs(tt):
    n_ch = tt // CHUNK
    return [list(range(c, min(c + 2, n_ch))) for c in range(0, n_ch, 2)]


def _head_cols(xb, chunks, h):
    return jnp.concatenate([xb[c * CHUNK:(c + 1) * CHUNK, h * HEAD_DIM:(h + 1) * HEAD_DIM] for c in chunks], axis=1)


def _head_mix(w_ref, xb, out_ref, tt, add=None):
    for chunks in _chunk_pairs(tt):
        per_head = [_nn(w_ref[h], _head_cols(xb, chunks, h)) for h in range(HEADS)]
        for k, c in enumerate(chunks):
            block = jnp.concatenate([r[:, k * HEAD_DIM:(k + 1) * HEAD_DIM] for r in per_head], axis=1)
            out_ref[c * CHUNK:(c + 1) * CHUNK, :] = block if add is None else block + add


def mixer_fwd(z, sc, bbd, cbd, dskip, wglu, bglu, wpool, pscale, lng, lnb, wsm, bsp, perm, name, exchange=None):
    T = z.shape[0]
    tt = TT_MIX
    n_tiles = T // tt

    def body(z_ref, sc_ref, bbd_ref, cbd_ref, dskip_ref, wglu_ref, bglu_ref, wpool_ref, pscale_ref, lng_ref, lnb_ref,
             wsm_ref, bsp_ref, p_ref, pt_ref, ycat_ref, hs_ref, ys_ref, carry_ref, halo_ref, pbuf_ref, mix_ref):
        i = pl.program_id(0)

        @pl.when(i == 0)
        def _():
            carry_ref[...] = jnp.zeros_like(carry_ref)
            halo_ref[...] = jnp.zeros_like(halo_ref)

        za = z_ref[:, 0:D_SSM]
        p, pt = p_ref[...], pt_ref[...]
        za = _interleave_f32(p, za)
        _bd_expand(za.astype(BF16), bbd_ref, hs_ref)
        _scan_tile(hs_ref, sc_ref, carry_ref, tt // SCAN_BLK, reverse=False)
        y = _bd_contract(hs_ref[...].astype(BF16), cbd_ref) + dskip_ref[...] * za
        ys_ref[...] = y
        g, _ = _gelu(y)
        q = _nn(g.astype(BF16), wglu_ref[...]) + bglu_ref[...]
        ycat_ref[:, 0:D_SSM] = _nn(pt, (g * jax.nn.sigmoid(q)).astype(BF16)).astype(BF16)
        zb = z_ref[:, D_SSM:D_SSM + D_POOL]
        pooled, _ = _pool_fwd(pbuf_ref, zb, halo_ref[...], i, tt)
        halo_ref[...] = zb[tt - HALO:tt, :]
        ycat_ref[:, D_SSM:D_SSM + D_POOL] = (_nn(pooled.astype(BF16), wpool_ref[...]) * pscale_ref[...]).astype(BF16)
        zu = z_ref[:, D_SSM + D_POOL:D_SSM + D_POOL + D_SGU]
        zv = z_ref[:, D_SSM + D_POOL + D_SGU:D_IN]
        u, _, _, _, _, _ = _sgu_fwd(zu, zv, lng_ref[...], lnb_ref[...], wsm_ref, bsp_ref[...], mix_ref, tt)
        ycat_ref[:, D_SSM + D_POOL:D] = (u * mix_ref[...]).astype(BF16)

    return _pallas(
        body, name=name, grid=(n_tiles,),
        in_specs=[_row(tt, D_IN), _full((32, D_ST)), _full((D_SSM, 2 * D_ST)), _full((D_SSM, 2 * D_ST)),
                  _full((1, D_SSM)), _full((D_SSM, D_SSM)), _full((1, D_SSM)), _full((D_POOL, D_POOL)),
                  _full((1, D_POOL)), _full((1, D_SGU)), _full((1, D_SGU)), _full((HEADS, CHUNK, CHUNK)),
                  _full((CHUNK, D_SGU)), _full((tt, tt)), _full((tt, tt))],
        out_specs=[_row(tt, D), _row(tt, 2 * D_ST), _row(tt, D_SSM)],
        out_shape=[S((T, D), BF16), S((T, 2 * D_ST), F32), S((T, D_SSM), F32)],
        scratch_shapes=[pltpu.VMEM((SCAN_BLK, 2 * D_ST), F32), pltpu.VMEM((HALO, D_POOL), F32),
                        pltpu.VMEM((tt + HALO, D_POOL), F32), pltpu.VMEM((tt, D_SGU), F32)],
        vmem_mb=48, operands=(z, sc, bbd, cbd, dskip, wglu, bglu, wpool, pscale, lng, lnb, wsm, bsp, *perm),
        exchange=exchange)


def mixer_bwd(dx1b, z, hs, ys, wout, sc, bbd, cbd, dskip, wglu, bglu, wpool, pscale, lng, lnb, wsm, wsmt, bsp, perm, name,
              exchange=None):
    T = z.shape[0]
    tt = TT_MIX
    n_tiles = T // tt

    def rev(i):
        return n_tiles - 1 - i

    def body(dx_ref, z_ref, zprev_ref, hs_ref, hsprev_ref, ys_ref, wout_ref, sc_ref, bbd_ref, cbd_ref, dskip_ref,
             wglu_ref, bglu_ref, wpool_ref, pscale_ref, lng_ref, lnb_ref, wsm_ref, wsmt_ref, bsp_ref, p_ref, pt_ref,
             dz_ref, o_da, o_dbt, o_dct, o_dd, o_dbglu, o_dwglu, o_dwpool, o_dpscale, o_dlng, o_dlnb, o_dws, o_dbsp,
             gbuf_ref, carry_ref, accb_ref, accc_ref, ehalo_ref, pbuf_ref, mix_ref, dvn_ref, accw_ref, accm_ref):
        i = pl.program_id(0)
        tile = rev(i)

        @pl.when(i == 0)
        def _():
            carry_ref[...] = jnp.zeros_like(carry_ref)
            accb_ref[...] = jnp.zeros_like(accb_ref)
            accc_ref[...] = jnp.zeros_like(accc_ref)
            for o in (o_da, o_dd, o_dbglu, o_dwglu):
                o[...] = jnp.zeros_like(o)

        p, pt = p_ref[...], pt_ref[...]
        dxb = dx_ref[...]
        d_a = _nt(_nn(p, dxb).astype(BF16), wout_ref[0:D_SSM, :])
        za = _interleave_f32(p, z_ref[:, 0:D_SSM])
        first_tile = (tile > 0).astype(F32)

        y = ys_ref[...]
        g, tg = _gelu(y)
        gb = g.astype(BF16)
        sg = jax.nn.sigmoid(_nn(gb, wglu_ref[...]) + bglu_ref[...])
        dq = d_a * g * sg * (1.0 - sg)
        dqb = dq.astype(BF16)
        o_dbglu[...] += _rowsum(dq)
        o_dwglu[...] += _tn(gb, dqb)
        dy = (d_a * sg + _nt(dqb, wglu_ref[...])) * _gelu_grad(y, tg)
        o_dd[...] += _rowsum(dy * za)
        dyb = dy.astype(BF16)
        _bd_expand(dyb, cbd_ref, gbuf_ref)
        hprev = hsprev_ref[SCAN_BLK - 1:SCAN_BLK, :] * first_tile
        sub = lax.broadcasted_iota(jnp.int32, (SCAN_BLK, 1), 0)
        edge = jnp.where(sub == 0, hprev, pltpu.roll(hs_ref[tt - SCAN_BLK:tt, :], 1, axis=0))
        da = [jnp.zeros((SCAN_BLK, D_ST), F32), jnp.zeros((SCAN_BLK, D_ST), F32)]

        def da_terms(blk, gr, gi):
            before = edge if blk == 0 else hs_ref[(blk - 1) * SCAN_BLK:blk * SCAN_BLK, :]
            hr, hi = before[:, 0:D_ST], before[:, D_ST:]
            da[0] = da[0] + (gr * hr + gi * hi)
            da[1] = da[1] + (gi * hr - gr * hi)

        _scan_tile(gbuf_ref, sc_ref, carry_ref, tt // SCAN_BLK, reverse=True, on_block=da_terms)
        o_da[:, 0:D_ST] += _rowsum(da[0])
        o_da[:, D_ST:] += _rowsum(da[1])
        gtb = gbuf_ref[...].astype(BF16)
        dza = (dy * dskip_ref[...] + _bd_contract(gtb, bbd_ref)).astype(BF16)
        dz_ref[:, 0:D_SSM] = _nn(pt, dza).astype(BF16)
        _bd_accumulate(accb_ref, za.astype(BF16), gtb)
        _bd_accumulate(accc_ref, dyb, hs_ref[...].astype(BF16))
        d_bc = _nt(dx_ref[...], wout_ref[D_SSM:D, :])
        _pool_sgu_bwd(i, n_tiles, tile, tt, d_bc, z_ref, zprev_ref, wpool_ref, pscale_ref, lng_ref, lnb_ref, wsm_ref,
                      wsmt_ref, bsp_ref, dz_ref, o_dwpool, o_dpscale, o_dlng, o_dlnb, o_dws, o_dbsp,
                      ehalo_ref, pbuf_ref, mix_ref, dvn_ref, accw_ref, accm_ref)

        @pl.when(i == n_tiles - 1)
        def _():
            mask = _group_mask(D_SSM, 2 * D_ST)
            for acc_ref, o_ref in ((accb_ref, o_dbt), (accc_ref, o_dct)):
                fold = jnp.zeros((GRP, 2 * D_ST), F32)
                for gidx in range(N_GRP):
                    rows = slice(gidx * GRP, (gidx + 1) * GRP)
                    fold = fold + jnp.where(mask[rows, :], acc_ref[rows, :], 0.0)
                o_ref[...] = fold

    def rowr(n):
        return pl.BlockSpec((tt, n), lambda i: (rev(i), 0))

    zprev_spec = pl.BlockSpec((HALO, D_IN), lambda i: (jnp.maximum(rev(i) * (tt // HALO) - 1, 0), 0))
    hsprev_spec = pl.BlockSpec((SCAN_BLK, 2 * D_ST), lambda i: (jnp.maximum(rev(i) * (tt // SCAN_BLK) - 1, 0), 0))
    small = [S((1, 2 * D_ST), F32), S((GRP, 2 * D_ST), F32), S((GRP, 2 * D_ST), F32), S((1, D_SSM), F32),
             S((1, D_SSM), F32), S((D_SSM, D_SSM), F32), S((D_POOL, D_POOL), F32), S((1, D_POOL), F32),
             S((1, D_SGU), F32), S((1, D_SGU), F32), S((HEADS, CHUNK, CHUNK), F32), S((CHUNK, 128), F32)]
    return _pallas(
        body, name=name, grid=(n_tiles,),
        in_specs=[rowr(D), rowr(D_IN), zprev_spec, rowr(2 * D_ST), hsprev_spec, rowr(D_SSM), _full((D, D)),
                  _full((32, D_ST)), _full((D_SSM, 2 * D_ST)), _full((D_SSM, 2 * D_ST)), _full((1, D_SSM)),
                  _full((D_SSM, D_SSM)), _full((1, D_SSM)), _full((D_POOL, D_POOL)), _full((1, D_POOL)),
                  _full((1, D_SGU)), _full((1, D_SGU)), _full((HEADS, CHUNK, CHUNK)), _full((HEADS, CHUNK, CHUNK)),
                  _full((CHUNK, D_SGU)), _full((tt, tt)), _full((tt, tt))],
        out_specs=[rowr(D_IN)] + [_full(s.shape) for s in small],
        out_shape=[S((T, D_IN), BF16)] + small,
        scratch_shapes=[pltpu.VMEM((tt, 2 * D_ST), F32), pltpu.VMEM((SCAN_BLK, 2 * D_ST), F32),
                        pltpu.VMEM((D_SSM, 2 * D_ST), F32), pltpu.VMEM((D_SSM, 2 * D_ST), F32),
                        pltpu.VMEM((HALO, D_POOL), F32), pltpu.VMEM((tt + HALO, D_POOL), F32),
                        pltpu.VMEM((tt, D_SGU), F32), pltpu.VMEM((tt, D_SGU), F32),
                        pltpu.VMEM((HEADS, CHUNK, CHUNK), F32), pltpu.VMEM((CHUNK, D_SGU), F32)],
        vmem_mb=56, exchange=exchange,
        operands=(dx1b, z, z, hs, hs, ys, wout, sc, bbd, cbd, dskip, wglu, bglu, wpool, pscale, lng, lnb, wsm, wsmt, bsp, *perm))


def inproj_fwd(x, g, w_t, name, exchange=None):
    T = x.shape[0]
    tt = min(TT_PROJ, T)

    def body(x_ref, g_ref, w_ref, h_ref, z_ref):
        xn, _ = _rms(x_ref[...])
        h = (xn * g_ref[...]).astype(BF16)
        h_ref[...] = h
        z_ref[...] = _nt(h, w_ref[...])

    return _pallas(
        body, name=name, grid=(T // tt,),
        in_specs=[_row(tt, D), _full((1, D)), _full((D_IN, D))],
        out_specs=[_row(tt, D), _row(tt, D_IN)],
        out_shape=[S((T, D), BF16), S((T, D_IN), F32)],
        scratch_shapes=[], vmem_mb=48, operands=(x, g, w_t), exchange=exchange)


def inproj_bwd(dzb, x, g, w_t, dx1):
    T = x.shape[0]
    tt = min(TT_PROJ, T)

    def body(dz_ref, x_ref, g_ref, w_ref, dx1_ref, dx_ref, dg_ref):
        @pl.when(pl.program_id(0) == 0)
        def _():
            dg_ref[...] = jnp.zeros_like(dg_ref)

        dh = _nn(dz_ref[...], w_ref[...])
        xn, r = _rms(x_ref[...])
        dg_ref[...] += _rowsum(dh * xn)
        dx_ref[...] = dx1_ref[...] + _rms_bwd(dh, xn, r, g_ref[...])

    return pl.pallas_call(
        body, name="inproj_bwd", grid=(T // tt,),
        in_specs=[_row(tt, D_IN), _row(tt, D), _full((1, D)), _full((D_IN, D)), _row(tt, D)],
        out_specs=[_row(tt, D), _full((1, D))],
        out_shape=[S((T, D), F32), S((1, D), F32)],
        compiler_params=_cp(48, 1),
    )(dzb, x, g, w_t, dx1)


def _load_weights(pairs, sem):
    @pl.when(pl.program_id(0) == 0)
    def _():
        copies = [pltpu.make_async_copy(src, dst, sem.at[k]) for k, (src, dst) in enumerate(pairs)]
        for cp in copies:
            cp.start()
        for cp in copies:
            cp.wait()


def ffn_fwd(x, ycat, wout, g, wg_t, wu_t, wd, name, exchange=None, head=None):
    T = x.shape[0]
    tt = TT_FFN
    any_spec = pl.BlockSpec(memory_space=pl.ANY)

    def body(*refs):
        if head is None:
            (x_ref, ycat_ref, g_ref, wout_hbm, wg_hbm, wu_hbm, wd_hbm,
             x1_ref, h_ref, gate_ref, up_ref, act_ref, x2_ref, wout_v, wg_v, wu_v, wd_v, sem) = refs
        else:
            (x_ref, ycat_ref, g_ref, t_ref, gf_ref, wout_hbm, wg_hbm, wu_hbm, wd_hbm,
             x1_ref, h_ref, gate_ref, up_ref, act_ref, x2_ref, lvec_ref, dgf_ref, wout_v, wg_v, wu_v, wd_v, sem) = refs
        _ffn_fwd_tile(x_ref, ycat_ref, g_ref, wout_hbm, wg_hbm, wu_hbm, wd_hbm, x1_ref, h_ref, gate_ref, up_ref, act_ref,
                      x2_ref, wout_v, wg_v, wu_v, wd_v, sem)
        if head is not None:
            @pl.when(pl.program_id(0) == 0)
            def _():
                lvec_ref[...] = jnp.zeros_like(lvec_ref)
                dgf_ref[...] = jnp.zeros_like(dgf_ref)

            xn, r = _rms(x2_ref[...])
            gf = gf_ref[...]
            err = xn * gf - t_ref[...]
            lvec_ref[...] += _rowsum(err * err)
            dy = err * (1.0 / D)
            dgf_ref[...] += _rowsum(dy * xn)
            x2_ref[...] = _rms_bwd(dy, xn, r, gf)

    def _ffn_fwd_tile(x_ref, ycat_ref, g_ref, wout_hbm, wg_hbm, wu_hbm, wd_hbm,
                      x1_ref, h_ref, gate_ref, up_ref, act_ref, x2_ref, wout_v, wg_v, wu_v, wd_v, sem):
        _load_weights([(wout_hbm, wout_v), (wg_hbm, wg_v), (wu_hbm, wu_v), (wd_hbm, wd_v)], sem)
        x1 = x_ref[...] + _nn(ycat_ref[...], wout_v[...])
        x1_ref[...] = x1
        xn, _ = _rms(x1)
        h = (xn * g_ref[...]).astype(BF16)
        h_ref[...] = h
        gate = _nt(h, wg_v[...])
        up = _nt(h, wu_v[...])
        gate_ref[...] = gate.astype(BF16)
        up_ref[...] = up.astype(BF16)
        act = (gate * jax.nn.sigmoid(gate) * up).astype(BF16)
        act_ref[...] = act
        x2_ref[...] = x1 + _nn(act, wd_v[...])

    with_head = head is not None
    return _pallas(
        body, name=name, grid=(T // tt,),
        in_specs=[_row(tt, D), _row(tt, D), _full((1, D))] + ([_row(tt, D), _full((1, D))] if with_head else [])
        + [any_spec, any_spec, any_spec, any_spec],
        out_specs=[_row(tt, D), _row(tt, D), _row(tt, D_FF), _row(tt, D_FF), _row(tt, D_FF), _row(tt, D)]
        + ([_full((1, D)), _full((1, D))] if with_head else []),
        out_shape=[S((T, D), F32), S((T, D), BF16), S((T, D_FF), BF16), S((T, D_FF), BF16), S((T, D_FF), BF16),
                   S((T, D), F32)] + ([S((1, D), F32), S((1, D), F32)] if with_head else []),
        scratch_shapes=[pltpu.VMEM((D, D), BF16), pltpu.VMEM((D_FF, D), BF16), pltpu.VMEM((D_FF, D), BF16),
                        pltpu.VMEM((D_FF, D), BF16), pltpu.SemaphoreType.DMA((4,))],
        vmem_mb=56, operands=(x, ycat, g) + (tuple(head) if with_head else ()) + (wout, wg_t, wu_t, wd), exchange=exchange)


def _pool_sgu_bwd(i, n_tiles, tile, tt, d_bc, z_ref, zprev_ref, wpool_ref, pscale_ref, lng_ref, lnb_ref, wsm_ref, wsmt_ref,
                  bsp_ref, dz_ref, o_dwpool, o_dpscale, o_dlng, o_dlnb, o_dws, o_dbsp,
                  ehalo_ref, pbuf_ref, mix_ref, dvn_ref, accw_ref, accm_ref):
    @pl.when(i == 0)
    def _():
        ehalo_ref[...] = jnp.zeros_like(ehalo_ref)
        accw_ref[...] = jnp.zeros_like(accw_ref)
        accm_ref[...] = jnp.zeros_like(accm_ref)
        for o in (o_dwpool, o_dpscale, o_dlng, o_dlnb):
            o[...] = jnp.zeros_like(o)

    d_b = d_bc[:, 0:D_POOL]
    d_c = d_bc[:, D_POOL:D_POOL + D_SGU]
    zb = z_ref[:, D_SSM:D_SSM + D_POOL]
    zu = z_ref[:, D_SSM + D_POOL:D_SSM + D_POOL + D_SGU]
    zv = z_ref[:, D_SSM + D_POOL + D_SGU:D_IN]
    not_first = (tile > 0).astype(F32)

    pooled, cnt = _pool_fwd(pbuf_ref, zb, zprev_ref[:, D_SSM:D_SSM + D_POOL] * not_first, tile, tt)
    pooledb = pooled.astype(BF16)
    mixed = _nn(pooledb, wpool_ref[...])
    o_dpscale[...] += _rowsum(d_b * mixed)
    dmixb = (d_b * pscale_ref[...]).astype(BF16)
    o_dwpool[...] += _tn(pooledb, dmixb)
    dpooled = _nt(dmixb, wpool_ref[...])
    e = dpooled / cnt
    pbuf_ref[0:tt, :] = e
    pbuf_ref[tt:tt + HALO, :] = ehalo_ref[...]
    ehalo_ref[...] = e[0:HALO, :]
    x = pbuf_ref[...]
    n = tt + HALO
    f2 = x + pltpu.roll(x, n - 1, axis=0)
    f4 = f2 + pltpu.roll(f2, n - 2, axis=0)
    f8 = f4 + pltpu.roll(f4, n - 4, axis=0)
    f16 = f8 + pltpu.roll(f8, n - 8, axis=0)
    fwd_sum = _select_window(_lane_windows(D_POOL), f2, f4, f8, f16)[0:tt, :]
    dz_ref[:, D_SSM:D_SSM + D_POOL] = (fwd_sum - dpooled).astype(BF16)

    lng = lng_ref[...]
    u, su, sv, vhat, rstd, vnb = _sgu_fwd(zu, zv, lng, lnb_ref[...], wsm_ref, bsp_ref[...], mix_ref, tt)
    dz_ref[:, D_SSM + D_POOL:D_SSM + D_POOL + D_SGU] = (d_c * mix_ref[...] * _gelu_grad(zu, su)).astype(BF16)
    dmix = d_c * u
    dmixb2 = dmix.astype(BF16)
    for c in range(tt // CHUNK):
        accm_ref[...] += dmix[c * CHUNK:(c + 1) * CHUNK, :]
    for chunks in _chunk_pairs(tt):
        for h in range(HEADS):
            accw_ref[h] += _nt(_head_cols(dmixb2, chunks, h), _head_cols(vnb, chunks, h))
    _head_mix(wsmt_ref, dmixb2, dvn_ref, tt)
    dvn = dvn_ref[...]
    o_dlng[...] += _rowsum(dvn * vhat)
    o_dlnb[...] += _rowsum(dvn)
    dvh = dvn * lng
    dv = rstd * (dvh - jnp.mean(dvh, axis=-1, keepdims=True) - vhat * jnp.mean(dvh * vhat, axis=-1, keepdims=True))
    dz_ref[:, D_SSM + D_POOL + D_SGU:D_IN] = (dv * _gelu_grad(zv, sv)).astype(BF16)

    @pl.when(i == n_tiles - 1)
    def _():
        tri = (lax.broadcasted_iota(jnp.int32, (CHUNK, CHUNK), 0) >= lax.broadcasted_iota(jnp.int32, (CHUNK, CHUNK), 1))
        for h in range(HEADS):
            o_dws[h] = jnp.where(tri, accw_ref[h], 0.0)
        lane = lax.broadcasted_iota(jnp.int32, (1, 128), 1)
        acc = jnp.zeros((CHUNK, 128), F32)
        for h in range(HEADS):
            sh = jnp.sum(accm_ref[:, h * HEAD_DIM:(h + 1) * HEAD_DIM], axis=1, keepdims=True)
            acc = jnp.where(lane == h, sh, acc)
        o_dbsp[...] = acc


def ffn_bwd(dx2, x1, gate, up, g, wg_t, wu_t, wd, name, exchange=None):
    T = x1.shape[0]
    tt = TT_FFN
    any_spec = pl.BlockSpec(memory_space=pl.ANY)

    def body(dx2_ref, x1_ref, gate_ref, up_ref, g_ref, wg_hbm, wu_hbm, wd_hbm,
             dgu_ref, dx2b_ref, dx1_ref, dx1b_ref, dg_ref, wg_v, wu_v, wd_v, sem):
        _load_weights([(wg_hbm, wg_v), (wu_hbm, wu_v), (wd_hbm, wd_v)], sem)

        @pl.when(pl.program_id(0) == 0)
        def _():
            dg_ref[...] = jnp.zeros_like(dg_ref)

        dx2 = dx2_ref[...]
        dx2b = dx2.astype(BF16)
        dx2b_ref[...] = dx2b
        dact = _nt(dx2b, wd_v[...])
        gate = gate_ref[...].astype(F32)
        up = up_ref[...].astype(F32)
        sg = jax.nn.sigmoid(gate)
        dgate = (dact * up * (sg * (1.0 + gate * (1.0 - sg)))).astype(BF16)
        dup = (dact * gate * sg).astype(BF16)
        dgu_ref[:, 0:D_FF] = dgate
        dgu_ref[:, D_FF:2 * D_FF] = dup
        dh = _nn(dgate, wg_v[...]) + _nn(dup, wu_v[...])
        xn, r = _rms(x1_ref[...])
        dg_ref[...] += _rowsum(dh * xn)
        dx1 = dx2 + _rms_bwd(dh, xn, r, g_ref[...])
        dx1_ref[...] = dx1
        dx1b_ref[...] = dx1.astype(BF16)

    return _pallas(
        body, name=name, grid=(T // tt,),
        in_specs=[_row(tt, D), _row(tt, D), _row(tt, D_FF), _row(tt, D_FF), _full((1, D)), any_spec, any_spec, any_spec],
        out_specs=[_row(tt, 2 * D_FF), _row(tt, D), _row(tt, D), _row(tt, D), _full((1, D))],
        out_shape=[S((T, 2 * D_FF), BF16), S((T, D), BF16), S((T, D), F32), S((T, D), BF16), S((1, D), F32)],
        scratch_shapes=[pltpu.VMEM((D_FF, D), BF16), pltpu.VMEM((D_FF, D), BF16), pltpu.VMEM((D_FF, D), BF16),
                        pltpu.SemaphoreType.DMA((3,))],
        vmem_mb=56, operands=(dx2, x1, gate, up, g, wg_t, wu_t, wd), exchange=exchange)


def wgrad(a, b, tm, name, exchange=None, tk=TK_WGRAD):
    T, M = a.shape
    N = b.shape[1]
    tk = min(tk, T)
    n_k = T // tk

    def body(a_ref, b_ref, o_ref, acc_ref):
        k = pl.program_id(1)

        @pl.when(k == 0)
        def _():
            acc_ref[...] = jnp.zeros_like(acc_ref)

        acc_ref[...] += _tn(a_ref[...], b_ref[...])

        @pl.when(k == n_k - 1)
        def _():
            o_ref[...] = acc_ref[...].astype(BF16)

    (out,), got = _pallas(
        body, name=name, grid=(M // tm, n_k),
        in_specs=[pl.BlockSpec((tk, tm), lambda m, k: (k, m)), pl.BlockSpec((tk, N), lambda m, k: (k, 0))],
        out_specs=[pl.BlockSpec((tm, N), lambda m, k: (m, 0))],
        out_shape=[S((M, N), BF16)],
        scratch_shapes=[pltpu.VMEM((tm, N), F32)],
        vmem_mb=48, operands=(a, b), exchange=exchange)
    return out if exchange is None else (out, got)


def _mesh_place():
    x, y, c = lax.axis_index("x"), lax.axis_index("y"), lax.axis_index("c")
    return x, y, c, 4 * x + 2 * y + c


def _peer(x, y, c, k):
    px = 1 - x if k & 4 else x
    py = 1 - y if k & 2 else y
    pc = 1 - c if k & 1 else c
    return (px, py, pc), 4 * px + 2 * py + pc


class _Exchange:
    SAME_CORE = (2, 4, 6)

    def __init__(self, gather=(), scatter=()):
        self.entries = [(a, None, a.shape[0]) for a in gather] + [(a, off, rows) for a, off, rows in scatter]
        self.n_gather = len(gather)

    @property
    def n(self):
        return len(self.entries)

    def operands(self):
        return [e[0] for e in self.entries]

    def out_shapes(self):
        return [S((N_DEV, rows, a.shape[1]), a.dtype) for a, _, rows in self.entries]

    def sems(self):
        return [pltpu.SemaphoreType.DMA((self.n, N_DEV)), pltpu.SemaphoreType.DMA((self.n, N_DEV)),
                pltpu.SemaphoreType.DMA((self.n,))]

    def _src(self, ref, e, idx):
        _, off, rows = self.entries[e]
        if off is None:
            return ref
        return ref.at[pl.ds(pl.multiple_of(off + idx * rows, 16), rows)]

    def _masks(self, e):
        return (1,) + self.SAME_CORE if e < self.n_gather else tuple(range(1, N_DEV))

    def _copy(self, ins, outs, sems, e, k, sending, passing_on=False):
        send_sems, recv_sems, _ = sems
        x, y, c, me = _mesh_place()
        peer, pidx = _peer(x, y, c, k)
        if passing_on:
            return pltpu.make_async_remote_copy(
                src_ref=outs[e].at[pidx], dst_ref=outs[e].at[pidx], send_sem=send_sems.at[e, k | 1],
                recv_sem=recv_sems.at[e, k | 1], device_id=_peer(x, y, c, 1)[0], device_id_type=pl.DeviceIdType.MESH)
        return pltpu.make_async_remote_copy(
            src_ref=self._src(ins[e], e, pidx), dst_ref=outs[e].at[me if sending else pidx], send_sem=send_sems.at[e, k],
            recv_sem=recv_sems.at[e, k], device_id=peer, device_id_type=pl.DeviceIdType.MESH)

    def _local(self, ins, outs, sems):
        me = _mesh_place()[3]
        return [pltpu.make_async_copy(self._src(ins[e], e, me), outs[e].at[me], sems[2].at[e]) for e in range(self.n)]

    def start(self, ins, outs, sems):
        for cp in self._local(ins, outs, sems):
            cp.start()
        for k in range(1, N_DEV):
            for e in range(self.n):
                if k in self._masks(e):
                    self._copy(ins, outs, sems, e, k, True).start()

    def forward(self, ins, outs, sems):
        for k in self.SAME_CORE:
            for e in range(self.n_gather):
                self._copy(ins, outs, sems, e, k, False).wait_recv()
                self._copy(ins, outs, sems, e, k, False, passing_on=True).start()

    def wait(self, ins, outs, sems):
        for k in range(1, N_DEV):
            for e in range(self.n):
                if e >= self.n_gather or k % 2:
                    self._copy(ins, outs, sems, e, k, False).wait_recv()
        for k in range(1, N_DEV):
            for e in range(self.n):
                self._copy(ins, outs, sems, e, k, True).wait_send()
        for cp in self._local(ins, outs, sems):
            cp.wait()


def _pallas(body, *, name, grid, in_specs, out_specs, out_shape, scratch_shapes, vmem_mb, operands, exchange=None,
            aliases=None):
    n_in, n_out, n_scr = len(in_specs), len(out_specs), len(scratch_shapes)
    n_steps = math.prod(grid)
    aliases = aliases or {}
    if exchange is None:
        res = pl.pallas_call(body, name=name, grid=grid, in_specs=in_specs, out_specs=out_specs, out_shape=out_shape,
                             scratch_shapes=scratch_shapes, input_output_aliases=aliases,
                             compiler_params=_cp(vmem_mb, len(grid)))(*operands)
        return list(res), []
    ex = exchange

    def hosted(*refs):
        ins, ex_in = refs[:n_in], refs[n_in:n_in + ex.n]
        outs = refs[n_in + ex.n:n_in + ex.n + n_out]
        ex_out = refs[n_in + ex.n + n_out:n_in + 2 * ex.n + n_out]
        scr = refs[n_in + 2 * ex.n + n_out:]
        sems = scr[n_scr:]
        step = pl.program_id(0)
        for axis in range(1, len(grid)):
            step = step * grid[axis] + pl.program_id(axis)

        @pl.when(step == 0)
        def _():
            ex.start(ex_in, ex_out, sems)

        body(*ins, *outs, *scr[:n_scr])

        if ex.n_gather:
            @pl.when(step == max(n_steps - 1 - max(2, n_steps // 8), 0))
            def _():
                ex.forward(ex_in, ex_out, sems)

        @pl.when(step == n_steps - 1)
        def _():
            ex.wait(ex_in, ex_out, sems)

    any_spec = pl.BlockSpec(memory_space=pl.ANY)
    res = pl.pallas_call(
        hosted, name=name, grid=grid, in_specs=list(in_specs) + [any_spec] * ex.n,
        out_specs=list(out_specs) + [any_spec] * ex.n, out_shape=list(out_shape) + ex.out_shapes(),
        scratch_shapes=list(scratch_shapes) + ex.sems(), input_output_aliases=aliases,
        compiler_params=_cp(vmem_mb, len(grid)),
    )(*operands, *ex.operands())
    return list(res[:n_out]), list(res[n_out:])


def exchange_only(ex, name):
    def body(*refs):
        ins, outs, sems = refs[:ex.n], refs[ex.n:2 * ex.n], refs[2 * ex.n:]
        ex.start(ins, outs, sems)
        ex.forward(ins, outs, sems)
        ex.wait(ins, outs, sems)

    any_spec = pl.BlockSpec(memory_space=pl.ANY)
    return list(pl.pallas_call(body, name=name, in_specs=[any_spec] * ex.n, out_specs=[any_spec] * ex.n,
                               out_shape=ex.out_shapes(), scratch_shapes=ex.sems())(*ex.operands()))


def _adamw(w, g, m, v):
    m = ADAM_B1 * m + (1.0 - ADAM_B1) * g
    v = ADAM_B2 * v + (1.0 - ADAM_B2) * (g * g)
    m_hat = m / (1.0 - ADAM_B1 ** ADAM_STEP)
    v_hat = v / (1.0 - ADAM_B2 ** ADAM_STEP)
    delta = -ADAM_LR * (m_hat / (jnp.sqrt(v_hat) + ADAM_EPS) + ADAM_WD * w)
    return delta, m, v


def _sum_parts(p_ref, rows=slice(None)):
    g = p_ref[0, rows].astype(F32)
    for k in range(1, N_DEV):
        g = g + p_ref[k, rows].astype(F32)
    return g


def adamw_layers(parts, w, m, v, name):
    n_l = len(parts)

    def body(*refs):
        p_refs = refs[:n_l]
        w_ref, m_ref, v_ref, g_out, d_out, m_out, v_out = refs[n_l:]
        for l in range(n_l):
            g = _sum_parts(p_refs[l])
            g_out[l] = g
            d_out[l], m_out[l], v_out[l] = _adamw(w_ref[l], g, m_ref[l], v_ref[l])

    return pl.pallas_call(
        body, name=name, out_shape=[S(w.shape, F32)] * 4, compiler_params=_cp(48),
    )(*parts, w, m, v)


def adamw_segments(parts, segments, w, m, v, name):
    n_p = len(parts)

    def body(*refs):
        p_refs = refs[:n_p]
        w_ref, m_ref, v_ref, g_out, d_out, m_out, v_out = refs[n_p:]
        for part, src, dst, rows in segments:
            g = _sum_parts(p_refs[part], slice(src, src + rows))
            to = slice(dst, dst + rows)
            g_out[to] = g
            d_out[to], m_out[to], v_out[to] = _adamw(w_ref[to], g, m_ref[to], v_ref[to])

    return pl.pallas_call(
        body, name=name, out_shape=[S(w.shape, F32)] * 4, compiler_params=_cp(48),
    )(*parts, w, m, v)


SMALL_LAYER = ("g_mix", "A_re", "A_im", "log_dt", "B_re", "B_im", "C_re", "C_im", "D_skip", "b_glu", "w_pool",
               "pool_scale", "sgu_ln_g", "sgu_ln_b", "w_spatial", "b_spatial", "g_ffn")
BIG_NAMES = ("w_in", "w_glu", "w_out", "w_gate", "w_up", "w_down")
COLUMN_SHARDED = ("w_in", "w_gate", "w_up")
WEIGHT_ORDER = ("g_mix", "w_in", "A_re", "A_im", "log_dt", "B_re", "B_im", "C_re", "C_im", "D_skip", "w_glu", "b_glu",
                "w_pool", "pool_scale", "sgu_ln_g", "sgu_ln_b", "w_spatial", "b_spatial", "w_out", "g_ffn", "w_gate",
                "w_up", "w_down", "g_final")
SEG = 1024
SMALL_AS_BF16 = ("B_re", "B_im", "C_re", "C_im", "w_pool", "w_spatial")


def _pack(arrays, dtype=F32):
    seg = SEG * 4 // jnp.dtype(dtype).itemsize
    parts = []
    for a in arrays:
        flat = a.reshape(-1).astype(dtype)
        parts.append(jnp.pad(flat, (0, (-flat.shape[0]) % seg)))
    return jnp.concatenate(parts).reshape(-1, 128)


def _state_rows(p):
    return p.reshape(1, D_ST)


def _chan_by_state(p):
    return jnp.transpose(p, (2, 0, 1)).reshape(GRP, D_ST)


def _chan_by_state_c(p):
    return jnp.transpose(p, (1, 0, 2)).reshape(GRP, D_ST)


def kernel(x, g_mix, w_in, A_re, A_im, log_dt, B_re, B_im, C_re, C_im, D_skip, w_glu, b_glu, w_pool, pool_scale, sgu_ln_g, sgu_ln_b, w_spatial, b_spatial, w_out, g_ffn, w_gate, w_up, w_down, g_final, loss_target, m_g_mix, m_w_in, m_A_re, m_A_im, m_log_dt, m_B_re, m_B_im, m_C_re, m_C_im, m_D_skip, m_w_glu, m_b_glu, m_w_pool, m_pool_scale, m_sgu_ln_g, m_sgu_ln_b, m_w_spatial, m_b_spatial, m_w_out, m_g_ffn, m_w_gate, m_w_up, m_w_down, m_g_final, v_g_mix, v_w_in, v_A_re, v_A_im, v_log_dt, v_B_re, v_B_im, v_C_re, v_C_im, v_D_skip, v_w_glu, v_b_glu, v_w_pool, v_pool_scale, v_sgu_ln_g, v_sgu_ln_b, v_w_spatial, v_b_spatial, v_w_out, v_g_ffn, v_w_gate, v_w_up, v_w_down, v_g_final):
    args = dict(locals())
    W = {n: args[n] for n in WEIGHT_ORDER}
    M = {n: args["m_" + n] for n in WEIGHT_ORDER}
    V = {n: args["v_" + n] for n in WEIGHT_ORDER}
    n_layers = g_mix.shape[0]
    x0 = x[0]
    target = loss_target[0]

    def my_rows(name, l):
        w = W[name][l]
        return (w.T if name in COLUMN_SHARDED else w).astype(BF16)

    full_w = [dict() for _ in range(n_layers)]

    def gather_of(*which):
        return _Exchange(gather=[my_rows(n, l) for n, l in which])

    def keep_gathered(which, arrays):
        for (n, l), a in zip(which, arrays):
            full_w[l][n] = a.reshape(-1, a.shape[-1])

    tri = jnp.tril(jnp.ones((CHUNK, CHUNK), bool))
    perm = _interleave_matrices(TT_MIX)
    consts = []
    for l in range(n_layers):
        a_re, a_im = _state_rows(A_re[l]), _state_rows(A_im[l])
        ldt = jnp.repeat(log_dt[l], N_STATE).reshape(1, D_ST)
        b_re_t, b_im_t = _chan_by_state(B_re[l]), _chan_by_state(B_im[l])
        sc, bbd, cbd = s5_prepare(a_re, a_im, ldt, b_re_t, b_im_t, _chan_by_state_c(C_re[l]), _chan_by_state_c(C_im[l]))
        wsm = jnp.where(tri[None], w_spatial[l], 0.0)
        wpool_bd = jnp.zeros((D_POOL, D_POOL), F32)
        for gi in range(len(POOL_WINDOWS)):
            wpool_bd = wpool_bd.at[gi * 64:(gi + 1) * 64, gi * 64:(gi + 1) * 64].set(w_pool[l, gi])
        consts.append(dict(
            disc=(a_re, a_im, ldt, b_re_t, b_im_t), sc=sc, bbd=bbd, cbd=cbd,
            dskip=D_skip[l].reshape(1, D_SSM), bglu=b_glu[l].reshape(1, D_SSM),
            wpool=wpool_bd.astype(BF16), pscale=pool_scale[l].reshape(1, D_POOL),
            lng=sgu_ln_g[l].reshape(1, D_SGU), lnb=sgu_ln_b[l].reshape(1, D_SGU),
            wsm=wsm.astype(BF16), wsmt=jnp.transpose(wsm, (0, 2, 1)).astype(BF16),
            bsp=jnp.repeat(b_spatial[l].T, HEAD_DIM, axis=1),
            gmix=g_mix[l].reshape(1, D), gffn=g_ffn[l].reshape(1, D)))

    def mixer_args(l):
        c = consts[l]
        return (c["bbd"], c["cbd"], c["dskip"], full_w[l]["w_glu"], c["bglu"], c["wpool"], c["pscale"], c["lng"], c["lnb"])

    first_needed = [("w_in", 0)]
    keep_gathered(first_needed, exchange_only(gather_of(*first_needed), "gather_first"))
    carried_fwd = {
        ("inproj", 0): [("w_glu", 0), ("w_out", 0)],
        ("mixer", 0): [("w_gate", 0), ("w_up", 0), ("w_down", 0)],
        ("ffn", 0): [("w_in", 1), ("w_glu", 1), ("w_out", 1), ("w_gate", 1)],
        ("mixer", 1): [("w_up", 1), ("w_down", 1)],
    }

    def carried(kind, l):
        which = carried_fwd.get((kind, l))
        return which, (gather_of(*which) if which else None)

    saved = []
    xl = x0
    for l in range(n_layers):
        c, fw = consts[l], full_w[l]
        which, ex = carried("inproj", l)
        (h, z), got = inproj_fwd(xl, c["gmix"], fw["w_in"], f"inproj_fwd_{l}", ex)
        keep_gathered(which or [], got)
        which, ex = carried("mixer", l)
        (ycat, hs, ys), got = mixer_fwd(z, c["sc"], *mixer_args(l), c["wsm"], c["bsp"], perm, f"mixer_fwd_{l}", ex)
        keep_gathered(which or [], got)
        which, ex = carried("ffn", l)
        head = (target, g_final.reshape(1, D)) if l == n_layers - 1 else None
        res, got = ffn_fwd(xl, ycat, fw["w_out"], c["gffn"], fw["w_gate"], fw["w_up"], fw["w_down"], f"ffn_fwd_{l}", ex, head)
        keep_gathered(which or [], got)
        x1, h2, gate, up, act, x2 = res[:6]
        saved.append(dict(x=xl, h=h, z=z, ycat=ycat, hs=hs, ys=ys, x1=x1, h2=h2, gate=gate, up=up, act=act))
        xl = x2
    dx, loss_vec, d_gfinal = xl, res[6], res[7]

    recv_big = {}
    recv_small = [None] * (2 * n_layers)

    def keep_received(which, arrays):
        for key, a in zip(which, arrays):
            if key[0] == "small":
                recv_small[key[1]] = a
            elif key[0] == "small16":
                recv_small[n_layers + key[1]] = a
            else:
                recv_big[key] = a

    pending = None
    for l in reversed(range(n_layers)):
        c, fw, sv = consts[l], full_w[l], saved[l]
        (dgu, dx2b, dx1, dx1b, d_gffn), got = ffn_bwd(dx, sv["x1"], sv["gate"], sv["up"], c["gffn"], fw["w_gate"], fw["w_up"],
                                                     fw["w_down"], f"ffn_bwd_{l}", pending[1] if pending else None)
        if pending:
            keep_received(pending[0], got)
        g_gu = wgrad(dgu, sv["h2"], D_FF // 2, f"wgrad_gate_up_{l}")
        g_down = wgrad(sv["act"], dx2b, D_FF // 2, f"wgrad_down_{l}")
        g_out = wgrad(sv["ycat"], dx1b, D, f"wgrad_out_{l}")
        ffn_rows = D_FF // N_DEV
        ex = _Exchange(scatter=[(g_gu, 0, ffn_rows), (g_gu, D_FF, ffn_rows), (g_down, 0, ffn_rows), (g_out, 0, D // N_DEV)])
        (dzb, da, dbt, dct, dd, dbglu, dwglu, dwpool, dpscale, dlng, dlnb, dws, dbsp), got = mixer_bwd(
            dx1b, sv["z"], sv["hs"], sv["ys"], fw["w_out"], c["sc"], *mixer_args(l), c["wsm"], c["wsmt"], c["bsp"],
            perm, f"mixer_bwd_{l}", ex)
        keep_received([("w_gate", l), ("w_up", l), ("w_down", l), ("w_out", l)], got)
        dx, d_gmix = inproj_bwd(dzb, sv["x"], c["gmix"], fw["w_in"], dx1)
        d_are, d_aim, d_ldt, d_bre_t, d_bim_t = s5_param_bwd(*c["disc"], da, dbt)
        small = dict(
            g_mix=d_gmix.reshape(D), A_re=d_are.reshape(N_GRP, N_STATE), A_im=d_aim.reshape(N_GRP, N_STATE),
            log_dt=d_ldt[0, :N_GRP],
            B_re=jnp.transpose(d_bre_t.reshape(GRP, N_GRP, N_STATE), (1, 0, 2)),
            B_im=jnp.transpose(d_bim_t.reshape(GRP, N_GRP, N_STATE), (1, 0, 2)),
            C_re=jnp.transpose(dct[:, :D_ST].reshape(GRP, N_GRP, N_STATE), (1, 0, 2)),
            C_im=-jnp.transpose(dct[:, D_ST:].reshape(GRP, N_GRP, N_STATE), (1, 0, 2)),
            D_skip=dd.reshape(D_SSM), b_glu=dbglu.reshape(D_SSM),
            w_pool=jnp.stack([dwpool[gi * 64:(gi + 1) * 64, gi * 64:(gi + 1) * 64] for gi in range(len(POOL_WINDOWS))]),
            pool_scale=dpscale.reshape(D_POOL), sgu_ln_g=dlng.reshape(D_SGU), sgu_ln_b=dlnb.reshape(D_SGU),
            w_spatial=dws, b_spatial=dbsp[:, :HEADS].T, g_ffn=d_gffn.reshape(D))
        packed = [small[n] for n in SMALL_LAYER if n not in SMALL_AS_BF16]
        packed += [d_gfinal.reshape(D), loss_vec.reshape(D)] if l == 0 else []
        packed16 = [small[n] for n in SMALL_LAYER if n in SMALL_AS_BF16]
        small_entries = dict(gather=[_pack(packed), _pack(packed16, BF16)], scatter=[(dwglu.astype(BF16), 0, D_SSM // N_DEV)])
        if l > 0:
            g_in = wgrad(dzb, sv["h"], D_IN, f"wgrad_in_{l}")
            small_entries["scatter"].append((g_in, 0, D_IN // N_DEV))
            pending = ([("small", l), ("small16", l), ("w_glu", l), ("w_in", l)], _Exchange(**small_entries))
        else:
            g_in, got = wgrad(dzb, sv["h"], D_IN, f"wgrad_in_{l}", tk=TK_WGRAD // 4, exchange=_Exchange(**small_entries))
            keep_received([("small", l), ("small16", l), ("w_glu", l)], got)
            pending = ([("w_in", l)], _Exchange(scatter=[(g_in, 0, D_IN // N_DEV)]))
    grad_x = dx
    keep_received(pending[0], exchange_only(pending[1], "exchange_last"))

    out = {}
    for n in BIG_NAMES:
        tr = (lambda a: jnp.transpose(a, (0, 2, 1))) if n in COLUMN_SHARDED else (lambda a: a)
        res = adamw_layers([recv_big[(n, l)] for l in range(n_layers)], tr(W[n]), tr(M[n]), tr(V[n]), f"adamw_{n}")
        out[n] = [tr(r) for r in res]

    seg_rows = [(-(-math.prod(W[n].shape[1:]) // SEG)) * (SEG // 128) for n in SMALL_LAYER]
    segments, src, src16, dst = [], 0, 0, 0
    for n, rows in zip(SMALL_LAYER, seg_rows):
        if n in SMALL_AS_BF16:
            segments += [(n_layers + l, src16, dst + l * rows, rows) for l in range(n_layers)]
            src16 += rows
        else:
            segments += [(l, src, dst + l * rows, rows) for l in range(n_layers)]
            src += rows
        dst += n_layers * rows
    tile_rows = SEG // 128
    segments += [(0, src, dst, tile_rows), (0, src + tile_rows, dst + tile_rows, tile_rows)]

    channel_major = ("B_re", "B_im")

    def pack_params(P):
        parts = []
        for n, rows in zip(SMALL_LAYER, seg_rows):
            flat = (jnp.swapaxes(P[n], -1, -2) if n in channel_major else P[n]).reshape(n_layers, -1)
            parts.append(jnp.pad(flat, ((0, 0), (0, rows * 128 - flat.shape[1]))).reshape(-1))
        return jnp.concatenate(parts + [P["g_final"], jnp.zeros((SEG,), F32)]).reshape(-1, 128)

    res = adamw_segments(recv_small, segments, pack_params(W), pack_params(M), pack_params(V), "adamw_small")
    for j in range(4):
        flat, off = res[j].reshape(-1), 0
        for n, rows in zip(SMALL_LAYER, seg_rows):
            size = math.prod(W[n].shape[1:])
            piece = flat[off:off + n_layers * rows * 128].reshape(n_layers, rows * 128)[:, :size]
            if n in channel_major:
                piece = jnp.swapaxes(piece.reshape(W[n].shape[:-2] + W[n].shape[:-3:-1]), -1, -2)
            else:
                piece = piece.reshape(W[n].shape)
            out.setdefault(n, []).append(piece)
            off += n_layers * rows * 128
        out.setdefault("g_final", []).append(flat[off:off + D])
        if j == 0:
            loss = (0.5 / D) * jnp.sum(flat[off + SEG:off + SEG + D])

    return (loss, grad_x[None], *[out[n][0] for n in WEIGHT_ORDER], *[out[n][1] for n in WEIGHT_ORDER],
            *[out[n][2] for n in WEIGHT_ORDER], *[out[n][3] for n in WEIGHT_ORDER])
```

```python
import math

import jax
import jax.numpy as jnp
from jax import lax
from jax.experimental import pallas as pl
from jax.experimental.pallas import tpu as pltpu

F32 = jnp.float32
BF16 = jnp.bfloat16
S = jax.ShapeDtypeStruct

N_DEV = 8
D = 1024
D_SSM = 384
N_GRP = 24
GRP = 16
N_STATE = 64
D_ST = N_GRP * N_STATE
D_POOL = 256
POOL_WINDOWS = (2, 4, 8, 16)
HALO = 16
D_SGU = 384
HEADS = 6
HEAD_DIM = 64
CHUNK = 128
D_IN = 1408
D_FF = 2816
EPS = 1e-6
SCAN_BLK = 8
BF16_TILE_ROWS = 16

ADAM_LR = 0.001
ADAM_B1 = 0.9
ADAM_B2 = 0.999
ADAM_EPS = 1e-08
ADAM_WD = 0.01
ADAM_STEP = 10

GELU_C0 = math.sqrt(2.0 / math.pi)
GELU_C1 = 0.044715

TT_MIX = 256
SEG_LEN = TT_MIX // SCAN_BLK
TT_FFN = 256
TT_PROJ = 1024
TK_WGRAD = 2048
VMEM_MB = 2 ** 20


def _cp(vmem_mb, grid_dims=0):
    kw = dict(vmem_limit_bytes=int(vmem_mb * VMEM_MB))
    if grid_dims:
        kw["dimension_semantics"] = ("arbitrary",) * grid_dims
    return pltpu.CompilerParams(**kw)


def _row(tt, n):
    return pl.BlockSpec((tt, n), lambda i: (i, 0))


def _full(shape):
    nd = len(shape)
    return pl.BlockSpec(shape, lambda *_: (0,) * nd)


def _nn(a, b):
    return jnp.dot(a, b, preferred_element_type=F32)


def _nt(a, b):
    return lax.dot_general(a, b, (((1,), (1,)), ((), ())), preferred_element_type=F32)


def _tn(a, b):
    return lax.dot_general(a, b, (((0,), (0,)), ((), ())), preferred_element_type=F32)


def _rowsum(x):
    return jnp.sum(x, axis=0, keepdims=True)


def _rms(x):
    r = lax.rsqrt(jnp.mean(x * x, axis=-1, keepdims=True) + EPS)
    return x * r, r


def _rms_bwd(dy, xn, r, g):
    dyg = dy * g
    return r * (dyg - xn * jnp.mean(dyg * xn, axis=-1, keepdims=True))


def _gelu(x):
    s = jax.nn.sigmoid(x * (2.0 * GELU_C0 + (2.0 * GELU_C0 * GELU_C1) * (x * x)))
    return x * s, s


def _gelu_grad(x, s):
    return s * (1.0 + x * (1.0 - s) * (2.0 * GELU_C0 + (6.0 * GELU_C0 * GELU_C1) * (x * x)))


def _discretise(a_re, a_im, ldt, b_re, b_im):
    dt = jnp.exp(ldt)
    mag = jnp.exp(a_re * dt)
    ar = mag * jnp.cos(a_im * dt)
    ai = mag * jnp.sin(a_im * dt)
    den = a_re * a_re + a_im * a_im
    f_re = ((ar - 1.0) * a_re + ai * a_im) / den
    f_im = (ai * a_re - (ar - 1.0) * a_im) / den
    bb_re = f_re * b_re - f_im * b_im
    bb_im = f_re * b_im + f_im * b_re
    return ar, ai, bb_re, bb_im


def _group_mask(rows, cols):
    r = lax.broadcasted_iota(jnp.int32, (rows, cols), 0) // GRP
    c = lax.broadcasted_iota(jnp.int32, (rows, cols), 1)
    c = jnp.where(c >= D_ST, c - D_ST, c) // N_STATE
    return r == c


def s5_prepare(a_re, a_im, ldt, b_re_t, b_im_t, c_re_t, c_im_t):
    def body(are_ref, aim_ref, ldt_ref, bre_ref, bim_ref, cre_ref, cim_ref, sc_ref, bbd_ref, cbd_ref):
        ar, ai, bb_re, bb_im = _discretise(are_ref[...], aim_ref[...], ldt_ref[...], bre_ref[...], bim_ref[...])
        mask = _group_mask(D_SSM, 2 * D_ST)
        bb = jnp.concatenate([jnp.tile(bb_re, (N_GRP, 1)), jnp.tile(bb_im, (N_GRP, 1))], axis=1)
        bbd_ref[...] = jnp.where(mask, bb, 0.0).astype(BF16)
        cc = jnp.concatenate([jnp.tile(cre_ref[...], (N_GRP, 1)), -jnp.tile(cim_ref[...], (N_GRP, 1))], axis=1)
        cbd_ref[...] = jnp.where(mask, cc, 0.0).astype(BF16)
        pr, pi = ar, ai
        for _ in range(SEG_LEN - 1):
            pr, pi = pr * ar - pi * ai, pr * ai + pi * ar
        for k, v in enumerate((ar, ai, pr, pi)):
            sc_ref[8 * k:8 * k + 8, :] = jnp.broadcast_to(v, (SCAN_BLK, D_ST))

    return pl.pallas_call(
        body, name="s5_prepare",
        out_shape=[S((32, D_ST), F32), S((D_SSM, 2 * D_ST), BF16), S((D_SSM, 2 * D_ST), BF16)],
        compiler_params=_cp(40),
    )(a_re, a_im, ldt, b_re_t, b_im_t, c_re_t, c_im_t)


def s5_param_bwd(a_re, a_im, ldt, b_re_t, b_im_t, da, dbt):
    def body(are_ref, aim_ref, ldt_ref, bre_ref, bim_ref, da_ref, dbt_ref, o_are, o_aim, o_ldt, o_bre, o_bim):
        _, vjp = jax.vjp(_discretise, are_ref[...], aim_ref[...], ldt_ref[...], bre_ref[...], bim_ref[...])
        da = da_ref[...]
        dbt = dbt_ref[...]
        g_are, g_aim, g_ldt, g_bre, g_bim = vjp((da[:, :D_ST], da[:, D_ST:], dbt[:, :D_ST], dbt[:, D_ST:]))
        o_are[...] = g_are
        o_aim[...] = g_aim
        o_bre[...] = g_bre
        o_bim[...] = g_bim
        grp = lax.broadcasted_iota(jnp.int32, (1, D_ST), 1) // N_STATE
        lane = lax.broadcasted_iota(jnp.int32, (1, 128), 1)
        out = jnp.zeros((1, 128), F32)
        for g in range(N_GRP):
            out = jnp.where(lane == g, jnp.sum(jnp.where(grp == g, g_ldt, 0.0), axis=1, keepdims=True), out)
        o_ldt[...] = out

    return pl.pallas_call(
        body, name="s5_param_bwd",
        out_shape=[S((1, D_ST), F32), S((1, D_ST), F32), S((1, 128), F32), S((GRP, D_ST), F32), S((GRP, D_ST), F32)],
        compiler_params=_cp(16),
    )(a_re, a_im, ldt, b_re_t, b_im_t, da, dbt)


_CH = ((0, 256), (256, D_SSM))
_ST = ((0, 1024), (1024, D_ST))


def _bd_expand(xb, w_ref, out_ref):
    for (c0, c1), (s0, s1) in zip(_CH, _ST):
        for half in (0, D_ST):
            out_ref[:, half + s0:half + s1] = _nn(xb[:, c0:c1], w_ref[c0:c1, half + s0:half + s1])


def _bd_contract(hb, w_ref):
    parts = []
    for (c0, c1), (s0, s1) in zip(_CH, _ST):
        parts.append(_nt(hb[:, s0:s1], w_ref[c0:c1, s0:s1]) + _nt(hb[:, D_ST + s0:D_ST + s1], w_ref[c0:c1, D_ST + s0:D_ST + s1]))
    return jnp.concatenate(parts, axis=1)


def _bd_accumulate(acc_ref, xb, hb):
    for j in range(N_GRP // 4):
        ch = slice(4 * GRP * j, 4 * GRP * (j + 1))
        for half in (0, D_ST):
            st = slice(half + 4 * N_STATE * j, half + 4 * N_STATE * (j + 1))
            acc_ref[ch, st] += _tn(xb[:, ch], hb[:, st])


def _interleave_matrices(tt):
    r = lax.broadcasted_iota(jnp.int32, (tt, tt), 0)
    t = lax.broadcasted_iota(jnp.int32, (tt, tt), 1)
    p = (t == (r % SCAN_BLK) * (tt // SCAN_BLK) + r // SCAN_BLK).astype(BF16)
    return p, p.T


def _interleave_f32(p, x):
    hi = x.astype(BF16)
    lo = (x - hi.astype(F32)).astype(BF16)
    return _nn(p, hi) + _nn(p, lo)


def _scan_tile(buf_ref, sc_ref, carry_ref, n_blk, reverse, on_block=None):
    ar = sc_ref[0:8, :]
    ai = -sc_ref[8:16, :] if reverse else sc_ref[8:16, :]

    def rows(i):
        blk = (n_blk - 1 - i) if reverse else i
        return pl.ds(pl.multiple_of(blk * SCAN_BLK, SCAN_BLK), SCAN_BLK)

    def local(i, x):
        xr, xi = x
        r = rows(i)
        xr, xi = buf_ref[r, 0:D_ST] + ar * xr - ai * xi, buf_ref[r, D_ST:2 * D_ST] + ar * xi + ai * xr
        buf_ref[r, 0:D_ST] = xr
        buf_ref[r, D_ST:2 * D_ST] = xi
        return xr, xi

    zero = jnp.zeros((SCAN_BLK, D_ST), F32)
    end_r, end_i = lax.fori_loop(0, n_blk, local, (zero, zero), unroll=True)

    seg_r = sc_ref[16:17, :]
    seg_i = -sc_ref[24:25, :] if reverse else sc_ref[24:25, :]
    cr, ci = carry_ref[0:1, 0:D_ST], carry_ref[0:1, D_ST:2 * D_ST]
    sub = lax.broadcasted_iota(jnp.int32, (SCAN_BLK, D_ST), 0)
    in_r, in_i = zero, zero
    for s in (reversed(range(SCAN_BLK)) if reverse else range(SCAN_BLK)):
        in_r = jnp.where(sub == s, cr, in_r)
        in_i = jnp.where(sub == s, ci, in_i)
        cr, ci = end_r[s:s + 1, :] + seg_r * cr - seg_i * ci, end_i[s:s + 1, :] + seg_r * ci + seg_i * cr
    carry_ref[0:1, 0:D_ST] = cr
    carry_ref[0:1, D_ST:2 * D_ST] = ci

    dr, di = in_r, in_i
    for i in range(n_blk):
        blk = (n_blk - 1 - i) if reverse else i
        r = slice(blk * SCAN_BLK, (blk + 1) * SCAN_BLK)
        dr, di = ar * dr - ai * di, ar * di + ai * dr
        xr, xi = buf_ref[r, 0:D_ST] + dr, buf_ref[r, D_ST:2 * D_ST] + di
        buf_ref[r, 0:D_ST] = xr
        buf_ref[r, D_ST:2 * D_ST] = xi
        if on_block is not None:
            on_block(blk, xr, xi)


def _lane_windows(n):
    lane = lax.broadcasted_iota(jnp.int32, (1, n), 1)
    return lane // (D_POOL // len(POOL_WINDOWS))


def _select_window(grp, s2, s4, s8, s16):
    return jnp.where(grp == 0, s2, jnp.where(grp == 1, s4, jnp.where(grp == 2, s8, s16)))


def _pool_fwd(pbuf_ref, zb, halo, tile_idx, tt):
    pbuf_ref[0:HALO, :] = halo
    pbuf_ref[HALO:HALO + tt, :] = zb
    x = pbuf_ref[...]
    s2 = x + pltpu.roll(x, 1, axis=0)
    s4 = s2 + pltpu.roll(s2, 2, axis=0)
    s8 = s4 + pltpu.roll(s4, 4, axis=0)
    s16 = s8 + pltpu.roll(s8, 8, axis=0)
    grp = _lane_windows(D_POOL)
    win = _select_window(grp, s2, s4, s8, s16)[HALO:HALO + tt, :]
    width = _select_window(grp, 2.0, 4.0, 8.0, 16.0).astype(F32)
    pos = (tile_idx * tt + 1 + lax.broadcasted_iota(jnp.int32, (tt, 1), 0)).astype(F32)
    cnt = jnp.minimum(pos, width)
    return win / cnt - zb, cnt


def _sgu_fwd(zu, zv, lng, lnb, wsm_ref, bsp, mix_ref, tt):
    u, tu = _gelu(zu)
    v, tv = _gelu(zv)
    mu = jnp.mean(v, axis=-1, keepdims=True)
    vc = v - mu
    rstd = lax.rsqrt(jnp.mean(vc * vc, axis=-1, keepdims=True) + EPS)
    vhat = vc * rstd
    vnb = (vhat * lng + lnb).astype(BF16)
    _head_mix(wsm_ref, vnb, mix_ref, tt, bsp)
    return u, tu, tv, vhat, rstd, vnb


def _chunk_pairs(tt):
    n_ch = tt // CHUNK
    return [list(range(c, min(c + 2, n_ch))) for c in range(0, n_ch, 2)]


def _head_cols(xb, chunks, h):
    return jnp.concatenate([xb[c * CHUNK:(c + 1) * CHUNK, h * HEAD_DIM:(h + 1) * HEAD_DIM] for c in chunks], axis=1)


def _head_mix(w_ref, xb, out_ref, tt, add=None):
    for chunks in _chunk_pairs(tt):
        per_head = [_nn(w_ref[h], _head_cols(xb, chunks, h)) for h in range(HEADS)]
        for k, c in enumerate(chunks):
            block = jnp.concatenate([r[:, k * HEAD_DIM:(k + 1) * HEAD_DIM] for r in per_head], axis=1)
            out_ref[c * CHUNK:(c + 1) * CHUNK, :] = block if add is None else block + add


def mixer_fwd(z, sc, bbd, cbd, dskip, wglu, bglu, wpool, pscale, lng, lnb, wsm, bsp, perm, name, exchange=None):
    T = z.shape[0]
    tt = TT_MIX
    n_tiles = T // tt

    def body(z_ref, sc_ref, bbd_ref, cbd_ref, dskip_ref, wglu_ref, bglu_ref, wpool_ref, pscale_ref, lng_ref, lnb_ref,
             wsm_ref, bsp_ref, p_ref, pt_ref, ycat_ref, hs_ref, ys_ref, carry_ref, halo_ref, pbuf_ref, mix_ref):
        i = pl.program_id(0)

        @pl.when(i == 0)
        def _():
            carry_ref[...] = jnp.zeros_like(carry_ref)
            halo_ref[...] = jnp.zeros_like(halo_ref)

        za = z_ref[:, 0:D_SSM]
        p, pt = p_ref[...], pt_ref[...]
        za = _interleave_f32(p, za)
        _bd_expand(za.astype(BF16), bbd_ref, hs_ref)
        _scan_tile(hs_ref, sc_ref, carry_ref, tt // SCAN_BLK, reverse=False)
        y = _bd_contract(hs_ref[...].astype(BF16), cbd_ref) + dskip_ref[...] * za
        ys_ref[...] = y
        g, _ = _gelu(y)
        q = _nn(g.astype(BF16), wglu_ref[...]) + bglu_ref[...]
        ycat_ref[:, 0:D_SSM] = _nn(pt, (g * jax.nn.sigmoid(q)).astype(BF16)).astype(BF16)
        zb = z_ref[:, D_SSM:D_SSM + D_POOL]
        pooled, _ = _pool_fwd(pbuf_ref, zb, halo_ref[...], i, tt)
        halo_ref[...] = zb[tt - HALO:tt, :]
        ycat_ref[:, D_SSM:D_SSM + D_POOL] = (_nn(pooled.astype(BF16), wpool_ref[...]) * pscale_ref[...]).astype(BF16)
        zu = z_ref[:, D_SSM + D_POOL:D_SSM + D_POOL + D_SGU]
        zv = z_ref[:, D_SSM + D_POOL + D_SGU:D_IN]
        u, _, _, _, _, _ = _sgu_fwd(zu, zv, lng_ref[...], lnb_ref[...], wsm_ref, bsp_ref[...], mix_ref, tt)
        ycat_ref[:, D_SSM + D_POOL:D] = (u * mix_ref[...]).astype(BF16)

    return _pallas(
        body, name=name, grid=(n_tiles,),
        in_specs=[_row(tt, D_IN), _full((32, D_ST)), _full((D_SSM, 2 * D_ST)), _full((D_SSM, 2 * D_ST)),
                  _full((1, D_SSM)), _full((D_SSM, D_SSM)), _full((1, D_SSM)), _full((D_POOL, D_POOL)),
                  _full((1, D_POOL)), _full((1, D_SGU)), _full((1, D_SGU)), _full((HEADS, CHUNK, CHUNK)),
                  _full((CHUNK, D_SGU)), _full((tt, tt)), _full((tt, tt))],
        out_specs=[_row(tt, D), _row(tt, 2 * D_ST), _row(tt, D_SSM)],
        out_shape=[S((T, D), BF16), S((T, 2 * D_ST), F32), S((T, D_SSM), F32)],
        scratch_shapes=[pltpu.VMEM((SCAN_BLK, 2 * D_ST), F32), pltpu.VMEM((HALO, D_POOL), F32),
                        pltpu.VMEM((tt + HALO, D_POOL), F32), pltpu.VMEM((tt, D_SGU), F32)],
        vmem_mb=48, operands=(z, sc, bbd, cbd, dskip, wglu, bglu, wpool, pscale, lng, lnb, wsm, bsp, *perm),
        exchange=exchange)


def mixer_bwd(dx1b, z, hs, ys, wout, sc, bbd, cbd, dskip, wglu, bglu, wpool, pscale, lng, lnb, wsm, wsmt, bsp, perm, name,
              exchange=None):
    T = z.shape[0]
    tt = TT_MIX
    n_tiles = T // tt

    def rev(i):
        return n_tiles - 1 - i

    def body(dx_ref, z_ref, zprev_ref, hs_ref, hsprev_ref, ys_ref, wout_ref, sc_ref, bbd_ref, cbd_ref, dskip_ref,
             wglu_ref, bglu_ref, wpool_ref, pscale_ref, lng_ref, lnb_ref, wsm_ref, wsmt_ref, bsp_ref, p_ref, pt_ref,
             dz_ref, o_da, o_dbt, o_dct, o_dd, o_dbglu, o_dwglu, o_dwpool, o_dpscale, o_dlng, o_dlnb, o_dws, o_dbsp,
             gbuf_ref, carry_ref, accb_ref, accc_ref, ehalo_ref, pbuf_ref, mix_ref, dvn_ref, accw_ref, accm_ref):
        i = pl.program_id(0)
        tile = rev(i)

        @pl.when(i == 0)
        def _():
            carry_ref[...] = jnp.zeros_like(carry_ref)
            accb_ref[...] = jnp.zeros_like(accb_ref)
            accc_ref[...] = jnp.zeros_like(accc_ref)
            for o in (o_da, o_dd, o_dbglu, o_dwglu):
                o[...] = jnp.zeros_like(o)

        p, pt = p_ref[...], pt_ref[...]
        dxb = dx_ref[...]
        d_a = _nt(_nn(p, dxb).astype(BF16), wout_ref[0:D_SSM, :])
        za = _interleave_f32(p, z_ref[:, 0:D_SSM])
        first_tile = (tile > 0).astype(F32)

        y = ys_ref[...]
        g, tg = _gelu(y)
        gb = g.astype(BF16)
        sg = jax.nn.sigmoid(_nn(gb, wglu_ref[...]) + bglu_ref[...])
        dq = d_a * g * sg * (1.0 - sg)
        dqb = dq.astype(BF16)
        o_dbglu[...] += _rowsum(dq)
        o_dwglu[...] += _tn(gb, dqb)
        dy = (d_a * sg + _nt(dqb, wglu_ref[...])) * _gelu_grad(y, tg)
        o_dd[...] += _rowsum(dy * za)
        dyb = dy.astype(BF16)
        _bd_expand(dyb, cbd_ref, gbuf_ref)
        hprev = hsprev_ref[SCAN_BLK - 1:SCAN_BLK, :] * first_tile
        sub = lax.broadcasted_iota(jnp.int32, (SCAN_BLK, 1), 0)
        edge = jnp.where(sub == 0, hprev, pltpu.roll(hs_ref[tt - SCAN_BLK:tt, :], 1, axis=0))
        da = [jnp.zeros((SCAN_BLK, D_ST), F32), jnp.zeros((SCAN_BLK, D_ST), F32)]

        def da_terms(blk, gr, gi):
            before = edge if blk == 0 else hs_ref[(blk - 1) * SCAN_BLK:blk * SCAN_BLK, :]
            hr, hi = before[:, 0:D_ST], before[:, D_ST:]
            da[0] = da[0] + (gr * hr + gi * hi)
            da[1] = da[1] + (gi * hr - gr * hi)

        _scan_tile(gbuf_ref, sc_ref, carry_ref, tt // SCAN_BLK, reverse=True, on_block=da_terms)
        o_da[:, 0:D_ST] += _rowsum(da[0])
        o_da[:, D_ST:] += _rowsum(da[1])
        gtb = gbuf_ref[...].astype(BF16)
        dza = (dy * dskip_ref[...] + _bd_contract(gtb, bbd_ref)).astype(BF16)
        dz_ref[:, 0:D_SSM] = _nn(pt, dza).astype(BF16)
        _bd_accumulate(accb_ref, za.astype(BF16), gtb)
        _bd_accumulate(accc_ref, dyb, hs_ref[...].astype(BF16))
        d_bc = _nt(dx_ref[...], wout_ref[D_SSM:D, :])
        _pool_sgu_bwd(i, n_tiles, tile, tt, d_bc, z_ref, zprev_ref, wpool_ref, pscale_ref, lng_ref, lnb_ref, wsm_ref,
                      wsmt_ref, bsp_ref, dz_ref, o_dwpool, o_dpscale, o_dlng, o_dlnb, o_dws, o_dbsp,
                      ehalo_ref, pbuf_ref, mix_ref, dvn_ref, accw_ref, accm_ref)

        @pl.when(i == n_tiles - 1)
        def _():
            mask = _group_mask(D_SSM, 2 * D_ST)
            for acc_ref, o_ref in ((accb_ref, o_dbt), (accc_ref, o_dct)):
                fold = jnp.zeros((GRP, 2 * D_ST), F32)
                for gidx in range(N_GRP):
                    rows = slice(gidx * GRP, (gidx + 1) * GRP)
                    fold = fold + jnp.where(mask[rows, :], acc_ref[rows, :], 0.0)
                o_ref[...] = fold

    def rowr(n):
        return pl.BlockSpec((tt, n), lambda i: (rev(i), 0))

    zprev_spec = pl.BlockSpec((HALO, D_IN), lambda i: (jnp.maximum(rev(i) * (tt // HALO) - 1, 0), 0))
    hsprev_spec = pl.BlockSpec((SCAN_BLK, 2 * D_ST), lambda i: (jnp.maximum(rev(i) * (tt // SCAN_BLK) - 1, 0), 0))
    small = [S((1, 2 * D_ST), F32), S((GRP, 2 * D_ST), F32), S((GRP, 2 * D_ST), F32), S((1, D_SSM), F32),
             S((1, D_SSM), F32), S((D_SSM, D_SSM), F32), S((D_POOL, D_POOL), F32), S((1, D_POOL), F32),
             S((1, D_SGU), F32), S((1, D_SGU), F32), S((HEADS, CHUNK, CHUNK), F32), S((CHUNK, 128), F32)]
    return _pallas(
        body, name=name, grid=(n_tiles,),
        in_specs=[rowr(D), rowr(D_IN), zprev_spec, rowr(2 * D_ST), hsprev_spec, rowr(D_SSM), _full((D, D)),
                  _full((32, D_ST)), _full((D_SSM, 2 * D_ST)), _full((D_SSM, 2 * D_ST)), _full((1, D_SSM)),
                  _full((D_SSM, D_SSM)), _full((1, D_SSM)), _full((D_POOL, D_POOL)), _full((1, D_POOL)),
                  _full((1, D_SGU)), _full((1, D_SGU)), _full((HEADS, CHUNK, CHUNK)), _full((HEADS, CHUNK, CHUNK)),
                  _full((CHUNK, D_SGU)), _full((tt, tt)), _full((tt, tt))],
        out_specs=[rowr(D_IN)] + [_full(s.shape) for s in small],
        out_shape=[S((T, D_IN), BF16)] + small,
        scratch_shapes=[pltpu.VMEM((tt, 2 * D_ST), F32), pltpu.VMEM((SCAN_BLK, 2 * D_ST), F32),
                        pltpu.VMEM((D_SSM, 2 * D_ST), F32), pltpu.VMEM((D_SSM, 2 * D_ST), F32),
                        pltpu.VMEM((HALO, D_POOL), F32), pltpu.VMEM((tt + HALO, D_POOL), F32),
                        pltpu.VMEM((tt, D_SGU), F32), pltpu.VMEM((tt, D_SGU), F32),
                        pltpu.VMEM((HEADS, CHUNK, CHUNK), F32), pltpu.VMEM((CHUNK, D_SGU), F32)],
        vmem_mb=56, exchange=exchange,
        operands=(dx1b, z, z, hs, hs, ys, wout, sc, bbd, cbd, dskip, wglu, bglu, wpool, pscale, lng, lnb, wsm, wsmt, bsp, *perm))


def inproj_fwd(x, g, w_t, name, exchange=None):
    T = x.shape[0]
    tt = min(TT_PROJ, T)

    def body(x_ref, g_ref, w_ref, h_ref, z_ref):
        xn, _ = _rms(x_ref[...])
        h = (xn * g_ref[...]).astype(BF16)
        h_ref[...] = h
        z_ref[...] = _nt(h, w_ref[...])

    return _pallas(
        body, name=name, grid=(T // tt,),
        in_specs=[_row(tt, D), _full((1, D)), _full((D_IN, D))],
        out_specs=[_row(tt, D), _row(tt, D_IN)],
        out_shape=[S((T, D), BF16), S((T, D_IN), F32)],
        scratch_shapes=[], vmem_mb=48, operands=(x, g, w_t), exchange=exchange)


def inproj_bwd(dzb, x, g, w_t, dx1):
    T = x.shape[0]
    tt = min(TT_PROJ, T)

    def body(dz_ref, x_ref, g_ref, w_ref, dx1_ref, dx_ref, dg_ref):
        @pl.when(pl.program_id(0) == 0)
        def _():
            dg_ref[...] = jnp.zeros_like(dg_ref)

        dh = _nn(dz_ref[...], w_ref[...])
        xn, r = _rms(x_ref[...])
        dg_ref[...] += _rowsum(dh * xn)
        dx_ref[...] = dx1_ref[...] + _rms_bwd(dh, xn, r, g_ref[...])

    return pl.pallas_call(
        body, name="inproj_bwd", grid=(T // tt,),
        in_specs=[_row(tt, D_IN), _row(tt, D), _full((1, D)), _full((D_IN, D)), _row(tt, D)],
        out_specs=[_row(tt, D), _full((1, D))],
        out_shape=[S((T, D), F32), S((1, D), F32)],
        compiler_params=_cp(48, 1),
    )(dzb, x, g, w_t, dx1)


def _load_weights(pairs, sem):
    @pl.when(pl.program_id(0) == 0)
    def _():
        copies = [pltpu.make_async_copy(src, dst, sem.at[k]) for k, (src, dst) in enumerate(pairs)]
        for cp in copies:
            cp.start()
        for cp in copies:
            cp.wait()


def ffn_fwd(x, ycat, wout, g, wg_t, wu_t, wd, name, exchange=None, head=None):
    T = x.shape[0]
    tt = TT_FFN
    any_spec = pl.BlockSpec(memory_space=pl.ANY)

    def body(*refs):
        if head is None:
            (x_ref, ycat_ref, g_ref, wout_hbm, wg_hbm, wu_hbm, wd_hbm,
             x1_ref, h_ref, gate_ref, up_ref, act_ref, x2_ref, wout_v, wg_v, wu_v, wd_v, sem) = refs
        else:
            (x_ref, ycat_ref, g_ref, t_ref, gf_ref, wout_hbm, wg_hbm, wu_hbm, wd_hbm,
             x1_ref, h_ref, gate_ref, up_ref, act_ref, x2_ref, lvec_ref, dgf_ref, wout_v, wg_v, wu_v, wd_v, sem) = refs
        _ffn_fwd_tile(x_ref, ycat_ref, g_ref, wout_hbm, wg_hbm, wu_hbm, wd_hbm, x1_ref, h_ref, gate_ref, up_ref, act_ref,
                      x2_ref, wout_v, wg_v, wu_v, wd_v, sem)
        if head is not None:
            @pl.when(pl.program_id(0) == 0)
            def _():
                lvec_ref[...] = jnp.zeros_like(lvec_ref)
                dgf_ref[...] = jnp.zeros_like(dgf_ref)

            xn, r = _rms(x2_ref[...])
            gf = gf_ref[...]
            err = xn * gf - t_ref[...]
            lvec_ref[...] += _rowsum(err * err)
            dy = err * (1.0 / D)
            dgf_ref[...] += _rowsum(dy * xn)
            x2_ref[...] = _rms_bwd(dy, xn, r, gf)

    def _ffn_fwd_tile(x_ref, ycat_ref, g_ref, wout_hbm, wg_hbm, wu_hbm, wd_hbm,
                      x1_ref, h_ref, gate_ref, up_ref, act_ref, x2_ref, wout_v, wg_v, wu_v, wd_v, sem):
        _load_weights([(wout_hbm, wout_v), (wg_hbm, wg_v), (wu_hbm, wu_v), (wd_hbm, wd_v)], sem)
        x1 = x_ref[...] + _nn(ycat_ref[...], wout_v[...])
        x1_ref[...] = x1
        xn, _ = _rms(x1)
        h = (xn * g_ref[...]).astype(BF16)
        h_ref[...] = h
        gate = _nt(h, wg_v[...])
        up = _nt(h, wu_v[...])
        gate_ref[...] = gate.astype(BF16)
        up_ref[...] = up.astype(BF16)
        act = (gate * jax.nn.sigmoid(gate) * up).astype(BF16)
        act_ref[...] = act
        x2_ref[...] = x1 + _nn(act, wd_v[...])

    with_head = head is not None
    return _pallas(
        body, name=name, grid=(T // tt,),
        in_specs=[_row(tt, D), _row(tt, D), _full((1, D))] + ([_row(tt, D), _full((1, D))] if with_head else [])
        + [any_spec, any_spec, any_spec, any_spec],
        out_specs=[_row(tt, D), _row(tt, D), _row(tt, D_FF), _row(tt, D_FF), _row(tt, D_FF), _row(tt, D)]
        + ([_full((1, D)), _full((1, D))] if with_head else []),
        out_shape=[S((T, D), F32), S((T, D), BF16), S((T, D_FF), BF16), S((T, D_FF), BF16), S((T, D_FF), BF16),
                   S((T, D), F32)] + ([S((1, D), F32), S((1, D), F32)] if with_head else []),
        scratch_shapes=[pltpu.VMEM((D, D), BF16), pltpu.VMEM((D_FF, D), BF16), pltpu.VMEM((D_FF, D), BF16),
                        pltpu.VMEM((D_FF, D), BF16), pltpu.SemaphoreType.DMA((4,))],
        vmem_mb=56, operands=(x, ycat, g) + (tuple(head) if with_head else ()) + (wout, wg_t, wu_t, wd), exchange=exchange)


def _pool_sgu_bwd(i, n_tiles, tile, tt, d_bc, z_ref, zprev_ref, wpool_ref, pscale_ref, lng_ref, lnb_ref, wsm_ref, wsmt_ref,
                  bsp_ref, dz_ref, o_dwpool, o_dpscale, o_dlng, o_dlnb, o_dws, o_dbsp,
                  ehalo_ref, pbuf_ref, mix_ref, dvn_ref, accw_ref, accm_ref):
    @pl.when(i == 0)
    def _():
        ehalo_ref[...] = jnp.zeros_like(ehalo_ref)
        accw_ref[...] = jnp.zeros_like(accw_ref)
        accm_ref[...] = jnp.zeros_like(accm_ref)
        for o in (o_dwpool, o_dpscale, o_dlng, o_dlnb):
            o[...] = jnp.zeros_like(o)

    d_b = d_bc[:, 0:D_POOL]
    d_c = d_bc[:, D_POOL:D_POOL + D_SGU]
    zb = z_ref[:, D_SSM:D_SSM + D_POOL]
    zu = z_ref[:, D_SSM + D_POOL:D_SSM + D_POOL + D_SGU]
    zv = z_ref[:, D_SSM + D_POOL + D_SGU:D_IN]
    not_first = (tile > 0).astype(F32)

    pooled, cnt = _pool_fwd(pbuf_ref, zb, zprev_ref[:, D_SSM:D_SSM + D_POOL] * not_first, tile, tt)
    pooledb = pooled.astype(BF16)
    mixed = _nn(pooledb, wpool_ref[...])
    o_dpscale[...] += _rowsum(d_b * mixed)
    dmixb = (d_b * pscale_ref[...]).astype(BF16)
    o_dwpool[...] += _tn(pooledb, dmixb)
    dpooled = _nt(dmixb, wpool_ref[...])
    e = dpooled / cnt
    pbuf_ref[0:tt, :] = e
    pbuf_ref[tt:tt + HALO, :] = ehalo_ref[...]
    ehalo_ref[...] = e[0:HALO, :]
    x = pbuf_ref[...]
    n = tt + HALO
    f2 = x + pltpu.roll(x, n - 1, axis=0)
    f4 = f2 + pltpu.roll(f2, n - 2, axis=0)
    f8 = f4 + pltpu.roll(f4, n - 4, axis=0)
    f16 = f8 + pltpu.roll(f8, n - 8, axis=0)
    fwd_sum = _select_window(_lane_windows(D_POOL), f2, f4, f8, f16)[0:tt, :]
    dz_ref[:, D_SSM:D_SSM + D_POOL] = (fwd_sum - dpooled).astype(BF16)

    lng = lng_ref[...]
    u, su, sv, vhat, rstd, vnb = _sgu_fwd(zu, zv, lng, lnb_ref[...], wsm_ref, bsp_ref[...], mix_ref, tt)
    dz_ref[:, D_SSM + D_POOL:D_SSM + D_POOL + D_SGU] = (d_c * mix_ref[...] * _gelu_grad(zu, su)).astype(BF16)
    dmix = d_c * u
    dmixb2 = dmix.astype(BF16)
    for c in range(tt // CHUNK):
        accm_ref[...] += dmix[c * CHUNK:(c + 1) * CHUNK, :]
    for chunks in _chunk_pairs(tt):
        for h in range(HEADS):
            accw_ref[h] += _nt(_head_cols(dmixb2, chunks, h), _head_cols(vnb, chunks, h))
    _head_mix(wsmt_ref, dmixb2, dvn_ref, tt)
    dvn = dvn_ref[...]
    o_dlng[...] += _rowsum(dvn * vhat)
    o_dlnb[...] += _rowsum(dvn)
    dvh = dvn * lng
    dv = rstd * (dvh - jnp.mean(dvh, axis=-1, keepdims=True) - vhat * jnp.mean(dvh * vhat, axis=-1, keepdims=True))
    dz_ref[:, D_SSM + D_POOL + D_SGU:D_IN] = (dv * _gelu_grad(zv, sv)).astype(BF16)

    @pl.when(i == n_tiles - 1)
    def _():
        tri = (lax.broadcasted_iota(jnp.int32, (CHUNK, CHUNK), 0) >= lax.broadcasted_iota(jnp.int32, (CHUNK, CHUNK), 1))
        for h in range(HEADS):
            o_dws[h] = jnp.where(tri, accw_ref[h], 0.0)
        lane = lax.broadcasted_iota(jnp.int32, (1, 128), 1)
        acc = jnp.zeros((CHUNK, 128), F32)
        for h in range(HEADS):
            sh = jnp.sum(accm_ref[:, h * HEAD_DIM:(h + 1) * HEAD_DIM], axis=1, keepdims=True)
            acc = jnp.where(lane == h, sh, acc)
        o_dbsp[...] = acc


def ffn_bwd(dx2, x1, gate, up, g, wg_t, wu_t, wd, name, exchange=None):
    T = x1.shape[0]
    tt = TT_FFN
    any_spec = pl.BlockSpec(memory_space=pl.ANY)

    def body(dx2_ref, x1_ref, gate_ref, up_ref, g_ref, wg_hbm, wu_hbm, wd_hbm,
             dgu_ref, dx2b_ref, dx1_ref, dx1b_ref, dg_ref, wg_v, wu_v, wd_v, sem):
        _load_weights([(wg_hbm, wg_v), (wu_hbm, wu_v), (wd_hbm, wd_v)], sem)

        @pl.when(pl.program_id(0) == 0)
        def _():
            dg_ref[...] = jnp.zeros_like(dg_ref)

        dx2 = dx2_ref[...]
        dx2b = dx2.astype(BF16)
        dx2b_ref[...] = dx2b
        dact = _nt(dx2b, wd_v[...])
        gate = gate_ref[...].astype(F32)
        up = up_ref[...].astype(F32)
        sg = jax.nn.sigmoid(gate)
        dgate = (dact * up * (sg * (1.0 + gate * (1.0 - sg)))).astype(BF16)
        dup = (dact * gate * sg).astype(BF16)
        dgu_ref[:, 0:D_FF] = dgate
        dgu_ref[:, D_FF:2 * D_FF] = dup
        dh = _nn(dgate, wg_v[...]) + _nn(dup, wu_v[...])
        xn, r = _rms(x1_ref[...])
        dg_ref[...] += _rowsum(dh * xn)
        dx1 = dx2 + _rms_bwd(dh, xn, r, g_ref[...])
        dx1_ref[...] = dx1
        dx1b_ref[...] = dx1.astype(BF16)

    return _pallas(
        body, name=name, grid=(T // tt,),
        in_specs=[_row(tt, D), _row(tt, D), _row(tt, D_FF), _row(tt, D_FF), _full((1, D)), any_spec, any_spec, any_spec],
        out_specs=[_row(tt, 2 * D_FF), _row(tt, D), _row(tt, D), _row(tt, D), _full((1, D))],
        out_shape=[S((T, 2 * D_FF), BF16), S((T, D), BF16), S((T, D), F32), S((T, D), BF16), S((1, D), F32)],
        scratch_shapes=[pltpu.VMEM((D_FF, D), BF16), pltpu.VMEM((D_FF, D), BF16), pltpu.VMEM((D_FF, D), BF16),
                        pltpu.SemaphoreType.DMA((3,))],
        vmem_mb=56, operands=(dx2, x1, gate, up, g, wg_t, wu_t, wd), exchange=exchange)


def wgrad(a, b, tm, name, exchange=None, tk=TK_WGRAD):
    T, M = a.shape
    N = b.shape[1]
    tk = min(tk, T)
    n_k = T // tk

    def body(a_ref, b_ref, o_ref, acc_ref):
        k = pl.program_id(1)

        @pl.when(k == 0)
        def _():
            acc_ref[...] = jnp.zeros_like(acc_ref)

        acc_ref[...] += _tn(a_ref[...], b_ref[...])

        @pl.when(k == n_k - 1)
        def _():
            o_ref[...] = acc_ref[...].astype(BF16)

    (out,), got = _pallas(
        body, name=name, grid=(M // tm, n_k),
        in_specs=[pl.BlockSpec((tk, tm), lambda m, k: (k, m)), pl.BlockSpec((tk, N), lambda m, k: (k, 0))],
        out_specs=[pl.BlockSpec((tm, N), lambda m, k: (m, 0))],
        out_shape=[S((M, N), BF16)],
        scratch_shapes=[pltpu.VMEM((tm, N), F32)],
        vmem_mb=48, operands=(a, b), exchange=exchange)
    return out if exchange is None else (out, got)


def _mesh_place():
    x, y, c = lax.axis_index("x"), lax.axis_index("y"), lax.axis_index("c")
    return x, y, c, 4 * x + 2 * y + c


def _peer(x, y, c, k):
    px = 1 - x if k & 4 else x
    py = 1 - y if k & 2 else y
    pc = 1 - c if k & 1 else c
    return (px, py, pc), 4 * px + 2 * py + pc


class _Exchange:
    SAME_CORE = (2, 4, 6)

    def __init__(self, gather=(), scatter=()):
        self.entries = [(a, None, a.shape[0]) for a in gather] + [(a, off, rows) for a, off, rows in scatter]
        self.n_gather = len(gather)

    @property
    def n(self):
        return len(self.entries)

    def operands(self):
        return [e[0] for e in self.entries]

    def out_shapes(self):
        return [S((N_DEV, rows, a.shape[1]), a.dtype) for a, _, rows in self.entries]

    def sems(self):
        return [pltpu.SemaphoreType.DMA((self.n, N_DEV)), pltpu.SemaphoreType.DMA((self.n, N_DEV)),
                pltpu.SemaphoreType.DMA((self.n,))]

    def _src(self, ref, e, idx):
        _, off, rows = self.entries[e]
        if off is None:
            return ref
        return ref.at[pl.ds(pl.multiple_of(off + idx * rows, 16), rows)]

    def _masks(self, e):
        return (1,) + self.SAME_CORE if e < self.n_gather else tuple(range(1, N_DEV))

    def _copy(self, ins, outs, sems, e, k, sending, passing_on=False):
        send_sems, recv_sems, _ = sems
        x, y, c, me = _mesh_place()
        peer, pidx = _peer(x, y, c, k)
        if passing_on:
            return pltpu.make_async_remote_copy(
                src_ref=outs[e].at[pidx], dst_ref=outs[e].at[pidx], send_sem=send_sems.at[e, k | 1],
                recv_sem=recv_sems.at[e, k | 1], device_id=_peer(x, y, c, 1)[0], device_id_type=pl.DeviceIdType.MESH)
        return pltpu.make_async_remote_copy(
            src_ref=self._src(ins[e], e, pidx), dst_ref=outs[e].at[me if sending else pidx], send_sem=send_sems.at[e, k],
            recv_sem=recv_sems.at[e, k], device_id=peer, device_id_type=pl.DeviceIdType.MESH)

    def _local(self, ins, outs, sems):
        me = _mesh_place()[3]
        return [pltpu.make_async_copy(self._src(ins[e], e, me), outs[e].at[me], sems[2].at[e]) for e in range(self.n)]

    def start(self, ins, outs, sems):
        for cp in self._local(ins, outs, sems):
            cp.start()
        for k in range(1, N_DEV):
            for e in range(self.n):
                if k in self._masks(e):
                    self._copy(ins, outs, sems, e, k, True).start()

    def forward(self, ins, outs, sems):
        for k in self.SAME_CORE:
            for e in range(self.n_gather):
                self._copy(ins, outs, sems, e, k, False).wait_recv()
                self._copy(ins, outs, sems, e, k, False, passing_on=True).start()

    def wait(self, ins, outs, sems):
        for k in range(1, N_DEV):
            for e in range(self.n):
                if e >= self.n_gather or k % 2:
                    self._copy(ins, outs, sems, e, k, False).wait_recv()
        for k in range(1, N_DEV):
            for e in range(self.n):
                self._copy(ins, outs, sems, e, k, True).wait_send()
        for cp in self._local(ins, outs, sems):
            cp.wait()


def _pallas(body, *, name, grid, in_specs, out_specs, out_shape, scratch_shapes, vmem_mb, operands, exchange=None,
            aliases=None):
    n_in, n_out, n_scr = len(in_specs), len(out_specs), len(scratch_shapes)
    n_steps = math.prod(grid)
    aliases = aliases or {}
    if exchange is None:
        res = pl.pallas_call(body, name=name, grid=grid, in_specs=in_specs, out_specs=out_specs, out_shape=out_shape,
                             scratch_shapes=scratch_shapes, input_output_aliases=aliases,
                             compiler_params=_cp(vmem_mb, len(grid)))(*operands)
        return list(res), []
    ex = exchange

    def hosted(*refs):
        ins, ex_in = refs[:n_in], refs[n_in:n_in + ex.n]
        outs = refs[n_in + ex.n:n_in + ex.n + n_out]
        ex_out = refs[n_in + ex.n + n_out:n_in + 2 * ex.n + n_out]
        scr = refs[n_in + 2 * ex.n + n_out:]
        sems = scr[n_scr:]
        step = pl.program_id(0)
        for axis in range(1, len(grid)):
            step = step * grid[axis] + pl.program_id(axis)

        @pl.when(step == 0)
        def _():
            ex.start(ex_in, ex_out, sems)

        body(*ins, *outs, *scr[:n_scr])

        if ex.n_gather:
            @pl.when(step == max(n_steps - 1 - max(2, n_steps // 8), 0))
            def _():
                ex.forward(ex_in, ex_out, sems)

        @pl.when(step == n_steps - 1)
        def _():
            ex.wait(ex_in, ex_out, sems)

    any_spec = pl.BlockSpec(memory_space=pl.ANY)
    res = pl.pallas_call(
        hosted, name=name, grid=grid, in_specs=list(in_specs) + [any_spec] * ex.n,
        out_specs=list(out_specs) + [any_spec] * ex.n, out_shape=list(out_shape) + ex.out_shapes(),
        scratch_shapes=list(scratch_shapes) + ex.sems(), input_output_aliases=aliases,
        compiler_params=_cp(vmem_mb, len(grid)),
    )(*operands, *ex.operands())
    return list(res[:n_out]), list(res[n_out:])


def exchange_only(ex, name):
    def body(*refs):
        ins, outs, sems = refs[:ex.n], refs[ex.n:2 * ex.n], refs[2 * ex.n:]
        ex.start(ins, outs, sems)
        ex.forward(ins, outs, sems)
        ex.wait(ins, outs, sems)

    any_spec = pl.BlockSpec(memory_space=pl.ANY)
    return list(pl.pallas_call(body, name=name, in_specs=[any_spec] * ex.n, out_specs=[any_spec] * ex.n,
                               out_shape=ex.out_shapes(), scratch_shapes=ex.sems())(*ex.operands()))


def _adamw(w, g, m, v):
    m = ADAM_B1 * m + (1.0 - ADAM_B1) * g
    v = ADAM_B2 * v + (1.0 - ADAM_B2) * (g * g)
    m_hat = m / (1.0 - ADAM_B1 ** ADAM_STEP)
    v_hat = v / (1.0 - ADAM_B2 ** ADAM_STEP)
    delta = -ADAM_LR * (m_hat / (jnp.sqrt(v_hat) + ADAM_EPS) + ADAM_WD * w)
    return delta, m, v


def _sum_parts(p_ref, rows=slice(None)):
    g = p_ref[0, rows].astype(F32)
    for k in range(1, N_DEV):
        g = g + p_ref[k, rows].astype(F32)
    return g


def adamw_layers(parts, w, m, v, name):
    n_l = len(parts)
    _, r, n = w.shape
    rb = r // 2 if r % (2 * BF16_TILE_ROWS) == 0 else r

    def body(*refs):
        p_refs = refs[:n_l]
        w_ref, m_ref, v_ref, g_out, d_out, m_out, v_out = refs[n_l:]
        for l in range(n_l):
            g = _sum_parts(p_refs[l])
            g_out[l] = g
            d_out[l], m_out[l], v_out[l] = _adamw(w_ref[l], g, m_ref[l], v_ref[l])

    part_spec = pl.BlockSpec((N_DEV, rb, n), lambda i: (0, i, 0))
    w_spec = pl.BlockSpec((n_l, rb, n), lambda i: (0, i, 0))
    return pl.pallas_call(
        body, name=name, grid=(r // rb,), in_specs=[part_spec] * n_l + [w_spec] * 3, out_specs=[w_spec] * 4,
        out_shape=[S(w.shape, F32)] * 4, compiler_params=_cp(48, 1),
    )(*parts, w, m, v)


def adamw_segments(parts, segments, w, m, v, name):
    n_p = len(parts)

    def body(*refs):
        p_refs = refs[:n_p]
        w_ref, m_ref, v_ref, g_out, d_out, m_out, v_out = refs[n_p:]
        for part, src, dst, rows in segments:
            g = _sum_parts(p_refs[part], slice(src, src + rows))
            to = slice(dst, dst + rows)
            g_out[to] = g
            d_out[to], m_out[to], v_out[to] = _adamw(w_ref[to], g, m_ref[to], v_ref[to])

    return pl.pallas_call(
        body, name=name, out_shape=[S(w.shape, F32)] * 4, compiler_params=_cp(48),
    )(*parts, w, m, v)


SMALL_LAYER = ("g_mix", "A_re", "A_im", "log_dt", "B_re", "B_im", "C_re", "C_im", "D_skip", "b_glu", "w_pool",
               "pool_scale", "sgu_ln_g", "sgu_ln_b", "w_spatial", "b_spatial", "g_ffn")
BIG_NAMES = ("w_in", "w_glu", "w_out", "w_gate", "w_up", "w_down")
COLUMN_SHARDED = ("w_in", "w_gate", "w_up")
WEIGHT_ORDER = ("g_mix", "w_in", "A_re", "A_im", "log_dt", "B_re", "B_im", "C_re", "C_im", "D_skip", "w_glu", "b_glu",
                "w_pool", "pool_scale", "sgu_ln_g", "sgu_ln_b", "w_spatial", "b_spatial", "w_out", "g_ffn", "w_gate",
                "w_up", "w_down", "g_final")
SEG = 1024
SMALL_AS_BF16 = ("B_re", "B_im", "C_re", "C_im", "w_pool", "w_spatial")


def _pack(arrays, dtype=F32):
    seg = SEG * 4 // jnp.dtype(dtype).itemsize
    parts = []
    for a in arrays:
        flat = a.reshape(-1).astype(dtype)
        parts.append(jnp.pad(flat, (0, (-flat.shape[0]) % seg)))
    return jnp.concatenate(parts).reshape(-1, 128)


def _state_rows(p):
    return p.reshape(1, D_ST)


def _chan_by_state(p):
    return jnp.transpose(p, (2, 0, 1)).reshape(GRP, D_ST)


def _chan_by_state_c(p):
    return jnp.transpose(p, (1, 0, 2)).reshape(GRP, D_ST)


def kernel(x, g_mix, w_in, A_re, A_im, log_dt, B_re, B_im, C_re, C_im, D_skip, w_glu, b_glu, w_pool, pool_scale, sgu_ln_g, sgu_ln_b, w_spatial, b_spatial, w_out, g_ffn, w_gate, w_up, w_down, g_final, loss_target, m_g_mix, m_w_in, m_A_re, m_A_im, m_log_dt, m_B_re, m_B_im, m_C_re, m_C_im, m_D_skip, m_w_glu, m_b_glu, m_w_pool, m_pool_scale, m_sgu_ln_g, m_sgu_ln_b, m_w_spatial, m_b_spatial, m_w_out, m_g_ffn, m_w_gate, m_w_up, m_w_down, m_g_final, v_g_mix, v_w_in, v_A_re, v_A_im, v_log_dt, v_B_re, v_B_im, v_C_re, v_C_im, v_D_skip, v_w_glu, v_b_glu, v_w_pool, v_pool_scale, v_sgu_ln_g, v_sgu_ln_b, v_w_spatial, v_b_spatial, v_w_out, v_g_ffn, v_w_gate, v_w_up, v_w_down, v_g_final):
    args = dict(locals())
    W = {n: args[n] for n in WEIGHT_ORDER}
    M = {n: args["m_" + n] for n in WEIGHT_ORDER}
    V = {n: args["v_" + n] for n in WEIGHT_ORDER}
    n_layers = g_mix.shape[0]
    x0 = x[0]
    target = loss_target[0]

    def my_rows(name, l):
        w = W[name][l]
        return (w.T if name in COLUMN_SHARDED else w).astype(BF16)

    full_w = [dict() for _ in range(n_layers)]

    def gather_of(*which):
        return _Exchange(gather=[my_rows(n, l) for n, l in which])

    def keep_gathered(which, arrays):
        for (n, l), a in zip(which, arrays):
            full_w[l][n] = a.reshape(-1, a.shape[-1])

    tri = jnp.tril(jnp.ones((CHUNK, CHUNK), bool))
    perm = _interleave_matrices(TT_MIX)
    consts = []
    for l in range(n_layers):
        a_re, a_im = _state_rows(A_re[l]), _state_rows(A_im[l])
        ldt = jnp.repeat(log_dt[l], N_STATE).reshape(1, D_ST)
        b_re_t, b_im_t = _chan_by_state(B_re[l]), _chan_by_state(B_im[l])
        sc, bbd, cbd = s5_prepare(a_re, a_im, ldt, b_re_t, b_im_t, _chan_by_state_c(C_re[l]), _chan_by_state_c(C_im[l]))
        wsm = jnp.where(tri[None], w_spatial[l], 0.0)
        wpool_bd = jnp.zeros((D_POOL, D_POOL), F32)
        for gi in range(len(POOL_WINDOWS)):
            wpool_bd = wpool_bd.at[gi * 64:(gi + 1) * 64, gi * 64:(gi + 1) * 64].set(w_pool[l, gi])
        consts.append(dict(
            disc=(a_re, a_im, ldt, b_re_t, b_im_t), sc=sc, bbd=bbd, cbd=cbd,
            dskip=D_skip[l].reshape(1, D_SSM), bglu=b_glu[l].reshape(1, D_SSM),
            wpool=wpool_bd.astype(BF16), pscale=pool_scale[l].reshape(1, D_POOL),
            lng=sgu_ln_g[l].reshape(1, D_SGU), lnb=sgu_ln_b[l].reshape(1, D_SGU),
            wsm=wsm.astype(BF16), wsmt=jnp.transpose(wsm, (0, 2, 1)).astype(BF16),
            bsp=jnp.repeat(b_spatial[l].T, HEAD_DIM, axis=1),
            gmix=g_mix[l].reshape(1, D), gffn=g_ffn[l].reshape(1, D)))

    def mixer_args(l):
        c = consts[l]
        return (c["bbd"], c["cbd"], c["dskip"], full_w[l]["w_glu"], c["bglu"], c["wpool"], c["pscale"], c["lng"], c["lnb"])

    first_needed = [("w_in", 0)]
    keep_gathered(first_needed, exchange_only(gather_of(*first_needed), "gather_first"))
    carried_fwd = {
        ("inproj", 0): [("w_glu", 0), ("w_out", 0)],
        ("mixer", 0): [("w_gate", 0), ("w_up", 0), ("w_down", 0)],
        ("ffn", 0): [("w_in", 1), ("w_glu", 1), ("w_out", 1), ("w_gate", 1)],
        ("mixer", 1): [("w_up", 1), ("w_down", 1)],
    }

    def carried(kind, l):
        which = carried_fwd.get((kind, l))
        return which, (gather_of(*which) if which else None)

    saved = []
    xl = x0
    for l in range(n_layers):
        c, fw = consts[l], full_w[l]
        which, ex = carried("inproj", l)
        (h, z), got = inproj_fwd(xl, c["gmix"], fw["w_in"], f"inproj_fwd_{l}", ex)
        keep_gathered(which or [], got)
        which, ex = carried("mixer", l)
        (ycat, hs, ys), got = mixer_fwd(z, c["sc"], *mixer_args(l), c["wsm"], c["bsp"], perm, f"mixer_fwd_{l}", ex)
        keep_gathered(which or [], got)
        which, ex = carried("ffn", l)
        head = (target, g_final.reshape(1, D)) if l == n_layers - 1 else None
        res, got = ffn_fwd(xl, ycat, fw["w_out"], c["gffn"], fw["w_gate"], fw["w_up"], fw["w_down"], f"ffn_fwd_{l}", ex, head)
        keep_gathered(which or [], got)
        x1, h2, gate, up, act, x2 = res[:6]
        saved.append(dict(x=xl, h=h, z=z, ycat=ycat, hs=hs, ys=ys, x1=x1, h2=h2, gate=gate, up=up, act=act))
        xl = x2
    dx, loss_vec, d_gfinal = xl, res[6], res[7]

    recv_big = {}
    recv_small = [None] * (2 * n_layers)

    def keep_received(which, arrays):
        for key, a in zip(which, arrays):
            if key[0] == "small":
                recv_small[key[1]] = a
            elif key[0] == "small16":
                recv_small[n_layers + key[1]] = a
            else:
                recv_big[key] = a

    pending = None
    for l in reversed(range(n_layers)):
        c, fw, sv = consts[l], full_w[l], saved[l]
        (dgu, dx2b, dx1, dx1b, d_gffn), got = ffn_bwd(dx, sv["x1"], sv["gate"], sv["up"], c["gffn"], fw["w_gate"], fw["w_up"],
                                                     fw["w_down"], f"ffn_bwd_{l}", pending[1] if pending else None)
        if pending:
            keep_received(pending[0], got)
        g_gu = wgrad(dgu, sv["h2"], D_FF // 2, f"wgrad_gate_up_{l}")
        g_down = wgrad(sv["act"], dx2b, D_FF // 2, f"wgrad_down_{l}")
        g_out = wgrad(sv["ycat"], dx1b, D, f"wgrad_out_{l}")
        ffn_rows = D_FF // N_DEV
        ex = _Exchange(scatter=[(g_gu, 0, ffn_rows), (g_gu, D_FF, ffn_rows), (g_down, 0, ffn_rows), (g_out, 0, D // N_DEV)])
        (dzb, da, dbt, dct, dd, dbglu, dwglu, dwpool, dpscale, dlng, dlnb, dws, dbsp), got = mixer_bwd(
            dx1b, sv["z"], sv["hs"], sv["ys"], fw["w_out"], c["sc"], *mixer_args(l), c["wsm"], c["wsmt"], c["bsp"],
            perm, f"mixer_bwd_{l}", ex)
        keep_received([("w_gate", l), ("w_up", l), ("w_down", l), ("w_out", l)], got)
        dx, d_gmix = inproj_bwd(dzb, sv["x"], c["gmix"], fw["w_in"], dx1)
        d_are, d_aim, d_ldt, d_bre_t, d_bim_t = s5_param_bwd(*c["disc"], da, dbt)
        small = dict(
            g_mix=d_gmix.reshape(D), A_re=d_are.reshape(N_GRP, N_STATE), A_im=d_aim.reshape(N_GRP, N_STATE),
            log_dt=d_ldt[0, :N_GRP],
            B_re=jnp.transpose(d_bre_t.reshape(GRP, N_GRP, N_STATE), (1, 0, 2)),
            B_im=jnp.transpose(d_bim_t.reshape(GRP, N_GRP, N_STATE), (1, 0, 2)),
            C_re=jnp.transpose(dct[:, :D_ST].reshape(GRP, N_GRP, N_STATE), (1, 0, 2)),
            C_im=-jnp.transpose(dct[:, D_ST:].reshape(GRP, N_GRP, N_STATE), (1, 0, 2)),
            D_skip=dd.reshape(D_SSM), b_glu=dbglu.reshape(D_SSM),
            w_pool=jnp.stack([dwpool[gi * 64:(gi + 1) * 64, gi * 64:(gi + 1) * 64] for gi in range(len(POOL_WINDOWS))]),
            pool_scale=dpscale.reshape(D_POOL), sgu_ln_g=dlng.reshape(D_SGU), sgu_ln_b=dlnb.reshape(D_SGU),
            w_spatial=dws, b_spatial=dbsp[:, :HEADS].T, g_ffn=d_gffn.reshape(D))
        packed = [small[n] for n in SMALL_LAYER if n not in SMALL_AS_BF16]
        packed += [d_gfinal.reshape(D), loss_vec.reshape(D)] if l == 0 else []
        packed16 = [small[n] for n in SMALL_LAYER if n in SMALL_AS_BF16]
        small_entries = dict(gather=[_pack(packed), _pack(packed16, BF16)], scatter=[(dwglu.astype(BF16), 0, D_SSM // N_DEV)])
        if l > 0:
            g_in = wgrad(dzb, sv["h"], D_IN, f"wgrad_in_{l}")
            small_entries["scatter"].append((g_in, 0, D_IN // N_DEV))
            pending = ([("small", l), ("small16", l), ("w_glu", l), ("w_in", l)], _Exchange(**small_entries))
        else:
            g_in, got = wgrad(dzb, sv["h"], D_IN, f"wgrad_in_{l}", tk=TK_WGRAD // 4, exchange=_Exchange(**small_entries))
            keep_received([("small", l), ("small16", l), ("w_glu", l)], got)
            pending = ([("w_in", l)], _Exchange(scatter=[(g_in, 0, D_IN // N_DEV)]))
    grad_x = dx
    keep_received(pending[0], exchange_only(pending[1], "exchange_last"))

    out = {}
    for n in BIG_NAMES:
        tr = (lambda a: jnp.transpose(a, (0, 2, 1))) if n in COLUMN_SHARDED else (lambda a: a)
        res = adamw_layers([recv_big[(n, l)] for l in range(n_layers)], tr(W[n]), tr(M[n]), tr(V[n]), f"adamw_{n}")
        out[n] = [tr(r) for r in res]

    seg_rows = [(-(-math.prod(W[n].shape[1:]) // SEG)) * (SEG // 128) for n in SMALL_LAYER]
    segments, src, src16, dst = [], 0, 0, 0
    for n, rows in zip(SMALL_LAYER, seg_rows):
        if n in SMALL_AS_BF16:
            segments += [(n_layers + l, src16, dst + l * rows, rows) for l in range(n_layers)]
            src16 += rows
        else:
            segments += [(l, src, dst + l * rows, rows) for l in range(n_layers)]
            src += rows
        dst += n_layers * rows
    tile_rows = SEG // 128
    segments += [(0, src, dst, tile_rows), (0, src + tile_rows, dst + tile_rows, tile_rows)]

    channel_major = ("B_re", "B_im")

    def pack_params(P):
        parts = []
        for n, rows in zip(SMALL_LAYER, seg_rows):
            flat = (jnp.swapaxes(P[n], -1, -2) if n in channel_major else P[n]).reshape(n_layers, -1)
            parts.append(jnp.pad(flat, ((0, 0), (0, rows * 128 - flat.shape[1]))).reshape(-1))
        return jnp.concatenate(parts + [P["g_final"], jnp.zeros((SEG,), F32)]).reshape(-1, 128)

    res = adamw_segments(recv_small, segments, pack_params(W), pack_params(M), pack_params(V), "adamw_small")
    for j in range(4):
        flat, off = res[j].reshape(-1), 0
        for n, rows in zip(SMALL_LAYER, seg_rows):
            size = math.prod(W[n].shape[1:])
            piece = flat[off:off + n_layers * rows * 128].reshape(n_layers, rows * 128)[:, :size]
            if n in channel_major:
                piece = jnp.swapaxes(piece.reshape(W[n].shape[:-2] + W[n].shape[:-3:-1]), -1, -2)
            else:
                piece = piece.reshape(W[n].shape)
            out.setdefault(n, []).append(piece)
            off += n_layers * rows * 128
        out.setdefault("g_final", []).append(flat[off:off + D])
        if j == 0:
            loss = (0.5 / D) * jnp.sum(flat[off + SEG:off + SEG + D])

    return (loss, grad_x[None], *[out[n][0] for n in WEIGHT_ORDER], *[out[n][1] for n in WEIGHT_ORDER],
            *[out[n][2] for n in WEIGHT_ORDER], *[out[n][3] for n in WEIGHT_ORDER])
```

```python
import math

import jax
import jax.numpy as jnp
from jax import lax
from jax.experimental import pallas as pl
from jax.experimental.pallas import tpu as pltpu

F32 = jnp.float32
BF16 = jnp.bfloat16
S = jax.ShapeDtypeStruct

N_DEV = 8
D = 1024
D_SSM = 384
N_GRP = 24
GRP = 16
N_STATE = 64
D_ST = N_GRP * N_STATE
D_POOL = 256
POOL_WINDOWS = (2, 4, 8, 16)
HALO = 16
D_SGU = 384
HEADS = 6
HEAD_DIM = 64
CHUNK = 128
D_IN = 1408
D_FF = 2816
EPS = 1e-6
SCAN_BLK = 8
BF16_TILE_ROWS = 16

ADAM_LR = 0.001
ADAM_B1 = 0.9
ADAM_B2 = 0.999
ADAM_EPS = 1e-08
ADAM_WD = 0.01
ADAM_STEP = 10

GELU_C0 = math.sqrt(2.0 / math.pi)
GELU_C1 = 0.044715

TT_MIX = 256
SEG_LEN = TT_MIX // SCAN_BLK
TT_FFN = 256
TT_PROJ = 1024
TK_WGRAD = 2048
VMEM_MB = 2 ** 20


def _cp(vmem_mb, grid_dims=0):
    kw = dict(vmem_limit_bytes=int(vmem_mb * VMEM_MB))
    if grid_dims:
        kw["dimension_semantics"] = ("arbitrary",) * grid_dims
    return pltpu.CompilerParams(**kw)


def _row(tt, n):
    return pl.BlockSpec((tt, n), lambda i: (i, 0))


def _full(shape):
    nd = len(shape)
    return pl.BlockSpec(shape, lambda *_: (0,) * nd)


def _nn(a, b):
    return jnp.dot(a, b, preferred_element_type=F32)


def _nt(a, b):
    return lax.dot_general(a, b, (((1,), (1,)), ((), ())), preferred_element_type=F32)


def _tn(a, b):
    return lax.dot_general(a, b, (((0,), (0,)), ((), ())), preferred_element_type=F32)


def _rowsum(x):
    return jnp.sum(x, axis=0, keepdims=True)


def _rms(x):
    r = lax.rsqrt(jnp.mean(x * x, axis=-1, keepdims=True) + EPS)
    return x * r, r


def _rms_bwd(dy, xn, r, g):
    dyg = dy * g
    return r * (dyg - xn * jnp.mean(dyg * xn, axis=-1, keepdims=True))


def _gelu(x):
    s = jax.nn.sigmoid(x * (2.0 * GELU_C0 + (2.0 * GELU_C0 * GELU_C1) * (x * x)))
    return x * s, s


def _gelu_grad(x, s):
    return s * (1.0 + x * (1.0 - s) * (2.0 * GELU_C0 + (6.0 * GELU_C0 * GELU_C1) * (x * x)))


def _discretise(a_re, a_im, ldt, b_re, b_im):
    dt = jnp.exp(ldt)
    mag = jnp.exp(a_re * dt)
    ar = mag * jnp.cos(a_im * dt)
    ai = mag * jnp.sin(a_im * dt)
    den = a_re * a_re + a_im * a_im
    f_re = ((ar - 1.0) * a_re + ai * a_im) / den
    f_im = (ai * a_re - (ar - 1.0) * a_im) / den
    bb_re = f_re * b_re - f_im * b_im
    bb_im = f_re * b_im + f_im * b_re
    return ar, ai, bb_re, bb_im


def _group_mask(rows, cols):
    r = lax.broadcasted_iota(jnp.int32, (rows, cols), 0) // GRP
    c = lax.broadcasted_iota(jnp.int32, (rows, cols), 1)
    c = jnp.where(c >= D_ST, c - D_ST, c) // N_STATE
    return r == c


def s5_prepare(a_re, a_im, ldt, b_re_t, b_im_t, c_re_t, c_im_t):
    def body(are_ref, aim_ref, ldt_ref, bre_ref, bim_ref, cre_ref, cim_ref, sc_ref, bbd_ref, cbd_ref):
        ar, ai, bb_re, bb_im = _discretise(are_ref[...], aim_ref[...], ldt_ref[...], bre_ref[...], bim_ref[...])
        mask = _group_mask(D_SSM, 2 * D_ST)
        bb = jnp.concatenate([jnp.tile(bb_re, (N_GRP, 1)), jnp.tile(bb_im, (N_GRP, 1))], axis=1)
        bbd_ref[...] = jnp.where(mask, bb, 0.0).astype(BF16)
        cc = jnp.concatenate([jnp.tile(cre_ref[...], (N_GRP, 1)), -jnp.tile(cim_ref[...], (N_GRP, 1))], axis=1)
        cbd_ref[...] = jnp.where(mask, cc, 0.0).astype(BF16)
        pr, pi = ar, ai
        for _ in range(SEG_LEN - 1):
            pr, pi = pr * ar - pi * ai, pr * ai + pi * ar
        for k, v in enumerate((ar, ai, pr, pi)):
            sc_ref[8 * k:8 * k + 8, :] = jnp.broadcast_to(v, (SCAN_BLK, D_ST))

    return pl.pallas_call(
        body, name="s5_prepare",
        out_shape=[S((32, D_ST), F32), S((D_SSM, 2 * D_ST), BF16), S((D_SSM, 2 * D_ST), BF16)],
        compiler_params=_cp(40),
    )(a_re, a_im, ldt, b_re_t, b_im_t, c_re_t, c_im_t)


def s5_param_bwd(a_re, a_im, ldt, b_re_t, b_im_t, da, dbt):
    def body(are_ref, aim_ref, ldt_ref, bre_ref, bim_ref, da_ref, dbt_ref, o_are, o_aim, o_ldt, o_bre, o_bim):
        _, vjp = jax.vjp(_discretise, are_ref[...], aim_ref[...], ldt_ref[...], bre_ref[...], bim_ref[...])
        da = da_ref[...]
        dbt = dbt_ref[...]
        g_are, g_aim, g_ldt, g_bre, g_bim = vjp((da[:, :D_ST], da[:, D_ST:], dbt[:, :D_ST], dbt[:, D_ST:]))
        o_are[...] = g_are
        o_aim[...] = g_aim
        o_bre[...] = g_bre
        o_bim[...] = g_bim
        grp = lax.broadcasted_iota(jnp.int32, (1, D_ST), 1) // N_STATE
        lane = lax.broadcasted_iota(jnp.int32, (1, 128), 1)
        out = jnp.zeros((1, 128), F32)
        for g in range(N_GRP):
            out = jnp.where(lane == g, jnp.sum(jnp.where(grp == g, g_ldt, 0.0), axis=1, keepdims=True), out)
        o_ldt[...] = out

    return pl.pallas_call(
        body, name="s5_param_bwd",
        out_shape=[S((1, D_ST), F32), S((1, D_ST), F32), S((1, 128), F32), S((GRP, D_ST), F32), S((GRP, D_ST), F32)],
        compiler_params=_cp(16),
    )(a_re, a_im, ldt, b_re_t, b_im_t, da, dbt)


_CH = ((0, 256), (256, D_SSM))
_ST = ((0, 1024), (1024, D_ST))


def _bd_expand(xb, w_ref, out_ref):
    for (c0, c1), (s0, s1) in zip(_CH, _ST):
        for half in (0, D_ST):
            out_ref[:, half + s0:half + s1] = _nn(xb[:, c0:c1], w_ref[c0:c1, half + s0:half + s1])


def _bd_contract(hb, w_ref):
    parts = []
    for (c0, c1), (s0, s1) in zip(_CH, _ST):
        parts.append(_nt(hb[:, s0:s1], w_ref[c0:c1, s0:s1]) + _nt(hb[:, D_ST + s0:D_ST + s1], w_ref[c0:c1, D_ST + s0:D_ST + s1]))
    return jnp.concatenate(parts, axis=1)


def _bd_accumulate(acc_ref, xb, hb):
    for j in range(N_GRP // 4):
        ch = slice(4 * GRP * j, 4 * GRP * (j + 1))
        for half in (0, D_ST):
            st = slice(half + 4 * N_STATE * j, half + 4 * N_STATE * (j + 1))
            acc_ref[ch, st] += _tn(xb[:, ch], hb[:, st])


def _interleave_matrices(tt):
    r = lax.broadcasted_iota(jnp.int32, (tt, tt), 0)
    t = lax.broadcasted_iota(jnp.int32, (tt, tt), 1)
    p = (t == (r % SCAN_BLK) * (tt // SCAN_BLK) + r // SCAN_BLK).astype(BF16)
    return p, p.T


def _interleave_f32(p, x):
    hi = x.astype(BF16)
    lo = (x - hi.astype(F32)).astype(BF16)
    return _nn(p, hi) + _nn(p, lo)


def _scan_tile(buf_ref, sc_ref, carry_ref, n_blk, reverse, on_block=None):
    ar = sc_ref[0:8, :]
    ai = -sc_ref[8:16, :] if reverse else sc_ref[8:16, :]

    def rows(i):
        blk = (n_blk - 1 - i) if reverse else i
        return pl.ds(pl.multiple_of(blk * SCAN_BLK, SCAN_BLK), SCAN_BLK)

    def local(i, x):
        xr, xi = x
        r = rows(i)
        xr, xi = buf_ref[r, 0:D_ST] + ar * xr - ai * xi, buf_ref[r, D_ST:2 * D_ST] + ar * xi + ai * xr
        buf_ref[r, 0:D_ST] = xr
        buf_ref[r, D_ST:2 * D_ST] = xi
        return xr, xi

    zero = jnp.zeros((SCAN_BLK, D_ST), F32)
    end_r, end_i = lax.fori_loop(0, n_blk, local, (zero, zero), unroll=True)

    seg_r = sc_ref[16:17, :]
    seg_i = -sc_ref[24:25, :] if reverse else sc_ref[24:25, :]
    cr, ci = carry_ref[0:1, 0:D_ST], carry_ref[0:1, D_ST:2 * D_ST]
    sub = lax.broadcasted_iota(jnp.int32, (SCAN_BLK, D_ST), 0)
    in_r, in_i = zero, zero
    for s in (reversed(range(SCAN_BLK)) if reverse else range(SCAN_BLK)):
        in_r = jnp.where(sub == s, cr, in_r)
        in_i = jnp.where(sub == s, ci, in_i)
        cr, ci = end_r[s:s + 1, :] + seg_r * cr - seg_i * ci, end_i[s:s + 1, :] + seg_r * ci + seg_i * cr
    carry_ref[0:1, 0:D_ST] = cr
    carry_ref[0:1, D_ST:2 * D_ST] = ci

    dr, di = in_r, in_i
    for i in range(n_blk):
        blk = (n_blk - 1 - i) if reverse else i
        r = slice(blk * SCAN_BLK, (blk + 1) * SCAN_BLK)
        dr, di = ar * dr - ai * di, ar * di + ai * dr
        xr, xi = buf_ref[r, 0:D_ST] + dr, buf_ref[r, D_ST:2 * D_ST] + di
        buf_ref[r, 0:D_ST] = xr
        buf_ref[r, D_ST:2 * D_ST] = xi
        if on_block is not None:
            on_block(blk, xr, xi)


def _lane_windows(n):
    lane = lax.broadcasted_iota(jnp.int32, (1, n), 1)
    return lane // (D_POOL // len(POOL_WINDOWS))


def _select_window(grp, s2, s4, s8, s16):
    return jnp.where(grp == 0, s2, jnp.where(grp == 1, s4, jnp.where(grp == 2, s8, s16)))


def _pool_fwd(pbuf_ref, zb, halo, tile_idx, tt):
    pbuf_ref[0:HALO, :] = halo
    pbuf_ref[HALO:HALO + tt, :] = zb
    x = pbuf_ref[...]
    s2 = x + pltpu.roll(x, 1, axis=0)
    s4 = s2 + pltpu.roll(s2, 2, axis=0)
    s8 = s4 + pltpu.roll(s4, 4, axis=0)
    s16 = s8 + pltpu.roll(s8, 8, axis=0)
    grp = _lane_windows(D_POOL)
    win = _select_window(grp, s2, s4, s8, s16)[HALO:HALO + tt, :]
    width = _select_window(grp, 2.0, 4.0, 8.0, 16.0).astype(F32)
    pos = (tile_idx * tt + 1 + lax.broadcasted_iota(jnp.int32, (tt, 1), 0)).astype(F32)
    cnt = jnp.minimum(pos, width)
    return win / cnt - zb, cnt


def _sgu_fwd(zu, zv, lng, lnb, wsm_ref, bsp, mix_ref, tt):
    u, tu = _gelu(zu)
    v, tv = _gelu(zv)
    mu = jnp.mean(v, axis=-1, keepdims=True)
    vc = v - mu
    rstd = lax.rsqrt(jnp.mean(vc * vc, axis=-1, keepdims=True) + EPS)
    vhat = vc * rstd
    vnb = (vhat * lng + lnb).astype(BF16)
    _head_mix(wsm_ref, vnb, mix_ref, tt, bsp)
    return u, tu, tv, vhat, rstd, vnb


def _chunk_pairs(tt):
    n_ch = tt // CHUNK
    return [list(range(c, min(c + 2, n_ch))) for c in range(0, n_ch, 2)]


def _head_cols(xb, chunks, h):
    return jnp.concatenate([xb[c * CHUNK:(c + 1) * CHUNK, h * HEAD_DIM:(h + 1) * HEAD_DIM] for c in chunks], axis=1)


def _head_mix(w_ref, xb, out_ref, tt, add=None):
    for chunks in _chunk_pairs(tt):
        per_head = [_nn(w_ref[h], _head_cols(xb, chunks, h)) for h in range(HEADS)]
        for k, c in enumerate(chunks):
            block = jnp.concatenate([r[:, k * HEAD_DIM:(k + 1) * HEAD_DIM] for r in per_head], axis=1)
            out_ref[c * CHUNK:(c + 1) * CHUNK, :] = block if add is None else block + add


def mixer_fwd(z, sc, bbd, cbd, dskip, wglu, bglu, wpool, pscale, lng, lnb, wsm, bsp, perm, name, exchange=None):
    T = z.shape[0]
    tt = TT_MIX
    n_tiles = T // tt

    def body(z_ref, sc_ref, bbd_ref, cbd_ref, dskip_ref, wglu_ref, bglu_ref, wpool_ref, pscale_ref, lng_ref, lnb_ref,
             wsm_ref, bsp_ref, p_ref, pt_ref, ycat_ref, hs_ref, ys_ref, carry_ref, halo_ref, pbuf_ref, mix_ref):
        i = pl.program_id(0)

        @pl.when(i == 0)
        def _():
            carry_ref[...] = jnp.zeros_like(carry_ref)
            halo_ref[...] = jnp.zeros_like(halo_ref)

        za = z_ref[:, 0:D_SSM]
        p, pt = p_ref[...], pt_ref[...]
        za = _interleave_f32(p, za)
        _bd_expand(za.astype(BF16), bbd_ref, hs_ref)
        _scan_tile(hs_ref, sc_ref, carry_ref, tt // SCAN_BLK, reverse=False)
        y = _bd_contract(hs_ref[...].astype(BF16), cbd_ref) + dskip_ref[...] * za
        ys_ref[...] = y
        g, _ = _gelu(y)
        q = _nn(g.astype(BF16), wglu_ref[...]) + bglu_ref[...]
        ycat_ref[:, 0:D_SSM] = _nn(pt, (g * jax.nn.sigmoid(q)).astype(BF16)).astype(BF16)
        zb = z_ref[:, D_SSM:D_SSM + D_POOL]
        pooled, _ = _pool_fwd(pbuf_ref, zb, halo_ref[...], i, tt)
        halo_ref[...] = zb[tt - HALO:tt, :]
        ycat_ref[:, D_SSM:D_SSM + D_POOL] = (_nn(pooled.astype(BF16), wpool_ref[...]) * pscale_ref[...]).astype(BF16)
        zu = z_ref[:, D_SSM + D_POOL:D_SSM + D_POOL + D_SGU]
        zv = z_ref[:, D_SSM + D_POOL + D_SGU:D_IN]
        u, _, _, _, _, _ = _sgu_fwd(zu, zv, lng_ref[...], lnb_ref[...], wsm_ref, bsp_ref[...], mix_ref, tt)
        ycat_ref[:, D_SSM + D_POOL:D] = (u * mix_ref[...]).astype(BF16)

    return _pallas(
        body, name=name, grid=(n_tiles,),
        in_specs=[_row(tt, D_IN), _full((32, D_ST)), _full((D_SSM, 2 * D_ST)), _full((D_SSM, 2 * D_ST)),
                  _full((1, D_SSM)), _full((D_SSM, D_SSM)), _full((1, D_SSM)), _full((D_POOL, D_POOL)),
                  _full((1, D_POOL)), _full((1, D_SGU)), _full((1, D_SGU)), _full((HEADS, CHUNK, CHUNK)),
                  _full((CHUNK, D_SGU)), _full((tt, tt)), _full((tt, tt))],
        out_specs=[_row(tt, D), _row(tt, 2 * D_ST), _row(tt, D_SSM)],
        out_shape=[S((T, D), BF16), S((T, 2 * D_ST), F32), S((T, D_SSM), F32)],
        scratch_shapes=[pltpu.VMEM((SCAN_BLK, 2 * D_ST), F32), pltpu.VMEM((HALO, D_POOL), F32),
                        pltpu.VMEM((tt + HALO, D_POOL), F32), pltpu.VMEM((tt, D_SGU), F32)],
        vmem_mb=48, operands=(z, sc, bbd, cbd, dskip, wglu, bglu, wpool, pscale, lng, lnb, wsm, bsp, *perm),
        exchange=exchange)


def mixer_bwd(dx1b, z, hs, ys, wout, sc, bbd, cbd, dskip, wglu, bglu, wpool, pscale, lng, lnb, wsm, wsmt, bsp, perm, name,
              exchange=None):
    T = z.shape[0]
    tt = TT_MIX
    n_tiles = T // tt

    def rev(i):
        return n_tiles - 1 - i

    def body(dx_ref, z_ref, zprev_ref, hs_ref, hsprev_ref, ys_ref, wout_ref, sc_ref, bbd_ref, cbd_ref, dskip_ref,
             wglu_ref, bglu_ref, wpool_ref, pscale_ref, lng_ref, lnb_ref, wsm_ref, wsmt_ref, bsp_ref, p_ref, pt_ref,
             dz_ref, o_da, o_dbt, o_dct, o_dd, o_dbglu, o_dwglu, o_dwpool, o_dpscale, o_dlng, o_dlnb, o_dws, o_dbsp,
             gbuf_ref, carry_ref, accb_ref, accc_ref, ehalo_ref, pbuf_ref, mix_ref, dvn_ref, accw_ref, accm_ref):
        i = pl.program_id(0)
        tile = rev(i)

        @pl.when(i == 0)
        def _():
            for ref in (carry_ref, accb_ref, accc_ref, ehalo_ref, accw_ref, accm_ref,
                        o_da, o_dd, o_dbglu, o_dwglu, o_dwpool, o_dpscale, o_dlng, o_dlnb):
                ref[...] = jnp.zeros_like(ref)

        p, pt = p_ref[...], pt_ref[...]
        dxb = dx_ref[...]
        d_a = _nt(_nn(p, dxb).astype(BF16), wout_ref[0:D_SSM, :])
        za = _interleave_f32(p, z_ref[:, 0:D_SSM])
        first_tile = (tile > 0).astype(F32)

        y = ys_ref[...]
        g, tg = _gelu(y)
        gb = g.astype(BF16)
        sg = jax.nn.sigmoid(_nn(gb, wglu_ref[...]) + bglu_ref[...])
        dq = d_a * g * sg * (1.0 - sg)
        dqb = dq.astype(BF16)
        o_dbglu[...] += _rowsum(dq)
        o_dwglu[...] += _tn(gb, dqb)
        dy = (d_a * sg + _nt(dqb, wglu_ref[...])) * _gelu_grad(y, tg)
        o_dd[...] += _rowsum(dy * za)
        dyb = dy.astype(BF16)
        _bd_expand(dyb, cbd_ref, gbuf_ref)
        hprev = hsprev_ref[SCAN_BLK - 1:SCAN_BLK, :] * first_tile
        sub = lax.broadcasted_iota(jnp.int32, (SCAN_BLK, 1), 0)
        edge = jnp.where(sub == 0, hprev, pltpu.roll(hs_ref[tt - SCAN_BLK:tt, :], 1, axis=0))
        da = [jnp.zeros((SCAN_BLK, D_ST), F32), jnp.zeros((SCAN_BLK, D_ST), F32)]

        def da_terms(blk, gr, gi):
            before = edge if blk == 0 else hs_ref[(blk - 1) * SCAN_BLK:blk * SCAN_BLK, :]
            hr, hi = before[:, 0:D_ST], before[:, D_ST:]
            da[0] = da[0] + (gr * hr + gi * hi)
            da[1] = da[1] + (gi * hr - gr * hi)

        _scan_tile(gbuf_ref, sc_ref, carry_ref, tt // SCAN_BLK, reverse=True, on_block=da_terms)
        o_da[:, 0:D_ST] += _rowsum(da[0])
        o_da[:, D_ST:] += _rowsum(da[1])
        gtb = gbuf_ref[...].astype(BF16)
        dza = (dy * dskip_ref[...] + _bd_contract(gtb, bbd_ref)).astype(BF16)
        dz_ref[:, 0:D_SSM] = _nn(pt, dza).astype(BF16)
        _bd_accumulate(accb_ref, za.astype(BF16), gtb)
        _bd_accumulate(accc_ref, dyb, hs_ref[...].astype(BF16))
        d_bc = _nt(dx_ref[...], wout_ref[D_SSM:D, :])
        _pool_sgu_bwd(i, n_tiles, tile, tt, d_bc, z_ref, zprev_ref, wpool_ref, pscale_ref, lng_ref, lnb_ref, wsm_ref,
                      wsmt_ref, bsp_ref, dz_ref, o_dwpool, o_dpscale, o_dlng, o_dlnb, o_dws, o_dbsp,
                      ehalo_ref, pbuf_ref, mix_ref, dvn_ref, accw_ref, accm_ref)

        @pl.when(i == n_tiles - 1)
        def _():
            mask = _group_mask(D_SSM, 2 * D_ST)
            for acc_ref, o_ref in ((accb_ref, o_dbt), (accc_ref, o_dct)):
                fold = jnp.zeros((GRP, 2 * D_ST), F32)
                for gidx in range(N_GRP):
                    rows = slice(gidx * GRP, (gidx + 1) * GRP)
                    fold = fold + jnp.where(mask[rows, :], acc_ref[rows, :], 0.0)
                o_ref[...] = fold

    def rowr(n):
        return pl.BlockSpec((tt, n), lambda i: (rev(i), 0))

    zprev_spec = pl.BlockSpec((HALO, D_IN), lambda i: (jnp.maximum(rev(i) * (tt // HALO) - 1, 0), 0))
    hsprev_spec = pl.BlockSpec((SCAN_BLK, 2 * D_ST), lambda i: (jnp.maximum(rev(i) * (tt // SCAN_BLK) - 1, 0), 0))
    small = [S((1, 2 * D_ST), F32), S((GRP, 2 * D_ST), F32), S((GRP, 2 * D_ST), F32), S((1, D_SSM), F32),
             S((1, D_SSM), F32), S((D_SSM, D_SSM), F32), S((D_POOL, D_POOL), F32), S((1, D_POOL), F32),
             S((1, D_SGU), F32), S((1, D_SGU), F32), S((HEADS, CHUNK, CHUNK), F32), S((CHUNK, 128), F32)]
    return _pallas(
        body, name=name, grid=(n_tiles,),
        in_specs=[rowr(D), rowr(D_IN), zprev_spec, rowr(2 * D_ST), hsprev_spec, rowr(D_SSM), _full((D, D)),
                  _full((32, D_ST)), _full((D_SSM, 2 * D_ST)), _full((D_SSM, 2 * D_ST)), _full((1, D_SSM)),
                  _full((D_SSM, D_SSM)), _full((1, D_SSM)), _full((D_POOL, D_POOL)), _full((1, D_POOL)),
                  _full((1, D_SGU)), _full((1, D_SGU)), _full((HEADS, CHUNK, CHUNK)), _full((HEADS, CHUNK, CHUNK)),
                  _full((CHUNK, D_SGU)), _full((tt, tt)), _full((tt, tt))],
        out_specs=[rowr(D_IN)] + [_full(s.shape) for s in small],
        out_shape=[S((T, D_IN), BF16)] + small,
        scratch_shapes=[pltpu.VMEM((tt, 2 * D_ST), F32), pltpu.VMEM((SCAN_BLK, 2 * D_ST), F32),
                        pltpu.VMEM((D_SSM, 2 * D_ST), F32), pltpu.VMEM((D_SSM, 2 * D_ST), F32),
                        pltpu.VMEM((HALO, D_POOL), F32), pltpu.VMEM((tt + HALO, D_POOL), F32),
                        pltpu.VMEM((tt, D_SGU), F32), pltpu.VMEM((tt, D_SGU), F32),
                        pltpu.VMEM((HEADS, CHUNK, CHUNK), F32), pltpu.VMEM((CHUNK, D_SGU), F32)],
        vmem_mb=56, exchange=exchange,
        operands=(dx1b, z, z, hs, hs, ys, wout, sc, bbd, cbd, dskip, wglu, bglu, wpool, pscale, lng, lnb, wsm, wsmt, bsp, *perm))


def inproj_fwd(x, g, w_t, name, exchange=None):
    T = x.shape[0]
    tt = min(TT_PROJ, T)

    def body(x_ref, g_ref, w_ref, h_ref, z_ref):
        xn, _ = _rms(x_ref[...])
        h = (xn * g_ref[...]).astype(BF16)
        h_ref[...] = h
        z_ref[...] = _nt(h, w_ref[...])

    return _pallas(
        body, name=name, grid=(T // tt,),
        in_specs=[_row(tt, D), _full((1, D)), _full((D_IN, D))],
        out_specs=[_row(tt, D), _row(tt, D_IN)],
        out_shape=[S((T, D), BF16), S((T, D_IN), F32)],
        scratch_shapes=[], vmem_mb=48, operands=(x, g, w_t), exchange=exchange)


def inproj_bwd(dzb, x, g, w_t, dx1):
    T = x.shape[0]
    tt = min(TT_PROJ, T)

    def body(dz_ref, x_ref, g_ref, w_ref, dx1_ref, dx_ref, dg_ref):
        @pl.when(pl.program_id(0) == 0)
        def _():
            dg_ref[...] = jnp.zeros_like(dg_ref)

        dh = _nn(dz_ref[...], w_ref[...])
        xn, r = _rms(x_ref[...])
        dg_ref[...] += _rowsum(dh * xn)
        dx_ref[...] = dx1_ref[...] + _rms_bwd(dh, xn, r, g_ref[...])

    return pl.pallas_call(
        body, name="inproj_bwd", grid=(T // tt,),
        in_specs=[_row(tt, D_IN), _row(tt, D), _full((1, D)), _full((D_IN, D)), _row(tt, D)],
        out_specs=[_row(tt, D), _full((1, D))],
        out_shape=[S((T, D), F32), S((1, D), F32)],
        compiler_params=_cp(48, 1),
    )(dzb, x, g, w_t, dx1)


def _load_weights(pairs, sem):
    @pl.when(pl.program_id(0) == 0)
    def _():
        copies = [pltpu.make_async_copy(src, dst, sem.at[k]) for k, (src, dst) in enumerate(pairs)]
        for cp in copies:
            cp.start()
        for cp in copies:
            cp.wait()


def ffn_fwd(x, ycat, wout, g, wg_t, wu_t, wd, name, exchange=None, head=None):
    T = x.shape[0]
    tt = TT_FFN
    any_spec = pl.BlockSpec(memory_space=pl.ANY)

    def body(*refs):
        if head is None:
            (x_ref, ycat_ref, g_ref, wout_hbm, wg_hbm, wu_hbm, wd_hbm,
             x1_ref, h_ref, gate_ref, up_ref, act_ref, x2_ref, wout_v, wg_v, wu_v, wd_v, sem) = refs
        else:
            (x_ref, ycat_ref, g_ref, t_ref, gf_ref, wout_hbm, wg_hbm, wu_hbm, wd_hbm,
             x1_ref, h_ref, gate_ref, up_ref, act_ref, x2_ref, lvec_ref, dgf_ref, wout_v, wg_v, wu_v, wd_v, sem) = refs
        if head is not None:
            @pl.when(pl.program_id(0) == 0)
            def _():
                lvec_ref[...] = jnp.zeros_like(lvec_ref)
                dgf_ref[...] = jnp.zeros_like(dgf_ref)

        _ffn_fwd_tile(x_ref, ycat_ref, g_ref, wout_hbm, wg_hbm, wu_hbm, wd_hbm, x1_ref, h_ref, gate_ref, up_ref, act_ref,
                      x2_ref, wout_v, wg_v, wu_v, wd_v, sem)
        if head is not None:
            xn, r = _rms(x2_ref[...])
            gf = gf_ref[...]
            err = xn * gf - t_ref[...]
            lvec_ref[...] += _rowsum(err * err)
            dy = err * (1.0 / D)
            dgf_ref[...] += _rowsum(dy * xn)
            x2_ref[...] = _rms_bwd(dy, xn, r, gf)

    def _ffn_fwd_tile(x_ref, ycat_ref, g_ref, wout_hbm, wg_hbm, wu_hbm, wd_hbm,
                      x1_ref, h_ref, gate_ref, up_ref, act_ref, x2_ref, wout_v, wg_v, wu_v, wd_v, sem):
        _load_weights([(wout_hbm, wout_v), (wg_hbm, wg_v), (wu_hbm, wu_v), (wd_hbm, wd_v)], sem)
        x1 = x_ref[...] + _nn(ycat_ref[...], wout_v[...])
        x1_ref[...] = x1
        xn, _ = _rms(x1)
        h = (xn * g_ref[...]).astype(BF16)
        h_ref[...] = h
        gate = _nt(h, wg_v[...])
        up = _nt(h, wu_v[...])
        gate_ref[...] = gate.astype(BF16)
        up_ref[...] = up.astype(BF16)
        act = (gate * jax.nn.sigmoid(gate) * up).astype(BF16)
        act_ref[...] = act
        x2_ref[...] = x1 + _nn(act, wd_v[...])

    with_head = head is not None
    return _pallas(
        body, name=name, grid=(T // tt,),
        in_specs=[_row(tt, D), _row(tt, D), _full((1, D))] + ([_row(tt, D), _full((1, D))] if with_head else [])
        + [any_spec, any_spec, any_spec, any_spec],
        out_specs=[_row(tt, D), _row(tt, D), _row(tt, D_FF), _row(tt, D_FF), _row(tt, D_FF), _row(tt, D)]
        + ([_full((1, D)), _full((1, D))] if with_head else []),
        out_shape=[S((T, D), F32), S((T, D), BF16), S((T, D_FF), BF16), S((T, D_FF), BF16), S((T, D_FF), BF16),
                   S((T, D), F32)] + ([S((1, D), F32), S((1, D), F32)] if with_head else []),
        scratch_shapes=[pltpu.VMEM((D, D), BF16), pltpu.VMEM((D_FF, D), BF16), pltpu.VMEM((D_FF, D), BF16),
                        pltpu.VMEM((D_FF, D), BF16), pltpu.SemaphoreType.DMA((4,))],
        vmem_mb=56, operands=(x, ycat, g) + (tuple(head) if with_head else ()) + (wout, wg_t, wu_t, wd), exchange=exchange)


def _pool_sgu_bwd(i, n_tiles, tile, tt, d_bc, z_ref, zprev_ref, wpool_ref, pscale_ref, lng_ref, lnb_ref, wsm_ref, wsmt_ref,
                  bsp_ref, dz_ref, o_dwpool, o_dpscale, o_dlng, o_dlnb, o_dws, o_dbsp,
                  ehalo_ref, pbuf_ref, mix_ref, dvn_ref, accw_ref, accm_ref):
    d_b = d_bc[:, 0:D_POOL]
    d_c = d_bc[:, D_POOL:D_POOL + D_SGU]
    zb = z_ref[:, D_SSM:D_SSM + D_POOL]
    zu = z_ref[:, D_SSM + D_POOL:D_SSM + D_POOL + D_SGU]
    zv = z_ref[:, D_SSM + D_POOL + D_SGU:D_IN]
    not_first = (tile > 0).astype(F32)

    pooled, cnt = _pool_fwd(pbuf_ref, zb, zprev_ref[:, D_SSM:D_SSM + D_POOL] * not_first, tile, tt)
    pooledb = pooled.astype(BF16)
    mixed = _nn(pooledb, wpool_ref[...])
    o_dpscale[...] += _rowsum(d_b * mixed)
    dmixb = (d_b * pscale_ref[...]).astype(BF16)
    o_dwpool[...] += _tn(pooledb, dmixb)
    dpooled = _nt(dmixb, wpool_ref[...])
    e = dpooled / cnt
    pbuf_ref[0:tt, :] = e
    pbuf_ref[tt:tt + HALO, :] = ehalo_ref[...]
    ehalo_ref[...] = e[0:HALO, :]
    x = pbuf_ref[...]
    n = tt + HALO
    f2 = x + pltpu.roll(x, n - 1, axis=0)
    f4 = f2 + pltpu.roll(f2, n - 2, axis=0)
    f8 = f4 + pltpu.roll(f4, n - 4, axis=0)
    f16 = f8 + pltpu.roll(f8, n - 8, axis=0)
    fwd_sum = _select_window(_lane_windows(D_POOL), f2, f4, f8, f16)[0:tt, :]
    dz_ref[:, D_SSM:D_SSM + D_POOL] = (fwd_sum - dpooled).astype(BF16)

    lng = lng_ref[...]
    u, su, sv, vhat, rstd, vnb = _sgu_fwd(zu, zv, lng, lnb_ref[...], wsm_ref, bsp_ref[...], mix_ref, tt)
    dz_ref[:, D_SSM + D_POOL:D_SSM + D_POOL + D_SGU] = (d_c * mix_ref[...] * _gelu_grad(zu, su)).astype(BF16)
    dmix = d_c * u
    dmixb2 = dmix.astype(BF16)
    for c in range(tt // CHUNK):
        accm_ref[...] += dmix[c * CHUNK:(c + 1) * CHUNK, :]
    for chunks in _chunk_pairs(tt):
        for h in range(HEADS):
            accw_ref[h] += _nt(_head_cols(dmixb2, chunks, h), _head_cols(vnb, chunks, h))
    _head_mix(wsmt_ref, dmixb2, dvn_ref, tt)
    dvn = dvn_ref[...]
    o_dlng[...] += _rowsum(dvn * vhat)
    o_dlnb[...] += _rowsum(dvn)
    dvh = dvn * lng
    dv = rstd * (dvh - jnp.mean(dvh, axis=-1, keepdims=True) - vhat * jnp.mean(dvh * vhat, axis=-1, keepdims=True))
    dz_ref[:, D_SSM + D_POOL + D_SGU:D_IN] = (dv * _gelu_grad(zv, sv)).astype(BF16)

    @pl.when(i == n_tiles - 1)
    def _():
        tri = (lax.broadcasted_iota(jnp.int32, (CHUNK, CHUNK), 0) >= lax.broadcasted_iota(jnp.int32, (CHUNK, CHUNK), 1))
        for h in range(HEADS):
            o_dws[h] = jnp.where(tri, accw_ref[h], 0.0)
        lane = lax.broadcasted_iota(jnp.int32, (1, 128), 1)
        acc = jnp.zeros((CHUNK, 128), F32)
        for h in range(HEADS):
            sh = jnp.sum(accm_ref[:, h * HEAD_DIM:(h + 1) * HEAD_DIM], axis=1, keepdims=True)
            acc = jnp.where(lane == h, sh, acc)
        o_dbsp[...] = acc


def ffn_bwd(dx2, x1, gate, up, g, wg_t, wu_t, wd, name, exchange=None):
    T = x1.shape[0]
    tt = TT_FFN
    any_spec = pl.BlockSpec(memory_space=pl.ANY)

    def body(dx2_ref, x1_ref, gate_ref, up_ref, g_ref, wg_hbm, wu_hbm, wd_hbm,
             dgu_ref, dx2b_ref, dx1_ref, dx1b_ref, dg_ref, wg_v, wu_v, wd_v, sem):
        _load_weights([(wg_hbm, wg_v), (wu_hbm, wu_v), (wd_hbm, wd_v)], sem)

        @pl.when(pl.program_id(0) == 0)
        def _():
            dg_ref[...] = jnp.zeros_like(dg_ref)

        dx2 = dx2_ref[...]
        dx2b = dx2.astype(BF16)
        dx2b_ref[...] = dx2b
        dact = _nt(dx2b, wd_v[...])
        gate = gate_ref[...].astype(F32)
        up = up_ref[...].astype(F32)
        sg = jax.nn.sigmoid(gate)
        dgate = (dact * up * (sg * (1.0 + gate * (1.0 - sg)))).astype(BF16)
        dup = (dact * gate * sg).astype(BF16)
        dgu_ref[:, 0:D_FF] = dgate
        dgu_ref[:, D_FF:2 * D_FF] = dup
        dh = _nn(dgate, wg_v[...]) + _nn(dup, wu_v[...])
        xn, r = _rms(x1_ref[...])
        dg_ref[...] += _rowsum(dh * xn)
        dx1 = dx2 + _rms_bwd(dh, xn, r, g_ref[...])
        dx1_ref[...] = dx1
        dx1b_ref[...] = dx1.astype(BF16)

    return _pallas(
        body, name=name, grid=(T // tt,),
        in_specs=[_row(tt, D), _row(tt, D), _row(tt, D_FF), _row(tt, D_FF), _full((1, D)), any_spec, any_spec, any_spec],
        out_specs=[_row(tt, 2 * D_FF), _row(tt, D), _row(tt, D), _row(tt, D), _full((1, D))],
        out_shape=[S((T, 2 * D_FF), BF16), S((T, D), BF16), S((T, D), F32), S((T, D), BF16), S((1, D), F32)],
        scratch_shapes=[pltpu.VMEM((D_FF, D), BF16), pltpu.VMEM((D_FF, D), BF16), pltpu.VMEM((D_FF, D), BF16),
                        pltpu.SemaphoreType.DMA((3,))],
        vmem_mb=56, operands=(dx2, x1, gate, up, g, wg_t, wu_t, wd), exchange=exchange)


def wgrad(a, b, tm, name, exchange=None, tk=TK_WGRAD):
    T, M = a.shape
    N = b.shape[1]
    tk = min(tk, T)
    n_k = T // tk

    def body(a_ref, b_ref, o_ref, acc_ref):
        k = pl.program_id(1)

        @pl.when(k == 0)
        def _():
            acc_ref[...] = jnp.zeros_like(acc_ref)

        acc_ref[...] += _tn(a_ref[...], b_ref[...])

        @pl.when(k == n_k - 1)
        def _():
            o_ref[...] = acc_ref[...].astype(BF16)

    (out,), got = _pallas(
        body, name=name, grid=(M // tm, n_k),
        in_specs=[pl.BlockSpec((tk, tm), lambda m, k: (k, m)), pl.BlockSpec((tk, N), lambda m, k: (k, 0))],
        out_specs=[pl.BlockSpec((tm, N), lambda m, k: (m, 0))],
        out_shape=[S((M, N), BF16)],
        scratch_shapes=[pltpu.VMEM((tm, N), F32)],
        vmem_mb=48, operands=(a, b), exchange=exchange)
    return out if exchange is None else (out, got)


def _mesh_place():
    x, y, c = lax.axis_index("x"), lax.axis_index("y"), lax.axis_index("c")
    return x, y, c, 4 * x + 2 * y + c


def _peer(x, y, c, k):
    px = 1 - x if k & 4 else x
    py = 1 - y if k & 2 else y
    pc = 1 - c if k & 1 else c
    return (px, py, pc), 4 * px + 2 * py + pc


class _Exchange:
    SAME_CORE = (2, 4, 6)

    def __init__(self, gather=(), scatter=()):
        self.entries = [(a, None, a.shape[0]) for a in gather] + [(a, off, rows) for a, off, rows in scatter]
        self.n_gather = len(gather)

    @property
    def n(self):
        return len(self.entries)

    def operands(self):
        return [e[0] for e in self.entries]

    def out_shapes(self):
        return [S((N_DEV, rows, a.shape[1]), a.dtype) for a, _, rows in self.entries]

    def sems(self):
        return [pltpu.SemaphoreType.DMA((self.n, N_DEV)), pltpu.SemaphoreType.DMA((self.n, N_DEV)),
                pltpu.SemaphoreType.DMA((self.n,))]

    def _src(self, ref, e, idx):
        _, off, rows = self.entries[e]
        if off is None:
            return ref
        return ref.at[pl.ds(pl.multiple_of(off + idx * rows, 16), rows)]

    def _masks(self, e):
        return (1,) + self.SAME_CORE if e < self.n_gather else tuple(range(1, N_DEV))

    def _copy(self, ins, outs, sems, e, k, sending, passing_on=False):
        send_sems, recv_sems, _ = sems
        x, y, c, me = _mesh_place()
        peer, pidx = _peer(x, y, c, k)
        if passing_on:
            return pltpu.make_async_remote_copy(
                src_ref=outs[e].at[pidx], dst_ref=outs[e].at[pidx], send_sem=send_sems.at[e, k | 1],
                recv_sem=recv_sems.at[e, k | 1], device_id=_peer(x, y, c, 1)[0], device_id_type=pl.DeviceIdType.MESH)
        return pltpu.make_async_remote_copy(
            src_ref=self._src(ins[e], e, pidx), dst_ref=outs[e].at[me if sending else pidx], send_sem=send_sems.at[e, k],
            recv_sem=recv_sems.at[e, k], device_id=peer, device_id_type=pl.DeviceIdType.MESH)

    def _local(self, ins, outs, sems):
        me = _mesh_place()[3]
        return [pltpu.make_async_copy(self._src(ins[e], e, me), outs[e].at[me], sems[2].at[e]) for e in range(self.n)]

    def start(self, ins, outs, sems):
        for cp in self._local(ins, outs, sems):
            cp.start()
        for k in range(1, N_DEV):
            for e in range(self.n):
                if k in self._masks(e):
                    self._copy(ins, outs, sems, e, k, True).start()

    def forward(self, ins, outs, sems):
        for k in self.SAME_CORE:
            for e in range(self.n_gather):
                self._copy(ins, outs, sems, e, k, False).wait_recv()
                self._copy(ins, outs, sems, e, k, False, passing_on=True).start()

    def wait(self, ins, outs, sems):
        for k in range(1, N_DEV):
            for e in range(self.n):
                if e >= self.n_gather or k % 2:
                    self._copy(ins, outs, sems, e, k, False).wait_recv()
        for k in range(1, N_DEV):
            for e in range(self.n):
                self._copy(ins, outs, sems, e, k, True).wait_send()
        for cp in self._local(ins, outs, sems):
            cp.wait()


def _pallas(body, *, name, grid, in_specs, out_specs, out_shape, scratch_shapes, vmem_mb, operands, exchange=None,
            aliases=None):
    n_in, n_out, n_scr = len(in_specs), len(out_specs), len(scratch_shapes)
    n_steps = math.prod(grid)
    aliases = aliases or {}
    if exchange is None:
        res = pl.pallas_call(body, name=name, grid=grid, in_specs=in_specs, out_specs=out_specs, out_shape=out_shape,
                             scratch_shapes=scratch_shapes, input_output_aliases=aliases,
                             compiler_params=_cp(vmem_mb, len(grid)))(*operands)
        return list(res), []
    ex = exchange

    def hosted(*refs):
        ins, ex_in = refs[:n_in], refs[n_in:n_in + ex.n]
        outs = refs[n_in + ex.n:n_in + ex.n + n_out]
        ex_out = refs[n_in + ex.n + n_out:n_in + 2 * ex.n + n_out]
        scr = refs[n_in + 2 * ex.n + n_out:]
        sems = scr[n_scr:]
        step = pl.program_id(0)
        for axis in range(1, len(grid)):
            step = step * grid[axis] + pl.program_id(axis)

        @pl.when(step == 0)
        def _():
            ex.start(ex_in, ex_out, sems)

        body(*ins, *outs, *scr[:n_scr])

        if ex.n_gather:
            @pl.when(step == max(n_steps - 1 - max(2, n_steps // 8), 0))
            def _():
                ex.forward(ex_in, ex_out, sems)

        @pl.when(step == n_steps - 1)
        def _():
            ex.wait(ex_in, ex_out, sems)

    any_spec = pl.BlockSpec(memory_space=pl.ANY)
    res = pl.pallas_call(
        hosted, name=name, grid=grid, in_specs=list(in_specs) + [any_spec] * ex.n,
        out_specs=list(out_specs) + [any_spec] * ex.n, out_shape=list(out_shape) + ex.out_shapes(),
        scratch_shapes=list(scratch_shapes) + ex.sems(), input_output_aliases=aliases,
        compiler_params=_cp(vmem_mb, len(grid)),
    )(*operands, *ex.operands())
    return list(res[:n_out]), list(res[n_out:])


def exchange_only(ex, name):
    def body(*refs):
        ins, outs, sems = refs[:ex.n], refs[ex.n:2 * ex.n], refs[2 * ex.n:]
        ex.start(ins, outs, sems)
        ex.forward(ins, outs, sems)
        ex.wait(ins, outs, sems)

    any_spec = pl.BlockSpec(memory_space=pl.ANY)
    return list(pl.pallas_call(body, name=name, in_specs=[any_spec] * ex.n, out_specs=[any_spec] * ex.n,
                               out_shape=ex.out_shapes(), scratch_shapes=ex.sems())(*ex.operands()))


def _adamw(w, g, m, v):
    m = ADAM_B1 * m + (1.0 - ADAM_B1) * g
    v = ADAM_B2 * v + (1.0 - ADAM_B2) * (g * g)
    m_hat = m / (1.0 - ADAM_B1 ** ADAM_STEP)
    v_hat = v / (1.0 - ADAM_B2 ** ADAM_STEP)
    delta = -ADAM_LR * (m_hat / (jnp.sqrt(v_hat) + ADAM_EPS) + ADAM_WD * w)
    return delta, m, v


def _sum_parts(p_ref, rows=slice(None)):
    g = p_ref[0, rows].astype(F32)
    for k in range(1, N_DEV):
        g = g + p_ref[k, rows].astype(F32)
    return g


def adamw_layers(parts, w, m, v, name):
    n_l = len(parts)
    _, r, n = w.shape
    rb = r // 2 if r % (2 * BF16_TILE_ROWS) == 0 else r

    def body(*refs):
        p_refs = refs[:n_l]
        w_ref, m_ref, v_ref, g_out, d_out, m_out, v_out = refs[n_l:]
        for l in range(n_l):
            g = _sum_parts(p_refs[l])
            g_out[l] = g
            d_out[l], m_out[l], v_out[l] = _adamw(w_ref[l], g, m_ref[l], v_ref[l])

    part_spec = pl.BlockSpec((N_DEV, rb, n), lambda i: (0, i, 0))
    w_spec = pl.BlockSpec((n_l, rb, n), lambda i: (0, i, 0))
    return pl.pallas_call(
        body, name=name, grid=(r // rb,), in_specs=[part_spec] * n_l + [w_spec] * 3, out_specs=[w_spec] * 4,
        out_shape=[S(w.shape, F32)] * 4, compiler_params=_cp(48, 1),
    )(*parts, w, m, v)


def adamw_segments(parts, segments, w, m, v, name):
    n_p = len(parts)

    def body(*refs):
        p_refs = refs[:n_p]
        w_ref, m_ref, v_ref, g_out, d_out, m_out, v_out = refs[n_p:]
        for part, src, dst, rows in segments:
            g = _sum_parts(p_refs[part], slice(src, src + rows))
            to = slice(dst, dst + rows)
            g_out[to] = g
            d_out[to], m_out[to], v_out[to] = _adamw(w_ref[to], g, m_ref[to], v_ref[to])

    return pl.pallas_call(
        body, name=name, out_shape=[S(w.shape, F32)] * 4, compiler_params=_cp(48),
    )(*parts, w, m, v)


SMALL_LAYER = ("g_mix", "A_re", "A_im", "log_dt", "B_re", "B_im", "C_re", "C_im", "D_skip", "b_glu", "w_pool",
               "pool_scale", "sgu_ln_g", "sgu_ln_b", "w_spatial", "b_spatial", "g_ffn")
BIG_NAMES = ("w_in", "w_glu", "w_out", "w_gate", "w_up", "w_down")
COLUMN_SHARDED = ("w_in", "w_gate", "w_up")
WEIGHT_ORDER = ("g_mix", "w_in", "A_re", "A_im", "log_dt", "B_re", "B_im", "C_re", "C_im", "D_skip", "w_glu", "b_glu",
                "w_pool", "pool_scale", "sgu_ln_g", "sgu_ln_b", "w_spatial", "b_spatial", "w_out", "g_ffn", "w_gate",
                "w_up", "w_down", "g_final")
SEG = 1024
SMALL_AS_BF16 = ("B_re", "B_im", "C_re", "C_im", "w_pool", "w_spatial")


def _pack(arrays, dtype=F32):
    seg = SEG * 4 // jnp.dtype(dtype).itemsize
    parts = []
    for a in arrays:
        flat = a.reshape(-1).astype(dtype)
        parts.append(jnp.pad(flat, (0, (-flat.shape[0]) % seg)))
    return jnp.concatenate(parts).reshape(-1, 128)


def _state_rows(p):
    return p.reshape(1, D_ST)


def _chan_by_state(p):
    return jnp.transpose(p, (2, 0, 1)).reshape(GRP, D_ST)


def _chan_by_state_c(p):
    return jnp.transpose(p, (1, 0, 2)).reshape(GRP, D_ST)


def kernel(x, g_mix, w_in, A_re, A_im, log_dt, B_re, B_im, C_re, C_im, D_skip, w_glu, b_glu, w_pool, pool_scale, sgu_ln_g, sgu_ln_b, w_spatial, b_spatial, w_out, g_ffn, w_gate, w_up, w_down, g_final, loss_target, m_g_mix, m_w_in, m_A_re, m_A_im, m_log_dt, m_B_re, m_B_im, m_C_re, m_C_im, m_D_skip, m_w_glu, m_b_glu, m_w_pool, m_pool_scale, m_sgu_ln_g, m_sgu_ln_b, m_w_spatial, m_b_spatial, m_w_out, m_g_ffn, m_w_gate, m_w_up, m_w_down, m_g_final, v_g_mix, v_w_in, v_A_re, v_A_im, v_log_dt, v_B_re, v_B_im, v_C_re, v_C_im, v_D_skip, v_w_glu, v_b_glu, v_w_pool, v_pool_scale, v_sgu_ln_g, v_sgu_ln_b, v_w_spatial, v_b_spatial, v_w_out, v_g_ffn, v_w_gate, v_w_up, v_w_down, v_g_final):
    args = dict(locals())
    W = {n: args[n] for n in WEIGHT_ORDER}
    M = {n: args["m_" + n] for n in WEIGHT_ORDER}
    V = {n: args["v_" + n] for n in WEIGHT_ORDER}
    n_layers = g_mix.shape[0]
    x0 = x[0]
    target = loss_target[0]

    def my_rows(name, l):
        w = W[name][l]
        return (w.T if name in COLUMN_SHARDED else w).astype(BF16)

    full_w = [dict() for _ in range(n_layers)]

    def gather_of(*which):
        return _Exchange(gather=[my_rows(n, l) for n, l in which])

    def keep_gathered(which, arrays):
        for (n, l), a in zip(which, arrays):
            full_w[l][n] = a.reshape(-1, a.shape[-1])

    tri = jnp.tril(jnp.ones((CHUNK, CHUNK), bool))
    perm = _interleave_matrices(TT_MIX)
    consts = []
    for l in range(n_layers):
        a_re, a_im = _state_rows(A_re[l]), _state_rows(A_im[l])
        ldt = jnp.repeat(log_dt[l], N_STATE).reshape(1, D_ST)
        b_re_t, b_im_t = _chan_by_state(B_re[l]), _chan_by_state(B_im[l])
        sc, bbd, cbd = s5_prepare(a_re, a_im, ldt, b_re_t, b_im_t, _chan_by_state_c(C_re[l]), _chan_by_state_c(C_im[l]))
        wsm = jnp.where(tri[None], w_spatial[l], 0.0)
        wpool_bd = jnp.zeros((D_POOL, D_POOL), F32)
        for gi in range(len(POOL_WINDOWS)):
            wpool_bd = wpool_bd.at[gi * 64:(gi + 1) * 64, gi * 64:(gi + 1) * 64].set(w_pool[l, gi])
        consts.append(dict(
            disc=(a_re, a_im, ldt, b_re_t, b_im_t), sc=sc, bbd=bbd, cbd=cbd,
            dskip=D_skip[l].reshape(1, D_SSM), bglu=b_glu[l].reshape(1, D_SSM),
            wpool=wpool_bd.astype(BF16), pscale=pool_scale[l].reshape(1, D_POOL),
            lng=sgu_ln_g[l].reshape(1, D_SGU), lnb=sgu_ln_b[l].reshape(1, D_SGU),
            wsm=wsm.astype(BF16), wsmt=jnp.transpose(wsm, (0, 2, 1)).astype(BF16),
            bsp=jnp.repeat(b_spatial[l].T, HEAD_DIM, axis=1),
            gmix=g_mix[l].reshape(1, D), gffn=g_ffn[l].reshape(1, D)))

    def mixer_args(l):
        c = consts[l]
        return (c["bbd"], c["cbd"], c["dskip"], full_w[l]["w_glu"], c["bglu"], c["wpool"], c["pscale"], c["lng"], c["lnb"])

    first_needed = [("w_in", 0)]
    keep_gathered(first_needed, exchange_only(gather_of(*first_needed), "gather_first"))
    carried_fwd = {
        ("inproj", 0): [("w_glu", 0), ("w_out", 0)],
        ("mixer", 0): [("w_gate", 0), ("w_up", 0), ("w_down", 0)],
        ("ffn", 0): [("w_in", 1), ("w_glu", 1), ("w_out", 1), ("w_gate", 1)],
        ("mixer", 1): [("w_up", 1), ("w_down", 1)],
    }

    def carried(kind, l):
        which = carried_fwd.get((kind, l))
        return which, (gather_of(*which) if which else None)

    saved = []
    xl = x0
    for l in range(n_layers):
        c, fw = consts[l], full_w[l]
        which, ex = carried("inproj", l)
        (h, z), got = inproj_fwd(xl, c["gmix"], fw["w_in"], f"inproj_fwd_{l}", ex)
        keep_gathered(which or [], got)
        which, ex = carried("mixer", l)
        (ycat, hs, ys), got = mixer_fwd(z, c["sc"], *mixer_args(l), c["wsm"], c["bsp"], perm, f"mixer_fwd_{l}", ex)
        keep_gathered(which or [], got)
        which, ex = carried("ffn", l)
        head = (target, g_final.reshape(1, D)) if l == n_layers - 1 else None
        res, got = ffn_fwd(xl, ycat, fw["w_out"], c["gffn"], fw["w_gate"], fw["w_up"], fw["w_down"], f"ffn_fwd_{l}", ex, head)
        keep_gathered(which or [], got)
        x1, h2, gate, up, act, x2 = res[:6]
        saved.append(dict(x=xl, h=h, z=z, ycat=ycat, hs=hs, ys=ys, x1=x1, h2=h2, gate=gate, up=up, act=act))
        xl = x2
    dx, loss_vec, d_gfinal = xl, res[6], res[7]

    recv_big = {}
    recv_small = [None] * (2 * n_layers)

    def keep_received(which, arrays):
        for key, a in zip(which, arrays):
            if key[0] == "small":
                recv_small[key[1]] = a
            elif key[0] == "small16":
                recv_small[n_layers + key[1]] = a
            else:
                recv_big[key] = a

    pending = None
    for l in reversed(range(n_layers)):
        c, fw, sv = consts[l], full_w[l], saved[l]
        (dgu, dx2b, dx1, dx1b, d_gffn), got = ffn_bwd(dx, sv["x1"], sv["gate"], sv["up"], c["gffn"], fw["w_gate"], fw["w_up"],
                                                     fw["w_down"], f"ffn_bwd_{l}", pending[1] if pending else None)
        if pending:
            keep_received(pending[0], got)
        g_gu = wgrad(dgu, sv["h2"], D_FF // 2, f"wgrad_gate_up_{l}")
        g_down = wgrad(sv["act"], dx2b, D_FF // 2, f"wgrad_down_{l}")
        g_out = wgrad(sv["ycat"], dx1b, D, f"wgrad_out_{l}")
        ffn_rows = D_FF // N_DEV
        ex = _Exchange(scatter=[(g_gu, 0, ffn_rows), (g_gu, D_FF, ffn_rows), (g_down, 0, ffn_rows), (g_out, 0, D // N_DEV)])
        (dzb, da, dbt, dct, dd, dbglu, dwglu, dwpool, dpscale, dlng, dlnb, dws, dbsp), got = mixer_bwd(
            dx1b, sv["z"], sv["hs"], sv["ys"], fw["w_out"], c["sc"], *mixer_args(l), c["wsm"], c["wsmt"], c["bsp"],
            perm, f"mixer_bwd_{l}", ex)
        keep_received([("w_gate", l), ("w_up", l), ("w_down", l), ("w_out", l)], got)
        dx, d_gmix = inproj_bwd(dzb, sv["x"], c["gmix"], fw["w_in"], dx1)
        d_are, d_aim, d_ldt, d_bre_t, d_bim_t = s5_param_bwd(*c["disc"], da, dbt)
        small = dict(
            g_mix=d_gmix.reshape(D), A_re=d_are.reshape(N_GRP, N_STATE), A_im=d_aim.reshape(N_GRP, N_STATE),
            log_dt=d_ldt[0, :N_GRP],
            B_re=jnp.transpose(d_bre_t.reshape(GRP, N_GRP, N_STATE), (1, 0, 2)),
            B_im=jnp.transpose(d_bim_t.reshape(GRP, N_GRP, N_STATE), (1, 0, 2)),
            C_re=jnp.transpose(dct[:, :D_ST].reshape(GRP, N_GRP, N_STATE), (1, 0, 2)),
            C_im=-jnp.transpose(dct[:, D_ST:].reshape(GRP, N_GRP, N_STATE), (1, 0, 2)),
            D_skip=dd.reshape(D_SSM), b_glu=dbglu.reshape(D_SSM),
            w_pool=jnp.stack([dwpool[gi * 64:(gi + 1) * 64, gi * 64:(gi + 1) * 64] for gi in range(len(POOL_WINDOWS))]),
            pool_scale=dpscale.reshape(D_POOL), sgu_ln_g=dlng.reshape(D_SGU), sgu_ln_b=dlnb.reshape(D_SGU),
            w_spatial=dws, b_spatial=dbsp[:, :HEADS].T, g_ffn=d_gffn.reshape(D))
        packed = [small[n] for n in SMALL_LAYER if n not in SMALL_AS_BF16]
        packed += [d_gfinal.reshape(D), loss_vec.reshape(D)] if l == 0 else []
        packed16 = [small[n] for n in SMALL_LAYER if n in SMALL_AS_BF16]
        small_entries = dict(gather=[_pack(packed), _pack(packed16, BF16)], scatter=[(dwglu.astype(BF16), 0, D_SSM // N_DEV)])
        if l > 0:
            g_in = wgrad(dzb, sv["h"], D_IN, f"wgrad_in_{l}")
            small_entries["scatter"].append((g_in, 0, D_IN // N_DEV))
            pending = ([("small", l), ("small16", l), ("w_glu", l), ("w_in", l)], _Exchange(**small_entries))
        else:
            g_in, got = wgrad(dzb, sv["h"], D_IN, f"wgrad_in_{l}", tk=TK_WGRAD // 4, exchange=_Exchange(**small_entries))
            keep_received([("small", l), ("small16", l), ("w_glu", l)], got)
            pending = ([("w_in", l)], _Exchange(scatter=[(g_in, 0, D_IN // N_DEV)]))
    grad_x = dx
    keep_received(pending[0], exchange_only(pending[1], "exchange_last"))

    out = {}
    for n in BIG_NAMES:
        tr = (lambda a: jnp.transpose(a, (0, 2, 1))) if n in COLUMN_SHARDED else (lambda a: a)
        res = adamw_layers([recv_big[(n, l)] for l in range(n_layers)], tr(W[n]), tr(M[n]), tr(V[n]), f"adamw_{n}")
        out[n] = [tr(r) for r in res]

    seg_rows = [(-(-math.prod(W[n].shape[1:]) // SEG)) * (SEG // 128) for n in SMALL_LAYER]
    segments, src, src16, dst = [], 0, 0, 0
    for n, rows in zip(SMALL_LAYER, seg_rows):
        if n in SMALL_AS_BF16:
            segments += [(n_layers + l, src16, dst + l * rows, rows) for l in range(n_layers)]
            src16 += rows
        else:
            segments += [(l, src, dst + l * rows, rows) for l in range(n_layers)]
            src += rows
        dst += n_layers * rows
    tile_rows = SEG // 128
    segments += [(0, src, dst, tile_rows), (0, src + tile_rows, dst + tile_rows, tile_rows)]

    channel_major = ("B_re", "B_im")

    def pack_params(P):
        parts = []
        for n, rows in zip(SMALL_LAYER, seg_rows):
            flat = (jnp.swapaxes(P[n], -1, -2) if n in channel_major else P[n]).reshape(n_layers, -1)
            parts.append(jnp.pad(flat, ((0, 0), (0, rows * 128 - flat.shape[1]))).reshape(-1))
        return jnp.concatenate(parts + [P["g_final"], jnp.zeros((SEG,), F32)]).reshape(-1, 128)

    res = adamw_segments(recv_small, segments, pack_params(W), pack_params(M), pack_params(V), "adamw_small")
    for j in range(4):
        flat, off = res[j].reshape(-1), 0
        for n, rows in zip(SMALL_LAYER, seg_rows):
            size = math.prod(W[n].shape[1:])
            piece = flat[off:off + n_layers * rows * 128].reshape(n_layers, rows * 128)[:, :size]
            if n in channel_major:
                piece = jnp.swapaxes(piece.reshape(W[n].shape[:-2] + W[n].shape[:-3:-1]), -1, -2)
            else:
                piece = piece.reshape(W[n].shape)
            out.setdefault(n, []).append(piece)
            off += n_layers * rows * 128
        out.setdefault("g_final", []).append(flat[off:off + D])
        if j == 0:
            loss = (0.5 / D) * jnp.sum(flat[off + SEG:off + SEG + D])

    return (loss, grad_x[None], *[out[n][0] for n in WEIGHT_ORDER], *[out[n][1] for n in WEIGHT_ORDER],
            *[out[n][2] for n in WEIGHT_ORDER], *[out[n][3] for n in WEIGHT_ORDER])
```
